```python
import jax, jax.numpy as jnp
from jax import lax
import numpy as np

D_MODEL = 1024
BATCH = 8
SEQ = 2048
DEPTH = 2

D_FF = 2816
NORM_EPS = 1e-6
CONV_WIDTH = 4

LRU_WIDTH = 256
LRU_BLOCKS = 4
LRU_BLOCK = LRU_WIDTH // LRU_BLOCKS
LRU_C = 8.0

RWKV_HEADS = 6
RWKV_HEAD_DIM = 64
RWKV_WIDTH = RWKV_HEADS * RWKV_HEAD_DIM
RWKV_DECAY_LORA = 64
RWKV_AAA_LORA = 64
RWKV_MV_LORA = 32
RWKV_GATE_LORA = 128
RWKV_GN_EPS = 64e-5
RWKV_IN = 3 * RWKV_WIDTH + RWKV_DECAY_LORA + RWKV_AAA_LORA + RWKV_GATE_LORA
RWKV_SPLITS = (RWKV_WIDTH, RWKV_WIDTH, RWKV_WIDTH, RWKV_DECAY_LORA, RWKV_AAA_LORA, RWKV_GATE_LORA)

GDN_HEADS = 6
GDN_HEAD_DIM = 64
GDN_WIDTH = GDN_HEADS * GDN_HEAD_DIM
GDN_CHUNK = 64

D_MIX = LRU_WIDTH + RWKV_WIDTH + GDN_WIDTH
IN_SPLITS = (LRU_WIDTH, LRU_WIDTH, RWKV_IN, 3 * GDN_WIDTH, GDN_WIDTH, GDN_HEADS, GDN_HEADS)
D_IN = 2 * LRU_WIDTH + RWKV_IN + 4 * GDN_WIDTH + 2 * GDN_HEADS

kernel_name = 'hymba_style_rglru_rwkv7_gdn_macaron'


def split_cols(t, sizes):
    idx = np.cumsum(sizes)[:-1].tolist()
    return jnp.split(t, idx, axis=-1)


def rms_norm(x, g, eps=NORM_EPS):
    xf = x.astype(jnp.float32)
    y = xf * lax.rsqrt(jnp.mean(xf * xf, axis=-1, keepdims=True) + eps)
    return (y * g.astype(jnp.float32)).astype(x.dtype)


def l2_normalize(t, eps=1e-6):
    return t * lax.rsqrt(jnp.sum(t * t, axis=-1, keepdims=True) + eps)


def swiglu_ffn(h, wi, wo):
    gate, up = jnp.split(h @ wi, 2, axis=-1)
    return (jax.nn.silu(gate) * up) @ wo


def token_shift(t):
    return jnp.pad(t, ((0, 0), (1, 0), (0, 0)))[:, :-1]


def causal_depthwise_conv(x, w):
    c = x.shape[-1]
    return lax.conv_general_dilated(
        x.astype(jnp.float32), w.astype(jnp.float32)[:, None, :],
        window_strides=(1,), padding=((CONV_WIDTH - 1, 0),),
        dimension_numbers=('NWC', 'WIO', 'NWC'), feature_group_count=c)


def _linear_combine(left, right):
    a1, b1 = left
    a2, b2 = right
    return a1 * a2, a2 * b1 + b2


def rglru_group(p_x, p_y, conv_w, conv_b, ga_w, ga_b, gx_w, gx_b, lam, out_g):
    bsz, T, _ = p_x.shape
    xc = causal_depthwise_conv(p_x, conv_w) + conv_b.astype(jnp.float32)
    xb = xc.reshape(bsz, T, LRU_BLOCKS, LRU_BLOCK)
    r = jax.nn.sigmoid(jnp.einsum('btnc,ncd->btnd', xb, ga_w.astype(jnp.float32)) + ga_b).reshape(bsz, T, LRU_WIDTH)
    i = jax.nn.sigmoid(jnp.einsum('btnc,ncd->btnd', xb, gx_w.astype(jnp.float32)) + gx_b).reshape(bsz, T, LRU_WIDTH)
    log_a = -LRU_C * r * jax.nn.softplus(-lam.astype(jnp.float32))
    a = jnp.exp(log_a)
    mult = jnp.sqrt(-jnp.expm1(2.0 * log_a))
    first = (jnp.arange(T) == 0)[None, :, None]
    mult = jnp.where(first, 1.0, mult)
    _, hseq = lax.associative_scan(_linear_combine, (a, mult * i * xc), axis=1)
    out = hseq * jax.nn.gelu(p_y.astype(jnp.float32))
    return rms_norm(out, out_g)


def rwkv7_group(p, mu, w_up, w_bias, a_up, a_bias, g_up, k_k, k_a, r_k, ln_g, ln_b, v_first, vres):
    p = p.astype(jnp.float32)
    bsz, T, _ = p.shape
    xm = p + (token_shift(p) - p) * mu
    r, k, v, xw, xa, xg = split_cols(xm, RWKV_SPLITS)
    w_log = -jax.nn.softplus(-(w_bias + jnp.tanh(xw) @ w_up)) - 0.5
    decay = jnp.exp(-jnp.exp(w_log))
    a = jax.nn.sigmoid(a_bias + xa @ a_up)
    g = jax.nn.sigmoid(xg) @ g_up
    if vres is None:
        v_first = v
    else:
        vw1, vw2, vb = vres
        v = v + (v_first - v) * jax.nn.sigmoid(vb + (v @ vw1) @ vw2)
    heads = lambda t: t.reshape(bsz, T, RWKV_HEADS, RWKV_HEAD_DIM)
    r, k, v, decay, a = heads(r), heads(k), heads(v), heads(decay), heads(a)
    kk = l2_normalize(k * k_k.reshape(RWKV_HEADS, RWKV_HEAD_DIM))
    k = k * (1.0 + (a - 1.0) * k_a.reshape(RWKV_HEADS, RWKV_HEAD_DIM))

    def step(S, inp):
        r_t, w_t, k_t, v_t, kk_t, b_t = inp
        sa = jnp.einsum('bhvk,bhk->bhv', S, -kk_t)
        S = S * w_t[:, :, None, :] + sa[..., None] * b_t[:, :, None, :] + v_t[..., None] * k_t[:, :, None, :]
        return S, jnp.einsum('bhvk,bhk->bhv', S, r_t)

    tm = lambda t: jnp.swapaxes(t, 0, 1)
    S0 = jnp.zeros((bsz, RWKV_HEADS, RWKV_HEAD_DIM, RWKV_HEAD_DIM), jnp.float32)
    _, y = lax.scan(step, S0, (tm(r), tm(decay), tm(k), tm(v), tm(kk), tm(kk * a)))
    y = tm(y)
    mean = jnp.mean(y, axis=-1, keepdims=True)
    var = jnp.mean(jnp.square(y - mean), axis=-1, keepdims=True)
    y = ((y - mean) * lax.rsqrt(var + RWKV_GN_EPS)).reshape(bsz, T, RWKV_WIDTH) * ln_g + ln_b
    bonus = jnp.sum(r * k * r_k, axis=-1, keepdims=True) * v
    y = (y + bonus.reshape(bsz, T, RWKV_WIDTH)) * g
    return y, v_first


def gated_delta_rule_chunked(q, k, v, g, beta):
    bsz, T, H, D = q.shape
    nc = T // GDN_CHUNK

    def to_chunks(t):
        return t.reshape(bsz, nc, GDN_CHUNK, H, -1).transpose(0, 3, 1, 2, 4)

    q, k, v = to_chunks(q), to_chunks(k), to_chunks(v)
    g = to_chunks(g[..., None])[..., 0]
    beta = to_chunks(beta[..., None])[..., 0]
    gc = jnp.cumsum(g, axis=-1)
    idx = jnp.arange(GDN_CHUNK)
    causal = idx[:, None] >= idx[None, :]
    strict = idx[:, None] > idx[None, :]
    diff = gc[..., :, None] - gc[..., None, :]
    decay = jnp.where(causal, jnp.exp(jnp.where(causal, diff, 0.0)), 0.0)
    k_beta = k * beta[..., None]
    lower = jnp.where(strict, jnp.einsum('bhncd,bhnsd->bhncs', k_beta, k) * decay, 0.0)
    eye = jnp.eye(GDN_CHUNK, dtype=lower.dtype)
    rhs = jnp.concatenate([v * beta[..., None], k_beta * jnp.exp(gc)[..., None]], axis=-1)
    sol = lax.linalg.triangular_solve(lower + eye, rhs, left_side=True, lower=True, unit_diagonal=True)
    u, w = sol[..., :D], sol[..., D:]
    qk = jnp.where(causal, jnp.einsum('bhncd,bhnsd->bhncs', q, k) * decay, 0.0)
    q_dec = q * jnp.exp(gc)[..., None]
    k_dec = k * jnp.exp(gc[..., -1:] - gc)[..., None]
    g_last = jnp.exp(gc[..., -1])

    def step(S, inp):
        q_i, k_i, u_i, w_i, qk_i, gl_i = inp
        v_new = u_i - jnp.einsum('bhcd,bhde->bhce', w_i, S)
        o = jnp.einsum('bhcd,bhde->bhce', q_i, S) + jnp.einsum('bhcs,bhse->bhce', qk_i, v_new)
        S = S * gl_i[..., None, None] + jnp.einsum('bhcd,bhce->bhde', k_i, v_new)
        return S, o

    mv = lambda t: jnp.moveaxis(t, 2, 0)
    S0 = jnp.zeros((bsz, H, D, v.shape[-1]), jnp.float32)
    _, o = lax.scan(step, S0, (mv(q_dec), mv(k_dec), mv(u), mv(w), mv(qk), mv(g_last)))
    return o.transpose(1, 0, 3, 2, 4).reshape(bsz, T, H, -1)


def gdn_group(p_qkv, p_z, p_alpha, p_beta, conv_w, a_log, dt_bias, norm_g):
    bsz, T, _ = p_qkv.shape
    qkv = jax.nn.silu(causal_depthwise_conv(p_qkv, conv_w))
    heads = lambda t: t.reshape(bsz, T, GDN_HEADS, GDN_HEAD_DIM)
    q, k, v = [heads(t) for t in jnp.split(qkv, 3, axis=-1)]
    q = l2_normalize(q) * (GDN_HEAD_DIM ** -0.5)
    k = l2_normalize(k)
    g = -jnp.exp(a_log.astype(jnp.float32)) * jax.nn.softplus(p_alpha.astype(jnp.float32) + dt_bias)
    beta = jax.nn.sigmoid(p_beta.astype(jnp.float32))
    o = gated_delta_rule_chunked(q, k, v, g, beta)
    o = rms_norm(o, norm_g) * jax.nn.silu(heads(p_z.astype(jnp.float32)))
    return o.reshape(bsz, T, GDN_WIDTH)


def _fwd_setup_inputs(seed: int = 0) -> dict:
    key = jax.random.key(seed)
    keys = jax.random.split(key, 64)
    counter = [0]

    def nk():
        counter[0] += 1
        return keys[counter[0] - 1]

    def nrm(shape, scale):
        return jax.random.normal(nk(), shape, jnp.float32) * scale

    def gain(shape):
        return 1.0 + nrm(shape, 0.02)

    def unif(shape, lo, hi):
        return jax.random.uniform(nk(), shape, jnp.float32, lo, hi)

    L = DEPTH
    x = nrm((BATCH, SEQ, D_MODEL), 1.0)
    ffn1_norm = gain((L, D_MODEL))
    ffn1_wi = nrm((L, D_MODEL, 2 * D_FF), D_MODEL ** -0.5)
    ffn1_wo = nrm((L, D_FF, D_MODEL), D_FF ** -0.5)
    mix_norm = gain((L, D_MODEL))
    w_in = nrm((L, D_MODEL, D_IN), D_MODEL ** -0.5)
    w_out = nrm((L, D_MIX, D_MODEL), D_MIX ** -0.5)
    lru_conv_w = nrm((L, CONV_WIDTH, LRU_WIDTH), 0.5)
    lru_conv_b = nrm((L, LRU_WIDTH), 0.02)
    lru_gate_a_w = nrm((L, LRU_BLOCKS, LRU_BLOCK, LRU_BLOCK), LRU_BLOCK ** -0.5)
    lru_gate_a_b = nrm((L, LRU_BLOCKS, LRU_BLOCK), 0.02)
    lru_gate_x_w = nrm((L, LRU_BLOCKS, LRU_BLOCK, LRU_BLOCK), LRU_BLOCK ** -0.5)
    lru_gate_x_b = nrm((L, LRU_BLOCKS, LRU_BLOCK), 0.02)
    s = unif((L, LRU_WIDTH), 0.9, 0.999) ** (1.0 / LRU_C)
    lru_lambda = jnp.log(s) - jnp.log1p(-s)
    lru_out_norm = gain((L, LRU_WIDTH))
    rwkv_mu = unif((L, RWKV_IN), 0.0, 1.0)
    rwkv_w_up = nrm((L, RWKV_DECAY_LORA, RWKV_WIDTH), 0.1)
    rwkv_w_bias = unif((L, RWKV_WIDTH), -6.5, -1.5)
    rwkv_a_up = nrm((L, RWKV_AAA_LORA, RWKV_WIDTH), 0.5 * RWKV_AAA_LORA ** -0.5)
    rwkv_a_bias = nrm((L, RWKV_WIDTH), 0.1)
    rwkv_g_up = nrm((L, RWKV_GATE_LORA, RWKV_WIDTH), RWKV_GATE_LORA ** -0.5)
    rwkv_k_k = 0.85 + nrm((L, RWKV_WIDTH), 0.02)
    rwkv_k_a = gain((L, RWKV_WIDTH))
    rwkv_r_k = nrm((L, RWKV_HEADS, RWKV_HEAD_DIM), 0.1)
    rwkv_ln_g = gain((L, RWKV_WIDTH))
    rwkv_ln_b = nrm((L, RWKV_WIDTH), 0.02)
    rwkv_vres_w1 = nrm((L - 1, RWKV_WIDTH, RWKV_MV_LORA), RWKV_WIDTH ** -0.5)
    rwkv_vres_w2 = nrm((L - 1, RWKV_MV_LORA, RWKV_WIDTH), 0.5 * RWKV_MV_LORA ** -0.5)
    rwkv_vres_b = 1.0 + nrm((L - 1, RWKV_WIDTH), 0.1)
    gdn_conv_w = nrm((L, CONV_WIDTH, 3 * GDN_WIDTH), 0.5)
    gdn_a_log = jnp.log(unif((L, GDN_HEADS), 1.0, 16.0))
    dt = jnp.exp(unif((L, GDN_HEADS), float(np.log(1e-3)), float(np.log(1e-1))))
    gdn_dt_bias = dt + jnp.log(-jnp.expm1(-dt))
    gdn_norm = gain((L, GDN_HEAD_DIM))
    ffn2_norm = gain((L, D_MODEL))
    ffn2_wi = nrm((L, D_MODEL, 2 * D_FF), D_MODEL ** -0.5)
    ffn2_wo = nrm((L, D_FF, D_MODEL), D_FF ** -0.5)
    final_norm = gain((D_MODEL,))
    return {
        'x': x,
        'ffn1_norm': ffn1_norm, 'ffn1_wi': ffn1_wi, 'ffn1_wo': ffn1_wo,
        'mix_norm': mix_norm, 'w_in': w_in, 'w_out': w_out,
        'lru_conv_w': lru_conv_w, 'lru_conv_b': lru_conv_b,
        'lru_gate_a_w': lru_gate_a_w, 'lru_gate_a_b': lru_gate_a_b,
        'lru_gate_x_w': lru_gate_x_w, 'lru_gate_x_b': lru_gate_x_b,
        'lru_lambda': lru_lambda, 'lru_out_norm': lru_out_norm,
        'rwkv_mu': rwkv_mu, 'rwkv_w_up': rwkv_w_up, 'rwkv_w_bias': rwkv_w_bias,
        'rwkv_a_up': rwkv_a_up, 'rwkv_a_bias': rwkv_a_bias, 'rwkv_g_up': rwkv_g_up,
        'rwkv_k_k': rwkv_k_k, 'rwkv_k_a': rwkv_k_a, 'rwkv_r_k': rwkv_r_k,
        'rwkv_ln_g': rwkv_ln_g, 'rwkv_ln_b': rwkv_ln_b,
        'rwkv_vres_w1': rwkv_vres_w1, 'rwkv_vres_w2': rwkv_vres_w2, 'rwkv_vres_b': rwkv_vres_b,
        'gdn_conv_w': gdn_conv_w, 'gdn_a_log': gdn_a_log, 'gdn_dt_bias': gdn_dt_bias, 'gdn_norm': gdn_norm,
        'ffn2_norm': ffn2_norm, 'ffn2_wi': ffn2_wi, 'ffn2_wo': ffn2_wo,
        'final_norm': final_norm,
    }


def _fwd_reference(x, ffn1_norm, ffn1_wi, ffn1_wo, mix_norm, w_in, w_out,
              lru_conv_w, lru_conv_b, lru_gate_a_w, lru_gate_a_b, lru_gate_x_w, lru_gate_x_b,
              lru_lambda, lru_out_norm,
              rwkv_mu, rwkv_w_up, rwkv_w_bias, rwkv_a_up, rwkv_a_bias, rwkv_g_up,
              rwkv_k_k, rwkv_k_a, rwkv_r_k, rwkv_ln_g, rwkv_ln_b,
              rwkv_vres_w1, rwkv_vres_w2, rwkv_vres_b,
              gdn_conv_w, gdn_a_log, gdn_dt_bias, gdn_norm,
              ffn2_norm, ffn2_wi, ffn2_wo, final_norm):
    dt = x.dtype
    v_first = None
    for l in range(DEPTH):
        x = x + 0.5 * swiglu_ffn(rms_norm(x, ffn1_norm[l]), ffn1_wi[l], ffn1_wo[l])
        h = rms_norm(x, mix_norm[l])
        p_lx, p_ly, p_rwkv, p_qkv, p_z, p_alpha, p_beta = split_cols(h @ w_in[l], IN_SPLITS)
        y_lru = rglru_group(p_lx, p_ly, lru_conv_w[l], lru_conv_b[l], lru_gate_a_w[l], lru_gate_a_b[l],
                            lru_gate_x_w[l], lru_gate_x_b[l], lru_lambda[l], lru_out_norm[l])
        vres = None if l == 0 else (rwkv_vres_w1[l - 1], rwkv_vres_w2[l - 1], rwkv_vres_b[l - 1])
        y_rwkv, v_first = rwkv7_group(p_rwkv, rwkv_mu[l], rwkv_w_up[l], rwkv_w_bias[l], rwkv_a_up[l],
                                      rwkv_a_bias[l], rwkv_g_up[l], rwkv_k_k[l], rwkv_k_a[l], rwkv_r_k[l],
                                      rwkv_ln_g[l], rwkv_ln_b[l], v_first, vres)
        y_gdn = gdn_group(p_qkv, p_z, p_alpha, p_beta, gdn_conv_w[l], gdn_a_log[l], gdn_dt_bias[l], gdn_norm[l])
        mixed = jnp.concatenate([y_lru, y_rwkv, y_gdn], axis=-1).astype(dt)
        x = x + mixed @ w_out[l]
        x = x + 0.5 * swiglu_ffn(rms_norm(x, ffn2_norm[l]), ffn2_wi[l], ffn2_wo[l])
    return rms_norm(x, final_norm)


import jax as _jax
import jax.numpy as _jnp

TWIN_FORMAT = 'train_step'
FWD_PARAMS = ['x', 'ffn1_norm', 'ffn1_wi', 'ffn1_wo', 'mix_norm', 'w_in', 'w_out', 'lru_conv_w', 'lru_conv_b', 'lru_gate_a_w', 'lru_gate_a_b', 'lru_gate_x_w', 'lru_gate_x_b', 'lru_lambda', 'lru_out_norm', 'rwkv_mu', 'rwkv_w_up', 'rwkv_w_bias', 'rwkv_a_up', 'rwkv_a_bias', 'rwkv_g_up', 'rwkv_k_k', 'rwkv_k_a', 'rwkv_r_k', 'rwkv_ln_g', 'rwkv_ln_b', 'rwkv_vres_w1', 'rwkv_vres_w2', 'rwkv_vres_b', 'gdn_conv_w', 'gdn_a_log', 'gdn_dt_bias', 'gdn_norm', 'ffn2_norm', 'ffn2_wi', 'ffn2_wo', 'final_norm']
TWIN_WEIGHTS = ['ffn1_norm', 'ffn1_wi', 'ffn1_wo', 'mix_norm', 'w_in', 'w_out', 'lru_conv_w', 'lru_conv_b', 'lru_gate_a_w', 'lru_gate_a_b', 'lru_gate_x_w', 'lru_gate_x_b', 'lru_lambda', 'lru_out_norm', 'rwkv_mu', 'rwkv_w_up', 'rwkv_w_bias', 'rwkv_a_up', 'rwkv_a_bias', 'rwkv_g_up', 'rwkv_k_k', 'rwkv_k_a', 'rwkv_r_k', 'rwkv_ln_g', 'rwkv_ln_b', 'rwkv_vres_w1', 'rwkv_vres_w2', 'rwkv_vres_b', 'gdn_conv_w', 'gdn_a_log', 'gdn_dt_bias', 'gdn_norm', 'ffn2_norm', 'ffn2_wi', 'ffn2_wo', 'final_norm']
TWIN_DIFF_INPUT = 'x'
TWIN_INPUTS = ['x', 'ffn1_norm', 'ffn1_wi', 'ffn1_wo', 'mix_norm', 'w_in', 'w_out', 'lru_conv_w', 'lru_conv_b', 'lru_gate_a_w', 'lru_gate_a_b', 'lru_gate_x_w', 'lru_gate_x_b', 'lru_lambda', 'lru_out_norm', 'rwkv_mu', 'rwkv_w_up', 'rwkv_w_bias', 'rwkv_a_up', 'rwkv_a_bias', 'rwkv_g_up', 'rwkv_k_k', 'rwkv_k_a', 'rwkv_r_k', 'rwkv_ln_g', 'rwkv_ln_b', 'rwkv_vres_w1', 'rwkv_vres_w2', 'rwkv_vres_b', 'gdn_conv_w', 'gdn_a_log', 'gdn_dt_bias', 'gdn_norm', 'ffn2_norm', 'ffn2_wi', 'ffn2_wo', 'final_norm', 'loss_target', 'm_ffn1_norm', 'm_ffn1_wi', 'm_ffn1_wo', 'm_mix_norm', 'm_w_in', 'm_w_out', 'm_lru_conv_w', 'm_lru_conv_b', 'm_lru_gate_a_w', 'm_lru_gate_a_b', 'm_lru_gate_x_w', 'm_lru_gate_x_b', 'm_lru_lambda', 'm_lru_out_norm', 'm_rwkv_mu', 'm_rwkv_w_up', 'm_rwkv_w_bias', 'm_rwkv_a_up', 'm_rwkv_a_bias', 'm_rwkv_g_up', 'm_rwkv_k_k', 'm_rwkv_k_a', 'm_rwkv_r_k', 'm_rwkv_ln_g', 'm_rwkv_ln_b', 'm_rwkv_vres_w1', 'm_rwkv_vres_w2', 'm_rwkv_vres_b', 'm_gdn_conv_w', 'm_gdn_a_log', 'm_gdn_dt_bias', 'm_gdn_norm', 'm_ffn2_norm', 'm_ffn2_wi', 'm_ffn2_wo', 'm_final_norm', 'v_ffn1_norm', 'v_ffn1_wi', 'v_ffn1_wo', 'v_mix_norm', 'v_w_in', 'v_w_out', 'v_lru_conv_w', 'v_lru_conv_b', 'v_lru_gate_a_w', 'v_lru_gate_a_b', 'v_lru_gate_x_w', 'v_lru_gate_x_b', 'v_lru_lambda', 'v_lru_out_norm', 'v_rwkv_mu', 'v_rwkv_w_up', 'v_rwkv_w_bias', 'v_rwkv_a_up', 'v_rwkv_a_bias', 'v_rwkv_g_up', 'v_rwkv_k_k', 'v_rwkv_k_a', 'v_rwkv_r_k', 'v_rwkv_ln_g', 'v_rwkv_ln_b', 'v_rwkv_vres_w1', 'v_rwkv_vres_w2', 'v_rwkv_vres_b', 'v_gdn_conv_w', 'v_gdn_a_log', 'v_gdn_dt_bias', 'v_gdn_norm', 'v_ffn2_norm', 'v_ffn2_wi', 'v_ffn2_wo', 'v_final_norm']
TWIN_OUTPUTS = ['loss', 'grad_x', 'grad_ffn1_norm', 'grad_ffn1_wi', 'grad_ffn1_wo', 'grad_mix_norm', 'grad_w_in', 'grad_w_out', 'grad_lru_conv_w', 'grad_lru_conv_b', 'grad_lru_gate_a_w', 'grad_lru_gate_a_b', 'grad_lru_gate_x_w', 'grad_lru_gate_x_b', 'grad_lru_lambda', 'grad_lru_out_norm', 'grad_rwkv_mu', 'grad_rwkv_w_up', 'grad_rwkv_w_bias', 'grad_rwkv_a_up', 'grad_rwkv_a_bias', 'grad_rwkv_g_up', 'grad_rwkv_k_k', 'grad_rwkv_k_a', 'grad_rwkv_r_k', 'grad_rwkv_ln_g', 'grad_rwkv_ln_b', 'grad_rwkv_vres_w1', 'grad_rwkv_vres_w2', 'grad_rwkv_vres_b', 'grad_gdn_conv_w', 'grad_gdn_a_log', 'grad_gdn_dt_bias', 'grad_gdn_norm', 'grad_ffn2_norm', 'grad_ffn2_wi', 'grad_ffn2_wo', 'grad_final_norm', 'delta_ffn1_norm', 'delta_ffn1_wi', 'delta_ffn1_wo', 'delta_mix_norm', 'delta_w_in', 'delta_w_out', 'delta_lru_conv_w', 'delta_lru_conv_b', 'delta_lru_gate_a_w', 'delta_lru_gate_a_b', 'delta_lru_gate_x_w', 'delta_lru_gate_x_b', 'delta_lru_lambda', 'delta_lru_out_norm', 'delta_rwkv_mu', 'delta_rwkv_w_up', 'delta_rwkv_w_bias', 'delta_rwkv_a_up', 'delta_rwkv_a_bias', 'delta_rwkv_g_up', 'delta_rwkv_k_k', 'delta_rwkv_k_a', 'delta_rwkv_r_k', 'delta_rwkv_ln_g', 'delta_rwkv_ln_b', 'delta_rwkv_vres_w1', 'delta_rwkv_vres_w2', 'delta_rwkv_vres_b', 'delta_gdn_conv_w', 'delta_gdn_a_log', 'delta_gdn_dt_bias', 'delta_gdn_norm', 'delta_ffn2_norm', 'delta_ffn2_wi', 'delta_ffn2_wo', 'delta_final_norm', 'new_m_ffn1_norm', 'new_m_ffn1_wi', 'new_m_ffn1_wo', 'new_m_mix_norm', 'new_m_w_in', 'new_m_w_out', 'new_m_lru_conv_w', 'new_m_lru_conv_b', 'new_m_lru_gate_a_w', 'new_m_lru_gate_a_b', 'new_m_lru_gate_x_w', 'new_m_lru_gate_x_b', 'new_m_lru_lambda', 'new_m_lru_out_norm', 'new_m_rwkv_mu', 'new_m_rwkv_w_up', 'new_m_rwkv_w_bias', 'new_m_rwkv_a_up', 'new_m_rwkv_a_bias', 'new_m_rwkv_g_up', 'new_m_rwkv_k_k', 'new_m_rwkv_k_a', 'new_m_rwkv_r_k', 'new_m_rwkv_ln_g', 'new_m_rwkv_ln_b', 'new_m_rwkv_vres_w1', 'new_m_rwkv_vres_w2', 'new_m_rwkv_vres_b', 'new_m_gdn_conv_w', 'new_m_gdn_a_log', 'new_m_gdn_dt_bias', 'new_m_gdn_norm', 'new_m_ffn2_norm', 'new_m_ffn2_wi', 'new_m_ffn2_wo', 'new_m_final_norm', 'new_v_ffn1_norm', 'new_v_ffn1_wi', 'new_v_ffn1_wo', 'new_v_mix_norm', 'new_v_w_in', 'new_v_w_out', 'new_v_lru_conv_w', 'new_v_lru_conv_b', 'new_v_lru_gate_a_w', 'new_v_lru_gate_a_b', 'new_v_lru_gate_x_w', 'new_v_lru_gate_x_b', 'new_v_lru_lambda', 'new_v_lru_out_norm', 'new_v_rwkv_mu', 'new_v_rwkv_w_up', 'new_v_rwkv_w_bias', 'new_v_rwkv_a_up', 'new_v_rwkv_a_bias', 'new_v_rwkv_g_up', 'new_v_rwkv_k_k', 'new_v_rwkv_k_a', 'new_v_rwkv_r_k', 'new_v_rwkv_ln_g', 'new_v_rwkv_ln_b', 'new_v_rwkv_vres_w1', 'new_v_rwkv_vres_w2', 'new_v_rwkv_vres_b', 'new_v_gdn_conv_w', 'new_v_gdn_a_log', 'new_v_gdn_dt_bias', 'new_v_gdn_norm', 'new_v_ffn2_norm', 'new_v_ffn2_wi', 'new_v_ffn2_wo', 'new_v_final_norm']
TWIN_LEAF_KINDS = {'loss': 'loss', 'grad_x': 'grad_x', 'grad_ffn1_norm': 'grad_w', 'grad_ffn1_wi': 'grad_w', 'grad_ffn1_wo': 'grad_w', 'grad_mix_norm': 'grad_w', 'grad_w_in': 'grad_w', 'grad_w_out': 'grad_w', 'grad_lru_conv_w': 'grad_w', 'grad_lru_conv_b': 'grad_w', 'grad_lru_gate_a_w': 'grad_w', 'grad_lru_gate_a_b': 'grad_w', 'grad_lru_gate_x_w': 'grad_w', 'grad_lru_gate_x_b': 'grad_w', 'grad_lru_lambda': 'grad_w', 'grad_lru_out_norm': 'grad_w', 'grad_rwkv_mu': 'grad_w', 'grad_rwkv_w_up': 'grad_w', 'grad_rwkv_w_bias': 'grad_w', 'grad_rwkv_a_up': 'grad_w', 'grad_rwkv_a_bias': 'grad_w', 'grad_rwkv_g_up': 'grad_w', 'grad_rwkv_k_k': 'grad_w', 'grad_rwkv_k_a': 'grad_w', 'grad_rwkv_r_k': 'grad_w', 'grad_rwkv_ln_g': 'grad_w', 'grad_rwkv_ln_b': 'grad_w', 'grad_rwkv_vres_w1': 'grad_w', 'grad_rwkv_vres_w2': 'grad_w', 'grad_rwkv_vres_b': 'grad_w', 'grad_gdn_conv_w': 'grad_w', 'grad_gdn_a_log': 'grad_w', 'grad_gdn_dt_bias': 'grad_w', 'grad_gdn_norm': 'grad_w', 'grad_ffn2_norm': 'grad_w', 'grad_ffn2_wi': 'grad_w', 'grad_ffn2_wo': 'grad_w', 'grad_final_norm': 'grad_w', 'delta_ffn1_norm': 'delta_w', 'delta_ffn1_wi': 'delta_w', 'delta_ffn1_wo': 'delta_w', 'delta_mix_norm': 'delta_w', 'delta_w_in': 'delta_w', 'delta_w_out': 'delta_w', 'delta_lru_conv_w': 'delta_w', 'delta_lru_conv_b': 'delta_w', 'delta_lru_gate_a_w': 'delta_w', 'delta_lru_gate_a_b': 'delta_w', 'delta_lru_gate_x_w': 'delta_w', 'delta_lru_gate_x_b': 'delta_w', 'delta_lru_lambda': 'delta_w', 'delta_lru_out_norm': 'delta_w', 'delta_rwkv_mu': 'delta_w', 'delta_rwkv_w_up': 'delta_w', 'delta_rwkv_w_bias': 'delta_w', 'delta_rwkv_a_up': 'delta_w', 'delta_rwkv_a_bias': 'delta_w', 'delta_rwkv_g_up': 'delta_w', 'delta_rwkv_k_k': 'delta_w', 'delta_rwkv_k_a': 'delta_w', 'delta_rwkv_r_k': 'delta_w', 'delta_rwkv_ln_g': 'delta_w', 'delta_rwkv_ln_b': 'delta_w', 'delta_rwkv_vres_w1': 'delta_w', 'delta_rwkv_vres_w2': 'delta_w', 'delta_rwkv_vres_b': 'delta_w', 'delta_gdn_conv_w': 'delta_w', 'delta_gdn_a_log': 'delta_w', 'delta_gdn_dt_bias': 'delta_w', 'delta_gdn_norm': 'delta_w', 'delta_ffn2_norm': 'delta_w', 'delta_ffn2_wi': 'delta_w', 'delta_ffn2_wo': 'delta_w', 'delta_final_norm': 'delta_w', 'new_m_ffn1_norm': 'new_m', 'new_m_ffn1_wi': 'new_m', 'new_m_ffn1_wo': 'new_m', 'new_m_mix_norm': 'new_m', 'new_m_w_in': 'new_m', 'new_m_w_out': 'new_m', 'new_m_lru_conv_w': 'new_m', 'new_m_lru_conv_b': 'new_m', 'new_m_lru_gate_a_w': 'new_m', 'new_m_lru_gate_a_b': 'new_m', 'new_m_lru_gate_x_w': 'new_m', 'new_m_lru_gate_x_b': 'new_m', 'new_m_lru_lambda': 'new_m', 'new_m_lru_out_norm': 'new_m', 'new_m_rwkv_mu': 'new_m', 'new_m_rwkv_w_up': 'new_m', 'new_m_rwkv_w_bias': 'new_m', 'new_m_rwkv_a_up': 'new_m', 'new_m_rwkv_a_bias': 'new_m', 'new_m_rwkv_g_up': 'new_m', 'new_m_rwkv_k_k': 'new_m', 'new_m_rwkv_k_a': 'new_m', 'new_m_rwkv_r_k': 'new_m', 'new_m_rwkv_ln_g': 'new_m', 'new_m_rwkv_ln_b': 'new_m', 'new_m_rwkv_vres_w1': 'new_m', 'new_m_rwkv_vres_w2': 'new_m', 'new_m_rwkv_vres_b': 'new_m', 'new_m_gdn_conv_w': 'new_m', 'new_m_gdn_a_log': 'new_m', 'new_m_gdn_dt_bias': 'new_m', 'new_m_gdn_norm': 'new_m', 'new_m_ffn2_norm': 'new_m', 'new_m_ffn2_wi': 'new_m', 'new_m_ffn2_wo': 'new_m', 'new_m_final_norm': 'new_m', 'new_v_ffn1_norm': 'new_v', 'new_v_ffn1_wi': 'new_v', 'new_v_ffn1_wo': 'new_v', 'new_v_mix_norm': 'new_v', 'new_v_w_in': 'new_v', 'new_v_w_out': 'new_v', 'new_v_lru_conv_w': 'new_v', 'new_v_lru_conv_b': 'new_v', 'new_v_lru_gate_a_w': 'new_v', 'new_v_lru_gate_a_b': 'new_v', 'new_v_lru_gate_x_w': 'new_v', 'new_v_lru_gate_x_b': 'new_v', 'new_v_lru_lambda': 'new_v', 'new_v_lru_out_norm': 'new_v', 'new_v_rwkv_mu': 'new_v', 'new_v_rwkv_w_up': 'new_v', 'new_v_rwkv_w_bias': 'new_v', 'new_v_rwkv_a_up': 'new_v', 'new_v_rwkv_a_bias': 'new_v', 'new_v_rwkv_g_up': 'new_v', 'new_v_rwkv_k_k': 'new_v', 'new_v_rwkv_k_a': 'new_v', 'new_v_rwkv_r_k': 'new_v', 'new_v_rwkv_ln_g': 'new_v', 'new_v_rwkv_ln_b': 'new_v', 'new_v_rwkv_vres_w1': 'new_v', 'new_v_rwkv_vres_w2': 'new_v', 'new_v_rwkv_vres_b': 'new_v', 'new_v_gdn_conv_w': 'new_v', 'new_v_gdn_a_log': 'new_v', 'new_v_gdn_dt_bias': 'new_v', 'new_v_gdn_norm': 'new_v', 'new_v_ffn2_norm': 'new_v', 'new_v_ffn2_wi': 'new_v', 'new_v_ffn2_wo': 'new_v', 'new_v_final_norm': 'new_v'}


def _forward(args):
    return _fwd_reference(*[args[k] for k in FWD_PARAMS])


def _output_shape():
    out = _jax.eval_shape(lambda: _forward(_fwd_setup_inputs(0)))
    return out.shape, out.dtype

N_MICROBATCH = 1
ADAM_LR = 0.001
ADAM_B1 = 0.9
ADAM_B2 = 0.999
ADAM_EPS = 1e-08
ADAM_WD = 0.01
ADAM_STEP = 10
PER_EXAMPLE_BATCH_AXIS = {'x': 0, 'loss_target': 0}
SHARED_INPUTS = []
_WEIGHT_DTYPES = {'ffn1_norm': _jnp.float32, 'ffn1_wi': _jnp.float32, 'ffn1_wo': _jnp.float32, 'mix_norm': _jnp.float32, 'w_in': _jnp.float32, 'w_out': _jnp.float32, 'lru_conv_w': _jnp.float32, 'lru_conv_b': _jnp.float32, 'lru_gate_a_w': _jnp.float32, 'lru_gate_a_b': _jnp.float32, 'lru_gate_x_w': _jnp.float32, 'lru_gate_x_b': _jnp.float32, 'lru_lambda': _jnp.float32, 'lru_out_norm': _jnp.float32, 'rwkv_mu': _jnp.float32, 'rwkv_w_up': _jnp.float32, 'rwkv_w_bias': _jnp.float32, 'rwkv_a_up': _jnp.float32, 'rwkv_a_bias': _jnp.float32, 'rwkv_g_up': _jnp.float32, 'rwkv_k_k': _jnp.float32, 'rwkv_k_a': _jnp.float32, 'rwkv_r_k': _jnp.float32, 'rwkv_ln_g': _jnp.float32, 'rwkv_ln_b': _jnp.float32, 'rwkv_vres_w1': _jnp.float32, 'rwkv_vres_w2': _jnp.float32, 'rwkv_vres_b': _jnp.float32, 'gdn_conv_w': _jnp.float32, 'gdn_a_log': _jnp.float32, 'gdn_dt_bias': _jnp.float32, 'gdn_norm': _jnp.float32, 'ffn2_norm': _jnp.float32, 'ffn2_wi': _jnp.float32, 'ffn2_wo': _jnp.float32, 'final_norm': _jnp.float32}
MOMENT_SCALE = {'ffn1_norm': 6.676896e-02, 'ffn1_wi': 2.842228e-02, 'ffn1_wo': 4.638364e-02, 'mix_norm': 1.298948e-01, 'w_in': 6.920180e-02, 'w_out': 7.888709e-02, 'lru_conv_w': 1.318627e-01, 'lru_conv_b': 1.057587e+00, 'lru_gate_a_w': 3.458307e-02, 'lru_gate_a_b': 2.759067e-02, 'lru_gate_x_w': 6.094435e-02, 'lru_gate_x_b': 4.316570e-02, 'lru_lambda': 6.221640e-02, 'lru_out_norm': 1.400636e-01, 'rwkv_mu': 9.705454e-02, 'rwkv_w_up': 2.807128e-03, 'rwkv_w_bias': 2.296216e-02, 'rwkv_a_up': 2.335623e-02, 'rwkv_a_bias': 2.456127e-02, 'rwkv_g_up': 5.807955e-02, 'rwkv_k_k': 8.452200e-02, 'rwkv_k_a': 6.545697e-02, 'rwkv_r_k': 1.268513e-01, 'rwkv_ln_g': 6.258303e-02, 'rwkv_ln_b': 1.146949e-01, 'rwkv_vres_w1': 2.328452e-02, 'rwkv_vres_w2': 1.312802e-02, 'rwkv_vres_b': 1.965102e-02, 'gdn_conv_w': 5.949575e-02, 'gdn_a_log': 2.522707e-01, 'gdn_dt_bias': 2.521030e-01, 'gdn_norm': 1.605729e-01, 'ffn2_norm': 4.286153e-02, 'ffn2_wi': 1.838075e-02, 'ffn2_wo': 2.999422e-02, 'final_norm': 1.601494e+01}


def _to_microbatches(a, axis):
    t = _jnp.moveaxis(a, axis, 0)
    t = t.reshape((N_MICROBATCH, t.shape[0] // N_MICROBATCH) + t.shape[1:])
    return _jnp.moveaxis(t, 1, axis + 1)


def setup_inputs(seed: int = 0) -> dict:
    inp = _fwd_setup_inputs(seed)
    key = _jax.random.fold_in(_jax.random.key(seed), 7919)
    shape, _ = _output_shape()
    out = dict(inp)
    out["loss_target"] = _jax.random.normal(_jax.random.fold_in(key, 0), shape, _jnp.float32)
    for i, name in enumerate(TWIN_WEIGHTS):
        w = inp[name].astype(_jnp.float32)
        if MOMENT_SCALE is None:
            s = _jnp.sqrt(_jnp.mean(_jnp.square(w)) + 1e-30)
        else:
            s = MOMENT_SCALE[name]
        km, kv = _jax.random.split(_jax.random.fold_in(key, i + 1))
        out[name] = w
        out["m_" + name] = s * _jax.random.normal(km, w.shape, _jnp.float32)
        out["v_" + name] = (s * s) * _jax.random.uniform(kv, w.shape, _jnp.float32, 0.5, 1.5)
    if N_MICROBATCH > 1:
        for name, axis in PER_EXAMPLE_BATCH_AXIS.items():
            out[name] = _to_microbatches(out[name], axis)
    return {'x': out['x'], 'ffn1_norm': out['ffn1_norm'], 'ffn1_wi': out['ffn1_wi'], 'ffn1_wo': out['ffn1_wo'], 'mix_norm': out['mix_norm'], 'w_in': out['w_in'], 'w_out': out['w_out'], 'lru_conv_w': out['lru_conv_w'], 'lru_conv_b': out['lru_conv_b'], 'lru_gate_a_w': out['lru_gate_a_w'], 'lru_gate_a_b': out['lru_gate_a_b'], 'lru_gate_x_w': out['lru_gate_x_w'], 'lru_gate_x_b': out['lru_gate_x_b'], 'lru_lambda': out['lru_lambda'], 'lru_out_norm': out['lru_out_norm'], 'rwkv_mu': out['rwkv_mu'], 'rwkv_w_up': out['rwkv_w_up'], 'rwkv_w_bias': out['rwkv_w_bias'], 'rwkv_a_up': out['rwkv_a_up'], 'rwkv_a_bias': out['rwkv_a_bias'], 'rwkv_g_up': out['rwkv_g_up'], 'rwkv_k_k': out['rwkv_k_k'], 'rwkv_k_a': out['rwkv_k_a'], 'rwkv_r_k': out['rwkv_r_k'], 'rwkv_ln_g': out['rwkv_ln_g'], 'rwkv_ln_b': out['rwkv_ln_b'], 'rwkv_vres_w1': out['rwkv_vres_w1'], 'rwkv_vres_w2': out['rwkv_vres_w2'], 'rwkv_vres_b': out['rwkv_vres_b'], 'gdn_conv_w': out['gdn_conv_w'], 'gdn_a_log': out['gdn_a_log'], 'gdn_dt_bias': out['gdn_dt_bias'], 'gdn_norm': out['gdn_norm'], 'ffn2_norm': out['ffn2_norm'], 'ffn2_wi': out['ffn2_wi'], 'ffn2_wo': out['ffn2_wo'], 'final_norm': out['final_norm'], 'loss_target': out['loss_target'], 'm_ffn1_norm': out['m_ffn1_norm'], 'm_ffn1_wi': out['m_ffn1_wi'], 'm_ffn1_wo': out['m_ffn1_wo'], 'm_mix_norm': out['m_mix_norm'], 'm_w_in': out['m_w_in'], 'm_w_out': out['m_w_out'], 'm_lru_conv_w': out['m_lru_conv_w'], 'm_lru_conv_b': out['m_lru_conv_b'], 'm_lru_gate_a_w': out['m_lru_gate_a_w'], 'm_lru_gate_a_b': out['m_lru_gate_a_b'], 'm_lru_gate_x_w': out['m_lru_gate_x_w'], 'm_lru_gate_x_b': out['m_lru_gate_x_b'], 'm_lru_lambda': out['m_lru_lambda'], 'm_lru_out_norm': out['m_lru_out_norm'], 'm_rwkv_mu': out['m_rwkv_mu'], 'm_rwkv_w_up': out['m_rwkv_w_up'], 'm_rwkv_w_bias': out['m_rwkv_w_bias'], 'm_rwkv_a_up': out['m_rwkv_a_up'], 'm_rwkv_a_bias': out['m_rwkv_a_bias'], 'm_rwkv_g_up': out['m_rwkv_g_up'], 'm_rwkv_k_k': out['m_rwkv_k_k'], 'm_rwkv_k_a': out['m_rwkv_k_a'], 'm_rwkv_r_k': out['m_rwkv_r_k'], 'm_rwkv_ln_g': out['m_rwkv_ln_g'], 'm_rwkv_ln_b': out['m_rwkv_ln_b'], 'm_rwkv_vres_w1': out['m_rwkv_vres_w1'], 'm_rwkv_vres_w2': out['m_rwkv_vres_w2'], 'm_rwkv_vres_b': out['m_rwkv_vres_b'], 'm_gdn_conv_w': out['m_gdn_conv_w'], 'm_gdn_a_log': out['m_gdn_a_log'], 'm_gdn_dt_bias': out['m_gdn_dt_bias'], 'm_gdn_norm': out['m_gdn_norm'], 'm_ffn2_norm': out['m_ffn2_norm'], 'm_ffn2_wi': out['m_ffn2_wi'], 'm_ffn2_wo': out['m_ffn2_wo'], 'm_final_norm': out['m_final_norm'], 'v_ffn1_norm': out['v_ffn1_norm'], 'v_ffn1_wi': out['v_ffn1_wi'], 'v_ffn1_wo': out['v_ffn1_wo'], 'v_mix_norm': out['v_mix_norm'], 'v_w_in': out['v_w_in'], 'v_w_out': out['v_w_out'], 'v_lru_conv_w': out['v_lru_conv_w'], 'v_lru_conv_b': out['v_lru_conv_b'], 'v_lru_gate_a_w': out['v_lru_gate_a_w'], 'v_lru_gate_a_b': out['v_lru_gate_a_b'], 'v_lru_gate_x_w': out['v_lru_gate_x_w'], 'v_lru_gate_x_b': out['v_lru_gate_x_b'], 'v_lru_lambda': out['v_lru_lambda'], 'v_lru_out_norm': out['v_lru_out_norm'], 'v_rwkv_mu': out['v_rwkv_mu'], 'v_rwkv_w_up': out['v_rwkv_w_up'], 'v_rwkv_w_bias': out['v_rwkv_w_bias'], 'v_rwkv_a_up': out['v_rwkv_a_up'], 'v_rwkv_a_bias': out['v_rwkv_a_bias'], 'v_rwkv_g_up': out['v_rwkv_g_up'], 'v_rwkv_k_k': out['v_rwkv_k_k'], 'v_rwkv_k_a': out['v_rwkv_k_a'], 'v_rwkv_r_k': out['v_rwkv_r_k'], 'v_rwkv_ln_g': out['v_rwkv_ln_g'], 'v_rwkv_ln_b': out['v_rwkv_ln_b'], 'v_rwkv_vres_w1': out['v_rwkv_vres_w1'], 'v_rwkv_vres_w2': out['v_rwkv_vres_w2'], 'v_rwkv_vres_b': out['v_rwkv_vres_b'], 'v_gdn_conv_w': out['v_gdn_conv_w'], 'v_gdn_a_log': out['v_gdn_a_log'], 'v_gdn_dt_bias': out['v_gdn_dt_bias'], 'v_gdn_norm': out['v_gdn_norm'], 'v_ffn2_norm': out['v_ffn2_norm'], 'v_ffn2_wi': out['v_ffn2_wi'], 'v_ffn2_wo': out['v_ffn2_wo'], 'v_final_norm': out['v_final_norm']}


def _loss(weights, diff, rest, loss_target):
    with _jax.named_scope("forward"):
        args = {**rest, TWIN_DIFF_INPUT: diff, **{k: w.astype(_WEIGHT_DTYPES[k]) for k, w in weights.items()}}
        y = _forward(args)
    with _jax.named_scope("loss_head"):
        err = _jnp.square(y.astype(_jnp.float32) - loss_target)
        return 0.5 * _jnp.sum(_jnp.mean(err, axis=-1)) if err.ndim else 0.5 * err


def _adamw(w, g, m, v):
    m = ADAM_B1 * m + (1.0 - ADAM_B1) * g
    v = ADAM_B2 * v + (1.0 - ADAM_B2) * _jnp.square(g)
    m_hat = m / (1.0 - ADAM_B1 ** ADAM_STEP)
    v_hat = v / (1.0 - ADAM_B2 ** ADAM_STEP)
    delta = -ADAM_LR * (m_hat / (_jnp.sqrt(v_hat) + ADAM_EPS) + ADAM_WD * w)
    return delta, m, v


def reference(x, ffn1_norm, ffn1_wi, ffn1_wo, mix_norm, w_in, w_out, lru_conv_w, lru_conv_b, lru_gate_a_w, lru_gate_a_b, lru_gate_x_w, lru_gate_x_b, lru_lambda, lru_out_norm, rwkv_mu, rwkv_w_up, rwkv_w_bias, rwkv_a_up, rwkv_a_bias, rwkv_g_up, rwkv_k_k, rwkv_k_a, rwkv_r_k, rwkv_ln_g, rwkv_ln_b, rwkv_vres_w1, rwkv_vres_w2, rwkv_vres_b, gdn_conv_w, gdn_a_log, gdn_dt_bias, gdn_norm, ffn2_norm, ffn2_wi, ffn2_wo, final_norm, loss_target, m_ffn1_norm, m_ffn1_wi, m_ffn1_wo, m_mix_norm, m_w_in, m_w_out, m_lru_conv_w, m_lru_conv_b, m_lru_gate_a_w, m_lru_gate_a_b, m_lru_gate_x_w, m_lru_gate_x_b, m_lru_lambda, m_lru_out_norm, m_rwkv_mu, m_rwkv_w_up, m_rwkv_w_bias, m_rwkv_a_up, m_rwkv_a_bias, m_rwkv_g_up, m_rwkv_k_k, m_rwkv_k_a, m_rwkv_r_k, m_rwkv_ln_g, m_rwkv_ln_b, m_rwkv_vres_w1, m_rwkv_vres_w2, m_rwkv_vres_b, m_gdn_conv_w, m_gdn_a_log, m_gdn_dt_bias, m_gdn_norm, m_ffn2_norm, m_ffn2_wi, m_ffn2_wo, m_final_norm, v_ffn1_norm, v_ffn1_wi, v_ffn1_wo, v_mix_norm, v_w_in, v_w_out, v_lru_conv_w, v_lru_conv_b, v_lru_gate_a_w, v_lru_gate_a_b, v_lru_gate_x_w, v_lru_gate_x_b, v_lru_lambda, v_lru_out_norm, v_rwkv_mu, v_rwkv_w_up, v_rwkv_w_bias, v_rwkv_a_up, v_rwkv_a_bias, v_rwkv_g_up, v_rwkv_k_k, v_rwkv_k_a, v_rwkv_r_k, v_rwkv_ln_g, v_rwkv_ln_b, v_rwkv_vres_w1, v_rwkv_vres_w2, v_rwkv_vres_b, v_gdn_conv_w, v_gdn_a_log, v_gdn_dt_bias, v_gdn_norm, v_ffn2_norm, v_ffn2_wi, v_ffn2_wo, v_final_norm):
    given = dict(x=x, ffn1_norm=ffn1_norm, ffn1_wi=ffn1_wi, ffn1_wo=ffn1_wo, mix_norm=mix_norm, w_in=w_in, w_out=w_out, lru_conv_w=lru_conv_w, lru_conv_b=lru_conv_b, lru_gate_a_w=lru_gate_a_w, lru_gate_a_b=lru_gate_a_b, lru_gate_x_w=lru_gate_x_w, lru_gate_x_b=lru_gate_x_b, lru_lambda=lru_lambda, lru_out_norm=lru_out_norm, rwkv_mu=rwkv_mu, rwkv_w_up=rwkv_w_up, rwkv_w_bias=rwkv_w_bias, rwkv_a_up=rwkv_a_up, rwkv_a_bias=rwkv_a_bias, rwkv_g_up=rwkv_g_up, rwkv_k_k=rwkv_k_k, rwkv_k_a=rwkv_k_a, rwkv_r_k=rwkv_r_k, rwkv_ln_g=rwkv_ln_g, rwkv_ln_b=rwkv_ln_b, rwkv_vres_w1=rwkv_vres_w1, rwkv_vres_w2=rwkv_vres_w2, rwkv_vres_b=rwkv_vres_b, gdn_conv_w=gdn_conv_w, gdn_a_log=gdn_a_log, gdn_dt_bias=gdn_dt_bias, gdn_norm=gdn_norm, ffn2_norm=ffn2_norm, ffn2_wi=ffn2_wi, ffn2_wo=ffn2_wo, final_norm=final_norm, loss_target=loss_target, m_ffn1_norm=m_ffn1_norm, m_ffn1_wi=m_ffn1_wi, m_ffn1_wo=m_ffn1_wo, m_mix_norm=m_mix_norm, m_w_in=m_w_in, m_w_out=m_w_out, m_lru_conv_w=m_lru_conv_w, m_lru_conv_b=m_lru_conv_b, m_lru_gate_a_w=m_lru_gate_a_w, m_lru_gate_a_b=m_lru_gate_a_b, m_lru_gate_x_w=m_lru_gate_x_w, m_lru_gate_x_b=m_lru_gate_x_b, m_lru_lambda=m_lru_lambda, m_lru_out_norm=m_lru_out_norm, m_rwkv_mu=m_rwkv_mu, m_rwkv_w_up=m_rwkv_w_up, m_rwkv_w_bias=m_rwkv_w_bias, m_rwkv_a_up=m_rwkv_a_up, m_rwkv_a_bias=m_rwkv_a_bias, m_rwkv_g_up=m_rwkv_g_up, m_rwkv_k_k=m_rwkv_k_k, m_rwkv_k_a=m_rwkv_k_a, m_rwkv_r_k=m_rwkv_r_k, m_rwkv_ln_g=m_rwkv_ln_g, m_rwkv_ln_b=m_rwkv_ln_b, m_rwkv_vres_w1=m_rwkv_vres_w1, m_rwkv_vres_w2=m_rwkv_vres_w2, m_rwkv_vres_b=m_rwkv_vres_b, m_gdn_conv_w=m_gdn_conv_w, m_gdn_a_log=m_gdn_a_log, m_gdn_dt_bias=m_gdn_dt_bias, m_gdn_norm=m_gdn_norm, m_ffn2_norm=m_ffn2_norm, m_ffn2_wi=m_ffn2_wi, m_ffn2_wo=m_ffn2_wo, m_final_norm=m_final_norm, v_ffn1_norm=v_ffn1_norm, v_ffn1_wi=v_ffn1_wi, v_ffn1_wo=v_ffn1_wo, v_mix_norm=v_mix_norm, v_w_in=v_w_in, v_w_out=v_w_out, v_lru_conv_w=v_lru_conv_w, v_lru_conv_b=v_lru_conv_b, v_lru_gate_a_w=v_lru_gate_a_w, v_lru_gate_a_b=v_lru_gate_a_b, v_lru_gate_x_w=v_lru_gate_x_w, v_lru_gate_x_b=v_lru_gate_x_b, v_lru_lambda=v_lru_lambda, v_lru_out_norm=v_lru_out_norm, v_rwkv_mu=v_rwkv_mu, v_rwkv_w_up=v_rwkv_w_up, v_rwkv_w_bias=v_rwkv_w_bias, v_rwkv_a_up=v_rwkv_a_up, v_rwkv_a_bias=v_rwkv_a_bias, v_rwkv_g_up=v_rwkv_g_up, v_rwkv_k_k=v_rwkv_k_k, v_rwkv_k_a=v_rwkv_k_a, v_rwkv_r_k=v_rwkv_r_k, v_rwkv_ln_g=v_rwkv_ln_g, v_rwkv_ln_b=v_rwkv_ln_b, v_rwkv_vres_w1=v_rwkv_vres_w1, v_rwkv_vres_w2=v_rwkv_vres_w2, v_rwkv_vres_b=v_rwkv_vres_b, v_gdn_conv_w=v_gdn_conv_w, v_gdn_a_log=v_gdn_a_log, v_gdn_dt_bias=v_gdn_dt_bias, v_gdn_norm=v_gdn_norm, v_ffn2_norm=v_ffn2_norm, v_ffn2_wi=v_ffn2_wi, v_ffn2_wo=v_ffn2_wo, v_final_norm=v_final_norm)
    weights = {n: given[n] for n in TWIN_WEIGHTS}
    shared = {n: given[n] for n in SHARED_INPUTS}
    per_example = {n: given[n] for n in ['x']}
    grad_fn = _jax.value_and_grad(_loss, argnums=(0, 1))

    def one_microbatch(ex, loss_target):
        ex = dict(ex)
        diff = ex.pop(TWIN_DIFF_INPUT)
        return grad_fn(weights, diff, {**shared, **ex}, loss_target)

    if N_MICROBATCH == 1:
        loss, (grad_w, grad_x) = one_microbatch(per_example, given["loss_target"])
    else:
        def body(carry, xs):
            loss_sum, grad_sum = carry
            l_k, (gw_k, gx_k) = one_microbatch(xs[0], xs[1])
            with _jax.named_scope("update"):
                return (loss_sum + l_k, _jax.tree.map(_jnp.add, grad_sum, gw_k)), gx_k

        init = (_jnp.zeros((), _jnp.float32), _jax.tree.map(_jnp.zeros_like, weights))
        (loss, grad_w), grad_x = _jax.lax.scan(body, init, (per_example, given["loss_target"]))
    with _jax.named_scope("update"):
        delta_w, new_m, new_v = {}, {}, {}
        for n in TWIN_WEIGHTS:
            delta_w[n], new_m[n], new_v[n] = _adamw(weights[n], grad_w[n], given["m_" + n], given["v_" + n])
    return (loss, grad_x, *[grad_w[n] for n in TWIN_WEIGHTS], *[delta_w[n] for n in TWIN_WEIGHTS],
            *[new_m[n] for n in TWIN_WEIGHTS], *[new_v[n] for n in TWIN_WEIGHTS])
```

```python
import functools

import numpy as np
import jax
import jax.numpy as jnp
from jax import lax
from jax.experimental import pallas as pl
from jax.experimental.pallas import tpu as pltpu

F32 = jnp.float32
BF16 = jnp.bfloat16
HIGHEST = lax.Precision.HIGHEST
MESH = pl.DeviceIdType.MESH

D_MODEL = 1024
D_FF = 2816
N_LAYERS = 2
HEADS = 6
HEAD_DIM = 64
MIX_W = HEADS * HEAD_DIM
LRU_W = 256
LRU_BLOCKS = 4
RWKV_IN = 1408
D_IN = 3468
D_IN_PAD = 3584
NORM_EPS = 1e-6
GN_EPS = 64e-5
LRU_C = 8.0
CHUNK = 64
ROWS = 256
FF_CHUNK = 256
IN_CHUNK = 512
PACK_W = 1024
VMEM_LIMIT = 56 * 1024 * 1024

ADAM_LR, ADAM_B1, ADAM_B2, ADAM_EPS, ADAM_WD, ADAM_STEP = 0.001, 0.9, 0.999, 1e-08, 0.01, 10

WEIGHTS = ['ffn1_norm', 'ffn1_wi', 'ffn1_wo', 'mix_norm', 'w_in', 'w_out', 'lru_conv_w', 'lru_conv_b',
           'lru_gate_a_w', 'lru_gate_a_b', 'lru_gate_x_w', 'lru_gate_x_b', 'lru_lambda', 'lru_out_norm',
           'rwkv_mu', 'rwkv_w_up', 'rwkv_w_bias', 'rwkv_a_up', 'rwkv_a_bias', 'rwkv_g_up', 'rwkv_k_k',
           'rwkv_k_a', 'rwkv_r_k', 'rwkv_ln_g', 'rwkv_ln_b', 'rwkv_vres_w1', 'rwkv_vres_w2', 'rwkv_vres_b',
           'gdn_conv_w', 'gdn_a_log', 'gdn_dt_bias', 'gdn_norm', 'ffn2_norm', 'ffn2_wi', 'ffn2_wo', 'final_norm']
BIG = {'ffn1_wi': 2, 'ffn1_wo': 1, 'w_in': 2, 'w_out': 1, 'ffn2_wi': 2, 'ffn2_wo': 1}
SMALL_SHARDED = {'lru_conv_w': 2, 'rwkv_w_up': 2, 'rwkv_a_up': 2, 'rwkv_g_up': 2, 'rwkv_vres_w1': 1,
                 'rwkv_vres_w2': 2, 'gdn_conv_w': 2}
N_CHIPS = 4


def _params(sem=None):
    kw = dict(vmem_limit_bytes=VMEM_LIMIT)
    if sem is not None:
        kw['dimension_semantics'] = sem
    return pltpu.CompilerParams(**kw)


def _bdot(a, b, dims=(((1,), (0,)), ((), ()))):
    return lax.dot_general(a.astype(BF16), b.astype(BF16), dims, preferred_element_type=F32)


def _bdot_nt(a, b):
    return _bdot(a, b, (((1,), (1,)), ((), ())))


def _bdot_tn(a, b):
    return _bdot(a, b, (((0,), (0,)), ((), ())))


CORE_DT = F32


def _cdot(a, b, dims=(((1,), (0,)), ((), ()))):
    return lax.dot_general(a.astype(CORE_DT), b.astype(CORE_DT), dims, precision=HIGHEST,
                           preferred_element_type=F32)


def _cdot_nt(a, b):
    return _cdot(a, b, (((1,), (1,)), ((), ())))


def _cdot_tn(a, b):
    return _cdot(a, b, (((0,), (0,)), ((), ())))


def _hdot(a, b):
    return jnp.dot(a, b, precision=HIGHEST, preferred_element_type=F32)


def _iota2(n, m):
    return lax.broadcasted_iota(jnp.int32, (n, m), 0), lax.broadcasted_iota(jnp.int32, (n, m), 1)


def _segsum(x):
    w = x.shape[-1]
    ri, ci = _iota2(w, w)
    bd = (ri // HEAD_DIM == ci // HEAD_DIM).astype(F32)
    return _hdot(x, bd)


def _rms(x, g):
    return x * lax.rsqrt(jnp.mean(x * x, axis=-1, keepdims=True) + NORM_EPS) * g


def _row_loop(n_rows, fn):
    def step(i, c):
        fn(pl.ds(pl.multiple_of(i * ROWS, ROWS), ROWS))
        return c
    lax.fori_loop(0, n_rows // ROWS, step, 0)


def ffn_fwd(x, g, wi, wo, name):
    T = x.shape[0]
    nj = D_FF // FF_CHUNK

    def body(x_ref, g_ref, wg_ref, wu_ref, wo_ref, o_ref, h_ref, acc_ref):
        j = pl.program_id(0)

        @pl.when(j == 0)
        def _():
            def init(r):
                h_ref[r, :] = _rms(x_ref[r, :], g_ref[...]).astype(BF16)
                acc_ref[r, :] = jnp.zeros((ROWS, D_MODEL), F32)
            _row_loop(T, init)

        def blk(r):
            hb = h_ref[r, :]
            gate = jnp.dot(hb, wg_ref[...], preferred_element_type=F32)
            up = jnp.dot(hb, wu_ref[...], preferred_element_type=F32)
            a = (gate * jax.nn.sigmoid(gate) * up).astype(BF16)
            acc_ref[r, :] += jnp.dot(a, wo_ref[...], preferred_element_type=F32)
        _row_loop(T, blk)

        @pl.when(j == nj - 1)
        def _():
            def fin(r):
                o_ref[r, :] = x_ref[r, :] + 0.5 * acc_ref[r, :]
            _row_loop(T, fin)

    full = pl.BlockSpec((T, D_MODEL), lambda j: (0, 0))
    return pl.pallas_call(
        body, name=name, grid=(nj,),
        in_specs=[full, pl.BlockSpec((1, D_MODEL), lambda j: (0, 0)),
                  pl.BlockSpec((D_MODEL, FF_CHUNK), lambda j: (0, j)),
                  pl.BlockSpec((D_MODEL, FF_CHUNK), lambda j: (0, j + nj)),
                  pl.BlockSpec((FF_CHUNK, D_MODEL), lambda j: (j, 0))],
        out_specs=full, out_shape=jax.ShapeDtypeStruct((T, D_MODEL), F32),
        scratch_shapes=[pltpu.VMEM((T, D_MODEL), BF16), pltpu.VMEM((T, D_MODEL), F32)],
        compiler_params=_params(("arbitrary",)))(x, g, wi, wi, wo)


def _norm_bwd_rows(x, g, dh, dres):
    rstd = lax.rsqrt(jnp.mean(x * x, axis=-1, keepdims=True) + NORM_EPS)
    xh = x * rstd
    dxh = dh * g
    dx = rstd * (dxh - xh * jnp.mean(dxh * xh, axis=-1, keepdims=True))
    return dres + dx, jnp.sum(dh * xh, axis=0, keepdims=True)


def ffn_bwd(x, dy, g, wi, wo, name):
    T = x.shape[0]
    nj = D_FF // FF_CHUNK

    def body(x_ref, dy_ref, g_ref, wg_ref, wu_ref, wo_ref, dx_ref, dg_ref, dwg_ref, dwu_ref, dwo_ref,
             h_ref, da_ref, dh_ref):
        j = pl.program_id(0)

        @pl.when(j == 0)
        def _():
            def init(r):
                h_ref[r, :] = _rms(x_ref[r, :], g_ref[...]).astype(BF16)
                da_ref[r, :] = (0.5 * dy_ref[r, :]).astype(BF16)
                dh_ref[r, :] = jnp.zeros((ROWS, D_MODEL), F32)
            _row_loop(T, init)

        dwg_ref[...] = jnp.zeros_like(dwg_ref)
        dwu_ref[...] = jnp.zeros_like(dwu_ref)
        dwo_ref[...] = jnp.zeros_like(dwo_ref)

        def blk(r):
            hb = h_ref[r, :]
            db = da_ref[r, :]
            gate = jnp.dot(hb, wg_ref[...], preferred_element_type=F32)
            up = jnp.dot(hb, wu_ref[...], preferred_element_type=F32)
            sg = jax.nn.sigmoid(gate)
            sl = gate * sg
            da = _bdot_nt(db, wo_ref[...])
            dup = (da * sl).astype(BF16)
            dgate = (da * up * (sg * (1.0 + gate * (1.0 - sg)))).astype(BF16)
            dwo_ref[...] += _bdot_tn((sl * up).astype(BF16), db)
            dwg_ref[...] += _bdot_tn(hb, dgate)
            dwu_ref[...] += _bdot_tn(hb, dup)
            dh_ref[r, :] += _bdot_nt(dgate, wg_ref[...]) + _bdot_nt(dup, wu_ref[...])
        _row_loop(T, blk)

        @pl.when(j == nj - 1)
        def _():
            dg_ref[...] = jnp.zeros_like(dg_ref)

            def fin(r):
                dx, dg = _norm_bwd_rows(x_ref[r, :], g_ref[...], dh_ref[r, :], dy_ref[r, :])
                dx_ref[r, :] = dx
                dg_ref[...] += dg
            _row_loop(T, fin)

    full = pl.BlockSpec((T, D_MODEL), lambda j: (0, 0))
    vec = pl.BlockSpec((1, D_MODEL), lambda j: (0, 0))
    return pl.pallas_call(
        body, name=name, grid=(nj,),
        in_specs=[full, full, vec,
                  pl.BlockSpec((D_MODEL, FF_CHUNK), lambda j: (0, j)),
                  pl.BlockSpec((D_MODEL, FF_CHUNK), lambda j: (0, j + nj)),
                  pl.BlockSpec((FF_CHUNK, D_MODEL), lambda j: (j, 0))],
        out_specs=(full, vec,
                   pl.BlockSpec((D_MODEL, FF_CHUNK), lambda j: (0, j)),
                   pl.BlockSpec((D_MODEL, FF_CHUNK), lambda j: (0, j)),
                   pl.BlockSpec((FF_CHUNK, D_MODEL), lambda j: (j, 0))),
        out_shape=(jax.ShapeDtypeStruct((T, D_MODEL), F32), jax.ShapeDtypeStruct((1, D_MODEL), F32),
                   jax.ShapeDtypeStruct((D_MODEL, D_FF), F32), jax.ShapeDtypeStruct((D_MODEL, D_FF), F32),
                   jax.ShapeDtypeStruct((D_FF, D_MODEL), F32)),
        scratch_shapes=[pltpu.VMEM((T, D_MODEL), BF16), pltpu.VMEM((T, D_MODEL), BF16),
                        pltpu.VMEM((T, D_MODEL), F32)],
        compiler_params=_params(("arbitrary",)))(x, dy, g, wi, wi, wo)


def proj_fwd(x, g, w, name):
    T = x.shape[0]
    nj = D_IN_PAD // IN_CHUNK

    def body(x_ref, g_ref, w_ref, o_ref, h_ref):
        @pl.when(pl.program_id(0) == 0)
        def _():
            def init(r):
                h_ref[r, :] = _rms(x_ref[r, :], g_ref[...]).astype(BF16)
            _row_loop(T, init)

        def blk(r):
            o_ref[r, :] = jnp.dot(h_ref[r, :], w_ref[...], preferred_element_type=F32)
        _row_loop(T, blk)

    return pl.pallas_call(
        body, name=name, grid=(nj,),
        in_specs=[pl.BlockSpec((T, D_MODEL), lambda j: (0, 0)), pl.BlockSpec((1, D_MODEL), lambda j: (0, 0)),
                  pl.BlockSpec((D_MODEL, IN_CHUNK), lambda j: (0, j))],
        out_specs=pl.BlockSpec((T, IN_CHUNK), lambda j: (0, j)),
        out_shape=jax.ShapeDtypeStruct((T, D_IN_PAD), F32),
        scratch_shapes=[pltpu.VMEM((T, D_MODEL), BF16)],
        compiler_params=_params(("arbitrary",)))(x, g, w)


def proj_bwd(x, dres, g, w, dp, name):
    T = x.shape[0]
    nj = D_IN_PAD // IN_CHUNK

    def body(x_ref, dres_ref, g_ref, w_ref, dp_ref, dx_ref, dg_ref, dw_ref, h_ref, dh_ref):
        j = pl.program_id(0)

        @pl.when(j == 0)
        def _():
            def init(r):
                h_ref[r, :] = _rms(x_ref[r, :], g_ref[...]).astype(BF16)
                dh_ref[r, :] = jnp.zeros((ROWS, D_MODEL), F32)
            _row_loop(T, init)

        dw_ref[...] = jnp.zeros_like(dw_ref)

        def blk(r):
            dpb = dp_ref[r, :].astype(BF16)
            dw_ref[...] += _bdot_tn(h_ref[r, :], dpb)
            dh_ref[r, :] += _bdot_nt(dpb, w_ref[...])
        _row_loop(T, blk)

        @pl.when(j == nj - 1)
        def _():
            dg_ref[...] = jnp.zeros_like(dg_ref)

            def fin(r):
                dx, dg = _norm_bwd_rows(x_ref[r, :], g_ref[...], dh_ref[r, :], dres_ref[r, :])
                dx_ref[r, :] = dx
                dg_ref[...] += dg
            _row_loop(T, fin)

    full = pl.BlockSpec((T, D_MODEL), lambda j: (0, 0))
    vec = pl.BlockSpec((1, D_MODEL), lambda j: (0, 0))
    return pl.pallas_call(
        body, name=name, grid=(nj,),
        in_specs=[full, full, vec, pl.BlockSpec((D_MODEL, IN_CHUNK), lambda j: (0, j)),
                  pl.BlockSpec((T, IN_CHUNK), lambda j: (0, j))],
        out_specs=(full, vec, pl.BlockSpec((D_MODEL, IN_CHUNK), lambda j: (0, j))),
        out_shape=(jax.ShapeDtypeStruct((T, D_MODEL), F32), jax.ShapeDtypeStruct((1, D_MODEL), F32),
                   jax.ShapeDtypeStruct((D_MODEL, D_IN_PAD), F32)),
        scratch_shapes=[pltpu.VMEM((T, D_MODEL), BF16), pltpu.VMEM((T, D_MODEL), F32)],
        compiler_params=_params(("arbitrary",)))(x, dres, g, w, dp)


def out_fwd(mixed, w, x, name):
    T = x.shape[0]

    def body(m_ref, w_ref, x_ref, o_ref):
        o_ref[...] = x_ref[...] + jnp.dot(m_ref[...].astype(BF16), w_ref[...], preferred_element_type=F32)

    blk = pl.BlockSpec((ROWS, D_MODEL), lambda i: (i, 0))
    return pl.pallas_call(
        body, name=name, grid=(T // ROWS,),
        in_specs=[blk, pl.BlockSpec((D_MODEL, D_MODEL), lambda i: (0, 0)), blk],
        out_specs=blk, out_shape=jax.ShapeDtypeStruct((T, D_MODEL), F32),
        compiler_params=_params(("arbitrary",)))(mixed, w, x)


def out_bwd(mixed, w, dy, name):
    T = dy.shape[0]

    def body(m_ref, w_ref, dy_ref, dm_ref, dw_ref):
        @pl.when(pl.program_id(0) == 0)
        def _():
            dw_ref[...] = jnp.zeros_like(dw_ref)
        dyb = dy_ref[...].astype(BF16)
        dm_ref[...] = _bdot_nt(dyb, w_ref[...])
        dw_ref[...] += _bdot_tn(m_ref[...].astype(BF16), dyb)

    blk = pl.BlockSpec((ROWS, D_MODEL), lambda i: (i, 0))
    sq = pl.BlockSpec((D_MODEL, D_MODEL), lambda i: (0, 0))
    return pl.pallas_call(
        body, name=name, grid=(T // ROWS,),
        in_specs=[blk, sq, blk], out_specs=(blk, sq),
        out_shape=(jax.ShapeDtypeStruct((T, D_MODEL), F32), jax.ShapeDtypeStruct((D_MODEL, D_MODEL), F32)),
        compiler_params=_params(("arbitrary",)))(mixed, w, dy)


def loss_head(x, g, target, name):
    T = x.shape[0]

    def body(x_ref, g_ref, t_ref, loss_ref, dx_ref, dg_ref):
        @pl.when(pl.program_id(0) == 0)
        def _():
            loss_ref[...] = jnp.zeros_like(loss_ref)
            dg_ref[...] = jnp.zeros_like(dg_ref)
        xb = x_ref[...]
        rstd = lax.rsqrt(jnp.mean(xb * xb, axis=-1, keepdims=True) + NORM_EPS)
        xh = xb * rstd
        err = xh * g_ref[...] - t_ref[...]
        loss_ref[...] += 0.5 * jnp.sum(jnp.mean(err * err, axis=-1, keepdims=True), axis=0, keepdims=True)
        dy = err * (1.0 / D_MODEL)
        dg_ref[...] += jnp.sum(dy * xh, axis=0, keepdims=True)
        dxh = dy * g_ref[...]
        dx_ref[...] = rstd * (dxh - xh * jnp.mean(dxh * xh, axis=-1, keepdims=True))

    blk = pl.BlockSpec((ROWS, D_MODEL), lambda i: (i, 0))
    vec = pl.BlockSpec((1, D_MODEL), lambda i: (0, 0))
    return pl.pallas_call(
        body, name=name, grid=(T // ROWS,),
        in_specs=[blk, vec, blk], out_specs=(pl.BlockSpec((1, 1), lambda i: (0, 0)), blk, vec),
        out_shape=(jax.ShapeDtypeStruct((1, 1), F32), jax.ShapeDtypeStruct((T, D_MODEL), F32),
                   jax.ShapeDtypeStruct((1, D_MODEL), F32)),
        compiler_params=_params(("arbitrary",)))(x, g, target)


def rowwise_fwd(fn, rows, shared, out_widths, name):
    T = rows[0].shape[0]
    n_in = len(rows) + len(shared)

    def body(*refs):
        res = fn(*[r[...] for r in refs[:n_in]])
        for o, v in zip(refs[n_in:], res):
            o[...] = v

    in_specs = ([pl.BlockSpec((ROWS, a.shape[1]), lambda i: (i, 0)) for a in rows]
                + [pl.BlockSpec(a.shape, lambda i: (0, 0)) for a in shared])
    return pl.pallas_call(
        body, name=name, grid=(T // ROWS,), in_specs=in_specs,
        out_specs=tuple(pl.BlockSpec((ROWS, w), lambda i: (i, 0)) for w in out_widths),
        out_shape=tuple(jax.ShapeDtypeStruct((T, w), F32) for w in out_widths),
        compiler_params=_params(("arbitrary",)))(*rows, *shared)


def rowwise_bwd(fn, rows, shared, cts, name, ct_fn=None):
    T = rows[0].shape[0]
    nr, ns, nc = len(rows), len(shared), len(cts)

    def body(*refs):
        ins = [r[...] for r in refs[:nr + ns]]
        ctv = tuple(r[...] for r in refs[nr + ns:nr + ns + nc])
        outs = refs[nr + ns + nc:]
        _, vjp = jax.vjp(fn, *ins)
        grads = vjp(ct_fn(*ctv) if ct_fn is not None else ctv)
        for k in range(nr):
            outs[k][...] = grads[k]

        @pl.when(pl.program_id(0) == 0)
        def _():
            for k in range(ns):
                outs[nr + k][...] = jnp.zeros_like(outs[nr + k])
        for k in range(ns):
            outs[nr + k][...] += grads[nr + k]

    row_spec = lambda a: pl.BlockSpec((ROWS, a.shape[1]), lambda i: (i, 0))
    sh_spec = lambda a: pl.BlockSpec(a.shape, lambda i: (0, 0))
    return pl.pallas_call(
        body, name=name, grid=(T // ROWS,),
        in_specs=[row_spec(a) for a in rows] + [sh_spec(a) for a in shared] + [row_spec(a) for a in cts],
        out_specs=tuple([row_spec(a) for a in rows] + [sh_spec(a) for a in shared]),
        out_shape=tuple(jax.ShapeDtypeStruct(a.shape, F32) for a in list(rows) + list(shared)),
        compiler_params=_params(("arbitrary",)))(*rows, *shared, *cts)


def shift_rows(x, s):
    return jnp.pad(x, ((s, 0), (0, 0)))[:x.shape[0]]


def unshift_rows(x, s):
    return jnp.pad(x, ((0, s), (0, 0)))[s:]


def _neg_expm1(y):
    series = -(y * (1.0 + y * (0.5 + y * (1.0 / 6.0 + y * (1.0 / 24.0)))))
    return jnp.where(y > -0.05, series, 1.0 - jnp.exp(y))


def lru_pre_fn(x0, x1, x2, x3, first, w0, w1, w2, w3, cb, ga, gab, gx, gxb, lam):
    xc = w3 * x0 + w2 * x1 + w1 * x2 + w0 * x3 + cb
    r = jax.nn.sigmoid(_hdot(xc, ga) + gab)
    i = jax.nn.sigmoid(_hdot(xc, gx) + gxb)
    log_a = -LRU_C * r * jax.nn.softplus(-lam)
    a = jnp.exp(log_a)
    mult = jnp.where(first > 0.5, 1.0, jnp.sqrt(_neg_expm1(2.0 * log_a)))
    return a, mult * i * xc


def lru_post_fn(h, py, og):
    return (_rms(h * jax.nn.gelu(py), og),)


def lru_scan(a, b, reverse, name):
    T, C = a.shape
    nb = T // 8

    def body(a_ref, b_ref, h_ref):
        rows = lax.broadcasted_iota(jnp.int32, (8, C), 0)

        def blk(i, carry):
            j = nb - 1 - i if reverse else i
            r = pl.ds(pl.multiple_of(j * 8, 8), 8)
            A = a_ref[r, :]
            B = b_ref[r, :]
            for s in (1, 2, 4):
                if reverse:
                    keep = rows < 8 - s
                    sh = 8 - s
                else:
                    keep = rows >= s
                    sh = s
                Bs = jnp.where(keep, pltpu.roll(B, sh, 0), 0.0)
                As = jnp.where(keep, pltpu.roll(A, sh, 0), 1.0)
                B = B + A * Bs
                A = A * As
            hb = B + A * carry
            h_ref[r, :] = hb
            edge = 0 if reverse else 7
            return jnp.sum(jnp.where(rows == edge, hb, 0.0), axis=0, keepdims=True)

        lax.fori_loop(0, nb, blk, jnp.zeros((1, C), F32))

    full = pl.BlockSpec((T, C), lambda: (0, 0))
    return pl.pallas_call(body, name=name, in_specs=[full, full], out_specs=full,
                          out_shape=jax.ShapeDtypeStruct((T, C), F32), compiler_params=_params())(a, b)


def make_rwkv_pre_fn(has_vres):
    def fn(p, pp, *rest):
        if has_vres:
            vf, mu, w_up, w_b, a_up, a_b, g_up, kk_w, ka_w, vw1, vw2, vb = rest
        else:
            mu, w_up, w_b, a_up, a_b, g_up, kk_w, ka_w = rest
        xm = p + (pp - p) * mu
        r, k, v = xm[:, 0:384], xm[:, 384:768], xm[:, 768:1152]
        xw, xa, xg = xm[:, 1152:1216], xm[:, 1216:1280], xm[:, 1280:1408]
        w_log = -jax.nn.softplus(-(w_b + _hdot(jnp.tanh(xw), w_up))) - 0.5
        lw = -jnp.exp(w_log)
        a = jax.nn.sigmoid(a_b + _hdot(xa, a_up))
        g = _hdot(jax.nn.sigmoid(xg), g_up)
        if has_vres:
            v = v + (vf - v) * jax.nn.sigmoid(vb + _hdot(_hdot(v, vw1), vw2))
        kkx = k * kk_w
        kk = kkx * lax.rsqrt(_segsum(kkx * kkx) + 1e-6)
        k2 = k * (1.0 + (a - 1.0) * ka_w)
        return r, lw, k2, v, kk, a, g
    return fn


def rwkv_post_fn(y, r, k2, v, g, ln_g, ln_b, r_k):
    mean = _segsum(y) * (1.0 / HEAD_DIM)
    yc = y - mean
    var = _segsum(yc * yc) * (1.0 / HEAD_DIM)
    yn = yc * lax.rsqrt(var + GN_EPS) * ln_g + ln_b
    bonus = _segsum(r * k2 * r_k) * v
    return ((yn + bonus) * g,)


def gdn_pre_fn(x0, x1, x2, x3, ab, w0, w1, w2, w3, alog, dtb):
    qkv = jax.nn.silu(w3 * x0 + w2 * x1 + w1 * x2 + w0 * x3)
    q, k, v = qkv[:, 0:384], qkv[:, 384:768], qkv[:, 768:1152]
    q = q * lax.rsqrt(_segsum(q * q) + 1e-6) * (HEAD_DIM ** -0.5)
    k = k * lax.rsqrt(_segsum(k * k) + 1e-6)
    g = -jnp.exp(alog) * jax.nn.softplus(ab + dtb)
    beta = jax.nn.sigmoid(ab)
    ri, ci = _iota2(128, MIX_W)
    ge = _hdot(g, (ri == ci // HEAD_DIM).astype(F32))
    be = _hdot(beta, (ri == ci // HEAD_DIM + HEADS).astype(F32))
    return q, k, v, ge, be


def gdn_post_fn(o, z, ng):
    ms = _segsum(o * o) * (1.0 / HEAD_DIM)
    return (o * lax.rsqrt(ms + NORM_EPS) * ng * jax.nn.silu(z),)


def _neumann_inv(m):
    n = m.shape[0]
    ri, ci = _iota2(n, n)
    eye = (ri == ci).astype(F32)
    md = jnp.where(ri // 16 == ci // 16, m, 0.0)
    mo = m - md
    t0 = eye + md
    p2 = _hdot(md, md)
    t0 = t0 + _hdot(t0, p2)
    p4 = _hdot(p2, p2)
    t0 = t0 + _hdot(t0, p4)
    p8 = _hdot(p4, p4)
    t0 = t0 + _hdot(t0, p8)
    nn = _hdot(t0, mo)
    n2 = _hdot(nn, nn)
    t1 = eye + nn + n2 + _hdot(nn, n2)
    return _hdot(t1, t0)


@jax.custom_vjp
def _inv_saved(m, t_saved):
    return t_saved


def _inv_saved_fwd(m, t_saved):
    return t_saved, t_saved


def _inv_saved_bwd(t_saved, dt):
    tt = t_saved.T
    return _hdot(_hdot(tt, dt), tt), jnp.zeros_like(t_saved)


_inv_saved.defvjp(_inv_saved_fwd, _inv_saved_bwd)


def rwkv_head(s0, r, lw, k2, v, kk, a, inv):
    n = r.shape[0]
    ri, ci = _iota2(n, n)
    low, strict = ri >= ci, ri > ci
    cs = _hdot(low.astype(F32), lw)
    cl = jnp.sum(lw, axis=0, keepdims=True)
    p_in, p_prev, p_inv = jnp.exp(cs), jnp.exp(cs - lw), jnp.exp(-cs)
    p_rest, p_all = jnp.exp(cl - cs), jnp.exp(cl)
    ad, bd = -kk, kk * a
    at, rt = ad * p_prev, r * p_in
    bh, kh = bd * p_inv, k2 * p_inv
    m_ab = jnp.where(strict, _cdot_nt(at, bh), 0.0)
    m_ak = jnp.where(strict, _cdot_nt(at, kh), 0.0)
    m_rb = jnp.where(low, _cdot_nt(rt, bh), 0.0)
    m_rk = jnp.where(low, _cdot_nt(rt, kh), 0.0)
    sa = _cdot(inv(m_ab), _cdot_nt(at, s0) + _cdot(m_ak, v))
    y = _cdot_nt(rt, s0) + _cdot(m_rb, sa) + _cdot(m_rk, v)
    s1 = s0 * p_all + _cdot_tn(sa, bd * p_rest) + _cdot_tn(v, k2 * p_rest)
    return y, s1


def gdn_head(s0, q, k, v, ge, be, inv):
    n = q.shape[0]
    ri, ci = _iota2(n, n)
    low, strict = ri >= ci, ri > ci
    gc = _hdot(low.astype(F32), ge)
    gl = jnp.sum(ge, axis=0, keepdims=True)
    decay = jnp.where(low, jnp.exp(jnp.where(low, gc - gc.T, 0.0)), 0.0)
    kb = k * be
    e = jnp.exp(gc)
    m = -jnp.where(strict, _cdot_nt(kb, k) * decay, 0.0)
    mr = jnp.where(low, _cdot_nt(q, k) * decay, 0.0)
    u = _cdot(inv(m), v * be - _cdot_nt(kb * e, s0))
    y = _cdot_nt(q * e, s0) + _cdot(mr, u)
    s1 = s0 * jnp.exp(gl) + _cdot_tn(u, k * jnp.exp(gl - gc))
    return y, s1


def core_fwd(head_fn, ins, name):
    T = ins[0].shape[0]
    nc = T // CHUNK
    n = len(ins)

    def body(*refs):
        y_ref, s0_ref, t_ref, s_ref = refs[n:n + 4]

        @pl.when(pl.program_id(0) == 0)
        def _():
            s_ref[...] = jnp.zeros_like(s_ref)

        for h in range(HEADS):
            lanes = slice(h * HEAD_DIM, (h + 1) * HEAD_DIM)
            s0 = s_ref[lanes, :]
            kept = []

            def inv(m):
                kept.append(_neumann_inv(m))
                return kept[0]

            y, s1 = head_fn(s0, *[r[:, lanes] for r in refs[:n]], inv)
            y_ref[:, lanes] = y
            s0_ref[0, lanes, :] = s0
            t_ref[0, lanes, :] = kept[0]
            s_ref[lanes, :] = s1

    row = pl.BlockSpec((CHUNK, MIX_W), lambda c: (c, 0))
    st = pl.BlockSpec((1, MIX_W, HEAD_DIM), lambda c: (c, 0, 0))
    return pl.pallas_call(
        body, name=name, grid=(nc,), in_specs=[row] * n, out_specs=(row, st, st),
        out_shape=(jax.ShapeDtypeStruct((T, MIX_W), F32), jax.ShapeDtypeStruct((nc, MIX_W, HEAD_DIM), F32),
                   jax.ShapeDtypeStruct((nc, MIX_W, HEAD_DIM), F32)),
        scratch_shapes=[pltpu.VMEM((MIX_W, HEAD_DIM), F32)],
        compiler_params=_params(("arbitrary",)))(*ins)


def core_bwd(head_fn, ins, s0_all, t_all, dy, name):
    T = ins[0].shape[0]
    nc = T // CHUNK
    n = len(ins)

    def body(*refs):
        s0_ref, t_ref, dy_ref = refs[n:n + 3]
        outs = refs[n + 3:n + 3 + n]
        ds_ref = refs[n + 3 + n]

        @pl.when(pl.program_id(0) == 0)
        def _():
            ds_ref[...] = jnp.zeros_like(ds_ref)

        for h in range(HEADS):
            lanes = slice(h * HEAD_DIM, (h + 1) * HEAD_DIM)
            t_saved = t_ref[0, lanes, :]
            f = lambda s0, *xs: head_fn(s0, *xs, lambda m: _inv_saved(m, t_saved))
            _, vjp = jax.vjp(f, s0_ref[0, lanes, :], *[r[:, lanes] for r in refs[:n]])
            grads = vjp((dy_ref[:, lanes], ds_ref[lanes, :]))
            ds_ref[lanes, :] = grads[0]
            for k in range(n):
                outs[k][:, lanes] = grads[1 + k]

    row = pl.BlockSpec((CHUNK, MIX_W), lambda c: (nc - 1 - c, 0))
    st = pl.BlockSpec((1, MIX_W, HEAD_DIM), lambda c: (nc - 1 - c, 0, 0))
    return pl.pallas_call(
        body, name=name, grid=(nc,), in_specs=[row] * n + [st, st, row], out_specs=tuple([row] * n),
        out_shape=tuple(jax.ShapeDtypeStruct((T, MIX_W), F32) for _ in range(n)),
        scratch_shapes=[pltpu.VMEM((MIX_W, HEAD_DIM), F32)],
        compiler_params=_params(("arbitrary",)))(*ins, s0_all, t_all, dy)


def _block_diag(w):
    out = jnp.zeros((LRU_W, LRU_W), w.dtype)
    for n in range(LRU_BLOCKS):
        out = lax.dynamic_update_slice(out, w[n], (n * 64, n * 64))
    return out


def _block_diag_grad(g):
    return jnp.stack([g[n * 64:(n + 1) * 64, n * 64:(n + 1) * 64] for n in range(LRU_BLOCKS)])


def _row(v):
    return v.reshape(1, -1)


def _pad128(v):
    return jnp.pad(v.reshape(1, -1), ((0, 0), (0, 128 - v.size)))


def _layer_shared(w, l):
    cw = w['lru_conv_w'][l]
    lru_pre = [_row(cw[0]), _row(cw[1]), _row(cw[2]), _row(cw[3]), _row(w['lru_conv_b'][l]),
               _block_diag(w['lru_gate_a_w'][l]), _row(w['lru_gate_a_b'][l]),
               _block_diag(w['lru_gate_x_w'][l]), _row(w['lru_gate_x_b'][l]), _row(w['lru_lambda'][l])]
    rw_pre = [_row(w['rwkv_mu'][l]), w['rwkv_w_up'][l], _row(w['rwkv_w_bias'][l]), w['rwkv_a_up'][l],
              _row(w['rwkv_a_bias'][l]), w['rwkv_g_up'][l], _row(w['rwkv_k_k'][l]), _row(w['rwkv_k_a'][l])]
    if l > 0:
        rw_pre += [w['rwkv_vres_w1'][l - 1], w['rwkv_vres_w2'][l - 1], _row(w['rwkv_vres_b'][l - 1])]
    rw_post = [_row(w['rwkv_ln_g'][l]), _row(w['rwkv_ln_b'][l]), _row(w['rwkv_r_k'][l])]
    gw = w['gdn_conv_w'][l]
    gdn_pre = [_row(gw[0]), _row(gw[1]), _row(gw[2]), _row(gw[3]), _pad128(w['gdn_a_log'][l]),
               _pad128(w['gdn_dt_bias'][l])]
    gdn_post = [_row(jnp.tile(w['gdn_norm'][l], HEADS))]
    return dict(lru_pre=lru_pre, lru_post=[_row(w['lru_out_norm'][l])], rw_pre=rw_pre, rw_post=rw_post,
                gdn_pre=gdn_pre, gdn_post=gdn_post)


def _mixer_fwd(p, sh, l, v_first):
    T = p.shape[0]
    lx, ly = p[:, 0:256], p[:, 256:512]
    prw, qkv, z, ab = p[:, 512:1920], p[:, 1920:3072], p[:, 3072:3456], p[:, 3456:3584]
    first = jnp.zeros((T, LRU_W), F32).at[0].set(1.0)
    lru_rows = [lx, shift_rows(lx, 1), shift_rows(lx, 2), shift_rows(lx, 3), first]
    a, b = rowwise_fwd(lru_pre_fn, lru_rows, sh['lru_pre'], (LRU_W, LRU_W), f"lru_pre_fwd{l}")
    hseq = lru_scan(a, b, False, f"lru_scan_fwd{l}")
    (y_lru,) = rowwise_fwd(lru_post_fn, [hseq, ly], sh['lru_post'], (LRU_W,), f"lru_post_fwd{l}")

    rw_rows = [prw, shift_rows(prw, 1)] + ([v_first] if l > 0 else [])
    rw = rowwise_fwd(make_rwkv_pre_fn(l > 0), rw_rows, sh['rw_pre'], (MIX_W,) * 7, f"rwkv_pre_fwd{l}")
    r, lw, k2, v, kk, ar, g = rw
    y_raw, rs0, rt = core_fwd(rwkv_head, [r, lw, k2, v, kk, ar], f"rwkv_core_fwd{l}")
    (y_rw,) = rowwise_fwd(rwkv_post_fn, [y_raw, r, k2, v, g], sh['rw_post'], (MIX_W,), f"rwkv_post_fwd{l}")

    gdn_rows = [qkv, shift_rows(qkv, 1), shift_rows(qkv, 2), shift_rows(qkv, 3), ab]
    gd = rowwise_fwd(gdn_pre_fn, gdn_rows, sh['gdn_pre'], (MIX_W,) * 5, f"gdn_pre_fwd{l}")
    o_raw, gs0, gt = core_fwd(gdn_head, list(gd), f"gdn_core_fwd{l}")
    (y_gdn,) = rowwise_fwd(gdn_post_fn, [o_raw, z], sh['gdn_post'], (MIX_W,), f"gdn_post_fwd{l}")

    mixed = jnp.concatenate([y_lru, y_rw, y_gdn], axis=1)
    saved = dict(lru_rows=lru_rows, a=a, hseq=hseq, ly=ly, rw_rows=rw_rows, rw=rw, y_raw=y_raw, rs0=rs0, rt=rt,
                 gdn_rows=gdn_rows, gd=gd, o_raw=o_raw, gs0=gs0, gt=gt, z=z)
    v_layer0 = v if l == 0 else None
    return mixed, saved, v_layer0


def _mixer_bwd(dmixed, sv, sh, l, dv_first):
    d_lru, d_rw, d_gdn = dmixed[:, 0:256], dmixed[:, 256:640], dmixed[:, 640:1024]
    gw = {}

    dh, dly, d_og = rowwise_bwd(lru_post_fn, [sv['hseq'], sv['ly']], sh['lru_post'], [d_lru], f"lru_post_bwd{l}")
    gscan = lru_scan(unshift_rows(sv['a'], 1), dh, True, f"lru_scan_bwd{l}")
    res = rowwise_bwd(lru_pre_fn, sv['lru_rows'], sh['lru_pre'], [gscan, shift_rows(sv['hseq'], 1)],
                      f"lru_pre_bwd{l}", ct_fn=lambda gs, hp: (gs * hp, gs))
    dlx = res[0] + unshift_rows(res[1], 1) + unshift_rows(res[2], 2) + unshift_rows(res[3], 3)
    dw0, dw1, dw2, dw3, dcb, dga, dgab, dgx, dgxb, dlam = res[5:]
    gw['lru_conv_w'] = jnp.concatenate([dw0, dw1, dw2, dw3], axis=0)
    gw['lru_conv_b'] = dcb[0]
    gw['lru_gate_a_w'] = _block_diag_grad(dga)
    gw['lru_gate_a_b'] = dgab.reshape(LRU_BLOCKS, 64)
    gw['lru_gate_x_w'] = _block_diag_grad(dgx)
    gw['lru_gate_x_b'] = dgxb.reshape(LRU_BLOCKS, 64)
    gw['lru_lambda'] = dlam[0]
    gw['lru_out_norm'] = d_og[0]

    r, lw, k2, v, kk, ar, g = sv['rw']
    res = rowwise_bwd(rwkv_post_fn, [sv['y_raw'], r, k2, v, g], sh['rw_post'], [d_rw], f"rwkv_post_bwd{l}")
    dy_raw, dr_p, dk2_p, dv_p, dg = res[:5]
    gw['rwkv_ln_g'], gw['rwkv_ln_b'], gw['rwkv_r_k'] = res[5][0], res[6][0], res[7].reshape(HEADS, HEAD_DIM)
    dr_c, dlw, dk2_c, dv_c, dkk, dar = core_bwd(rwkv_head, [r, lw, k2, v, kk, ar], sv['rs0'], sv['rt'], dy_raw,
                                                 f"rwkv_core_bwd{l}")
    cts = [dr_p, dr_c, dlw, dk2_p, dk2_c, dv_p, dv_c, dkk, dar, dg]
    if l == 0:
        cts.append(dv_first)
        ct_fn = lambda a1, a2, b, c1, c2, d1, d2, e, f, gg, vf: (a1 + a2, b, c1 + c2, d1 + d2 + vf, e, f, gg)
    else:
        ct_fn = lambda a1, a2, b, c1, c2, d1, d2, e, f, gg: (a1 + a2, b, c1 + c2, d1 + d2, e, f, gg)
    res = rowwise_bwd(make_rwkv_pre_fn(l > 0), sv['rw_rows'], sh['rw_pre'], cts, f"rwkv_pre_bwd{l}", ct_fn=ct_fn)
    dprw = res[0] + unshift_rows(res[1], 1)
    nrow = len(sv['rw_rows'])
    dv_first_out = res[2] if l > 0 else None
    sg = res[nrow:]
    gw['rwkv_mu'], gw['rwkv_w_up'], gw['rwkv_w_bias'], gw['rwkv_a_up'] = sg[0][0], sg[1], sg[2][0], sg[3]
    gw['rwkv_a_bias'], gw['rwkv_g_up'], gw['rwkv_k_k'], gw['rwkv_k_a'] = sg[4][0], sg[5], sg[6][0], sg[7][0]
    if l > 0:
        gw['rwkv_vres_w1'], gw['rwkv_vres_w2'], gw['rwkv_vres_b'] = sg[8], sg[9], sg[10][0]

    do_raw, dz, d_ng = rowwise_bwd(gdn_post_fn, [sv['o_raw'], sv['z']], sh['gdn_post'], [d_gdn], f"gdn_post_bwd{l}")
    gw['gdn_norm'] = jnp.sum(d_ng.reshape(HEADS, HEAD_DIM), axis=0)
    dgd = core_bwd(gdn_head, list(sv['gd']), sv['gs0'], sv['gt'], do_raw, f"gdn_core_bwd{l}")
    res = rowwise_bwd(gdn_pre_fn, sv['gdn_rows'], sh['gdn_pre'], list(dgd), f"gdn_pre_bwd{l}")
    dqkv = res[0] + unshift_rows(res[1], 1) + unshift_rows(res[2], 2) + unshift_rows(res[3], 3)
    dab = res[4]
    gw['gdn_conv_w'] = jnp.concatenate(res[5:9], axis=0)
    gw['gdn_a_log'], gw['gdn_dt_bias'] = res[9][0, :HEADS], res[10][0, :HEADS]

    dp = jnp.concatenate([dlx, dly, dprw, dqkv, dz, dab], axis=1)
    return dp, gw, dv_first_out


def local_step(x, target, w, wb):
    saved = []
    v_first = None
    for l in range(N_LAYERS):
        sh = _layer_shared(w, l)
        x1 = ffn_fwd(x, _row(w['ffn1_norm'][l]), wb['ffn1_wi'][l], wb['ffn1_wo'][l], f"ffn1_fwd{l}")
        p = proj_fwd(x1, _row(w['mix_norm'][l]), wb['w_in'][l], f"proj_fwd{l}")
        mixed, sv, v0 = _mixer_fwd(p, sh, l, v_first)
        if l == 0:
            v_first = v0
        x2 = out_fwd(mixed, wb['w_out'][l], x1, f"out_fwd{l}")
        x3 = ffn_fwd(x2, _row(w['ffn2_norm'][l]), wb['ffn2_wi'][l], wb['ffn2_wo'][l], f"ffn2_fwd{l}")
        saved.append(dict(x0=x, x1=x1, x2=x2, mixed=mixed, sv=sv, sh=sh))
        x = x3

    loss, dx, dgf = loss_head(x, _row(w['final_norm']), target, "loss_head")
    per_layer = [None] * N_LAYERS
    dv_first = jnp.zeros((x.shape[0], MIX_W), F32)
    for l in reversed(range(N_LAYERS)):
        s = saved[l]
        gw = {}
        dx, dg2, dwg, dwu, dwo = ffn_bwd(s['x2'], dx, _row(w['ffn2_norm'][l]), wb['ffn2_wi'][l], wb['ffn2_wo'][l],
                                         f"ffn2_bwd{l}")
        gw['ffn2_norm'], gw['ffn2_wi'], gw['ffn2_wo'] = dg2[0], jnp.concatenate([dwg, dwu], axis=1), dwo
        dmixed, gw['w_out'] = out_bwd(s['mixed'], wb['w_out'][l], dx, f"out_bwd{l}")
        dp, gmix, dvf = _mixer_bwd(dmixed, s['sv'], s['sh'], l, dv_first)
        if l > 0:
            dv_first = dvf
        gw.update(gmix)
        dx, dgm, dwin = proj_bwd(s['x1'], dx, _row(w['mix_norm'][l]), wb['w_in'][l], dp, f"proj_bwd{l}")
        gw['mix_norm'], gw['w_in'] = dgm[0], dwin[:, :D_IN]
        dx, dg1, dwg, dwu, dwo = ffn_bwd(s['x0'], dx, _row(w['ffn1_norm'][l]), wb['ffn1_wi'][l], wb['ffn1_wo'][l],
                                         f"ffn1_bwd{l}")
        gw['ffn1_norm'], gw['ffn1_wi'], gw['ffn1_wo'] = dg1[0], jnp.concatenate([dwg, dwu], axis=1), dwo
        per_layer[l] = gw

    grads = {'final_norm': dgf[0]}
    for name in WEIGHTS:
        if name == 'final_norm':
            continue
        if name.startswith('rwkv_vres'):
            grads[name] = per_layer[1][name][None]
        else:
            grads[name] = jnp.stack([per_layer[l][name] for l in range(N_LAYERS)])
    return loss[0, 0], dx, grads


ANY = pl.BlockSpec(memory_space=pl.ANY)


def _coords():
    return lax.axis_index("x"), lax.axis_index("y"), lax.axis_index("c")


def _other_chips(x, y):
    return [((x + 1) % 2, y), (x, (y + 1) % 2), ((x + 1) % 2, (y + 1) % 2)]


def allreduce_small(pack, name):
    R = pack.shape[0]

    def body(x_ref, o_ref, buf, send_sems, recv_sems):
        x, y, c = _coords()
        me = 4 * x + 2 * y + c
        buf[me] = x_ref[...]
        copies = []
        for k in range(1, 8):
            peer = ((x + (k >> 2)) % 2, (y + ((k >> 1) & 1)) % 2, (c + (k & 1)) % 2)
            cp = pltpu.make_async_remote_copy(src_ref=x_ref, dst_ref=buf.at[me], send_sem=send_sems.at[k - 1],
                                              recv_sem=recv_sems.at[k - 1], device_id=peer, device_id_type=MESH)
            cp.start()
            copies.append(cp)
        for cp in copies:
            cp.wait()
        acc = buf[0]
        for d in range(1, 8):
            acc = acc + buf[d]
        o_ref[...] = acc

    vm = pl.BlockSpec(memory_space=pltpu.VMEM)
    return pl.pallas_call(
        body, name=name, in_specs=[vm], out_specs=vm, out_shape=jax.ShapeDtypeStruct((R, 128), F32),
        scratch_shapes=[pltpu.VMEM((8, R, 128), F32), pltpu.SemaphoreType.DMA((7,)), pltpu.SemaphoreType.DMA((7,))],
        compiler_params=_params())(pack)


def allgather_chips(shard, name):
    R, W = shard.shape
    rh = R // 2

    def body(x_ref, o_ref, send_sems, recv_sems, local_sem):
        x, y, c = _coords()
        s_me = 2 * x + y
        sib = (x, y, 1 - c)
        mine_rows = pl.ds(pl.multiple_of(c * rh, 16), rh)
        sib_rows = pl.ds(pl.multiple_of((1 - c) * rh, 16), rh)
        chips = _other_chips(x, y)

        def copy(k, src, dst, to):
            return pltpu.make_async_remote_copy(src_ref=src, dst_ref=dst, send_sem=send_sems.at[k],
                                                recv_sem=recv_sems.at[k], device_id=to, device_id_type=MESH)

        own = pltpu.make_async_copy(x_ref, o_ref.at[s_me], local_sem)
        own.start()
        sent = [copy(j, x_ref.at[mine_rows], o_ref.at[s_me, mine_rows], (px, py, c))
                for j, (px, py) in enumerate(chips)]
        for cp in sent:
            cp.start()
        passed = []
        for j, (px, py) in enumerate(chips):
            part = o_ref.at[2 * px + py, mine_rows]
            copy(j, part, part, (px, py, c)).wait_recv()
            fw = copy(3 + j, part, part, sib)
            fw.start()
            passed.append(fw)
        for j, (px, py) in enumerate(chips):
            part = o_ref.at[2 * px + py, sib_rows]
            copy(3 + j, part, part, sib).wait_recv()
        for cp in sent + passed:
            cp.wait_send()
        own.wait()

    return pl.pallas_call(
        body, name=name, in_specs=[ANY], out_specs=ANY, out_shape=jax.ShapeDtypeStruct((N_CHIPS, R, W), shard.dtype),
        scratch_shapes=[pltpu.SemaphoreType.DMA((6,)), pltpu.SemaphoreType.DMA((6,)), pltpu.SemaphoreType.DMA],
        compiler_params=_params())(shard)


def sibling_swap(src, halves, name):
    if halves:
        n, R, W = src.shape
        rh = R // 2
        out_shape = (n, rh, W)
    else:
        out_shape = src.shape

    def body(x_ref, o_ref, send_sem, recv_sem):
        x, y, c = _coords()
        part = x_ref.at[:, pl.ds(pl.multiple_of((1 - c) * rh, 16), rh)] if halves else x_ref
        cp = pltpu.make_async_remote_copy(src_ref=part, dst_ref=o_ref, send_sem=send_sem, recv_sem=recv_sem,
                                          device_id=(x, y, 1 - c), device_id_type=MESH)
        cp.start()
        cp.wait()

    return pl.pallas_call(
        body, name=name, in_specs=[ANY], out_specs=ANY, out_shape=jax.ShapeDtypeStruct(out_shape, src.dtype),
        scratch_shapes=[pltpu.SemaphoreType.DMA, pltpu.SemaphoreType.DMA], compiler_params=_params())(src)


def scatter_chips(parts, name):
    n, R, W = parts.shape

    def body(x_ref, o_ref, send_sems, recv_sems):
        x, y, c = _coords()
        copies = []
        for j, (px, py) in enumerate(_other_chips(x, y)):
            cp = pltpu.make_async_remote_copy(src_ref=x_ref.at[2 * px + py], dst_ref=o_ref.at[j],
                                              send_sem=send_sems.at[j], recv_sem=recv_sems.at[j],
                                              device_id=(px, py, c), device_id_type=MESH)
            cp.start()
            copies.append(cp)
        for cp in copies:
            cp.wait()

    return pl.pallas_call(
        body, name=name, in_specs=[ANY], out_specs=ANY, out_shape=jax.ShapeDtypeStruct((3, R, W), parts.dtype),
        scratch_shapes=[pltpu.SemaphoreType.DMA((3,)), pltpu.SemaphoreType.DMA((3,))],
        compiler_params=_params())(parts)


def _row_block(rows):
    return max(b for b in range(16, 257, 16) if rows % b == 0)


def chip_sum(gpack, recv, core, name):
    n, R, W = gpack.shape
    rh = R // 2
    rb = _row_block(rh)
    nb = rh // rb

    def body(c_ref, g_ref, r_ref, o_ref, ob_ref):
        s = g_ref[...] + r_ref[...]
        o_ref[...] = s
        ob_ref[...] = s.astype(BF16)

    blk = pl.BlockSpec((1, rb, W), lambda i, j, c_ref: (i, j, 0))
    spec = pltpu.PrefetchScalarGridSpec(
        num_scalar_prefetch=1, grid=(n, nb),
        in_specs=[pl.BlockSpec((1, rb, W), lambda i, j, c_ref: (i, c_ref[0] * nb + j, 0)), blk],
        out_specs=(blk, blk))
    return pl.pallas_call(
        body, name=name, grid_spec=spec,
        out_shape=(jax.ShapeDtypeStruct((n, rh, W), F32), jax.ShapeDtypeStruct((n, rh, W), BF16)),
        compiler_params=_params(("arbitrary", "arbitrary")))(core, gpack, recv)


def shard_sum(own, recv, name):
    R, W = own.shape
    rb = _row_block(R)

    def body(a_ref, r_ref, o_ref):
        acc = a_ref[...]
        for j in range(3):
            acc = acc + r_ref[j].astype(F32)
        o_ref[...] = acc

    return pl.pallas_call(
        body, name=name, grid=(R // rb,),
        in_specs=[pl.BlockSpec((rb, W), lambda i: (i, 0)), pl.BlockSpec((3, rb, W), lambda i: (0, i, 0))],
        out_specs=pl.BlockSpec((rb, W), lambda i: (i, 0)), out_shape=jax.ShapeDtypeStruct((R, W), F32),
        compiler_params=_params(("arbitrary",)))(own, recv)


def adamw(w, m, v, g, name):
    R, C = w.shape
    rb = 128 if R % 128 == 0 else R
    bc1 = 1.0 - ADAM_B1 ** ADAM_STEP
    bc2 = 1.0 - ADAM_B2 ** ADAM_STEP

    def body(w_ref, m_ref, v_ref, g_ref, d_ref, nm_ref, nv_ref):
        gv = g_ref[...]
        nm = ADAM_B1 * m_ref[...] + (1.0 - ADAM_B1) * gv
        nv = ADAM_B2 * v_ref[...] + (1.0 - ADAM_B2) * (gv * gv)
        d_ref[...] = -ADAM_LR * ((nm / bc1) / (jnp.sqrt(nv / bc2) + ADAM_EPS) + ADAM_WD * w_ref[...])
        nm_ref[...] = nm
        nv_ref[...] = nv

    blk = pl.BlockSpec((rb, C), lambda i: (i, 0))
    sh = jax.ShapeDtypeStruct((R, C), F32)
    return pl.pallas_call(body, name=name, grid=(R // rb,), in_specs=[blk] * 4, out_specs=(blk,) * 3,
                          out_shape=(sh, sh, sh), compiler_params=_params(("arbitrary",)))(w, m, v, g)


SMALL = [n for n in WEIGHTS if n not in BIG]


def _pack_rows(flat, width, row_multiple):
    n = flat.shape[-1]
    per = width * row_multiple
    total = -(-n // per) * per
    flat = jnp.pad(flat, [(0, 0)] * (flat.ndim - 1) + [(0, total - n)])
    return flat.reshape(flat.shape[:-1] + (total // width, width))


def _to_shards(full, axis):
    size = full.shape[axis] // N_CHIPS
    shape = full.shape[:axis] + (N_CHIPS, size) + full.shape[axis + 1:]
    return jnp.moveaxis(full.reshape(shape), axis, 0).reshape(N_CHIPS, -1)


def _from_shards(flat, local_shape, axis):
    stacked = jnp.moveaxis(flat.reshape((N_CHIPS,) + tuple(local_shape)), 0, axis)
    shape = tuple(local_shape[:axis]) + (N_CHIPS * local_shape[axis],) + tuple(local_shape[axis + 1:])
    return stacked.reshape(shape)


def _local_shard(full, axis, chip):
    size = full.shape[axis] // N_CHIPS
    return lax.dynamic_slice_in_dim(full, chip * size, size, axis)


def kernel(x, ffn1_norm, ffn1_wi, ffn1_wo, mix_norm, w_in, w_out, lru_conv_w, lru_conv_b, lru_gate_a_w, lru_gate_a_b, lru_gate_x_w, lru_gate_x_b, lru_lambda, lru_out_norm, rwkv_mu, rwkv_w_up, rwkv_w_bias, rwkv_a_up, rwkv_a_bias, rwkv_g_up, rwkv_k_k, rwkv_k_a, rwkv_r_k, rwkv_ln_g, rwkv_ln_b, rwkv_vres_w1, rwkv_vres_w2, rwkv_vres_b, gdn_conv_w, gdn_a_log, gdn_dt_bias, gdn_norm, ffn2_norm, ffn2_wi, ffn2_wo, final_norm, loss_target, m_ffn1_norm, m_ffn1_wi, m_ffn1_wo, m_mix_norm, m_w_in, m_w_out, m_lru_conv_w, m_lru_conv_b, m_lru_gate_a_w, m_lru_gate_a_b, m_lru_gate_x_w, m_lru_gate_x_b, m_lru_lambda, m_lru_out_norm, m_rwkv_mu, m_rwkv_w_up, m_rwkv_w_bias, m_rwkv_a_up, m_rwkv_a_bias, m_rwkv_g_up, m_rwkv_k_k, m_rwkv_k_a, m_rwkv_r_k, m_rwkv_ln_g, m_rwkv_ln_b, m_rwkv_vres_w1, m_rwkv_vres_w2, m_rwkv_vres_b, m_gdn_conv_w, m_gdn_a_log, m_gdn_dt_bias, m_gdn_norm, m_ffn2_norm, m_ffn2_wi, m_ffn2_wo, m_final_norm, v_ffn1_norm, v_ffn1_wi, v_ffn1_wo, v_mix_norm, v_w_in, v_w_out, v_lru_conv_w, v_lru_conv_b, v_lru_gate_a_w, v_lru_gate_a_b, v_lru_gate_x_w, v_lru_gate_x_b, v_lru_lambda, v_lru_out_norm, v_rwkv_mu, v_rwkv_w_up, v_rwkv_w_bias, v_rwkv_a_up, v_rwkv_a_bias, v_rwkv_g_up, v_rwkv_k_k, v_rwkv_k_a, v_rwkv_r_k, v_rwkv_ln_g, v_rwkv_ln_b, v_rwkv_vres_w1, v_rwkv_vres_w2, v_rwkv_vres_b, v_gdn_conv_w, v_gdn_a_log, v_gdn_dt_bias, v_gdn_norm, v_ffn2_norm, v_ffn2_wi, v_ffn2_wo, v_final_norm):
    args = locals()
    w_loc = {n: args[n] for n in WEIGHTS}
    m_loc = {n: args['m_' + n] for n in WEIGHTS}
    v_loc = {n: args['v_' + n] for n in WEIGHTS}
    chip = 2 * lax.axis_index("x") + lax.axis_index("y")
    core = lax.axis_index("c")

    big_names = list(BIG)
    flat = jnp.concatenate([w_loc[n].reshape(-1) for n in big_names]).astype(BF16)
    gathered = allgather_chips(_pack_rows(flat, PACK_W, 32), "allgather_big")
    gathered = gathered.reshape(N_CHIPS, -1)
    wb, off = {}, 0
    for n in big_names:
        size = w_loc[n].size
        wb[n] = _from_shards(gathered[:, off:off + size], w_loc[n].shape, BIG[n])
        off += size
    wb['w_in'] = jnp.pad(wb['w_in'], ((0, 0), (0, 0), (0, D_IN_PAD - D_IN)))

    sm_names = list(SMALL_SHARDED)
    placed = []
    for n in sm_names:
        ax = SMALL_SHARDED[n]
        full_shape = w_loc[n].shape[:ax] + (N_CHIPS * w_loc[n].shape[ax],) + w_loc[n].shape[ax + 1:]
        src = w_loc[n] * (core == 0).astype(F32)
        placed.append(lax.dynamic_update_slice_in_dim(jnp.zeros(full_shape, F32), src, chip * w_loc[n].shape[ax], ax))
    summed = allreduce_small(_pack_rows(jnp.concatenate([p.reshape(-1) for p in placed]), 128, 8),
                             "allgather_small").reshape(-1)
    w_full, off = dict(w_loc), 0
    for n, p in zip(sm_names, placed):
        w_full[n] = summed[off:off + p.size].reshape(p.shape)
        off += p.size

    loss, dx, grads = local_step(x[0], loss_target[0], w_full, wb)
    loss = lax.psum(loss, ("x", "y", "c"))

    gsum = allreduce_small(_pack_rows(jnp.concatenate([grads[n].reshape(-1) for n in SMALL]), 128, 8),
                           "allreduce_small").reshape(-1)
    g_loc, off = {}, 0
    for n in SMALL:
        g = gsum[off:off + grads[n].size].reshape(grads[n].shape)
        off += grads[n].size
        g_loc[n] = _local_shard(g, SMALL_SHARDED[n], chip) if n in SMALL_SHARDED else g

    gpack = _pack_rows(jnp.concatenate([_to_shards(grads[n], BIG[n]) for n in big_names], axis=1), PACK_W, 32)
    from_sib = sibling_swap(gpack, True, "grad_swap_cores")
    part, part_bf = chip_sum(gpack, from_sib, core.reshape(1).astype(jnp.int32), "grad_chip_sum")
    from_chips = scatter_chips(part_bf, "grad_scatter")
    own = lax.dynamic_index_in_dim(part, chip, 0, keepdims=False)
    half = shard_sum(own, from_chips, "grad_shard_sum")
    other = sibling_swap(half, False, "grad_share_cores")
    both = jnp.stack([half, other])
    shard = jnp.concatenate([lax.dynamic_index_in_dim(both, core, 0, keepdims=False),
                             lax.dynamic_index_in_dim(both, 1 - core, 0, keepdims=False)], axis=0).reshape(-1)
    off = 0
    for n in big_names:
        g_loc[n] = shard[off:off + w_loc[n].size].reshape(w_loc[n].shape)
        off += w_loc[n].size

    delta, new_m, new_v = {}, {}, {}
    for n in big_names:
        shp = w_loc[n].shape
        two_d = lambda a: a.reshape(-1, shp[-1])
        d, nm, nv = adamw(two_d(w_loc[n]), two_d(m_loc[n]), two_d(v_loc[n]), two_d(g_loc[n]), f"adamw_{n}")
        delta[n], new_m[n], new_v[n] = d.reshape(shp), nm.reshape(shp), nv.reshape(shp)
    pack = lambda d: _pack_rows(jnp.concatenate([d[n].reshape(-1) for n in SMALL]), 128, 128)
    res = adamw(pack(w_loc), pack(m_loc), pack(v_loc), pack(g_loc), "adamw_small")
    off = 0
    for n in SMALL:
        size = w_loc[n].size
        for dst, r in zip((delta, new_m, new_v), res):
            dst[n] = r.reshape(-1)[off:off + size].reshape(w_loc[n].shape)
        off += size

    return (loss, dx[None], *[g_loc[n] for n in WEIGHTS], *[delta[n] for n in WEIGHTS],
            *[new_m[n] for n in WEIGHTS], *[new_v[n] for n in WEIGHTS])
```

```python
import functools

import numpy as np
import jax
import jax.numpy as jnp
from jax import lax
from jax.experimental import pallas as pl
from jax.experimental.pallas import tpu as pltpu

F32 = jnp.float32
BF16 = jnp.bfloat16
HIGHEST = lax.Precision.HIGHEST
MESH = pl.DeviceIdType.MESH

D_MODEL = 1024
D_FF = 2816
N_LAYERS = 2
HEADS = 6
HEAD_DIM = 64
MIX_W = HEADS * HEAD_DIM
LRU_W = 256
LRU_BLOCKS = 4
RWKV_IN = 1408
D_IN = 3468
D_IN_PAD = 3584
NORM_EPS = 1e-6
GN_EPS = 64e-5
LRU_C = 8.0
CHUNK = 64
ROWS = 256
FF_CHUNK = 256
IN_CHUNK = 512
PACK_W = 1024
VMEM_LIMIT = 56 * 1024 * 1024

ADAM_LR, ADAM_B1, ADAM_B2, ADAM_EPS, ADAM_WD, ADAM_STEP = 0.001, 0.9, 0.999, 1e-08, 0.01, 10

WEIGHTS = ['ffn1_norm', 'ffn1_wi', 'ffn1_wo', 'mix_norm', 'w_in', 'w_out', 'lru_conv_w', 'lru_conv_b',
           'lru_gate_a_w', 'lru_gate_a_b', 'lru_gate_x_w', 'lru_gate_x_b', 'lru_lambda', 'lru_out_norm',
           'rwkv_mu', 'rwkv_w_up', 'rwkv_w_bias', 'rwkv_a_up', 'rwkv_a_bias', 'rwkv_g_up', 'rwkv_k_k',
           'rwkv_k_a', 'rwkv_r_k', 'rwkv_ln_g', 'rwkv_ln_b', 'rwkv_vres_w1', 'rwkv_vres_w2', 'rwkv_vres_b',
           'gdn_conv_w', 'gdn_a_log', 'gdn_dt_bias', 'gdn_norm', 'ffn2_norm', 'ffn2_wi', 'ffn2_wo', 'final_norm']
BIG = {'ffn1_wi': 2, 'ffn1_wo': 1, 'w_in': 2, 'w_out': 1, 'ffn2_wi': 2, 'ffn2_wo': 1}
SMALL_SHARDED = {'lru_conv_w': 2, 'rwkv_w_up': 2, 'rwkv_a_up': 2, 'rwkv_g_up': 2, 'rwkv_vres_w1': 1,
                 'rwkv_vres_w2': 2, 'gdn_conv_w': 2}
N_CHIPS = 4


def _params(sem=None):
    kw = dict(vmem_limit_bytes=VMEM_LIMIT)
    if sem is not None:
        kw['dimension_semantics'] = sem
    return pltpu.CompilerParams(**kw)


def _bdot(a, b, dims=(((1,), (0,)), ((), ()))):
    return lax.dot_general(a.astype(BF16), b.astype(BF16), dims, preferred_element_type=F32)


def _bdot_nt(a, b):
    return _bdot(a, b, (((1,), (1,)), ((), ())))


def _bdot_tn(a, b):
    return _bdot(a, b, (((0,), (0,)), ((), ())))


_DIMS = {'nn': (((1,), (0,)), ((), ())), 'nt': (((1,), (1,)), ((), ())), 'tn': (((0,), (0,)), ((), ()))}


def _split(a, terms):
    parts = []
    for _ in range(terms - 1):
        hi = a.astype(BF16)
        parts.append(hi)
        a = a - hi.astype(F32)
    parts.append(a.astype(BF16))
    return parts


def _dot3(a, b, kind):
    ah, al = _split(a, 2)
    bh, bl = _split(b, 2)
    d = lambda p, q: lax.dot_general(p, q, _DIMS[kind], preferred_element_type=F32)
    return d(ah, bh) + (d(ah, bl) + d(al, bh))


@functools.partial(jax.custom_vjp, nondiff_argnums=(2,))
def _cdot_k(a, b, kind):
    return _dot3(a, b, kind)


def _cdot_k_fwd(a, b, kind):
    return _dot3(a, b, kind), (a, b)


def _cdot_k_bwd(kind, res, ct):
    a, b = res
    if kind == 'nn':
        return _dot3(ct, b, 'nt'), _dot3(a, ct, 'tn')
    if kind == 'nt':
        return _dot3(ct, b, 'nn'), _dot3(ct, a, 'tn')
    return _dot3(b, ct, 'nt'), _dot3(a, ct, 'nn')


_cdot_k.defvjp(_cdot_k_fwd, _cdot_k_bwd)


def _cdot(a, b):
    return _cdot_k(a, b, 'nn')


def _cdot_nt(a, b):
    return _cdot_k(a, b, 'nt')


def _cdot_tn(a, b):
    return _cdot_k(a, b, 'tn')


def _hdot(a, b):
    return _cdot_k(a, b, 'nn')


def _dot_exact(x, m01, kind):
    d = lambda p: lax.dot_general(p, m01.astype(BF16), _DIMS[kind], preferred_element_type=F32)
    hi, mid, lo = _split(x, 3)
    return d(hi) + (d(mid) + d(lo))


@functools.partial(jax.custom_vjp, nondiff_argnums=(1,))
def _xdot(x, make_m):
    return _dot_exact(x, make_m(), 'nn')


def _xdot_fwd(x, make_m):
    return _dot_exact(x, make_m(), 'nn'), None


def _xdot_bwd(make_m, _, ct):
    return (_dot_exact(ct, make_m(), 'nt'),)


_xdot.defvjp(_xdot_fwd, _xdot_bwd)


def _iota2(n, m):
    return lax.broadcasted_iota(jnp.int32, (n, m), 0), lax.broadcasted_iota(jnp.int32, (n, m), 1)


def _head_blocks(w):
    ri, ci = _iota2(w, w)
    return (ri // HEAD_DIM == ci // HEAD_DIM).astype(F32)


def _segsum(x):
    return _xdot(x, functools.partial(_head_blocks, x.shape[-1]))


def _cumsum_rows(x):
    return _cumsum_k(x, x.shape[0])


@functools.partial(jax.custom_vjp, nondiff_argnums=(1,))
def _cumsum_k(x, n):
    return _lower_dot(x, n, False)


def _lower_dot(x, n, transpose):
    ri, ci = _iota2(n, n)
    m = ((ri <= ci) if transpose else (ri >= ci)).astype(BF16)
    d = lambda p: lax.dot_general(m, p, _DIMS['nn'], preferred_element_type=F32)
    hi, mid, lo = _split(x, 3)
    return d(hi) + (d(mid) + d(lo))


def _cumsum_k_fwd(x, n):
    return _lower_dot(x, n, False), None


def _cumsum_k_bwd(n, _, ct):
    return (_lower_dot(ct, n, True),)


_cumsum_k.defvjp(_cumsum_k_fwd, _cumsum_k_bwd)


def _rms(x, g):
    return x * lax.rsqrt(jnp.mean(x * x, axis=-1, keepdims=True) + NORM_EPS) * g


def _row_loop(n_rows, fn):
    def step(i, c):
        fn(pl.ds(pl.multiple_of(i * ROWS, ROWS), ROWS))
        return c
    lax.fori_loop(0, n_rows // ROWS, step, 0)


def ffn_fwd(x, g, wi, wo, name):
    T = x.shape[0]
    nj = D_FF // FF_CHUNK

    def body(x_ref, g_ref, wg_ref, wu_ref, wo_ref, o_ref, h_ref, acc_ref):
        j = pl.program_id(0)

        @pl.when(j == 0)
        def _():
            def init(r):
                h_ref[r, :] = _rms(x_ref[r, :], g_ref[...]).astype(BF16)
                acc_ref[r, :] = jnp.zeros((ROWS, D_MODEL), F32)
            _row_loop(T, init)

        def blk(r):
            hb = h_ref[r, :]
            gate = jnp.dot(hb, wg_ref[...], preferred_element_type=F32)
            up = jnp.dot(hb, wu_ref[...], preferred_element_type=F32)
            a = (gate * jax.nn.sigmoid(gate) * up).astype(BF16)
            acc_ref[r, :] += jnp.dot(a, wo_ref[...], preferred_element_type=F32)
        _row_loop(T, blk)

        @pl.when(j == nj - 1)
        def _():
            def fin(r):
                o_ref[r, :] = x_ref[r, :] + 0.5 * acc_ref[r, :]
            _row_loop(T, fin)

    full = pl.BlockSpec((T, D_MODEL), lambda j: (0, 0))
    return pl.pallas_call(
        body, name=name, grid=(nj,),
        in_specs=[full, pl.BlockSpec((1, D_MODEL), lambda j: (0, 0)),
                  pl.BlockSpec((D_MODEL, FF_CHUNK), lambda j: (0, j)),
                  pl.BlockSpec((D_MODEL, FF_CHUNK), lambda j: (0, j + nj)),
                  pl.BlockSpec((FF_CHUNK, D_MODEL), lambda j: (j, 0))],
        out_specs=full, out_shape=jax.ShapeDtypeStruct((T, D_MODEL), F32),
        scratch_shapes=[pltpu.VMEM((T, D_MODEL), BF16), pltpu.VMEM((T, D_MODEL), F32)],
        compiler_params=_params(("arbitrary",)))(x, g, wi, wi, wo)


def _norm_bwd_rows(x, g, dh, dres):
    rstd = lax.rsqrt(jnp.mean(x * x, axis=-1, keepdims=True) + NORM_EPS)
    xh = x * rstd
    dxh = dh * g
    dx = rstd * (dxh - xh * jnp.mean(dxh * xh, axis=-1, keepdims=True))
    return dres + dx, jnp.sum(dh * xh, axis=0, keepdims=True)


def ffn_bwd(x, dy, g, wi, wo, name):
    T = x.shape[0]
    nj = D_FF // FF_CHUNK

    def body(x_ref, dy_ref, g_ref, wg_ref, wu_ref, wo_ref, dx_ref, dg_ref, dwg_ref, dwu_ref, dwo_ref,
             h_ref, da_ref, dh_ref):
        j = pl.program_id(0)

        @pl.when(j == 0)
        def _():
            def init(r):
                h_ref[r, :] = _rms(x_ref[r, :], g_ref[...]).astype(BF16)
                da_ref[r, :] = (0.5 * dy_ref[r, :]).astype(BF16)
                dh_ref[r, :] = jnp.zeros((ROWS, D_MODEL), F32)
            _row_loop(T, init)

        dwg_ref[...] = jnp.zeros_like(dwg_ref)
        dwu_ref[...] = jnp.zeros_like(dwu_ref)
        dwo_ref[...] = jnp.zeros_like(dwo_ref)

        def blk(r):
            hb = h_ref[r, :]
            db = da_ref[r, :]
            gate = jnp.dot(hb, wg_ref[...], preferred_element_type=F32)
            up = jnp.dot(hb, wu_ref[...], preferred_element_type=F32)
            sg = jax.nn.sigmoid(gate)
            sl = gate * sg
            da = _bdot_nt(db, wo_ref[...])
            dup = (da * sl).astype(BF16)
            dgate = (da * up * (sg * (1.0 + gate * (1.0 - sg)))).astype(BF16)
            dwo_ref[...] += _bdot_tn((sl * up).astype(BF16), db)
            dwg_ref[...] += _bdot_tn(hb, dgate)
            dwu_ref[...] += _bdot_tn(hb, dup)
            dh_ref[r, :] += _bdot_nt(dgate, wg_ref[...]) + _bdot_nt(dup, wu_ref[...])
        _row_loop(T, blk)

        @pl.when(j == nj - 1)
        def _():
            dg_ref[...] = jnp.zeros_like(dg_ref)

            def fin(r):
                dx, dg = _norm_bwd_rows(x_ref[r, :], g_ref[...], dh_ref[r, :], dy_ref[r, :])
                dx_ref[r, :] = dx
                dg_ref[...] += dg
            _row_loop(T, fin)

    full = pl.BlockSpec((T, D_MODEL), lambda j: (0, 0))
    vec = pl.BlockSpec((1, D_MODEL), lambda j: (0, 0))
    return pl.pallas_call(
        body, name=name, grid=(nj,),
        in_specs=[full, full, vec,
                  pl.BlockSpec((D_MODEL, FF_CHUNK), lambda j: (0, j)),
                  pl.BlockSpec((D_MODEL, FF_CHUNK), lambda j: (0, j + nj)),
                  pl.BlockSpec((FF_CHUNK, D_MODEL), lambda j: (j, 0))],
        out_specs=(full, vec,
                   pl.BlockSpec((D_MODEL, FF_CHUNK), lambda j: (0, j)),
                   pl.BlockSpec((D_MODEL, FF_CHUNK), lambda j: (0, j)),
                   pl.BlockSpec((FF_CHUNK, D_MODEL), lambda j: (j, 0))),
        out_shape=(jax.ShapeDtypeStruct((T, D_MODEL), F32), jax.ShapeDtypeStruct((1, D_MODEL), F32),
                   jax.ShapeDtypeStruct((D_MODEL, D_FF), F32), jax.ShapeDtypeStruct((D_MODEL, D_FF), F32),
                   jax.ShapeDtypeStruct((D_FF, D_MODEL), F32)),
        scratch_shapes=[pltpu.VMEM((T, D_MODEL), BF16), pltpu.VMEM((T, D_MODEL), BF16),
                        pltpu.VMEM((T, D_MODEL), F32)],
        compiler_params=_params(("arbitrary",)))(x, dy, g, wi, wi, wo)


def proj_fwd(x, g, w, name):
    T = x.shape[0]
    nj = D_IN_PAD // IN_CHUNK

    def body(x_ref, g_ref, w_ref, o_ref, h_ref):
        @pl.when(pl.program_id(0) == 0)
        def _():
            def init(r):
                h_ref[r, :] = _rms(x_ref[r, :], g_ref[...]).astype(BF16)
            _row_loop(T, init)

        def blk(r):
            o_ref[r, :] = jnp.dot(h_ref[r, :], w_ref[...], preferred_element_type=F32)
        _row_loop(T, blk)

    return pl.pallas_call(
        body, name=name, grid=(nj,),
        in_specs=[pl.BlockSpec((T, D_MODEL), lambda j: (0, 0)), pl.BlockSpec((1, D_MODEL), lambda j: (0, 0)),
                  pl.BlockSpec((D_MODEL, IN_CHUNK), lambda j: (0, j))],
        out_specs=pl.BlockSpec((T, IN_CHUNK), lambda j: (0, j)),
        out_shape=jax.ShapeDtypeStruct((T, D_IN_PAD), F32),
        scratch_shapes=[pltpu.VMEM((T, D_MODEL), BF16)],
        compiler_params=_params(("arbitrary",)))(x, g, w)


def proj_bwd(x, dres, g, w, dp, name):
    T = x.shape[0]
    nj = D_IN_PAD // IN_CHUNK

    def body(x_ref, dres_ref, g_ref, w_ref, dp_ref, dx_ref, dg_ref, dw_ref, h_ref, dh_ref):
        j = pl.program_id(0)

        @pl.when(j == 0)
        def _():
            def init(r):
                h_ref[r, :] = _rms(x_ref[r, :], g_ref[...]).astype(BF16)
                dh_ref[r, :] = jnp.zeros((ROWS, D_MODEL), F32)
            _row_loop(T, init)

        dw_ref[...] = jnp.zeros_like(dw_ref)

        def blk(r):
            dpb = dp_ref[r, :].astype(BF16)
            dw_ref[...] += _bdot_tn(h_ref[r, :], dpb)
            dh_ref[r, :] += _bdot_nt(dpb, w_ref[...])
        _row_loop(T, blk)

        @pl.when(j == nj - 1)
        def _():
            dg_ref[...] = jnp.zeros_like(dg_ref)

            def fin(r):
                dx, dg = _norm_bwd_rows(x_ref[r, :], g_ref[...], dh_ref[r, :], dres_ref[r, :])
                dx_ref[r, :] = dx
                dg_ref[...] += dg
            _row_loop(T, fin)

    full = pl.BlockSpec((T, D_MODEL), lambda j: (0, 0))
    vec = pl.BlockSpec((1, D_MODEL), lambda j: (0, 0))
    return pl.pallas_call(
        body, name=name, grid=(nj,),
        in_specs=[full, full, vec, pl.BlockSpec((D_MODEL, IN_CHUNK), lambda j: (0, j)),
                  pl.BlockSpec((T, IN_CHUNK), lambda j: (0, j))],
        out_specs=(full, vec, pl.BlockSpec((D_MODEL, IN_CHUNK), lambda j: (0, j))),
        out_shape=(jax.ShapeDtypeStruct((T, D_MODEL), F32), jax.ShapeDtypeStruct((1, D_MODEL), F32),
                   jax.ShapeDtypeStruct((D_MODEL, D_IN_PAD), F32)),
        scratch_shapes=[pltpu.VMEM((T, D_MODEL), BF16), pltpu.VMEM((T, D_MODEL), F32)],
        compiler_params=_params(("arbitrary",)))(x, dres, g, w, dp)


def out_fwd(mixed, w, x, name):
    T = x.shape[0]

    def body(m_ref, w_ref, x_ref, o_ref):
        o_ref[...] = x_ref[...] + jnp.dot(m_ref[...].astype(BF16), w_ref[...], preferred_element_type=F32)

    blk = pl.BlockSpec((ROWS, D_MODEL), lambda i: (i, 0))
    return pl.pallas_call(
        body, name=name, grid=(T // ROWS,),
        in_specs=[blk, pl.BlockSpec((D_MODEL, D_MODEL), lambda i: (0, 0)), blk],
        out_specs=blk, out_shape=jax.ShapeDtypeStruct((T, D_MODEL), F32),
        compiler_params=_params(("arbitrary",)))(mixed, w, x)


def out_bwd(mixed, w, dy, name):
    T = dy.shape[0]

    def body(m_ref, w_ref, dy_ref, dm_ref, dw_ref):
        @pl.when(pl.program_id(0) == 0)
        def _():
            dw_ref[...] = jnp.zeros_like(dw_ref)
        dyb = dy_ref[...].astype(BF16)
        dm_ref[...] = _bdot_nt(dyb, w_ref[...])
        dw_ref[...] += _bdot_tn(m_ref[...].astype(BF16), dyb)

    blk = pl.BlockSpec((ROWS, D_MODEL), lambda i: (i, 0))
    sq = pl.BlockSpec((D_MODEL, D_MODEL), lambda i: (0, 0))
    return pl.pallas_call(
        body, name=name, grid=(T // ROWS,),
        in_specs=[blk, sq, blk], out_specs=(blk, sq),
        out_shape=(jax.ShapeDtypeStruct((T, D_MODEL), F32), jax.ShapeDtypeStruct((D_MODEL, D_MODEL), F32)),
        compiler_params=_params(("arbitrary",)))(mixed, w, dy)


def loss_head(x, g, target, name):
    T = x.shape[0]

    def body(x_ref, g_ref, t_ref, loss_ref, dx_ref, dg_ref):
        @pl.when(pl.program_id(0) == 0)
        def _():
            loss_ref[...] = jnp.zeros_like(loss_ref)
            dg_ref[...] = jnp.zeros_like(dg_ref)
        xb = x_ref[...]
        rstd = lax.rsqrt(jnp.mean(xb * xb, axis=-1, keepdims=True) + NORM_EPS)
        xh = xb * rstd
        err = xh * g_ref[...] - t_ref[...]
        loss_ref[...] += 0.5 * jnp.sum(jnp.mean(err * err, axis=-1, keepdims=True), axis=0, keepdims=True)
        dy = err * (1.0 / D_MODEL)
        dg_ref[...] += jnp.sum(dy * xh, axis=0, keepdims=True)
        dxh = dy * g_ref[...]
        dx_ref[...] = rstd * (dxh - xh * jnp.mean(dxh * xh, axis=-1, keepdims=True))

    blk = pl.BlockSpec((ROWS, D_MODEL), lambda i: (i, 0))
    vec = pl.BlockSpec((1, D_MODEL), lambda i: (0, 0))
    return pl.pallas_call(
        body, name=name, grid=(T // ROWS,),
        in_specs=[blk, vec, blk], out_specs=(pl.BlockSpec((1, 1), lambda i: (0, 0)), blk, vec),
        out_shape=(jax.ShapeDtypeStruct((1, 1), F32), jax.ShapeDtypeStruct((T, D_MODEL), F32),
                   jax.ShapeDtypeStruct((1, D_MODEL), F32)),
        compiler_params=_params(("arbitrary",)))(x, g, target)


def rowwise_fwd(fn, rows, shared, out_widths, name):
    T = rows[0].shape[0]
    n_in = len(rows) + len(shared)

    def body(*refs):
        res = fn(*[r[...] for r in refs[:n_in]])
        for o, v in zip(refs[n_in:], res):
            o[...] = v

    in_specs = ([pl.BlockSpec((ROWS, a.shape[1]), lambda i: (i, 0)) for a in rows]
                + [pl.BlockSpec(a.shape, lambda i: (0, 0)) for a in shared])
    return pl.pallas_call(
        body, name=name, grid=(T // ROWS,), in_specs=in_specs,
        out_specs=tuple(pl.BlockSpec((ROWS, w), lambda i: (i, 0)) for w in out_widths),
        out_shape=tuple(jax.ShapeDtypeStruct((T, w), F32) for w in out_widths),
        compiler_params=_params(("arbitrary",)))(*rows, *shared)


def rowwise_bwd(fn, rows, shared, cts, name, ct_fn=None):
    T = rows[0].shape[0]
    nr, ns, nc = len(rows), len(shared), len(cts)

    def body(*refs):
        ins = [r[...] for r in refs[:nr + ns]]
        ctv = tuple(r[...] for r in refs[nr + ns:nr + ns + nc])
        outs = refs[nr + ns + nc:]
        _, vjp = jax.vjp(fn, *ins)
        grads = vjp(ct_fn(*ctv) if ct_fn is not None else ctv)
        for k in range(nr):
            outs[k][...] = grads[k]

        @pl.when(pl.program_id(0) == 0)
        def _():
            for k in range(ns):
                outs[nr + k][...] = jnp.zeros_like(outs[nr + k])
        for k in range(ns):
            outs[nr + k][...] += grads[nr + k]

    row_spec = lambda a: pl.BlockSpec((ROWS, a.shape[1]), lambda i: (i, 0))
    sh_spec = lambda a: pl.BlockSpec(a.shape, lambda i: (0, 0))
    return pl.pallas_call(
        body, name=name, grid=(T // ROWS,),
        in_specs=[row_spec(a) for a in rows] + [sh_spec(a) for a in shared] + [row_spec(a) for a in cts],
        out_specs=tuple([row_spec(a) for a in rows] + [sh_spec(a) for a in shared]),
        out_shape=tuple(jax.ShapeDtypeStruct(a.shape, F32) for a in list(rows) + list(shared)),
        compiler_params=_params(("arbitrary",)))(*rows, *shared, *cts)


def shift_rows(x, s):
    return jnp.pad(x, ((s, 0), (0, 0)))[:x.shape[0]]


def unshift_rows(x, s):
    return jnp.pad(x, ((0, s), (0, 0)))[s:]


def _neg_expm1(y):
    series = -(y * (1.0 + y * (0.5 + y * (1.0 / 6.0 + y * (1.0 / 24.0)))))
    return jnp.where(y > -0.05, series, 1.0 - jnp.exp(y))


def lru_pre_fn(x0, x1, x2, x3, first, w0, w1, w2, w3, cb, ga, gab, gx, gxb, lam):
    xc = w3 * x0 + w2 * x1 + w1 * x2 + w0 * x3 + cb
    r = jax.nn.sigmoid(_hdot(xc, ga) + gab)
    i = jax.nn.sigmoid(_hdot(xc, gx) + gxb)
    log_a = -LRU_C * r * jax.nn.softplus(-lam)
    a = jnp.exp(log_a)
    mult = jnp.where(first > 0.5, 1.0, jnp.sqrt(_neg_expm1(2.0 * log_a)))
    return a, mult * i * xc


def lru_post_fn(h, py, og):
    return (_rms(h * jax.nn.gelu(py), og),)


def lru_scan(a, b, reverse, name):
    T, C = a.shape
    nb = T // 8

    def body(a_ref, b_ref, h_ref):
        rows = lax.broadcasted_iota(jnp.int32, (8, C), 0)

        def blk(i, carry):
            j = nb - 1 - i if reverse else i
            r = pl.ds(pl.multiple_of(j * 8, 8), 8)
            A = a_ref[r, :]
            B = b_ref[r, :]
            for s in (1, 2, 4):
                if reverse:
                    keep = rows < 8 - s
                    sh = 8 - s
                else:
                    keep = rows >= s
                    sh = s
                Bs = jnp.where(keep, pltpu.roll(B, sh, 0), 0.0)
                As = jnp.where(keep, pltpu.roll(A, sh, 0), 1.0)
                B = B + A * Bs
                A = A * As
            hb = B + A * carry
            h_ref[r, :] = hb
            edge = 0 if reverse else 7
            return jnp.sum(jnp.where(rows == edge, hb, 0.0), axis=0, keepdims=True)

        lax.fori_loop(0, nb, blk, jnp.zeros((1, C), F32))

    full = pl.BlockSpec((T, C), lambda: (0, 0))
    return pl.pallas_call(body, name=name, in_specs=[full, full], out_specs=full,
                          out_shape=jax.ShapeDtypeStruct((T, C), F32), compiler_params=_params())(a, b)


def make_rwkv_pre_fn(has_vres):
    def fn(p, pp, *rest):
        if has_vres:
            vf, mu, w_up, w_b, a_up, a_b, g_up, kk_w, ka_w, vw1, vw2, vb = rest
        else:
            mu, w_up, w_b, a_up, a_b, g_up, kk_w, ka_w = rest
        xm = p + (pp - p) * mu
        r, k, v = xm[:, 0:384], xm[:, 384:768], xm[:, 768:1152]
        xw, xa, xg = xm[:, 1152:1216], xm[:, 1216:1280], xm[:, 1280:1408]
        w_log = -jax.nn.softplus(-(w_b + _hdot(jnp.tanh(xw), w_up))) - 0.5
        lw = -jnp.exp(w_log)
        a = jax.nn.sigmoid(a_b + _hdot(xa, a_up))
        g = _hdot(jax.nn.sigmoid(xg), g_up)
        if has_vres:
            v = v + (vf - v) * jax.nn.sigmoid(vb + _hdot(_hdot(v, vw1), vw2))
        kkx = k * kk_w
        kk = kkx * lax.rsqrt(_segsum(kkx * kkx) + 1e-6)
        k2 = k * (1.0 + (a - 1.0) * ka_w)
        return r, lw, k2, v, kk, a, g
    return fn


def rwkv_post_fn(y, r, k2, v, g, ln_g, ln_b, r_k):
    mean = _segsum(y) * (1.0 / HEAD_DIM)
    yc = y - mean
    var = _segsum(yc * yc) * (1.0 / HEAD_DIM)
    yn = yc * lax.rsqrt(var + GN_EPS) * ln_g + ln_b
    bonus = _segsum(r * k2 * r_k) * v
    return ((yn + bonus) * g,)


def _head_expander(first_lane):
    ri, ci = _iota2(128, MIX_W)
    return (ri == ci // HEAD_DIM + first_lane).astype(F32)


def gdn_pre_fn(x0, x1, x2, x3, ab, w0, w1, w2, w3, alog, dtb):
    qkv = jax.nn.silu(w3 * x0 + w2 * x1 + w1 * x2 + w0 * x3)
    q, k, v = qkv[:, 0:384], qkv[:, 384:768], qkv[:, 768:1152]
    q = q * lax.rsqrt(_segsum(q * q) + 1e-6) * (HEAD_DIM ** -0.5)
    k = k * lax.rsqrt(_segsum(k * k) + 1e-6)
    g = -jnp.exp(alog) * jax.nn.softplus(ab + dtb)
    beta = jax.nn.sigmoid(ab)
    ge = _xdot(g, functools.partial(_head_expander, 0))
    be = _xdot(beta, functools.partial(_head_expander, HEADS))
    return q, k, v, ge, be


def gdn_post_fn(o, z, ng):
    ms = _segsum(o * o) * (1.0 / HEAD_DIM)
    return (o * lax.rsqrt(ms + NORM_EPS) * ng * jax.nn.silu(z),)


def _neumann_inv(m):
    n = m.shape[0]
    ri, ci = _iota2(n, n)
    eye = (ri == ci).astype(F32)
    md = jnp.where(ri // 16 == ci // 16, m, 0.0)
    mo = m - md
    t0 = eye + md
    p2 = _hdot(md, md)
    t0 = t0 + _hdot(t0, p2)
    p4 = _hdot(p2, p2)
    t0 = t0 + _hdot(t0, p4)
    p8 = _hdot(p4, p4)
    t0 = t0 + _hdot(t0, p8)
    nn = _hdot(t0, mo)
    n2 = _hdot(nn, nn)
    t1 = eye + nn + n2 + _hdot(nn, n2)
    return _hdot(t1, t0)


@jax.custom_vjp
def _inv_saved(m, t_saved):
    return t_saved


def _inv_saved_fwd(m, t_saved):
    return t_saved, t_saved


def _inv_saved_bwd(t_saved, dt):
    tt = t_saved.T
    return _hdot(_hdot(tt, dt), tt), jnp.zeros_like(t_saved)


_inv_saved.defvjp(_inv_saved_fwd, _inv_saved_bwd)


def rwkv_head(s0, r, lw, k2, v, kk, a, inv):
    n = r.shape[0]
    ri, ci = _iota2(n, n)
    low, strict = ri >= ci, ri > ci
    cs = _cumsum_rows(lw)
    cl = jnp.sum(lw, axis=0, keepdims=True)
    p_in, p_prev, p_inv = jnp.exp(cs), jnp.exp(cs - lw), jnp.exp(-cs)
    p_rest, p_all = jnp.exp(cl - cs), jnp.exp(cl)
    ad, bd = -kk, kk * a
    at, rt = ad * p_prev, r * p_in
    bh, kh = bd * p_inv, k2 * p_inv
    m_ab = jnp.where(strict, _cdot_nt(at, bh), 0.0)
    m_ak = jnp.where(strict, _cdot_nt(at, kh), 0.0)
    m_rb = jnp.where(low, _cdot_nt(rt, bh), 0.0)
    m_rk = jnp.where(low, _cdot_nt(rt, kh), 0.0)
    sa = _cdot(inv(m_ab), _cdot_nt(at, s0) + _cdot(m_ak, v))
    y = _cdot_nt(rt, s0) + _cdot(m_rb, sa) + _cdot(m_rk, v)
    s1 = s0 * p_all + _cdot_tn(sa, bd * p_rest) + _cdot_tn(v, k2 * p_rest)
    return y, s1


def gdn_head(s0, q, k, v, ge, be, inv):
    n = q.shape[0]
    ri, ci = _iota2(n, n)
    low, strict = ri >= ci, ri > ci
    gc = _cumsum_rows(ge)
    gl = jnp.sum(ge, axis=0, keepdims=True)
    decay = jnp.where(low, jnp.exp(jnp.where(low, gc - gc.T, 0.0)), 0.0)
    kb = k * be
    e = jnp.exp(gc)
    m = -jnp.where(strict, _cdot_nt(kb, k) * decay, 0.0)
    mr = jnp.where(low, _cdot_nt(q, k) * decay, 0.0)
    u = _cdot(inv(m), v * be - _cdot_nt(kb * e, s0))
    y = _cdot_nt(q * e, s0) + _cdot(mr, u)
    s1 = s0 * jnp.exp(gl) + _cdot_tn(u, k * jnp.exp(gl - gc))
    return y, s1


def core_fwd(head_fn, ins, name):
    T = ins[0].shape[0]
    nc = T // CHUNK
    n = len(ins)

    def body(*refs):
        y_ref, s0_ref, t_ref, s_ref = refs[n:n + 4]

        @pl.when(pl.program_id(0) == 0)
        def _():
            s_ref[...] = jnp.zeros_like(s_ref)

        s_all = s_ref[...]
        s0_ref[0] = s_all
        new_states = []
        for h in range(HEADS):
            lanes = slice(h * HEAD_DIM, (h + 1) * HEAD_DIM)
            kept = []

            def inv(m):
                kept.append(_neumann_inv(m))
                return kept[0]

            y, s1 = head_fn(s_all[lanes, :], *[r[:, lanes] for r in refs[:n]], inv)
            y_ref[:, lanes] = y
            t_ref[0, lanes, :] = kept[0]
            new_states.append(s1)
        s_ref[...] = jnp.concatenate(new_states, axis=0)

    row = pl.BlockSpec((CHUNK, MIX_W), lambda c: (c, 0))
    st = pl.BlockSpec((1, MIX_W, HEAD_DIM), lambda c: (c, 0, 0))
    return pl.pallas_call(
        body, name=name, grid=(nc,), in_specs=[row] * n, out_specs=(row, st, st),
        out_shape=(jax.ShapeDtypeStruct((T, MIX_W), F32), jax.ShapeDtypeStruct((nc, MIX_W, HEAD_DIM), F32),
                   jax.ShapeDtypeStruct((nc, MIX_W, HEAD_DIM), F32)),
        scratch_shapes=[pltpu.VMEM((MIX_W, HEAD_DIM), F32)],
        compiler_params=_params(("arbitrary",)))(*ins)


def core_bwd(head_fn, ins, s0_all, t_all, dy, name):
    T = ins[0].shape[0]
    nc = T // CHUNK
    n = len(ins)

    def body(*refs):
        s0_ref, t_ref, dy_ref = refs[n:n + 3]
        outs = refs[n + 3:n + 3 + n]
        ds_ref = refs[n + 3 + n]

        @pl.when(pl.program_id(0) == 0)
        def _():
            ds_ref[...] = jnp.zeros_like(ds_ref)

        ds_all = ds_ref[...]
        new_ds = []
        for h in range(HEADS):
            lanes = slice(h * HEAD_DIM, (h + 1) * HEAD_DIM)
            t_saved = t_ref[0, lanes, :]
            f = lambda s0, *xs: head_fn(s0, *xs, lambda m: _inv_saved(m, t_saved))
            _, vjp = jax.vjp(f, s0_ref[0, lanes, :], *[r[:, lanes] for r in refs[:n]])
            grads = vjp((dy_ref[:, lanes], ds_all[lanes, :]))
            new_ds.append(grads[0])
            for k in range(n):
                outs[k][:, lanes] = grads[1 + k]
        ds_ref[...] = jnp.concatenate(new_ds, axis=0)

    row = pl.BlockSpec((CHUNK, MIX_W), lambda c: (nc - 1 - c, 0))
    st = pl.BlockSpec((1, MIX_W, HEAD_DIM), lambda c: (nc - 1 - c, 0, 0))
    return pl.pallas_call(
        body, name=name, grid=(nc,), in_specs=[row] * n + [st, st, row], out_specs=tuple([row] * n),
        out_shape=tuple(jax.ShapeDtypeStruct((T, MIX_W), F32) for _ in range(n)),
        scratch_shapes=[pltpu.VMEM((MIX_W, HEAD_DIM), F32)],
        compiler_params=_params(("arbitrary",)))(*ins, s0_all, t_all, dy)


def _block_diag(w):
    out = jnp.zeros((LRU_W, LRU_W), w.dtype)
    for n in range(LRU_BLOCKS):
        out = lax.dynamic_update_slice(out, w[n], (n * 64, n * 64))
    return out


def _block_diag_grad(g):
    return jnp.stack([g[n * 64:(n + 1) * 64, n * 64:(n + 1) * 64] for n in range(LRU_BLOCKS)])


def _row(v):
    return v.reshape(1, -1)


def _pad128(v):
    return jnp.pad(v.reshape(1, -1), ((0, 0), (0, 128 - v.size)))


def _layer_shared(w, l):
    cw = w['lru_conv_w'][l]
    lru_pre = [_row(cw[0]), _row(cw[1]), _row(cw[2]), _row(cw[3]), _row(w['lru_conv_b'][l]),
               _block_diag(w['lru_gate_a_w'][l]), _row(w['lru_gate_a_b'][l]),
               _block_diag(w['lru_gate_x_w'][l]), _row(w['lru_gate_x_b'][l]), _row(w['lru_lambda'][l])]
    rw_pre = [_row(w['rwkv_mu'][l]), w['rwkv_w_up'][l], _row(w['rwkv_w_bias'][l]), w['rwkv_a_up'][l],
              _row(w['rwkv_a_bias'][l]), w['rwkv_g_up'][l], _row(w['rwkv_k_k'][l]), _row(w['rwkv_k_a'][l])]
    if l > 0:
        rw_pre += [w['rwkv_vres_w1'][l - 1], w['rwkv_vres_w2'][l - 1], _row(w['rwkv_vres_b'][l - 1])]
    rw_post = [_row(w['rwkv_ln_g'][l]), _row(w['rwkv_ln_b'][l]), _row(w['rwkv_r_k'][l])]
    gw = w['gdn_conv_w'][l]
    gdn_pre = [_row(gw[0]), _row(gw[1]), _row(gw[2]), _row(gw[3]), _pad128(w['gdn_a_log'][l]),
               _pad128(w['gdn_dt_bias'][l])]
    gdn_post = [_row(jnp.tile(w['gdn_norm'][l], HEADS))]
    return dict(lru_pre=lru_pre, lru_post=[_row(w['lru_out_norm'][l])], rw_pre=rw_pre, rw_post=rw_post,
                gdn_pre=gdn_pre, gdn_post=gdn_post)


def _mixer_fwd(p, sh, l, v_first):
    T = p.shape[0]
    lx, ly = p[:, 0:256], p[:, 256:512]
    prw, qkv, z, ab = p[:, 512:1920], p[:, 1920:3072], p[:, 3072:3456], p[:, 3456:3584]
    first = jnp.zeros((T, LRU_W), F32).at[0].set(1.0)
    lru_rows = [lx, shift_rows(lx, 1), shift_rows(lx, 2), shift_rows(lx, 3), first]
    a, b = rowwise_fwd(lru_pre_fn, lru_rows, sh['lru_pre'], (LRU_W, LRU_W), f"lru_pre_fwd{l}")
    hseq = lru_scan(a, b, False, f"lru_scan_fwd{l}")
    (y_lru,) = rowwise_fwd(lru_post_fn, [hseq, ly], sh['lru_post'], (LRU_W,), f"lru_post_fwd{l}")

    rw_rows = [prw, shift_rows(prw, 1)] + ([v_first] if l > 0 else [])
    rw = rowwise_fwd(make_rwkv_pre_fn(l > 0), rw_rows, sh['rw_pre'], (MIX_W,) * 7, f"rwkv_pre_fwd{l}")
    r, lw, k2, v, kk, ar, g = rw
    y_raw, rs0, rt = core_fwd(rwkv_head, [r, lw, k2, v, kk, ar], f"rwkv_core_fwd{l}")
    (y_rw,) = rowwise_fwd(rwkv_post_fn, [y_raw, r, k2, v, g], sh['rw_post'], (MIX_W,), f"rwkv_post_fwd{l}")

    gdn_rows = [qkv, shift_rows(qkv, 1), shift_rows(qkv, 2), shift_rows(qkv, 3), ab]
    gd = rowwise_fwd(gdn_pre_fn, gdn_rows, sh['gdn_pre'], (MIX_W,) * 5, f"gdn_pre_fwd{l}")
    o_raw, gs0, gt = core_fwd(gdn_head, list(gd), f"gdn_core_fwd{l}")
    (y_gdn,) = rowwise_fwd(gdn_post_fn, [o_raw, z], sh['gdn_post'], (MIX_W,), f"gdn_post_fwd{l}")

    mixed = jnp.concatenate([y_lru, y_rw, y_gdn], axis=1)
    saved = dict(lru_rows=lru_rows, a=a, hseq=hseq, ly=ly, rw_rows=rw_rows, rw=rw, y_raw=y_raw, rs0=rs0, rt=rt,
                 gdn_rows=gdn_rows, gd=gd, o_raw=o_raw, gs0=gs0, gt=gt, z=z)
    v_layer0 = v if l == 0 else None
    return mixed, saved, v_layer0


def _mixer_bwd(dmixed, sv, sh, l, dv_first):
    d_lru, d_rw, d_gdn = dmixed[:, 0:256], dmixed[:, 256:640], dmixed[:, 640:1024]
    gw = {}

    dh, dly, d_og = rowwise_bwd(lru_post_fn, [sv['hseq'], sv['ly']], sh['lru_post'], [d_lru], f"lru_post_bwd{l}")
    gscan = lru_scan(unshift_rows(sv['a'], 1), dh, True, f"lru_scan_bwd{l}")
    res = rowwise_bwd(lru_pre_fn, sv['lru_rows'], sh['lru_pre'], [gscan, shift_rows(sv['hseq'], 1)],
                      f"lru_pre_bwd{l}", ct_fn=lambda gs, hp: (gs * hp, gs))
    dlx = res[0] + unshift_rows(res[1], 1) + unshift_rows(res[2], 2) + unshift_rows(res[3], 3)
    dw0, dw1, dw2, dw3, dcb, dga, dgab, dgx, dgxb, dlam = res[5:]
    gw['lru_conv_w'] = jnp.concatenate([dw0, dw1, dw2, dw3], axis=0)
    gw['lru_conv_b'] = dcb[0]
    gw['lru_gate_a_w'] = _block_diag_grad(dga)
    gw['lru_gate_a_b'] = dgab.reshape(LRU_BLOCKS, 64)
    gw['lru_gate_x_w'] = _block_diag_grad(dgx)
    gw['lru_gate_x_b'] = dgxb.reshape(LRU_BLOCKS, 64)
    gw['lru_lambda'] = dlam[0]
    gw['lru_out_norm'] = d_og[0]

    r, lw, k2, v, kk, ar, g = sv['rw']
    res = rowwise_bwd(rwkv_post_fn, [sv['y_raw'], r, k2, v, g], sh['rw_post'], [d_rw], f"rwkv_post_bwd{l}")
    dy_raw, dr_p, dk2_p, dv_p, dg = res[:5]
    gw['rwkv_ln_g'], gw['rwkv_ln_b'], gw['rwkv_r_k'] = res[5][0], res[6][0], res[7].reshape(HEADS, HEAD_DIM)
    dr_c, dlw, dk2_c, dv_c, dkk, dar = core_bwd(rwkv_head, [r, lw, k2, v, kk, ar], sv['rs0'], sv['rt'], dy_raw,
                                                 f"rwkv_core_bwd{l}")
    cts = [dr_p, dr_c, dlw, dk2_p, dk2_c, dv_p, dv_c, dkk, dar, dg]
    if l == 0:
        cts.append(dv_first)
        ct_fn = lambda a1, a2, b, c1, c2, d1, d2, e, f, gg, vf: (a1 + a2, b, c1 + c2, d1 + d2 + vf, e, f, gg)
    else:
        ct_fn = lambda a1, a2, b, c1, c2, d1, d2, e, f, gg: (a1 + a2, b, c1 + c2, d1 + d2, e, f, gg)
    res = rowwise_bwd(make_rwkv_pre_fn(l > 0), sv['rw_rows'], sh['rw_pre'], cts, f"rwkv_pre_bwd{l}", ct_fn=ct_fn)
    dprw = res[0] + unshift_rows(res[1], 1)
    nrow = len(sv['rw_rows'])
    dv_first_out = res[2] if l > 0 else None
    sg = res[nrow:]
    gw['rwkv_mu'], gw['rwkv_w_up'], gw['rwkv_w_bias'], gw['rwkv_a_up'] = sg[0][0], sg[1], sg[2][0], sg[3]
    gw['rwkv_a_bias'], gw['rwkv_g_up'], gw['rwkv_k_k'], gw['rwkv_k_a'] = sg[4][0], sg[5], sg[6][0], sg[7][0]
    if l > 0:
        gw['rwkv_vres_w1'], gw['rwkv_vres_w2'], gw['rwkv_vres_b'] = sg[8], sg[9], sg[10][0]

    do_raw, dz, d_ng = rowwise_bwd(gdn_post_fn, [sv['o_raw'], sv['z']], sh['gdn_post'], [d_gdn], f"gdn_post_bwd{l}")
    gw['gdn_norm'] = jnp.sum(d_ng.reshape(HEADS, HEAD_DIM), axis=0)
    dgd = core_bwd(gdn_head, list(sv['gd']), sv['gs0'], sv['gt'], do_raw, f"gdn_core_bwd{l}")
    res = rowwise_bwd(gdn_pre_fn, sv['gdn_rows'], sh['gdn_pre'], list(dgd), f"gdn_pre_bwd{l}")
    dqkv = res[0] + unshift_rows(res[1], 1) + unshift_rows(res[2], 2) + unshift_rows(res[3], 3)
    dab = res[4]
    gw['gdn_conv_w'] = jnp.concatenate(res[5:9], axis=0)
    gw['gdn_a_log'], gw['gdn_dt_bias'] = res[9][0, :HEADS], res[10][0, :HEADS]

    dp = jnp.concatenate([dlx, dly, dprw, dqkv, dz, dab], axis=1)
    return dp, gw, dv_first_out


IN_SHARD = D_IN // N_CHIPS
IN_SHARD_PAD = D_IN_PAD // N_CHIPS


def _cols_from_shards(p):
    T = p.shape[0]
    nat = p.reshape(T, N_CHIPS, IN_SHARD_PAD)[:, :, :IN_SHARD].reshape(T, D_IN)
    return jnp.pad(nat, ((0, 0), (0, D_IN_PAD - D_IN)))


def _cols_to_shards(p):
    T = p.shape[0]
    sh = jnp.pad(p[:, :D_IN].reshape(T, N_CHIPS, IN_SHARD), ((0, 0), (0, 0), (0, IN_SHARD_PAD - IN_SHARD)))
    return sh.reshape(T, D_IN_PAD)


def local_step(x, target, w, wb):
    saved = []
    v_first = None
    for l in range(N_LAYERS):
        sh = _layer_shared(w, l)
        x1 = ffn_fwd(x, _row(w['ffn1_norm'][l]), wb['ffn1_wi'][l], wb['ffn1_wo'][l], f"ffn1_fwd{l}")
        p = _cols_from_shards(proj_fwd(x1, _row(w['mix_norm'][l]), wb['w_in'][l], f"proj_fwd{l}"))
        mixed, sv, v0 = _mixer_fwd(p, sh, l, v_first)
        if l == 0:
            v_first = v0
        x2 = out_fwd(mixed, wb['w_out'][l], x1, f"out_fwd{l}")
        x3 = ffn_fwd(x2, _row(w['ffn2_norm'][l]), wb['ffn2_wi'][l], wb['ffn2_wo'][l], f"ffn2_fwd{l}")
        saved.append(dict(x0=x, x1=x1, x2=x2, mixed=mixed, sv=sv, sh=sh))
        x = x3

    loss, dx, dgf = loss_head(x, _row(w['final_norm']), target, "loss_head")
    per_layer = [None] * N_LAYERS
    dv_first = jnp.zeros((x.shape[0], MIX_W), F32)
    for l in reversed(range(N_LAYERS)):
        s = saved[l]
        gw = {}
        dx, dg2, dwg, dwu, dwo = ffn_bwd(s['x2'], dx, _row(w['ffn2_norm'][l]), wb['ffn2_wi'][l], wb['ffn2_wo'][l],
                                         f"ffn2_bwd{l}")
        gw['ffn2_norm'], gw['ffn2_wi'], gw['ffn2_wo'] = dg2[0], jnp.concatenate([dwg, dwu], axis=1), dwo
        dmixed, gw['w_out'] = out_bwd(s['mixed'], wb['w_out'][l], dx, f"out_bwd{l}")
        dp, gmix, dvf = _mixer_bwd(dmixed, s['sv'], s['sh'], l, dv_first)
        if l > 0:
            dv_first = dvf
        gw.update(gmix)
        dx, dgm, dwin = proj_bwd(s['x1'], dx, _row(w['mix_norm'][l]), wb['w_in'][l], _cols_to_shards(dp),
                                 f"proj_bwd{l}")
        gw['mix_norm'], gw['w_in'] = dgm[0], dwin
        dx, dg1, dwg, dwu, dwo = ffn_bwd(s['x0'], dx, _row(w['ffn1_norm'][l]), wb['ffn1_wi'][l], wb['ffn1_wo'][l],
                                         f"ffn1_bwd{l}")
        gw['ffn1_norm'], gw['ffn1_wi'], gw['ffn1_wo'] = dg1[0], jnp.concatenate([dwg, dwu], axis=1), dwo
        per_layer[l] = gw

    grads = {'final_norm': dgf[0]}
    for name in WEIGHTS:
        if name == 'final_norm':
            continue
        if name.startswith('rwkv_vres'):
            grads[name] = per_layer[1][name][None]
        else:
            grads[name] = jnp.stack([per_layer[l][name] for l in range(N_LAYERS)])
    return loss[0, 0], dx, grads


ANY = pl.BlockSpec(memory_space=pl.ANY)


def _coords():
    return lax.axis_index("x"), lax.axis_index("y"), lax.axis_index("c")


def _other_chips(x, y):
    return [((x + 1) % 2, y), (x, (y + 1) % 2), ((x + 1) % 2, (y + 1) % 2)]


def allreduce_small(pack, name):
    R = pack.shape[0]

    def body(x_ref, o_ref, buf, send_sems, recv_sems):
        x, y, c = _coords()
        me = 4 * x + 2 * y + c
        buf[me] = x_ref[...]
        copies = []
        for k in range(1, 8):
            peer = ((x + (k >> 2)) % 2, (y + ((k >> 1) & 1)) % 2, (c + (k & 1)) % 2)
            cp = pltpu.make_async_remote_copy(src_ref=x_ref, dst_ref=buf.at[me], send_sem=send_sems.at[k - 1],
                                              recv_sem=recv_sems.at[k - 1], device_id=peer, device_id_type=MESH)
            cp.start()
            copies.append(cp)
        for cp in copies:
            cp.wait()
        acc = buf[0]
        for d in range(1, 8):
            acc = acc + buf[d]
        o_ref[...] = acc

    vm = pl.BlockSpec(memory_space=pltpu.VMEM)
    return pl.pallas_call(
        body, name=name, in_specs=[vm], out_specs=vm, out_shape=jax.ShapeDtypeStruct((R, 128), F32),
        scratch_shapes=[pltpu.VMEM((8, R, 128), F32), pltpu.SemaphoreType.DMA((7,)), pltpu.SemaphoreType.DMA((7,))],
        compiler_params=_params())(pack)


def allgather_chips(shards, name):
    n = len(shards)

    def body(*refs):
        x_refs, o_refs = refs[:n], refs[n:2 * n]
        send_sems, recv_sems, local_sems = refs[2 * n:]
        x, y, c = _coords()
        s_me = 2 * x + y
        sib = (x, y, 1 - c)
        chips = _other_chips(x, y)

        def rows(a, core):
            rh = shards[a].shape[0] // 2
            return pl.ds(pl.multiple_of(core * rh, 16), rh)

        def copy(a, k, src, dst, to):
            return pltpu.make_async_remote_copy(src_ref=src, dst_ref=dst, send_sem=send_sems.at[6 * a + k],
                                                recv_sem=recv_sems.at[6 * a + k], device_id=to, device_id_type=MESH)

        own, sent, passed = [], [], []
        for a in range(n):
            cp = pltpu.make_async_copy(x_refs[a], o_refs[a].at[s_me], local_sems.at[a])
            cp.start()
            own.append(cp)
        for j, (px, py) in enumerate(chips):
            for a in range(n):
                cp = copy(a, j, x_refs[a].at[rows(a, c)], o_refs[a].at[s_me, rows(a, c)], (px, py, c))
                cp.start()
                sent.append(cp)
        for j, (px, py) in enumerate(chips):
            for a in range(n):
                part = o_refs[a].at[2 * px + py, rows(a, c)]
                copy(a, j, part, part, (px, py, c)).wait_recv()
                fw = copy(a, 3 + j, part, part, sib)
                fw.start()
                passed.append(fw)
        for j, (px, py) in enumerate(chips):
            for a in range(n):
                part = o_refs[a].at[2 * px + py, rows(a, 1 - c)]
                copy(a, 3 + j, part, part, sib).wait_recv()
        for cp in sent + passed:
            cp.wait_send()
        for cp in own:
            cp.wait()

    return pl.pallas_call(
        body, name=name, in_specs=[ANY] * n, out_specs=tuple([ANY] * n),
        out_shape=tuple(jax.ShapeDtypeStruct((N_CHIPS,) + s.shape, s.dtype) for s in shards),
        scratch_shapes=[pltpu.SemaphoreType.DMA((6 * n,)), pltpu.SemaphoreType.DMA((6 * n,)),
                        pltpu.SemaphoreType.DMA((n,))],
        compiler_params=_params())(*shards)


def sibling_swap(srcs, halves, name):
    n = len(srcs)
    out_shapes = [(s.shape[0], s.shape[1] // 2, s.shape[2]) if halves else s.shape for s in srcs]

    def body(*refs):
        x_refs, o_refs = refs[:n], refs[n:2 * n]
        send_sems, recv_sems = refs[2 * n:]
        x, y, c = _coords()
        copies = []
        for a in range(n):
            part = x_refs[a]
            if halves:
                rh = srcs[a].shape[1] // 2
                part = part.at[:, pl.ds(pl.multiple_of((1 - c) * rh, 16), rh)]
            cp = pltpu.make_async_remote_copy(src_ref=part, dst_ref=o_refs[a], send_sem=send_sems.at[a],
                                              recv_sem=recv_sems.at[a], device_id=(x, y, 1 - c), device_id_type=MESH)
            cp.start()
            copies.append(cp)
        for cp in copies:
            cp.wait()

    return pl.pallas_call(
        body, name=name, in_specs=[ANY] * n, out_specs=tuple([ANY] * n),
        out_shape=tuple(jax.ShapeDtypeStruct(sh, s.dtype) for sh, s in zip(out_shapes, srcs)),
        scratch_shapes=[pltpu.SemaphoreType.DMA((n,)), pltpu.SemaphoreType.DMA((n,))],
        compiler_params=_params())(*srcs)


def scatter_chips(parts, name):
    n = len(parts)

    def body(*refs):
        x_refs, o_refs = refs[:n], refs[n:2 * n]
        send_sems, recv_sems = refs[2 * n:]
        x, y, c = _coords()
        copies = []
        for j, (px, py) in enumerate(_other_chips(x, y)):
            for a in range(n):
                cp = pltpu.make_async_remote_copy(src_ref=x_refs[a].at[2 * px + py], dst_ref=o_refs[a].at[j],
                                                  send_sem=send_sems.at[3 * a + j], recv_sem=recv_sems.at[3 * a + j],
                                                  device_id=(px, py, c), device_id_type=MESH)
                cp.start()
                copies.append(cp)
        for cp in copies:
            cp.wait()

    return pl.pallas_call(
        body, name=name, in_specs=[ANY] * n, out_specs=tuple([ANY] * n),
        out_shape=tuple(jax.ShapeDtypeStruct((3,) + p.shape[1:], p.dtype) for p in parts),
        scratch_shapes=[pltpu.SemaphoreType.DMA((3 * n,)), pltpu.SemaphoreType.DMA((3 * n,))],
        compiler_params=_params())(*parts)


def _row_block(rows):
    return max(b for b in range(16, 257, 16) if rows % b == 0)


def chip_sum(gpack, recv, core, name):
    n, R, W = gpack.shape
    rh = R // 2
    rb = _row_block(rh)
    nb = rh // rb

    def body(c_ref, g_ref, r_ref, o_ref, ob_ref):
        s = g_ref[...] + r_ref[...]
        o_ref[...] = s
        ob_ref[...] = s.astype(BF16)

    blk = pl.BlockSpec((1, rb, W), lambda i, j, c_ref: (i, j, 0))
    spec = pltpu.PrefetchScalarGridSpec(
        num_scalar_prefetch=1, grid=(n, nb),
        in_specs=[pl.BlockSpec((1, rb, W), lambda i, j, c_ref: (i, c_ref[0] * nb + j, 0)), blk],
        out_specs=(blk, blk))
    return pl.pallas_call(
        body, name=name, grid_spec=spec,
        out_shape=(jax.ShapeDtypeStruct((n, rh, W), F32), jax.ShapeDtypeStruct((n, rh, W), BF16)),
        compiler_params=_params(("arbitrary", "arbitrary")))(core, gpack, recv)


def shard_sum(own, recv, name):
    R, W = own.shape
    rb = _row_block(R)

    def body(a_ref, r_ref, o_ref):
        acc = a_ref[...]
        for j in range(3):
            acc = acc + r_ref[j].astype(F32)
        o_ref[...] = acc

    return pl.pallas_call(
        body, name=name, grid=(R // rb,),
        in_specs=[pl.BlockSpec((rb, W), lambda i: (i, 0)), pl.BlockSpec((3, rb, W), lambda i: (0, i, 0))],
        out_specs=pl.BlockSpec((rb, W), lambda i: (i, 0)), out_shape=jax.ShapeDtypeStruct((R, W), F32),
        compiler_params=_params(("arbitrary",)))(own, recv)


def adamw(w, m, v, g, name):
    R, C = w.shape
    rb = 128 if R % 128 == 0 else R
    bc1 = 1.0 - ADAM_B1 ** ADAM_STEP
    bc2 = 1.0 - ADAM_B2 ** ADAM_STEP

    def body(w_ref, m_ref, v_ref, g_ref, d_ref, nm_ref, nv_ref):
        gv = g_ref[...]
        nm = ADAM_B1 * m_ref[...] + (1.0 - ADAM_B1) * gv
        nv = ADAM_B2 * v_ref[...] + (1.0 - ADAM_B2) * (gv * gv)
        d_ref[...] = -ADAM_LR * ((nm / bc1) / (jnp.sqrt(nv / bc2) + ADAM_EPS) + ADAM_WD * w_ref[...])
        nm_ref[...] = nm
        nv_ref[...] = nv

    blk = pl.BlockSpec((rb, C), lambda i: (i, 0))
    sh = jax.ShapeDtypeStruct((R, C), F32)
    return pl.pallas_call(body, name=name, grid=(R // rb,), in_specs=[blk] * 4, out_specs=(blk,) * 3,
                          out_shape=(sh, sh, sh), compiler_params=_params(("arbitrary",)))(w, m, v, g)


SMALL = [n for n in WEIGHTS if n not in BIG]


def _pack_rows(flat, width, row_multiple):
    n = flat.shape[-1]
    per = width * row_multiple
    total = -(-n // per) * per
    flat = jnp.pad(flat, [(0, 0)] * (flat.ndim - 1) + [(0, total - n)])
    return flat.reshape(flat.shape[:-1] + (total // width, width))


def _shard_2d(local):
    L, r, c = local.shape
    pad = -c % 128
    return jnp.pad(local, ((0, 0), (0, 0), (0, pad))).reshape(L * r, c + pad)


def _from_chips(g, axis):
    c = g.shape[2]
    g = g.reshape(N_CHIPS, N_LAYERS, -1, c)
    if axis == 2:
        return jnp.transpose(g, (1, 2, 0, 3)).reshape(N_LAYERS, -1, N_CHIPS * c)
    return jnp.transpose(g, (1, 0, 2, 3)).reshape(N_LAYERS, -1, c)


def _to_chips(full, axis):
    if axis == 2:
        c = full.shape[2] // N_CHIPS
        g = jnp.transpose(full.reshape(N_LAYERS, -1, N_CHIPS, c), (2, 0, 1, 3))
    else:
        c = full.shape[2]
        g = jnp.transpose(full.reshape(N_LAYERS, N_CHIPS, -1, c), (1, 0, 2, 3))
    return g.reshape(N_CHIPS, -1, c)


def _local_shard(full, axis, chip):
    size = full.shape[axis] // N_CHIPS
    return lax.dynamic_slice_in_dim(full, chip * size, size, axis)


def kernel(x, ffn1_norm, ffn1_wi, ffn1_wo, mix_norm, w_in, w_out, lru_conv_w, lru_conv_b, lru_gate_a_w, lru_gate_a_b, lru_gate_x_w, lru_gate_x_b, lru_lambda, lru_out_norm, rwkv_mu, rwkv_w_up, rwkv_w_bias, rwkv_a_up, rwkv_a_bias, rwkv_g_up, rwkv_k_k, rwkv_k_a, rwkv_r_k, rwkv_ln_g, rwkv_ln_b, rwkv_vres_w1, rwkv_vres_w2, rwkv_vres_b, gdn_conv_w, gdn_a_log, gdn_dt_bias, gdn_norm, ffn2_norm, ffn2_wi, ffn2_wo, final_norm, loss_target, m_ffn1_norm, m_ffn1_wi, m_ffn1_wo, m_mix_norm, m_w_in, m_w_out, m_lru_conv_w, m_lru_conv_b, m_lru_gate_a_w, m_lru_gate_a_b, m_lru_gate_x_w, m_lru_gate_x_b, m_lru_lambda, m_lru_out_norm, m_rwkv_mu, m_rwkv_w_up, m_rwkv_w_bias, m_rwkv_a_up, m_rwkv_a_bias, m_rwkv_g_up, m_rwkv_k_k, m_rwkv_k_a, m_rwkv_r_k, m_rwkv_ln_g, m_rwkv_ln_b, m_rwkv_vres_w1, m_rwkv_vres_w2, m_rwkv_vres_b, m_gdn_conv_w, m_gdn_a_log, m_gdn_dt_bias, m_gdn_norm, m_ffn2_norm, m_ffn2_wi, m_ffn2_wo, m_final_norm, v_ffn1_norm, v_ffn1_wi, v_ffn1_wo, v_mix_norm, v_w_in, v_w_out, v_lru_conv_w, v_lru_conv_b, v_lru_gate_a_w, v_lru_gate_a_b, v_lru_gate_x_w, v_lru_gate_x_b, v_lru_lambda, v_lru_out_norm, v_rwkv_mu, v_rwkv_w_up, v_rwkv_w_bias, v_rwkv_a_up, v_rwkv_a_bias, v_rwkv_g_up, v_rwkv_k_k, v_rwkv_k_a, v_rwkv_r_k, v_rwkv_ln_g, v_rwkv_ln_b, v_rwkv_vres_w1, v_rwkv_vres_w2, v_rwkv_vres_b, v_gdn_conv_w, v_gdn_a_log, v_gdn_dt_bias, v_gdn_norm, v_ffn2_norm, v_ffn2_wi, v_ffn2_wo, v_final_norm):
    args = locals()
    w_loc = {n: args[n] for n in WEIGHTS}
    m_loc = {n: args['m_' + n] for n in WEIGHTS}
    v_loc = {n: args['v_' + n] for n in WEIGHTS}
    chip = 2 * lax.axis_index("x") + lax.axis_index("y")
    core = lax.axis_index("c")

    big_names = list(BIG)
    gathered = allgather_chips([_shard_2d(w_loc[n].astype(BF16)) for n in big_names], "allgather_big")
    wb = {n: _from_chips(g, BIG[n]) for n, g in zip(big_names, gathered)}

    sm_names = list(SMALL_SHARDED)
    placed = []
    for n in sm_names:
        ax = SMALL_SHARDED[n]
        full_shape = w_loc[n].shape[:ax] + (N_CHIPS * w_loc[n].shape[ax],) + w_loc[n].shape[ax + 1:]
        src = w_loc[n] * (core == 0).astype(F32)
        placed.append(lax.dynamic_update_slice_in_dim(jnp.zeros(full_shape, F32), src, chip * w_loc[n].shape[ax], ax))
    summed = allreduce_small(_pack_rows(jnp.concatenate([p.reshape(-1) for p in placed]), 128, 8),
                             "allgather_small").reshape(-1)
    w_full, off = dict(w_loc), 0
    for n, p in zip(sm_names, placed):
        w_full[n] = summed[off:off + p.size].reshape(p.shape)
        off += p.size

    loss, dx, grads = local_step(x[0], loss_target[0], w_full, wb)
    loss = lax.psum(loss, ("x", "y", "c"))

    gsum = allreduce_small(_pack_rows(jnp.concatenate([grads[n].reshape(-1) for n in SMALL]), 128, 8),
                           "allreduce_small").reshape(-1)
    g_loc, off = {}, 0
    for n in SMALL:
        g = gsum[off:off + grads[n].size].reshape(grads[n].shape)
        off += grads[n].size
        g_loc[n] = _local_shard(g, SMALL_SHARDED[n], chip) if n in SMALL_SHARDED else g

    parts = [_to_chips(grads[n], BIG[n]) for n in big_names]
    from_sib = sibling_swap(parts, True, "grad_swap_cores")
    core_arg = core.reshape(1).astype(jnp.int32)
    sums = [chip_sum(p, r, core_arg, f"grad_chip_sum_{n}") for n, p, r in zip(big_names, parts, from_sib)]
    from_chips = scatter_chips([s[1] for s in sums], "grad_scatter")
    halves = [shard_sum(lax.dynamic_index_in_dim(s[0], chip, 0, keepdims=False), r, f"grad_shard_sum_{n}")
              for n, s, r in zip(big_names, sums, from_chips)]
    others = sibling_swap(halves, False, "grad_share_cores")
    for n, half, other in zip(big_names, halves, others):
        both = jnp.stack([half, other])
        rows = jnp.concatenate([lax.dynamic_index_in_dim(both, core, 0, keepdims=False),
                                lax.dynamic_index_in_dim(both, 1 - core, 0, keepdims=False)], axis=0)
        g_loc[n] = rows[:, :w_loc[n].shape[-1]].reshape(w_loc[n].shape)

    delta, new_m, new_v = {}, {}, {}
    for n in big_names:
        shp = w_loc[n].shape
        two_d = lambda a: a.reshape(-1, shp[-1])
        d, nm, nv = adamw(two_d(w_loc[n]), two_d(m_loc[n]), two_d(v_loc[n]), two_d(g_loc[n]), f"adamw_{n}")
        delta[n], new_m[n], new_v[n] = d.reshape(shp), nm.reshape(shp), nv.reshape(shp)
    pack = lambda d: _pack_rows(jnp.concatenate([d[n].reshape(-1) for n in SMALL]), 128, 128)
    res = adamw(pack(w_loc), pack(m_loc), pack(v_loc), pack(g_loc), "adamw_small")
    off = 0
    for n in SMALL:
        size = w_loc[n].size
        for dst, r in zip((delta, new_m, new_v), res):
            dst[n] = r.reshape(-1)[off:off + size].reshape(w_loc[n].shape)
        off += size

    return (loss, dx[None], *[g_loc[n] for n in WEIGHTS], *[delta[n] for n in WEIGHTS],
            *[new_m[n] for n in WEIGHTS], *[new_v[n] for n in WEIGHTS])
```

```python
import functools

import numpy as np
import jax
import jax.numpy as jnp
from jax import lax
from jax.experimental import pallas as pl
from jax.experimental.pallas import tpu as pltpu

F32 = jnp.float32
BF16 = jnp.bfloat16
HIGHEST = lax.Precision.HIGHEST
MESH = pl.DeviceIdType.MESH

D_MODEL = 1024
D_FF = 2816
N_LAYERS = 2
HEADS = 6
HEAD_DIM = 64
MIX_W = HEADS * HEAD_DIM
LRU_W = 256
LRU_BLOCKS = 4
RWKV_IN = 1408
D_IN = 3468
D_IN_PAD = 3584
NORM_EPS = 1e-6
GN_EPS = 64e-5
LRU_C = 8.0
CHUNK = 64
ROWS = 256
FF_CHUNK = 256
IN_CHUNK = 512
PACK_W = 1024
VMEM_LIMIT = 56 * 1024 * 1024

ADAM_LR, ADAM_B1, ADAM_B2, ADAM_EPS, ADAM_WD, ADAM_STEP = 0.001, 0.9, 0.999, 1e-08, 0.01, 10

WEIGHTS = ['ffn1_norm', 'ffn1_wi', 'ffn1_wo', 'mix_norm', 'w_in', 'w_out', 'lru_conv_w', 'lru_conv_b',
           'lru_gate_a_w', 'lru_gate_a_b', 'lru_gate_x_w', 'lru_gate_x_b', 'lru_lambda', 'lru_out_norm',
           'rwkv_mu', 'rwkv_w_up', 'rwkv_w_bias', 'rwkv_a_up', 'rwkv_a_bias', 'rwkv_g_up', 'rwkv_k_k',
           'rwkv_k_a', 'rwkv_r_k', 'rwkv_ln_g', 'rwkv_ln_b', 'rwkv_vres_w1', 'rwkv_vres_w2', 'rwkv_vres_b',
           'gdn_conv_w', 'gdn_a_log', 'gdn_dt_bias', 'gdn_norm', 'ffn2_norm', 'ffn2_wi', 'ffn2_wo', 'final_norm']
BIG = {'ffn1_wi': 2, 'ffn1_wo': 1, 'w_in': 2, 'w_out': 1, 'ffn2_wi': 2, 'ffn2_wo': 1}
SMALL_SHARDED = {'lru_conv_w': 2, 'rwkv_w_up': 2, 'rwkv_a_up': 2, 'rwkv_g_up': 2, 'rwkv_vres_w1': 1,
                 'rwkv_vres_w2': 2, 'gdn_conv_w': 2}
N_CHIPS = 4


def _params(sem=None):
    kw = dict(vmem_limit_bytes=VMEM_LIMIT)
    if sem is not None:
        kw['dimension_semantics'] = sem
    return pltpu.CompilerParams(**kw)


def _bdot(a, b, dims=(((1,), (0,)), ((), ()))):
    return lax.dot_general(a.astype(BF16), b.astype(BF16), dims, preferred_element_type=F32)


def _bdot_nt(a, b):
    return _bdot(a, b, (((1,), (1,)), ((), ())))


def _bdot_tn(a, b):
    return _bdot(a, b, (((0,), (0,)), ((), ())))


_DIMS = {'nn': (((1,), (0,)), ((), ())), 'nt': (((1,), (1,)), ((), ())), 'tn': (((0,), (0,)), ((), ()))}


def _split(a, terms):
    parts = []
    for _ in range(terms - 1):
        hi = a.astype(BF16)
        parts.append(hi)
        a = a - hi.astype(F32)
    parts.append(a.astype(BF16))
    return parts


_BATCH_DIMS = {'nn': (((2,), (1,)), ((0,), (0,))), 'nt': (((2,), (2,)), ((0,), (0,))),
               'tn': (((1,), (1,)), ((0,), (0,)))}


def _dot3(a, b, kind):
    ah, al = _split(a, 2)
    bh, bl = _split(b, 2)
    dims = _BATCH_DIMS[kind] if a.ndim == 3 else _DIMS[kind]
    d = lambda p, q: lax.dot_general(p, q, dims, preferred_element_type=F32)
    return d(ah, bh) + (d(ah, bl) + d(al, bh))


@functools.partial(jax.custom_vjp, nondiff_argnums=(2,))
def _cdot_k(a, b, kind):
    return _dot3(a, b, kind)


def _cdot_k_fwd(a, b, kind):
    return _dot3(a, b, kind), (a, b)


def _cdot_k_bwd(kind, res, ct):
    a, b = res
    if kind == 'nn':
        return _dot3(ct, b, 'nt'), _dot3(a, ct, 'tn')
    if kind == 'nt':
        return _dot3(ct, b, 'nn'), _dot3(ct, a, 'tn')
    return _dot3(b, ct, 'nt'), _dot3(a, ct, 'nn')


_cdot_k.defvjp(_cdot_k_fwd, _cdot_k_bwd)


def _cdot(a, b):
    return _cdot_k(a, b, 'nn')


def _cdot_nt(a, b):
    return _cdot_k(a, b, 'nt')


def _cdot_tn(a, b):
    return _cdot_k(a, b, 'tn')


def _hdot(a, b):
    return _cdot_k(a, b, 'nn')


def _dot_exact(x, m01, kind):
    d = lambda p: lax.dot_general(p, m01.astype(BF16), _DIMS[kind], preferred_element_type=F32)
    hi, mid, lo = _split(x, 3)
    return d(hi) + (d(mid) + d(lo))


@functools.partial(jax.custom_vjp, nondiff_argnums=(1,))
def _xdot(x, make_m):
    return _dot_exact(x, make_m(), 'nn')


def _xdot_fwd(x, make_m):
    return _dot_exact(x, make_m(), 'nn'), None


def _xdot_bwd(make_m, _, ct):
    return (_dot_exact(ct, make_m(), 'nt'),)


_xdot.defvjp(_xdot_fwd, _xdot_bwd)


def _iota2(n, m):
    return lax.broadcasted_iota(jnp.int32, (n, m), 0), lax.broadcasted_iota(jnp.int32, (n, m), 1)


def _head_blocks(w):
    ri, ci = _iota2(w, w)
    return (ri // HEAD_DIM == ci // HEAD_DIM).astype(F32)


def _segsum(x):
    return _xdot(x, functools.partial(_head_blocks, x.shape[-1]))


def _cumsum_rows(x):
    return _cumsum_k(x, x.shape[0])


@functools.partial(jax.custom_vjp, nondiff_argnums=(1,))
def _cumsum_k(x, n):
    return _lower_dot(x, n, False)


def _lower_dot(x, n, transpose):
    ri, ci = _iota2(n, n)
    m = ((ri <= ci) if transpose else (ri >= ci)).astype(BF16)
    d = lambda p: lax.dot_general(m, p, _DIMS['nn'], preferred_element_type=F32)
    hi, mid, lo = _split(x, 3)
    return d(hi) + (d(mid) + d(lo))


def _cumsum_k_fwd(x, n):
    return _lower_dot(x, n, False), None


def _cumsum_k_bwd(n, _, ct):
    return (_lower_dot(ct, n, True),)


_cumsum_k.defvjp(_cumsum_k_fwd, _cumsum_k_bwd)


def _rms(x, g):
    return x * lax.rsqrt(jnp.mean(x * x, axis=-1, keepdims=True) + NORM_EPS) * g


DENSE_ROWS = 1024


def _row_loop(n_rows, fn):
    rows = min(DENSE_ROWS, n_rows)

    def step(i, c):
        fn(pl.ds(pl.multiple_of(i * rows, rows), rows))
        return c
    lax.fori_loop(0, n_rows // rows, step, 0)


def ffn_fwd(x, g, wi, wo, name):
    T = x.shape[0]
    nj = D_FF // FF_CHUNK

    def body(x_ref, g_ref, wg_ref, wu_ref, wo_ref, o_ref, h_ref, acc_ref):
        j = pl.program_id(0)

        @pl.when(j == 0)
        def _():
            def init(r):
                h_ref[r, :] = _rms(x_ref[r, :], g_ref[...]).astype(BF16)
                acc_ref[r, :] = jnp.zeros((r.size, D_MODEL), F32)
            _row_loop(T, init)

        def blk(r):
            hb = h_ref[r, :]
            gate = jnp.dot(hb, wg_ref[...], preferred_element_type=F32)
            up = jnp.dot(hb, wu_ref[...], preferred_element_type=F32)
            a = (gate * jax.nn.sigmoid(gate) * up).astype(BF16)
            acc_ref[r, :] += jnp.dot(a, wo_ref[...], preferred_element_type=F32)
        _row_loop(T, blk)

        @pl.when(j == nj - 1)
        def _():
            def fin(r):
                o_ref[r, :] = x_ref[r, :] + 0.5 * acc_ref[r, :]
            _row_loop(T, fin)

    full = pl.BlockSpec((T, D_MODEL), lambda j: (0, 0))
    return pl.pallas_call(
        body, name=name, grid=(nj,),
        in_specs=[full, pl.BlockSpec((1, D_MODEL), lambda j: (0, 0)),
                  pl.BlockSpec((D_MODEL, FF_CHUNK), lambda j: (0, j)),
                  pl.BlockSpec((D_MODEL, FF_CHUNK), lambda j: (0, j + nj)),
                  pl.BlockSpec((FF_CHUNK, D_MODEL), lambda j: (j, 0))],
        out_specs=full, out_shape=jax.ShapeDtypeStruct((T, D_MODEL), F32),
        scratch_shapes=[pltpu.VMEM((T, D_MODEL), BF16), pltpu.VMEM((T, D_MODEL), F32)],
        compiler_params=_params(("arbitrary",)))(x, g, wi, wi, wo)


def _norm_bwd_rows(x, g, dh, dres):
    rstd = lax.rsqrt(jnp.mean(x * x, axis=-1, keepdims=True) + NORM_EPS)
    xh = x * rstd
    dxh = dh * g
    dx = rstd * (dxh - xh * jnp.mean(dxh * xh, axis=-1, keepdims=True))
    return dres + dx, jnp.sum(dh * xh, axis=0, keepdims=True)


def ffn_bwd(x, dy, g, wi, wo, name):
    T = x.shape[0]
    nj = D_FF // FF_CHUNK

    def body(x_ref, dy_ref, g_ref, wg_ref, wu_ref, wo_ref, dx_ref, dg_ref, dwg_ref, dwu_ref, dwo_ref,
             h_ref, da_ref, dh_ref):
        j = pl.program_id(0)

        @pl.when(j == 0)
        def _():
            def init(r):
                h_ref[r, :] = _rms(x_ref[r, :], g_ref[...]).astype(BF16)
                da_ref[r, :] = (0.5 * dy_ref[r, :]).astype(BF16)
                dh_ref[r, :] = jnp.zeros((r.size, D_MODEL), F32)
            _row_loop(T, init)

        dwg_ref[...] = jnp.zeros_like(dwg_ref)
        dwu_ref[...] = jnp.zeros_like(dwu_ref)
        dwo_ref[...] = jnp.zeros_like(dwo_ref)

        def blk(r):
            hb = h_ref[r, :]
            db = da_ref[r, :]
            gate = jnp.dot(hb, wg_ref[...], preferred_element_type=F32)
            up = jnp.dot(hb, wu_ref[...], preferred_element_type=F32)
            sg = jax.nn.sigmoid(gate)
            sl = gate * sg
            da = _bdot_nt(db, wo_ref[...])
            dup = (da * sl).astype(BF16)
            dgate = (da * up * (sg * (1.0 + gate * (1.0 - sg)))).astype(BF16)
            dwo_ref[...] += _bdot_tn((sl * up).astype(BF16), db)
            dwg_ref[...] += _bdot_tn(hb, dgate)
            dwu_ref[...] += _bdot_tn(hb, dup)
            dh_ref[r, :] += _bdot_nt(dgate, wg_ref[...]) + _bdot_nt(dup, wu_ref[...])
        _row_loop(T, blk)

        @pl.when(j == nj - 1)
        def _():
            dg_ref[...] = jnp.zeros_like(dg_ref)

            def fin(r):
                dx, dg = _norm_bwd_rows(x_ref[r, :], g_ref[...], dh_ref[r, :], dy_ref[r, :])
                dx_ref[r, :] = dx
                dg_ref[...] += dg
            _row_loop(T, fin)

    full = pl.BlockSpec((T, D_MODEL), lambda j: (0, 0))
    vec = pl.BlockSpec((1, D_MODEL), lambda j: (0, 0))
    return pl.pallas_call(
        body, name=name, grid=(nj,),
        in_specs=[full, full, vec,
                  pl.BlockSpec((D_MODEL, FF_CHUNK), lambda j: (0, j)),
                  pl.BlockSpec((D_MODEL, FF_CHUNK), lambda j: (0, j + nj)),
                  pl.BlockSpec((FF_CHUNK, D_MODEL), lambda j: (j, 0))],
        out_specs=(full, vec,
                   pl.BlockSpec((D_MODEL, FF_CHUNK), lambda j: (0, j)),
                   pl.BlockSpec((D_MODEL, FF_CHUNK), lambda j: (0, j)),
                   pl.BlockSpec((FF_CHUNK, D_MODEL), lambda j: (j, 0))),
        out_shape=(jax.ShapeDtypeStruct((T, D_MODEL), F32), jax.ShapeDtypeStruct((1, D_MODEL), F32),
                   jax.ShapeDtypeStruct((D_MODEL, D_FF), F32), jax.ShapeDtypeStruct((D_MODEL, D_FF), F32),
                   jax.ShapeDtypeStruct((D_FF, D_MODEL), F32)),
        scratch_shapes=[pltpu.VMEM((T, D_MODEL), BF16), pltpu.VMEM((T, D_MODEL), BF16),
                        pltpu.VMEM((T, D_MODEL), F32)],
        compiler_params=_params(("arbitrary",)))(x, dy, g, wi, wi, wo)


def proj_fwd(x, g, w, name):
    T = x.shape[0]
    nj = D_IN_PAD // IN_CHUNK

    def body(x_ref, g_ref, w_ref, o_ref, h_ref):
        @pl.when(pl.program_id(0) == 0)
        def _():
            def init(r):
                h_ref[r, :] = _rms(x_ref[r, :], g_ref[...]).astype(BF16)
            _row_loop(T, init)

        def blk(r):
            o_ref[r, :] = jnp.dot(h_ref[r, :], w_ref[...], preferred_element_type=F32)
        _row_loop(T, blk)

    return pl.pallas_call(
        body, name=name, grid=(nj,),
        in_specs=[pl.BlockSpec((T, D_MODEL), lambda j: (0, 0)), pl.BlockSpec((1, D_MODEL), lambda j: (0, 0)),
                  pl.BlockSpec((D_MODEL, IN_CHUNK), lambda j: (0, j))],
        out_specs=pl.BlockSpec((T, IN_CHUNK), lambda j: (0, j)),
        out_shape=jax.ShapeDtypeStruct((T, D_IN_PAD), F32),
        scratch_shapes=[pltpu.VMEM((T, D_MODEL), BF16)],
        compiler_params=_params(("arbitrary",)))(x, g, w)


def proj_bwd(x, dres, g, w, dp, name):
    T = x.shape[0]
    nj = D_IN_PAD // IN_CHUNK

    def body(x_ref, dres_ref, g_ref, w_ref, dp_ref, dx_ref, dg_ref, dw_ref, h_ref, dh_ref):
        j = pl.program_id(0)

        @pl.when(j == 0)
        def _():
            def init(r):
                h_ref[r, :] = _rms(x_ref[r, :], g_ref[...]).astype(BF16)
                dh_ref[r, :] = jnp.zeros((r.size, D_MODEL), F32)
            _row_loop(T, init)

        dw_ref[...] = jnp.zeros_like(dw_ref)

        def blk(r):
            dpb = dp_ref[r, :].astype(BF16)
            dw_ref[...] += _bdot_tn(h_ref[r, :], dpb)
            dh_ref[r, :] += _bdot_nt(dpb, w_ref[...])
        _row_loop(T, blk)

        @pl.when(j == nj - 1)
        def _():
            dg_ref[...] = jnp.zeros_like(dg_ref)

            def fin(r):
                dx, dg = _norm_bwd_rows(x_ref[r, :], g_ref[...], dh_ref[r, :], dres_ref[r, :])
                dx_ref[r, :] = dx
                dg_ref[...] += dg
            _row_loop(T, fin)

    full = pl.BlockSpec((T, D_MODEL), lambda j: (0, 0))
    vec = pl.BlockSpec((1, D_MODEL), lambda j: (0, 0))
    return pl.pallas_call(
        body, name=name, grid=(nj,),
        in_specs=[full, full, vec, pl.BlockSpec((D_MODEL, IN_CHUNK), lambda j: (0, j)),
                  pl.BlockSpec((T, IN_CHUNK), lambda j: (0, j))],
        out_specs=(full, vec, pl.BlockSpec((D_MODEL, IN_CHUNK), lambda j: (0, j))),
        out_shape=(jax.ShapeDtypeStruct((T, D_MODEL), F32), jax.ShapeDtypeStruct((1, D_MODEL), F32),
                   jax.ShapeDtypeStruct((D_MODEL, D_IN_PAD), F32)),
        scratch_shapes=[pltpu.VMEM((T, D_MODEL), BF16), pltpu.VMEM((T, D_MODEL), F32)],
        compiler_params=_params(("arbitrary",)))(x, dres, g, w, dp)


def out_fwd(mixed, w, x, name):
    T = x.shape[0]

    def body(m_ref, w_ref, x_ref, o_ref):
        o_ref[...] = x_ref[...] + jnp.dot(m_ref[...].astype(BF16), w_ref[...], preferred_element_type=F32)

    blk = pl.BlockSpec((ROWS, D_MODEL), lambda i: (i, 0))
    return pl.pallas_call(
        body, name=name, grid=(T // ROWS,),
        in_specs=[blk, pl.BlockSpec((D_MODEL, D_MODEL), lambda i: (0, 0)), blk],
        out_specs=blk, out_shape=jax.ShapeDtypeStruct((T, D_MODEL), F32),
        compiler_params=_params(("arbitrary",)))(mixed, w, x)


def out_bwd(mixed, w, dy, name):
    T = dy.shape[0]

    def body(m_ref, w_ref, dy_ref, dm_ref, dw_ref):
        @pl.when(pl.program_id(0) == 0)
        def _():
            dw_ref[...] = jnp.zeros_like(dw_ref)
        dyb = dy_ref[...].astype(BF16)
        dm_ref[...] = _bdot_nt(dyb, w_ref[...])
        dw_ref[...] += _bdot_tn(m_ref[...].astype(BF16), dyb)

    blk = pl.BlockSpec((ROWS, D_MODEL), lambda i: (i, 0))
    sq = pl.BlockSpec((D_MODEL, D_MODEL), lambda i: (0, 0))
    return pl.pallas_call(
        body, name=name, grid=(T // ROWS,),
        in_specs=[blk, sq, blk], out_specs=(blk, sq),
        out_shape=(jax.ShapeDtypeStruct((T, D_MODEL), F32), jax.ShapeDtypeStruct((D_MODEL, D_MODEL), F32)),
        compiler_params=_params(("arbitrary",)))(mixed, w, dy)


def loss_head(x, g, target, name):
    T = x.shape[0]

    def body(x_ref, g_ref, t_ref, loss_ref, dx_ref, dg_ref):
        @pl.when(pl.program_id(0) == 0)
        def _():
            loss_ref[...] = jnp.zeros_like(loss_ref)
            dg_ref[...] = jnp.zeros_like(dg_ref)
        xb = x_ref[...]
        rstd = lax.rsqrt(jnp.mean(xb * xb, axis=-1, keepdims=True) + NORM_EPS)
        xh = xb * rstd
        err = xh * g_ref[...] - t_ref[...]
        loss_ref[...] += 0.5 * jnp.sum(jnp.mean(err * err, axis=-1, keepdims=True), axis=0, keepdims=True)
        dy = err * (1.0 / D_MODEL)
        dg_ref[...] += jnp.sum(dy * xh, axis=0, keepdims=True)
        dxh = dy * g_ref[...]
        dx_ref[...] = rstd * (dxh - xh * jnp.mean(dxh * xh, axis=-1, keepdims=True))

    blk = pl.BlockSpec((ROWS, D_MODEL), lambda i: (i, 0))
    vec = pl.BlockSpec((1, D_MODEL), lambda i: (0, 0))
    return pl.pallas_call(
        body, name=name, grid=(T // ROWS,),
        in_specs=[blk, vec, blk], out_specs=(pl.BlockSpec((1, 1), lambda i: (0, 0)), blk, vec),
        out_shape=(jax.ShapeDtypeStruct((1, 1), F32), jax.ShapeDtypeStruct((T, D_MODEL), F32),
                   jax.ShapeDtypeStruct((1, D_MODEL), F32)),
        compiler_params=_params(("arbitrary",)))(x, g, target)


def rowwise_fwd(fn, rows, shared, out_widths, name):
    T = rows[0].shape[0]
    n_in = len(rows) + len(shared)

    def body(*refs):
        res = fn(*[r[...] for r in refs[:n_in]])
        for o, v in zip(refs[n_in:], res):
            o[...] = v

    in_specs = ([pl.BlockSpec((ROWS, a.shape[1]), lambda i: (i, 0)) for a in rows]
                + [pl.BlockSpec(a.shape, lambda i: (0, 0)) for a in shared])
    return pl.pallas_call(
        body, name=name, grid=(T // ROWS,), in_specs=in_specs,
        out_specs=tuple(pl.BlockSpec((ROWS, w), lambda i: (i, 0)) for w in out_widths),
        out_shape=tuple(jax.ShapeDtypeStruct((T, w), F32) for w in out_widths),
        compiler_params=_params(("arbitrary",)))(*rows, *shared)


def rowwise_bwd(fn, rows, shared, cts, name, ct_fn=None):
    T = rows[0].shape[0]
    nr, ns, nc = len(rows), len(shared), len(cts)

    def body(*refs):
        ins = [r[...] for r in refs[:nr + ns]]
        ctv = tuple(r[...] for r in refs[nr + ns:nr + ns + nc])
        outs = refs[nr + ns + nc:]
        _, vjp = jax.vjp(fn, *ins)
        grads = vjp(ct_fn(*ctv) if ct_fn is not None else ctv)
        for k in range(nr):
            outs[k][...] = grads[k]

        @pl.when(pl.program_id(0) == 0)
        def _():
            for k in range(ns):
                outs[nr + k][...] = jnp.zeros_like(outs[nr + k])
        for k in range(ns):
            outs[nr + k][...] += grads[nr + k]

    row_spec = lambda a: pl.BlockSpec((ROWS, a.shape[1]), lambda i: (i, 0))
    sh_spec = lambda a: pl.BlockSpec(a.shape, lambda i: (0, 0))
    return pl.pallas_call(
        body, name=name, grid=(T // ROWS,),
        in_specs=[row_spec(a) for a in rows] + [sh_spec(a) for a in shared] + [row_spec(a) for a in cts],
        out_specs=tuple([row_spec(a) for a in rows] + [sh_spec(a) for a in shared]),
        out_shape=tuple(jax.ShapeDtypeStruct(a.shape, F32) for a in list(rows) + list(shared)),
        compiler_params=_params(("arbitrary",)))(*rows, *shared, *cts)


def shift_rows(x, s):
    return jnp.pad(x, ((s, 0), (0, 0)))[:x.shape[0]]


def unshift_rows(x, s):
    return jnp.pad(x, ((0, s), (0, 0)))[s:]


def _neg_expm1(y):
    series = -(y * (1.0 + y * (0.5 + y * (1.0 / 6.0 + y * (1.0 / 24.0)))))
    return jnp.where(y > -0.05, series, 1.0 - jnp.exp(y))


def lru_pre_fn(x0, x1, x2, x3, first, w0, w1, w2, w3, cb, ga, gab, gx, gxb, lam):
    xc = w3 * x0 + w2 * x1 + w1 * x2 + w0 * x3 + cb
    r = jax.nn.sigmoid(_hdot(xc, ga) + gab)
    i = jax.nn.sigmoid(_hdot(xc, gx) + gxb)
    log_a = -LRU_C * r * jax.nn.softplus(-lam)
    a = jnp.exp(log_a)
    mult = jnp.where(first > 0.5, 1.0, jnp.sqrt(_neg_expm1(2.0 * log_a)))
    return a, mult * i * xc


def lru_post_fn(h, py, og):
    return (_rms(h * jax.nn.gelu(py), og),)


def lru_scan(a, b, reverse, name):
    T, C = a.shape
    nb = T // 8

    def body(a_ref, b_ref, h_ref):
        rows = lax.broadcasted_iota(jnp.int32, (8, C), 0)

        def blk(i, carry):
            j = nb - 1 - i if reverse else i
            r = pl.ds(pl.multiple_of(j * 8, 8), 8)
            A = a_ref[r, :]
            B = b_ref[r, :]
            for s in (1, 2, 4):
                if reverse:
                    keep = rows < 8 - s
                    sh = 8 - s
                else:
                    keep = rows >= s
                    sh = s
                Bs = jnp.where(keep, pltpu.roll(B, sh, 0), 0.0)
                As = jnp.where(keep, pltpu.roll(A, sh, 0), 1.0)
                B = B + A * Bs
                A = A * As
            hb = B + A * carry
            h_ref[r, :] = hb
            edge = 0 if reverse else 7
            return jnp.sum(jnp.where(rows == edge, hb, 0.0), axis=0, keepdims=True)

        lax.fori_loop(0, nb, blk, jnp.zeros((1, C), F32))

    full = pl.BlockSpec((T, C), lambda: (0, 0))
    return pl.pallas_call(body, name=name, in_specs=[full, full], out_specs=full,
                          out_shape=jax.ShapeDtypeStruct((T, C), F32), compiler_params=_params())(a, b)


def make_rwkv_pre_fn(has_vres):
    def fn(p, pp, *rest):
        if has_vres:
            vf, mu, w_up, w_b, a_up, a_b, g_up, kk_w, ka_w, vw1, vw2, vb = rest
        else:
            mu, w_up, w_b, a_up, a_b, g_up, kk_w, ka_w = rest
        xm = p + (pp - p) * mu
        r, k, v = xm[:, 0:384], xm[:, 384:768], xm[:, 768:1152]
        xw, xa, xg = xm[:, 1152:1216], xm[:, 1216:1280], xm[:, 1280:1408]
        w_log = -jax.nn.softplus(-(w_b + _hdot(jnp.tanh(xw), w_up))) - 0.5
        lw = -jnp.exp(w_log)
        a = jax.nn.sigmoid(a_b + _hdot(xa, a_up))
        g = _hdot(jax.nn.sigmoid(xg), g_up)
        if has_vres:
            v = v + (vf - v) * jax.nn.sigmoid(vb + _hdot(_hdot(v, vw1), vw2))
        kkx = k * kk_w
        kk = kkx * lax.rsqrt(_segsum(kkx * kkx) + 1e-6)
        k2 = k * (1.0 + (a - 1.0) * ka_w)
        return r, lw, k2, v, kk, a, g
    return fn


def rwkv_post_fn(y, r, k2, v, g, ln_g, ln_b, r_k):
    mean = _segsum(y) * (1.0 / HEAD_DIM)
    yc = y - mean
    var = _segsum(yc * yc) * (1.0 / HEAD_DIM)
    yn = yc * lax.rsqrt(var + GN_EPS) * ln_g + ln_b
    bonus = _segsum(r * k2 * r_k) * v
    return ((yn + bonus) * g,)


def _head_expander(first_lane):
    ri, ci = _iota2(128, MIX_W)
    return (ri == ci // HEAD_DIM + first_lane).astype(F32)


def gdn_pre_fn(x0, x1, x2, x3, ab, w0, w1, w2, w3, alog, dtb):
    qkv = jax.nn.silu(w3 * x0 + w2 * x1 + w1 * x2 + w0 * x3)
    q, k, v = qkv[:, 0:384], qkv[:, 384:768], qkv[:, 768:1152]
    q = q * lax.rsqrt(_segsum(q * q) + 1e-6) * (HEAD_DIM ** -0.5)
    k = k * lax.rsqrt(_segsum(k * k) + 1e-6)
    g = -jnp.exp(alog) * jax.nn.softplus(ab + dtb)
    beta = jax.nn.sigmoid(ab)
    ge = _xdot(g, functools.partial(_head_expander, 0))
    be = _xdot(beta, functools.partial(_head_expander, HEADS))
    return q, k, v, ge, be


def gdn_post_fn(o, z, ng):
    ms = _segsum(o * o) * (1.0 / HEAD_DIM)
    return (o * lax.rsqrt(ms + NORM_EPS) * ng * jax.nn.silu(z),)


def _neumann_inv(m):
    n = m.shape[-1]
    ri, ci = _iota2(n, n)
    eye = (ri == ci).astype(F32)
    md = jnp.where(ri // 16 == ci // 16, m, 0.0)
    mo = m - md
    t0 = eye + md
    p2 = _hdot(md, md)
    t0 = t0 + _hdot(t0, p2)
    p4 = _hdot(p2, p2)
    t0 = t0 + _hdot(t0, p4)
    p8 = _hdot(p4, p4)
    t0 = t0 + _hdot(t0, p8)
    nn = _hdot(t0, mo)
    n2 = _hdot(nn, nn)
    t1 = eye + nn + n2 + _hdot(nn, n2)
    return _hdot(t1, t0)


@jax.custom_vjp
def _inv_saved(m, t_saved):
    return t_saved


def _inv_saved_fwd(m, t_saved):
    return t_saved, t_saved


def _inv_saved_bwd(t_saved, dt):
    tt = jnp.swapaxes(t_saved, -1, -2)
    return _hdot(_hdot(tt, dt), tt), jnp.zeros_like(t_saved)


_inv_saved.defvjp(_inv_saved_fwd, _inv_saved_bwd)


def _heads(x):
    return jnp.stack([x[:, h * HEAD_DIM:(h + 1) * HEAD_DIM] for h in range(HEADS)], axis=0)


def _unheads(y):
    return jnp.concatenate([y[h] for h in range(HEADS)], axis=1)


def rwkv_heads(s0, r, lw, k2, v, kk, a, inv):
    n = r.shape[0]
    ri, ci = _iota2(n, n)
    low, strict = ri >= ci, ri > ci
    cs = _cumsum_rows(lw)
    cl = jnp.sum(lw, axis=0, keepdims=True)
    p_in, p_prev, p_inv = jnp.exp(cs), jnp.exp(cs - lw), jnp.exp(-cs)
    p_rest, p_all = jnp.exp(cl - cs), jnp.exp(cl)
    bd = kk * a
    at, rt = _heads(-kk * p_prev), _heads(r * p_in)
    bh, kh = _heads(bd * p_inv), _heads(k2 * p_inv)
    vh = _heads(v)
    m_ab = jnp.where(strict, _cdot_nt(at, bh), 0.0)
    m_ak = jnp.where(strict, _cdot_nt(at, kh), 0.0)
    m_rb = jnp.where(low, _cdot_nt(rt, bh), 0.0)
    m_rk = jnp.where(low, _cdot_nt(rt, kh), 0.0)
    sa = _cdot(inv(m_ab), _cdot_nt(at, s0) + _cdot(m_ak, vh))
    y = _cdot_nt(rt, s0) + _cdot(m_rb, sa) + _cdot(m_rk, vh)
    s1 = s0 * _heads(p_all) + _cdot_tn(sa, _heads(bd * p_rest)) + _cdot_tn(vh, _heads(k2 * p_rest))
    return _unheads(y), s1


def gdn_heads(s0, q, k, v, ge, be, inv):
    n = q.shape[0]
    ri, ci = _iota2(n, n)
    low, strict = ri >= ci, ri > ci
    gc = _cumsum_rows(ge)
    gl = jnp.sum(ge, axis=0, keepdims=True)
    gch = _heads(gc)
    decay = jnp.where(low, jnp.exp(jnp.where(low, gch - jnp.swapaxes(gch, 1, 2), 0.0)), 0.0)
    kb = k * be
    e = jnp.exp(gc)
    kh = _heads(k)
    m = -jnp.where(strict, _cdot_nt(_heads(kb), kh) * decay, 0.0)
    mr = jnp.where(low, _cdot_nt(_heads(q), kh) * decay, 0.0)
    u = _cdot(inv(m), _heads(v * be) - _cdot_nt(_heads(kb * e), s0))
    y = _cdot_nt(_heads(q * e), s0) + _cdot(mr, u)
    s1 = s0 * _heads(jnp.exp(gl)) + _cdot_tn(u, _heads(k * jnp.exp(gl - gc)))
    return _unheads(y), s1


def core_fwd(heads_fn, ins, name):
    T = ins[0].shape[0]
    nc = T // CHUNK
    n = len(ins)

    def body(*refs):
        y_ref, s0_ref, t_ref, s_ref = refs[n:n + 4]

        @pl.when(pl.program_id(0) == 0)
        def _():
            s_ref[...] = jnp.zeros_like(s_ref)

        s0 = s_ref[...]
        kept = []

        def inv(m):
            kept.append(_neumann_inv(m))
            return kept[0]

        y, s1 = heads_fn(s0, *[r[...] for r in refs[:n]], inv)
        y_ref[...] = y
        s0_ref[0] = s0
        t_ref[0] = kept[0]
        s_ref[...] = s1

    row = pl.BlockSpec((CHUNK, MIX_W), lambda c: (c, 0))
    st_shape = (HEADS, HEAD_DIM, HEAD_DIM)
    st = pl.BlockSpec((1,) + st_shape, lambda c: (c, 0, 0, 0))
    return pl.pallas_call(
        body, name=name, grid=(nc,), in_specs=[row] * n, out_specs=(row, st, st),
        out_shape=(jax.ShapeDtypeStruct((T, MIX_W), F32), jax.ShapeDtypeStruct((nc,) + st_shape, F32),
                   jax.ShapeDtypeStruct((nc,) + st_shape, F32)),
        scratch_shapes=[pltpu.VMEM(st_shape, F32)],
        compiler_params=_params(("arbitrary",)))(*ins)


def core_bwd(heads_fn, ins, s0_all, t_all, dy, name):
    T = ins[0].shape[0]
    nc = T // CHUNK
    n = len(ins)

    def body(*refs):
        s0_ref, t_ref, dy_ref = refs[n:n + 3]
        outs = refs[n + 3:n + 3 + n]
        ds_ref = refs[n + 3 + n]

        @pl.when(pl.program_id(0) == 0)
        def _():
            ds_ref[...] = jnp.zeros_like(ds_ref)

        t_saved = t_ref[0]
        f = lambda s0, *xs: heads_fn(s0, *xs, lambda m: _inv_saved(m, t_saved))
        _, vjp = jax.vjp(f, s0_ref[0], *[r[...] for r in refs[:n]])
        grads = vjp((dy_ref[...], ds_ref[...]))
        ds_ref[...] = grads[0]
        for k in range(n):
            outs[k][...] = grads[1 + k]

    row = pl.BlockSpec((CHUNK, MIX_W), lambda c: (nc - 1 - c, 0))
    st_shape = (HEADS, HEAD_DIM, HEAD_DIM)
    st = pl.BlockSpec((1,) + st_shape, lambda c: (nc - 1 - c, 0, 0, 0))
    return pl.pallas_call(
        body, name=name, grid=(nc,), in_specs=[row] * n + [st, st, row], out_specs=tuple([row] * n),
        out_shape=tuple(jax.ShapeDtypeStruct((T, MIX_W), F32) for _ in range(n)),
        scratch_shapes=[pltpu.VMEM(st_shape, F32)],
        compiler_params=_params(("arbitrary",)))(*ins, s0_all, t_all, dy)


def _block_diag(w):
    out = jnp.zeros((LRU_W, LRU_W), w.dtype)
    for n in range(LRU_BLOCKS):
        out = lax.dynamic_update_slice(out, w[n], (n * 64, n * 64))
    return out


def _block_diag_grad(g):
    return jnp.stack([g[n * 64:(n + 1) * 64, n * 64:(n + 1) * 64] for n in range(LRU_BLOCKS)])


def _row(v):
    return v.reshape(1, -1)


def _pad128(v):
    return jnp.pad(v.reshape(1, -1), ((0, 0), (0, 128 - v.size)))


def _layer_shared(w, l):
    cw = w['lru_conv_w'][l]
    lru_pre = [_row(cw[0]), _row(cw[1]), _row(cw[2]), _row(cw[3]), _row(w['lru_conv_b'][l]),
               _block_diag(w['lru_gate_a_w'][l]), _row(w['lru_gate_a_b'][l]),
               _block_diag(w['lru_gate_x_w'][l]), _row(w['lru_gate_x_b'][l]), _row(w['lru_lambda'][l])]
    rw_pre = [_row(w['rwkv_mu'][l]), w['rwkv_w_up'][l], _row(w['rwkv_w_bias'][l]), w['rwkv_a_up'][l],
              _row(w['rwkv_a_bias'][l]), w['rwkv_g_up'][l], _row(w['rwkv_k_k'][l]), _row(w['rwkv_k_a'][l])]
    if l > 0:
        rw_pre += [w['rwkv_vres_w1'][l - 1], w['rwkv_vres_w2'][l - 1], _row(w['rwkv_vres_b'][l - 1])]
    rw_post = [_row(w['rwkv_ln_g'][l]), _row(w['rwkv_ln_b'][l]), _row(w['rwkv_r_k'][l])]
    gw = w['gdn_conv_w'][l]
    gdn_pre = [_row(gw[0]), _row(gw[1]), _row(gw[2]), _row(gw[3]), _pad128(w['gdn_a_log'][l]),
               _pad128(w['gdn_dt_bias'][l])]
    gdn_post = [_row(jnp.tile(w['gdn_norm'][l], HEADS))]
    return dict(lru_pre=lru_pre, lru_post=[_row(w['lru_out_norm'][l])], rw_pre=rw_pre, rw_post=rw_post,
                gdn_pre=gdn_pre, gdn_post=gdn_post)


def _mixer_fwd(p, sh, l, v_first):
    T = p.shape[0]
    lx, ly = p[:, 0:256], p[:, 256:512]
    prw, qkv, z, ab = p[:, 512:1920], p[:, 1920:3072], p[:, 3072:3456], p[:, 3456:3584]
    first = jnp.zeros((T, LRU_W), F32).at[0].set(1.0)
    lru_rows = [lx, shift_rows(lx, 1), shift_rows(lx, 2), shift_rows(lx, 3), first]
    a, b = rowwise_fwd(lru_pre_fn, lru_rows, sh['lru_pre'], (LRU_W, LRU_W), f"lru_pre_fwd{l}")
    hseq = lru_scan(a, b, False, f"lru_scan_fwd{l}")
    (y_lru,) = rowwise_fwd(lru_post_fn, [hseq, ly], sh['lru_post'], (LRU_W,), f"lru_post_fwd{l}")

    rw_rows = [prw, shift_rows(prw, 1)] + ([v_first] if l > 0 else [])
    rw = rowwise_fwd(make_rwkv_pre_fn(l > 0), rw_rows, sh['rw_pre'], (MIX_W,) * 7, f"rwkv_pre_fwd{l}")
    r, lw, k2, v, kk, ar, g = rw
    y_raw, rs0, rt = core_fwd(rwkv_heads, [r, lw, k2, v, kk, ar], f"rwkv_core_fwd{l}")
    (y_rw,) = rowwise_fwd(rwkv_post_fn, [y_raw, r, k2, v, g], sh['rw_post'], (MIX_W,), f"rwkv_post_fwd{l}")

    gdn_rows = [qkv, shift_rows(qkv, 1), shift_rows(qkv, 2), shift_rows(qkv, 3), ab]
    gd = rowwise_fwd(gdn_pre_fn, gdn_rows, sh['gdn_pre'], (MIX_W,) * 5, f"gdn_pre_fwd{l}")
    o_raw, gs0, gt = core_fwd(gdn_heads, list(gd), f"gdn_core_fwd{l}")
    (y_gdn,) = rowwise_fwd(gdn_post_fn, [o_raw, z], sh['gdn_post'], (MIX_W,), f"gdn_post_fwd{l}")

    mixed = jnp.concatenate([y_lru, y_rw, y_gdn], axis=1)
    saved = dict(lru_rows=lru_rows, a=a, hseq=hseq, ly=ly, rw_rows=rw_rows, rw=rw, y_raw=y_raw, rs0=rs0, rt=rt,
                 gdn_rows=gdn_rows, gd=gd, o_raw=o_raw, gs0=gs0, gt=gt, z=z)
    v_layer0 = v if l == 0 else None
    return mixed, saved, v_layer0


def _mixer_bwd(dmixed, sv, sh, l, dv_first):
    d_lru, d_rw, d_gdn = dmixed[:, 0:256], dmixed[:, 256:640], dmixed[:, 640:1024]
    gw = {}

    dh, dly, d_og = rowwise_bwd(lru_post_fn, [sv['hseq'], sv['ly']], sh['lru_post'], [d_lru], f"lru_post_bwd{l}")
    gscan = lru_scan(unshift_rows(sv['a'], 1), dh, True, f"lru_scan_bwd{l}")
    res = rowwise_bwd(lru_pre_fn, sv['lru_rows'], sh['lru_pre'], [gscan, shift_rows(sv['hseq'], 1)],
                      f"lru_pre_bwd{l}", ct_fn=lambda gs, hp: (gs * hp, gs))
    dlx = res[0] + unshift_rows(res[1], 1) + unshift_rows(res[2], 2) + unshift_rows(res[3], 3)
    dw0, dw1, dw2, dw3, dcb, dga, dgab, dgx, dgxb, dlam = res[5:]
    gw['lru_conv_w'] = jnp.concatenate([dw0, dw1, dw2, dw3], axis=0)
    gw['lru_conv_b'] = dcb[0]
    gw['lru_gate_a_w'] = _block_diag_grad(dga)
    gw['lru_gate_a_b'] = dgab.reshape(LRU_BLOCKS, 64)
    gw['lru_gate_x_w'] = _block_diag_grad(dgx)
    gw['lru_gate_x_b'] = dgxb.reshape(LRU_BLOCKS, 64)
    gw['lru_lambda'] = dlam[0]
    gw['lru_out_norm'] = d_og[0]

    r, lw, k2, v, kk, ar, g = sv['rw']
    res = rowwise_bwd(rwkv_post_fn, [sv['y_raw'], r, k2, v, g], sh['rw_post'], [d_rw], f"rwkv_post_bwd{l}")
    dy_raw, dr_p, dk2_p, dv_p, dg = res[:5]
    gw['rwkv_ln_g'], gw['rwkv_ln_b'], gw['rwkv_r_k'] = res[5][0], res[6][0], res[7].reshape(HEADS, HEAD_DIM)
    dr_c, dlw, dk2_c, dv_c, dkk, dar = core_bwd(rwkv_heads, [r, lw, k2, v, kk, ar], sv['rs0'], sv['rt'], dy_raw,
                                                 f"rwkv_core_bwd{l}")
    cts = [dr_p, dr_c, dlw, dk2_p, dk2_c, dv_p, dv_c, dkk, dar, dg]
    if l == 0:
        cts.append(dv_first)
        ct_fn = lambda a1, a2, b, c1, c2, d1, d2, e, f, gg, vf: (a1 + a2, b, c1 + c2, d1 + d2 + vf, e, f, gg)
    else:
        ct_fn = lambda a1, a2, b, c1, c2, d1, d2, e, f, gg: (a1 + a2, b, c1 + c2, d1 + d2, e, f, gg)
    res = rowwise_bwd(make_rwkv_pre_fn(l > 0), sv['rw_rows'], sh['rw_pre'], cts, f"rwkv_pre_bwd{l}", ct_fn=ct_fn)
    dprw = res[0] + unshift_rows(res[1], 1)
    nrow = len(sv['rw_rows'])
    dv_first_out = res[2] if l > 0 else None
    sg = res[nrow:]
    gw['rwkv_mu'], gw['rwkv_w_up'], gw['rwkv_w_bias'], gw['rwkv_a_up'] = sg[0][0], sg[1], sg[2][0], sg[3]
    gw['rwkv_a_bias'], gw['rwkv_g_up'], gw['rwkv_k_k'], gw['rwkv_k_a'] = sg[4][0], sg[5], sg[6][0], sg[7][0]
    if l > 0:
        gw['rwkv_vres_w1'], gw['rwkv_vres_w2'], gw['rwkv_vres_b'] = sg[8], sg[9], sg[10][0]

    do_raw, dz, d_ng = rowwise_bwd(gdn_post_fn, [sv['o_raw'], sv['z']], sh['gdn_post'], [d_gdn], f"gdn_post_bwd{l}")
    gw['gdn_norm'] = jnp.sum(d_ng.reshape(HEADS, HEAD_DIM), axis=0)
    dgd = core_bwd(gdn_heads, list(sv['gd']), sv['gs0'], sv['gt'], do_raw, f"gdn_core_bwd{l}")
    res = rowwise_bwd(gdn_pre_fn, sv['gdn_rows'], sh['gdn_pre'], list(dgd), f"gdn_pre_bwd{l}")
    dqkv = res[0] + unshift_rows(res[1], 1) + unshift_rows(res[2], 2) + unshift_rows(res[3], 3)
    dab = res[4]
    gw['gdn_conv_w'] = jnp.concatenate(res[5:9], axis=0)
    gw['gdn_a_log'], gw['gdn_dt_bias'] = res[9][0, :HEADS], res[10][0, :HEADS]

    dp = jnp.concatenate([dlx, dly, dprw, dqkv, dz, dab], axis=1)
    return dp, gw, dv_first_out


IN_SHARD = D_IN // N_CHIPS
IN_SHARD_PAD = D_IN_PAD // N_CHIPS


def _cols_from_shards(p):
    T = p.shape[0]
    nat = p.reshape(T, N_CHIPS, IN_SHARD_PAD)[:, :, :IN_SHARD].reshape(T, D_IN)
    return jnp.pad(nat, ((0, 0), (0, D_IN_PAD - D_IN)))


def _cols_to_shards(p):
    T = p.shape[0]
    sh = jnp.pad(p[:, :D_IN].reshape(T, N_CHIPS, IN_SHARD), ((0, 0), (0, 0), (0, IN_SHARD_PAD - IN_SHARD)))
    return sh.reshape(T, D_IN_PAD)


def local_step(x, target, w, wb):
    saved = []
    v_first = None
    for l in range(N_LAYERS):
        sh = _layer_shared(w, l)
        x1 = ffn_fwd(x, _row(w['ffn1_norm'][l]), wb['ffn1_wi'][l], wb['ffn1_wo'][l], f"ffn1_fwd{l}")
        p = _cols_from_shards(proj_fwd(x1, _row(w['mix_norm'][l]), wb['w_in'][l], f"proj_fwd{l}"))
        mixed, sv, v0 = _mixer_fwd(p, sh, l, v_first)
        if l == 0:
            v_first = v0
        x2 = out_fwd(mixed, wb['w_out'][l], x1, f"out_fwd{l}")
        x3 = ffn_fwd(x2, _row(w['ffn2_norm'][l]), wb['ffn2_wi'][l], wb['ffn2_wo'][l], f"ffn2_fwd{l}")
        saved.append(dict(x0=x, x1=x1, x2=x2, mixed=mixed, sv=sv, sh=sh))
        x = x3

    loss, dx, dgf = loss_head(x, _row(w['final_norm']), target, "loss_head")
    per_layer = [None] * N_LAYERS
    dv_first = jnp.zeros((x.shape[0], MIX_W), F32)
    for l in reversed(range(N_LAYERS)):
        s = saved[l]
        gw = {}
        dx, dg2, dwg, dwu, dwo = ffn_bwd(s['x2'], dx, _row(w['ffn2_norm'][l]), wb['ffn2_wi'][l], wb['ffn2_wo'][l],
                                         f"ffn2_bwd{l}")
        gw['ffn2_norm'], gw['ffn2_wi'], gw['ffn2_wo'] = dg2[0], jnp.concatenate([dwg, dwu], axis=1), dwo
        dmixed, gw['w_out'] = out_bwd(s['mixed'], wb['w_out'][l], dx, f"out_bwd{l}")
        dp, gmix, dvf = _mixer_bwd(dmixed, s['sv'], s['sh'], l, dv_first)
        if l > 0:
            dv_first = dvf
        gw.update(gmix)
        dx, dgm, dwin = proj_bwd(s['x1'], dx, _row(w['mix_norm'][l]), wb['w_in'][l], _cols_to_shards(dp),
                                 f"proj_bwd{l}")
        gw['mix_norm'], gw['w_in'] = dgm[0], dwin
        dx, dg1, dwg, dwu, dwo = ffn_bwd(s['x0'], dx, _row(w['ffn1_norm'][l]), wb['ffn1_wi'][l], wb['ffn1_wo'][l],
                                         f"ffn1_bwd{l}")
        gw['ffn1_norm'], gw['ffn1_wi'], gw['ffn1_wo'] = dg1[0], jnp.concatenate([dwg, dwu], axis=1), dwo
        per_layer[l] = gw

    grads = {'final_norm': dgf[0]}
    for name in WEIGHTS:
        if name == 'final_norm':
            continue
        if name.startswith('rwkv_vres'):
            grads[name] = per_layer[1][name][None]
        else:
            grads[name] = jnp.stack([per_layer[l][name] for l in range(N_LAYERS)])
    return loss[0, 0], dx, grads


ANY = pl.BlockSpec(memory_space=pl.ANY)


def _coords():
    return lax.axis_index("x"), lax.axis_index("y"), lax.axis_index("c")


def _other_chips(x, y):
    return [((x + 1) % 2, y), (x, (y + 1) % 2), ((x + 1) % 2, (y + 1) % 2)]


def allreduce_small(pack, name):
    R = pack.shape[0]

    def body(x_ref, o_ref, buf, send_sems, recv_sems):
        x, y, c = _coords()
        me = 4 * x + 2 * y + c
        buf[me] = x_ref[...]
        copies = []
        for k in range(1, 8):
            peer = ((x + (k >> 2)) % 2, (y + ((k >> 1) & 1)) % 2, (c + (k & 1)) % 2)
            cp = pltpu.make_async_remote_copy(src_ref=x_ref, dst_ref=buf.at[me], send_sem=send_sems.at[k - 1],
                                              recv_sem=recv_sems.at[k - 1], device_id=peer, device_id_type=MESH)
            cp.start()
            copies.append(cp)
        for cp in copies:
            cp.wait()
        acc = buf[0]
        for d in range(1, 8):
            acc = acc + buf[d]
        o_ref[...] = acc

    vm = pl.BlockSpec(memory_space=pltpu.VMEM)
    return pl.pallas_call(
        body, name=name, in_specs=[vm], out_specs=vm, out_shape=jax.ShapeDtypeStruct((R, 128), F32),
        scratch_shapes=[pltpu.VMEM((8, R, 128), F32), pltpu.SemaphoreType.DMA((7,)), pltpu.SemaphoreType.DMA((7,))],
        compiler_params=_params())(pack)


def allgather_chips(shards, name):
    n = len(shards)

    def body(*refs):
        x_refs, o_refs = refs[:n], refs[n:2 * n]
        send_sems, recv_sems, local_sems = refs[2 * n:]
        x, y, c = _coords()
        s_me = 2 * x + y
        sib = (x, y, 1 - c)
        chips = _other_chips(x, y)

        def rows(a, core):
            rh = shards[a].shape[0] // 2
            return pl.ds(pl.multiple_of(core * rh, 16), rh)

        def copy(a, k, src, dst, to):
            return pltpu.make_async_remote_copy(src_ref=src, dst_ref=dst, send_sem=send_sems.at[6 * a + k],
                                                recv_sem=recv_sems.at[6 * a + k], device_id=to, device_id_type=MESH)

        own, sent, passed = [], [], []
        for a in range(n):
            cp = pltpu.make_async_copy(x_refs[a], o_refs[a].at[s_me], local_sems.at[a])
            cp.start()
            own.append(cp)
        for j, (px, py) in enumerate(chips):
            for a in range(n):
                cp = copy(a, j, x_refs[a].at[rows(a, c)], o_refs[a].at[s_me, rows(a, c)], (px, py, c))
                cp.start()
                sent.append(cp)
        for j, (px, py) in enumerate(chips):
            for a in range(n):
                part = o_refs[a].at[2 * px + py, rows(a, c)]
                copy(a, j, part, part, (px, py, c)).wait_recv()
                fw = copy(a, 3 + j, part, part, sib)
                fw.start()
                passed.append(fw)
        for j, (px, py) in enumerate(chips):
            for a in range(n):
                part = o_refs[a].at[2 * px + py, rows(a, 1 - c)]
                copy(a, 3 + j, part, part, sib).wait_recv()
        for cp in sent + passed:
            cp.wait_send()
        for cp in own:
            cp.wait()

    return pl.pallas_call(
        body, name=name, in_specs=[ANY] * n, out_specs=tuple([ANY] * n),
        out_shape=tuple(jax.ShapeDtypeStruct((N_CHIPS,) + s.shape, s.dtype) for s in shards),
        scratch_shapes=[pltpu.SemaphoreType.DMA((6 * n,)), pltpu.SemaphoreType.DMA((6 * n,)),
                        pltpu.SemaphoreType.DMA((n,))],
        compiler_params=_params())(*shards)


def sibling_swap(srcs, halves, name):
    n = len(srcs)
    out_shapes = [(s.shape[0], s.shape[1] // 2, s.shape[2]) if halves else s.shape for s in srcs]

    def body(*refs):
        x_refs, o_refs = refs[:n], refs[n:2 * n]
        send_sems, recv_sems = refs[2 * n:]
        x, y, c = _coords()
        copies = []
        for a in range(n):
            part = x_refs[a]
            if halves:
                rh = srcs[a].shape[1] // 2
                part = part.at[:, pl.ds(pl.multiple_of((1 - c) * rh, 16), rh)]
            cp = pltpu.make_async_remote_copy(src_ref=part, dst_ref=o_refs[a], send_sem=send_sems.at[a],
                                              recv_sem=recv_sems.at[a], device_id=(x, y, 1 - c), device_id_type=MESH)
            cp.start()
            copies.append(cp)
        for cp in copies:
            cp.wait()

    return pl.pallas_call(
        body, name=name, in_specs=[ANY] * n, out_specs=tuple([ANY] * n),
        out_shape=tuple(jax.ShapeDtypeStruct(sh, s.dtype) for sh, s in zip(out_shapes, srcs)),
        scratch_shapes=[pltpu.SemaphoreType.DMA((n,)), pltpu.SemaphoreType.DMA((n,))],
        compiler_params=_params())(*srcs)


def scatter_chips(parts, name):
    n = len(parts)

    def body(*refs):
        x_refs, o_refs = refs[:n], refs[n:2 * n]
        send_sems, recv_sems = refs[2 * n:]
        x, y, c = _coords()
        copies = []
        for j, (px, py) in enumerate(_other_chips(x, y)):
            for a in range(n):
                cp = pltpu.make_async_remote_copy(src_ref=x_refs[a].at[2 * px + py], dst_ref=o_refs[a].at[j],
                                                  send_sem=send_sems.at[3 * a + j], recv_sem=recv_sems.at[3 * a + j],
                                                  device_id=(px, py, c), device_id_type=MESH)
                cp.start()
                copies.append(cp)
        for cp in copies:
            cp.wait()

    return pl.pallas_call(
        body, name=name, in_specs=[ANY] * n, out_specs=tuple([ANY] * n),
        out_shape=tuple(jax.ShapeDtypeStruct((3,) + p.shape[1:], p.dtype) for p in parts),
        scratch_shapes=[pltpu.SemaphoreType.DMA((3 * n,)), pltpu.SemaphoreType.DMA((3 * n,))],
        compiler_params=_params())(*parts)


def _row_block(rows):
    return max(b for b in range(16, 257, 16) if rows % b == 0)


def chip_sum(gpack, recv, core, name):
    n, R, W = gpack.shape
    rh = R // 2
    rb = _row_block(rh)
    nb = rh // rb

    def body(c_ref, g_ref, r_ref, o_ref, ob_ref):
        s = g_ref[...] + r_ref[...]
        o_ref[...] = s
        ob_ref[...] = s.astype(BF16)

    blk = pl.BlockSpec((1, rb, W), lambda i, j, c_ref: (i, j, 0))
    spec = pltpu.PrefetchScalarGridSpec(
        num_scalar_prefetch=1, grid=(n, nb),
        in_specs=[pl.BlockSpec((1, rb, W), lambda i, j, c_ref: (i, c_ref[0] * nb + j, 0)), blk],
        out_specs=(blk, blk))
    return pl.pallas_call(
        body, name=name, grid_spec=spec,
        out_shape=(jax.ShapeDtypeStruct((n, rh, W), F32), jax.ShapeDtypeStruct((n, rh, W), BF16)),
        compiler_params=_params(("arbitrary", "arbitrary")))(core, gpack, recv)


def shard_sum(own, recv, name):
    R, W = own.shape
    rb = _row_block(R)

    def body(a_ref, r_ref, o_ref):
        acc = a_ref[...]
        for j in range(3):
            acc = acc + r_ref[j].astype(F32)
        o_ref[...] = acc

    return pl.pallas_call(
        body, name=name, grid=(R // rb,),
        in_specs=[pl.BlockSpec((rb, W), lambda i: (i, 0)), pl.BlockSpec((3, rb, W), lambda i: (0, i, 0))],
        out_specs=pl.BlockSpec((rb, W), lambda i: (i, 0)), out_shape=jax.ShapeDtypeStruct((R, W), F32),
        compiler_params=_params(("arbitrary",)))(own, recv)


def adamw(w, m, v, g, name):
    R, C = w.shape
    rb = 128 if R % 128 == 0 else R
    bc1 = 1.0 - ADAM_B1 ** ADAM_STEP
    bc2 = 1.0 - ADAM_B2 ** ADAM_STEP

    def body(w_ref, m_ref, v_ref, g_ref, d_ref, nm_ref, nv_ref):
        gv = g_ref[...]
        nm = ADAM_B1 * m_ref[...] + (1.0 - ADAM_B1) * gv
        nv = ADAM_B2 * v_ref[...] + (1.0 - ADAM_B2) * (gv * gv)
        d_ref[...] = -ADAM_LR * ((nm / bc1) / (jnp.sqrt(nv / bc2) + ADAM_EPS) + ADAM_WD * w_ref[...])
        nm_ref[...] = nm
        nv_ref[...] = nv

    blk = pl.BlockSpec((rb, C), lambda i: (i, 0))
    sh = jax.ShapeDtypeStruct((R, C), F32)
    return pl.pallas_call(body, name=name, grid=(R // rb,), in_specs=[blk] * 4, out_specs=(blk,) * 3,
                          out_shape=(sh, sh, sh), compiler_params=_params(("arbitrary",)))(w, m, v, g)


SMALL = [n for n in WEIGHTS if n not in BIG]


def _pack_rows(flat, width, row_multiple):
    n = flat.shape[-1]
    per = width * row_multiple
    total = -(-n // per) * per
    flat = jnp.pad(flat, [(0, 0)] * (flat.ndim - 1) + [(0, total - n)])
    return flat.reshape(flat.shape[:-1] + (total // width, width))


def _shard_2d(local):
    L, r, c = local.shape
    pad = -c % 128
    return jnp.pad(local, ((0, 0), (0, 0), (0, pad))).reshape(L * r, c + pad)


def _from_chips(g, axis):
    c = g.shape[2]
    g = g.reshape(N_CHIPS, N_LAYERS, -1, c)
    if axis == 2:
        return jnp.transpose(g, (1, 2, 0, 3)).reshape(N_LAYERS, -1, N_CHIPS * c)
    return jnp.transpose(g, (1, 0, 2, 3)).reshape(N_LAYERS, -1, c)


def _to_chips(full, axis):
    if axis == 2:
        c = full.shape[2] // N_CHIPS
        g = jnp.transpose(full.reshape(N_LAYERS, -1, N_CHIPS, c), (2, 0, 1, 3))
    else:
        c = full.shape[2]
        g = jnp.transpose(full.reshape(N_LAYERS, N_CHIPS, -1, c), (1, 0, 2, 3))
    return g.reshape(N_CHIPS, -1, c)


def _local_shard(full, axis, chip):
    size = full.shape[axis] // N_CHIPS
    return lax.dynamic_slice_in_dim(full, chip * size, size, axis)


def kernel(x, ffn1_norm, ffn1_wi, ffn1_wo, mix_norm, w_in, w_out, lru_conv_w, lru_conv_b, lru_gate_a_w, lru_gate_a_b, lru_gate_x_w, lru_gate_x_b, lru_lambda, lru_out_norm, rwkv_mu, rwkv_w_up, rwkv_w_bias, rwkv_a_up, rwkv_a_bias, rwkv_g_up, rwkv_k_k, rwkv_k_a, rwkv_r_k, rwkv_ln_g, rwkv_ln_b, rwkv_vres_w1, rwkv_vres_w2, rwkv_vres_b, gdn_conv_w, gdn_a_log, gdn_dt_bias, gdn_norm, ffn2_norm, ffn2_wi, ffn2_wo, final_norm, loss_target, m_ffn1_norm, m_ffn1_wi, m_ffn1_wo, m_mix_norm, m_w_in, m_w_out, m_lru_conv_w, m_lru_conv_b, m_lru_gate_a_w, m_lru_gate_a_b, m_lru_gate_x_w, m_lru_gate_x_b, m_lru_lambda, m_lru_out_norm, m_rwkv_mu, m_rwkv_w_up, m_rwkv_w_bias, m_rwkv_a_up, m_rwkv_a_bias, m_rwkv_g_up, m_rwkv_k_k, m_rwkv_k_a, m_rwkv_r_k, m_rwkv_ln_g, m_rwkv_ln_b, m_rwkv_vres_w1, m_rwkv_vres_w2, m_rwkv_vres_b, m_gdn_conv_w, m_gdn_a_log, m_gdn_dt_bias, m_gdn_norm, m_ffn2_norm, m_ffn2_wi, m_ffn2_wo, m_final_norm, v_ffn1_norm, v_ffn1_wi, v_ffn1_wo, v_mix_norm, v_w_in, v_w_out, v_lru_conv_w, v_lru_conv_b, v_lru_gate_a_w, v_lru_gate_a_b, v_lru_gate_x_w, v_lru_gate_x_b, v_lru_lambda, v_lru_out_norm, v_rwkv_mu, v_rwkv_w_up, v_rwkv_w_bias, v_rwkv_a_up, v_rwkv_a_bias, v_rwkv_g_up, v_rwkv_k_k, v_rwkv_k_a, v_rwkv_r_k, v_rwkv_ln_g, v_rwkv_ln_b, v_rwkv_vres_w1, v_rwkv_vres_w2, v_rwkv_vres_b, v_gdn_conv_w, v_gdn_a_log, v_gdn_dt_bias, v_gdn_norm, v_ffn2_norm, v_ffn2_wi, v_ffn2_wo, v_final_norm):
    args = locals()
    w_loc = {n: args[n] for n in WEIGHTS}
    m_loc = {n: args['m_' + n] for n in WEIGHTS}
    v_loc = {n: args['v_' + n] for n in WEIGHTS}
    chip = 2 * lax.axis_index("x") + lax.axis_index("y")
    core = lax.axis_index("c")

    big_names = list(BIG)
    gathered = allgather_chips([_shard_2d(w_loc[n].astype(BF16)) for n in big_names], "allgather_big")
    wb = {n: _from_chips(g, BIG[n]) for n, g in zip(big_names, gathered)}

    sm_names = list(SMALL_SHARDED)
    placed = []
    for n in sm_names:
        ax = SMALL_SHARDED[n]
        full_shape = w_loc[n].shape[:ax] + (N_CHIPS * w_loc[n].shape[ax],) + w_loc[n].shape[ax + 1:]
        src = w_loc[n] * (core == 0).astype(F32)
        placed.append(lax.dynamic_update_slice_in_dim(jnp.zeros(full_shape, F32), src, chip * w_loc[n].shape[ax], ax))
    summed = allreduce_small(_pack_rows(jnp.concatenate([p.reshape(-1) for p in placed]), 128, 8),
                             "allgather_small").reshape(-1)
    w_full, off = dict(w_loc), 0
    for n, p in zip(sm_names, placed):
        w_full[n] = summed[off:off + p.size].reshape(p.shape)
        off += p.size

    loss, dx, grads = local_step(x[0], loss_target[0], w_full, wb)
    loss = lax.psum(loss, ("x", "y", "c"))

    gsum = allreduce_small(_pack_rows(jnp.concatenate([grads[n].reshape(-1) for n in SMALL]), 128, 8),
                           "allreduce_small").reshape(-1)
    g_loc, off = {}, 0
    for n in SMALL:
        g = gsum[off:off + grads[n].size].reshape(grads[n].shape)
        off += grads[n].size
        g_loc[n] = _local_shard(g, SMALL_SHARDED[n], chip) if n in SMALL_SHARDED else g

    parts = [_to_chips(grads[n], BIG[n]) for n in big_names]
    from_sib = sibling_swap(parts, True, "grad_swap_cores")
    core_arg = core.reshape(1).astype(jnp.int32)
    sums = [chip_sum(p, r, core_arg, f"grad_chip_sum_{n}") for n, p, r in zip(big_names, parts, from_sib)]
    from_chips = scatter_chips([s[1] for s in sums], "grad_scatter")
    halves = [shard_sum(lax.dynamic_index_in_dim(s[0], chip, 0, keepdims=False), r, f"grad_shard_sum_{n}")
              for n, s, r in zip(big_names, sums, from_chips)]
    others = sibling_swap(halves, False, "grad_share_cores")
    for n, half, other in zip(big_names, halves, others):
        both = jnp.stack([half, other])
        rows = jnp.concatenate([lax.dynamic_index_in_dim(both, core, 0, keepdims=False),
                                lax.dynamic_index_in_dim(both, 1 - core, 0, keepdims=False)], axis=0)
        g_loc[n] = rows[:, :w_loc[n].shape[-1]].reshape(w_loc[n].shape)

    delta, new_m, new_v = {}, {}, {}
    for n in big_names:
        shp = w_loc[n].shape
        two_d = lambda a: a.reshape(-1, shp[-1])
        d, nm, nv = adamw(two_d(w_loc[n]), two_d(m_loc[n]), two_d(v_loc[n]), two_d(g_loc[n]), f"adamw_{n}")
        delta[n], new_m[n], new_v[n] = d.reshape(shp), nm.reshape(shp), nv.reshape(shp)
    pack = lambda d: _pack_rows(jnp.concatenate([d[n].reshape(-1) for n in SMALL]), 128, 128)
    res = adamw(pack(w_loc), pack(m_loc), pack(v_loc), pack(g_loc), "adamw_small")
    off = 0
    for n in SMALL:
        size = w_loc[n].size
        for dst, r in zip((delta, new_m, new_v), res):
            dst[n] = r.reshape(-1)[off:off + size].reshape(w_loc[n].shape)
        off += size

    return (loss, dx[None], *[g_loc[n] for n in WEIGHTS], *[delta[n] for n in WEIGHTS],
            *[new_m[n] for n in WEIGHTS], *[new_v[n] for n in WEIGHTS])
```

```python
import functools

import numpy as np
import jax
import jax.numpy as jnp
from jax import lax
from jax.experimental import pallas as pl
from jax.experimental.pallas import tpu as pltpu

F32 = jnp.float32
BF16 = jnp.bfloat16
HIGHEST = lax.Precision.HIGHEST
MESH = pl.DeviceIdType.MESH

D_MODEL = 1024
D_FF = 2816
N_LAYERS = 2
HEADS = 6
HEAD_DIM = 64
MIX_W = HEADS * HEAD_DIM
LRU_W = 256
LRU_BLOCKS = 4
RWKV_IN = 1408
D_IN = 3468
D_IN_PAD = 3584
NORM_EPS = 1e-6
GN_EPS = 64e-5
LRU_C = 8.0
CHUNK = 64
ROWS = 256
FF_CHUNK = 256
IN_CHUNK = 512
PACK_W = 1024
VMEM_LIMIT = 56 * 1024 * 1024

ADAM_LR, ADAM_B1, ADAM_B2, ADAM_EPS, ADAM_WD, ADAM_STEP = 0.001, 0.9, 0.999, 1e-08, 0.01, 10

WEIGHTS = ['ffn1_norm', 'ffn1_wi', 'ffn1_wo', 'mix_norm', 'w_in', 'w_out', 'lru_conv_w', 'lru_conv_b',
           'lru_gate_a_w', 'lru_gate_a_b', 'lru_gate_x_w', 'lru_gate_x_b', 'lru_lambda', 'lru_out_norm',
           'rwkv_mu', 'rwkv_w_up', 'rwkv_w_bias', 'rwkv_a_up', 'rwkv_a_bias', 'rwkv_g_up', 'rwkv_k_k',
           'rwkv_k_a', 'rwkv_r_k', 'rwkv_ln_g', 'rwkv_ln_b', 'rwkv_vres_w1', 'rwkv_vres_w2', 'rwkv_vres_b',
           'gdn_conv_w', 'gdn_a_log', 'gdn_dt_bias', 'gdn_norm', 'ffn2_norm', 'ffn2_wi', 'ffn2_wo', 'final_norm']
BIG = {'ffn1_wi': 2, 'ffn1_wo': 1, 'w_in': 2, 'w_out': 1, 'ffn2_wi': 2, 'ffn2_wo': 1}
SMALL_SHARDED = {'lru_conv_w': 2, 'rwkv_w_up': 2, 'rwkv_a_up': 2, 'rwkv_g_up': 2, 'rwkv_vres_w1': 1,
                 'rwkv_vres_w2': 2, 'gdn_conv_w': 2}
N_CHIPS = 4


def _params(sem=None):
    kw = dict(vmem_limit_bytes=VMEM_LIMIT)
    if sem is not None:
        kw['dimension_semantics'] = sem
    return pltpu.CompilerParams(**kw)


def _bdot(a, b, dims=(((1,), (0,)), ((), ()))):
    return lax.dot_general(a.astype(BF16), b.astype(BF16), dims, preferred_element_type=F32)


def _bdot_nt(a, b):
    return _bdot(a, b, (((1,), (1,)), ((), ())))


def _bdot_tn(a, b):
    return _bdot(a, b, (((0,), (0,)), ((), ())))


_DIMS = {'nn': (((1,), (0,)), ((), ())), 'nt': (((1,), (1,)), ((), ())), 'tn': (((0,), (0,)), ((), ()))}


def _split(a, terms):
    parts = []
    for _ in range(terms - 1):
        hi = a.astype(BF16)
        parts.append(hi)
        a = a - hi.astype(F32)
    parts.append(a.astype(BF16))
    return parts


_BATCH_DIMS = {'nn': (((2,), (1,)), ((0,), (0,))), 'nt': (((2,), (2,)), ((0,), (0,))),
               'tn': (((1,), (1,)), ((0,), (0,)))}


def _dot3(a, b, kind):
    ah, al = _split(a, 2)
    bh, bl = _split(b, 2)
    dims = _BATCH_DIMS[kind] if a.ndim == 3 else _DIMS[kind]
    d = lambda p, q: lax.dot_general(p, q, dims, preferred_element_type=F32)
    return d(ah, bh) + (d(ah, bl) + d(al, bh))


@functools.partial(jax.custom_vjp, nondiff_argnums=(2,))
def _cdot_k(a, b, kind):
    return _dot3(a, b, kind)


def _cdot_k_fwd(a, b, kind):
    return _dot3(a, b, kind), (a, b)


def _cdot_k_bwd(kind, res, ct):
    a, b = res
    if kind == 'nn':
        return _dot3(ct, b, 'nt'), _dot3(a, ct, 'tn')
    if kind == 'nt':
        return _dot3(ct, b, 'nn'), _dot3(ct, a, 'tn')
    return _dot3(b, ct, 'nt'), _dot3(a, ct, 'nn')


_cdot_k.defvjp(_cdot_k_fwd, _cdot_k_bwd)


def _cdot(a, b):
    return _cdot_k(a, b, 'nn')


def _cdot_nt(a, b):
    return _cdot_k(a, b, 'nt')


def _cdot_tn(a, b):
    return _cdot_k(a, b, 'tn')


def _hdot(a, b):
    return _cdot_k(a, b, 'nn')


def _dot_exact(x, m01, kind):
    d = lambda p: lax.dot_general(p, m01.astype(BF16), _DIMS[kind], preferred_element_type=F32)
    hi, mid, lo = _split(x, 3)
    return d(hi) + (d(mid) + d(lo))


@functools.partial(jax.custom_vjp, nondiff_argnums=(1,))
def _xdot(x, make_m):
    return _dot_exact(x, make_m(), 'nn')


def _xdot_fwd(x, make_m):
    return _dot_exact(x, make_m(), 'nn'), None


def _xdot_bwd(make_m, _, ct):
    return (_dot_exact(ct, make_m(), 'nt'),)


_xdot.defvjp(_xdot_fwd, _xdot_bwd)


def _iota2(n, m):
    return lax.broadcasted_iota(jnp.int32, (n, m), 0), lax.broadcasted_iota(jnp.int32, (n, m), 1)


def _head_blocks(w):
    ri, ci = _iota2(w, w)
    return (ri // HEAD_DIM == ci // HEAD_DIM).astype(F32)


def _segsum(x):
    return _xdot(x, functools.partial(_head_blocks, x.shape[-1]))


def _cumsum_rows(x):
    return _cumsum_k(x, x.shape[0])


@functools.partial(jax.custom_vjp, nondiff_argnums=(1,))
def _cumsum_k(x, n):
    return _lower_dot(x, n, False)


def _lower_dot(x, n, transpose):
    ri, ci = _iota2(n, n)
    m = ((ri <= ci) if transpose else (ri >= ci)).astype(BF16)
    d = lambda p: lax.dot_general(m, p, _DIMS['nn'], preferred_element_type=F32)
    hi, mid, lo = _split(x, 3)
    return d(hi) + (d(mid) + d(lo))


def _cumsum_k_fwd(x, n):
    return _lower_dot(x, n, False), None


def _cumsum_k_bwd(n, _, ct):
    return (_lower_dot(ct, n, True),)


_cumsum_k.defvjp(_cumsum_k_fwd, _cumsum_k_bwd)


def _rms(x, g):
    return x * lax.rsqrt(jnp.mean(x * x, axis=-1, keepdims=True) + NORM_EPS) * g


DENSE_ROWS = 1024


def _row_loop(n_rows, fn):
    rows = min(DENSE_ROWS, n_rows)

    def step(i, c):
        fn(pl.ds(pl.multiple_of(i * rows, rows), rows))
        return c
    lax.fori_loop(0, n_rows // rows, step, 0)


def ffn_fwd(x, g, wi, wo, name):
    T = x.shape[0]
    nj = D_FF // FF_CHUNK

    def body(x_ref, g_ref, wg_ref, wu_ref, wo_ref, o_ref, h_ref, acc_ref):
        j = pl.program_id(0)

        @pl.when(j == 0)
        def _():
            def init(r):
                h_ref[r, :] = _rms(x_ref[r, :], g_ref[...]).astype(BF16)
                acc_ref[r, :] = jnp.zeros((r.size, D_MODEL), F32)
            _row_loop(T, init)

        def blk(r):
            hb = h_ref[r, :]
            gate = jnp.dot(hb, wg_ref[...], preferred_element_type=F32)
            up = jnp.dot(hb, wu_ref[...], preferred_element_type=F32)
            a = (gate * jax.nn.sigmoid(gate) * up).astype(BF16)
            acc_ref[r, :] += jnp.dot(a, wo_ref[...], preferred_element_type=F32)
        _row_loop(T, blk)

        @pl.when(j == nj - 1)
        def _():
            def fin(r):
                o_ref[r, :] = x_ref[r, :] + 0.5 * acc_ref[r, :]
            _row_loop(T, fin)

    full = pl.BlockSpec((T, D_MODEL), lambda j: (0, 0))
    return pl.pallas_call(
        body, name=name, grid=(nj,),
        in_specs=[full, pl.BlockSpec((1, D_MODEL), lambda j: (0, 0)),
                  pl.BlockSpec((D_MODEL, FF_CHUNK), lambda j: (0, j)),
                  pl.BlockSpec((D_MODEL, FF_CHUNK), lambda j: (0, j + nj)),
                  pl.BlockSpec((FF_CHUNK, D_MODEL), lambda j: (j, 0))],
        out_specs=full, out_shape=jax.ShapeDtypeStruct((T, D_MODEL), F32),
        scratch_shapes=[pltpu.VMEM((T, D_MODEL), BF16), pltpu.VMEM((T, D_MODEL), F32)],
        compiler_params=_params(("arbitrary",)))(x, g, wi, wi, wo)


def _norm_bwd_rows(x, g, dh, dres):
    rstd = lax.rsqrt(jnp.mean(x * x, axis=-1, keepdims=True) + NORM_EPS)
    xh = x * rstd
    dxh = dh * g
    dx = rstd * (dxh - xh * jnp.mean(dxh * xh, axis=-1, keepdims=True))
    return dres + dx, jnp.sum(dh * xh, axis=0, keepdims=True)


def ffn_bwd(x, dy, g, wi, wo, name):
    T = x.shape[0]
    nj = D_FF // FF_CHUNK

    def body(x_ref, dy_ref, g_ref, wg_ref, wu_ref, wo_ref, dx_ref, dg_ref, dwg_ref, dwu_ref, dwo_ref,
             h_ref, da_ref, dh_ref):
        j = pl.program_id(0)

        @pl.when(j == 0)
        def _():
            def init(r):
                h_ref[r, :] = _rms(x_ref[r, :], g_ref[...]).astype(BF16)
                da_ref[r, :] = (0.5 * dy_ref[r, :]).astype(BF16)
                dh_ref[r, :] = jnp.zeros((r.size, D_MODEL), F32)
            _row_loop(T, init)

        dwg_ref[...] = jnp.zeros_like(dwg_ref)
        dwu_ref[...] = jnp.zeros_like(dwu_ref)
        dwo_ref[...] = jnp.zeros_like(dwo_ref)

        def blk(r):
            hb = h_ref[r, :]
            db = da_ref[r, :]
            gate = jnp.dot(hb, wg_ref[...], preferred_element_type=F32)
            up = jnp.dot(hb, wu_ref[...], preferred_element_type=F32)
            sg = jax.nn.sigmoid(gate)
            sl = gate * sg
            da = _bdot_nt(db, wo_ref[...])
            dup = (da * sl).astype(BF16)
            dgate = (da * up * (sg * (1.0 + gate * (1.0 - sg)))).astype(BF16)
            dwo_ref[...] += _bdot_tn((sl * up).astype(BF16), db)
            dwg_ref[...] += _bdot_tn(hb, dgate)
            dwu_ref[...] += _bdot_tn(hb, dup)
            dh_ref[r, :] += _bdot_nt(dgate, wg_ref[...]) + _bdot_nt(dup, wu_ref[...])
        _row_loop(T, blk)

        @pl.when(j == nj - 1)
        def _():
            dg_ref[...] = jnp.zeros_like(dg_ref)

            def fin(r):
                dx, dg = _norm_bwd_rows(x_ref[r, :], g_ref[...], dh_ref[r, :], dy_ref[r, :])
                dx_ref[r, :] = dx
                dg_ref[...] += dg
            _row_loop(T, fin)

    full = pl.BlockSpec((T, D_MODEL), lambda j: (0, 0))
    vec = pl.BlockSpec((1, D_MODEL), lambda j: (0, 0))
    return pl.pallas_call(
        body, name=name, grid=(nj,),
        in_specs=[full, full, vec,
                  pl.BlockSpec((D_MODEL, FF_CHUNK), lambda j: (0, j)),
                  pl.BlockSpec((D_MODEL, FF_CHUNK), lambda j: (0, j + nj)),
                  pl.BlockSpec((FF_CHUNK, D_MODEL), lambda j: (j, 0))],
        out_specs=(full, vec,
                   pl.BlockSpec((D_MODEL, FF_CHUNK), lambda j: (0, j)),
                   pl.BlockSpec((D_MODEL, FF_CHUNK), lambda j: (0, j)),
                   pl.BlockSpec((FF_CHUNK, D_MODEL), lambda j: (j, 0))),
        out_shape=(jax.ShapeDtypeStruct((T, D_MODEL), F32), jax.ShapeDtypeStruct((1, D_MODEL), F32),
                   jax.ShapeDtypeStruct((D_MODEL, D_FF), F32), jax.ShapeDtypeStruct((D_MODEL, D_FF), F32),
                   jax.ShapeDtypeStruct((D_FF, D_MODEL), F32)),
        scratch_shapes=[pltpu.VMEM((T, D_MODEL), BF16), pltpu.VMEM((T, D_MODEL), BF16),
                        pltpu.VMEM((T, D_MODEL), F32)],
        compiler_params=_params(("arbitrary",)))(x, dy, g, wi, wi, wo)


def proj_fwd(x, g, w, name):
    T = x.shape[0]
    nj = D_IN_PAD // IN_CHUNK

    def body(x_ref, g_ref, w_ref, o_ref, h_ref):
        @pl.when(pl.program_id(0) == 0)
        def _():
            def init(r):
                h_ref[r, :] = _rms(x_ref[r, :], g_ref[...]).astype(BF16)
            _row_loop(T, init)

        def blk(r):
            o_ref[r, :] = jnp.dot(h_ref[r, :], w_ref[...], preferred_element_type=F32)
        _row_loop(T, blk)

    return pl.pallas_call(
        body, name=name, grid=(nj,),
        in_specs=[pl.BlockSpec((T, D_MODEL), lambda j: (0, 0)), pl.BlockSpec((1, D_MODEL), lambda j: (0, 0)),
                  pl.BlockSpec((D_MODEL, IN_CHUNK), lambda j: (0, j))],
        out_specs=pl.BlockSpec((T, IN_CHUNK), lambda j: (0, j)),
        out_shape=jax.ShapeDtypeStruct((T, D_IN_PAD), F32),
        scratch_shapes=[pltpu.VMEM((T, D_MODEL), BF16)],
        compiler_params=_params(("arbitrary",)))(x, g, w)


def proj_bwd(x, dres, g, w, dp, name):
    T = x.shape[0]
    nj = D_IN_PAD // IN_CHUNK

    def body(x_ref, dres_ref, g_ref, w_ref, dp_ref, dx_ref, dg_ref, dw_ref, h_ref, dh_ref):
        j = pl.program_id(0)

        @pl.when(j == 0)
        def _():
            def init(r):
                h_ref[r, :] = _rms(x_ref[r, :], g_ref[...]).astype(BF16)
                dh_ref[r, :] = jnp.zeros((r.size, D_MODEL), F32)
            _row_loop(T, init)

        dw_ref[...] = jnp.zeros_like(dw_ref)

        def blk(r):
            dpb = dp_ref[r, :].astype(BF16)
            dw_ref[...] += _bdot_tn(h_ref[r, :], dpb)
            dh_ref[r, :] += _bdot_nt(dpb, w_ref[...])
        _row_loop(T, blk)

        @pl.when(j == nj - 1)
        def _():
            dg_ref[...] = jnp.zeros_like(dg_ref)

            def fin(r):
                dx, dg = _norm_bwd_rows(x_ref[r, :], g_ref[...], dh_ref[r, :], dres_ref[r, :])
                dx_ref[r, :] = dx
                dg_ref[...] += dg
            _row_loop(T, fin)

    full = pl.BlockSpec((T, D_MODEL), lambda j: (0, 0))
    vec = pl.BlockSpec((1, D_MODEL), lambda j: (0, 0))
    return pl.pallas_call(
        body, name=name, grid=(nj,),
        in_specs=[full, full, vec, pl.BlockSpec((D_MODEL, IN_CHUNK), lambda j: (0, j)),
                  pl.BlockSpec((T, IN_CHUNK), lambda j: (0, j))],
        out_specs=(full, vec, pl.BlockSpec((D_MODEL, IN_CHUNK), lambda j: (0, j))),
        out_shape=(jax.ShapeDtypeStruct((T, D_MODEL), F32), jax.ShapeDtypeStruct((1, D_MODEL), F32),
                   jax.ShapeDtypeStruct((D_MODEL, D_IN_PAD), F32)),
        scratch_shapes=[pltpu.VMEM((T, D_MODEL), BF16), pltpu.VMEM((T, D_MODEL), F32)],
        compiler_params=_params(("arbitrary",)))(x, dres, g, w, dp)


def out_fwd(mixed, w, x, name):
    T = x.shape[0]

    def body(m_ref, w_ref, x_ref, o_ref):
        o_ref[...] = x_ref[...] + jnp.dot(m_ref[...].astype(BF16), w_ref[...], preferred_element_type=F32)

    blk = pl.BlockSpec((ROWS, D_MODEL), lambda i: (i, 0))
    return pl.pallas_call(
        body, name=name, grid=(T // ROWS,),
        in_specs=[blk, pl.BlockSpec((D_MODEL, D_MODEL), lambda i: (0, 0)), blk],
        out_specs=blk, out_shape=jax.ShapeDtypeStruct((T, D_MODEL), F32),
        compiler_params=_params(("arbitrary",)))(mixed, w, x)


def out_bwd(mixed, w, dy, name):
    T = dy.shape[0]

    def body(m_ref, w_ref, dy_ref, dm_ref, dw_ref):
        @pl.when(pl.program_id(0) == 0)
        def _():
            dw_ref[...] = jnp.zeros_like(dw_ref)
        dyb = dy_ref[...].astype(BF16)
        dm_ref[...] = _bdot_nt(dyb, w_ref[...])
        dw_ref[...] += _bdot_tn(m_ref[...].astype(BF16), dyb)

    blk = pl.BlockSpec((ROWS, D_MODEL), lambda i: (i, 0))
    sq = pl.BlockSpec((D_MODEL, D_MODEL), lambda i: (0, 0))
    return pl.pallas_call(
        body, name=name, grid=(T // ROWS,),
        in_specs=[blk, sq, blk], out_specs=(blk, sq),
        out_shape=(jax.ShapeDtypeStruct((T, D_MODEL), F32), jax.ShapeDtypeStruct((D_MODEL, D_MODEL), F32)),
        compiler_params=_params(("arbitrary",)))(mixed, w, dy)


def loss_head(x, g, target, name):
    T = x.shape[0]

    def body(x_ref, g_ref, t_ref, loss_ref, dx_ref, dg_ref):
        @pl.when(pl.program_id(0) == 0)
        def _():
            loss_ref[...] = jnp.zeros_like(loss_ref)
            dg_ref[...] = jnp.zeros_like(dg_ref)
        xb = x_ref[...]
        rstd = lax.rsqrt(jnp.mean(xb * xb, axis=-1, keepdims=True) + NORM_EPS)
        xh = xb * rstd
        err = xh * g_ref[...] - t_ref[...]
        loss_ref[...] += 0.5 * jnp.sum(jnp.mean(err * err, axis=-1, keepdims=True), axis=0, keepdims=True)
        dy = err * (1.0 / D_MODEL)
        dg_ref[...] += jnp.sum(dy * xh, axis=0, keepdims=True)
        dxh = dy * g_ref[...]
        dx_ref[...] = rstd * (dxh - xh * jnp.mean(dxh * xh, axis=-1, keepdims=True))

    blk = pl.BlockSpec((ROWS, D_MODEL), lambda i: (i, 0))
    vec = pl.BlockSpec((1, D_MODEL), lambda i: (0, 0))
    return pl.pallas_call(
        body, name=name, grid=(T // ROWS,),
        in_specs=[blk, vec, blk], out_specs=(pl.BlockSpec((1, 1), lambda i: (0, 0)), blk, vec),
        out_shape=(jax.ShapeDtypeStruct((1, 1), F32), jax.ShapeDtypeStruct((T, D_MODEL), F32),
                   jax.ShapeDtypeStruct((1, D_MODEL), F32)),
        compiler_params=_params(("arbitrary",)))(x, g, target)


def rowwise_fwd(fn, rows, shared, out_widths, name):
    T = rows[0].shape[0]
    n_in = len(rows) + len(shared)

    def body(*refs):
        res = fn(*[r[...] for r in refs[:n_in]])
        for o, v in zip(refs[n_in:], res):
            o[...] = v

    in_specs = ([pl.BlockSpec((ROWS, a.shape[1]), lambda i: (i, 0)) for a in rows]
                + [pl.BlockSpec(a.shape, lambda i: (0, 0)) for a in shared])
    return pl.pallas_call(
        body, name=name, grid=(T // ROWS,), in_specs=in_specs,
        out_specs=tuple(pl.BlockSpec((ROWS, w), lambda i: (i, 0)) for w in out_widths),
        out_shape=tuple(jax.ShapeDtypeStruct((T, w), F32) for w in out_widths),
        compiler_params=_params(("arbitrary",)))(*rows, *shared)


def rowwise_bwd(fn, rows, shared, cts, name, ct_fn=None):
    T = rows[0].shape[0]
    nr, ns, nc = len(rows), len(shared), len(cts)

    def body(*refs):
        ins = [r[...] for r in refs[:nr + ns]]
        ctv = tuple(r[...] for r in refs[nr + ns:nr + ns + nc])
        outs = refs[nr + ns + nc:]
        _, vjp = jax.vjp(fn, *ins)
        grads = vjp(ct_fn(*ctv) if ct_fn is not None else ctv)
        for k in range(nr):
            outs[k][...] = grads[k]

        @pl.when(pl.program_id(0) == 0)
        def _():
            for k in range(ns):
                outs[nr + k][...] = jnp.zeros_like(outs[nr + k])
        for k in range(ns):
            outs[nr + k][...] += grads[nr + k]

    row_spec = lambda a: pl.BlockSpec((ROWS, a.shape[1]), lambda i: (i, 0))
    sh_spec = lambda a: pl.BlockSpec(a.shape, lambda i: (0, 0))
    return pl.pallas_call(
        body, name=name, grid=(T // ROWS,),
        in_specs=[row_spec(a) for a in rows] + [sh_spec(a) for a in shared] + [row_spec(a) for a in cts],
        out_specs=tuple([row_spec(a) for a in rows] + [sh_spec(a) for a in shared]),
        out_shape=tuple(jax.ShapeDtypeStruct(a.shape, F32) for a in list(rows) + list(shared)),
        compiler_params=_params(("arbitrary",)))(*rows, *shared, *cts)


def shift_rows(x, s):
    return jnp.pad(x, ((s, 0), (0, 0)))[:x.shape[0]]


def unshift_rows(x, s):
    return jnp.pad(x, ((0, s), (0, 0)))[s:]


def _neg_expm1(y):
    series = -(y * (1.0 + y * (0.5 + y * (1.0 / 6.0 + y * (1.0 / 24.0)))))
    return jnp.where(y > -0.05, series, 1.0 - jnp.exp(y))


def lru_pre_fn(x0, x1, x2, x3, first, w0, w1, w2, w3, cb, ga, gab, gx, gxb, lam):
    xc = w3 * x0 + w2 * x1 + w1 * x2 + w0 * x3 + cb
    r = jax.nn.sigmoid(_hdot(xc, ga) + gab)
    i = jax.nn.sigmoid(_hdot(xc, gx) + gxb)
    log_a = -LRU_C * r * jax.nn.softplus(-lam)
    a = jnp.exp(log_a)
    mult = jnp.where(first > 0.5, 1.0, jnp.sqrt(_neg_expm1(2.0 * log_a)))
    return a, mult * i * xc


def lru_post_fn(h, py, og):
    return (_rms(h * jax.nn.gelu(py), og),)


def lru_scan(a, b, reverse, name):
    T, C = a.shape
    nb = T // 8

    def body(a_ref, b_ref, h_ref):
        rows = lax.broadcasted_iota(jnp.int32, (8, C), 0)

        def blk(i, carry):
            j = nb - 1 - i if reverse else i
            r = pl.ds(pl.multiple_of(j * 8, 8), 8)
            A = a_ref[r, :]
            B = b_ref[r, :]
            for s in (1, 2, 4):
                if reverse:
                    keep = rows < 8 - s
                    sh = 8 - s
                else:
                    keep = rows >= s
                    sh = s
                Bs = jnp.where(keep, pltpu.roll(B, sh, 0), 0.0)
                As = jnp.where(keep, pltpu.roll(A, sh, 0), 1.0)
                B = B + A * Bs
                A = A * As
            hb = B + A * carry
            h_ref[r, :] = hb
            edge = 0 if reverse else 7
            return jnp.sum(jnp.where(rows == edge, hb, 0.0), axis=0, keepdims=True)

        lax.fori_loop(0, nb, blk, jnp.zeros((1, C), F32))

    full = pl.BlockSpec((T, C), lambda: (0, 0))
    return pl.pallas_call(body, name=name, in_specs=[full, full], out_specs=full,
                          out_shape=jax.ShapeDtypeStruct((T, C), F32), compiler_params=_params())(a, b)


def make_rwkv_pre_fn(has_vres):
    def fn(p, pp, *rest):
        if has_vres:
            vf, mu, w_up, w_b, a_up, a_b, g_up, kk_w, ka_w, vw1, vw2, vb = rest
        else:
            mu, w_up, w_b, a_up, a_b, g_up, kk_w, ka_w = rest
        xm = p + (pp - p) * mu
        r, k, v = xm[:, 0:384], xm[:, 384:768], xm[:, 768:1152]
        xw, xa, xg = xm[:, 1152:1216], xm[:, 1216:1280], xm[:, 1280:1408]
        w_log = -jax.nn.softplus(-(w_b + _hdot(jnp.tanh(xw), w_up))) - 0.5
        lw = -jnp.exp(w_log)
        a = jax.nn.sigmoid(a_b + _hdot(xa, a_up))
        g = _hdot(jax.nn.sigmoid(xg), g_up)
        if has_vres:
            v = v + (vf - v) * jax.nn.sigmoid(vb + _hdot(_hdot(v, vw1), vw2))
        kkx = k * kk_w
        kk = kkx * lax.rsqrt(_segsum(kkx * kkx) + 1e-6)
        k2 = k * (1.0 + (a - 1.0) * ka_w)
        return r, lw, k2, v, kk, a, g
    return fn


def rwkv_post_fn(y, r, k2, v, g, ln_g, ln_b, r_k):
    mean = _segsum(y) * (1.0 / HEAD_DIM)
    yc = y - mean
    var = _segsum(yc * yc) * (1.0 / HEAD_DIM)
    yn = yc * lax.rsqrt(var + GN_EPS) * ln_g + ln_b
    bonus = _segsum(r * k2 * r_k) * v
    return ((yn + bonus) * g,)


def _head_expander(first_lane):
    ri, ci = _iota2(128, MIX_W)
    return (ri == ci // HEAD_DIM + first_lane).astype(F32)


def gdn_pre_fn(x0, x1, x2, x3, ab, w0, w1, w2, w3, alog, dtb):
    qkv = jax.nn.silu(w3 * x0 + w2 * x1 + w1 * x2 + w0 * x3)
    q, k, v = qkv[:, 0:384], qkv[:, 384:768], qkv[:, 768:1152]
    q = q * lax.rsqrt(_segsum(q * q) + 1e-6) * (HEAD_DIM ** -0.5)
    k = k * lax.rsqrt(_segsum(k * k) + 1e-6)
    g = -jnp.exp(alog) * jax.nn.softplus(ab + dtb)
    beta = jax.nn.sigmoid(ab)
    ge = _xdot(g, functools.partial(_head_expander, 0))
    be = _xdot(beta, functools.partial(_head_expander, HEADS))
    return q, k, v, ge, be


def gdn_post_fn(o, z, ng):
    ms = _segsum(o * o) * (1.0 / HEAD_DIM)
    return (o * lax.rsqrt(ms + NORM_EPS) * ng * jax.nn.silu(z),)


def _neumann_inv(m):
    n = m.shape[-1]
    ri, ci = _iota2(n, n)
    eye = (ri == ci).astype(F32)
    md = jnp.where(ri // 16 == ci // 16, m, 0.0)
    mo = m - md
    t0 = eye + md
    p2 = _hdot(md, md)
    t0 = t0 + _hdot(t0, p2)
    p4 = _hdot(p2, p2)
    t0 = t0 + _hdot(t0, p4)
    p8 = _hdot(p4, p4)
    t0 = t0 + _hdot(t0, p8)
    nn = _hdot(t0, mo)
    n2 = _hdot(nn, nn)
    t1 = eye + nn + n2 + _hdot(nn, n2)
    return _hdot(t1, t0)


@jax.custom_vjp
def _inv_saved(m, t_saved):
    return t_saved


def _inv_saved_fwd(m, t_saved):
    return t_saved, t_saved


def _inv_saved_bwd(t_saved, dt):
    tt = jnp.swapaxes(t_saved, -1, -2)
    return _hdot(_hdot(tt, dt), tt), jnp.zeros_like(t_saved)


_inv_saved.defvjp(_inv_saved_fwd, _inv_saved_bwd)


def _heads(x):
    return jnp.concatenate([x[None, :, h * HEAD_DIM:(h + 1) * HEAD_DIM] for h in range(HEADS)], axis=0)


def _unheads(y):
    return jnp.concatenate([lax.index_in_dim(y, h, 0, keepdims=False) for h in range(HEADS)], axis=1)


def rwkv_heads(s0, r, lw, k2, v, kk, a, inv):
    n = r.shape[0]
    ri, ci = _iota2(n, n)
    low, strict = ri >= ci, ri > ci
    cs = _cumsum_rows(lw)
    cl = jnp.sum(lw, axis=0, keepdims=True)
    p_in, p_prev, p_inv = jnp.exp(cs), jnp.exp(cs - lw), jnp.exp(-cs)
    p_rest, p_all = jnp.exp(cl - cs), jnp.exp(cl)
    bd = kk * a
    at, rt = _heads(-kk * p_prev), _heads(r * p_in)
    bh, kh = _heads(bd * p_inv), _heads(k2 * p_inv)
    vh = _heads(v)
    m_ab = jnp.where(strict, _cdot_nt(at, bh), 0.0)
    m_ak = jnp.where(strict, _cdot_nt(at, kh), 0.0)
    m_rb = jnp.where(low, _cdot_nt(rt, bh), 0.0)
    m_rk = jnp.where(low, _cdot_nt(rt, kh), 0.0)
    sa = _cdot(inv(m_ab), _cdot_nt(at, s0) + _cdot(m_ak, vh))
    y = _cdot_nt(rt, s0) + _cdot(m_rb, sa) + _cdot(m_rk, vh)
    s1 = s0 * _heads(p_all) + _cdot_tn(sa, _heads(bd * p_rest)) + _cdot_tn(vh, _heads(k2 * p_rest))
    return _unheads(y), s1


def gdn_heads(s0, q, k, v, ge, be, inv):
    n = q.shape[0]
    ri, ci = _iota2(n, n)
    low, strict = ri >= ci, ri > ci
    gc = _cumsum_rows(ge)
    gl = jnp.sum(ge, axis=0, keepdims=True)
    gch = _heads(gc)
    decay = jnp.where(low, jnp.exp(jnp.where(low, gch - jnp.swapaxes(gch, 1, 2), 0.0)), 0.0)
    kb = k * be
    e = jnp.exp(gc)
    kh = _heads(k)
    m = -jnp.where(strict, _cdot_nt(_heads(kb), kh) * decay, 0.0)
    mr = jnp.where(low, _cdot_nt(_heads(q), kh) * decay, 0.0)
    u = _cdot(inv(m), _heads(v * be) - _cdot_nt(_heads(kb * e), s0))
    y = _cdot_nt(_heads(q * e), s0) + _cdot(mr, u)
    s1 = s0 * _heads(jnp.exp(gl)) + _cdot_tn(u, _heads(k * jnp.exp(gl - gc)))
    return _unheads(y), s1


def core_fwd(heads_fn, ins, name):
    T = ins[0].shape[0]
    nc = T // CHUNK
    n = len(ins)

    def body(*refs):
        y_ref, s0_ref, t_ref, s_ref = refs[n:n + 4]

        @pl.when(pl.program_id(0) == 0)
        def _():
            s_ref[...] = jnp.zeros_like(s_ref)

        s0 = s_ref[...]
        kept = []

        def inv(m):
            kept.append(_neumann_inv(m))
            return kept[0]

        y, s1 = heads_fn(s0, *[r[...] for r in refs[:n]], inv)
        y_ref[...] = y
        s0_ref[0] = s0
        t_ref[0] = kept[0]
        s_ref[...] = s1

    row = pl.BlockSpec((CHUNK, MIX_W), lambda c: (c, 0))
    st_shape = (HEADS, HEAD_DIM, HEAD_DIM)
    st = pl.BlockSpec((1,) + st_shape, lambda c: (c, 0, 0, 0))
    return pl.pallas_call(
        body, name=name, grid=(nc,), in_specs=[row] * n, out_specs=(row, st, st),
        out_shape=(jax.ShapeDtypeStruct((T, MIX_W), F32), jax.ShapeDtypeStruct((nc,) + st_shape, F32),
                   jax.ShapeDtypeStruct((nc,) + st_shape, F32)),
        scratch_shapes=[pltpu.VMEM(st_shape, F32)],
        compiler_params=_params(("arbitrary",)))(*ins)


def core_bwd(heads_fn, ins, s0_all, t_all, dy, name):
    T = ins[0].shape[0]
    nc = T // CHUNK
    n = len(ins)

    def body(*refs):
        s0_ref, t_ref, dy_ref = refs[n:n + 3]
        outs = refs[n + 3:n + 3 + n]
        ds_ref = refs[n + 3 + n]

        @pl.when(pl.program_id(0) == 0)
        def _():
            ds_ref[...] = jnp.zeros_like(ds_ref)

        t_saved = t_ref[0]
        f = lambda s0, *xs: heads_fn(s0, *xs, lambda m: _inv_saved(m, t_saved))
        _, vjp = jax.vjp(f, s0_ref[0], *[r[...] for r in refs[:n]])
        grads = vjp((dy_ref[...], ds_ref[...]))
        ds_ref[...] = grads[0]
        for k in range(n):
            outs[k][...] = grads[1 + k]

    row = pl.BlockSpec((CHUNK, MIX_W), lambda c: (nc - 1 - c, 0))
    st_shape = (HEADS, HEAD_DIM, HEAD_DIM)
    st = pl.BlockSpec((1,) + st_shape, lambda c: (nc - 1 - c, 0, 0, 0))
    return pl.pallas_call(
        body, name=name, grid=(nc,), in_specs=[row] * n + [st, st, row], out_specs=tuple([row] * n),
        out_shape=tuple(jax.ShapeDtypeStruct((T, MIX_W), F32) for _ in range(n)),
        scratch_shapes=[pltpu.VMEM(st_shape, F32)],
        compiler_params=_params(("arbitrary",)))(*ins, s0_all, t_all, dy)


def _block_diag(w):
    out = jnp.zeros((LRU_W, LRU_W), w.dtype)
    for n in range(LRU_BLOCKS):
        out = lax.dynamic_update_slice(out, w[n], (n * 64, n * 64))
    return out


def _block_diag_grad(g):
    return jnp.stack([g[n * 64:(n + 1) * 64, n * 64:(n + 1) * 64] for n in range(LRU_BLOCKS)])


def _row(v):
    return v.reshape(1, -1)


def _pad128(v):
    return jnp.pad(v.reshape(1, -1), ((0, 0), (0, 128 - v.size)))


def _layer_shared(w, l):
    cw = w['lru_conv_w'][l]
    lru_pre = [_row(cw[0]), _row(cw[1]), _row(cw[2]), _row(cw[3]), _row(w['lru_conv_b'][l]),
               _block_diag(w['lru_gate_a_w'][l]), _row(w['lru_gate_a_b'][l]),
               _block_diag(w['lru_gate_x_w'][l]), _row(w['lru_gate_x_b'][l]), _row(w['lru_lambda'][l])]
    rw_pre = [_row(w['rwkv_mu'][l]), w['rwkv_w_up'][l], _row(w['rwkv_w_bias'][l]), w['rwkv_a_up'][l],
              _row(w['rwkv_a_bias'][l]), w['rwkv_g_up'][l], _row(w['rwkv_k_k'][l]), _row(w['rwkv_k_a'][l])]
    if l > 0:
        rw_pre += [w['rwkv_vres_w1'][l - 1], w['rwkv_vres_w2'][l - 1], _row(w['rwkv_vres_b'][l - 1])]
    rw_post = [_row(w['rwkv_ln_g'][l]), _row(w['rwkv_ln_b'][l]), _row(w['rwkv_r_k'][l])]
    gw = w['gdn_conv_w'][l]
    gdn_pre = [_row(gw[0]), _row(gw[1]), _row(gw[2]), _row(gw[3]), _pad128(w['gdn_a_log'][l]),
               _pad128(w['gdn_dt_bias'][l])]
    gdn_post = [_row(jnp.tile(w['gdn_norm'][l], HEADS))]
    return dict(lru_pre=lru_pre, lru_post=[_row(w['lru_out_norm'][l])], rw_pre=rw_pre, rw_post=rw_post,
                gdn_pre=gdn_pre, gdn_post=gdn_post)


def _mixer_fwd(p, sh, l, v_first):
    T = p.shape[0]
    lx, ly = p[:, 0:256], p[:, 256:512]
    prw, qkv, z, ab = p[:, 512:1920], p[:, 1920:3072], p[:, 3072:3456], p[:, 3456:3584]
    first = jnp.zeros((T, LRU_W), F32).at[0].set(1.0)
    lru_rows = [lx, shift_rows(lx, 1), shift_rows(lx, 2), shift_rows(lx, 3), first]
    a, b = rowwise_fwd(lru_pre_fn, lru_rows, sh['lru_pre'], (LRU_W, LRU_W), f"lru_pre_fwd{l}")
    hseq = lru_scan(a, b, False, f"lru_scan_fwd{l}")
    (y_lru,) = rowwise_fwd(lru_post_fn, [hseq, ly], sh['lru_post'], (LRU_W,), f"lru_post_fwd{l}")

    rw_rows = [prw, shift_rows(prw, 1)] + ([v_first] if l > 0 else [])
    rw = rowwise_fwd(make_rwkv_pre_fn(l > 0), rw_rows, sh['rw_pre'], (MIX_W,) * 7, f"rwkv_pre_fwd{l}")
    r, lw, k2, v, kk, ar, g = rw
    y_raw, rs0, rt = core_fwd(rwkv_heads, [r, lw, k2, v, kk, ar], f"rwkv_core_fwd{l}")
    (y_rw,) = rowwise_fwd(rwkv_post_fn, [y_raw, r, k2, v, g], sh['rw_post'], (MIX_W,), f"rwkv_post_fwd{l}")

    gdn_rows = [qkv, shift_rows(qkv, 1), shift_rows(qkv, 2), shift_rows(qkv, 3), ab]
    gd = rowwise_fwd(gdn_pre_fn, gdn_rows, sh['gdn_pre'], (MIX_W,) * 5, f"gdn_pre_fwd{l}")
    o_raw, gs0, gt = core_fwd(gdn_heads, list(gd), f"gdn_core_fwd{l}")
    (y_gdn,) = rowwise_fwd(gdn_post_fn, [o_raw, z], sh['gdn_post'], (MIX_W,), f"gdn_post_fwd{l}")

    mixed = jnp.concatenate([y_lru, y_rw, y_gdn], axis=1)
    saved = dict(lru_rows=lru_rows, a=a, hseq=hseq, ly=ly, rw_rows=rw_rows, rw=rw, y_raw=y_raw, rs0=rs0, rt=rt,
                 gdn_rows=gdn_rows, gd=gd, o_raw=o_raw, gs0=gs0, gt=gt, z=z)
    v_layer0 = v if l == 0 else None
    return mixed, saved, v_layer0


def _mixer_bwd(dmixed, sv, sh, l, dv_first):
    d_lru, d_rw, d_gdn = dmixed[:, 0:256], dmixed[:, 256:640], dmixed[:, 640:1024]
    gw = {}

    dh, dly, d_og = rowwise_bwd(lru_post_fn, [sv['hseq'], sv['ly']], sh['lru_post'], [d_lru], f"lru_post_bwd{l}")
    gscan = lru_scan(unshift_rows(sv['a'], 1), dh, True, f"lru_scan_bwd{l}")
    res = rowwise_bwd(lru_pre_fn, sv['lru_rows'], sh['lru_pre'], [gscan, shift_rows(sv['hseq'], 1)],
                      f"lru_pre_bwd{l}", ct_fn=lambda gs, hp: (gs * hp, gs))
    dlx = res[0] + unshift_rows(res[1], 1) + unshift_rows(res[2], 2) + unshift_rows(res[3], 3)
    dw0, dw1, dw2, dw3, dcb, dga, dgab, dgx, dgxb, dlam = res[5:]
    gw['lru_conv_w'] = jnp.concatenate([dw0, dw1, dw2, dw3], axis=0)
    gw['lru_conv_b'] = dcb[0]
    gw['lru_gate_a_w'] = _block_diag_grad(dga)
    gw['lru_gate_a_b'] = dgab.reshape(LRU_BLOCKS, 64)
    gw['lru_gate_x_w'] = _block_diag_grad(dgx)
    gw['lru_gate_x_b'] = dgxb.reshape(LRU_BLOCKS, 64)
    gw['lru_lambda'] = dlam[0]
    gw['lru_out_norm'] = d_og[0]

    r, lw, k2, v, kk, ar, g = sv['rw']
    res = rowwise_bwd(rwkv_post_fn, [sv['y_raw'], r, k2, v, g], sh['rw_post'], [d_rw], f"rwkv_post_bwd{l}")
    dy_raw, dr_p, dk2_p, dv_p, dg = res[:5]
    gw['rwkv_ln_g'], gw['rwkv_ln_b'], gw['rwkv_r_k'] = res[5][0], res[6][0], res[7].reshape(HEADS, HEAD_DIM)
    dr_c, dlw, dk2_c, dv_c, dkk, dar = core_bwd(rwkv_heads, [r, lw, k2, v, kk, ar], sv['rs0'], sv['rt'], dy_raw,
                                                 f"rwkv_core_bwd{l}")
    cts = [dr_p, dr_c, dlw, dk2_p, dk2_c, dv_p, dv_c, dkk, dar, dg]
    if l == 0:
        cts.append(dv_first)
        ct_fn = lambda a1, a2, b, c1, c2, d1, d2, e, f, gg, vf: (a1 + a2, b, c1 + c2, d1 + d2 + vf, e, f, gg)
    else:
        ct_fn = lambda a1, a2, b, c1, c2, d1, d2, e, f, gg: (a1 + a2, b, c1 + c2, d1 + d2, e, f, gg)
    res = rowwise_bwd(make_rwkv_pre_fn(l > 0), sv['rw_rows'], sh['rw_pre'], cts, f"rwkv_pre_bwd{l}", ct_fn=ct_fn)
    dprw = res[0] + unshift_rows(res[1], 1)
    nrow = len(sv['rw_rows'])
    dv_first_out = res[2] if l > 0 else None
    sg = res[nrow:]
    gw['rwkv_mu'], gw['rwkv_w_up'], gw['rwkv_w_bias'], gw['rwkv_a_up'] = sg[0][0], sg[1], sg[2][0], sg[3]
    gw['rwkv_a_bias'], gw['rwkv_g_up'], gw['rwkv_k_k'], gw['rwkv_k_a'] = sg[4][0], sg[5], sg[6][0], sg[7][0]
    if l > 0:
        gw['rwkv_vres_w1'], gw['rwkv_vres_w2'], gw['rwkv_vres_b'] = sg[8], sg[9], sg[10][0]

    do_raw, dz, d_ng = rowwise_bwd(gdn_post_fn, [sv['o_raw'], sv['z']], sh['gdn_post'], [d_gdn], f"gdn_post_bwd{l}")
    gw['gdn_norm'] = jnp.sum(d_ng.reshape(HEADS, HEAD_DIM), axis=0)
    dgd = core_bwd(gdn_heads, list(sv['gd']), sv['gs0'], sv['gt'], do_raw, f"gdn_core_bwd{l}")
    res = rowwise_bwd(gdn_pre_fn, sv['gdn_rows'], sh['gdn_pre'], list(dgd), f"gdn_pre_bwd{l}")
    dqkv = res[0] + unshift_rows(res[1], 1) + unshift_rows(res[2], 2) + unshift_rows(res[3], 3)
    dab = res[4]
    gw['gdn_conv_w'] = jnp.concatenate(res[5:9], axis=0)
    gw['gdn_a_log'], gw['gdn_dt_bias'] = res[9][0, :HEADS], res[10][0, :HEADS]

    dp = jnp.concatenate([dlx, dly, dprw, dqkv, dz, dab], axis=1)
    return dp, gw, dv_first_out


IN_SHARD = D_IN // N_CHIPS
IN_SHARD_PAD = D_IN_PAD // N_CHIPS


def _cols_to_chips(g, n=N_CHIPS):
    r = g.shape[0]
    return jnp.transpose(g.reshape(r, n, -1), (1, 0, 2))


def _cols_from_chips(g):
    return jnp.transpose(g, (1, 0, 2)).reshape(g.shape[1], -1)


def _w_in_from_chips(g):
    nat = _cols_from_chips(g[:, :, :IN_SHARD])
    return jnp.pad(nat, ((0, 0), (0, D_IN_PAD - D_IN)))


def _w_in_to_chips(g):
    return jnp.pad(_cols_to_chips(g[:, :D_IN]), ((0, 0), (0, 0), (0, IN_SHARD_PAD - IN_SHARD)))


def local_step(x, target, w, wb):
    saved = []
    v_first = None
    for l in range(N_LAYERS):
        sh = _layer_shared(w, l)
        x1 = ffn_fwd(x, _row(w['ffn1_norm'][l]), wb['ffn1_wi'][l], wb['ffn1_wo'][l], f"ffn1_fwd{l}")
        p = proj_fwd(x1, _row(w['mix_norm'][l]), wb['w_in'][l], f"proj_fwd{l}")
        mixed, sv, v0 = _mixer_fwd(p, sh, l, v_first)
        if l == 0:
            v_first = v0
        x2 = out_fwd(mixed, wb['w_out'][l], x1, f"out_fwd{l}")
        x3 = ffn_fwd(x2, _row(w['ffn2_norm'][l]), wb['ffn2_wi'][l], wb['ffn2_wo'][l], f"ffn2_fwd{l}")
        saved.append(dict(x0=x, x1=x1, x2=x2, mixed=mixed, sv=sv, sh=sh))
        x = x3

    loss, dx, dgf = loss_head(x, _row(w['final_norm']), target, "loss_head")
    per_layer = [None] * N_LAYERS
    dv_first = jnp.zeros((x.shape[0], MIX_W), F32)
    for l in reversed(range(N_LAYERS)):
        s = saved[l]
        gw = {}
        dx, dg2, dwg, dwu, dwo = ffn_bwd(s['x2'], dx, _row(w['ffn2_norm'][l]), wb['ffn2_wi'][l], wb['ffn2_wo'][l],
                                         f"ffn2_bwd{l}")
        wi_parts = lambda dwg, dwu: jnp.concatenate([_cols_to_chips(dwg, 2), _cols_to_chips(dwu, 2)], axis=0)
        row_parts = lambda dw: dw.reshape(N_CHIPS, -1, dw.shape[1])
        gw['ffn2_norm'], gw['ffn2_wi'], gw['ffn2_wo'] = dg2[0], wi_parts(dwg, dwu), row_parts(dwo)
        dmixed, dw_out = out_bwd(s['mixed'], wb['w_out'][l], dx, f"out_bwd{l}")
        gw['w_out'] = row_parts(dw_out)
        dp, gmix, dvf = _mixer_bwd(dmixed, s['sv'], s['sh'], l, dv_first)
        if l > 0:
            dv_first = dvf
        gw.update(gmix)
        dx, dgm, dwin = proj_bwd(s['x1'], dx, _row(w['mix_norm'][l]), wb['w_in'][l], dp, f"proj_bwd{l}")
        gw['mix_norm'], gw['w_in'] = dgm[0], _w_in_to_chips(dwin)
        dx, dg1, dwg, dwu, dwo = ffn_bwd(s['x0'], dx, _row(w['ffn1_norm'][l]), wb['ffn1_wi'][l], wb['ffn1_wo'][l],
                                         f"ffn1_bwd{l}")
        gw['ffn1_norm'], gw['ffn1_wi'], gw['ffn1_wo'] = dg1[0], wi_parts(dwg, dwu), row_parts(dwo)
        per_layer[l] = gw

    grads = {'final_norm': dgf[0]}
    for name in WEIGHTS:
        if name == 'final_norm':
            continue
        if name in BIG:
            grads[name] = [per_layer[l][name] for l in range(N_LAYERS)]
        elif name.startswith('rwkv_vres'):
            grads[name] = per_layer[1][name][None]
        else:
            grads[name] = jnp.stack([per_layer[l][name] for l in range(N_LAYERS)])
    return loss[0, 0], dx, grads


ANY = pl.BlockSpec(memory_space=pl.ANY)


def _coords():
    return lax.axis_index("x"), lax.axis_index("y"), lax.axis_index("c")


def _other_chips(x, y):
    return [((x + 1) % 2, y), (x, (y + 1) % 2), ((x + 1) % 2, (y + 1) % 2)]


def allreduce_small(pack, name):
    R = pack.shape[0]

    def body(x_ref, o_ref, buf, send_sems, recv_sems):
        x, y, c = _coords()
        me = 4 * x + 2 * y + c
        buf[me] = x_ref[...]
        copies = []
        for k in range(1, 8):
            peer = ((x + (k >> 2)) % 2, (y + ((k >> 1) & 1)) % 2, (c + (k & 1)) % 2)
            cp = pltpu.make_async_remote_copy(src_ref=x_ref, dst_ref=buf.at[me], send_sem=send_sems.at[k - 1],
                                              recv_sem=recv_sems.at[k - 1], device_id=peer, device_id_type=MESH)
            cp.start()
            copies.append(cp)
        for cp in copies:
            cp.wait()
        acc = buf[0]
        for d in range(1, 8):
            acc = acc + buf[d]
        o_ref[...] = acc

    vm = pl.BlockSpec(memory_space=pltpu.VMEM)
    return pl.pallas_call(
        body, name=name, in_specs=[vm], out_specs=vm, out_shape=jax.ShapeDtypeStruct((R, 128), F32),
        scratch_shapes=[pltpu.VMEM((8, R, 128), F32), pltpu.SemaphoreType.DMA((7,)), pltpu.SemaphoreType.DMA((7,))],
        compiler_params=_params())(pack)


def allgather_chips(shards, name):
    n = len(shards)

    def body(*refs):
        x_refs, o_refs = refs[:n], refs[n:2 * n]
        send_sems, recv_sems, local_sems = refs[2 * n:]
        x, y, c = _coords()
        s_me = 2 * x + y
        sib = (x, y, 1 - c)
        chips = _other_chips(x, y)

        def rows(a, core):
            rh = shards[a].shape[0] // 2
            return pl.ds(pl.multiple_of(core * rh, 16), rh)

        def copy(a, k, src, dst, to):
            return pltpu.make_async_remote_copy(src_ref=src, dst_ref=dst, send_sem=send_sems.at[6 * a + k],
                                                recv_sem=recv_sems.at[6 * a + k], device_id=to, device_id_type=MESH)

        own, sent, passed = [], [], []
        for a in range(n):
            cp = pltpu.make_async_copy(x_refs[a], o_refs[a].at[s_me], local_sems.at[a])
            cp.start()
            own.append(cp)
        for j, (px, py) in enumerate(chips):
            for a in range(n):
                cp = copy(a, j, x_refs[a].at[rows(a, c)], o_refs[a].at[s_me, rows(a, c)], (px, py, c))
                cp.start()
                sent.append(cp)
        for j, (px, py) in enumerate(chips):
            for a in range(n):
                part = o_refs[a].at[2 * px + py, rows(a, c)]
                copy(a, j, part, part, (px, py, c)).wait_recv()
                fw = copy(a, 3 + j, part, part, sib)
                fw.start()
                passed.append(fw)
        for j, (px, py) in enumerate(chips):
            for a in range(n):
                part = o_refs[a].at[2 * px + py, rows(a, 1 - c)]
                copy(a, 3 + j, part, part, sib).wait_recv()
        for cp in sent + passed:
            cp.wait_send()
        for cp in own:
            cp.wait()

    return pl.pallas_call(
        body, name=name, in_specs=[ANY] * n, out_specs=tuple([ANY] * n),
        out_shape=tuple(jax.ShapeDtypeStruct((N_CHIPS,) + s.shape, s.dtype) for s in shards),
        scratch_shapes=[pltpu.SemaphoreType.DMA((6 * n,)), pltpu.SemaphoreType.DMA((6 * n,)),
                        pltpu.SemaphoreType.DMA((n,))],
        compiler_params=_params())(*shards)


def sibling_swap(srcs, halves, name):
    n = len(srcs)
    out_shapes = [(s.shape[0], s.shape[1] // 2, s.shape[2]) if halves else s.shape for s in srcs]

    def body(*refs):
        x_refs, o_refs = refs[:n], refs[n:2 * n]
        send_sems, recv_sems = refs[2 * n:]
        x, y, c = _coords()
        copies = []
        for a in range(n):
            part = x_refs[a]
            if halves:
                rh = srcs[a].shape[1] // 2
                part = part.at[:, pl.ds(pl.multiple_of((1 - c) * rh, 16), rh)]
            cp = pltpu.make_async_remote_copy(src_ref=part, dst_ref=o_refs[a], send_sem=send_sems.at[a],
                                              recv_sem=recv_sems.at[a], device_id=(x, y, 1 - c), device_id_type=MESH)
            cp.start()
            copies.append(cp)
        for cp in copies:
            cp.wait()

    return pl.pallas_call(
        body, name=name, in_specs=[ANY] * n, out_specs=tuple([ANY] * n),
        out_shape=tuple(jax.ShapeDtypeStruct(sh, s.dtype) for sh, s in zip(out_shapes, srcs)),
        scratch_shapes=[pltpu.SemaphoreType.DMA((n,)), pltpu.SemaphoreType.DMA((n,))],
        compiler_params=_params())(*srcs)


def scatter_chips(parts, name):
    n = len(parts)

    def body(*refs):
        x_refs, o_refs = refs[:n], refs[n:2 * n]
        send_sems, recv_sems = refs[2 * n:]
        x, y, c = _coords()
        copies = []
        for j, (px, py) in enumerate(_other_chips(x, y)):
            for a in range(n):
                cp = pltpu.make_async_remote_copy(src_ref=x_refs[a].at[2 * px + py], dst_ref=o_refs[a].at[j],
                                                  send_sem=send_sems.at[3 * a + j], recv_sem=recv_sems.at[3 * a + j],
                                                  device_id=(px, py, c), device_id_type=MESH)
                cp.start()
                copies.append(cp)
        for cp in copies:
            cp.wait()

    return pl.pallas_call(
        body, name=name, in_specs=[ANY] * n, out_specs=tuple([ANY] * n),
        out_shape=tuple(jax.ShapeDtypeStruct((3,) + p.shape[1:], p.dtype) for p in parts),
        scratch_shapes=[pltpu.SemaphoreType.DMA((3 * n,)), pltpu.SemaphoreType.DMA((3 * n,))],
        compiler_params=_params())(*parts)


def _row_block(rows):
    return max(b for b in range(16, 257, 16) if rows % b == 0)


def chip_sum(gpack, recv, core, name):
    n, R, W = gpack.shape
    rh = R // 2
    rb = _row_block(rh)
    nb = rh // rb

    def body(c_ref, g_ref, r_ref, o_ref, ob_ref):
        s = g_ref[...] + r_ref[...]
        o_ref[...] = s
        ob_ref[...] = s.astype(BF16)

    blk = pl.BlockSpec((1, rb, W), lambda i, j, c_ref: (i, j, 0))
    spec = pltpu.PrefetchScalarGridSpec(
        num_scalar_prefetch=1, grid=(n, nb),
        in_specs=[pl.BlockSpec((1, rb, W), lambda i, j, c_ref: (i, c_ref[0] * nb + j, 0)), blk],
        out_specs=(blk, blk))
    return pl.pallas_call(
        body, name=name, grid_spec=spec,
        out_shape=(jax.ShapeDtypeStruct((n, rh, W), F32), jax.ShapeDtypeStruct((n, rh, W), BF16)),
        compiler_params=_params(("arbitrary", "arbitrary")))(core, gpack, recv)


def shard_sum(own, recv, name):
    R, W = own.shape
    rb = _row_block(R)

    def body(a_ref, r_ref, o_ref):
        acc = a_ref[...]
        for j in range(3):
            acc = acc + r_ref[j].astype(F32)
        o_ref[...] = acc

    return pl.pallas_call(
        body, name=name, grid=(R // rb,),
        in_specs=[pl.BlockSpec((rb, W), lambda i: (i, 0)), pl.BlockSpec((3, rb, W), lambda i: (0, i, 0))],
        out_specs=pl.BlockSpec((rb, W), lambda i: (i, 0)), out_shape=jax.ShapeDtypeStruct((R, W), F32),
        compiler_params=_params(("arbitrary",)))(own, recv)


def adamw(w, m, v, g, name):
    R, C = w.shape
    rb = 128 if R % 128 == 0 else R
    bc1 = 1.0 - ADAM_B1 ** ADAM_STEP
    bc2 = 1.0 - ADAM_B2 ** ADAM_STEP

    def body(w_ref, m_ref, v_ref, g_ref, d_ref, nm_ref, nv_ref):
        gv = g_ref[...]
        nm = ADAM_B1 * m_ref[...] + (1.0 - ADAM_B1) * gv
        nv = ADAM_B2 * v_ref[...] + (1.0 - ADAM_B2) * (gv * gv)
        d_ref[...] = -ADAM_LR * ((nm / bc1) / (jnp.sqrt(nv / bc2) + ADAM_EPS) + ADAM_WD * w_ref[...])
        nm_ref[...] = nm
        nv_ref[...] = nv

    blk = pl.BlockSpec((rb, C), lambda i: (i, 0))
    sh = jax.ShapeDtypeStruct((R, C), F32)
    return pl.pallas_call(body, name=name, grid=(R // rb,), in_specs=[blk] * 4, out_specs=(blk,) * 3,
                          out_shape=(sh, sh, sh), compiler_params=_params(("arbitrary",)))(w, m, v, g)


SMALL = [n for n in WEIGHTS if n not in BIG]


def _pack_rows(flat, width, row_multiple):
    n = flat.shape[-1]
    per = width * row_multiple
    total = -(-n // per) * per
    flat = jnp.pad(flat, [(0, 0)] * (flat.ndim - 1) + [(0, total - n)])
    return flat.reshape(flat.shape[:-1] + (total // width, width))


def _pad_lanes(a):
    return jnp.pad(a, ((0, 0), (0, -a.shape[1] % 128)))


def _local_shard(full, axis, chip):
    size = full.shape[axis] // N_CHIPS
    return lax.dynamic_slice_in_dim(full, chip * size, size, axis)


def kernel(x, ffn1_norm, ffn1_wi, ffn1_wo, mix_norm, w_in, w_out, lru_conv_w, lru_conv_b, lru_gate_a_w, lru_gate_a_b, lru_gate_x_w, lru_gate_x_b, lru_lambda, lru_out_norm, rwkv_mu, rwkv_w_up, rwkv_w_bias, rwkv_a_up, rwkv_a_bias, rwkv_g_up, rwkv_k_k, rwkv_k_a, rwkv_r_k, rwkv_ln_g, rwkv_ln_b, rwkv_vres_w1, rwkv_vres_w2, rwkv_vres_b, gdn_conv_w, gdn_a_log, gdn_dt_bias, gdn_norm, ffn2_norm, ffn2_wi, ffn2_wo, final_norm, loss_target, m_ffn1_norm, m_ffn1_wi, m_ffn1_wo, m_mix_norm, m_w_in, m_w_out, m_lru_conv_w, m_lru_conv_b, m_lru_gate_a_w, m_lru_gate_a_b, m_lru_gate_x_w, m_lru_gate_x_b, m_lru_lambda, m_lru_out_norm, m_rwkv_mu, m_rwkv_w_up, m_rwkv_w_bias, m_rwkv_a_up, m_rwkv_a_bias, m_rwkv_g_up, m_rwkv_k_k, m_rwkv_k_a, m_rwkv_r_k, m_rwkv_ln_g, m_rwkv_ln_b, m_rwkv_vres_w1, m_rwkv_vres_w2, m_rwkv_vres_b, m_gdn_conv_w, m_gdn_a_log, m_gdn_dt_bias, m_gdn_norm, m_ffn2_norm, m_ffn2_wi, m_ffn2_wo, m_final_norm, v_ffn1_norm, v_ffn1_wi, v_ffn1_wo, v_mix_norm, v_w_in, v_w_out, v_lru_conv_w, v_lru_conv_b, v_lru_gate_a_w, v_lru_gate_a_b, v_lru_gate_x_w, v_lru_gate_x_b, v_lru_lambda, v_lru_out_norm, v_rwkv_mu, v_rwkv_w_up, v_rwkv_w_bias, v_rwkv_a_up, v_rwkv_a_bias, v_rwkv_g_up, v_rwkv_k_k, v_rwkv_k_a, v_rwkv_r_k, v_rwkv_ln_g, v_rwkv_ln_b, v_rwkv_vres_w1, v_rwkv_vres_w2, v_rwkv_vres_b, v_gdn_conv_w, v_gdn_a_log, v_gdn_dt_bias, v_gdn_norm, v_ffn2_norm, v_ffn2_wi, v_ffn2_wo, v_final_norm):
    args = locals()
    w_loc = {n: args[n] for n in WEIGHTS}
    m_loc = {n: args['m_' + n] for n in WEIGHTS}
    v_loc = {n: args['v_' + n] for n in WEIGHTS}
    chip = 2 * lax.axis_index("x") + lax.axis_index("y")
    core = lax.axis_index("c")

    big = [(n, l) for n in BIG for l in range(N_LAYERS)]
    gathered = allgather_chips([_pad_lanes(w_loc[n][l].astype(BF16)) for n, l in big], "allgather_big")
    wb = {n: [None] * N_LAYERS for n in BIG}
    for (n, l), g in zip(big, gathered):
        if n == 'w_in':
            wb[n][l] = _w_in_from_chips(g)
        elif BIG[n] == 2:
            wb[n][l] = _cols_from_chips(g)
        else:
            wb[n][l] = g.reshape(-1, g.shape[2])

    sm_names = list(SMALL_SHARDED)
    placed = []
    for n in sm_names:
        ax = SMALL_SHARDED[n]
        full_shape = w_loc[n].shape[:ax] + (N_CHIPS * w_loc[n].shape[ax],) + w_loc[n].shape[ax + 1:]
        src = w_loc[n] * (core == 0).astype(F32)
        placed.append(lax.dynamic_update_slice_in_dim(jnp.zeros(full_shape, F32), src, chip * w_loc[n].shape[ax], ax))
    summed = allreduce_small(_pack_rows(jnp.concatenate([p.reshape(-1) for p in placed]), 128, 8),
                             "allgather_small").reshape(-1)
    w_full, off = dict(w_loc), 0
    for n, p in zip(sm_names, placed):
        w_full[n] = summed[off:off + p.size].reshape(p.shape)
        off += p.size

    loss, dx, grads = local_step(x[0], loss_target[0], w_full, wb)
    loss = lax.psum(loss, ("x", "y", "c"))

    gsum = allreduce_small(_pack_rows(jnp.concatenate([grads[n].reshape(-1) for n in SMALL]), 128, 8),
                           "allreduce_small").reshape(-1)
    g_loc, off = {}, 0
    for n in SMALL:
        g = gsum[off:off + grads[n].size].reshape(grads[n].shape)
        off += grads[n].size
        g_loc[n] = _local_shard(g, SMALL_SHARDED[n], chip) if n in SMALL_SHARDED else g

    parts = [grads[n][l] for n, l in big]
    from_sib = sibling_swap(parts, True, "grad_swap_cores")
    core_arg = core.reshape(1).astype(jnp.int32)
    sums = [chip_sum(p, r, core_arg, f"grad_chip_sum_{n}{l}") for (n, l), p, r in zip(big, parts, from_sib)]
    from_chips = scatter_chips([s[1] for s in sums], "grad_scatter")
    halves = [shard_sum(lax.dynamic_index_in_dim(s[0], chip, 0, keepdims=False), r, f"grad_shard_sum_{n}{l}")
              for (n, l), s, r in zip(big, sums, from_chips)]
    others = sibling_swap(halves, False, "grad_share_cores")
    rows = {n: [None] * N_LAYERS for n in BIG}
    for (n, l), half, other in zip(big, halves, others):
        lower = jnp.where(core == 0, half, other)
        upper = jnp.where(core == 0, other, half)
        rows[n][l] = jnp.concatenate([lower, upper], axis=0)[:, :w_loc[n].shape[-1]]
    big_names = list(BIG)
    for n in big_names:
        g_loc[n] = jnp.stack(rows[n])

    delta, new_m, new_v = {}, {}, {}
    for n in big_names:
        shp = w_loc[n].shape
        two_d = lambda a: a.reshape(-1, shp[-1])
        d, nm, nv = adamw(two_d(w_loc[n]), two_d(m_loc[n]), two_d(v_loc[n]), two_d(g_loc[n]), f"adamw_{n}")
        delta[n], new_m[n], new_v[n] = d.reshape(shp), nm.reshape(shp), nv.reshape(shp)
    pack = lambda d: _pack_rows(jnp.concatenate([d[n].reshape(-1) for n in SMALL]), 128, 128)
    res = adamw(pack(w_loc), pack(m_loc), pack(v_loc), pack(g_loc), "adamw_small")
    off = 0
    for n in SMALL:
        size = w_loc[n].size
        for dst, r in zip((delta, new_m, new_v), res):
            dst[n] = r.reshape(-1)[off:off + size].reshape(w_loc[n].shape)
        off += size

    return (loss, dx[None], *[g_loc[n] for n in WEIGHTS], *[delta[n] for n in WEIGHTS],
            *[new_m[n] for n in WEIGHTS], *[new_v[n] for n in WEIGHTS])
```

```python
import functools

import numpy as np
import jax
import jax.numpy as jnp
from jax import lax
from jax.experimental import pallas as pl
from jax.experimental.pallas import tpu as pltpu

F32 = jnp.float32
BF16 = jnp.bfloat16
HIGHEST = lax.Precision.HIGHEST
MESH = pl.DeviceIdType.MESH

D_MODEL = 1024
D_FF = 2816
N_LAYERS = 2
HEADS = 6
HEAD_DIM = 64
MIX_W = HEADS * HEAD_DIM
LRU_W = 256
LRU_BLOCKS = 4
RWKV_IN = 1408
D_IN = 3468
D_IN_PAD = 3584
NORM_EPS = 1e-6
GN_EPS = 64e-5
LRU_C = 8.0
CHUNK = 64
ROWS = 256
FF_CHUNK = 256
IN_CHUNK = 512
PACK_W = 1024
VMEM_LIMIT = 56 * 1024 * 1024

ADAM_LR, ADAM_B1, ADAM_B2, ADAM_EPS, ADAM_WD, ADAM_STEP = 0.001, 0.9, 0.999, 1e-08, 0.01, 10

WEIGHTS = ['ffn1_norm', 'ffn1_wi', 'ffn1_wo', 'mix_norm', 'w_in', 'w_out', 'lru_conv_w', 'lru_conv_b',
           'lru_gate_a_w', 'lru_gate_a_b', 'lru_gate_x_w', 'lru_gate_x_b', 'lru_lambda', 'lru_out_norm',
           'rwkv_mu', 'rwkv_w_up', 'rwkv_w_bias', 'rwkv_a_up', 'rwkv_a_bias', 'rwkv_g_up', 'rwkv_k_k',
           'rwkv_k_a', 'rwkv_r_k', 'rwkv_ln_g', 'rwkv_ln_b', 'rwkv_vres_w1', 'rwkv_vres_w2', 'rwkv_vres_b',
           'gdn_conv_w', 'gdn_a_log', 'gdn_dt_bias', 'gdn_norm', 'ffn2_norm', 'ffn2_wi', 'ffn2_wo', 'final_norm']
BIG = {'ffn1_wi': 2, 'ffn1_wo': 1, 'w_in': 2, 'w_out': 1, 'ffn2_wi': 2, 'ffn2_wo': 1}
SMALL_SHARDED = {'lru_conv_w': 2, 'rwkv_w_up': 2, 'rwkv_a_up': 2, 'rwkv_g_up': 2, 'rwkv_vres_w1': 1,
                 'rwkv_vres_w2': 2, 'gdn_conv_w': 2}
N_CHIPS = 4


def _params(sem=None):
    kw = dict(vmem_limit_bytes=VMEM_LIMIT)
    if sem is not None:
        kw['dimension_semantics'] = sem
    return pltpu.CompilerParams(**kw)


def _bdot(a, b, dims=(((1,), (0,)), ((), ()))):
    return lax.dot_general(a.astype(BF16), b.astype(BF16), dims, preferred_element_type=F32)


def _bdot_nt(a, b):
    return _bdot(a, b, (((1,), (1,)), ((), ())))


def _bdot_tn(a, b):
    return _bdot(a, b, (((0,), (0,)), ((), ())))


_DIMS = {'nn': (((1,), (0,)), ((), ())), 'nt': (((1,), (1,)), ((), ())), 'tn': (((0,), (0,)), ((), ()))}


def _split(a, terms):
    parts = []
    for _ in range(terms - 1):
        hi = a.astype(BF16)
        parts.append(hi)
        a = a - hi.astype(F32)
    parts.append(a.astype(BF16))
    return parts


_BATCH_DIMS = {'nn': (((2,), (1,)), ((0,), (0,))), 'nt': (((2,), (2,)), ((0,), (0,))),
               'tn': (((1,), (1,)), ((0,), (0,)))}


def _dot3(a, b, kind):
    ah, al = _split(a, 2)
    bh, bl = _split(b, 2)
    dims = _BATCH_DIMS[kind] if a.ndim == 3 else _DIMS[kind]
    d = lambda p, q: lax.dot_general(p, q, dims, preferred_element_type=F32)
    return d(ah, bh) + (d(ah, bl) + d(al, bh))


@functools.partial(jax.custom_vjp, nondiff_argnums=(2,))
def _cdot_k(a, b, kind):
    return _dot3(a, b, kind)


def _cdot_k_fwd(a, b, kind):
    return _dot3(a, b, kind), (a, b)


def _cdot_k_bwd(kind, res, ct):
    a, b = res
    if kind == 'nn':
        return _dot3(ct, b, 'nt'), _dot3(a, ct, 'tn')
    if kind == 'nt':
        return _dot3(ct, b, 'nn'), _dot3(ct, a, 'tn')
    return _dot3(b, ct, 'nt'), _dot3(a, ct, 'nn')


_cdot_k.defvjp(_cdot_k_fwd, _cdot_k_bwd)


def _cdot(a, b):
    return _cdot_k(a, b, 'nn')


def _cdot_nt(a, b):
    return _cdot_k(a, b, 'nt')


def _cdot_tn(a, b):
    return _cdot_k(a, b, 'tn')


def _hdot(a, b):
    return _cdot_k(a, b, 'nn')


def _dot_exact(x, m01, kind):
    d = lambda p: lax.dot_general(p, m01.astype(BF16), _DIMS[kind], preferred_element_type=F32)
    hi, mid, lo = _split(x, 3)
    return d(hi) + (d(mid) + d(lo))


@functools.partial(jax.custom_vjp, nondiff_argnums=(1,))
def _xdot(x, make_m):
    return _dot_exact(x, make_m(), 'nn')


def _xdot_fwd(x, make_m):
    return _dot_exact(x, make_m(), 'nn'), None


def _xdot_bwd(make_m, _, ct):
    return (_dot_exact(ct, make_m(), 'nt'),)


_xdot.defvjp(_xdot_fwd, _xdot_bwd)


def _iota2(n, m):
    return lax.broadcasted_iota(jnp.int32, (n, m), 0), lax.broadcasted_iota(jnp.int32, (n, m), 1)


def _head_blocks(w):
    ri, ci = _iota2(w, w)
    return (ri // HEAD_DIM == ci // HEAD_DIM).astype(F32)


def _segsum(x):
    return _xdot(x, functools.partial(_head_blocks, x.shape[-1]))


def _cumsum_rows(x):
    return _cumsum_k(x, x.shape[0])


@functools.partial(jax.custom_vjp, nondiff_argnums=(1,))
def _cumsum_k(x, n):
    return _lower_dot(x, n, False)


def _lower_dot(x, n, transpose):
    ri, ci = _iota2(n, n)
    m = ((ri <= ci) if transpose else (ri >= ci)).astype(BF16)
    d = lambda p: lax.dot_general(m, p, _DIMS['nn'], preferred_element_type=F32)
    hi, mid, lo = _split(x, 3)
    return d(hi) + (d(mid) + d(lo))


def _cumsum_k_fwd(x, n):
    return _lower_dot(x, n, False), None


def _cumsum_k_bwd(n, _, ct):
    return (_lower_dot(ct, n, True),)


_cumsum_k.defvjp(_cumsum_k_fwd, _cumsum_k_bwd)


def _rms(x, g):
    return x * lax.rsqrt(jnp.mean(x * x, axis=-1, keepdims=True) + NORM_EPS) * g


DENSE_ROWS = 1024


def _row_loop(n_rows, fn):
    rows = min(DENSE_ROWS, n_rows)

    def step(i, c):
        fn(pl.ds(pl.multiple_of(i * rows, rows), rows))
        return c
    lax.fori_loop(0, n_rows // rows, step, 0)


def ffn_fwd(x, g, wi, wo, name, hosted=()):
    T = x.shape[0]
    nj = D_FF // FF_CHUNK
    gather = ChipGather(list(hosted))
    n = gather.n

    def body(*refs):
        x_ref, g_ref, wg_ref, wu_ref, wo_ref = refs[:5]
        hx, o_ref, ho = refs[5:5 + n], refs[5 + n], refs[6 + n:6 + 2 * n]
        h_ref, acc_ref = refs[6 + 2 * n:8 + 2 * n]
        sems = refs[8 + 2 * n:]
        j = pl.program_id(0)

        @pl.when(j == 0)
        def _():
            gather.start(hx, ho, sems)

            def init(r):
                h_ref[r, :] = _rms(x_ref[r, :], g_ref[...]).astype(BF16)
                acc_ref[r, :] = jnp.zeros((r.size, D_MODEL), F32)
            _row_loop(T, init)

        def blk(r):
            hb = h_ref[r, :]
            gate = jnp.dot(hb, wg_ref[...], preferred_element_type=F32)
            up = jnp.dot(hb, wu_ref[...], preferred_element_type=F32)
            a = (gate * jax.nn.sigmoid(gate) * up).astype(BF16)
            acc_ref[r, :] += jnp.dot(a, wo_ref[...], preferred_element_type=F32)
        _row_loop(T, blk)

        @pl.when(j == nj // 2)
        def _():
            gather.relay(hx, ho, sems)

        @pl.when(j == nj - 1)
        def _():
            def fin(r):
                o_ref[r, :] = x_ref[r, :] + 0.5 * acc_ref[r, :]
            _row_loop(T, fin)
            gather.finish(hx, ho, sems)

    full = pl.BlockSpec((T, D_MODEL), lambda j: (0, 0))
    res = pl.pallas_call(
        body, name=name, grid=(nj,),
        in_specs=[full, pl.BlockSpec((1, D_MODEL), lambda j: (0, 0)),
                  pl.BlockSpec((D_MODEL, FF_CHUNK), lambda j: (0, j)),
                  pl.BlockSpec((D_MODEL, FF_CHUNK), lambda j: (0, j + nj)),
                  pl.BlockSpec((FF_CHUNK, D_MODEL), lambda j: (j, 0))] + gather.in_specs,
        out_specs=tuple([full] + gather.out_specs),
        out_shape=tuple([jax.ShapeDtypeStruct((T, D_MODEL), F32)] + gather.out_shape),
        scratch_shapes=[pltpu.VMEM((T, D_MODEL), BF16), pltpu.VMEM((T, D_MODEL), F32)] + gather.scratch,
        compiler_params=_params(("arbitrary",)))(x, g, wi, wi, wo, *hosted)
    return res[0], list(res[1:])


def _norm_bwd_rows(x, g, dh, dres):
    rstd = lax.rsqrt(jnp.mean(x * x, axis=-1, keepdims=True) + NORM_EPS)
    xh = x * rstd
    dxh = dh * g
    dx = rstd * (dxh - xh * jnp.mean(dxh * xh, axis=-1, keepdims=True))
    return dres + dx, jnp.sum(dh * xh, axis=0, keepdims=True)


def ffn_bwd(x, dy, g, wi, wo, name):
    T = x.shape[0]
    nj = D_FF // FF_CHUNK

    def body(x_ref, dy_ref, g_ref, wg_ref, wu_ref, wo_ref, dx_ref, dg_ref, dwg_ref, dwu_ref, dwo_ref,
             h_ref, da_ref, dh_ref):
        j = pl.program_id(0)

        @pl.when(j == 0)
        def _():
            def init(r):
                h_ref[r, :] = _rms(x_ref[r, :], g_ref[...]).astype(BF16)
                da_ref[r, :] = (0.5 * dy_ref[r, :]).astype(BF16)
                dh_ref[r, :] = jnp.zeros((r.size, D_MODEL), F32)
            _row_loop(T, init)

        dwg_ref[...] = jnp.zeros_like(dwg_ref)
        dwu_ref[...] = jnp.zeros_like(dwu_ref)
        dwo_ref[...] = jnp.zeros_like(dwo_ref)

        def blk(r):
            hb = h_ref[r, :]
            db = da_ref[r, :]
            gate = jnp.dot(hb, wg_ref[...], preferred_element_type=F32)
            up = jnp.dot(hb, wu_ref[...], preferred_element_type=F32)
            sg = jax.nn.sigmoid(gate)
            sl = gate * sg
            da = _bdot_nt(db, wo_ref[...])
            dup = (da * sl).astype(BF16)
            dgate = (da * up * (sg * (1.0 + gate * (1.0 - sg)))).astype(BF16)
            dwo_ref[...] += _bdot_tn((sl * up).astype(BF16), db)
            dwg_ref[...] += _bdot_tn(hb, dgate)
            dwu_ref[...] += _bdot_tn(hb, dup)
            dh_ref[r, :] += _bdot_nt(dgate, wg_ref[...]) + _bdot_nt(dup, wu_ref[...])
        _row_loop(T, blk)

        @pl.when(j == nj - 1)
        def _():
            dg_ref[...] = jnp.zeros_like(dg_ref)

            def fin(r):
                dx, dg = _norm_bwd_rows(x_ref[r, :], g_ref[...], dh_ref[r, :], dy_ref[r, :])
                dx_ref[r, :] = dx
                dg_ref[...] += dg
            _row_loop(T, fin)

    full = pl.BlockSpec((T, D_MODEL), lambda j: (0, 0))
    vec = pl.BlockSpec((1, D_MODEL), lambda j: (0, 0))
    return pl.pallas_call(
        body, name=name, grid=(nj,),
        in_specs=[full, full, vec,
                  pl.BlockSpec((D_MODEL, FF_CHUNK), lambda j: (0, j)),
                  pl.BlockSpec((D_MODEL, FF_CHUNK), lambda j: (0, j + nj)),
                  pl.BlockSpec((FF_CHUNK, D_MODEL), lambda j: (j, 0))],
        out_specs=(full, vec,
                   pl.BlockSpec((D_MODEL, FF_CHUNK), lambda j: (0, j)),
                   pl.BlockSpec((D_MODEL, FF_CHUNK), lambda j: (0, j)),
                   pl.BlockSpec((FF_CHUNK, D_MODEL), lambda j: (j, 0))),
        out_shape=(jax.ShapeDtypeStruct((T, D_MODEL), F32), jax.ShapeDtypeStruct((1, D_MODEL), F32),
                   jax.ShapeDtypeStruct((D_MODEL, D_FF), F32), jax.ShapeDtypeStruct((D_MODEL, D_FF), F32),
                   jax.ShapeDtypeStruct((D_FF, D_MODEL), F32)),
        scratch_shapes=[pltpu.VMEM((T, D_MODEL), BF16), pltpu.VMEM((T, D_MODEL), BF16),
                        pltpu.VMEM((T, D_MODEL), F32)],
        compiler_params=_params(("arbitrary",)))(x, dy, g, wi, wi, wo)


def proj_fwd(x, g, w, name):
    T = x.shape[0]
    nj = D_IN_PAD // IN_CHUNK

    def body(x_ref, g_ref, w_ref, o_ref, h_ref):
        @pl.when(pl.program_id(0) == 0)
        def _():
            def init(r):
                h_ref[r, :] = _rms(x_ref[r, :], g_ref[...]).astype(BF16)
            _row_loop(T, init)

        def blk(r):
            o_ref[r, :] = jnp.dot(h_ref[r, :], w_ref[...], preferred_element_type=F32)
        _row_loop(T, blk)

    return pl.pallas_call(
        body, name=name, grid=(nj,),
        in_specs=[pl.BlockSpec((T, D_MODEL), lambda j: (0, 0)), pl.BlockSpec((1, D_MODEL), lambda j: (0, 0)),
                  pl.BlockSpec((D_MODEL, IN_CHUNK), lambda j: (0, j))],
        out_specs=pl.BlockSpec((T, IN_CHUNK), lambda j: (0, j)),
        out_shape=jax.ShapeDtypeStruct((T, D_IN_PAD), F32),
        scratch_shapes=[pltpu.VMEM((T, D_MODEL), BF16)],
        compiler_params=_params(("arbitrary",)))(x, g, w)


def proj_bwd(x, dres, g, w, dp, name):
    T = x.shape[0]
    nj = D_IN_PAD // IN_CHUNK

    def body(x_ref, dres_ref, g_ref, w_ref, dp_ref, dx_ref, dg_ref, dw_ref, h_ref, dh_ref):
        j = pl.program_id(0)

        @pl.when(j == 0)
        def _():
            def init(r):
                h_ref[r, :] = _rms(x_ref[r, :], g_ref[...]).astype(BF16)
                dh_ref[r, :] = jnp.zeros((r.size, D_MODEL), F32)
            _row_loop(T, init)

        dw_ref[...] = jnp.zeros_like(dw_ref)

        def blk(r):
            dpb = dp_ref[r, :].astype(BF16)
            dw_ref[...] += _bdot_tn(h_ref[r, :], dpb)
            dh_ref[r, :] += _bdot_nt(dpb, w_ref[...])
        _row_loop(T, blk)

        @pl.when(j == nj - 1)
        def _():
            dg_ref[...] = jnp.zeros_like(dg_ref)

            def fin(r):
                dx, dg = _norm_bwd_rows(x_ref[r, :], g_ref[...], dh_ref[r, :], dres_ref[r, :])
                dx_ref[r, :] = dx
                dg_ref[...] += dg
            _row_loop(T, fin)

    full = pl.BlockSpec((T, D_MODEL), lambda j: (0, 0))
    vec = pl.BlockSpec((1, D_MODEL), lambda j: (0, 0))
    return pl.pallas_call(
        body, name=name, grid=(nj,),
        in_specs=[full, full, vec, pl.BlockSpec((D_MODEL, IN_CHUNK), lambda j: (0, j)),
                  pl.BlockSpec((T, IN_CHUNK), lambda j: (0, j))],
        out_specs=(full, vec, pl.BlockSpec((D_MODEL, IN_CHUNK), lambda j: (0, j))),
        out_shape=(jax.ShapeDtypeStruct((T, D_MODEL), F32), jax.ShapeDtypeStruct((1, D_MODEL), F32),
                   jax.ShapeDtypeStruct((D_MODEL, D_IN_PAD), F32)),
        scratch_shapes=[pltpu.VMEM((T, D_MODEL), BF16), pltpu.VMEM((T, D_MODEL), F32)],
        compiler_params=_params(("arbitrary",)))(x, dres, g, w, dp)


def out_fwd(mixed, w, x, name):
    T = x.shape[0]

    def body(m_ref, w_ref, x_ref, o_ref):
        o_ref[...] = x_ref[...] + jnp.dot(m_ref[...].astype(BF16), w_ref[...], preferred_element_type=F32)

    blk = pl.BlockSpec((ROWS, D_MODEL), lambda i: (i, 0))
    return pl.pallas_call(
        body, name=name, grid=(T // ROWS,),
        in_specs=[blk, pl.BlockSpec((D_MODEL, D_MODEL), lambda i: (0, 0)), blk],
        out_specs=blk, out_shape=jax.ShapeDtypeStruct((T, D_MODEL), F32),
        compiler_params=_params(("arbitrary",)))(mixed, w, x)


def out_bwd(mixed, w, dy, name):
    T = dy.shape[0]

    def body(m_ref, w_ref, dy_ref, dm_ref, dw_ref):
        @pl.when(pl.program_id(0) == 0)
        def _():
            dw_ref[...] = jnp.zeros_like(dw_ref)
        dyb = dy_ref[...].astype(BF16)
        dm_ref[...] = _bdot_nt(dyb, w_ref[...])
        dw_ref[...] += _bdot_tn(m_ref[...].astype(BF16), dyb)

    blk = pl.BlockSpec((ROWS, D_MODEL), lambda i: (i, 0))
    sq = pl.BlockSpec((D_MODEL, D_MODEL), lambda i: (0, 0))
    return pl.pallas_call(
        body, name=name, grid=(T // ROWS,),
        in_specs=[blk, sq, blk], out_specs=(blk, sq),
        out_shape=(jax.ShapeDtypeStruct((T, D_MODEL), F32), jax.ShapeDtypeStruct((D_MODEL, D_MODEL), F32)),
        compiler_params=_params(("arbitrary",)))(mixed, w, dy)


def loss_head(x, g, target, name):
    T = x.shape[0]

    def body(x_ref, g_ref, t_ref, loss_ref, dx_ref, dg_ref):
        @pl.when(pl.program_id(0) == 0)
        def _():
            loss_ref[...] = jnp.zeros_like(loss_ref)
            dg_ref[...] = jnp.zeros_like(dg_ref)
        xb = x_ref[...]
        rstd = lax.rsqrt(jnp.mean(xb * xb, axis=-1, keepdims=True) + NORM_EPS)
        xh = xb * rstd
        err = xh * g_ref[...] - t_ref[...]
        loss_ref[...] += 0.5 * jnp.sum(jnp.mean(err * err, axis=-1, keepdims=True), axis=0, keepdims=True)
        dy = err * (1.0 / D_MODEL)
        dg_ref[...] += jnp.sum(dy * xh, axis=0, keepdims=True)
        dxh = dy * g_ref[...]
        dx_ref[...] = rstd * (dxh - xh * jnp.mean(dxh * xh, axis=-1, keepdims=True))

    blk = pl.BlockSpec((ROWS, D_MODEL), lambda i: (i, 0))
    vec = pl.BlockSpec((1, D_MODEL), lambda i: (0, 0))
    return pl.pallas_call(
        body, name=name, grid=(T // ROWS,),
        in_specs=[blk, vec, blk], out_specs=(pl.BlockSpec((1, 1), lambda i: (0, 0)), blk, vec),
        out_shape=(jax.ShapeDtypeStruct((1, 1), F32), jax.ShapeDtypeStruct((T, D_MODEL), F32),
                   jax.ShapeDtypeStruct((1, D_MODEL), F32)),
        compiler_params=_params(("arbitrary",)))(x, g, target)


def rowwise_fwd(fn, rows, shared, out_widths, name):
    T = rows[0].shape[0]
    n_in = len(rows) + len(shared)

    def body(*refs):
        res = fn(*[r[...] for r in refs[:n_in]])
        for o, v in zip(refs[n_in:], res):
            o[...] = v

    in_specs = ([pl.BlockSpec((ROWS, a.shape[1]), lambda i: (i, 0)) for a in rows]
                + [pl.BlockSpec(a.shape, lambda i: (0, 0)) for a in shared])
    return pl.pallas_call(
        body, name=name, grid=(T // ROWS,), in_specs=in_specs,
        out_specs=tuple(pl.BlockSpec((ROWS, w), lambda i: (i, 0)) for w in out_widths),
        out_shape=tuple(jax.ShapeDtypeStruct((T, w), F32) for w in out_widths),
        compiler_params=_params(("arbitrary",)))(*rows, *shared)


def rowwise_bwd(fn, rows, shared, cts, name, ct_fn=None):
    T = rows[0].shape[0]
    nr, ns, nc = len(rows), len(shared), len(cts)

    def body(*refs):
        ins = [r[...] for r in refs[:nr + ns]]
        ctv = tuple(r[...] for r in refs[nr + ns:nr + ns + nc])
        outs = refs[nr + ns + nc:]
        _, vjp = jax.vjp(fn, *ins)
        grads = vjp(ct_fn(*ctv) if ct_fn is not None else ctv)
        for k in range(nr):
            outs[k][...] = grads[k]

        @pl.when(pl.program_id(0) == 0)
        def _():
            for k in range(ns):
                outs[nr + k][...] = jnp.zeros_like(outs[nr + k])
        for k in range(ns):
            outs[nr + k][...] += grads[nr + k]

    row_spec = lambda a: pl.BlockSpec((ROWS, a.shape[1]), lambda i: (i, 0))
    sh_spec = lambda a: pl.BlockSpec(a.shape, lambda i: (0, 0))
    return pl.pallas_call(
        body, name=name, grid=(T // ROWS,),
        in_specs=[row_spec(a) for a in rows] + [sh_spec(a) for a in shared] + [row_spec(a) for a in cts],
        out_specs=tuple([row_spec(a) for a in rows] + [sh_spec(a) for a in shared]),
        out_shape=tuple(jax.ShapeDtypeStruct(a.shape, F32) for a in list(rows) + list(shared)),
        compiler_params=_params(("arbitrary",)))(*rows, *shared, *cts)


def shift_rows(x, s):
    return jnp.pad(x, ((s, 0), (0, 0)))[:x.shape[0]]


def unshift_rows(x, s):
    return jnp.pad(x, ((0, s), (0, 0)))[s:]


def _neg_expm1(y):
    series = -(y * (1.0 + y * (0.5 + y * (1.0 / 6.0 + y * (1.0 / 24.0)))))
    return jnp.where(y > -0.05, series, 1.0 - jnp.exp(y))


def lru_pre_fn(x0, x1, x2, x3, first, w0, w1, w2, w3, cb, ga, gab, gx, gxb, lam):
    xc = w3 * x0 + w2 * x1 + w1 * x2 + w0 * x3 + cb
    r = jax.nn.sigmoid(_hdot(xc, ga) + gab)
    i = jax.nn.sigmoid(_hdot(xc, gx) + gxb)
    log_a = -LRU_C * r * jax.nn.softplus(-lam)
    a = jnp.exp(log_a)
    mult = jnp.where(first > 0.5, 1.0, jnp.sqrt(_neg_expm1(2.0 * log_a)))
    return a, mult * i * xc


def lru_post_fn(h, py, og):
    return (_rms(h * jax.nn.gelu(py), og),)


def lru_scan(a, b, reverse, name):
    T, C = a.shape
    nb = T // 8

    def body(a_ref, b_ref, h_ref):
        rows = lax.broadcasted_iota(jnp.int32, (8, C), 0)

        def blk(i, carry):
            j = nb - 1 - i if reverse else i
            r = pl.ds(pl.multiple_of(j * 8, 8), 8)
            A = a_ref[r, :]
            B = b_ref[r, :]
            for s in (1, 2, 4):
                if reverse:
                    keep = rows < 8 - s
                    sh = 8 - s
                else:
                    keep = rows >= s
                    sh = s
                Bs = jnp.where(keep, pltpu.roll(B, sh, 0), 0.0)
                As = jnp.where(keep, pltpu.roll(A, sh, 0), 1.0)
                B = B + A * Bs
                A = A * As
            hb = B + A * carry
            h_ref[r, :] = hb
            edge = 0 if reverse else 7
            return jnp.sum(jnp.where(rows == edge, hb, 0.0), axis=0, keepdims=True)

        lax.fori_loop(0, nb, blk, jnp.zeros((1, C), F32))

    full = pl.BlockSpec((T, C), lambda: (0, 0))
    return pl.pallas_call(body, name=name, in_specs=[full, full], out_specs=full,
                          out_shape=jax.ShapeDtypeStruct((T, C), F32), compiler_params=_params())(a, b)


def make_rwkv_pre_fn(has_vres):
    def fn(p, pp, *rest):
        if has_vres:
            vf, mu, w_up, w_b, a_up, a_b, g_up, kk_w, ka_w, vw1, vw2, vb = rest
        else:
            mu, w_up, w_b, a_up, a_b, g_up, kk_w, ka_w = rest
        xm = p + (pp - p) * mu
        r, k, v = xm[:, 0:384], xm[:, 384:768], xm[:, 768:1152]
        xw, xa, xg = xm[:, 1152:1216], xm[:, 1216:1280], xm[:, 1280:1408]
        w_log = -jax.nn.softplus(-(w_b + _hdot(jnp.tanh(xw), w_up))) - 0.5
        lw = -jnp.exp(w_log)
        a = jax.nn.sigmoid(a_b + _hdot(xa, a_up))
        g = _hdot(jax.nn.sigmoid(xg), g_up)
        if has_vres:
            v = v + (vf - v) * jax.nn.sigmoid(vb + _hdot(_hdot(v, vw1), vw2))
        kkx = k * kk_w
        kk = kkx * lax.rsqrt(_segsum(kkx * kkx) + 1e-6)
        k2 = k * (1.0 + (a - 1.0) * ka_w)
        return r, lw, k2, v, kk, a, g
    return fn


def rwkv_post_fn(y, r, k2, v, g, ln_g, ln_b, r_k):
    mean = _segsum(y) * (1.0 / HEAD_DIM)
    yc = y - mean
    var = _segsum(yc * yc) * (1.0 / HEAD_DIM)
    yn = yc * lax.rsqrt(var + GN_EPS) * ln_g + ln_b
    bonus = _segsum(r * k2 * r_k) * v
    return ((yn + bonus) * g,)


def _head_expander(first_lane):
    ri, ci = _iota2(128, MIX_W)
    return (ri == ci // HEAD_DIM + first_lane).astype(F32)


def gdn_pre_fn(x0, x1, x2, x3, ab, w0, w1, w2, w3, alog, dtb):
    qkv = jax.nn.silu(w3 * x0 + w2 * x1 + w1 * x2 + w0 * x3)
    q, k, v = qkv[:, 0:384], qkv[:, 384:768], qkv[:, 768:1152]
    q = q * lax.rsqrt(_segsum(q * q) + 1e-6) * (HEAD_DIM ** -0.5)
    k = k * lax.rsqrt(_segsum(k * k) + 1e-6)
    g = -jnp.exp(alog) * jax.nn.softplus(ab + dtb)
    beta = jax.nn.sigmoid(ab)
    ge = _xdot(g, functools.partial(_head_expander, 0))
    be = _xdot(beta, functools.partial(_head_expander, HEADS))
    return q, k, v, ge, be


def gdn_post_fn(o, z, ng):
    ms = _segsum(o * o) * (1.0 / HEAD_DIM)
    return (o * lax.rsqrt(ms + NORM_EPS) * ng * jax.nn.silu(z),)


def _neumann_inv(m):
    n = m.shape[-1]
    ri, ci = _iota2(n, n)
    eye = (ri == ci).astype(F32)
    md = jnp.where(ri // 16 == ci // 16, m, 0.0)
    mo = m - md
    t0 = eye + md
    p2 = _hdot(md, md)
    t0 = t0 + _hdot(t0, p2)
    p4 = _hdot(p2, p2)
    t0 = t0 + _hdot(t0, p4)
    p8 = _hdot(p4, p4)
    t0 = t0 + _hdot(t0, p8)
    nn = _hdot(t0, mo)
    n2 = _hdot(nn, nn)
    t1 = eye + nn + n2 + _hdot(nn, n2)
    return _hdot(t1, t0)


@jax.custom_vjp
def _inv_saved(m, t_saved):
    return t_saved


def _inv_saved_fwd(m, t_saved):
    return t_saved, t_saved


def _inv_saved_bwd(t_saved, dt):
    tt = jnp.swapaxes(t_saved, -1, -2)
    return _hdot(_hdot(tt, dt), tt), jnp.zeros_like(t_saved)


_inv_saved.defvjp(_inv_saved_fwd, _inv_saved_bwd)


def _heads(x):
    return jnp.concatenate([x[None, :, h * HEAD_DIM:(h + 1) * HEAD_DIM] for h in range(HEADS)], axis=0)


def _unheads(y):
    return jnp.concatenate([lax.index_in_dim(y, h, 0, keepdims=False) for h in range(HEADS)], axis=1)


def rwkv_heads(s0, r, lw, k2, v, kk, a, inv):
    n = r.shape[0]
    ri, ci = _iota2(n, n)
    low, strict = ri >= ci, ri > ci
    cs = _cumsum_rows(lw)
    cl = jnp.sum(lw, axis=0, keepdims=True)
    p_in, p_prev, p_inv = jnp.exp(cs), jnp.exp(cs - lw), jnp.exp(-cs)
    p_rest, p_all = jnp.exp(cl - cs), jnp.exp(cl)
    bd = kk * a
    at, rt = _heads(-kk * p_prev), _heads(r * p_in)
    bh, kh = _heads(bd * p_inv), _heads(k2 * p_inv)
    vh = _heads(v)
    m_ab = jnp.where(strict, _cdot_nt(at, bh), 0.0)
    m_ak = jnp.where(strict, _cdot_nt(at, kh), 0.0)
    m_rb = jnp.where(low, _cdot_nt(rt, bh), 0.0)
    m_rk = jnp.where(low, _cdot_nt(rt, kh), 0.0)
    sa = _cdot(inv(m_ab), _cdot_nt(at, s0) + _cdot(m_ak, vh))
    y = _cdot_nt(rt, s0) + _cdot(m_rb, sa) + _cdot(m_rk, vh)
    s1 = s0 * _heads(p_all) + _cdot_tn(sa, _heads(bd * p_rest)) + _cdot_tn(vh, _heads(k2 * p_rest))
    return _unheads(y), s1


def gdn_heads(s0, q, k, v, ge, be, inv):
    n = q.shape[0]
    ri, ci = _iota2(n, n)
    low, strict = ri >= ci, ri > ci
    gc = _cumsum_rows(ge)
    gl = jnp.sum(ge, axis=0, keepdims=True)
    gch = _heads(gc)
    decay = jnp.where(low, jnp.exp(jnp.where(low, gch - jnp.swapaxes(gch, 1, 2), 0.0)), 0.0)
    kb = k * be
    e = jnp.exp(gc)
    kh = _heads(k)
    m = -jnp.where(strict, _cdot_nt(_heads(kb), kh) * decay, 0.0)
    mr = jnp.where(low, _cdot_nt(_heads(q), kh) * decay, 0.0)
    u = _cdot(inv(m), _heads(v * be) - _cdot_nt(_heads(kb * e), s0))
    y = _cdot_nt(_heads(q * e), s0) + _cdot(mr, u)
    s1 = s0 * _heads(jnp.exp(gl)) + _cdot_tn(u, _heads(k * jnp.exp(gl - gc)))
    return _unheads(y), s1


def core_fwd(heads_fn, ins, name, hosted=()):
    T = ins[0].shape[0]
    nc = T // CHUNK
    n = len(ins)
    gather = ChipGather(list(hosted))
    ng = gather.n

    def body(*refs):
        hx = refs[n:n + ng]
        y_ref, s0_ref, t_ref = refs[n + ng:n + ng + 3]
        ho = refs[n + ng + 3:n + 2 * ng + 3]
        s_ref = refs[n + 2 * ng + 3]
        sems = refs[n + 2 * ng + 4:]
        c = pl.program_id(0)

        @pl.when(c == 0)
        def _():
            gather.start(hx, ho, sems)
            s_ref[...] = jnp.zeros_like(s_ref)

        s0 = s_ref[...]
        kept = []

        def inv(m):
            kept.append(_neumann_inv(m))
            return kept[0]

        y, s1 = heads_fn(s0, *[r[...] for r in refs[:n]], inv)
        y_ref[...] = y
        s0_ref[0] = s0
        t_ref[0] = kept[0]
        s_ref[...] = s1

        @pl.when(c == nc // 2)
        def _():
            gather.relay(hx, ho, sems)

        @pl.when(c == nc - 1)
        def _():
            gather.finish(hx, ho, sems)

    row = pl.BlockSpec((CHUNK, MIX_W), lambda c: (c, 0))
    st_shape = (HEADS, HEAD_DIM, HEAD_DIM)
    st = pl.BlockSpec((1,) + st_shape, lambda c: (c, 0, 0, 0))
    res = pl.pallas_call(
        body, name=name, grid=(nc,), in_specs=[row] * n + gather.in_specs,
        out_specs=tuple([row, st, st] + gather.out_specs),
        out_shape=tuple([jax.ShapeDtypeStruct((T, MIX_W), F32), jax.ShapeDtypeStruct((nc,) + st_shape, F32),
                         jax.ShapeDtypeStruct((nc,) + st_shape, F32)] + gather.out_shape),
        scratch_shapes=[pltpu.VMEM(st_shape, F32)] + gather.scratch,
        compiler_params=_params(("arbitrary",)))(*ins, *hosted)
    return res[0], res[1], res[2], list(res[3:])


def core_bwd(heads_fn, ins, s0_all, t_all, dy, name):
    T = ins[0].shape[0]
    nc = T // CHUNK
    n = len(ins)

    def body(*refs):
        s0_ref, t_ref, dy_ref = refs[n:n + 3]
        outs = refs[n + 3:n + 3 + n]
        ds_ref = refs[n + 3 + n]

        @pl.when(pl.program_id(0) == 0)
        def _():
            ds_ref[...] = jnp.zeros_like(ds_ref)

        t_saved = t_ref[0]
        f = lambda s0, *xs: heads_fn(s0, *xs, lambda m: _inv_saved(m, t_saved))
        _, vjp = jax.vjp(f, s0_ref[0], *[r[...] for r in refs[:n]])
        grads = vjp((dy_ref[...], ds_ref[...]))
        ds_ref[...] = grads[0]
        for k in range(n):
            outs[k][...] = grads[1 + k]

    row = pl.BlockSpec((CHUNK, MIX_W), lambda c: (nc - 1 - c, 0))
    st_shape = (HEADS, HEAD_DIM, HEAD_DIM)
    st = pl.BlockSpec((1,) + st_shape, lambda c: (nc - 1 - c, 0, 0, 0))
    return pl.pallas_call(
        body, name=name, grid=(nc,), in_specs=[row] * n + [st, st, row], out_specs=tuple([row] * n),
        out_shape=tuple(jax.ShapeDtypeStruct((T, MIX_W), F32) for _ in range(n)),
        scratch_shapes=[pltpu.VMEM(st_shape, F32)],
        compiler_params=_params(("arbitrary",)))(*ins, s0_all, t_all, dy)


def _block_diag(w):
    out = jnp.zeros((LRU_W, LRU_W), w.dtype)
    for n in range(LRU_BLOCKS):
        out = lax.dynamic_update_slice(out, w[n], (n * 64, n * 64))
    return out


def _block_diag_grad(g):
    return jnp.stack([g[n * 64:(n + 1) * 64, n * 64:(n + 1) * 64] for n in range(LRU_BLOCKS)])


def _row(v):
    return v.reshape(1, -1)


def _pad128(v):
    return jnp.pad(v.reshape(1, -1), ((0, 0), (0, 128 - v.size)))


def _layer_shared(w, l):
    cw = w['lru_conv_w'][l]
    lru_pre = [_row(cw[0]), _row(cw[1]), _row(cw[2]), _row(cw[3]), _row(w['lru_conv_b'][l]),
               _block_diag(w['lru_gate_a_w'][l]), _row(w['lru_gate_a_b'][l]),
               _block_diag(w['lru_gate_x_w'][l]), _row(w['lru_gate_x_b'][l]), _row(w['lru_lambda'][l])]
    rw_pre = [_row(w['rwkv_mu'][l]), w['rwkv_w_up'][l], _row(w['rwkv_w_bias'][l]), w['rwkv_a_up'][l],
              _row(w['rwkv_a_bias'][l]), w['rwkv_g_up'][l], _row(w['rwkv_k_k'][l]), _row(w['rwkv_k_a'][l])]
    if l > 0:
        rw_pre += [w['rwkv_vres_w1'][l - 1], w['rwkv_vres_w2'][l - 1], _row(w['rwkv_vres_b'][l - 1])]
    rw_post = [_row(w['rwkv_ln_g'][l]), _row(w['rwkv_ln_b'][l]), _row(w['rwkv_r_k'][l])]
    gw = w['gdn_conv_w'][l]
    gdn_pre = [_row(gw[0]), _row(gw[1]), _row(gw[2]), _row(gw[3]), _pad128(w['gdn_a_log'][l]),
               _pad128(w['gdn_dt_bias'][l])]
    gdn_post = [_row(jnp.tile(w['gdn_norm'][l], HEADS))]
    return dict(lru_pre=lru_pre, lru_post=[_row(w['lru_out_norm'][l])], rw_pre=rw_pre, rw_post=rw_post,
                gdn_pre=gdn_pre, gdn_post=gdn_post)


def _mixer_fwd(p, sh, l, v_first, host_rwkv=(), host_gdn=()):
    T = p.shape[0]
    lx, ly = p[:, 0:256], p[:, 256:512]
    prw, qkv, z, ab = p[:, 512:1920], p[:, 1920:3072], p[:, 3072:3456], p[:, 3456:3584]
    first = jnp.zeros((T, LRU_W), F32).at[0].set(1.0)
    lru_rows = [lx, shift_rows(lx, 1), shift_rows(lx, 2), shift_rows(lx, 3), first]
    a, b = rowwise_fwd(lru_pre_fn, lru_rows, sh['lru_pre'], (LRU_W, LRU_W), f"lru_pre_fwd{l}")
    hseq = lru_scan(a, b, False, f"lru_scan_fwd{l}")
    (y_lru,) = rowwise_fwd(lru_post_fn, [hseq, ly], sh['lru_post'], (LRU_W,), f"lru_post_fwd{l}")

    rw_rows = [prw, shift_rows(prw, 1)] + ([v_first] if l > 0 else [])
    rw = rowwise_fwd(make_rwkv_pre_fn(l > 0), rw_rows, sh['rw_pre'], (MIX_W,) * 7, f"rwkv_pre_fwd{l}")
    r, lw, k2, v, kk, ar, g = rw
    y_raw, rs0, rt, got_rwkv = core_fwd(rwkv_heads, [r, lw, k2, v, kk, ar], f"rwkv_core_fwd{l}", host_rwkv)
    (y_rw,) = rowwise_fwd(rwkv_post_fn, [y_raw, r, k2, v, g], sh['rw_post'], (MIX_W,), f"rwkv_post_fwd{l}")

    gdn_rows = [qkv, shift_rows(qkv, 1), shift_rows(qkv, 2), shift_rows(qkv, 3), ab]
    gd = rowwise_fwd(gdn_pre_fn, gdn_rows, sh['gdn_pre'], (MIX_W,) * 5, f"gdn_pre_fwd{l}")
    o_raw, gs0, gt, got_gdn = core_fwd(gdn_heads, list(gd), f"gdn_core_fwd{l}", host_gdn)
    (y_gdn,) = rowwise_fwd(gdn_post_fn, [o_raw, z], sh['gdn_post'], (MIX_W,), f"gdn_post_fwd{l}")

    mixed = jnp.concatenate([y_lru, y_rw, y_gdn], axis=1)
    saved = dict(lru_rows=lru_rows, a=a, hseq=hseq, ly=ly, rw_rows=rw_rows, rw=rw, y_raw=y_raw, rs0=rs0, rt=rt,
                 gdn_rows=gdn_rows, gd=gd, o_raw=o_raw, gs0=gs0, gt=gt, z=z)
    v_layer0 = v if l == 0 else None
    return mixed, saved, v_layer0, got_rwkv, got_gdn


def _mixer_bwd(dmixed, sv, sh, l, dv_first):
    d_lru, d_rw, d_gdn = dmixed[:, 0:256], dmixed[:, 256:640], dmixed[:, 640:1024]
    gw = {}

    dh, dly, d_og = rowwise_bwd(lru_post_fn, [sv['hseq'], sv['ly']], sh['lru_post'], [d_lru], f"lru_post_bwd{l}")
    gscan = lru_scan(unshift_rows(sv['a'], 1), dh, True, f"lru_scan_bwd{l}")
    res = rowwise_bwd(lru_pre_fn, sv['lru_rows'], sh['lru_pre'], [gscan, shift_rows(sv['hseq'], 1)],
                      f"lru_pre_bwd{l}", ct_fn=lambda gs, hp: (gs * hp, gs))
    dlx = res[0] + unshift_rows(res[1], 1) + unshift_rows(res[2], 2) + unshift_rows(res[3], 3)
    dw0, dw1, dw2, dw3, dcb, dga, dgab, dgx, dgxb, dlam = res[5:]
    gw['lru_conv_w'] = jnp.concatenate([dw0, dw1, dw2, dw3], axis=0)
    gw['lru_conv_b'] = dcb[0]
    gw['lru_gate_a_w'] = _block_diag_grad(dga)
    gw['lru_gate_a_b'] = dgab.reshape(LRU_BLOCKS, 64)
    gw['lru_gate_x_w'] = _block_diag_grad(dgx)
    gw['lru_gate_x_b'] = dgxb.reshape(LRU_BLOCKS, 64)
    gw['lru_lambda'] = dlam[0]
    gw['lru_out_norm'] = d_og[0]

    r, lw, k2, v, kk, ar, g = sv['rw']
    res = rowwise_bwd(rwkv_post_fn, [sv['y_raw'], r, k2, v, g], sh['rw_post'], [d_rw], f"rwkv_post_bwd{l}")
    dy_raw, dr_p, dk2_p, dv_p, dg = res[:5]
    gw['rwkv_ln_g'], gw['rwkv_ln_b'], gw['rwkv_r_k'] = res[5][0], res[6][0], res[7].reshape(HEADS, HEAD_DIM)
    dr_c, dlw, dk2_c, dv_c, dkk, dar = core_bwd(rwkv_heads, [r, lw, k2, v, kk, ar], sv['rs0'], sv['rt'], dy_raw,
                                                 f"rwkv_core_bwd{l}")
    cts = [dr_p, dr_c, dlw, dk2_p, dk2_c, dv_p, dv_c, dkk, dar, dg]
    if l == 0:
        cts.append(dv_first)
        ct_fn = lambda a1, a2, b, c1, c2, d1, d2, e, f, gg, vf: (a1 + a2, b, c1 + c2, d1 + d2 + vf, e, f, gg)
    else:
        ct_fn = lambda a1, a2, b, c1, c2, d1, d2, e, f, gg: (a1 + a2, b, c1 + c2, d1 + d2, e, f, gg)
    res = rowwise_bwd(make_rwkv_pre_fn(l > 0), sv['rw_rows'], sh['rw_pre'], cts, f"rwkv_pre_bwd{l}", ct_fn=ct_fn)
    dprw = res[0] + unshift_rows(res[1], 1)
    nrow = len(sv['rw_rows'])
    dv_first_out = res[2] if l > 0 else None
    sg = res[nrow:]
    gw['rwkv_mu'], gw['rwkv_w_up'], gw['rwkv_w_bias'], gw['rwkv_a_up'] = sg[0][0], sg[1], sg[2][0], sg[3]
    gw['rwkv_a_bias'], gw['rwkv_g_up'], gw['rwkv_k_k'], gw['rwkv_k_a'] = sg[4][0], sg[5], sg[6][0], sg[7][0]
    if l > 0:
        gw['rwkv_vres_w1'], gw['rwkv_vres_w2'], gw['rwkv_vres_b'] = sg[8], sg[9], sg[10][0]

    do_raw, dz, d_ng = rowwise_bwd(gdn_post_fn, [sv['o_raw'], sv['z']], sh['gdn_post'], [d_gdn], f"gdn_post_bwd{l}")
    gw['gdn_norm'] = jnp.sum(d_ng.reshape(HEADS, HEAD_DIM), axis=0)
    dgd = core_bwd(gdn_heads, list(sv['gd']), sv['gs0'], sv['gt'], do_raw, f"gdn_core_bwd{l}")
    res = rowwise_bwd(gdn_pre_fn, sv['gdn_rows'], sh['gdn_pre'], list(dgd), f"gdn_pre_bwd{l}")
    dqkv = res[0] + unshift_rows(res[1], 1) + unshift_rows(res[2], 2) + unshift_rows(res[3], 3)
    dab = res[4]
    gw['gdn_conv_w'] = jnp.concatenate(res[5:9], axis=0)
    gw['gdn_a_log'], gw['gdn_dt_bias'] = res[9][0, :HEADS], res[10][0, :HEADS]

    dp = jnp.concatenate([dlx, dly, dprw, dqkv, dz, dab], axis=1)
    return dp, gw, dv_first_out


IN_SHARD = D_IN // N_CHIPS
IN_SHARD_PAD = D_IN_PAD // N_CHIPS


def _cols_to_chips(g, n=N_CHIPS):
    r = g.shape[0]
    return jnp.transpose(g.reshape(r, n, -1), (1, 0, 2))


def _cols_from_chips(g):
    return jnp.transpose(g, (1, 0, 2)).reshape(g.shape[1], -1)


def _w_in_from_chips(g):
    nat = _cols_from_chips(g[:, :, :IN_SHARD])
    return jnp.pad(nat, ((0, 0), (0, D_IN_PAD - D_IN)))


def _w_in_to_chips(g):
    return jnp.pad(_cols_to_chips(g[:, :D_IN]), ((0, 0), (0, 0), (0, IN_SHARD_PAD - IN_SHARD)))


def _natural(name, g):
    if name == 'w_in':
        return _w_in_from_chips(g)
    if BIG[name] == 2:
        return _cols_from_chips(g)
    return g.reshape(-1, g.shape[2])


def local_step(x, target, w, wb, shards=None):
    def hosted(keys):
        return [shards[k] for k in keys] if shards is not None else []

    def arrived(keys, gathered):
        for (name, layer), g in zip(keys if shards is not None else [], gathered):
            wb[name][layer] = _natural(name, g)

    saved = []
    v_first = None
    for l in range(N_LAYERS):
        sh = _layer_shared(w, l)
        for_mixer = [('w_in', l), ('w_out', l)]
        for_ffn2 = [('ffn2_wi', l), ('ffn2_wo', l)]
        for_next = [('ffn1_wi', l + 1), ('ffn1_wo', l + 1)] if l + 1 < N_LAYERS else []
        x1, got = ffn_fwd(x, _row(w['ffn1_norm'][l]), wb['ffn1_wi'][l], wb['ffn1_wo'][l], f"ffn1_fwd{l}",
                          hosted(for_mixer))
        arrived(for_mixer, got)
        p = proj_fwd(x1, _row(w['mix_norm'][l]), wb['w_in'][l], f"proj_fwd{l}")
        mixed, sv, v0, got_ffn2, got_next = _mixer_fwd(p, sh, l, v_first, hosted(for_ffn2), hosted(for_next))
        arrived(for_ffn2, got_ffn2)
        arrived(for_next, got_next)
        if l == 0:
            v_first = v0
        x2 = out_fwd(mixed, wb['w_out'][l], x1, f"out_fwd{l}")
        x3, _ = ffn_fwd(x2, _row(w['ffn2_norm'][l]), wb['ffn2_wi'][l], wb['ffn2_wo'][l], f"ffn2_fwd{l}")
        saved.append(dict(x0=x, x1=x1, x2=x2, mixed=mixed, sv=sv, sh=sh))
        x = x3

    loss, dx, dgf = loss_head(x, _row(w['final_norm']), target, "loss_head")
    per_layer = [None] * N_LAYERS
    dv_first = jnp.zeros((x.shape[0], MIX_W), F32)
    for l in reversed(range(N_LAYERS)):
        s = saved[l]
        gw = {}
        dx, dg2, dwg, dwu, dwo = ffn_bwd(s['x2'], dx, _row(w['ffn2_norm'][l]), wb['ffn2_wi'][l], wb['ffn2_wo'][l],
                                         f"ffn2_bwd{l}")
        wi_parts = lambda dwg, dwu: jnp.concatenate([_cols_to_chips(dwg, 2), _cols_to_chips(dwu, 2)], axis=0)
        row_parts = lambda dw: dw.reshape(N_CHIPS, -1, dw.shape[1])
        gw['ffn2_norm'], gw['ffn2_wi'], gw['ffn2_wo'] = dg2[0], wi_parts(dwg, dwu), row_parts(dwo)
        dmixed, dw_out = out_bwd(s['mixed'], wb['w_out'][l], dx, f"out_bwd{l}")
        gw['w_out'] = row_parts(dw_out)
        dp, gmix, dvf = _mixer_bwd(dmixed, s['sv'], s['sh'], l, dv_first)
        if l > 0:
            dv_first = dvf
        gw.update(gmix)
        dx, dgm, dwin = proj_bwd(s['x1'], dx, _row(w['mix_norm'][l]), wb['w_in'][l], dp, f"proj_bwd{l}")
        gw['mix_norm'], gw['w_in'] = dgm[0], _w_in_to_chips(dwin)
        dx, dg1, dwg, dwu, dwo = ffn_bwd(s['x0'], dx, _row(w['ffn1_norm'][l]), wb['ffn1_wi'][l], wb['ffn1_wo'][l],
                                         f"ffn1_bwd{l}")
        gw['ffn1_norm'], gw['ffn1_wi'], gw['ffn1_wo'] = dg1[0], wi_parts(dwg, dwu), row_parts(dwo)
        per_layer[l] = gw

    grads = {'final_norm': dgf[0]}
    for name in WEIGHTS:
        if name == 'final_norm':
            continue
        if name in BIG:
            grads[name] = [per_layer[l][name] for l in range(N_LAYERS)]
        elif name.startswith('rwkv_vres'):
            grads[name] = per_layer[1][name][None]
        else:
            grads[name] = jnp.stack([per_layer[l][name] for l in range(N_LAYERS)])
    return loss[0, 0], dx, grads


ANY = pl.BlockSpec(memory_space=pl.ANY)


def _coords():
    return lax.axis_index("x"), lax.axis_index("y"), lax.axis_index("c")


def _other_chips(x, y):
    return [((x + 1) % 2, y), (x, (y + 1) % 2), ((x + 1) % 2, (y + 1) % 2)]


def allreduce_small(pack, name):
    R = pack.shape[0]

    def body(x_ref, o_ref, buf, send_sems, recv_sems):
        x, y, c = _coords()
        me = 4 * x + 2 * y + c
        buf[me] = x_ref[...]
        copies = []
        for k in range(1, 8):
            peer = ((x + (k >> 2)) % 2, (y + ((k >> 1) & 1)) % 2, (c + (k & 1)) % 2)
            cp = pltpu.make_async_remote_copy(src_ref=x_ref, dst_ref=buf.at[me], send_sem=send_sems.at[k - 1],
                                              recv_sem=recv_sems.at[k - 1], device_id=peer, device_id_type=MESH)
            cp.start()
            copies.append(cp)
        for cp in copies:
            cp.wait()
        acc = buf[0]
        for d in range(1, 8):
            acc = acc + buf[d]
        o_ref[...] = acc

    vm = pl.BlockSpec(memory_space=pltpu.VMEM)
    return pl.pallas_call(
        body, name=name, in_specs=[vm], out_specs=vm, out_shape=jax.ShapeDtypeStruct((R, 128), F32),
        scratch_shapes=[pltpu.VMEM((8, R, 128), F32), pltpu.SemaphoreType.DMA((7,)), pltpu.SemaphoreType.DMA((7,))],
        compiler_params=_params())(pack)


class ChipGather:
    def __init__(self, shards):
        self.shapes = [s.shape for s in shards]
        self.n = len(shards)
        self.in_specs = [ANY] * self.n
        self.out_specs = [ANY] * self.n
        self.out_shape = [jax.ShapeDtypeStruct((N_CHIPS,) + s.shape, s.dtype) for s in shards]
        self.scratch = [pltpu.SemaphoreType.DMA((6 * self.n,)), pltpu.SemaphoreType.DMA((6 * self.n,)),
                        pltpu.SemaphoreType.DMA((self.n,))] if self.n else []

    def _rows(self, a, core):
        rh = self.shapes[a][0] // 2
        return pl.ds(pl.multiple_of(core * rh, 16), rh)

    def _copies(self, kind, x_refs, o_refs, sems):
        send_sems, recv_sems, local_sems = sems
        x, y, c = _coords()
        s_me = 2 * x + y
        sib = (x, y, 1 - c)

        def copy(a, k, src, dst, to):
            return pltpu.make_async_remote_copy(src_ref=src, dst_ref=dst, send_sem=send_sems.at[6 * a + k],
                                                recv_sem=recv_sems.at[6 * a + k], device_id=to, device_id_type=MESH)

        if kind == 'own':
            return [pltpu.make_async_copy(x_refs[a], o_refs[a].at[s_me], local_sems.at[a]) for a in range(self.n)]
        out = []
        for j, (px, py) in enumerate(_other_chips(x, y)):
            for a in range(self.n):
                mine = self._rows(a, c)
                part = o_refs[a].at[2 * px + py, mine]
                if kind == 'sent':
                    out.append(copy(a, j, x_refs[a].at[mine], o_refs[a].at[s_me, mine], (px, py, c)))
                elif kind == 'arrived':
                    out.append(copy(a, j, part, part, (px, py, c)))
                elif kind == 'passed':
                    out.append(copy(a, 3 + j, part, part, sib))
                else:
                    theirs = o_refs[a].at[2 * px + py, self._rows(a, 1 - c)]
                    out.append(copy(a, 3 + j, theirs, theirs, sib))
        return out

    def start(self, x_refs, o_refs, sems):
        if not self.n:
            return
        for cp in self._copies('own', x_refs, o_refs, sems) + self._copies('sent', x_refs, o_refs, sems):
            cp.start()

    def relay(self, x_refs, o_refs, sems):
        if not self.n:
            return
        for got, fw in zip(self._copies('arrived', x_refs, o_refs, sems),
                           self._copies('passed', x_refs, o_refs, sems)):
            got.wait_recv()
            fw.start()

    def finish(self, x_refs, o_refs, sems):
        if not self.n:
            return
        for cp in self._copies('from_sibling', x_refs, o_refs, sems):
            cp.wait_recv()
        for cp in self._copies('sent', x_refs, o_refs, sems) + self._copies('passed', x_refs, o_refs, sems):
            cp.wait_send()
        for cp in self._copies('own', x_refs, o_refs, sems):
            cp.wait()


def allgather_chips(shards, name):
    gather = ChipGather(shards)
    n = gather.n

    def body(*refs):
        x_refs, o_refs, sems = refs[:n], refs[n:2 * n], refs[2 * n:]
        gather.start(x_refs, o_refs, sems)
        gather.relay(x_refs, o_refs, sems)
        gather.finish(x_refs, o_refs, sems)

    return pl.pallas_call(
        body, name=name, in_specs=gather.in_specs, out_specs=tuple(gather.out_specs),
        out_shape=tuple(gather.out_shape), scratch_shapes=gather.scratch, compiler_params=_params())(*shards)


def sibling_swap(srcs, halves, name):
    n = len(srcs)
    out_shapes = [(s.shape[0], s.shape[1] // 2, s.shape[2]) if halves else s.shape for s in srcs]

    def body(*refs):
        x_refs, o_refs = refs[:n], refs[n:2 * n]
        send_sems, recv_sems = refs[2 * n:]
        x, y, c = _coords()
        copies = []
        for a in range(n):
            part = x_refs[a]
            if halves:
                rh = srcs[a].shape[1] // 2
                part = part.at[:, pl.ds(pl.multiple_of((1 - c) * rh, 16), rh)]
            cp = pltpu.make_async_remote_copy(src_ref=part, dst_ref=o_refs[a], send_sem=send_sems.at[a],
                                              recv_sem=recv_sems.at[a], device_id=(x, y, 1 - c), device_id_type=MESH)
            cp.start()
            copies.append(cp)
        for cp in copies:
            cp.wait()

    return pl.pallas_call(
        body, name=name, in_specs=[ANY] * n, out_specs=tuple([ANY] * n),
        out_shape=tuple(jax.ShapeDtypeStruct(sh, s.dtype) for sh, s in zip(out_shapes, srcs)),
        scratch_shapes=[pltpu.SemaphoreType.DMA((n,)), pltpu.SemaphoreType.DMA((n,))],
        compiler_params=_params())(*srcs)


def scatter_chips(parts, name):
    n = len(parts)

    def body(*refs):
        x_refs, o_refs = refs[:n], refs[n:2 * n]
        send_sems, recv_sems = refs[2 * n:]
        x, y, c = _coords()
        copies = []
        for j, (px, py) in enumerate(_other_chips(x, y)):
            for a in range(n):
                cp = pltpu.make_async_remote_copy(src_ref=x_refs[a].at[2 * px + py], dst_ref=o_refs[a].at[j],
                                                  send_sem=send_sems.at[3 * a + j], recv_sem=recv_sems.at[3 * a + j],
                                                  device_id=(px, py, c), device_id_type=MESH)
                cp.start()
                copies.append(cp)
        for cp in copies:
            cp.wait()

    return pl.pallas_call(
        body, name=name, in_specs=[ANY] * n, out_specs=tuple([ANY] * n),
        out_shape=tuple(jax.ShapeDtypeStruct((3,) + p.shape[1:], p.dtype) for p in parts),
        scratch_shapes=[pltpu.SemaphoreType.DMA((3 * n,)), pltpu.SemaphoreType.DMA((3 * n,))],
        compiler_params=_params())(*parts)


def _row_block(rows):
    return max(b for b in range(16, 257, 16) if rows % b == 0)


def chip_sum(gpack, recv, core, name):
    n, R, W = gpack.shape
    rh = R // 2
    rb = _row_block(rh)
    nb = rh // rb

    def body(c_ref, g_ref, r_ref, o_ref, ob_ref):
        s = g_ref[...] + r_ref[...]
        o_ref[...] = s
        ob_ref[...] = s.astype(BF16)

    blk = pl.BlockSpec((1, rb, W), lambda i, j, c_ref: (i, j, 0))
    spec = pltpu.PrefetchScalarGridSpec(
        num_scalar_prefetch=1, grid=(n, nb),
        in_specs=[pl.BlockSpec((1, rb, W), lambda i, j, c_ref: (i, c_ref[0] * nb + j, 0)), blk],
        out_specs=(blk, blk))
    return pl.pallas_call(
        body, name=name, grid_spec=spec,
        out_shape=(jax.ShapeDtypeStruct((n, rh, W), F32), jax.ShapeDtypeStruct((n, rh, W), BF16)),
        compiler_params=_params(("arbitrary", "arbitrary")))(core, gpack, recv)


def shard_sum(own, recv, name):
    R, W = own.shape
    rb = _row_block(R)

    def body(a_ref, r_ref, o_ref):
        acc = a_ref[...]
        for j in range(3):
            acc = acc + r_ref[j].astype(F32)
        o_ref[...] = acc

    return pl.pallas_call(
        body, name=name, grid=(R // rb,),
        in_specs=[pl.BlockSpec((rb, W), lambda i: (i, 0)), pl.BlockSpec((3, rb, W), lambda i: (0, i, 0))],
        out_specs=pl.BlockSpec((rb, W), lambda i: (i, 0)), out_shape=jax.ShapeDtypeStruct((R, W), F32),
        compiler_params=_params(("arbitrary",)))(own, recv)


def adamw(w, m, v, g, name):
    R, C = w.shape
    rb = 128 if R % 128 == 0 else R
    bc1 = 1.0 - ADAM_B1 ** ADAM_STEP
    bc2 = 1.0 - ADAM_B2 ** ADAM_STEP

    def body(w_ref, m_ref, v_ref, g_ref, d_ref, nm_ref, nv_ref):
        gv = g_ref[...]
        nm = ADAM_B1 * m_ref[...] + (1.0 - ADAM_B1) * gv
        nv = ADAM_B2 * v_ref[...] + (1.0 - ADAM_B2) * (gv * gv)
        d_ref[...] = -ADAM_LR * ((nm / bc1) / (jnp.sqrt(nv / bc2) + ADAM_EPS) + ADAM_WD * w_ref[...])
        nm_ref[...] = nm
        nv_ref[...] = nv

    blk = pl.BlockSpec((rb, C), lambda i: (i, 0))
    sh = jax.ShapeDtypeStruct((R, C), F32)
    return pl.pallas_call(body, name=name, grid=(R // rb,), in_specs=[blk] * 4, out_specs=(blk,) * 3,
                          out_shape=(sh, sh, sh), compiler_params=_params(("arbitrary",)))(w, m, v, g)


SMALL = [n for n in WEIGHTS if n not in BIG]


def _pack_rows(flat, width, row_multiple):
    n = flat.shape[-1]
    per = width * row_multiple
    total = -(-n // per) * per
    flat = jnp.pad(flat, [(0, 0)] * (flat.ndim - 1) + [(0, total - n)])
    return flat.reshape(flat.shape[:-1] + (total // width, width))


def _pad_lanes(a):
    return jnp.pad(a, ((0, 0), (0, -a.shape[1] % 128)))


def _local_shard(full, axis, chip):
    size = full.shape[axis] // N_CHIPS
    return lax.dynamic_slice_in_dim(full, chip * size, size, axis)


def kernel(x, ffn1_norm, ffn1_wi, ffn1_wo, mix_norm, w_in, w_out, lru_conv_w, lru_conv_b, lru_gate_a_w, lru_gate_a_b, lru_gate_x_w, lru_gate_x_b, lru_lambda, lru_out_norm, rwkv_mu, rwkv_w_up, rwkv_w_bias, rwkv_a_up, rwkv_a_bias, rwkv_g_up, rwkv_k_k, rwkv_k_a, rwkv_r_k, rwkv_ln_g, rwkv_ln_b, rwkv_vres_w1, rwkv_vres_w2, rwkv_vres_b, gdn_conv_w, gdn_a_log, gdn_dt_bias, gdn_norm, ffn2_norm, ffn2_wi, ffn2_wo, final_norm, loss_target, m_ffn1_norm, m_ffn1_wi, m_ffn1_wo, m_mix_norm, m_w_in, m_w_out, m_lru_conv_w, m_lru_conv_b, m_lru_gate_a_w, m_lru_gate_a_b, m_lru_gate_x_w, m_lru_gate_x_b, m_lru_lambda, m_lru_out_norm, m_rwkv_mu, m_rwkv_w_up, m_rwkv_w_bias, m_rwkv_a_up, m_rwkv_a_bias, m_rwkv_g_up, m_rwkv_k_k, m_rwkv_k_a, m_rwkv_r_k, m_rwkv_ln_g, m_rwkv_ln_b, m_rwkv_vres_w1, m_rwkv_vres_w2, m_rwkv_vres_b, m_gdn_conv_w, m_gdn_a_log, m_gdn_dt_bias, m_gdn_norm, m_ffn2_norm, m_ffn2_wi, m_ffn2_wo, m_final_norm, v_ffn1_norm, v_ffn1_wi, v_ffn1_wo, v_mix_norm, v_w_in, v_w_out, v_lru_conv_w, v_lru_conv_b, v_lru_gate_a_w, v_lru_gate_a_b, v_lru_gate_x_w, v_lru_gate_x_b, v_lru_lambda, v_lru_out_norm, v_rwkv_mu, v_rwkv_w_up, v_rwkv_w_bias, v_rwkv_a_up, v_rwkv_a_bias, v_rwkv_g_up, v_rwkv_k_k, v_rwkv_k_a, v_rwkv_r_k, v_rwkv_ln_g, v_rwkv_ln_b, v_rwkv_vres_w1, v_rwkv_vres_w2, v_rwkv_vres_b, v_gdn_conv_w, v_gdn_a_log, v_gdn_dt_bias, v_gdn_norm, v_ffn2_norm, v_ffn2_wi, v_ffn2_wo, v_final_norm):
    args = locals()
    w_loc = {n: args[n] for n in WEIGHTS}
    m_loc = {n: args['m_' + n] for n in WEIGHTS}
    v_loc = {n: args['v_' + n] for n in WEIGHTS}
    chip = 2 * lax.axis_index("x") + lax.axis_index("y")
    core = lax.axis_index("c")

    big = [(n, l) for n in BIG for l in range(N_LAYERS)]
    shards = {(n, l): _pad_lanes(w_loc[n][l].astype(BF16)) for n, l in big}
    first = [('ffn1_wi', 0), ('ffn1_wo', 0)]
    wb = {n: [None] * N_LAYERS for n in BIG}
    for (n, l), g in zip(first, allgather_chips([shards[k] for k in first], "allgather_first")):
        wb[n][l] = _natural(n, g)

    sm_names = list(SMALL_SHARDED)
    placed = []
    for n in sm_names:
        ax = SMALL_SHARDED[n]
        full_shape = w_loc[n].shape[:ax] + (N_CHIPS * w_loc[n].shape[ax],) + w_loc[n].shape[ax + 1:]
        src = w_loc[n] * (core == 0).astype(F32)
        placed.append(lax.dynamic_update_slice_in_dim(jnp.zeros(full_shape, F32), src, chip * w_loc[n].shape[ax], ax))
    summed = allreduce_small(_pack_rows(jnp.concatenate([p.reshape(-1) for p in placed]), 128, 8),
                             "allgather_small").reshape(-1)
    w_full, off = dict(w_loc), 0
    for n, p in zip(sm_names, placed):
        w_full[n] = summed[off:off + p.size].reshape(p.shape)
        off += p.size

    loss, dx, grads = local_step(x[0], loss_target[0], w_full, wb, shards)
    loss = lax.psum(loss, ("x", "y", "c"))

    gsum = allreduce_small(_pack_rows(jnp.concatenate([grads[n].reshape(-1) for n in SMALL]), 128, 8),
                           "allreduce_small").reshape(-1)
    g_loc, off = {}, 0
    for n in SMALL:
        g = gsum[off:off + grads[n].size].reshape(grads[n].shape)
        off += grads[n].size
        g_loc[n] = _local_shard(g, SMALL_SHARDED[n], chip) if n in SMALL_SHARDED else g

    parts = [grads[n][l] for n, l in big]
    from_sib = sibling_swap(parts, True, "grad_swap_cores")
    core_arg = core.reshape(1).astype(jnp.int32)
    sums = [chip_sum(p, r, core_arg, f"grad_chip_sum_{n}{l}") for (n, l), p, r in zip(big, parts, from_sib)]
    from_chips = scatter_chips([s[1] for s in sums], "grad_scatter")
    halves = [shard_sum(lax.dynamic_index_in_dim(s[0], chip, 0, keepdims=False), r, f"grad_shard_sum_{n}{l}")
              for (n, l), s, r in zip(big, sums, from_chips)]
    others = sibling_swap(halves, False, "grad_share_cores")
    rows = {n: [None] * N_LAYERS for n in BIG}
    for (n, l), half, other in zip(big, halves, others):
        lower = jnp.where(core == 0, half, other)
        upper = jnp.where(core == 0, other, half)
        rows[n][l] = jnp.concatenate([lower, upper], axis=0)[:, :w_loc[n].shape[-1]]
    big_names = list(BIG)
    for n in big_names:
        g_loc[n] = jnp.stack(rows[n])

    delta, new_m, new_v = {}, {}, {}
    for n in big_names:
        shp = w_loc[n].shape
        two_d = lambda a: a.reshape(-1, shp[-1])
        d, nm, nv = adamw(two_d(w_loc[n]), two_d(m_loc[n]), two_d(v_loc[n]), two_d(g_loc[n]), f"adamw_{n}")
        delta[n], new_m[n], new_v[n] = d.reshape(shp), nm.reshape(shp), nv.reshape(shp)
    pack = lambda d: _pack_rows(jnp.concatenate([d[n].reshape(-1) for n in SMALL]), 128, 128)
    res = adamw(pack(w_loc), pack(m_loc), pack(v_loc), pack(g_loc), "adamw_small")
    off = 0
    for n in SMALL:
        size = w_loc[n].size
        for dst, r in zip((delta, new_m, new_v), res):
            dst[n] = r.reshape(-1)[off:off + size].reshape(w_loc[n].shape)
        off += size

    return (loss, dx[None], *[g_loc[n] for n in WEIGHTS], *[delta[n] for n in WEIGHTS],
            *[new_m[n] for n in WEIGHTS], *[new_v[n] for n in WEIGHTS])
```

```python
import functools

import numpy as np
import jax
import jax.numpy as jnp
from jax import lax
from jax.experimental import pallas as pl
from jax.experimental.pallas import tpu as pltpu

F32 = jnp.float32
BF16 = jnp.bfloat16
HIGHEST = lax.Precision.HIGHEST
MESH = pl.DeviceIdType.MESH

D_MODEL = 1024
D_FF = 2816
N_LAYERS = 2
HEADS = 6
HEAD_DIM = 64
MIX_W = HEADS * HEAD_DIM
LRU_W = 256
LRU_BLOCKS = 4
RWKV_IN = 1408
D_IN = 3468
D_IN_PAD = 3584
NORM_EPS = 1e-6
GN_EPS = 64e-5
LRU_C = 8.0
CHUNK = 64
ROWS = 256
FF_CHUNK = 256
IN_CHUNK = 512
PACK_W = 1024
VMEM_LIMIT = 56 * 1024 * 1024

ADAM_LR, ADAM_B1, ADAM_B2, ADAM_EPS, ADAM_WD, ADAM_STEP = 0.001, 0.9, 0.999, 1e-08, 0.01, 10

WEIGHTS = ['ffn1_norm', 'ffn1_wi', 'ffn1_wo', 'mix_norm', 'w_in', 'w_out', 'lru_conv_w', 'lru_conv_b',
           'lru_gate_a_w', 'lru_gate_a_b', 'lru_gate_x_w', 'lru_gate_x_b', 'lru_lambda', 'lru_out_norm',
           'rwkv_mu', 'rwkv_w_up', 'rwkv_w_bias', 'rwkv_a_up', 'rwkv_a_bias', 'rwkv_g_up', 'rwkv_k_k',
           'rwkv_k_a', 'rwkv_r_k', 'rwkv_ln_g', 'rwkv_ln_b', 'rwkv_vres_w1', 'rwkv_vres_w2', 'rwkv_vres_b',
           'gdn_conv_w', 'gdn_a_log', 'gdn_dt_bias', 'gdn_norm', 'ffn2_norm', 'ffn2_wi', 'ffn2_wo', 'final_norm']
BIG = {'ffn1_wi': 2, 'ffn1_wo': 1, 'w_in': 2, 'w_out': 1, 'ffn2_wi': 2, 'ffn2_wo': 1}
SMALL_SHARDED = {'lru_conv_w': 2, 'rwkv_w_up': 2, 'rwkv_a_up': 2, 'rwkv_g_up': 2, 'rwkv_vres_w1': 1,
                 'rwkv_vres_w2': 2, 'gdn_conv_w': 2}
N_CHIPS = 4


def _params(sem=None):
    kw = dict(vmem_limit_bytes=VMEM_LIMIT)
    if sem is not None:
        kw['dimension_semantics'] = sem
    return pltpu.CompilerParams(**kw)


def _bdot(a, b, dims=(((1,), (0,)), ((), ()))):
    return lax.dot_general(a.astype(BF16), b.astype(BF16), dims, preferred_element_type=F32)


def _bdot_nt(a, b):
    return _bdot(a, b, (((1,), (1,)), ((), ())))


def _bdot_tn(a, b):
    return _bdot(a, b, (((0,), (0,)), ((), ())))


_DIMS = {'nn': (((1,), (0,)), ((), ())), 'nt': (((1,), (1,)), ((), ())), 'tn': (((0,), (0,)), ((), ()))}


def _split(a, terms):
    parts = []
    for _ in range(terms - 1):
        hi = a.astype(BF16)
        parts.append(hi)
        a = a - hi.astype(F32)
    parts.append(a.astype(BF16))
    return parts


_BATCH_DIMS = {'nn': (((2,), (1,)), ((0,), (0,))), 'nt': (((2,), (2,)), ((0,), (0,))),
               'tn': (((1,), (1,)), ((0,), (0,)))}


def _dot3(a, b, kind):
    ah, al = _split(a, 2)
    bh, bl = _split(b, 2)
    dims = _BATCH_DIMS[kind] if a.ndim == 3 else _DIMS[kind]
    d = lambda p, q: lax.dot_general(p, q, dims, preferred_element_type=F32)
    return d(ah, bh) + (d(ah, bl) + d(al, bh))


@functools.partial(jax.custom_vjp, nondiff_argnums=(2,))
def _cdot_k(a, b, kind):
    return _dot3(a, b, kind)


def _cdot_k_fwd(a, b, kind):
    return _dot3(a, b, kind), (a, b)


def _cdot_k_bwd(kind, res, ct):
    a, b = res
    if kind == 'nn':
        return _dot3(ct, b, 'nt'), _dot3(a, ct, 'tn')
    if kind == 'nt':
        return _dot3(ct, b, 'nn'), _dot3(ct, a, 'tn')
    return _dot3(b, ct, 'nt'), _dot3(a, ct, 'nn')


_cdot_k.defvjp(_cdot_k_fwd, _cdot_k_bwd)


def _cdot(a, b):
    return _cdot_k(a, b, 'nn')


def _cdot_nt(a, b):
    return _cdot_k(a, b, 'nt')


def _cdot_tn(a, b):
    return _cdot_k(a, b, 'tn')


def _hdot(a, b):
    return _cdot_k(a, b, 'nn')


def _dot_exact(x, m01, kind):
    d = lambda p: lax.dot_general(p, m01.astype(BF16), _DIMS[kind], preferred_element_type=F32)
    hi, mid, lo = _split(x, 3)
    return d(hi) + (d(mid) + d(lo))


@functools.partial(jax.custom_vjp, nondiff_argnums=(1,))
def _xdot(x, make_m):
    return _dot_exact(x, make_m(), 'nn')


def _xdot_fwd(x, make_m):
    return _dot_exact(x, make_m(), 'nn'), None


def _xdot_bwd(make_m, _, ct):
    return (_dot_exact(ct, make_m(), 'nt'),)


_xdot.defvjp(_xdot_fwd, _xdot_bwd)


def _iota2(n, m):
    return lax.broadcasted_iota(jnp.int32, (n, m), 0), lax.broadcasted_iota(jnp.int32, (n, m), 1)


def _head_blocks(w):
    ri, ci = _iota2(w, w)
    return (ri // HEAD_DIM == ci // HEAD_DIM).astype(F32)


def _segsum(x):
    return _xdot(x, functools.partial(_head_blocks, x.shape[-1]))


def _cumsum_rows(x):
    return _cumsum_k(x, x.shape[0])


@functools.partial(jax.custom_vjp, nondiff_argnums=(1,))
def _cumsum_k(x, n):
    return _lower_dot(x, n, False)


def _lower_dot(x, n, transpose):
    ri, ci = _iota2(n, n)
    m = ((ri <= ci) if transpose else (ri >= ci)).astype(BF16)
    d = lambda p: lax.dot_general(m, p, _DIMS['nn'], preferred_element_type=F32)
    hi, mid, lo = _split(x, 3)
    return d(hi) + (d(mid) + d(lo))


def _cumsum_k_fwd(x, n):
    return _lower_dot(x, n, False), None


def _cumsum_k_bwd(n, _, ct):
    return (_lower_dot(ct, n, True),)


_cumsum_k.defvjp(_cumsum_k_fwd, _cumsum_k_bwd)


def _rms(x, g):
    return x * lax.rsqrt(jnp.mean(x * x, axis=-1, keepdims=True) + NORM_EPS) * g


DENSE_ROWS = 1024


def _row_loop(n_rows, fn):
    rows = min(DENSE_ROWS, n_rows)

    def step(i, c):
        fn(pl.ds(pl.multiple_of(i * rows, rows), rows))
        return c
    lax.fori_loop(0, n_rows // rows, step, 0)


def ffn_fwd(x, g, wi, wo, name, hosted=()):
    T = x.shape[0]
    nj = D_FF // FF_CHUNK
    gather = ChipGather(list(hosted))
    n = gather.n

    def body(*refs):
        x_ref, g_ref, wg_ref, wu_ref, wo_ref = refs[:5]
        hx, o_ref, ho = refs[5:5 + n], refs[5 + n], refs[6 + n:6 + 2 * n]
        h_ref, acc_ref = refs[6 + 2 * n:8 + 2 * n]
        sems = refs[8 + 2 * n:]
        j = pl.program_id(0)

        @pl.when(j == 0)
        def _():
            gather.start(hx, ho, sems)

            def init(r):
                h_ref[r, :] = _rms(x_ref[r, :], g_ref[...]).astype(BF16)
                acc_ref[r, :] = jnp.zeros((r.size, D_MODEL), F32)
            _row_loop(T, init)

        def blk(r):
            hb = h_ref[r, :]
            gate = jnp.dot(hb, wg_ref[...], preferred_element_type=F32)
            up = jnp.dot(hb, wu_ref[...], preferred_element_type=F32)
            a = (gate * jax.nn.sigmoid(gate) * up).astype(BF16)
            acc_ref[r, :] += jnp.dot(a, wo_ref[...], preferred_element_type=F32)
        _row_loop(T, blk)

        @pl.when(j == nj // 2)
        def _():
            gather.relay(hx, ho, sems)

        @pl.when(j == nj - 1)
        def _():
            def fin(r):
                o_ref[r, :] = x_ref[r, :] + 0.5 * acc_ref[r, :]
            _row_loop(T, fin)
            gather.finish(hx, ho, sems)

    full = pl.BlockSpec((T, D_MODEL), lambda j: (0, 0))
    res = pl.pallas_call(
        body, name=name, grid=(nj,),
        in_specs=[full, pl.BlockSpec((1, D_MODEL), lambda j: (0, 0)),
                  pl.BlockSpec((D_MODEL, FF_CHUNK), lambda j: (0, j)),
                  pl.BlockSpec((D_MODEL, FF_CHUNK), lambda j: (0, j + nj)),
                  pl.BlockSpec((FF_CHUNK, D_MODEL), lambda j: (j, 0))] + gather.in_specs,
        out_specs=tuple([full] + gather.out_specs),
        out_shape=tuple([jax.ShapeDtypeStruct((T, D_MODEL), F32)] + gather.out_shape),
        scratch_shapes=[pltpu.VMEM((T, D_MODEL), BF16), pltpu.VMEM((T, D_MODEL), F32)] + gather.scratch,
        compiler_params=_params(("arbitrary",)))(x, g, wi, wi, wo, *hosted)
    return res[0], list(res[1:])


def _norm_bwd_rows(x, g, dh, dres):
    rstd = lax.rsqrt(jnp.mean(x * x, axis=-1, keepdims=True) + NORM_EPS)
    xh = x * rstd
    dxh = dh * g
    dx = rstd * (dxh - xh * jnp.mean(dxh * xh, axis=-1, keepdims=True))
    return dres + dx, jnp.sum(dh * xh, axis=0, keepdims=True)


def ffn_bwd(x, dy, g, wi, wo, name):
    T = x.shape[0]
    nj = D_FF // FF_CHUNK

    def body(x_ref, dy_ref, g_ref, wg_ref, wu_ref, wo_ref, dx_ref, dg_ref, dwg_ref, dwu_ref, dwo_ref,
             h_ref, da_ref, dh_ref):
        j = pl.program_id(0)

        @pl.when(j == 0)
        def _():
            def init(r):
                h_ref[r, :] = _rms(x_ref[r, :], g_ref[...]).astype(BF16)
                da_ref[r, :] = (0.5 * dy_ref[r, :]).astype(BF16)
                dh_ref[r, :] = jnp.zeros((r.size, D_MODEL), F32)
            _row_loop(T, init)

        dwg_ref[...] = jnp.zeros_like(dwg_ref)
        dwu_ref[...] = jnp.zeros_like(dwu_ref)
        dwo_ref[...] = jnp.zeros_like(dwo_ref)

        def blk(r):
            hb = h_ref[r, :]
            db = da_ref[r, :]
            gate = jnp.dot(hb, wg_ref[...], preferred_element_type=F32)
            up = jnp.dot(hb, wu_ref[...], preferred_element_type=F32)
            sg = jax.nn.sigmoid(gate)
            sl = gate * sg
            da = _bdot_nt(db, wo_ref[...])
            dup = (da * sl).astype(BF16)
            dgate = (da * up * (sg * (1.0 + gate * (1.0 - sg)))).astype(BF16)
            dwo_ref[...] += _bdot_tn((sl * up).astype(BF16), db)
            dwg_ref[...] += _bdot_tn(hb, dgate)
            dwu_ref[...] += _bdot_tn(hb, dup)
            dh_ref[r, :] += _bdot_nt(dgate, wg_ref[...]) + _bdot_nt(dup, wu_ref[...])
        _row_loop(T, blk)

        @pl.when(j == nj - 1)
        def _():
            dg_ref[...] = jnp.zeros_like(dg_ref)

            def fin(r):
                dx, dg = _norm_bwd_rows(x_ref[r, :], g_ref[...], dh_ref[r, :], dy_ref[r, :])
                dx_ref[r, :] = dx
                dg_ref[...] += dg
            _row_loop(T, fin)

    full = pl.BlockSpec((T, D_MODEL), lambda j: (0, 0))
    vec = pl.BlockSpec((1, D_MODEL), lambda j: (0, 0))
    return pl.pallas_call(
        body, name=name, grid=(nj,),
        in_specs=[full, full, vec,
                  pl.BlockSpec((D_MODEL, FF_CHUNK), lambda j: (0, j)),
                  pl.BlockSpec((D_MODEL, FF_CHUNK), lambda j: (0, j + nj)),
                  pl.BlockSpec((FF_CHUNK, D_MODEL), lambda j: (j, 0))],
        out_specs=(full, vec,
                   pl.BlockSpec((D_MODEL, FF_CHUNK), lambda j: (0, j)),
                   pl.BlockSpec((D_MODEL, FF_CHUNK), lambda j: (0, j)),
                   pl.BlockSpec((FF_CHUNK, D_MODEL), lambda j: (j, 0))),
        out_shape=(jax.ShapeDtypeStruct((T, D_MODEL), F32), jax.ShapeDtypeStruct((1, D_MODEL), F32),
                   jax.ShapeDtypeStruct((D_MODEL, D_FF), F32), jax.ShapeDtypeStruct((D_MODEL, D_FF), F32),
                   jax.ShapeDtypeStruct((D_FF, D_MODEL), F32)),
        scratch_shapes=[pltpu.VMEM((T, D_MODEL), BF16), pltpu.VMEM((T, D_MODEL), BF16),
                        pltpu.VMEM((T, D_MODEL), F32)],
        compiler_params=_params(("arbitrary",)))(x, dy, g, wi, wi, wo)


def proj_fwd(x, g, w, name):
    T = x.shape[0]
    nj = D_IN_PAD // IN_CHUNK

    def body(x_ref, g_ref, w_ref, o_ref, h_ref):
        @pl.when(pl.program_id(0) == 0)
        def _():
            def init(r):
                h_ref[r, :] = _rms(x_ref[r, :], g_ref[...]).astype(BF16)
            _row_loop(T, init)

        def blk(r):
            o_ref[r, :] = jnp.dot(h_ref[r, :], w_ref[...], preferred_element_type=F32)
        _row_loop(T, blk)

    return pl.pallas_call(
        body, name=name, grid=(nj,),
        in_specs=[pl.BlockSpec((T, D_MODEL), lambda j: (0, 0)), pl.BlockSpec((1, D_MODEL), lambda j: (0, 0)),
                  pl.BlockSpec((D_MODEL, IN_CHUNK), lambda j: (0, j))],
        out_specs=pl.BlockSpec((T, IN_CHUNK), lambda j: (0, j)),
        out_shape=jax.ShapeDtypeStruct((T, D_IN_PAD), F32),
        scratch_shapes=[pltpu.VMEM((T, D_MODEL), BF16)],
        compiler_params=_params(("arbitrary",)))(x, g, w)


def proj_bwd(x, dres, g, w, dp, name):
    T = x.shape[0]
    nj = D_IN_PAD // IN_CHUNK

    def body(x_ref, dres_ref, g_ref, w_ref, dp_ref, dx_ref, dg_ref, dw_ref, h_ref, dh_ref):
        j = pl.program_id(0)

        @pl.when(j == 0)
        def _():
            def init(r):
                h_ref[r, :] = _rms(x_ref[r, :], g_ref[...]).astype(BF16)
                dh_ref[r, :] = jnp.zeros((r.size, D_MODEL), F32)
            _row_loop(T, init)

        dw_ref[...] = jnp.zeros_like(dw_ref)

        def blk(r):
            dpb = dp_ref[r, :].astype(BF16)
            dw_ref[...] += _bdot_tn(h_ref[r, :], dpb)
            dh_ref[r, :] += _bdot_nt(dpb, w_ref[...])
        _row_loop(T, blk)

        @pl.when(j == nj - 1)
        def _():
            dg_ref[...] = jnp.zeros_like(dg_ref)

            def fin(r):
                dx, dg = _norm_bwd_rows(x_ref[r, :], g_ref[...], dh_ref[r, :], dres_ref[r, :])
                dx_ref[r, :] = dx
                dg_ref[...] += dg
            _row_loop(T, fin)

    full = pl.BlockSpec((T, D_MODEL), lambda j: (0, 0))
    vec = pl.BlockSpec((1, D_MODEL), lambda j: (0, 0))
    return pl.pallas_call(
        body, name=name, grid=(nj,),
        in_specs=[full, full, vec, pl.BlockSpec((D_MODEL, IN_CHUNK), lambda j: (0, j)),
                  pl.BlockSpec((T, IN_CHUNK), lambda j: (0, j))],
        out_specs=(full, vec, pl.BlockSpec((D_MODEL, IN_CHUNK), lambda j: (0, j))),
        out_shape=(jax.ShapeDtypeStruct((T, D_MODEL), F32), jax.ShapeDtypeStruct((1, D_MODEL), F32),
                   jax.ShapeDtypeStruct((D_MODEL, D_IN_PAD), F32)),
        scratch_shapes=[pltpu.VMEM((T, D_MODEL), BF16), pltpu.VMEM((T, D_MODEL), F32)],
        compiler_params=_params(("arbitrary",)))(x, dres, g, w, dp)


def out_fwd(mixed, w, x, name):
    T = x.shape[0]

    def body(m_ref, w_ref, x_ref, o_ref):
        o_ref[...] = x_ref[...] + jnp.dot(m_ref[...].astype(BF16), w_ref[...], preferred_element_type=F32)

    blk = pl.BlockSpec((ROWS, D_MODEL), lambda i: (i, 0))
    return pl.pallas_call(
        body, name=name, grid=(T // ROWS,),
        in_specs=[blk, pl.BlockSpec((D_MODEL, D_MODEL), lambda i: (0, 0)), blk],
        out_specs=blk, out_shape=jax.ShapeDtypeStruct((T, D_MODEL), F32),
        compiler_params=_params(("arbitrary",)))(mixed, w, x)


def out_bwd(mixed, w, dy, name):
    T = dy.shape[0]

    def body(m_ref, w_ref, dy_ref, dm_ref, dw_ref):
        @pl.when(pl.program_id(0) == 0)
        def _():
            dw_ref[...] = jnp.zeros_like(dw_ref)
        dyb = dy_ref[...].astype(BF16)
        dm_ref[...] = _bdot_nt(dyb, w_ref[...])
        dw_ref[...] += _bdot_tn(m_ref[...].astype(BF16), dyb)

    blk = pl.BlockSpec((ROWS, D_MODEL), lambda i: (i, 0))
    sq = pl.BlockSpec((D_MODEL, D_MODEL), lambda i: (0, 0))
    return pl.pallas_call(
        body, name=name, grid=(T // ROWS,),
        in_specs=[blk, sq, blk], out_specs=(blk, sq),
        out_shape=(jax.ShapeDtypeStruct((T, D_MODEL), F32), jax.ShapeDtypeStruct((D_MODEL, D_MODEL), F32)),
        compiler_params=_params(("arbitrary",)))(mixed, w, dy)


def loss_head(x, g, target, name):
    T = x.shape[0]

    def body(x_ref, g_ref, t_ref, loss_ref, dx_ref, dg_ref):
        @pl.when(pl.program_id(0) == 0)
        def _():
            loss_ref[...] = jnp.zeros_like(loss_ref)
            dg_ref[...] = jnp.zeros_like(dg_ref)
        xb = x_ref[...]
        rstd = lax.rsqrt(jnp.mean(xb * xb, axis=-1, keepdims=True) + NORM_EPS)
        xh = xb * rstd
        err = xh * g_ref[...] - t_ref[...]
        loss_ref[...] += 0.5 * jnp.sum(jnp.mean(err * err, axis=-1, keepdims=True), axis=0, keepdims=True)
        dy = err * (1.0 / D_MODEL)
        dg_ref[...] += jnp.sum(dy * xh, axis=0, keepdims=True)
        dxh = dy * g_ref[...]
        dx_ref[...] = rstd * (dxh - xh * jnp.mean(dxh * xh, axis=-1, keepdims=True))

    blk = pl.BlockSpec((ROWS, D_MODEL), lambda i: (i, 0))
    vec = pl.BlockSpec((1, D_MODEL), lambda i: (0, 0))
    return pl.pallas_call(
        body, name=name, grid=(T // ROWS,),
        in_specs=[blk, vec, blk], out_specs=(pl.BlockSpec((1, 1), lambda i: (0, 0)), blk, vec),
        out_shape=(jax.ShapeDtypeStruct((1, 1), F32), jax.ShapeDtypeStruct((T, D_MODEL), F32),
                   jax.ShapeDtypeStruct((1, D_MODEL), F32)),
        compiler_params=_params(("arbitrary",)))(x, g, target)


def rowwise_fwd(fn, rows, shared, out_widths, name):
    T = rows[0].shape[0]
    n_in = len(rows) + len(shared)

    def body(*refs):
        res = fn(*[r[...] for r in refs[:n_in]])
        for o, v in zip(refs[n_in:], res):
            o[...] = v

    in_specs = ([pl.BlockSpec((ROWS, a.shape[1]), lambda i: (i, 0)) for a in rows]
                + [pl.BlockSpec(a.shape, lambda i: (0, 0)) for a in shared])
    return pl.pallas_call(
        body, name=name, grid=(T // ROWS,), in_specs=in_specs,
        out_specs=tuple(pl.BlockSpec((ROWS, w), lambda i: (i, 0)) for w in out_widths),
        out_shape=tuple(jax.ShapeDtypeStruct((T, w), F32) for w in out_widths),
        compiler_params=_params(("arbitrary",)))(*rows, *shared)


def rowwise_bwd(fn, rows, shared, cts, name, ct_fn=None):
    T = rows[0].shape[0]
    nr, ns, nc = len(rows), len(shared), len(cts)

    def body(*refs):
        ins = [r[...] for r in refs[:nr + ns]]
        ctv = tuple(r[...] for r in refs[nr + ns:nr + ns + nc])
        outs = refs[nr + ns + nc:]
        _, vjp = jax.vjp(fn, *ins)
        grads = vjp(ct_fn(*ctv) if ct_fn is not None else ctv)
        for k in range(nr):
            outs[k][...] = grads[k]

        @pl.when(pl.program_id(0) == 0)
        def _():
            for k in range(ns):
                outs[nr + k][...] = jnp.zeros_like(outs[nr + k])
        for k in range(ns):
            outs[nr + k][...] += grads[nr + k]

    row_spec = lambda a: pl.BlockSpec((ROWS, a.shape[1]), lambda i: (i, 0))
    sh_spec = lambda a: pl.BlockSpec(a.shape, lambda i: (0, 0))
    return pl.pallas_call(
        body, name=name, grid=(T // ROWS,),
        in_specs=[row_spec(a) for a in rows] + [sh_spec(a) for a in shared] + [row_spec(a) for a in cts],
        out_specs=tuple([row_spec(a) for a in rows] + [sh_spec(a) for a in shared]),
        out_shape=tuple(jax.ShapeDtypeStruct(a.shape, F32) for a in list(rows) + list(shared)),
        compiler_params=_params(("arbitrary",)))(*rows, *shared, *cts)


def shift_rows(x, s):
    return jnp.pad(x, ((s, 0), (0, 0)))[:x.shape[0]]


def unshift_rows(x, s):
    return jnp.pad(x, ((0, s), (0, 0)))[s:]


def _neg_expm1(y):
    series = -(y * (1.0 + y * (0.5 + y * (1.0 / 6.0 + y * (1.0 / 24.0)))))
    return jnp.where(y > -0.05, series, 1.0 - jnp.exp(y))


def lru_pre_fn(x0, x1, x2, x3, first, w0, w1, w2, w3, cb, ga, gab, gx, gxb, lam):
    xc = w3 * x0 + w2 * x1 + w1 * x2 + w0 * x3 + cb
    r = jax.nn.sigmoid(_hdot(xc, ga) + gab)
    i = jax.nn.sigmoid(_hdot(xc, gx) + gxb)
    log_a = -LRU_C * r * jax.nn.softplus(-lam)
    a = jnp.exp(log_a)
    mult = jnp.where(first > 0.5, 1.0, jnp.sqrt(_neg_expm1(2.0 * log_a)))
    return a, mult * i * xc


def lru_post_fn(h, py, og):
    return (_rms(h * jax.nn.gelu(py), og),)


def lru_scan(a, b, reverse, name):
    T, C = a.shape
    nb = T // 8

    def body(a_ref, b_ref, h_ref):
        rows = lax.broadcasted_iota(jnp.int32, (8, C), 0)

        def blk(i, carry):
            j = nb - 1 - i if reverse else i
            r = pl.ds(pl.multiple_of(j * 8, 8), 8)
            A = a_ref[r, :]
            B = b_ref[r, :]
            for s in (1, 2, 4):
                if reverse:
                    keep = rows < 8 - s
                    sh = 8 - s
                else:
                    keep = rows >= s
                    sh = s
                Bs = jnp.where(keep, pltpu.roll(B, sh, 0), 0.0)
                As = jnp.where(keep, pltpu.roll(A, sh, 0), 1.0)
                B = B + A * Bs
                A = A * As
            hb = B + A * carry
            h_ref[r, :] = hb
            edge = 0 if reverse else 7
            return jnp.sum(jnp.where(rows == edge, hb, 0.0), axis=0, keepdims=True)

        lax.fori_loop(0, nb, blk, jnp.zeros((1, C), F32))

    full = pl.BlockSpec((T, C), lambda: (0, 0))
    return pl.pallas_call(body, name=name, in_specs=[full, full], out_specs=full,
                          out_shape=jax.ShapeDtypeStruct((T, C), F32), compiler_params=_params())(a, b)


def make_rwkv_pre_fn(has_vres):
    def fn(p, pp, *rest):
        if has_vres:
            vf, mu, w_up, w_b, a_up, a_b, g_up, kk_w, ka_w, vw1, vw2, vb = rest
        else:
            mu, w_up, w_b, a_up, a_b, g_up, kk_w, ka_w = rest
        xm = p + (pp - p) * mu
        r, k, v = xm[:, 0:384], xm[:, 384:768], xm[:, 768:1152]
        xw, xa, xg = xm[:, 1152:1216], xm[:, 1216:1280], xm[:, 1280:1408]
        w_log = -jax.nn.softplus(-(w_b + _hdot(jnp.tanh(xw), w_up))) - 0.5
        lw = -jnp.exp(w_log)
        a = jax.nn.sigmoid(a_b + _hdot(xa, a_up))
        g = _hdot(jax.nn.sigmoid(xg), g_up)
        if has_vres:
            v = v + (vf - v) * jax.nn.sigmoid(vb + _hdot(_hdot(v, vw1), vw2))
        kkx = k * kk_w
        kk = kkx * lax.rsqrt(_segsum(kkx * kkx) + 1e-6)
        k2 = k * (1.0 + (a - 1.0) * ka_w)
        return r, lw, k2, v, kk, a, g
    return fn


def rwkv_post_fn(y, r, k2, v, g, ln_g, ln_b, r_k):
    mean = _segsum(y) * (1.0 / HEAD_DIM)
    yc = y - mean
    var = _segsum(yc * yc) * (1.0 / HEAD_DIM)
    yn = yc * lax.rsqrt(var + GN_EPS) * ln_g + ln_b
    bonus = _segsum(r * k2 * r_k) * v
    return ((yn + bonus) * g,)


def _head_expander(first_lane):
    ri, ci = _iota2(128, MIX_W)
    return (ri == ci // HEAD_DIM + first_lane).astype(F32)


def gdn_pre_fn(x0, x1, x2, x3, ab, w0, w1, w2, w3, alog, dtb):
    qkv = jax.nn.silu(w3 * x0 + w2 * x1 + w1 * x2 + w0 * x3)
    q, k, v = qkv[:, 0:384], qkv[:, 384:768], qkv[:, 768:1152]
    q = q * lax.rsqrt(_segsum(q * q) + 1e-6) * (HEAD_DIM ** -0.5)
    k = k * lax.rsqrt(_segsum(k * k) + 1e-6)
    g = -jnp.exp(alog) * jax.nn.softplus(ab + dtb)
    beta = jax.nn.sigmoid(ab)
    ge = _xdot(g, functools.partial(_head_expander, 0))
    be = _xdot(beta, functools.partial(_head_expander, HEADS))
    return q, k, v, ge, be


def gdn_post_fn(o, z, ng):
    ms = _segsum(o * o) * (1.0 / HEAD_DIM)
    return (o * lax.rsqrt(ms + NORM_EPS) * ng * jax.nn.silu(z),)


def _neumann_inv(m):
    n = m.shape[-1]
    ri, ci = _iota2(n, n)
    eye = (ri == ci).astype(F32)
    md = jnp.where(ri // 16 == ci // 16, m, 0.0)
    mo = m - md
    t0 = eye + md
    p2 = _hdot(md, md)
    t0 = t0 + _hdot(t0, p2)
    p4 = _hdot(p2, p2)
    t0 = t0 + _hdot(t0, p4)
    p8 = _hdot(p4, p4)
    t0 = t0 + _hdot(t0, p8)
    nn = _hdot(t0, mo)
    n2 = _hdot(nn, nn)
    t1 = eye + nn + n2 + _hdot(nn, n2)
    return _hdot(t1, t0)


@jax.custom_vjp
def _inv_saved(m, t_saved):
    return t_saved


def _inv_saved_fwd(m, t_saved):
    return t_saved, t_saved


def _inv_saved_bwd(t_saved, dt):
    tt = jnp.swapaxes(t_saved, -1, -2)
    return _hdot(_hdot(tt, dt), tt), jnp.zeros_like(t_saved)


_inv_saved.defvjp(_inv_saved_fwd, _inv_saved_bwd)


def _heads(x):
    return jnp.concatenate([x[None, :, h * HEAD_DIM:(h + 1) * HEAD_DIM] for h in range(HEADS)], axis=0)


def _unheads(y):
    return jnp.concatenate([lax.index_in_dim(y, h, 0, keepdims=False) for h in range(HEADS)], axis=1)


def rwkv_heads(s0, r, lw, k2, v, kk, a, inv):
    n = r.shape[0]
    ri, ci = _iota2(n, n)
    low, strict = ri >= ci, ri > ci
    cs = _cumsum_rows(lw)
    cl = jnp.sum(lw, axis=0, keepdims=True)
    p_in, p_prev, p_inv = jnp.exp(cs), jnp.exp(cs - lw), jnp.exp(-cs)
    p_rest, p_all = jnp.exp(cl - cs), jnp.exp(cl)
    bd = kk * a
    at, rt = _heads(-kk * p_prev), _heads(r * p_in)
    bh, kh = _heads(bd * p_inv), _heads(k2 * p_inv)
    vh = _heads(v)
    m_ab = jnp.where(strict, _cdot_nt(at, bh), 0.0)
    m_ak = jnp.where(strict, _cdot_nt(at, kh), 0.0)
    m_rb = jnp.where(low, _cdot_nt(rt, bh), 0.0)
    m_rk = jnp.where(low, _cdot_nt(rt, kh), 0.0)
    sa = _cdot(inv(m_ab), _cdot_nt(at, s0) + _cdot(m_ak, vh))
    y = _cdot_nt(rt, s0) + _cdot(m_rb, sa) + _cdot(m_rk, vh)
    s1 = s0 * _heads(p_all) + _cdot_tn(sa, _heads(bd * p_rest)) + _cdot_tn(vh, _heads(k2 * p_rest))
    return _unheads(y), s1


def gdn_heads(s0, q, k, v, ge, be, inv):
    n = q.shape[0]
    ri, ci = _iota2(n, n)
    low, strict = ri >= ci, ri > ci
    gc = _cumsum_rows(ge)
    gl = jnp.sum(ge, axis=0, keepdims=True)
    gch = _heads(gc)
    decay = jnp.where(low, jnp.exp(jnp.where(low, gch - jnp.swapaxes(gch, 1, 2), 0.0)), 0.0)
    kb = k * be
    e = jnp.exp(gc)
    kh = _heads(k)
    m = -jnp.where(strict, _cdot_nt(_heads(kb), kh) * decay, 0.0)
    mr = jnp.where(low, _cdot_nt(_heads(q), kh) * decay, 0.0)
    u = _cdot(inv(m), _heads(v * be) - _cdot_nt(_heads(kb * e), s0))
    y = _cdot_nt(_heads(q * e), s0) + _cdot(mr, u)
    s1 = s0 * _heads(jnp.exp(gl)) + _cdot_tn(u, _heads(k * jnp.exp(gl - gc)))
    return _unheads(y), s1


def core_fwd(heads_fn, ins, name, hosted=()):
    T = ins[0].shape[0]
    nc = T // CHUNK
    n = len(ins)
    gather = ChipGather(list(hosted))
    ng = gather.n

    def body(*refs):
        hx = refs[n:n + ng]
        y_ref, s0_ref, t_ref = refs[n + ng:n + ng + 3]
        ho = refs[n + ng + 3:n + 2 * ng + 3]
        s_ref = refs[n + 2 * ng + 3]
        sems = refs[n + 2 * ng + 4:]
        c = pl.program_id(0)

        @pl.when(c == 0)
        def _():
            gather.start(hx, ho, sems)
            s_ref[...] = jnp.zeros_like(s_ref)

        s0 = s_ref[...]
        kept = []

        def inv(m):
            kept.append(_neumann_inv(m))
            return kept[0]

        y, s1 = heads_fn(s0, *[r[...] for r in refs[:n]], inv)
        y_ref[...] = y
        s0_ref[0] = s0
        t_ref[0] = kept[0]
        s_ref[...] = s1

        @pl.when(c == nc // 2)
        def _():
            gather.relay(hx, ho, sems)

        @pl.when(c == nc - 1)
        def _():
            gather.finish(hx, ho, sems)

    row = pl.BlockSpec((CHUNK, MIX_W), lambda c: (c, 0))
    st_shape = (HEADS, HEAD_DIM, HEAD_DIM)
    st = pl.BlockSpec((1,) + st_shape, lambda c: (c, 0, 0, 0))
    res = pl.pallas_call(
        body, name=name, grid=(nc,), in_specs=[row] * n + gather.in_specs,
        out_specs=tuple([row, st, st] + gather.out_specs),
        out_shape=tuple([jax.ShapeDtypeStruct((T, MIX_W), F32), jax.ShapeDtypeStruct((nc,) + st_shape, F32),
                         jax.ShapeDtypeStruct((nc,) + st_shape, F32)] + gather.out_shape),
        scratch_shapes=[pltpu.VMEM(st_shape, F32)] + gather.scratch,
        compiler_params=_params(("arbitrary",)))(*ins, *hosted)
    return res[0], res[1], res[2], list(res[3:])


def core_bwd(heads_fn, ins, s0_all, t_all, dy, name):
    T = ins[0].shape[0]
    nc = T // CHUNK
    n = len(ins)

    def body(*refs):
        s0_ref, t_ref, dy_ref = refs[n:n + 3]
        outs = refs[n + 3:n + 3 + n]
        ds_ref = refs[n + 3 + n]

        @pl.when(pl.program_id(0) == 0)
        def _():
            ds_ref[...] = jnp.zeros_like(ds_ref)

        t_saved = t_ref[0]
        f = lambda s0, *xs: heads_fn(s0, *xs, lambda m: _inv_saved(m, t_saved))
        _, vjp = jax.vjp(f, s0_ref[0], *[r[...] for r in refs[:n]])
        grads = vjp((dy_ref[...], ds_ref[...]))
        ds_ref[...] = grads[0]
        for k in range(n):
            outs[k][...] = grads[1 + k]

    row = pl.BlockSpec((CHUNK, MIX_W), lambda c: (nc - 1 - c, 0))
    st_shape = (HEADS, HEAD_DIM, HEAD_DIM)
    st = pl.BlockSpec((1,) + st_shape, lambda c: (nc - 1 - c, 0, 0, 0))
    return pl.pallas_call(
        body, name=name, grid=(nc,), in_specs=[row] * n + [st, st, row], out_specs=tuple([row] * n),
        out_shape=tuple(jax.ShapeDtypeStruct((T, MIX_W), F32) for _ in range(n)),
        scratch_shapes=[pltpu.VMEM(st_shape, F32)],
        compiler_params=_params(("arbitrary",)))(*ins, s0_all, t_all, dy)


def _block_diag(w):
    out = jnp.zeros((LRU_W, LRU_W), w.dtype)
    for n in range(LRU_BLOCKS):
        out = lax.dynamic_update_slice(out, w[n], (n * 64, n * 64))
    return out


def _block_diag_grad(g):
    return jnp.stack([g[n * 64:(n + 1) * 64, n * 64:(n + 1) * 64] for n in range(LRU_BLOCKS)])


def _row(v):
    return v.reshape(1, -1)


def _pad128(v):
    return jnp.pad(v.reshape(1, -1), ((0, 0), (0, 128 - v.size)))


def _layer_shared(w, l):
    cw = w['lru_conv_w'][l]
    lru_pre = [_row(cw[0]), _row(cw[1]), _row(cw[2]), _row(cw[3]), _row(w['lru_conv_b'][l]),
               _block_diag(w['lru_gate_a_w'][l]), _row(w['lru_gate_a_b'][l]),
               _block_diag(w['lru_gate_x_w'][l]), _row(w['lru_gate_x_b'][l]), _row(w['lru_lambda'][l])]
    rw_pre = [_row(w['rwkv_mu'][l]), w['rwkv_w_up'][l], _row(w['rwkv_w_bias'][l]), w['rwkv_a_up'][l],
              _row(w['rwkv_a_bias'][l]), w['rwkv_g_up'][l], _row(w['rwkv_k_k'][l]), _row(w['rwkv_k_a'][l])]
    if l > 0:
        rw_pre += [w['rwkv_vres_w1'][l - 1], w['rwkv_vres_w2'][l - 1], _row(w['rwkv_vres_b'][l - 1])]
    rw_post = [_row(w['rwkv_ln_g'][l]), _row(w['rwkv_ln_b'][l]), _row(w['rwkv_r_k'][l])]
    gw = w['gdn_conv_w'][l]
    gdn_pre = [_row(gw[0]), _row(gw[1]), _row(gw[2]), _row(gw[3]), _pad128(w['gdn_a_log'][l]),
               _pad128(w['gdn_dt_bias'][l])]
    gdn_post = [_row(jnp.tile(w['gdn_norm'][l], HEADS))]
    return dict(lru_pre=lru_pre, lru_post=[_row(w['lru_out_norm'][l])], rw_pre=rw_pre, rw_post=rw_post,
                gdn_pre=gdn_pre, gdn_post=gdn_post)


def _mixer_fwd(p, sh, l, v_first, host_rwkv=(), host_gdn=()):
    T = p.shape[0]
    lx, ly = p[:, 0:256], p[:, 256:512]
    prw, qkv, z, ab = p[:, 512:1920], p[:, 1920:3072], p[:, 3072:3456], p[:, 3456:3584]
    first = jnp.zeros((T, LRU_W), F32).at[0].set(1.0)
    lru_rows = [lx, shift_rows(lx, 1), shift_rows(lx, 2), shift_rows(lx, 3), first]
    a, b = rowwise_fwd(lru_pre_fn, lru_rows, sh['lru_pre'], (LRU_W, LRU_W), f"lru_pre_fwd{l}")
    hseq = lru_scan(a, b, False, f"lru_scan_fwd{l}")
    (y_lru,) = rowwise_fwd(lru_post_fn, [hseq, ly], sh['lru_post'], (LRU_W,), f"lru_post_fwd{l}")

    rw_rows = [prw, shift_rows(prw, 1)] + ([v_first] if l > 0 else [])
    rw = rowwise_fwd(make_rwkv_pre_fn(l > 0), rw_rows, sh['rw_pre'], (MIX_W,) * 7, f"rwkv_pre_fwd{l}")
    r, lw, k2, v, kk, ar, g = rw
    y_raw, rs0, rt, got_rwkv = core_fwd(rwkv_heads, [r, lw, k2, v, kk, ar], f"rwkv_core_fwd{l}", host_rwkv)
    (y_rw,) = rowwise_fwd(rwkv_post_fn, [y_raw, r, k2, v, g], sh['rw_post'], (MIX_W,), f"rwkv_post_fwd{l}")

    gdn_rows = [qkv, shift_rows(qkv, 1), shift_rows(qkv, 2), shift_rows(qkv, 3), ab]
    gd = rowwise_fwd(gdn_pre_fn, gdn_rows, sh['gdn_pre'], (MIX_W,) * 5, f"gdn_pre_fwd{l}")
    o_raw, gs0, gt, got_gdn = core_fwd(gdn_heads, list(gd), f"gdn_core_fwd{l}", host_gdn)
    (y_gdn,) = rowwise_fwd(gdn_post_fn, [o_raw, z], sh['gdn_post'], (MIX_W,), f"gdn_post_fwd{l}")

    mixed = jnp.concatenate([y_lru, y_rw, y_gdn], axis=1)
    saved = dict(lru_rows=lru_rows, a=a, hseq=hseq, ly=ly, rw_rows=rw_rows, rw=rw, y_raw=y_raw, rs0=rs0, rt=rt,
                 gdn_rows=gdn_rows, gd=gd, o_raw=o_raw, gs0=gs0, gt=gt, z=z)
    v_layer0 = v if l == 0 else None
    return mixed, saved, v_layer0, got_rwkv, got_gdn


def _mixer_bwd(dmixed, sv, sh, l, dv_first):
    d_lru, d_rw, d_gdn = dmixed[:, 0:256], dmixed[:, 256:640], dmixed[:, 640:1024]
    gw = {}

    dh, dly, d_og = rowwise_bwd(lru_post_fn, [sv['hseq'], sv['ly']], sh['lru_post'], [d_lru], f"lru_post_bwd{l}")
    gscan = lru_scan(unshift_rows(sv['a'], 1), dh, True, f"lru_scan_bwd{l}")
    res = rowwise_bwd(lru_pre_fn, sv['lru_rows'], sh['lru_pre'], [gscan, shift_rows(sv['hseq'], 1)],
                      f"lru_pre_bwd{l}", ct_fn=lambda gs, hp: (gs * hp, gs))
    dlx = res[0] + unshift_rows(res[1], 1) + unshift_rows(res[2], 2) + unshift_rows(res[3], 3)
    dw0, dw1, dw2, dw3, dcb, dga, dgab, dgx, dgxb, dlam = res[5:]
    gw['lru_conv_w'] = jnp.concatenate([dw0, dw1, dw2, dw3], axis=0)
    gw['lru_conv_b'] = dcb[0]
    gw['lru_gate_a_w'] = _block_diag_grad(dga)
    gw['lru_gate_a_b'] = dgab.reshape(LRU_BLOCKS, 64)
    gw['lru_gate_x_w'] = _block_diag_grad(dgx)
    gw['lru_gate_x_b'] = dgxb.reshape(LRU_BLOCKS, 64)
    gw['lru_lambda'] = dlam[0]
    gw['lru_out_norm'] = d_og[0]

    r, lw, k2, v, kk, ar, g = sv['rw']
    res = rowwise_bwd(rwkv_post_fn, [sv['y_raw'], r, k2, v, g], sh['rw_post'], [d_rw], f"rwkv_post_bwd{l}")
    dy_raw, dr_p, dk2_p, dv_p, dg = res[:5]
    gw['rwkv_ln_g'], gw['rwkv_ln_b'], gw['rwkv_r_k'] = res[5][0], res[6][0], res[7].reshape(HEADS, HEAD_DIM)
    dr_c, dlw, dk2_c, dv_c, dkk, dar = core_bwd(rwkv_heads, [r, lw, k2, v, kk, ar], sv['rs0'], sv['rt'], dy_raw,
                                                 f"rwkv_core_bwd{l}")
    cts = [dr_p, dr_c, dlw, dk2_p, dk2_c, dv_p, dv_c, dkk, dar, dg]
    if l == 0:
        cts.append(dv_first)
        ct_fn = lambda a1, a2, b, c1, c2, d1, d2, e, f, gg, vf: (a1 + a2, b, c1 + c2, d1 + d2 + vf, e, f, gg)
    else:
        ct_fn = lambda a1, a2, b, c1, c2, d1, d2, e, f, gg: (a1 + a2, b, c1 + c2, d1 + d2, e, f, gg)
    res = rowwise_bwd(make_rwkv_pre_fn(l > 0), sv['rw_rows'], sh['rw_pre'], cts, f"rwkv_pre_bwd{l}", ct_fn=ct_fn)
    dprw = res[0] + unshift_rows(res[1], 1)
    nrow = len(sv['rw_rows'])
    dv_first_out = res[2] if l > 0 else None
    sg = res[nrow:]
    gw['rwkv_mu'], gw['rwkv_w_up'], gw['rwkv_w_bias'], gw['rwkv_a_up'] = sg[0][0], sg[1], sg[2][0], sg[3]
    gw['rwkv_a_bias'], gw['rwkv_g_up'], gw['rwkv_k_k'], gw['rwkv_k_a'] = sg[4][0], sg[5], sg[6][0], sg[7][0]
    if l > 0:
        gw['rwkv_vres_w1'], gw['rwkv_vres_w2'], gw['rwkv_vres_b'] = sg[8], sg[9], sg[10][0]

    do_raw, dz, d_ng = rowwise_bwd(gdn_post_fn, [sv['o_raw'], sv['z']], sh['gdn_post'], [d_gdn], f"gdn_post_bwd{l}")
    gw['gdn_norm'] = jnp.sum(d_ng.reshape(HEADS, HEAD_DIM), axis=0)
    dgd = core_bwd(gdn_heads, list(sv['gd']), sv['gs0'], sv['gt'], do_raw, f"gdn_core_bwd{l}")
    res = rowwise_bwd(gdn_pre_fn, sv['gdn_rows'], sh['gdn_pre'], list(dgd), f"gdn_pre_bwd{l}")
    dqkv = res[0] + unshift_rows(res[1], 1) + unshift_rows(res[2], 2) + unshift_rows(res[3], 3)
    dab = res[4]
    gw['gdn_conv_w'] = jnp.concatenate(res[5:9], axis=0)
    gw['gdn_a_log'], gw['gdn_dt_bias'] = res[9][0, :HEADS], res[10][0, :HEADS]

    dp = jnp.concatenate([dlx, dly, dprw, dqkv, dz, dab], axis=1)
    return dp, gw, dv_first_out


IN_SHARD = D_IN // N_CHIPS
IN_SHARD_PAD = D_IN_PAD // N_CHIPS


def _cols_to_chips(g, n=N_CHIPS):
    r = g.shape[0]
    return jnp.transpose(g.reshape(r, n, -1), (1, 0, 2))


def _cols_from_chips(g):
    return jnp.transpose(g, (1, 0, 2)).reshape(g.shape[1], -1)


def _w_in_from_chips(g):
    nat = _cols_from_chips(g[:, :, :IN_SHARD])
    return jnp.pad(nat, ((0, 0), (0, D_IN_PAD - D_IN)))


def _w_in_to_chips(g):
    return jnp.pad(_cols_to_chips(g[:, :D_IN]), ((0, 0), (0, 0), (0, IN_SHARD_PAD - IN_SHARD)))


def _natural(name, g):
    if name == 'w_in':
        return _w_in_from_chips(g)
    if BIG[name] == 2:
        return _cols_from_chips(g)
    return g.reshape(-1, g.shape[2])


def local_step(x, target, w, wb, shards=None):
    def hosted(keys):
        return [shards[k] for k in keys] if shards is not None else []

    def arrived(keys, gathered):
        for (name, layer), g in zip(keys if shards is not None else [], gathered):
            wb[name][layer] = _natural(name, g)

    saved = []
    v_first = None
    for l in range(N_LAYERS):
        sh = _layer_shared(w, l)
        for_mixer = [('w_in', l), ('w_out', l)]
        for_ffn2 = [('ffn2_wi', l), ('ffn2_wo', l)]
        for_next = [('ffn1_wi', l + 1), ('ffn1_wo', l + 1)] if l + 1 < N_LAYERS else []
        x1, got = ffn_fwd(x, _row(w['ffn1_norm'][l]), wb['ffn1_wi'][l], wb['ffn1_wo'][l], f"ffn1_fwd{l}",
                          hosted(for_mixer))
        arrived(for_mixer, got)
        p = proj_fwd(x1, _row(w['mix_norm'][l]), wb['w_in'][l], f"proj_fwd{l}")
        mixed, sv, v0, got_ffn2, got_next = _mixer_fwd(p, sh, l, v_first, hosted(for_ffn2), hosted(for_next))
        arrived(for_ffn2, got_ffn2)
        arrived(for_next, got_next)
        if l == 0:
            v_first = v0
        x2 = out_fwd(mixed, wb['w_out'][l], x1, f"out_fwd{l}")
        x3, _ = ffn_fwd(x2, _row(w['ffn2_norm'][l]), wb['ffn2_wi'][l], wb['ffn2_wo'][l], f"ffn2_fwd{l}")
        saved.append(dict(x0=x, x1=x1, x2=x2, mixed=mixed, sv=sv, sh=sh))
        x = x3

    loss, dx, dgf = loss_head(x, _row(w['final_norm']), target, "loss_head")
    per_layer = [None] * N_LAYERS
    dv_first = jnp.zeros((x.shape[0], MIX_W), F32)
    for l in reversed(range(N_LAYERS)):
        s = saved[l]
        gw = {}
        dx, dg2, dwg, dwu, dwo = ffn_bwd(s['x2'], dx, _row(w['ffn2_norm'][l]), wb['ffn2_wi'][l], wb['ffn2_wo'][l],
                                         f"ffn2_bwd{l}")
        wi_parts = lambda dwg, dwu: (dwg, dwu)
        row_parts = lambda dw: dw.reshape(N_CHIPS, -1, dw.shape[1])
        gw['ffn2_norm'], gw['ffn2_wi'], gw['ffn2_wo'] = dg2[0], wi_parts(dwg, dwu), row_parts(dwo)
        dmixed, dw_out = out_bwd(s['mixed'], wb['w_out'][l], dx, f"out_bwd{l}")
        gw['w_out'] = row_parts(dw_out)
        dp, gmix, dvf = _mixer_bwd(dmixed, s['sv'], s['sh'], l, dv_first)
        if l > 0:
            dv_first = dvf
        gw.update(gmix)
        dx, dgm, dwin = proj_bwd(s['x1'], dx, _row(w['mix_norm'][l]), wb['w_in'][l], dp, f"proj_bwd{l}")
        gw['mix_norm'], gw['w_in'] = dgm[0], _w_in_to_chips(dwin)
        dx, dg1, dwg, dwu, dwo = ffn_bwd(s['x0'], dx, _row(w['ffn1_norm'][l]), wb['ffn1_wi'][l], wb['ffn1_wo'][l],
                                         f"ffn1_bwd{l}")
        gw['ffn1_norm'], gw['ffn1_wi'], gw['ffn1_wo'] = dg1[0], wi_parts(dwg, dwu), row_parts(dwo)
        per_layer[l] = gw

    grads = {'final_norm': dgf[0]}
    for name in WEIGHTS:
        if name == 'final_norm':
            continue
        if name in BIG:
            grads[name] = [per_layer[l][name] for l in range(N_LAYERS)]
        elif name.startswith('rwkv_vres'):
            grads[name] = per_layer[1][name][None]
        else:
            grads[name] = jnp.stack([per_layer[l][name] for l in range(N_LAYERS)])
    return loss[0, 0], dx, grads


ANY = pl.BlockSpec(memory_space=pl.ANY)


def _coords():
    return lax.axis_index("x"), lax.axis_index("y"), lax.axis_index("c")


def _other_chips(x, y):
    return [((x + 1) % 2, y), (x, (y + 1) % 2), ((x + 1) % 2, (y + 1) % 2)]


def allreduce_small(pack, name):
    R = pack.shape[0]

    def body(x_ref, o_ref, buf, send_sems, recv_sems):
        x, y, c = _coords()
        me = 4 * x + 2 * y + c
        buf[me] = x_ref[...]
        copies = []
        for k in range(1, 8):
            peer = ((x + (k >> 2)) % 2, (y + ((k >> 1) & 1)) % 2, (c + (k & 1)) % 2)
            cp = pltpu.make_async_remote_copy(src_ref=x_ref, dst_ref=buf.at[me], send_sem=send_sems.at[k - 1],
                                              recv_sem=recv_sems.at[k - 1], device_id=peer, device_id_type=MESH)
            cp.start()
            copies.append(cp)
        for cp in copies:
            cp.wait()
        acc = buf[0]
        for d in range(1, 8):
            acc = acc + buf[d]
        o_ref[...] = acc

    vm = pl.BlockSpec(memory_space=pltpu.VMEM)
    return pl.pallas_call(
        body, name=name, in_specs=[vm], out_specs=vm, out_shape=jax.ShapeDtypeStruct((R, 128), F32),
        scratch_shapes=[pltpu.VMEM((8, R, 128), F32), pltpu.SemaphoreType.DMA((7,)), pltpu.SemaphoreType.DMA((7,))],
        compiler_params=_params())(pack)


class ChipGather:
    def __init__(self, shards):
        self.shapes = [s.shape for s in shards]
        self.n = len(shards)
        self.in_specs = [ANY] * self.n
        self.out_specs = [ANY] * self.n
        self.out_shape = [jax.ShapeDtypeStruct((N_CHIPS,) + s.shape, s.dtype) for s in shards]
        self.scratch = [pltpu.SemaphoreType.DMA((6 * self.n,)), pltpu.SemaphoreType.DMA((6 * self.n,)),
                        pltpu.SemaphoreType.DMA((self.n,))] if self.n else []

    def _rows(self, a, core):
        rh = self.shapes[a][0] // 2
        return pl.ds(pl.multiple_of(core * rh, 16), rh)

    def _copies(self, kind, x_refs, o_refs, sems):
        send_sems, recv_sems, local_sems = sems
        x, y, c = _coords()
        s_me = 2 * x + y
        sib = (x, y, 1 - c)

        def copy(a, k, src, dst, to):
            return pltpu.make_async_remote_copy(src_ref=src, dst_ref=dst, send_sem=send_sems.at[6 * a + k],
                                                recv_sem=recv_sems.at[6 * a + k], device_id=to, device_id_type=MESH)

        if kind == 'own':
            return [pltpu.make_async_copy(x_refs[a], o_refs[a].at[s_me], local_sems.at[a]) for a in range(self.n)]
        out = []
        for j, (px, py) in enumerate(_other_chips(x, y)):
            for a in range(self.n):
                mine = self._rows(a, c)
                part = o_refs[a].at[2 * px + py, mine]
                if kind == 'sent':
                    out.append(copy(a, j, x_refs[a].at[mine], o_refs[a].at[s_me, mine], (px, py, c)))
                elif kind == 'arrived':
                    out.append(copy(a, j, part, part, (px, py, c)))
                elif kind == 'passed':
                    out.append(copy(a, 3 + j, part, part, sib))
                else:
                    theirs = o_refs[a].at[2 * px + py, self._rows(a, 1 - c)]
                    out.append(copy(a, 3 + j, theirs, theirs, sib))
        return out

    def start(self, x_refs, o_refs, sems):
        if not self.n:
            return
        for cp in self._copies('own', x_refs, o_refs, sems) + self._copies('sent', x_refs, o_refs, sems):
            cp.start()

    def relay(self, x_refs, o_refs, sems):
        if not self.n:
            return
        for got, fw in zip(self._copies('arrived', x_refs, o_refs, sems),
                           self._copies('passed', x_refs, o_refs, sems)):
            got.wait_recv()
            fw.start()

    def finish(self, x_refs, o_refs, sems):
        if not self.n:
            return
        for cp in self._copies('from_sibling', x_refs, o_refs, sems):
            cp.wait_recv()
        for cp in self._copies('sent', x_refs, o_refs, sems) + self._copies('passed', x_refs, o_refs, sems):
            cp.wait_send()
        for cp in self._copies('own', x_refs, o_refs, sems):
            cp.wait()


def allgather_chips(shards, name):
    gather = ChipGather(shards)
    n = gather.n

    def body(*refs):
        x_refs, o_refs, sems = refs[:n], refs[n:2 * n], refs[2 * n:]
        gather.start(x_refs, o_refs, sems)
        gather.relay(x_refs, o_refs, sems)
        gather.finish(x_refs, o_refs, sems)

    return pl.pallas_call(
        body, name=name, in_specs=gather.in_specs, out_specs=tuple(gather.out_specs),
        out_shape=tuple(gather.out_shape), scratch_shapes=gather.scratch, compiler_params=_params())(*shards)


def sibling_swap(srcs, halves, name):
    n = len(srcs)
    row_axis = [s.ndim - 2 for s in srcs]
    out_shapes = [s.shape[:ax] + (s.shape[ax] // 2,) + s.shape[ax + 1:] if halves else s.shape
                  for s, ax in zip(srcs, row_axis)]

    def body(*refs):
        x_refs, o_refs = refs[:n], refs[n:2 * n]
        send_sems, recv_sems = refs[2 * n:]
        x, y, c = _coords()
        copies = []
        for a in range(n):
            part = x_refs[a]
            if halves:
                rh = srcs[a].shape[row_axis[a]] // 2
                theirs = pl.ds(pl.multiple_of((1 - c) * rh, 16), rh)
                part = part.at[:, theirs] if row_axis[a] == 1 else part.at[theirs]
            cp = pltpu.make_async_remote_copy(src_ref=part, dst_ref=o_refs[a], send_sem=send_sems.at[a],
                                              recv_sem=recv_sems.at[a], device_id=(x, y, 1 - c), device_id_type=MESH)
            cp.start()
            copies.append(cp)
        for cp in copies:
            cp.wait()

    return pl.pallas_call(
        body, name=name, in_specs=[ANY] * n, out_specs=tuple([ANY] * n),
        out_shape=tuple(jax.ShapeDtypeStruct(sh, s.dtype) for sh, s in zip(out_shapes, srcs)),
        scratch_shapes=[pltpu.SemaphoreType.DMA((n,)), pltpu.SemaphoreType.DMA((n,))],
        compiler_params=_params())(*srcs)


def scatter_chips(parts, name):
    n = len(parts)

    def body(*refs):
        x_refs, o_refs = refs[:n], refs[n:2 * n]
        send_sems, recv_sems = refs[2 * n:]
        x, y, c = _coords()
        copies = []
        for j, (px, py) in enumerate(_other_chips(x, y)):
            for a in range(n):
                cp = pltpu.make_async_remote_copy(src_ref=x_refs[a].at[2 * px + py], dst_ref=o_refs[a].at[j],
                                                  send_sem=send_sems.at[3 * a + j], recv_sem=recv_sems.at[3 * a + j],
                                                  device_id=(px, py, c), device_id_type=MESH)
                cp.start()
                copies.append(cp)
        for cp in copies:
            cp.wait()

    return pl.pallas_call(
        body, name=name, in_specs=[ANY] * n, out_specs=tuple([ANY] * n),
        out_shape=tuple(jax.ShapeDtypeStruct((3,) + p.shape[1:], p.dtype) for p in parts),
        scratch_shapes=[pltpu.SemaphoreType.DMA((3 * n,)), pltpu.SemaphoreType.DMA((3 * n,))],
        compiler_params=_params())(*parts)


def _row_block(rows):
    return max(b for b in range(16, 257, 16) if rows % b == 0)


def chip_sum(gpack, recv, core, name):
    n, R, W = gpack.shape
    rh = R // 2
    rb = _row_block(rh)
    nb = rh // rb

    def body(c_ref, g_ref, r_ref, o_ref, ob_ref):
        s = g_ref[...] + r_ref[...]
        o_ref[...] = s
        ob_ref[...] = s.astype(BF16)

    blk = pl.BlockSpec((1, rb, W), lambda i, j, c_ref: (i, j, 0))
    spec = pltpu.PrefetchScalarGridSpec(
        num_scalar_prefetch=1, grid=(n, nb),
        in_specs=[pl.BlockSpec((1, rb, W), lambda i, j, c_ref: (i, c_ref[0] * nb + j, 0)), blk],
        out_specs=(blk, blk))
    return pl.pallas_call(
        body, name=name, grid_spec=spec,
        out_shape=(jax.ShapeDtypeStruct((n, rh, W), F32), jax.ShapeDtypeStruct((n, rh, W), BF16)),
        compiler_params=_params(("arbitrary", "arbitrary")))(core, gpack, recv)


def chip_sum_cols(gate, up, recv_gate, recv_up, core, name):
    R, W = gate.shape
    cw = W // 2
    rh = R // 2
    rb = _row_block(rh)
    nb = rh // rb

    def body(c_ref, g_ref, u_ref, rg_ref, ru_ref, o_ref, ob_ref):
        s = jnp.where(pl.program_id(0) < 2, g_ref[...] + rg_ref[...], u_ref[...] + ru_ref[...])
        o_ref[0] = s
        ob_ref[0] = s.astype(BF16)

    gate_col = lambda s: jnp.minimum(s, 1)
    up_col = lambda s: jnp.maximum(s - 2, 0)
    out = pl.BlockSpec((1, rb, cw), lambda s, j, c_ref: (s, j, 0))
    spec = pltpu.PrefetchScalarGridSpec(
        num_scalar_prefetch=1, grid=(N_CHIPS, nb),
        in_specs=[pl.BlockSpec((rb, cw), lambda s, j, c_ref: (c_ref[0] * nb + j, gate_col(s))),
                  pl.BlockSpec((rb, cw), lambda s, j, c_ref: (c_ref[0] * nb + j, up_col(s))),
                  pl.BlockSpec((rb, cw), lambda s, j, c_ref: (j, gate_col(s))),
                  pl.BlockSpec((rb, cw), lambda s, j, c_ref: (j, up_col(s)))],
        out_specs=(out, out))
    return pl.pallas_call(
        body, name=name, grid_spec=spec,
        out_shape=(jax.ShapeDtypeStruct((N_CHIPS, rh, cw), F32), jax.ShapeDtypeStruct((N_CHIPS, rh, cw), BF16)),
        compiler_params=_params(("arbitrary", "arbitrary")))(core, gate, up, recv_gate, recv_up)


def shard_sum(own, recv, name):
    R, W = own.shape
    rb = _row_block(R)

    def body(a_ref, r_ref, o_ref):
        acc = a_ref[...]
        for j in range(3):
            acc = acc + r_ref[j].astype(F32)
        o_ref[...] = acc

    return pl.pallas_call(
        body, name=name, grid=(R // rb,),
        in_specs=[pl.BlockSpec((rb, W), lambda i: (i, 0)), pl.BlockSpec((3, rb, W), lambda i: (0, i, 0))],
        out_specs=pl.BlockSpec((rb, W), lambda i: (i, 0)), out_shape=jax.ShapeDtypeStruct((R, W), F32),
        compiler_params=_params(("arbitrary",)))(own, recv)


def adamw(w, m, v, g, name):
    L, R, C = w.shape
    rb = max(b for b in range(8, 257, 8) if R % b == 0)
    bc1 = 1.0 - ADAM_B1 ** ADAM_STEP
    bc2 = 1.0 - ADAM_B2 ** ADAM_STEP

    def body(w_ref, m_ref, v_ref, g_ref, d_ref, nm_ref, nv_ref):
        gv = g_ref[...]
        nm = ADAM_B1 * m_ref[...] + (1.0 - ADAM_B1) * gv
        nv = ADAM_B2 * v_ref[...] + (1.0 - ADAM_B2) * (gv * gv)
        d_ref[...] = -ADAM_LR * ((nm / bc1) / (jnp.sqrt(nv / bc2) + ADAM_EPS) + ADAM_WD * w_ref[...])
        nm_ref[...] = nm
        nv_ref[...] = nv

    blk = pl.BlockSpec((1, rb, C), lambda l, i: (l, i, 0))
    sh = jax.ShapeDtypeStruct((L, R, C), F32)
    return pl.pallas_call(body, name=name, grid=(L, R // rb), in_specs=[blk] * 4, out_specs=(blk,) * 3,
                          out_shape=(sh, sh, sh), compiler_params=_params(("arbitrary", "arbitrary")))(w, m, v, g)


SMALL = [n for n in WEIGHTS if n not in BIG]


PACK_TILE = 8 * 128


def _pack(arrays):
    blocks = []
    for a in arrays:
        flat = a.reshape(-1)
        flat = jnp.pad(flat, (0, -flat.size % PACK_TILE))
        blocks.append(flat.reshape(-1, 128))
    return jnp.concatenate(blocks, axis=0)


def _unpack(pack, shapes):
    out, row = [], 0
    for shape in shapes:
        size = int(np.prod(shape))
        rows = -(-size // PACK_TILE) * 8
        out.append(pack[row:row + rows].reshape(-1)[:size].reshape(shape))
        row += rows
    return out


def _pad_lanes(a):
    return jnp.pad(a, ((0, 0), (0, -a.shape[1] % 128)))


def _local_shard(full, axis, chip):
    size = full.shape[axis] // N_CHIPS
    return lax.dynamic_slice_in_dim(full, chip * size, size, axis)


def kernel(x, ffn1_norm, ffn1_wi, ffn1_wo, mix_norm, w_in, w_out, lru_conv_w, lru_conv_b, lru_gate_a_w, lru_gate_a_b, lru_gate_x_w, lru_gate_x_b, lru_lambda, lru_out_norm, rwkv_mu, rwkv_w_up, rwkv_w_bias, rwkv_a_up, rwkv_a_bias, rwkv_g_up, rwkv_k_k, rwkv_k_a, rwkv_r_k, rwkv_ln_g, rwkv_ln_b, rwkv_vres_w1, rwkv_vres_w2, rwkv_vres_b, gdn_conv_w, gdn_a_log, gdn_dt_bias, gdn_norm, ffn2_norm, ffn2_wi, ffn2_wo, final_norm, loss_target, m_ffn1_norm, m_ffn1_wi, m_ffn1_wo, m_mix_norm, m_w_in, m_w_out, m_lru_conv_w, m_lru_conv_b, m_lru_gate_a_w, m_lru_gate_a_b, m_lru_gate_x_w, m_lru_gate_x_b, m_lru_lambda, m_lru_out_norm, m_rwkv_mu, m_rwkv_w_up, m_rwkv_w_bias, m_rwkv_a_up, m_rwkv_a_bias, m_rwkv_g_up, m_rwkv_k_k, m_rwkv_k_a, m_rwkv_r_k, m_rwkv_ln_g, m_rwkv_ln_b, m_rwkv_vres_w1, m_rwkv_vres_w2, m_rwkv_vres_b, m_gdn_conv_w, m_gdn_a_log, m_gdn_dt_bias, m_gdn_norm, m_ffn2_norm, m_ffn2_wi, m_ffn2_wo, m_final_norm, v_ffn1_norm, v_ffn1_wi, v_ffn1_wo, v_mix_norm, v_w_in, v_w_out, v_lru_conv_w, v_lru_conv_b, v_lru_gate_a_w, v_lru_gate_a_b, v_lru_gate_x_w, v_lru_gate_x_b, v_lru_lambda, v_lru_out_norm, v_rwkv_mu, v_rwkv_w_up, v_rwkv_w_bias, v_rwkv_a_up, v_rwkv_a_bias, v_rwkv_g_up, v_rwkv_k_k, v_rwkv_k_a, v_rwkv_r_k, v_rwkv_ln_g, v_rwkv_ln_b, v_rwkv_vres_w1, v_rwkv_vres_w2, v_rwkv_vres_b, v_gdn_conv_w, v_gdn_a_log, v_gdn_dt_bias, v_gdn_norm, v_ffn2_norm, v_ffn2_wi, v_ffn2_wo, v_final_norm):
    args = locals()
    w_loc = {n: args[n] for n in WEIGHTS}
    m_loc = {n: args['m_' + n] for n in WEIGHTS}
    v_loc = {n: args['v_' + n] for n in WEIGHTS}
    chip = 2 * lax.axis_index("x") + lax.axis_index("y")
    core = lax.axis_index("c")

    big = [(n, l) for n in BIG for l in range(N_LAYERS)]
    shards = {(n, l): _pad_lanes(w_loc[n][l].astype(BF16)) for n, l in big}
    first = [('ffn1_wi', 0), ('ffn1_wo', 0)]
    wb = {n: [None] * N_LAYERS for n in BIG}
    for (n, l), g in zip(first, allgather_chips([shards[k] for k in first], "allgather_first")):
        wb[n][l] = _natural(n, g)

    sm_names = list(SMALL_SHARDED)
    placed = []
    for n in sm_names:
        ax = SMALL_SHARDED[n]
        full_shape = w_loc[n].shape[:ax] + (N_CHIPS * w_loc[n].shape[ax],) + w_loc[n].shape[ax + 1:]
        src = w_loc[n] * (core == 0).astype(F32)
        placed.append(lax.dynamic_update_slice_in_dim(jnp.zeros(full_shape, F32), src, chip * w_loc[n].shape[ax], ax))
    summed = allreduce_small(_pack(placed), "allgather_small")
    w_full = dict(w_loc)
    w_full.update(zip(sm_names, _unpack(summed, [p.shape for p in placed])))

    loss, dx, grads = local_step(x[0], loss_target[0], w_full, wb, shards)
    loss = lax.psum(loss, ("x", "y", "c"))

    gsum = allreduce_small(_pack([grads[n] for n in SMALL]), "allreduce_small")
    g_loc = {}
    for n, g in zip(SMALL, _unpack(gsum, [grads[n].shape for n in SMALL])):
        g_loc[n] = _local_shard(g, SMALL_SHARDED[n], chip) if n in SMALL_SHARDED else g

    parts = []
    for n, l in big:
        parts += list(grads[n][l]) if isinstance(grads[n][l], tuple) else [grads[n][l]]
    swapped = iter(zip(parts, sibling_swap(parts, True, "grad_swap_cores")))
    core_arg = core.reshape(1).astype(jnp.int32)
    sums = []
    for n, l in big:
        if isinstance(grads[n][l], tuple):
            (dwg, from_g), (dwu, from_u) = next(swapped), next(swapped)
            sums.append(chip_sum_cols(dwg, dwu, from_g, from_u, core_arg, f"grad_chip_sum_{n}{l}"))
        else:
            p, r = next(swapped)
            sums.append(chip_sum(p, r, core_arg, f"grad_chip_sum_{n}{l}"))
    from_chips = scatter_chips([s[1] for s in sums], "grad_scatter")
    halves = [shard_sum(lax.dynamic_index_in_dim(s[0], chip, 0, keepdims=False), r, f"grad_shard_sum_{n}{l}")
              for (n, l), s, r in zip(big, sums, from_chips)]
    others = sibling_swap(halves, False, "grad_share_cores")
    rows = {n: [None] * N_LAYERS for n in BIG}
    for (n, l), half, other in zip(big, halves, others):
        lower = jnp.where(core == 0, half, other)
        upper = jnp.where(core == 0, other, half)
        rows[n][l] = jnp.concatenate([lower, upper], axis=0)[:, :w_loc[n].shape[-1]]
    big_names = list(BIG)
    for n in big_names:
        g_loc[n] = jnp.stack(rows[n])

    delta, new_m, new_v = {}, {}, {}
    for n in big_names:
        delta[n], new_m[n], new_v[n] = adamw(w_loc[n], m_loc[n], v_loc[n], g_loc[n], f"adamw_{n}")
    pack = lambda d: _pack([d[n] for n in SMALL])[None]
    res = adamw(pack(w_loc), pack(m_loc), pack(v_loc), pack(g_loc), "adamw_small")
    for dst, r in zip((delta, new_m, new_v), res):
        dst.update(zip(SMALL, _unpack(r[0], [w_loc[n].shape for n in SMALL])))

    return (loss, dx[None], *[g_loc[n] for n in WEIGHTS], *[delta[n] for n in WEIGHTS],
            *[new_m[n] for n in WEIGHTS], *[new_v[n] for n in WEIGHTS])
```

```python
import functools

import numpy as np
import jax
import jax.numpy as jnp
from jax import lax
from jax.experimental import pallas as pl
from jax.experimental.pallas import tpu as pltpu

F32 = jnp.float32
BF16 = jnp.bfloat16
HIGHEST = lax.Precision.HIGHEST
MESH = pl.DeviceIdType.MESH

D_MODEL = 1024
D_FF = 2816
N_LAYERS = 2
HEADS = 6
HEAD_DIM = 64
MIX_W = HEADS * HEAD_DIM
LRU_W = 256
LRU_BLOCKS = 4
RWKV_IN = 1408
D_IN = 3468
D_IN_PAD = 3584
NORM_EPS = 1e-6
GN_EPS = 64e-5
LRU_C = 8.0
CHUNK = 64
ROWS = 256
FF_CHUNK = 256
IN_CHUNK = 512
PACK_W = 1024
VMEM_LIMIT = 56 * 1024 * 1024

ADAM_LR, ADAM_B1, ADAM_B2, ADAM_EPS, ADAM_WD, ADAM_STEP = 0.001, 0.9, 0.999, 1e-08, 0.01, 10

WEIGHTS = ['ffn1_norm', 'ffn1_wi', 'ffn1_wo', 'mix_norm', 'w_in', 'w_out', 'lru_conv_w', 'lru_conv_b',
           'lru_gate_a_w', 'lru_gate_a_b', 'lru_gate_x_w', 'lru_gate_x_b', 'lru_lambda', 'lru_out_norm',
           'rwkv_mu', 'rwkv_w_up', 'rwkv_w_bias', 'rwkv_a_up', 'rwkv_a_bias', 'rwkv_g_up', 'rwkv_k_k',
           'rwkv_k_a', 'rwkv_r_k', 'rwkv_ln_g', 'rwkv_ln_b', 'rwkv_vres_w1', 'rwkv_vres_w2', 'rwkv_vres_b',
           'gdn_conv_w', 'gdn_a_log', 'gdn_dt_bias', 'gdn_norm', 'ffn2_norm', 'ffn2_wi', 'ffn2_wo', 'final_norm']
BIG = {'ffn1_wi': 2, 'ffn1_wo': 1, 'w_in': 2, 'w_out': 1, 'ffn2_wi': 2, 'ffn2_wo': 1}
SMALL_SHARDED = {'lru_conv_w': 2, 'rwkv_w_up': 2, 'rwkv_a_up': 2, 'rwkv_g_up': 2, 'rwkv_vres_w1': 1,
                 'rwkv_vres_w2': 2, 'gdn_conv_w': 2}
N_CHIPS = 4


def _params(sem=None):
    kw = dict(vmem_limit_bytes=VMEM_LIMIT)
    if sem is not None:
        kw['dimension_semantics'] = sem
    return pltpu.CompilerParams(**kw)


def _bdot(a, b, dims=(((1,), (0,)), ((), ()))):
    return lax.dot_general(a.astype(BF16), b.astype(BF16), dims, preferred_element_type=F32)


def _bdot_nt(a, b):
    return _bdot(a, b, (((1,), (1,)), ((), ())))


def _bdot_tn(a, b):
    return _bdot(a, b, (((0,), (0,)), ((), ())))


_DIMS = {'nn': (((1,), (0,)), ((), ())), 'nt': (((1,), (1,)), ((), ())), 'tn': (((0,), (0,)), ((), ()))}


def _split(a, terms):
    parts = []
    for _ in range(terms - 1):
        hi = a.astype(BF16)
        parts.append(hi)
        a = a - hi.astype(F32)
    parts.append(a.astype(BF16))
    return parts


_BATCH_DIMS = {'nn': (((2,), (1,)), ((0,), (0,))), 'nt': (((2,), (2,)), ((0,), (0,))),
               'tn': (((1,), (1,)), ((0,), (0,)))}


def _dot3(a, b, kind):
    ah, al = _split(a, 2)
    bh, bl = _split(b, 2)
    dims = _BATCH_DIMS[kind] if a.ndim == 3 else _DIMS[kind]
    d = lambda p, q: lax.dot_general(p, q, dims, preferred_element_type=F32)
    return d(ah, bh) + (d(ah, bl) + d(al, bh))


@functools.partial(jax.custom_vjp, nondiff_argnums=(2,))
def _cdot_k(a, b, kind):
    return _dot3(a, b, kind)


def _cdot_k_fwd(a, b, kind):
    return _dot3(a, b, kind), (a, b)


def _cdot_k_bwd(kind, res, ct):
    a, b = res
    if kind == 'nn':
        return _dot3(ct, b, 'nt'), _dot3(a, ct, 'tn')
    if kind == 'nt':
        return _dot3(ct, b, 'nn'), _dot3(ct, a, 'tn')
    return _dot3(b, ct, 'nt'), _dot3(a, ct, 'nn')


_cdot_k.defvjp(_cdot_k_fwd, _cdot_k_bwd)


def _dot1(a, b, kind):
    dims = _BATCH_DIMS[kind] if a.ndim == 3 else _DIMS[kind]
    return lax.dot_general(a.astype(BF16), b.astype(BF16), dims, preferred_element_type=F32)


@functools.partial(jax.custom_vjp, nondiff_argnums=(2,))
def _cdot1_k(a, b, kind):
    return _dot1(a, b, kind)


def _cdot1_k_fwd(a, b, kind):
    return _dot1(a, b, kind), (a, b)


def _cdot1_k_bwd(kind, res, ct):
    a, b = res
    if kind == 'nn':
        return _dot1(ct, b, 'nt'), _dot1(a, ct, 'tn')
    if kind == 'nt':
        return _dot1(ct, b, 'nn'), _dot1(ct, a, 'tn')
    return _dot1(b, ct, 'nt'), _dot1(a, ct, 'nn')


_cdot1_k.defvjp(_cdot1_k_fwd, _cdot1_k_bwd)


def _cdot(a, b):
    return _cdot1_k(a, b, 'nn')


def _cdot_nt(a, b):
    return _cdot1_k(a, b, 'nt')


def _cdot_tn(a, b):
    return _cdot1_k(a, b, 'tn')


def _hdot(a, b):
    return _cdot_k(a, b, 'nn')


def _dot_exact(x, m01, kind):
    d = lambda p: lax.dot_general(p, m01.astype(BF16), _DIMS[kind], preferred_element_type=F32)
    hi, mid, lo = _split(x, 3)
    return d(hi) + (d(mid) + d(lo))


@functools.partial(jax.custom_vjp, nondiff_argnums=(1,))
def _xdot(x, make_m):
    return _dot_exact(x, make_m(), 'nn')


def _xdot_fwd(x, make_m):
    return _dot_exact(x, make_m(), 'nn'), None


def _xdot_bwd(make_m, _, ct):
    return (_dot_exact(ct, make_m(), 'nt'),)


_xdot.defvjp(_xdot_fwd, _xdot_bwd)


def _iota2(n, m):
    return lax.broadcasted_iota(jnp.int32, (n, m), 0), lax.broadcasted_iota(jnp.int32, (n, m), 1)


def _head_blocks(w):
    ri, ci = _iota2(w, w)
    return (ri // HEAD_DIM == ci // HEAD_DIM).astype(F32)


def _segsum(x):
    return _xdot(x, functools.partial(_head_blocks, x.shape[-1]))


def _cumsum_rows(x):
    return _cumsum_k(x, x.shape[0])


@functools.partial(jax.custom_vjp, nondiff_argnums=(1,))
def _cumsum_k(x, n):
    return _lower_dot(x, n, False)


def _lower_dot(x, n, transpose):
    ri, ci = _iota2(n, n)
    m = ((ri <= ci) if transpose else (ri >= ci)).astype(BF16)
    d = lambda p: lax.dot_general(m, p, _DIMS['nn'], preferred_element_type=F32)
    hi, mid, lo = _split(x, 3)
    return d(hi) + (d(mid) + d(lo))


def _cumsum_k_fwd(x, n):
    return _lower_dot(x, n, False), None


def _cumsum_k_bwd(n, _, ct):
    return (_lower_dot(ct, n, True),)


_cumsum_k.defvjp(_cumsum_k_fwd, _cumsum_k_bwd)


def _rms(x, g):
    return x * lax.rsqrt(jnp.mean(x * x, axis=-1, keepdims=True) + NORM_EPS) * g


DENSE_ROWS = 1024


def _row_loop(n_rows, fn):
    rows = min(DENSE_ROWS, n_rows)

    def step(i, c):
        fn(pl.ds(pl.multiple_of(i * rows, rows), rows))
        return c
    lax.fori_loop(0, n_rows // rows, step, 0)


def ffn_fwd(x, g, wi, wo, name, hosted=()):
    T = x.shape[0]
    nj = D_FF // FF_CHUNK
    gather = ChipGather(list(hosted))
    n = gather.n

    def body(*refs):
        x_ref, g_ref, wg_ref, wu_ref, wo_ref = refs[:5]
        hx, o_ref, ho = refs[5:5 + n], refs[5 + n], refs[6 + n:6 + 2 * n]
        h_ref, acc_ref = refs[6 + 2 * n:8 + 2 * n]
        sems = refs[8 + 2 * n:]
        j = pl.program_id(0)

        @pl.when(j == 0)
        def _():
            gather.start(hx, ho, sems)

            def init(r):
                h_ref[r, :] = _rms(x_ref[r, :], g_ref[...]).astype(BF16)
                acc_ref[r, :] = jnp.zeros((r.size, D_MODEL), F32)
            _row_loop(T, init)

        def blk(r):
            hb = h_ref[r, :]
            gate = jnp.dot(hb, wg_ref[...], preferred_element_type=F32)
            up = jnp.dot(hb, wu_ref[...], preferred_element_type=F32)
            a = (gate * jax.nn.sigmoid(gate) * up).astype(BF16)
            acc_ref[r, :] += jnp.dot(a, wo_ref[...], preferred_element_type=F32)
        _row_loop(T, blk)

        @pl.when(j == nj // 2)
        def _():
            gather.relay(hx, ho, sems)

        @pl.when(j == nj - 1)
        def _():
            def fin(r):
                o_ref[r, :] = x_ref[r, :] + 0.5 * acc_ref[r, :]
            _row_loop(T, fin)
            gather.finish(hx, ho, sems)

    full = pl.BlockSpec((T, D_MODEL), lambda j: (0, 0))
    res = pl.pallas_call(
        body, name=name, grid=(nj,),
        in_specs=[full, pl.BlockSpec((1, D_MODEL), lambda j: (0, 0)),
                  pl.BlockSpec((D_MODEL, FF_CHUNK), lambda j: (0, j)),
                  pl.BlockSpec((D_MODEL, FF_CHUNK), lambda j: (0, j + nj)),
                  pl.BlockSpec((FF_CHUNK, D_MODEL), lambda j: (j, 0))] + gather.in_specs,
        out_specs=tuple([full] + gather.out_specs),
        out_shape=tuple([jax.ShapeDtypeStruct((T, D_MODEL), F32)] + gather.out_shape),
        scratch_shapes=[pltpu.VMEM((T, D_MODEL), BF16), pltpu.VMEM((T, D_MODEL), F32)] + gather.scratch,
        compiler_params=_params(("arbitrary",)))(x, g, wi, wi, wo, *hosted)
    return res[0], list(res[1:])


def _norm_bwd_rows(x, g, dh, dres):
    rstd = lax.rsqrt(jnp.mean(x * x, axis=-1, keepdims=True) + NORM_EPS)
    xh = x * rstd
    dxh = dh * g
    dx = rstd * (dxh - xh * jnp.mean(dxh * xh, axis=-1, keepdims=True))
    return dres + dx, jnp.sum(dh * xh, axis=0, keepdims=True)


def ffn_bwd(x, dy, g, wi, wo, name):
    T = x.shape[0]
    nj = D_FF // FF_CHUNK

    def body(x_ref, dy_ref, g_ref, wg_ref, wu_ref, wo_ref, dx_ref, dg_ref, dwg_ref, dwu_ref, dwo_ref,
             h_ref, da_ref, dh_ref):
        j = pl.program_id(0)

        @pl.when(j == 0)
        def _():
            def init(r):
                h_ref[r, :] = _rms(x_ref[r, :], g_ref[...]).astype(BF16)
                da_ref[r, :] = (0.5 * dy_ref[r, :]).astype(BF16)
                dh_ref[r, :] = jnp.zeros((r.size, D_MODEL), F32)
            _row_loop(T, init)

        dwg_ref[...] = jnp.zeros_like(dwg_ref)
        dwu_ref[...] = jnp.zeros_like(dwu_ref)
        dwo_ref[...] = jnp.zeros_like(dwo_ref)

        def blk(r):
            hb = h_ref[r, :]
            db = da_ref[r, :]
            gate = jnp.dot(hb, wg_ref[...], preferred_element_type=F32)
            up = jnp.dot(hb, wu_ref[...], preferred_element_type=F32)
            sg = jax.nn.sigmoid(gate)
            sl = gate * sg
            da = _bdot_nt(db, wo_ref[...])
            dup = (da * sl).astype(BF16)
            dgate = (da * up * (sg * (1.0 + gate * (1.0 - sg)))).astype(BF16)
            dwo_ref[...] += _bdot_tn((sl * up).astype(BF16), db)
            dwg_ref[...] += _bdot_tn(hb, dgate)
            dwu_ref[...] += _bdot_tn(hb, dup)
            dh_ref[r, :] += _bdot_nt(dgate, wg_ref[...]) + _bdot_nt(dup, wu_ref[...])
        _row_loop(T, blk)

        @pl.when(j == nj - 1)
        def _():
            dg_ref[...] = jnp.zeros_like(dg_ref)

            def fin(r):
                dx, dg = _norm_bwd_rows(x_ref[r, :], g_ref[...], dh_ref[r, :], dy_ref[r, :])
                dx_ref[r, :] = dx
                dg_ref[...] += dg
            _row_loop(T, fin)

    full = pl.BlockSpec((T, D_MODEL), lambda j: (0, 0))
    vec = pl.BlockSpec((1, D_MODEL), lambda j: (0, 0))
    return pl.pallas_call(
        body, name=name, grid=(nj,),
        in_specs=[full, full, vec,
                  pl.BlockSpec((D_MODEL, FF_CHUNK), lambda j: (0, j)),
                  pl.BlockSpec((D_MODEL, FF_CHUNK), lambda j: (0, j + nj)),
                  pl.BlockSpec((FF_CHUNK, D_MODEL), lambda j: (j, 0))],
        out_specs=(full, vec,
                   pl.BlockSpec((D_MODEL, FF_CHUNK), lambda j: (0, j)),
                   pl.BlockSpec((D_MODEL, FF_CHUNK), lambda j: (0, j)),
                   pl.BlockSpec((FF_CHUNK, D_MODEL), lambda j: (j, 0))),
        out_shape=(jax.ShapeDtypeStruct((T, D_MODEL), F32), jax.ShapeDtypeStruct((1, D_MODEL), F32),
                   jax.ShapeDtypeStruct((D_MODEL, D_FF), F32), jax.ShapeDtypeStruct((D_MODEL, D_FF), F32),
                   jax.ShapeDtypeStruct((D_FF, D_MODEL), F32)),
        scratch_shapes=[pltpu.VMEM((T, D_MODEL), BF16), pltpu.VMEM((T, D_MODEL), BF16),
                        pltpu.VMEM((T, D_MODEL), F32)],
        compiler_params=_params(("arbitrary",)))(x, dy, g, wi, wi, wo)


def proj_fwd(x, g, w, name):
    T = x.shape[0]
    nj = D_IN_PAD // IN_CHUNK

    def body(x_ref, g_ref, w_ref, o_ref, h_ref):
        @pl.when(pl.program_id(0) == 0)
        def _():
            def init(r):
                h_ref[r, :] = _rms(x_ref[r, :], g_ref[...]).astype(BF16)
            _row_loop(T, init)

        def blk(r):
            o_ref[r, :] = jnp.dot(h_ref[r, :], w_ref[...], preferred_element_type=F32)
        _row_loop(T, blk)

    return pl.pallas_call(
        body, name=name, grid=(nj,),
        in_specs=[pl.BlockSpec((T, D_MODEL), lambda j: (0, 0)), pl.BlockSpec((1, D_MODEL), lambda j: (0, 0)),
                  pl.BlockSpec((D_MODEL, IN_CHUNK), lambda j: (0, j))],
        out_specs=pl.BlockSpec((T, IN_CHUNK), lambda j: (0, j)),
        out_shape=jax.ShapeDtypeStruct((T, D_IN_PAD), F32),
        scratch_shapes=[pltpu.VMEM((T, D_MODEL), BF16)],
        compiler_params=_params(("arbitrary",)))(x, g, w)


def proj_bwd(x, dres, g, w, dp, name):
    T = x.shape[0]
    nj = D_IN_PAD // IN_CHUNK

    def body(x_ref, dres_ref, g_ref, w_ref, dp_ref, dx_ref, dg_ref, dw_ref, h_ref, dh_ref):
        j = pl.program_id(0)

        @pl.when(j == 0)
        def _():
            def init(r):
                h_ref[r, :] = _rms(x_ref[r, :], g_ref[...]).astype(BF16)
                dh_ref[r, :] = jnp.zeros((r.size, D_MODEL), F32)
            _row_loop(T, init)

        dw_ref[...] = jnp.zeros_like(dw_ref)

        def blk(r):
            dpb = dp_ref[r, :].astype(BF16)
            dw_ref[...] += _bdot_tn(h_ref[r, :], dpb)
            dh_ref[r, :] += _bdot_nt(dpb, w_ref[...])
        _row_loop(T, blk)

        @pl.when(j == nj - 1)
        def _():
            dg_ref[...] = jnp.zeros_like(dg_ref)

            def fin(r):
                dx, dg = _norm_bwd_rows(x_ref[r, :], g_ref[...], dh_ref[r, :], dres_ref[r, :])
                dx_ref[r, :] = dx
                dg_ref[...] += dg
            _row_loop(T, fin)

    full = pl.BlockSpec((T, D_MODEL), lambda j: (0, 0))
    vec = pl.BlockSpec((1, D_MODEL), lambda j: (0, 0))
    return pl.pallas_call(
        body, name=name, grid=(nj,),
        in_specs=[full, full, vec, pl.BlockSpec((D_MODEL, IN_CHUNK), lambda j: (0, j)),
                  pl.BlockSpec((T, IN_CHUNK), lambda j: (0, j))],
        out_specs=(full, vec, pl.BlockSpec((D_MODEL, IN_CHUNK), lambda j: (0, j))),
        out_shape=(jax.ShapeDtypeStruct((T, D_MODEL), F32), jax.ShapeDtypeStruct((1, D_MODEL), F32),
                   jax.ShapeDtypeStruct((D_MODEL, D_IN_PAD), F32)),
        scratch_shapes=[pltpu.VMEM((T, D_MODEL), BF16), pltpu.VMEM((T, D_MODEL), F32)],
        compiler_params=_params(("arbitrary",)))(x, dres, g, w, dp)


def out_fwd(mixed, w, x, name):
    T = x.shape[0]

    def body(m_ref, w_ref, x_ref, o_ref):
        o_ref[...] = x_ref[...] + jnp.dot(m_ref[...].astype(BF16), w_ref[...], preferred_element_type=F32)

    blk = pl.BlockSpec((ROWS, D_MODEL), lambda i: (i, 0))
    return pl.pallas_call(
        body, name=name, grid=(T // ROWS,),
        in_specs=[blk, pl.BlockSpec((D_MODEL, D_MODEL), lambda i: (0, 0)), blk],
        out_specs=blk, out_shape=jax.ShapeDtypeStruct((T, D_MODEL), F32),
        compiler_params=_params(("arbitrary",)))(mixed, w, x)


def out_bwd(mixed, w, dy, name):
    T = dy.shape[0]

    def body(m_ref, w_ref, dy_ref, dm_ref, dw_ref):
        @pl.when(pl.program_id(0) == 0)
        def _():
            dw_ref[...] = jnp.zeros_like(dw_ref)
        dyb = dy_ref[...].astype(BF16)
        dm_ref[...] = _bdot_nt(dyb, w_ref[...])
        dw_ref[...] += _bdot_tn(m_ref[...].astype(BF16), dyb)

    blk = pl.BlockSpec((ROWS, D_MODEL), lambda i: (i, 0))
    sq = pl.BlockSpec((D_MODEL, D_MODEL), lambda i: (0, 0))
    return pl.pallas_call(
        body, name=name, grid=(T // ROWS,),
        in_specs=[blk, sq, blk], out_specs=(blk, sq),
        out_shape=(jax.ShapeDtypeStruct((T, D_MODEL), F32), jax.ShapeDtypeStruct((D_MODEL, D_MODEL), F32)),
        compiler_params=_params(("arbitrary",)))(mixed, w, dy)


def loss_head(x, g, target, name):
    T = x.shape[0]

    def body(x_ref, g_ref, t_ref, loss_ref, dx_ref, dg_ref):
        @pl.when(pl.program_id(0) == 0)
        def _():
            loss_ref[...] = jnp.zeros_like(loss_ref)
            dg_ref[...] = jnp.zeros_like(dg_ref)
        xb = x_ref[...]
        rstd = lax.rsqrt(jnp.mean(xb * xb, axis=-1, keepdims=True) + NORM_EPS)
        xh = xb * rstd
        err = xh * g_ref[...] - t_ref[...]
        loss_ref[...] += 0.5 * jnp.sum(jnp.mean(err * err, axis=-1, keepdims=True), axis=0, keepdims=True)
        dy = err * (1.0 / D_MODEL)
        dg_ref[...] += jnp.sum(dy * xh, axis=0, keepdims=True)
        dxh = dy * g_ref[...]
        dx_ref[...] = rstd * (dxh - xh * jnp.mean(dxh * xh, axis=-1, keepdims=True))

    blk = pl.BlockSpec((ROWS, D_MODEL), lambda i: (i, 0))
    vec = pl.BlockSpec((1, D_MODEL), lambda i: (0, 0))
    return pl.pallas_call(
        body, name=name, grid=(T // ROWS,),
        in_specs=[blk, vec, blk], out_specs=(pl.BlockSpec((1, 1), lambda i: (0, 0)), blk, vec),
        out_shape=(jax.ShapeDtypeStruct((1, 1), F32), jax.ShapeDtypeStruct((T, D_MODEL), F32),
                   jax.ShapeDtypeStruct((1, D_MODEL), F32)),
        compiler_params=_params(("arbitrary",)))(x, g, target)


def rowwise_fwd(fn, rows, shared, out_widths, name):
    T = rows[0].shape[0]
    n_in = len(rows) + len(shared)

    def body(*refs):
        res = fn(*[r[...] for r in refs[:n_in]])
        for o, v in zip(refs[n_in:], res):
            o[...] = v

    in_specs = ([pl.BlockSpec((ROWS, a.shape[1]), lambda i: (i, 0)) for a in rows]
                + [pl.BlockSpec(a.shape, lambda i: (0, 0)) for a in shared])
    return pl.pallas_call(
        body, name=name, grid=(T // ROWS,), in_specs=in_specs,
        out_specs=tuple(pl.BlockSpec((ROWS, w), lambda i: (i, 0)) for w in out_widths),
        out_shape=tuple(jax.ShapeDtypeStruct((T, w), F32) for w in out_widths),
        compiler_params=_params(("arbitrary",)))(*rows, *shared)


def rowwise_bwd(fn, rows, shared, cts, name, ct_fn=None):
    T = rows[0].shape[0]
    nr, ns, nc = len(rows), len(shared), len(cts)

    def body(*refs):
        ins = [r[...] for r in refs[:nr + ns]]
        ctv = tuple(r[...] for r in refs[nr + ns:nr + ns + nc])
        outs = refs[nr + ns + nc:]
        _, vjp = jax.vjp(fn, *ins)
        grads = vjp(ct_fn(*ctv) if ct_fn is not None else ctv)
        for k in range(nr):
            outs[k][...] = grads[k]

        @pl.when(pl.program_id(0) == 0)
        def _():
            for k in range(ns):
                outs[nr + k][...] = jnp.zeros_like(outs[nr + k])
        for k in range(ns):
            outs[nr + k][...] += grads[nr + k]

    row_spec = lambda a: pl.BlockSpec((ROWS, a.shape[1]), lambda i: (i, 0))
    sh_spec = lambda a: pl.BlockSpec(a.shape, lambda i: (0, 0))
    return pl.pallas_call(
        body, name=name, grid=(T // ROWS,),
        in_specs=[row_spec(a) for a in rows] + [sh_spec(a) for a in shared] + [row_spec(a) for a in cts],
        out_specs=tuple([row_spec(a) for a in rows] + [sh_spec(a) for a in shared]),
        out_shape=tuple(jax.ShapeDtypeStruct(a.shape, F32) for a in list(rows) + list(shared)),
        compiler_params=_params(("arbitrary",)))(*rows, *shared, *cts)


def shift_rows(x, s):
    return jnp.pad(x, ((s, 0), (0, 0)))[:x.shape[0]]


def unshift_rows(x, s):
    return jnp.pad(x, ((0, s), (0, 0)))[s:]


def _neg_expm1(y):
    series = -(y * (1.0 + y * (0.5 + y * (1.0 / 6.0 + y * (1.0 / 24.0)))))
    return jnp.where(y > -0.05, series, 1.0 - jnp.exp(y))


def lru_pre_fn(x0, x1, x2, x3, first, w0, w1, w2, w3, cb, ga, gab, gx, gxb, lam):
    xc = w3 * x0 + w2 * x1 + w1 * x2 + w0 * x3 + cb
    r = jax.nn.sigmoid(_hdot(xc, ga) + gab)
    i = jax.nn.sigmoid(_hdot(xc, gx) + gxb)
    log_a = -LRU_C * r * jax.nn.softplus(-lam)
    a = jnp.exp(log_a)
    mult = jnp.where(first > 0.5, 1.0, jnp.sqrt(_neg_expm1(2.0 * log_a)))
    return a, mult * i * xc


def lru_post_fn(h, py, og):
    return (_rms(h * jax.nn.gelu(py), og),)


def lru_scan(a, b, reverse, name):
    T, C = a.shape
    nb = T // 8

    def body(a_ref, b_ref, h_ref):
        rows = lax.broadcasted_iota(jnp.int32, (8, C), 0)

        def blk(i, carry):
            j = nb - 1 - i if reverse else i
            r = pl.ds(pl.multiple_of(j * 8, 8), 8)
            A = a_ref[r, :]
            B = b_ref[r, :]
            for s in (1, 2, 4):
                if reverse:
                    keep = rows < 8 - s
                    sh = 8 - s
                else:
                    keep = rows >= s
                    sh = s
                Bs = jnp.where(keep, pltpu.roll(B, sh, 0), 0.0)
                As = jnp.where(keep, pltpu.roll(A, sh, 0), 1.0)
                B = B + A * Bs
                A = A * As
            hb = B + A * carry
            h_ref[r, :] = hb
            edge = 0 if reverse else 7
            return jnp.sum(jnp.where(rows == edge, hb, 0.0), axis=0, keepdims=True)

        lax.fori_loop(0, nb, blk, jnp.zeros((1, C), F32))

    full = pl.BlockSpec((T, C), lambda: (0, 0))
    return pl.pallas_call(body, name=name, in_specs=[full, full], out_specs=full,
                          out_shape=jax.ShapeDtypeStruct((T, C), F32), compiler_params=_params())(a, b)


def make_rwkv_pre_fn(has_vres):
    def fn(p, pp, *rest):
        if has_vres:
            vf, mu, w_up, w_b, a_up, a_b, g_up, kk_w, ka_w, vw1, vw2, vb = rest
        else:
            mu, w_up, w_b, a_up, a_b, g_up, kk_w, ka_w = rest
        xm = p + (pp - p) * mu
        r, k, v = xm[:, 0:384], xm[:, 384:768], xm[:, 768:1152]
        xw, xa, xg = xm[:, 1152:1216], xm[:, 1216:1280], xm[:, 1280:1408]
        w_log = -jax.nn.softplus(-(w_b + _hdot(jnp.tanh(xw), w_up))) - 0.5
        lw = -jnp.exp(w_log)
        a = jax.nn.sigmoid(a_b + _hdot(xa, a_up))
        g = _hdot(jax.nn.sigmoid(xg), g_up)
        if has_vres:
            v = v + (vf - v) * jax.nn.sigmoid(vb + _hdot(_hdot(v, vw1), vw2))
        kkx = k * kk_w
        kk = kkx * lax.rsqrt(_segsum(kkx * kkx) + 1e-6)
        k2 = k * (1.0 + (a - 1.0) * ka_w)
        return r, lw, k2, v, kk, a, g
    return fn


def rwkv_post_fn(y, r, k2, v, g, ln_g, ln_b, r_k):
    mean = _segsum(y) * (1.0 / HEAD_DIM)
    yc = y - mean
    var = _segsum(yc * yc) * (1.0 / HEAD_DIM)
    yn = yc * lax.rsqrt(var + GN_EPS) * ln_g + ln_b
    bonus = _segsum(r * k2 * r_k) * v
    return ((yn + bonus) * g,)


def _head_expander(first_lane):
    ri, ci = _iota2(128, MIX_W)
    return (ri == ci // HEAD_DIM + first_lane).astype(F32)


def gdn_pre_fn(x0, x1, x2, x3, ab, w0, w1, w2, w3, alog, dtb):
    qkv = jax.nn.silu(w3 * x0 + w2 * x1 + w1 * x2 + w0 * x3)
    q, k, v = qkv[:, 0:384], qkv[:, 384:768], qkv[:, 768:1152]
    q = q * lax.rsqrt(_segsum(q * q) + 1e-6) * (HEAD_DIM ** -0.5)
    k = k * lax.rsqrt(_segsum(k * k) + 1e-6)
    g = -jnp.exp(alog) * jax.nn.softplus(ab + dtb)
    beta = jax.nn.sigmoid(ab)
    ge = _xdot(g, functools.partial(_head_expander, 0))
    be = _xdot(beta, functools.partial(_head_expander, HEADS))
    return q, k, v, ge, be


def gdn_post_fn(o, z, ng):
    ms = _segsum(o * o) * (1.0 / HEAD_DIM)
    return (o * lax.rsqrt(ms + NORM_EPS) * ng * jax.nn.silu(z),)


def _neumann_inv(m):
    n = m.shape[-1]
    ri, ci = _iota2(n, n)
    eye = (ri == ci).astype(F32)
    md = jnp.where(ri // 16 == ci // 16, m, 0.0)
    mo = m - md
    t0 = eye + md
    p2 = _hdot(md, md)
    t0 = t0 + _hdot(t0, p2)
    p4 = _hdot(p2, p2)
    t0 = t0 + _hdot(t0, p4)
    p8 = _hdot(p4, p4)
    t0 = t0 + _hdot(t0, p8)
    nn = _hdot(t0, mo)
    n2 = _hdot(nn, nn)
    t1 = eye + nn + n2 + _hdot(nn, n2)
    return _hdot(t1, t0)


@jax.custom_vjp
def _inv_saved(m, t_saved):
    return t_saved


def _inv_saved_fwd(m, t_saved):
    return t_saved, t_saved


def _inv_saved_bwd(t_saved, dt):
    tt = jnp.swapaxes(t_saved, -1, -2)
    return _hdot(_hdot(tt, dt), tt), jnp.zeros_like(t_saved)


_inv_saved.defvjp(_inv_saved_fwd, _inv_saved_bwd)


def _heads(x):
    return jnp.concatenate([x[None, :, h * HEAD_DIM:(h + 1) * HEAD_DIM] for h in range(HEADS)], axis=0)


def _unheads(y):
    return jnp.concatenate([lax.index_in_dim(y, h, 0, keepdims=False) for h in range(HEADS)], axis=1)


def rwkv_heads(s0, r, lw, k2, v, kk, a, inv):
    n = r.shape[0]
    ri, ci = _iota2(n, n)
    low, strict = ri >= ci, ri > ci
    cs = _cumsum_rows(lw)
    cl = jnp.sum(lw, axis=0, keepdims=True)
    p_in, p_prev, p_inv = jnp.exp(cs), jnp.exp(cs - lw), jnp.exp(-cs)
    p_rest, p_all = jnp.exp(cl - cs), jnp.exp(cl)
    bd = kk * a
    at, rt = _heads(-kk * p_prev), _heads(r * p_in)
    bh, kh = _heads(bd * p_inv), _heads(k2 * p_inv)
    vh = _heads(v)
    m_ab = jnp.where(strict, _cdot_nt(at, bh), 0.0)
    m_ak = jnp.where(strict, _cdot_nt(at, kh), 0.0)
    m_rb = jnp.where(low, _cdot_nt(rt, bh), 0.0)
    m_rk = jnp.where(low, _cdot_nt(rt, kh), 0.0)
    sa = _cdot(inv(m_ab), _cdot_nt(at, s0) + _cdot(m_ak, vh))
    y = _cdot_nt(rt, s0) + _cdot(m_rb, sa) + _cdot(m_rk, vh)
    s1 = s0 * _heads(p_all) + _cdot_tn(sa, _heads(bd * p_rest)) + _cdot_tn(vh, _heads(k2 * p_rest))
    return _unheads(y), s1


def gdn_heads(s0, q, k, v, ge, be, inv):
    n = q.shape[0]
    ri, ci = _iota2(n, n)
    low, strict = ri >= ci, ri > ci
    gc = _cumsum_rows(ge)
    gl = jnp.sum(ge, axis=0, keepdims=True)
    gch = _heads(gc)
    decay = jnp.where(low, jnp.exp(jnp.where(low, gch - jnp.swapaxes(gch, 1, 2), 0.0)), 0.0)
    kb = k * be
    e = jnp.exp(gc)
    kh = _heads(k)
    m = -jnp.where(strict, _cdot_nt(_heads(kb), kh) * decay, 0.0)
    mr = jnp.where(low, _cdot_nt(_heads(q), kh) * decay, 0.0)
    u = _cdot(inv(m), _heads(v * be) - _cdot_nt(_heads(kb * e), s0))
    y = _cdot_nt(_heads(q * e), s0) + _cdot(mr, u)
    s1 = s0 * _heads(jnp.exp(gl)) + _cdot_tn(u, _heads(k * jnp.exp(gl - gc)))
    return _unheads(y), s1


def core_fwd(heads_fn, ins, name, hosted=()):
    T = ins[0].shape[0]
    nc = T // CHUNK
    n = len(ins)
    gather = ChipGather(list(hosted))
    ng = gather.n

    def body(*refs):
        hx = refs[n:n + ng]
        y_ref, s0_ref, t_ref = refs[n + ng:n + ng + 3]
        ho = refs[n + ng + 3:n + 2 * ng + 3]
        s_ref = refs[n + 2 * ng + 3]
        sems = refs[n + 2 * ng + 4:]
        c = pl.program_id(0)

        @pl.when(c == 0)
        def _():
            gather.start(hx, ho, sems)
            s_ref[...] = jnp.zeros_like(s_ref)

        s0 = s_ref[...]
        kept = []

        def inv(m):
            kept.append(_neumann_inv(m))
            return kept[0]

        y, s1 = heads_fn(s0, *[r[...] for r in refs[:n]], inv)
        y_ref[...] = y
        s0_ref[0] = s0
        t_ref[0] = kept[0]
        s_ref[...] = s1

        @pl.when(c == nc // 2)
        def _():
            gather.relay(hx, ho, sems)

        @pl.when(c == nc - 1)
        def _():
            gather.finish(hx, ho, sems)

    row = pl.BlockSpec((CHUNK, MIX_W), lambda c: (c, 0))
    st_shape = (HEADS, HEAD_DIM, HEAD_DIM)
    st = pl.BlockSpec((1,) + st_shape, lambda c: (c, 0, 0, 0))
    res = pl.pallas_call(
        body, name=name, grid=(nc,), in_specs=[row] * n + gather.in_specs,
        out_specs=tuple([row, st, st] + gather.out_specs),
        out_shape=tuple([jax.ShapeDtypeStruct((T, MIX_W), F32), jax.ShapeDtypeStruct((nc,) + st_shape, F32),
                         jax.ShapeDtypeStruct((nc,) + st_shape, F32)] + gather.out_shape),
        scratch_shapes=[pltpu.VMEM(st_shape, F32)] + gather.scratch,
        compiler_params=_params(("arbitrary",)))(*ins, *hosted)
    return res[0], res[1], res[2], list(res[3:])


def core_bwd(heads_fn, ins, s0_all, t_all, dy, name):
    T = ins[0].shape[0]
    nc = T // CHUNK
    n = len(ins)

    def body(*refs):
        s0_ref, t_ref, dy_ref = refs[n:n + 3]
        outs = refs[n + 3:n + 3 + n]
        ds_ref = refs[n + 3 + n]

        @pl.when(pl.program_id(0) == 0)
        def _():
            ds_ref[...] = jnp.zeros_like(ds_ref)

        t_saved = t_ref[0]
        f = lambda s0, *xs: heads_fn(s0, *xs, lambda m: _inv_saved(m, t_saved))
        _, vjp = jax.vjp(f, s0_ref[0], *[r[...] for r in refs[:n]])
        grads = vjp((dy_ref[...], ds_ref[...]))
        ds_ref[...] = grads[0]
        for k in range(n):
            outs[k][...] = grads[1 + k]

    row = pl.BlockSpec((CHUNK, MIX_W), lambda c: (nc - 1 - c, 0))
    st_shape = (HEADS, HEAD_DIM, HEAD_DIM)
    st = pl.BlockSpec((1,) + st_shape, lambda c: (nc - 1 - c, 0, 0, 0))
    return pl.pallas_call(
        body, name=name, grid=(nc,), in_specs=[row] * n + [st, st, row], out_specs=tuple([row] * n),
        out_shape=tuple(jax.ShapeDtypeStruct((T, MIX_W), F32) for _ in range(n)),
        scratch_shapes=[pltpu.VMEM(st_shape, F32)],
        compiler_params=_params(("arbitrary",)))(*ins, s0_all, t_all, dy)


def _block_diag(w):
    out = jnp.zeros((LRU_W, LRU_W), w.dtype)
    for n in range(LRU_BLOCKS):
        out = lax.dynamic_update_slice(out, w[n], (n * 64, n * 64))
    return out


def _block_diag_grad(g):
    return jnp.stack([g[n * 64:(n + 1) * 64, n * 64:(n + 1) * 64] for n in range(LRU_BLOCKS)])


def _row(v):
    return v.reshape(1, -1)


def _pad128(v):
    return jnp.pad(v.reshape(1, -1), ((0, 0), (0, 128 - v.size)))


def _layer_shared(w, l):
    cw = w['lru_conv_w'][l]
    lru_pre = [_row(cw[0]), _row(cw[1]), _row(cw[2]), _row(cw[3]), _row(w['lru_conv_b'][l]),
               _block_diag(w['lru_gate_a_w'][l]), _row(w['lru_gate_a_b'][l]),
               _block_diag(w['lru_gate_x_w'][l]), _row(w['lru_gate_x_b'][l]), _row(w['lru_lambda'][l])]
    rw_pre = [_row(w['rwkv_mu'][l]), w['rwkv_w_up'][l], _row(w['rwkv_w_bias'][l]), w['rwkv_a_up'][l],
              _row(w['rwkv_a_bias'][l]), w['rwkv_g_up'][l], _row(w['rwkv_k_k'][l]), _row(w['rwkv_k_a'][l])]
    if l > 0:
        rw_pre += [w['rwkv_vres_w1'][l - 1], w['rwkv_vres_w2'][l - 1], _row(w['rwkv_vres_b'][l - 1])]
    rw_post = [_row(w['rwkv_ln_g'][l]), _row(w['rwkv_ln_b'][l]), _row(w['rwkv_r_k'][l])]
    gw = w['gdn_conv_w'][l]
    gdn_pre = [_row(gw[0]), _row(gw[1]), _row(gw[2]), _row(gw[3]), _pad128(w['gdn_a_log'][l]),
               _pad128(w['gdn_dt_bias'][l])]
    gdn_post = [_row(jnp.tile(w['gdn_norm'][l], HEADS))]
    return dict(lru_pre=lru_pre, lru_post=[_row(w['lru_out_norm'][l])], rw_pre=rw_pre, rw_post=rw_post,
                gdn_pre=gdn_pre, gdn_post=gdn_post)


def _mixer_fwd(p, sh, l, v_first, host_rwkv=(), host_gdn=()):
    T = p.shape[0]
    lx, ly = p[:, 0:256], p[:, 256:512]
    prw, qkv, z, ab = p[:, 512:1920], p[:, 1920:3072], p[:, 3072:3456], p[:, 3456:3584]
    first = jnp.zeros((T, LRU_W), F32).at[0].set(1.0)
    lru_rows = [lx, shift_rows(lx, 1), shift_rows(lx, 2), shift_rows(lx, 3), first]
    a, b = rowwise_fwd(lru_pre_fn, lru_rows, sh['lru_pre'], (LRU_W, LRU_W), f"lru_pre_fwd{l}")
    hseq = lru_scan(a, b, False, f"lru_scan_fwd{l}")
    (y_lru,) = rowwise_fwd(lru_post_fn, [hseq, ly], sh['lru_post'], (LRU_W,), f"lru_post_fwd{l}")

    rw_rows = [prw, shift_rows(prw, 1)] + ([v_first] if l > 0 else [])
    rw = rowwise_fwd(make_rwkv_pre_fn(l > 0), rw_rows, sh['rw_pre'], (MIX_W,) * 7, f"rwkv_pre_fwd{l}")
    r, lw, k2, v, kk, ar, g = rw
    y_raw, rs0, rt, got_rwkv = core_fwd(rwkv_heads, [r, lw, k2, v, kk, ar], f"rwkv_core_fwd{l}", host_rwkv)
    (y_rw,) = rowwise_fwd(rwkv_post_fn, [y_raw, r, k2, v, g], sh['rw_post'], (MIX_W,), f"rwkv_post_fwd{l}")

    gdn_rows = [qkv, shift_rows(qkv, 1), shift_rows(qkv, 2), shift_rows(qkv, 3), ab]
    gd = rowwise_fwd(gdn_pre_fn, gdn_rows, sh['gdn_pre'], (MIX_W,) * 5, f"gdn_pre_fwd{l}")
    o_raw, gs0, gt, got_gdn = core_fwd(gdn_heads, list(gd), f"gdn_core_fwd{l}", host_gdn)
    (y_gdn,) = rowwise_fwd(gdn_post_fn, [o_raw, z], sh['gdn_post'], (MIX_W,), f"gdn_post_fwd{l}")

    mixed = jnp.concatenate([y_lru, y_rw, y_gdn], axis=1)
    saved = dict(lru_rows=lru_rows, a=a, hseq=hseq, ly=ly, rw_rows=rw_rows, rw=rw, y_raw=y_raw, rs0=rs0, rt=rt,
                 gdn_rows=gdn_rows, gd=gd, o_raw=o_raw, gs0=gs0, gt=gt, z=z)
    v_layer0 = v if l == 0 else None
    return mixed, saved, v_layer0, got_rwkv, got_gdn


def _mixer_bwd(dmixed, sv, sh, l, dv_first):
    d_lru, d_rw, d_gdn = dmixed[:, 0:256], dmixed[:, 256:640], dmixed[:, 640:1024]
    gw = {}

    dh, dly, d_og = rowwise_bwd(lru_post_fn, [sv['hseq'], sv['ly']], sh['lru_post'], [d_lru], f"lru_post_bwd{l}")
    gscan = lru_scan(unshift_rows(sv['a'], 1), dh, True, f"lru_scan_bwd{l}")
    res = rowwise_bwd(lru_pre_fn, sv['lru_rows'], sh['lru_pre'], [gscan, shift_rows(sv['hseq'], 1)],
                      f"lru_pre_bwd{l}", ct_fn=lambda gs, hp: (gs * hp, gs))
    dlx = res[0] + unshift_rows(res[1], 1) + unshift_rows(res[2], 2) + unshift_rows(res[3], 3)
    dw0, dw1, dw2, dw3, dcb, dga, dgab, dgx, dgxb, dlam = res[5:]
    gw['lru_conv_w'] = jnp.concatenate([dw0, dw1, dw2, dw3], axis=0)
    gw['lru_conv_b'] = dcb[0]
    gw['lru_gate_a_w'] = _block_diag_grad(dga)
    gw['lru_gate_a_b'] = dgab.reshape(LRU_BLOCKS, 64)
    gw['lru_gate_x_w'] = _block_diag_grad(dgx)
    gw['lru_gate_x_b'] = dgxb.reshape(LRU_BLOCKS, 64)
    gw['lru_lambda'] = dlam[0]
    gw['lru_out_norm'] = d_og[0]

    r, lw, k2, v, kk, ar, g = sv['rw']
    res = rowwise_bwd(rwkv_post_fn, [sv['y_raw'], r, k2, v, g], sh['rw_post'], [d_rw], f"rwkv_post_bwd{l}")
    dy_raw, dr_p, dk2_p, dv_p, dg = res[:5]
    gw['rwkv_ln_g'], gw['rwkv_ln_b'], gw['rwkv_r_k'] = res[5][0], res[6][0], res[7].reshape(HEADS, HEAD_DIM)
    dr_c, dlw, dk2_c, dv_c, dkk, dar = core_bwd(rwkv_heads, [r, lw, k2, v, kk, ar], sv['rs0'], sv['rt'], dy_raw,
                                                 f"rwkv_core_bwd{l}")
    cts = [dr_p, dr_c, dlw, dk2_p, dk2_c, dv_p, dv_c, dkk, dar, dg]
    if l == 0:
        cts.append(dv_first)
        ct_fn = lambda a1, a2, b, c1, c2, d1, d2, e, f, gg, vf: (a1 + a2, b, c1 + c2, d1 + d2 + vf, e, f, gg)
    else:
        ct_fn = lambda a1, a2, b, c1, c2, d1, d2, e, f, gg: (a1 + a2, b, c1 + c2, d1 + d2, e, f, gg)
    res = rowwise_bwd(make_rwkv_pre_fn(l > 0), sv['rw_rows'], sh['rw_pre'], cts, f"rwkv_pre_bwd{l}", ct_fn=ct_fn)
    dprw = res[0] + unshift_rows(res[1], 1)
    nrow = len(sv['rw_rows'])
    dv_first_out = res[2] if l > 0 else None
    sg = res[nrow:]
    gw['rwkv_mu'], gw['rwkv_w_up'], gw['rwkv_w_bias'], gw['rwkv_a_up'] = sg[0][0], sg[1], sg[2][0], sg[3]
    gw['rwkv_a_bias'], gw['rwkv_g_up'], gw['rwkv_k_k'], gw['rwkv_k_a'] = sg[4][0], sg[5], sg[6][0], sg[7][0]
    if l > 0:
        gw['rwkv_vres_w1'], gw['rwkv_vres_w2'], gw['rwkv_vres_b'] = sg[8], sg[9], sg[10][0]

    do_raw, dz, d_ng = rowwise_bwd(gdn_post_fn, [sv['o_raw'], sv['z']], sh['gdn_post'], [d_gdn], f"gdn_post_bwd{l}")
    gw['gdn_norm'] = jnp.sum(d_ng.reshape(HEADS, HEAD_DIM), axis=0)
    dgd = core_bwd(gdn_heads, list(sv['gd']), sv['gs0'], sv['gt'], do_raw, f"gdn_core_bwd{l}")
    res = rowwise_bwd(gdn_pre_fn, sv['gdn_rows'], sh['gdn_pre'], list(dgd), f"gdn_pre_bwd{l}")
    dqkv = res[0] + unshift_rows(res[1], 1) + unshift_rows(res[2], 2) + unshift_rows(res[3], 3)
    dab = res[4]
    gw['gdn_conv_w'] = jnp.concatenate(res[5:9], axis=0)
    gw['gdn_a_log'], gw['gdn_dt_bias'] = res[9][0, :HEADS], res[10][0, :HEADS]

    dp = jnp.concatenate([dlx, dly, dprw, dqkv, dz, dab], axis=1)
    return dp, gw, dv_first_out


IN_SHARD = D_IN // N_CHIPS
IN_SHARD_PAD = D_IN_PAD // N_CHIPS


def _cols_to_chips(g, n=N_CHIPS):
    r = g.shape[0]
    return jnp.transpose(g.reshape(r, n, -1), (1, 0, 2))


def _cols_from_chips(g):
    return jnp.transpose(g, (1, 0, 2)).reshape(g.shape[1], -1)


def _w_in_from_chips(g):
    nat = _cols_from_chips(g[:, :, :IN_SHARD])
    return jnp.pad(nat, ((0, 0), (0, D_IN_PAD - D_IN)))


def _w_in_to_chips(g):
    return jnp.pad(_cols_to_chips(g[:, :D_IN]), ((0, 0), (0, 0), (0, IN_SHARD_PAD - IN_SHARD)))


def _natural(name, g):
    if name == 'w_in':
        return _w_in_from_chips(g)
    if BIG[name] == 2:
        return _cols_from_chips(g)
    return g.reshape(-1, g.shape[2])


def local_step(x, target, w, wb, shards=None):
    def hosted(keys):
        return [shards[k] for k in keys] if shards is not None else []

    def arrived(keys, gathered):
        for (name, layer), g in zip(keys if shards is not None else [], gathered):
            wb[name][layer] = _natural(name, g)

    saved = []
    v_first = None
    for l in range(N_LAYERS):
        sh = _layer_shared(w, l)
        for_mixer = [('w_in', l), ('w_out', l)]
        for_ffn2 = [('ffn2_wi', l), ('ffn2_wo', l)]
        for_next = [('ffn1_wi', l + 1), ('ffn1_wo', l + 1)] if l + 1 < N_LAYERS else []
        x1, got = ffn_fwd(x, _row(w['ffn1_norm'][l]), wb['ffn1_wi'][l], wb['ffn1_wo'][l], f"ffn1_fwd{l}",
                          hosted(for_mixer))
        arrived(for_mixer, got)
        p = proj_fwd(x1, _row(w['mix_norm'][l]), wb['w_in'][l], f"proj_fwd{l}")
        mixed, sv, v0, got_ffn2, got_next = _mixer_fwd(p, sh, l, v_first, hosted(for_ffn2), hosted(for_next))
        arrived(for_ffn2, got_ffn2)
        arrived(for_next, got_next)
        if l == 0:
            v_first = v0
        x2 = out_fwd(mixed, wb['w_out'][l], x1, f"out_fwd{l}")
        x3, _ = ffn_fwd(x2, _row(w['ffn2_norm'][l]), wb['ffn2_wi'][l], wb['ffn2_wo'][l], f"ffn2_fwd{l}")
        saved.append(dict(x0=x, x1=x1, x2=x2, mixed=mixed, sv=sv, sh=sh))
        x = x3

    loss, dx, dgf = loss_head(x, _row(w['final_norm']), target, "loss_head")
    per_layer = [None] * N_LAYERS
    dv_first = jnp.zeros((x.shape[0], MIX_W), F32)
    for l in reversed(range(N_LAYERS)):
        s = saved[l]
        gw = {}
        dx, dg2, dwg, dwu, dwo = ffn_bwd(s['x2'], dx, _row(w['ffn2_norm'][l]), wb['ffn2_wi'][l], wb['ffn2_wo'][l],
                                         f"ffn2_bwd{l}")
        wi_parts = lambda dwg, dwu: (dwg, dwu)
        row_parts = lambda dw: dw.reshape(N_CHIPS, -1, dw.shape[1])
        gw['ffn2_norm'], gw['ffn2_wi'], gw['ffn2_wo'] = dg2[0], wi_parts(dwg, dwu), row_parts(dwo)
        dmixed, dw_out = out_bwd(s['mixed'], wb['w_out'][l], dx, f"out_bwd{l}")
        gw['w_out'] = row_parts(dw_out)
        dp, gmix, dvf = _mixer_bwd(dmixed, s['sv'], s['sh'], l, dv_first)
        if l > 0:
            dv_first = dvf
        gw.update(gmix)
        dx, dgm, dwin = proj_bwd(s['x1'], dx, _row(w['mix_norm'][l]), wb['w_in'][l], dp, f"proj_bwd{l}")
        gw['mix_norm'], gw['w_in'] = dgm[0], _w_in_to_chips(dwin)
        dx, dg1, dwg, dwu, dwo = ffn_bwd(s['x0'], dx, _row(w['ffn1_norm'][l]), wb['ffn1_wi'][l], wb['ffn1_wo'][l],
                                         f"ffn1_bwd{l}")
        gw['ffn1_norm'], gw['ffn1_wi'], gw['ffn1_wo'] = dg1[0], wi_parts(dwg, dwu), row_parts(dwo)
        per_layer[l] = gw

    grads = {'final_norm': dgf[0]}
    for name in WEIGHTS:
        if name == 'final_norm':
            continue
        if name in BIG:
            grads[name] = [per_layer[l][name] for l in range(N_LAYERS)]
        elif name.startswith('rwkv_vres'):
            grads[name] = per_layer[1][name][None]
        else:
            grads[name] = jnp.stack([per_layer[l][name] for l in range(N_LAYERS)])
    return loss[0, 0], dx, grads


ANY = pl.BlockSpec(memory_space=pl.ANY)


def _coords():
    return lax.axis_index("x"), lax.axis_index("y"), lax.axis_index("c")


def _other_chips(x, y):
    return [((x + 1) % 2, y), (x, (y + 1) % 2), ((x + 1) % 2, (y + 1) % 2)]


def allreduce_small(pack, name):
    R = pack.shape[0]

    def body(x_ref, o_ref, buf, send_sems, recv_sems):
        x, y, c = _coords()
        me = 4 * x + 2 * y + c
        buf[me] = x_ref[...]
        copies = []
        for k in range(1, 8):
            peer = ((x + (k >> 2)) % 2, (y + ((k >> 1) & 1)) % 2, (c + (k & 1)) % 2)
            cp = pltpu.make_async_remote_copy(src_ref=x_ref, dst_ref=buf.at[me], send_sem=send_sems.at[k - 1],
                                              recv_sem=recv_sems.at[k - 1], device_id=peer, device_id_type=MESH)
            cp.start()
            copies.append(cp)
        for cp in copies:
            cp.wait()
        acc = buf[0]
        for d in range(1, 8):
            acc = acc + buf[d]
        o_ref[...] = acc

    vm = pl.BlockSpec(memory_space=pltpu.VMEM)
    return pl.pallas_call(
        body, name=name, in_specs=[vm], out_specs=vm, out_shape=jax.ShapeDtypeStruct((R, 128), F32),
        scratch_shapes=[pltpu.VMEM((8, R, 128), F32), pltpu.SemaphoreType.DMA((7,)), pltpu.SemaphoreType.DMA((7,))],
        compiler_params=_params())(pack)


class ChipGather:
    def __init__(self, shards):
        self.shapes = [s.shape for s in shards]
        self.n = len(shards)
        self.in_specs = [ANY] * self.n
        self.out_specs = [ANY] * self.n
        self.out_shape = [jax.ShapeDtypeStruct((N_CHIPS,) + s.shape, s.dtype) for s in shards]
        self.scratch = [pltpu.SemaphoreType.DMA((6 * self.n,)), pltpu.SemaphoreType.DMA((6 * self.n,)),
                        pltpu.SemaphoreType.DMA((self.n,))] if self.n else []

    def _rows(self, a, core):
        rh = self.shapes[a][0] // 2
        return pl.ds(pl.multiple_of(core * rh, 16), rh)

    def _copies(self, kind, x_refs, o_refs, sems):
        send_sems, recv_sems, local_sems = sems
        x, y, c = _coords()
        s_me = 2 * x + y
        sib = (x, y, 1 - c)

        def copy(a, k, src, dst, to):
            return pltpu.make_async_remote_copy(src_ref=src, dst_ref=dst, send_sem=send_sems.at[6 * a + k],
                                                recv_sem=recv_sems.at[6 * a + k], device_id=to, device_id_type=MESH)

        if kind == 'own':
            return [pltpu.make_async_copy(x_refs[a], o_refs[a].at[s_me], local_sems.at[a]) for a in range(self.n)]
        out = []
        for j, (px, py) in enumerate(_other_chips(x, y)):
            for a in range(self.n):
                mine = self._rows(a, c)
                part = o_refs[a].at[2 * px + py, mine]
                if kind == 'sent':
                    out.append(copy(a, j, x_refs[a].at[mine], o_refs[a].at[s_me, mine], (px, py, c)))
                elif kind == 'arrived':
                    out.append(copy(a, j, part, part, (px, py, c)))
                elif kind == 'passed':
                    out.append(copy(a, 3 + j, part, part, sib))
                else:
                    theirs = o_refs[a].at[2 * px + py, self._rows(a, 1 - c)]
                    out.append(copy(a, 3 + j, theirs, theirs, sib))
        return out

    def start(self, x_refs, o_refs, sems):
        if not self.n:
            return
        for cp in self._copies('own', x_refs, o_refs, sems) + self._copies('sent', x_refs, o_refs, sems):
            cp.start()

    def relay(self, x_refs, o_refs, sems):
        if not self.n:
            return
        for got, fw in zip(self._copies('arrived', x_refs, o_refs, sems),
                           self._copies('passed', x_refs, o_refs, sems)):
            got.wait_recv()
            fw.start()

    def finish(self, x_refs, o_refs, sems):
        if not self.n:
            return
        for cp in self._copies('from_sibling', x_refs, o_refs, sems):
            cp.wait_recv()
        for cp in self._copies('sent', x_refs, o_refs, sems) + self._copies('passed', x_refs, o_refs, sems):
            cp.wait_send()
        for cp in self._copies('own', x_refs, o_refs, sems):
            cp.wait()


def allgather_chips(shards, name):
    gather = ChipGather(shards)
    n = gather.n

    def body(*refs):
        x_refs, o_refs, sems = refs[:n], refs[n:2 * n], refs[2 * n:]
        gather.start(x_refs, o_refs, sems)
        gather.relay(x_refs, o_refs, sems)
        gather.finish(x_refs, o_refs, sems)

    return pl.pallas_call(
        body, name=name, in_specs=gather.in_specs, out_specs=tuple(gather.out_specs),
        out_shape=tuple(gather.out_shape), scratch_shapes=gather.scratch, compiler_params=_params())(*shards)


def sibling_swap(srcs, halves, name):
    n = len(srcs)
    row_axis = [s.ndim - 2 for s in srcs]
    out_shapes = [s.shape[:ax] + (s.shape[ax] // 2,) + s.shape[ax + 1:] if halves else s.shape
                  for s, ax in zip(srcs, row_axis)]

    def body(*refs):
        x_refs, o_refs = refs[:n], refs[n:2 * n]
        send_sems, recv_sems = refs[2 * n:]
        x, y, c = _coords()
        copies = []
        for a in range(n):
            part = x_refs[a]
            if halves:
                rh = srcs[a].shape[row_axis[a]] // 2
                theirs = pl.ds(pl.multiple_of((1 - c) * rh, 16), rh)
                part = part.at[:, theirs] if row_axis[a] == 1 else part.at[theirs]
            cp = pltpu.make_async_remote_copy(src_ref=part, dst_ref=o_refs[a], send_sem=send_sems.at[a],
                                              recv_sem=recv_sems.at[a], device_id=(x, y, 1 - c), device_id_type=MESH)
            cp.start()
            copies.append(cp)
        for cp in copies:
            cp.wait()

    return pl.pallas_call(
        body, name=name, in_specs=[ANY] * n, out_specs=tuple([ANY] * n),
        out_shape=tuple(jax.ShapeDtypeStruct(sh, s.dtype) for sh, s in zip(out_shapes, srcs)),
        scratch_shapes=[pltpu.SemaphoreType.DMA((n,)), pltpu.SemaphoreType.DMA((n,))],
        compiler_params=_params())(*srcs)


def scatter_chips(parts, name):
    n = len(parts)

    def body(*refs):
        x_refs, o_refs = refs[:n], refs[n:2 * n]
        send_sems, recv_sems = refs[2 * n:]
        x, y, c = _coords()
        copies = []
        for j, (px, py) in enumerate(_other_chips(x, y)):
            for a in range(n):
                cp = pltpu.make_async_remote_copy(src_ref=x_refs[a].at[2 * px + py], dst_ref=o_refs[a].at[j],
                                                  send_sem=send_sems.at[3 * a + j], recv_sem=recv_sems.at[3 * a + j],
                                                  device_id=(px, py, c), device_id_type=MESH)
                cp.start()
                copies.append(cp)
        for cp in copies:
            cp.wait()

    return pl.pallas_call(
        body, name=name, in_specs=[ANY] * n, out_specs=tuple([ANY] * n),
        out_shape=tuple(jax.ShapeDtypeStruct((3,) + p.shape[1:], p.dtype) for p in parts),
        scratch_shapes=[pltpu.SemaphoreType.DMA((3 * n,)), pltpu.SemaphoreType.DMA((3 * n,))],
        compiler_params=_params())(*parts)


def _row_block(rows):
    return max(b for b in range(16, 257, 16) if rows % b == 0)


def chip_sum(gpack, recv, core, name):
    n, R, W = gpack.shape
    rh = R // 2
    rb = _row_block(rh)
    nb = rh // rb

    def body(c_ref, g_ref, r_ref, o_ref, ob_ref):
        s = g_ref[...] + r_ref[...]
        o_ref[...] = s
        ob_ref[...] = s.astype(BF16)

    blk = pl.BlockSpec((1, rb, W), lambda i, j, c_ref: (i, j, 0))
    spec = pltpu.PrefetchScalarGridSpec(
        num_scalar_prefetch=1, grid=(n, nb),
        in_specs=[pl.BlockSpec((1, rb, W), lambda i, j, c_ref: (i, c_ref[0] * nb + j, 0)), blk],
        out_specs=(blk, blk))
    return pl.pallas_call(
        body, name=name, grid_spec=spec,
        out_shape=(jax.ShapeDtypeStruct((n, rh, W), F32), jax.ShapeDtypeStruct((n, rh, W), BF16)),
        compiler_params=_params(("arbitrary", "arbitrary")))(core, gpack, recv)


def chip_sum_cols(gate, up, recv_gate, recv_up, core, name):
    R, W = gate.shape
    cw = W // 2
    rh = R // 2
    rb = _row_block(rh)
    nb = rh // rb

    def body(c_ref, g_ref, u_ref, rg_ref, ru_ref, o_ref, ob_ref):
        s = jnp.where(pl.program_id(0) < 2, g_ref[...] + rg_ref[...], u_ref[...] + ru_ref[...])
        o_ref[0] = s
        ob_ref[0] = s.astype(BF16)

    gate_col = lambda s: jnp.minimum(s, 1)
    up_col = lambda s: jnp.maximum(s - 2, 0)
    out = pl.BlockSpec((1, rb, cw), lambda s, j, c_ref: (s, j, 0))
    spec = pltpu.PrefetchScalarGridSpec(
        num_scalar_prefetch=1, grid=(N_CHIPS, nb),
        in_specs=[pl.BlockSpec((rb, cw), lambda s, j, c_ref: (c_ref[0] * nb + j, gate_col(s))),
                  pl.BlockSpec((rb, cw), lambda s, j, c_ref: (c_ref[0] * nb + j, up_col(s))),
                  pl.BlockSpec((rb, cw), lambda s, j, c_ref: (j, gate_col(s))),
                  pl.BlockSpec((rb, cw), lambda s, j, c_ref: (j, up_col(s)))],
        out_specs=(out, out))
    return pl.pallas_call(
        body, name=name, grid_spec=spec,
        out_shape=(jax.ShapeDtypeStruct((N_CHIPS, rh, cw), F32), jax.ShapeDtypeStruct((N_CHIPS, rh, cw), BF16)),
        compiler_params=_params(("arbitrary", "arbitrary")))(core, gate, up, recv_gate, recv_up)


def shard_sum(own, recv, name):
    R, W = own.shape
    rb = _row_block(R)

    def body(a_ref, r_ref, o_ref):
        acc = a_ref[...]
        for j in range(3):
            acc = acc + r_ref[j].astype(F32)
        o_ref[...] = acc

    return pl.pallas_call(
        body, name=name, grid=(R // rb,),
        in_specs=[pl.BlockSpec((rb, W), lambda i: (i, 0)), pl.BlockSpec((3, rb, W), lambda i: (0, i, 0))],
        out_specs=pl.BlockSpec((rb, W), lambda i: (i, 0)), out_shape=jax.ShapeDtypeStruct((R, W), F32),
        compiler_params=_params(("arbitrary",)))(own, recv)


def adamw(w, m, v, g, name):
    L, R, C = w.shape
    rb = max(b for b in range(8, 257, 8) if R % b == 0)
    bc1 = 1.0 - ADAM_B1 ** ADAM_STEP
    bc2 = 1.0 - ADAM_B2 ** ADAM_STEP

    def body(w_ref, m_ref, v_ref, g_ref, d_ref, nm_ref, nv_ref):
        gv = g_ref[...]
        nm = ADAM_B1 * m_ref[...] + (1.0 - ADAM_B1) * gv
        nv = ADAM_B2 * v_ref[...] + (1.0 - ADAM_B2) * (gv * gv)
        d_ref[...] = -ADAM_LR * ((nm / bc1) / (jnp.sqrt(nv / bc2) + ADAM_EPS) + ADAM_WD * w_ref[...])
        nm_ref[...] = nm
        nv_ref[...] = nv

    blk = pl.BlockSpec((1, rb, C), lambda l, i: (l, i, 0))
    sh = jax.ShapeDtypeStruct((L, R, C), F32)
    return pl.pallas_call(body, name=name, grid=(L, R // rb), in_specs=[blk] * 4, out_specs=(blk,) * 3,
                          out_shape=(sh, sh, sh), compiler_params=_params(("arbitrary", "arbitrary")))(w, m, v, g)


SMALL = [n for n in WEIGHTS if n not in BIG]


PACK_TILE = 8 * 128


def _pack(arrays):
    blocks = []
    for a in arrays:
        flat = a.reshape(-1)
        flat = jnp.pad(flat, (0, -flat.size % PACK_TILE))
        blocks.append(flat.reshape(-1, 128))
    return jnp.concatenate(blocks, axis=0)


def _unpack(pack, shapes):
    out, row = [], 0
    for shape in shapes:
        size = int(np.prod(shape))
        rows = -(-size // PACK_TILE) * 8
        out.append(pack[row:row + rows].reshape(-1)[:size].reshape(shape))
        row += rows
    return out


def _pad_lanes(a):
    return jnp.pad(a, ((0, 0), (0, -a.shape[1] % 128)))


def _local_shard(full, axis, chip):
    size = full.shape[axis] // N_CHIPS
    return lax.dynamic_slice_in_dim(full, chip * size, size, axis)


def kernel(x, ffn1_norm, ffn1_wi, ffn1_wo, mix_norm, w_in, w_out, lru_conv_w, lru_conv_b, lru_gate_a_w, lru_gate_a_b, lru_gate_x_w, lru_gate_x_b, lru_lambda, lru_out_norm, rwkv_mu, rwkv_w_up, rwkv_w_bias, rwkv_a_up, rwkv_a_bias, rwkv_g_up, rwkv_k_k, rwkv_k_a, rwkv_r_k, rwkv_ln_g, rwkv_ln_b, rwkv_vres_w1, rwkv_vres_w2, rwkv_vres_b, gdn_conv_w, gdn_a_log, gdn_dt_bias, gdn_norm, ffn2_norm, ffn2_wi, ffn2_wo, final_norm, loss_target, m_ffn1_norm, m_ffn1_wi, m_ffn1_wo, m_mix_norm, m_w_in, m_w_out, m_lru_conv_w, m_lru_conv_b, m_lru_gate_a_w, m_lru_gate_a_b, m_lru_gate_x_w, m_lru_gate_x_b, m_lru_lambda, m_lru_out_norm, m_rwkv_mu, m_rwkv_w_up, m_rwkv_w_bias, m_rwkv_a_up, m_rwkv_a_bias, m_rwkv_g_up, m_rwkv_k_k, m_rwkv_k_a, m_rwkv_r_k, m_rwkv_ln_g, m_rwkv_ln_b, m_rwkv_vres_w1, m_rwkv_vres_w2, m_rwkv_vres_b, m_gdn_conv_w, m_gdn_a_log, m_gdn_dt_bias, m_gdn_norm, m_ffn2_norm, m_ffn2_wi, m_ffn2_wo, m_final_norm, v_ffn1_norm, v_ffn1_wi, v_ffn1_wo, v_mix_norm, v_w_in, v_w_out, v_lru_conv_w, v_lru_conv_b, v_lru_gate_a_w, v_lru_gate_a_b, v_lru_gate_x_w, v_lru_gate_x_b, v_lru_lambda, v_lru_out_norm, v_rwkv_mu, v_rwkv_w_up, v_rwkv_w_bias, v_rwkv_a_up, v_rwkv_a_bias, v_rwkv_g_up, v_rwkv_k_k, v_rwkv_k_a, v_rwkv_r_k, v_rwkv_ln_g, v_rwkv_ln_b, v_rwkv_vres_w1, v_rwkv_vres_w2, v_rwkv_vres_b, v_gdn_conv_w, v_gdn_a_log, v_gdn_dt_bias, v_gdn_norm, v_ffn2_norm, v_ffn2_wi, v_ffn2_wo, v_final_norm):
    args = locals()
    w_loc = {n: args[n] for n in WEIGHTS}
    m_loc = {n: args['m_' + n] for n in WEIGHTS}
    v_loc = {n: args['v_' + n] for n in WEIGHTS}
    chip = 2 * lax.axis_index("x") + lax.axis_index("y")
    core = lax.axis_index("c")

    big = [(n, l) for n in BIG for l in range(N_LAYERS)]
    shards = {(n, l): _pad_lanes(w_loc[n][l].astype(BF16)) for n, l in big}
    first = [('ffn1_wi', 0), ('ffn1_wo', 0)]
    wb = {n: [None] * N_LAYERS for n in BIG}
    for (n, l), g in zip(first, allgather_chips([shards[k] for k in first], "allgather_first")):
        wb[n][l] = _natural(n, g)

    sm_names = list(SMALL_SHARDED)
    placed = []
    for n in sm_names:
        ax = SMALL_SHARDED[n]
        full_shape = w_loc[n].shape[:ax] + (N_CHIPS * w_loc[n].shape[ax],) + w_loc[n].shape[ax + 1:]
        src = w_loc[n] * (core == 0).astype(F32)
        placed.append(lax.dynamic_update_slice_in_dim(jnp.zeros(full_shape, F32), src, chip * w_loc[n].shape[ax], ax))
    summed = allreduce_small(_pack(placed), "allgather_small")
    w_full = dict(w_loc)
    w_full.update(zip(sm_names, _unpack(summed, [p.shape for p in placed])))

    loss, dx, grads = local_step(x[0], loss_target[0], w_full, wb, shards)
    loss = lax.psum(loss, ("x", "y", "c"))

    gsum = allreduce_small(_pack([grads[n] for n in SMALL]), "allreduce_small")
    g_loc = {}
    for n, g in zip(SMALL, _unpack(gsum, [grads[n].shape for n in SMALL])):
        g_loc[n] = _local_shard(g, SMALL_SHARDED[n], chip) if n in SMALL_SHARDED else g

    parts = []
    for n, l in big:
        parts += list(grads[n][l]) if isinstance(grads[n][l], tuple) else [grads[n][l]]
    swapped = iter(zip(parts, sibling_swap(parts, True, "grad_swap_cores")))
    core_arg = core.reshape(1).astype(jnp.int32)
    sums = []
    for n, l in big:
        if isinstance(grads[n][l], tuple):
            (dwg, from_g), (dwu, from_u) = next(swapped), next(swapped)
            sums.append(chip_sum_cols(dwg, dwu, from_g, from_u, core_arg, f"grad_chip_sum_{n}{l}"))
        else:
            p, r = next(swapped)
            sums.append(chip_sum(p, r, core_arg, f"grad_chip_sum_{n}{l}"))
    from_chips = scatter_chips([s[1] for s in sums], "grad_scatter")
    halves = [shard_sum(lax.dynamic_index_in_dim(s[0], chip, 0, keepdims=False), r, f"grad_shard_sum_{n}{l}")
              for (n, l), s, r in zip(big, sums, from_chips)]
    others = sibling_swap(halves, False, "grad_share_cores")
    rows = {n: [None] * N_LAYERS for n in BIG}
    for (n, l), half, other in zip(big, halves, others):
        lower = jnp.where(core == 0, half, other)
        upper = jnp.where(core == 0, other, half)
        rows[n][l] = jnp.concatenate([lower, upper], axis=0)[:, :w_loc[n].shape[-1]]
    big_names = list(BIG)
    for n in big_names:
        g_loc[n] = jnp.stack(rows[n])

    delta, new_m, new_v = {}, {}, {}
    for n in big_names:
        delta[n], new_m[n], new_v[n] = adamw(w_loc[n], m_loc[n], v_loc[n], g_loc[n], f"adamw_{n}")
    pack = lambda d: _pack([d[n] for n in SMALL])[None]
    res = adamw(pack(w_loc), pack(m_loc), pack(v_loc), pack(g_loc), "adamw_small")
    for dst, r in zip((delta, new_m, new_v), res):
        dst.update(zip(SMALL, _unpack(r[0], [w_loc[n].shape for n in SMALL])))

    return (loss, dx[None], *[g_loc[n] for n in WEIGHTS], *[delta[n] for n in WEIGHTS],
            *[new_m[n] for n in WEIGHTS], *[new_v[n] for n in WEIGHTS])
```

```python
import functools

import numpy as np
import jax
import jax.numpy as jnp
from jax import lax
from jax.experimental import pallas as pl
from jax.experimental.pallas import tpu as pltpu

F32 = jnp.float32
BF16 = jnp.bfloat16
HIGHEST = lax.Precision.HIGHEST
MESH = pl.DeviceIdType.MESH

D_MODEL = 1024
D_FF = 2816
N_LAYERS = 2
HEADS = 6
HEAD_DIM = 64
MIX_W = HEADS * HEAD_DIM
LRU_W = 256
LRU_BLOCKS = 4
RWKV_IN = 1408
D_IN = 3468
D_IN_PAD = 3584
NORM_EPS = 1e-6
GN_EPS = 64e-5
LRU_C = 8.0
CHUNK = 64
ROWS = 256
FF_CHUNK = 256
IN_CHUNK = 512
PACK_W = 1024
VMEM_LIMIT = 56 * 1024 * 1024

ADAM_LR, ADAM_B1, ADAM_B2, ADAM_EPS, ADAM_WD, ADAM_STEP = 0.001, 0.9, 0.999, 1e-08, 0.01, 10

WEIGHTS = ['ffn1_norm', 'ffn1_wi', 'ffn1_wo', 'mix_norm', 'w_in', 'w_out', 'lru_conv_w', 'lru_conv_b',
           'lru_gate_a_w', 'lru_gate_a_b', 'lru_gate_x_w', 'lru_gate_x_b', 'lru_lambda', 'lru_out_norm',
           'rwkv_mu', 'rwkv_w_up', 'rwkv_w_bias', 'rwkv_a_up', 'rwkv_a_bias', 'rwkv_g_up', 'rwkv_k_k',
           'rwkv_k_a', 'rwkv_r_k', 'rwkv_ln_g', 'rwkv_ln_b', 'rwkv_vres_w1', 'rwkv_vres_w2', 'rwkv_vres_b',
           'gdn_conv_w', 'gdn_a_log', 'gdn_dt_bias', 'gdn_norm', 'ffn2_norm', 'ffn2_wi', 'ffn2_wo', 'final_norm']
BIG = {'ffn1_wi': 2, 'ffn1_wo': 1, 'w_in': 2, 'w_out': 1, 'ffn2_wi': 2, 'ffn2_wo': 1}
SMALL_SHARDED = {'lru_conv_w': 2, 'rwkv_w_up': 2, 'rwkv_a_up': 2, 'rwkv_g_up': 2, 'rwkv_vres_w1': 1,
                 'rwkv_vres_w2': 2, 'gdn_conv_w': 2}
N_CHIPS = 4


def _params(sem=None):
    kw = dict(vmem_limit_bytes=VMEM_LIMIT)
    if sem is not None:
        kw['dimension_semantics'] = sem
    return pltpu.CompilerParams(**kw)


def _bdot(a, b, dims=(((1,), (0,)), ((), ()))):
    return lax.dot_general(a.astype(BF16), b.astype(BF16), dims, preferred_element_type=F32)


def _bdot_nt(a, b):
    return _bdot(a, b, (((1,), (1,)), ((), ())))


def _bdot_tn(a, b):
    return _bdot(a, b, (((0,), (0,)), ((), ())))


_DIMS = {'nn': (((1,), (0,)), ((), ())), 'nt': (((1,), (1,)), ((), ())), 'tn': (((0,), (0,)), ((), ()))}


def _split(a, terms):
    parts = []
    for _ in range(terms - 1):
        hi = a.astype(BF16)
        parts.append(hi)
        a = a - hi.astype(F32)
    parts.append(a.astype(BF16))
    return parts


_BATCH_DIMS = {'nn': (((2,), (1,)), ((0,), (0,))), 'nt': (((2,), (2,)), ((0,), (0,))),
               'tn': (((1,), (1,)), ((0,), (0,)))}


def _dot3(a, b, kind):
    ah, al = _split(a, 2)
    bh, bl = _split(b, 2)
    dims = _BATCH_DIMS[kind] if a.ndim == 3 else _DIMS[kind]
    d = lambda p, q: lax.dot_general(p, q, dims, preferred_element_type=F32)
    return d(ah, bh) + (d(ah, bl) + d(al, bh))


@functools.partial(jax.custom_vjp, nondiff_argnums=(2,))
def _cdot_k(a, b, kind):
    return _dot3(a, b, kind)


def _cdot_k_fwd(a, b, kind):
    return _dot3(a, b, kind), (a, b)


def _cdot_k_bwd(kind, res, ct):
    a, b = res
    if kind == 'nn':
        return _dot3(ct, b, 'nt'), _dot3(a, ct, 'tn')
    if kind == 'nt':
        return _dot3(ct, b, 'nn'), _dot3(ct, a, 'tn')
    return _dot3(b, ct, 'nt'), _dot3(a, ct, 'nn')


_cdot_k.defvjp(_cdot_k_fwd, _cdot_k_bwd)


def _dot1(a, b, kind):
    dims = _BATCH_DIMS[kind] if a.ndim == 3 else _DIMS[kind]
    return lax.dot_general(a.astype(BF16), b.astype(BF16), dims, preferred_element_type=F32)


@functools.partial(jax.custom_vjp, nondiff_argnums=(2,))
def _cdot1_k(a, b, kind):
    return _dot1(a, b, kind)


def _cdot1_k_fwd(a, b, kind):
    return _dot1(a, b, kind), (a, b)


def _cdot1_k_bwd(kind, res, ct):
    a, b = res
    if kind == 'nn':
        return _dot1(ct, b, 'nt'), _dot1(a, ct, 'tn')
    if kind == 'nt':
        return _dot1(ct, b, 'nn'), _dot1(ct, a, 'tn')
    return _dot1(b, ct, 'nt'), _dot1(a, ct, 'nn')


_cdot1_k.defvjp(_cdot1_k_fwd, _cdot1_k_bwd)


def _cdot(a, b):
    return _cdot1_k(a, b, 'nn')


def _cdot_nt(a, b):
    return _cdot1_k(a, b, 'nt')


def _cdot_tn(a, b):
    return _cdot1_k(a, b, 'tn')


def _hdot(a, b):
    return _cdot_k(a, b, 'nn')


def _dot_exact(x, m01, kind):
    d = lambda p: lax.dot_general(p, m01.astype(BF16), _DIMS[kind], preferred_element_type=F32)
    hi, mid, lo = _split(x, 3)
    return d(hi) + (d(mid) + d(lo))


@functools.partial(jax.custom_vjp, nondiff_argnums=(1,))
def _xdot(x, make_m):
    return _dot_exact(x, make_m(), 'nn')


def _xdot_fwd(x, make_m):
    return _dot_exact(x, make_m(), 'nn'), None


def _xdot_bwd(make_m, _, ct):
    return (_dot_exact(ct, make_m(), 'nt'),)


_xdot.defvjp(_xdot_fwd, _xdot_bwd)


def _iota2(n, m):
    return lax.broadcasted_iota(jnp.int32, (n, m), 0), lax.broadcasted_iota(jnp.int32, (n, m), 1)


def _head_blocks(w):
    ri, ci = _iota2(w, w)
    return (ri // HEAD_DIM == ci // HEAD_DIM).astype(F32)


def _segsum(x):
    return _xdot(x, functools.partial(_head_blocks, x.shape[-1]))


def _cumsum_rows(x):
    return _cumsum_k(x, x.shape[0])


@functools.partial(jax.custom_vjp, nondiff_argnums=(1,))
def _cumsum_k(x, n):
    return _lower_dot(x, n, False)


def _lower_dot(x, n, transpose):
    ri, ci = _iota2(n, n)
    m = ((ri <= ci) if transpose else (ri >= ci)).astype(BF16)
    d = lambda p: lax.dot_general(m, p, _DIMS['nn'], preferred_element_type=F32)
    hi, mid, lo = _split(x, 3)
    return d(hi) + (d(mid) + d(lo))


def _cumsum_k_fwd(x, n):
    return _lower_dot(x, n, False), None


def _cumsum_k_bwd(n, _, ct):
    return (_lower_dot(ct, n, True),)


_cumsum_k.defvjp(_cumsum_k_fwd, _cumsum_k_bwd)


def _rms(x, g):
    return x * lax.rsqrt(jnp.mean(x * x, axis=-1, keepdims=True) + NORM_EPS) * g


DENSE_ROWS = 1024


def _row_loop(n_rows, fn):
    rows = min(DENSE_ROWS, n_rows)

    def step(i, c):
        fn(pl.ds(pl.multiple_of(i * rows, rows), rows))
        return c
    lax.fori_loop(0, n_rows // rows, step, 0)


def ffn_fwd(x, g, wi, wo, name, hosted=()):
    T = x.shape[0]
    nj = D_FF // FF_CHUNK
    gather = ChipGather(list(hosted))
    n = gather.n

    def body(*refs):
        x_ref, g_ref, wg_ref, wu_ref, wo_ref = refs[:5]
        hx, o_ref, ho = refs[5:5 + n], refs[5 + n], refs[6 + n:6 + 2 * n]
        h_ref, acc_ref = refs[6 + 2 * n:8 + 2 * n]
        sems = refs[8 + 2 * n:]
        j = pl.program_id(0)

        @pl.when(j == 0)
        def _():
            gather.start(hx, ho, sems)

            def init(r):
                h_ref[r, :] = _rms(x_ref[r, :], g_ref[...]).astype(BF16)
                acc_ref[r, :] = jnp.zeros((r.size, D_MODEL), F32)
            _row_loop(T, init)

        def blk(r):
            hb = h_ref[r, :]
            gate = jnp.dot(hb, wg_ref[...], preferred_element_type=F32)
            up = jnp.dot(hb, wu_ref[...], preferred_element_type=F32)
            a = (gate * jax.nn.sigmoid(gate) * up).astype(BF16)
            acc_ref[r, :] += jnp.dot(a, wo_ref[...], preferred_element_type=F32)
        _row_loop(T, blk)

        @pl.when(j == nj // 2)
        def _():
            gather.relay(hx, ho, sems)

        @pl.when(j == nj - 1)
        def _():
            def fin(r):
                o_ref[r, :] = x_ref[r, :] + 0.5 * acc_ref[r, :]
            _row_loop(T, fin)
            gather.finish(hx, ho, sems)

    full = pl.BlockSpec((T, D_MODEL), lambda j: (0, 0))
    res = pl.pallas_call(
        body, name=name, grid=(nj,),
        in_specs=[full, pl.BlockSpec((1, D_MODEL), lambda j: (0, 0)),
                  pl.BlockSpec((D_MODEL, FF_CHUNK), lambda j: (0, j)),
                  pl.BlockSpec((D_MODEL, FF_CHUNK), lambda j: (0, j + nj)),
                  pl.BlockSpec((FF_CHUNK, D_MODEL), lambda j: (j, 0))] + gather.in_specs,
        out_specs=tuple([full] + gather.out_specs),
        out_shape=tuple([jax.ShapeDtypeStruct((T, D_MODEL), F32)] + gather.out_shape),
        scratch_shapes=[pltpu.VMEM((T, D_MODEL), BF16), pltpu.VMEM((T, D_MODEL), F32)] + gather.scratch,
        compiler_params=_params(("arbitrary",)))(x, g, wi, wi, wo, *hosted)
    return res[0], list(res[1:])


def _norm_bwd_rows(x, g, dh, dres):
    rstd = lax.rsqrt(jnp.mean(x * x, axis=-1, keepdims=True) + NORM_EPS)
    xh = x * rstd
    dxh = dh * g
    dx = rstd * (dxh - xh * jnp.mean(dxh * xh, axis=-1, keepdims=True))
    return dres + dx, jnp.sum(dh * xh, axis=0, keepdims=True)


def ffn_bwd(x, dy, g, wi, wo, name):
    T = x.shape[0]
    nj = D_FF // FF_CHUNK

    def body(x_ref, dy_ref, g_ref, wg_ref, wu_ref, wo_ref, dx_ref, dg_ref, dwg_ref, dwu_ref, dwo_ref,
             h_ref, da_ref, dh_ref):
        j = pl.program_id(0)

        @pl.when(j == 0)
        def _():
            def init(r):
                h_ref[r, :] = _rms(x_ref[r, :], g_ref[...]).astype(BF16)
                da_ref[r, :] = (0.5 * dy_ref[r, :]).astype(BF16)
                dh_ref[r, :] = jnp.zeros((r.size, D_MODEL), F32)
            _row_loop(T, init)

        dwg_ref[...] = jnp.zeros_like(dwg_ref)
        dwu_ref[...] = jnp.zeros_like(dwu_ref)
        dwo_ref[...] = jnp.zeros_like(dwo_ref)

        def blk(r):
            hb = h_ref[r, :]
            db = da_ref[r, :]
            gate = jnp.dot(hb, wg_ref[...], preferred_element_type=F32)
            up = jnp.dot(hb, wu_ref[...], preferred_element_type=F32)
            sg = jax.nn.sigmoid(gate)
            sl = gate * sg
            da = _bdot_nt(db, wo_ref[...])
            dup = (da * sl).astype(BF16)
            dgate = (da * up * (sg * (1.0 + gate * (1.0 - sg)))).astype(BF16)
            dwo_ref[...] += _bdot_tn((sl * up).astype(BF16), db)
            dwg_ref[...] += _bdot_tn(hb, dgate)
            dwu_ref[...] += _bdot_tn(hb, dup)
            dh_ref[r, :] += _bdot_nt(dgate, wg_ref[...]) + _bdot_nt(dup, wu_ref[...])
        _row_loop(T, blk)

        @pl.when(j == nj - 1)
        def _():
            dg_ref[...] = jnp.zeros_like(dg_ref)

            def fin(r):
                dx, dg = _norm_bwd_rows(x_ref[r, :], g_ref[...], dh_ref[r, :], dy_ref[r, :])
                dx_ref[r, :] = dx
                dg_ref[...] += dg
            _row_loop(T, fin)

    full = pl.BlockSpec((T, D_MODEL), lambda j: (0, 0))
    vec = pl.BlockSpec((1, D_MODEL), lambda j: (0, 0))
    return pl.pallas_call(
        body, name=name, grid=(nj,),
        in_specs=[full, full, vec,
                  pl.BlockSpec((D_MODEL, FF_CHUNK), lambda j: (0, j)),
                  pl.BlockSpec((D_MODEL, FF_CHUNK), lambda j: (0, j + nj)),
                  pl.BlockSpec((FF_CHUNK, D_MODEL), lambda j: (j, 0))],
        out_specs=(full, vec,
                   pl.BlockSpec((D_MODEL, FF_CHUNK), lambda j: (0, j)),
                   pl.BlockSpec((D_MODEL, FF_CHUNK), lambda j: (0, j)),
                   pl.BlockSpec((FF_CHUNK, D_MODEL), lambda j: (j, 0))),
        out_shape=(jax.ShapeDtypeStruct((T, D_MODEL), F32), jax.ShapeDtypeStruct((1, D_MODEL), F32),
                   jax.ShapeDtypeStruct((D_MODEL, D_FF), F32), jax.ShapeDtypeStruct((D_MODEL, D_FF), F32),
                   jax.ShapeDtypeStruct((D_FF, D_MODEL), F32)),
        scratch_shapes=[pltpu.VMEM((T, D_MODEL), BF16), pltpu.VMEM((T, D_MODEL), BF16),
                        pltpu.VMEM((T, D_MODEL), F32)],
        compiler_params=_params(("arbitrary",)))(x, dy, g, wi, wi, wo)


def proj_fwd(x, g, w, name):
    T = x.shape[0]
    nj = D_IN_PAD // IN_CHUNK

    def body(x_ref, g_ref, w_ref, o_ref, h_ref):
        @pl.when(pl.program_id(0) == 0)
        def _():
            def init(r):
                h_ref[r, :] = _rms(x_ref[r, :], g_ref[...]).astype(BF16)
            _row_loop(T, init)

        def blk(r):
            o_ref[r, :] = jnp.dot(h_ref[r, :], w_ref[...], preferred_element_type=F32)
        _row_loop(T, blk)

    return pl.pallas_call(
        body, name=name, grid=(nj,),
        in_specs=[pl.BlockSpec((T, D_MODEL), lambda j: (0, 0)), pl.BlockSpec((1, D_MODEL), lambda j: (0, 0)),
                  pl.BlockSpec((D_MODEL, IN_CHUNK), lambda j: (0, j))],
        out_specs=pl.BlockSpec((T, IN_CHUNK), lambda j: (0, j)),
        out_shape=jax.ShapeDtypeStruct((T, D_IN_PAD), F32),
        scratch_shapes=[pltpu.VMEM((T, D_MODEL), BF16)],
        compiler_params=_params(("arbitrary",)))(x, g, w)


def proj_bwd(x, dres, g, w, dp, name):
    T = x.shape[0]
    nj = D_IN_PAD // IN_CHUNK

    def body(x_ref, dres_ref, g_ref, w_ref, dp_ref, dx_ref, dg_ref, dw_ref, h_ref, dh_ref):
        j = pl.program_id(0)

        @pl.when(j == 0)
        def _():
            def init(r):
                h_ref[r, :] = _rms(x_ref[r, :], g_ref[...]).astype(BF16)
                dh_ref[r, :] = jnp.zeros((r.size, D_MODEL), F32)
            _row_loop(T, init)

        dw_ref[...] = jnp.zeros_like(dw_ref)

        def blk(r):
            dpb = dp_ref[r, :].astype(BF16)
            dw_ref[...] += _bdot_tn(h_ref[r, :], dpb)
            dh_ref[r, :] += _bdot_nt(dpb, w_ref[...])
        _row_loop(T, blk)

        @pl.when(j == nj - 1)
        def _():
            dg_ref[...] = jnp.zeros_like(dg_ref)

            def fin(r):
                dx, dg = _norm_bwd_rows(x_ref[r, :], g_ref[...], dh_ref[r, :], dres_ref[r, :])
                dx_ref[r, :] = dx
                dg_ref[...] += dg
            _row_loop(T, fin)

    full = pl.BlockSpec((T, D_MODEL), lambda j: (0, 0))
    vec = pl.BlockSpec((1, D_MODEL), lambda j: (0, 0))
    return pl.pallas_call(
        body, name=name, grid=(nj,),
        in_specs=[full, full, vec, pl.BlockSpec((D_MODEL, IN_CHUNK), lambda j: (0, j)),
                  pl.BlockSpec((T, IN_CHUNK), lambda j: (0, j))],
        out_specs=(full, vec, pl.BlockSpec((D_MODEL, IN_CHUNK), lambda j: (0, j))),
        out_shape=(jax.ShapeDtypeStruct((T, D_MODEL), F32), jax.ShapeDtypeStruct((1, D_MODEL), F32),
                   jax.ShapeDtypeStruct((D_MODEL, D_IN_PAD), F32)),
        scratch_shapes=[pltpu.VMEM((T, D_MODEL), BF16), pltpu.VMEM((T, D_MODEL), F32)],
        compiler_params=_params(("arbitrary",)))(x, dres, g, w, dp)


def out_fwd(mixed, w, x, name):
    T = x.shape[0]

    def body(m_ref, w_ref, x_ref, o_ref):
        o_ref[...] = x_ref[...] + jnp.dot(m_ref[...].astype(BF16), w_ref[...], preferred_element_type=F32)

    blk = pl.BlockSpec((ROWS, D_MODEL), lambda i: (i, 0))
    return pl.pallas_call(
        body, name=name, grid=(T // ROWS,),
        in_specs=[blk, pl.BlockSpec((D_MODEL, D_MODEL), lambda i: (0, 0)), blk],
        out_specs=blk, out_shape=jax.ShapeDtypeStruct((T, D_MODEL), F32),
        compiler_params=_params(("arbitrary",)))(mixed, w, x)


def out_bwd(mixed, w, dy, name):
    T = dy.shape[0]

    def body(m_ref, w_ref, dy_ref, dm_ref, dw_ref):
        @pl.when(pl.program_id(0) == 0)
        def _():
            dw_ref[...] = jnp.zeros_like(dw_ref)
        dyb = dy_ref[...].astype(BF16)
        dm_ref[...] = _bdot_nt(dyb, w_ref[...])
        dw_ref[...] += _bdot_tn(m_ref[...].astype(BF16), dyb)

    blk = pl.BlockSpec((ROWS, D_MODEL), lambda i: (i, 0))
    sq = pl.BlockSpec((D_MODEL, D_MODEL), lambda i: (0, 0))
    return pl.pallas_call(
        body, name=name, grid=(T // ROWS,),
        in_specs=[blk, sq, blk], out_specs=(blk, sq),
        out_shape=(jax.ShapeDtypeStruct((T, D_MODEL), F32), jax.ShapeDtypeStruct((D_MODEL, D_MODEL), F32)),
        compiler_params=_params(("arbitrary",)))(mixed, w, dy)


def loss_head(x, g, target, name):
    T = x.shape[0]

    def body(x_ref, g_ref, t_ref, loss_ref, dx_ref, dg_ref):
        @pl.when(pl.program_id(0) == 0)
        def _():
            loss_ref[...] = jnp.zeros_like(loss_ref)
            dg_ref[...] = jnp.zeros_like(dg_ref)
        xb = x_ref[...]
        rstd = lax.rsqrt(jnp.mean(xb * xb, axis=-1, keepdims=True) + NORM_EPS)
        xh = xb * rstd
        err = xh * g_ref[...] - t_ref[...]
        loss_ref[...] += 0.5 * jnp.sum(jnp.mean(err * err, axis=-1, keepdims=True), axis=0, keepdims=True)
        dy = err * (1.0 / D_MODEL)
        dg_ref[...] += jnp.sum(dy * xh, axis=0, keepdims=True)
        dxh = dy * g_ref[...]
        dx_ref[...] = rstd * (dxh - xh * jnp.mean(dxh * xh, axis=-1, keepdims=True))

    blk = pl.BlockSpec((ROWS, D_MODEL), lambda i: (i, 0))
    vec = pl.BlockSpec((1, D_MODEL), lambda i: (0, 0))
    return pl.pallas_call(
        body, name=name, grid=(T // ROWS,),
        in_specs=[blk, vec, blk], out_specs=(pl.BlockSpec((1, 1), lambda i: (0, 0)), blk, vec),
        out_shape=(jax.ShapeDtypeStruct((1, 1), F32), jax.ShapeDtypeStruct((T, D_MODEL), F32),
                   jax.ShapeDtypeStruct((1, D_MODEL), F32)),
        compiler_params=_params(("arbitrary",)))(x, g, target)


def rowwise_fwd(fn, rows, shared, out_widths, name):
    T = rows[0].shape[0]
    n_in = len(rows) + len(shared)

    def body(*refs):
        res = fn(*[r[...] for r in refs[:n_in]])
        for o, v in zip(refs[n_in:], res):
            o[...] = v

    in_specs = ([pl.BlockSpec((ROWS, a.shape[1]), lambda i: (i, 0)) for a in rows]
                + [pl.BlockSpec(a.shape, lambda i: (0, 0)) for a in shared])
    return pl.pallas_call(
        body, name=name, grid=(T // ROWS,), in_specs=in_specs,
        out_specs=tuple(pl.BlockSpec((ROWS, w), lambda i: (i, 0)) for w in out_widths),
        out_shape=tuple(jax.ShapeDtypeStruct((T, w), F32) for w in out_widths),
        compiler_params=_params(("arbitrary",)))(*rows, *shared)


def rowwise_bwd(fn, rows, shared, cts, name, ct_fn=None):
    T = rows[0].shape[0]
    nr, ns, nc = len(rows), len(shared), len(cts)

    def body(*refs):
        ins = [r[...] for r in refs[:nr + ns]]
        ctv = tuple(r[...] for r in refs[nr + ns:nr + ns + nc])
        outs = refs[nr + ns + nc:]
        _, vjp = jax.vjp(fn, *ins)
        grads = vjp(ct_fn(*ctv) if ct_fn is not None else ctv)
        for k in range(nr):
            outs[k][...] = grads[k]

        @pl.when(pl.program_id(0) == 0)
        def _():
            for k in range(ns):
                outs[nr + k][...] = jnp.zeros_like(outs[nr + k])
        for k in range(ns):
            outs[nr + k][...] += grads[nr + k]

    row_spec = lambda a: pl.BlockSpec((ROWS, a.shape[1]), lambda i: (i, 0))
    sh_spec = lambda a: pl.BlockSpec(a.shape, lambda i: (0, 0))
    return pl.pallas_call(
        body, name=name, grid=(T // ROWS,),
        in_specs=[row_spec(a) for a in rows] + [sh_spec(a) for a in shared] + [row_spec(a) for a in cts],
        out_specs=tuple([row_spec(a) for a in rows] + [sh_spec(a) for a in shared]),
        out_shape=tuple(jax.ShapeDtypeStruct(a.shape, F32) for a in list(rows) + list(shared)),
        compiler_params=_params(("arbitrary",)))(*rows, *shared, *cts)


def shift_rows(x, s):
    return jnp.pad(x, ((s, 0), (0, 0)))[:x.shape[0]]


def unshift_rows(x, s):
    return jnp.pad(x, ((0, s), (0, 0)))[s:]


def _neg_expm1(y):
    series = -(y * (1.0 + y * (0.5 + y * (1.0 / 6.0 + y * (1.0 / 24.0)))))
    return jnp.where(y > -0.05, series, 1.0 - jnp.exp(y))


def lru_pre_fn(x0, x1, x2, x3, first, w0, w1, w2, w3, cb, ga, gab, gx, gxb, lam):
    xc = w3 * x0 + w2 * x1 + w1 * x2 + w0 * x3 + cb
    r = jax.nn.sigmoid(_hdot(xc, ga) + gab)
    i = jax.nn.sigmoid(_hdot(xc, gx) + gxb)
    log_a = -LRU_C * r * jax.nn.softplus(-lam)
    a = jnp.exp(log_a)
    mult = jnp.where(first > 0.5, 1.0, jnp.sqrt(_neg_expm1(2.0 * log_a)))
    return a, mult * i * xc


def lru_post_fn(h, py, og):
    return (_rms(h * jax.nn.gelu(py), og),)


def lru_scan(a, b, reverse, name):
    T, C = a.shape
    nb = T // 8

    def body(a_ref, b_ref, h_ref):
        rows = lax.broadcasted_iota(jnp.int32, (8, C), 0)

        def blk(i, carry):
            j = nb - 1 - i if reverse else i
            r = pl.ds(pl.multiple_of(j * 8, 8), 8)
            A = a_ref[r, :]
            B = b_ref[r, :]
            for s in (1, 2, 4):
                if reverse:
                    keep = rows < 8 - s
                    sh = 8 - s
                else:
                    keep = rows >= s
                    sh = s
                Bs = jnp.where(keep, pltpu.roll(B, sh, 0), 0.0)
                As = jnp.where(keep, pltpu.roll(A, sh, 0), 1.0)
                B = B + A * Bs
                A = A * As
            hb = B + A * carry
            h_ref[r, :] = hb
            edge = 0 if reverse else 7
            return jnp.sum(jnp.where(rows == edge, hb, 0.0), axis=0, keepdims=True)

        lax.fori_loop(0, nb, blk, jnp.zeros((1, C), F32))

    full = pl.BlockSpec((T, C), lambda: (0, 0))
    return pl.pallas_call(body, name=name, in_specs=[full, full], out_specs=full,
                          out_shape=jax.ShapeDtypeStruct((T, C), F32), compiler_params=_params())(a, b)


def make_rwkv_pre_fn(has_vres):
    def fn(p, pp, *rest):
        if has_vres:
            vf, mu, w_up, w_b, a_up, a_b, g_up, kk_w, ka_w, vw1, vw2, vb = rest
        else:
            mu, w_up, w_b, a_up, a_b, g_up, kk_w, ka_w = rest
        xm = p + (pp - p) * mu
        r, k, v = xm[:, 0:384], xm[:, 384:768], xm[:, 768:1152]
        xw, xa, xg = xm[:, 1152:1216], xm[:, 1216:1280], xm[:, 1280:1408]
        w_log = -jax.nn.softplus(-(w_b + _hdot(jnp.tanh(xw), w_up))) - 0.5
        lw = -jnp.exp(w_log)
        a = jax.nn.sigmoid(a_b + _hdot(xa, a_up))
        g = _hdot(jax.nn.sigmoid(xg), g_up)
        if has_vres:
            v = v + (vf - v) * jax.nn.sigmoid(vb + _hdot(_hdot(v, vw1), vw2))
        kkx = k * kk_w
        kk = kkx * lax.rsqrt(_segsum(kkx * kkx) + 1e-6)
        k2 = k * (1.0 + (a - 1.0) * ka_w)
        return r, lw, k2, v, kk, a, g
    return fn


def rwkv_post_fn(y, r, k2, v, g, ln_g, ln_b, r_k):
    mean = _segsum(y) * (1.0 / HEAD_DIM)
    yc = y - mean
    var = _segsum(yc * yc) * (1.0 / HEAD_DIM)
    yn = yc * lax.rsqrt(var + GN_EPS) * ln_g + ln_b
    bonus = _segsum(r * k2 * r_k) * v
    return ((yn + bonus) * g,)


def _head_expander(first_lane):
    ri, ci = _iota2(128, MIX_W)
    return (ri == ci // HEAD_DIM + first_lane).astype(F32)


def gdn_pre_fn(x0, x1, x2, x3, ab, w0, w1, w2, w3, alog, dtb):
    qkv = jax.nn.silu(w3 * x0 + w2 * x1 + w1 * x2 + w0 * x3)
    q, k, v = qkv[:, 0:384], qkv[:, 384:768], qkv[:, 768:1152]
    q = q * lax.rsqrt(_segsum(q * q) + 1e-6) * (HEAD_DIM ** -0.5)
    k = k * lax.rsqrt(_segsum(k * k) + 1e-6)
    g = -jnp.exp(alog) * jax.nn.softplus(ab + dtb)
    beta = jax.nn.sigmoid(ab)
    ge = _xdot(g, functools.partial(_head_expander, 0))
    be = _xdot(beta, functools.partial(_head_expander, HEADS))
    return q, k, v, ge, be


def gdn_post_fn(o, z, ng):
    ms = _segsum(o * o) * (1.0 / HEAD_DIM)
    return (o * lax.rsqrt(ms + NORM_EPS) * ng * jax.nn.silu(z),)


def _neumann_inv(m):
    n = m.shape[-1]
    ri, ci = _iota2(n, n)
    eye = (ri == ci).astype(F32)
    md = jnp.where(ri // 16 == ci // 16, m, 0.0)
    mo = m - md
    t0 = eye + md
    p2 = _hdot(md, md)
    t0 = t0 + _hdot(t0, p2)
    p4 = _hdot(p2, p2)
    t0 = t0 + _hdot(t0, p4)
    p8 = _hdot(p4, p4)
    t0 = t0 + _hdot(t0, p8)
    nn = _hdot(t0, mo)
    n2 = _hdot(nn, nn)
    t1 = eye + nn + n2 + _hdot(nn, n2)
    return _hdot(t1, t0)


@jax.custom_vjp
def _inv_saved(m, t_saved):
    return t_saved


def _inv_saved_fwd(m, t_saved):
    return t_saved, t_saved


def _inv_saved_bwd(t_saved, dt):
    tt = jnp.swapaxes(t_saved, -1, -2)
    return _hdot(_hdot(tt, dt), tt), jnp.zeros_like(t_saved)


_inv_saved.defvjp(_inv_saved_fwd, _inv_saved_bwd)


def _heads(x):
    return jnp.concatenate([x[None, :, h * HEAD_DIM:(h + 1) * HEAD_DIM] for h in range(HEADS)], axis=0)


def _unheads(y):
    return jnp.concatenate([lax.index_in_dim(y, h, 0, keepdims=False) for h in range(HEADS)], axis=1)


def rwkv_heads(s0, r, lw, k2, v, kk, a, inv):
    n = r.shape[0]
    ri, ci = _iota2(n, n)
    low, strict = ri >= ci, ri > ci
    cs = _cumsum_rows(lw)
    cl = jnp.sum(lw, axis=0, keepdims=True)
    p_in, p_prev, p_inv = jnp.exp(cs), jnp.exp(cs - lw), jnp.exp(-cs)
    p_rest, p_all = jnp.exp(cl - cs), jnp.exp(cl)
    bd = kk * a
    at, rt = _heads(-kk * p_prev), _heads(r * p_in)
    bh, kh = _heads(bd * p_inv), _heads(k2 * p_inv)
    vh = _heads(v)
    m_ab = jnp.where(strict, _cdot_nt(at, bh), 0.0)
    m_ak = jnp.where(strict, _cdot_nt(at, kh), 0.0)
    m_rb = jnp.where(low, _cdot_nt(rt, bh), 0.0)
    m_rk = jnp.where(low, _cdot_nt(rt, kh), 0.0)
    sa = _cdot(inv(m_ab), _cdot_nt(at, s0) + _cdot(m_ak, vh))
    y = _cdot_nt(rt, s0) + _cdot(m_rb, sa) + _cdot(m_rk, vh)
    s1 = s0 * _heads(p_all) + _cdot_tn(sa, _heads(bd * p_rest)) + _cdot_tn(vh, _heads(k2 * p_rest))
    return _unheads(y), s1


def gdn_heads(s0, q, k, v, ge, be, inv):
    n = q.shape[0]
    ri, ci = _iota2(n, n)
    low, strict = ri >= ci, ri > ci
    gc = _cumsum_rows(ge)
    gl = jnp.sum(ge, axis=0, keepdims=True)
    gch = _heads(gc)
    decay = jnp.where(low, jnp.exp(jnp.where(low, gch - jnp.swapaxes(gch, 1, 2), 0.0)), 0.0)
    kb = k * be
    e = jnp.exp(gc)
    kh = _heads(k)
    m = -jnp.where(strict, _cdot_nt(_heads(kb), kh) * decay, 0.0)
    mr = jnp.where(low, _cdot_nt(_heads(q), kh) * decay, 0.0)
    u = _cdot(inv(m), _heads(v * be) - _cdot_nt(_heads(kb * e), s0))
    y = _cdot_nt(_heads(q * e), s0) + _cdot(mr, u)
    s1 = s0 * _heads(jnp.exp(gl)) + _cdot_tn(u, _heads(k * jnp.exp(gl - gc)))
    return _unheads(y), s1


def core_fwd(heads_fn, ins, name, hosted=()):
    T = ins[0].shape[0]
    nc = T // CHUNK
    n = len(ins)
    gather = ChipGather(list(hosted))
    ng = gather.n

    def body(*refs):
        hx = refs[n:n + ng]
        y_ref, s0_ref, t_ref = refs[n + ng:n + ng + 3]
        ho = refs[n + ng + 3:n + 2 * ng + 3]
        s_ref = refs[n + 2 * ng + 3]
        sems = refs[n + 2 * ng + 4:]
        c = pl.program_id(0)

        @pl.when(c == 0)
        def _():
            gather.start(hx, ho, sems)
            s_ref[...] = jnp.zeros_like(s_ref)

        s0 = s_ref[...]
        kept = []

        def inv(m):
            kept.append(_neumann_inv(m))
            return kept[0]

        y, s1 = heads_fn(s0, *[r[...] for r in refs[:n]], inv)
        y_ref[...] = y
        s0_ref[0] = s0
        t_ref[0] = kept[0]
        s_ref[...] = s1

        @pl.when(c == nc // 2)
        def _():
            gather.relay(hx, ho, sems)

        @pl.when(c == nc - 1)
        def _():
            gather.finish(hx, ho, sems)

    row = pl.BlockSpec((CHUNK, MIX_W), lambda c: (c, 0))
    st_shape = (HEADS, HEAD_DIM, HEAD_DIM)
    st = pl.BlockSpec((1,) + st_shape, lambda c: (c, 0, 0, 0))
    res = pl.pallas_call(
        body, name=name, grid=(nc,), in_specs=[row] * n + gather.in_specs,
        out_specs=tuple([row, st, st] + gather.out_specs),
        out_shape=tuple([jax.ShapeDtypeStruct((T, MIX_W), F32), jax.ShapeDtypeStruct((nc,) + st_shape, F32),
                         jax.ShapeDtypeStruct((nc,) + st_shape, F32)] + gather.out_shape),
        scratch_shapes=[pltpu.VMEM(st_shape, F32)] + gather.scratch,
        compiler_params=_params(("arbitrary",)))(*ins, *hosted)
    return res[0], res[1], res[2], list(res[3:])


def core_bwd(heads_fn, ins, s0_all, t_all, dy, name):
    T = ins[0].shape[0]
    nc = T // CHUNK
    n = len(ins)

    def body(*refs):
        s0_ref, t_ref, dy_ref = refs[n:n + 3]
        outs = refs[n + 3:n + 3 + n]
        ds_ref = refs[n + 3 + n]

        @pl.when(pl.program_id(0) == 0)
        def _():
            ds_ref[...] = jnp.zeros_like(ds_ref)

        t_saved = t_ref[0]
        f = lambda s0, *xs: heads_fn(s0, *xs, lambda m: _inv_saved(m, t_saved))
        _, vjp = jax.vjp(f, s0_ref[0], *[r[...] for r in refs[:n]])
        grads = vjp((dy_ref[...], ds_ref[...]))
        ds_ref[...] = grads[0]
        for k in range(n):
            outs[k][...] = grads[1 + k]

    row = pl.BlockSpec((CHUNK, MIX_W), lambda c: (nc - 1 - c, 0))
    st_shape = (HEADS, HEAD_DIM, HEAD_DIM)
    st = pl.BlockSpec((1,) + st_shape, lambda c: (nc - 1 - c, 0, 0, 0))
    return pl.pallas_call(
        body, name=name, grid=(nc,), in_specs=[row] * n + [st, st, row], out_specs=tuple([row] * n),
        out_shape=tuple(jax.ShapeDtypeStruct((T, MIX_W), F32) for _ in range(n)),
        scratch_shapes=[pltpu.VMEM(st_shape, F32)],
        compiler_params=_params(("arbitrary",)))(*ins, s0_all, t_all, dy)


def _block_diag(w):
    out = jnp.zeros((LRU_W, LRU_W), w.dtype)
    for n in range(LRU_BLOCKS):
        out = lax.dynamic_update_slice(out, w[n], (n * 64, n * 64))
    return out


def _block_diag_grad(g):
    return jnp.stack([g[n * 64:(n + 1) * 64, n * 64:(n + 1) * 64] for n in range(LRU_BLOCKS)])


def _row(v):
    return v.reshape(1, -1)


def _pad128(v):
    return jnp.pad(v.reshape(1, -1), ((0, 0), (0, 128 - v.size)))


def _layer_shared(w, l):
    cw = w['lru_conv_w'][l]
    lru_pre = [_row(cw[0]), _row(cw[1]), _row(cw[2]), _row(cw[3]), _row(w['lru_conv_b'][l]),
               _block_diag(w['lru_gate_a_w'][l]), _row(w['lru_gate_a_b'][l]),
               _block_diag(w['lru_gate_x_w'][l]), _row(w['lru_gate_x_b'][l]), _row(w['lru_lambda'][l])]
    rw_pre = [_row(w['rwkv_mu'][l]), w['rwkv_w_up'][l], _row(w['rwkv_w_bias'][l]), w['rwkv_a_up'][l],
              _row(w['rwkv_a_bias'][l]), w['rwkv_g_up'][l], _row(w['rwkv_k_k'][l]), _row(w['rwkv_k_a'][l])]
    if l > 0:
        rw_pre += [w['rwkv_vres_w1'][l - 1], w['rwkv_vres_w2'][l - 1], _row(w['rwkv_vres_b'][l - 1])]
    rw_post = [_row(w['rwkv_ln_g'][l]), _row(w['rwkv_ln_b'][l]), _row(w['rwkv_r_k'][l])]
    gw = w['gdn_conv_w'][l]
    gdn_pre = [_row(gw[0]), _row(gw[1]), _row(gw[2]), _row(gw[3]), _pad128(w['gdn_a_log'][l]),
               _pad128(w['gdn_dt_bias'][l])]
    gdn_post = [_row(jnp.tile(w['gdn_norm'][l], HEADS))]
    return dict(lru_pre=lru_pre, lru_post=[_row(w['lru_out_norm'][l])], rw_pre=rw_pre, rw_post=rw_post,
                gdn_pre=gdn_pre, gdn_post=gdn_post)


def _mixer_fwd(p, sh, l, v_first, host_rwkv=(), host_gdn=()):
    T = p.shape[0]
    lx, ly = p[:, 0:256], p[:, 256:512]
    prw, qkv, z, ab = p[:, 512:1920], p[:, 1920:3072], p[:, 3072:3456], p[:, 3456:3584]
    first = jnp.zeros((T, LRU_W), F32).at[0].set(1.0)
    lru_rows = [lx, shift_rows(lx, 1), shift_rows(lx, 2), shift_rows(lx, 3), first]
    a, b = rowwise_fwd(lru_pre_fn, lru_rows, sh['lru_pre'], (LRU_W, LRU_W), f"lru_pre_fwd{l}")
    hseq = lru_scan(a, b, False, f"lru_scan_fwd{l}")
    (y_lru,) = rowwise_fwd(lru_post_fn, [hseq, ly], sh['lru_post'], (LRU_W,), f"lru_post_fwd{l}")

    rw_rows = [prw, shift_rows(prw, 1)] + ([v_first] if l > 0 else [])
    rw = rowwise_fwd(make_rwkv_pre_fn(l > 0), rw_rows, sh['rw_pre'], (MIX_W,) * 7, f"rwkv_pre_fwd{l}")
    r, lw, k2, v, kk, ar, g = rw
    y_raw, rs0, rt, got_rwkv = core_fwd(rwkv_heads, [r, lw, k2, v, kk, ar], f"rwkv_core_fwd{l}", host_rwkv)
    (y_rw,) = rowwise_fwd(rwkv_post_fn, [y_raw, r, k2, v, g], sh['rw_post'], (MIX_W,), f"rwkv_post_fwd{l}")

    gdn_rows = [qkv, shift_rows(qkv, 1), shift_rows(qkv, 2), shift_rows(qkv, 3), ab]
    gd = rowwise_fwd(gdn_pre_fn, gdn_rows, sh['gdn_pre'], (MIX_W,) * 5, f"gdn_pre_fwd{l}")
    o_raw, gs0, gt, got_gdn = core_fwd(gdn_heads, list(gd), f"gdn_core_fwd{l}", host_gdn)
    (y_gdn,) = rowwise_fwd(gdn_post_fn, [o_raw, z], sh['gdn_post'], (MIX_W,), f"gdn_post_fwd{l}")

    mixed = jnp.concatenate([y_lru, y_rw, y_gdn], axis=1)
    saved = dict(lru_rows=lru_rows, a=a, hseq=hseq, ly=ly, rw_rows=rw_rows, rw=rw, y_raw=y_raw, rs0=rs0, rt=rt,
                 gdn_rows=gdn_rows, gd=gd, o_raw=o_raw, gs0=gs0, gt=gt, z=z)
    v_layer0 = v if l == 0 else None
    return mixed, saved, v_layer0, got_rwkv, got_gdn


def _mixer_bwd(dmixed, sv, sh, l, dv_first):
    d_lru, d_rw, d_gdn = dmixed[:, 0:256], dmixed[:, 256:640], dmixed[:, 640:1024]
    gw = {}

    dh, dly, d_og = rowwise_bwd(lru_post_fn, [sv['hseq'], sv['ly']], sh['lru_post'], [d_lru], f"lru_post_bwd{l}")
    gscan = lru_scan(unshift_rows(sv['a'], 1), dh, True, f"lru_scan_bwd{l}")
    res = rowwise_bwd(lru_pre_fn, sv['lru_rows'], sh['lru_pre'], [gscan, shift_rows(sv['hseq'], 1)],
                      f"lru_pre_bwd{l}", ct_fn=lambda gs, hp: (gs * hp, gs))
    dlx = res[0] + unshift_rows(res[1], 1) + unshift_rows(res[2], 2) + unshift_rows(res[3], 3)
    dw0, dw1, dw2, dw3, dcb, dga, dgab, dgx, dgxb, dlam = res[5:]
    gw['lru_conv_w'] = jnp.concatenate([dw0, dw1, dw2, dw3], axis=0)
    gw['lru_conv_b'] = dcb[0]
    gw['lru_gate_a_w'] = _block_diag_grad(dga)
    gw['lru_gate_a_b'] = dgab.reshape(LRU_BLOCKS, 64)
    gw['lru_gate_x_w'] = _block_diag_grad(dgx)
    gw['lru_gate_x_b'] = dgxb.reshape(LRU_BLOCKS, 64)
    gw['lru_lambda'] = dlam[0]
    gw['lru_out_norm'] = d_og[0]

    r, lw, k2, v, kk, ar, g = sv['rw']
    res = rowwise_bwd(rwkv_post_fn, [sv['y_raw'], r, k2, v, g], sh['rw_post'], [d_rw], f"rwkv_post_bwd{l}")
    dy_raw, dr_p, dk2_p, dv_p, dg = res[:5]
    gw['rwkv_ln_g'], gw['rwkv_ln_b'], gw['rwkv_r_k'] = res[5][0], res[6][0], res[7].reshape(HEADS, HEAD_DIM)
    dr_c, dlw, dk2_c, dv_c, dkk, dar = core_bwd(rwkv_heads, [r, lw, k2, v, kk, ar], sv['rs0'], sv['rt'], dy_raw,
                                                 f"rwkv_core_bwd{l}")
    cts = [dr_p, dr_c, dlw, dk2_p, dk2_c, dv_p, dv_c, dkk, dar, dg]
    if l == 0:
        cts.append(dv_first)
        ct_fn = lambda a1, a2, b, c1, c2, d1, d2, e, f, gg, vf: (a1 + a2, b, c1 + c2, d1 + d2 + vf, e, f, gg)
    else:
        ct_fn = lambda a1, a2, b, c1, c2, d1, d2, e, f, gg: (a1 + a2, b, c1 + c2, d1 + d2, e, f, gg)
    res = rowwise_bwd(make_rwkv_pre_fn(l > 0), sv['rw_rows'], sh['rw_pre'], cts, f"rwkv_pre_bwd{l}", ct_fn=ct_fn)
    dprw = res[0] + unshift_rows(res[1], 1)
    nrow = len(sv['rw_rows'])
    dv_first_out = res[2] if l > 0 else None
    sg = res[nrow:]
    gw['rwkv_mu'], gw['rwkv_w_up'], gw['rwkv_w_bias'], gw['rwkv_a_up'] = sg[0][0], sg[1], sg[2][0], sg[3]
    gw['rwkv_a_bias'], gw['rwkv_g_up'], gw['rwkv_k_k'], gw['rwkv_k_a'] = sg[4][0], sg[5], sg[6][0], sg[7][0]
    if l > 0:
        gw['rwkv_vres_w1'], gw['rwkv_vres_w2'], gw['rwkv_vres_b'] = sg[8], sg[9], sg[10][0]

    do_raw, dz, d_ng = rowwise_bwd(gdn_post_fn, [sv['o_raw'], sv['z']], sh['gdn_post'], [d_gdn], f"gdn_post_bwd{l}")
    gw['gdn_norm'] = jnp.sum(d_ng.reshape(HEADS, HEAD_DIM), axis=0)
    dgd = core_bwd(gdn_heads, list(sv['gd']), sv['gs0'], sv['gt'], do_raw, f"gdn_core_bwd{l}")
    res = rowwise_bwd(gdn_pre_fn, sv['gdn_rows'], sh['gdn_pre'], list(dgd), f"gdn_pre_bwd{l}")
    dqkv = res[0] + unshift_rows(res[1], 1) + unshift_rows(res[2], 2) + unshift_rows(res[3], 3)
    dab = res[4]
    gw['gdn_conv_w'] = jnp.concatenate(res[5:9], axis=0)
    gw['gdn_a_log'], gw['gdn_dt_bias'] = res[9][0, :HEADS], res[10][0, :HEADS]

    dp = jnp.concatenate([dlx, dly, dprw, dqkv, dz, dab], axis=1)
    return dp, gw, dv_first_out


IN_SHARD = D_IN // N_CHIPS
IN_SHARD_PAD = D_IN_PAD // N_CHIPS


def _cols_to_chips(g, n=N_CHIPS):
    r = g.shape[0]
    return jnp.transpose(g.reshape(r, n, -1), (1, 0, 2))


def _cols_from_chips(g):
    return jnp.transpose(g, (1, 0, 2)).reshape(g.shape[1], -1)


def _w_in_from_chips(g):
    nat = _cols_from_chips(g[:, :, :IN_SHARD])
    return jnp.pad(nat, ((0, 0), (0, D_IN_PAD - D_IN)))


def _w_in_to_chips(g):
    return jnp.pad(_cols_to_chips(g[:, :D_IN]), ((0, 0), (0, 0), (0, IN_SHARD_PAD - IN_SHARD)))


def _natural(name, g):
    if name == 'w_in':
        return _w_in_from_chips(g)
    if BIG[name] == 2:
        return _cols_from_chips(g)
    return g.reshape(-1, g.shape[2])


def local_step(x, target, w, wb, shards=None):
    def hosted(keys):
        return [shards[k] for k in keys] if shards is not None else []

    def arrived(keys, gathered):
        for (name, layer), g in zip(keys if shards is not None else [], gathered):
            wb[name][layer] = _natural(name, g)

    saved = []
    v_first = None
    for l in range(N_LAYERS):
        sh = _layer_shared(w, l)
        for_mixer = [('w_in', l), ('w_out', l)]
        for_ffn2 = [('ffn2_wi', l), ('ffn2_wo', l)]
        for_next = [('ffn1_wi', l + 1), ('ffn1_wo', l + 1)] if l + 1 < N_LAYERS else []
        x1, got = ffn_fwd(x, _row(w['ffn1_norm'][l]), wb['ffn1_wi'][l], wb['ffn1_wo'][l], f"ffn1_fwd{l}",
                          hosted(for_mixer))
        arrived(for_mixer, got)
        p = proj_fwd(x1, _row(w['mix_norm'][l]), wb['w_in'][l], f"proj_fwd{l}")
        mixed, sv, v0, got_ffn2, got_next = _mixer_fwd(p, sh, l, v_first, hosted(for_ffn2), hosted(for_next))
        arrived(for_ffn2, got_ffn2)
        arrived(for_next, got_next)
        if l == 0:
            v_first = v0
        x2 = out_fwd(mixed, wb['w_out'][l], x1, f"out_fwd{l}")
        x3, _ = ffn_fwd(x2, _row(w['ffn2_norm'][l]), wb['ffn2_wi'][l], wb['ffn2_wo'][l], f"ffn2_fwd{l}")
        saved.append(dict(x0=x, x1=x1, x2=x2, mixed=mixed, sv=sv, sh=sh))
        x = x3

    loss, dx, dgf = loss_head(x, _row(w['final_norm']), target, "loss_head")
    per_layer = [None] * N_LAYERS
    dv_first = jnp.zeros((x.shape[0], MIX_W), F32)
    for l in reversed(range(N_LAYERS)):
        s = saved[l]
        gw = {}
        dx, dg2, dwg, dwu, dwo = ffn_bwd(s['x2'], dx, _row(w['ffn2_norm'][l]), wb['ffn2_wi'][l], wb['ffn2_wo'][l],
                                         f"ffn2_bwd{l}")
        wi_parts = lambda dwg, dwu: (dwg, dwu)
        row_parts = lambda dw: dw.reshape(N_CHIPS, -1, dw.shape[1])
        gw['ffn2_norm'], gw['ffn2_wi'], gw['ffn2_wo'] = dg2[0], wi_parts(dwg, dwu), row_parts(dwo)
        dmixed, dw_out = out_bwd(s['mixed'], wb['w_out'][l], dx, f"out_bwd{l}")
        gw['w_out'] = row_parts(dw_out)
        dp, gmix, dvf = _mixer_bwd(dmixed, s['sv'], s['sh'], l, dv_first)
        if l > 0:
            dv_first = dvf
        gw.update(gmix)
        dx, dgm, dwin = proj_bwd(s['x1'], dx, _row(w['mix_norm'][l]), wb['w_in'][l], dp, f"proj_bwd{l}")
        gw['mix_norm'], gw['w_in'] = dgm[0], _w_in_to_chips(dwin)
        dx, dg1, dwg, dwu, dwo = ffn_bwd(s['x0'], dx, _row(w['ffn1_norm'][l]), wb['ffn1_wi'][l], wb['ffn1_wo'][l],
                                         f"ffn1_bwd{l}")
        gw['ffn1_norm'], gw['ffn1_wi'], gw['ffn1_wo'] = dg1[0], wi_parts(dwg, dwu), row_parts(dwo)
        per_layer[l] = gw

    grads = {'final_norm': dgf[0]}
    for name in WEIGHTS:
        if name == 'final_norm':
            continue
        if name in BIG:
            grads[name] = [per_layer[l][name] for l in range(N_LAYERS)]
        elif name.startswith('rwkv_vres'):
            grads[name] = per_layer[1][name][None]
        else:
            grads[name] = jnp.stack([per_layer[l][name] for l in range(N_LAYERS)])
    return loss[0, 0], dx, grads


ANY = pl.BlockSpec(memory_space=pl.ANY)


def _coords():
    return lax.axis_index("x"), lax.axis_index("y"), lax.axis_index("c")


def _other_chips(x, y):
    return [((x + 1) % 2, y), (x, (y + 1) % 2), ((x + 1) % 2, (y + 1) % 2)]


def allreduce_small(pack, name):
    R = pack.shape[0]
    rh = R // 2

    def body(x_ref, o_ref, sib_ref, chip_ref, parts_ref, send_sems, recv_sems):
        x, y, c = _coords()
        sib = (x, y, 1 - c)

        def copy(k, src, dst, to):
            return pltpu.make_async_remote_copy(src_ref=src, dst_ref=dst, send_sem=send_sems.at[k],
                                                recv_sem=recv_sems.at[k], device_id=to, device_id_type=MESH)

        swap = copy(0, x_ref, sib_ref, sib)
        swap.start()
        swap.wait()
        chip_ref[...] = jnp.where(c == 0, x_ref[...], sib_ref[...]) + jnp.where(c == 0, sib_ref[...], x_ref[...])

        mine = pl.ds(pl.multiple_of(c * rh, 8), rh)
        sends = [copy(1 + j, chip_ref.at[mine], parts_ref.at[j], (px, py, c))
                 for j, (px, py) in enumerate(_other_chips(x, y))]
        for cp in sends:
            cp.start()
        for cp in sends:
            cp.wait()
        s = 2 * x + y
        own = chip_ref[mine, :]
        from_chip = {2: parts_ref[0], 1: parts_ref[1], 3: parts_ref[2]}
        terms = []
        for k in range(N_CHIPS):
            t = own
            for d, part in from_chip.items():
                t = jnp.where(jnp.bitwise_xor(s, d) == k, part, t)
            terms.append(t)
        o_ref[mine, :] = ((terms[0] + terms[1]) + terms[2]) + terms[3]

        share = copy(4, o_ref.at[mine], o_ref.at[mine], sib)
        share.start()
        share.wait()

    vm = pl.BlockSpec(memory_space=pltpu.VMEM)
    return pl.pallas_call(
        body, name=name, in_specs=[vm], out_specs=vm, out_shape=jax.ShapeDtypeStruct((R, 128), F32),
        scratch_shapes=[pltpu.VMEM((R, 128), F32), pltpu.VMEM((R, 128), F32), pltpu.VMEM((3, rh, 128), F32),
                        pltpu.SemaphoreType.DMA((5,)), pltpu.SemaphoreType.DMA((5,))],
        compiler_params=_params())(pack)


class ChipGather:
    def __init__(self, shards):
        self.shapes = [s.shape for s in shards]
        self.n = len(shards)
        self.in_specs = [ANY] * self.n
        self.out_specs = [ANY] * self.n
        self.out_shape = [jax.ShapeDtypeStruct((N_CHIPS,) + s.shape, s.dtype) for s in shards]
        self.scratch = [pltpu.SemaphoreType.DMA((6 * self.n,)), pltpu.SemaphoreType.DMA((6 * self.n,)),
                        pltpu.SemaphoreType.DMA((self.n,))] if self.n else []

    def _rows(self, a, core):
        rh = self.shapes[a][0] // 2
        return pl.ds(pl.multiple_of(core * rh, 16), rh)

    def _copies(self, kind, x_refs, o_refs, sems):
        send_sems, recv_sems, local_sems = sems
        x, y, c = _coords()
        s_me = 2 * x + y
        sib = (x, y, 1 - c)

        def copy(a, k, src, dst, to):
            return pltpu.make_async_remote_copy(src_ref=src, dst_ref=dst, send_sem=send_sems.at[6 * a + k],
                                                recv_sem=recv_sems.at[6 * a + k], device_id=to, device_id_type=MESH)

        if kind == 'own':
            return [pltpu.make_async_copy(x_refs[a], o_refs[a].at[s_me], local_sems.at[a]) for a in range(self.n)]
        out = []
        for j, (px, py) in enumerate(_other_chips(x, y)):
            for a in range(self.n):
                mine = self._rows(a, c)
                part = o_refs[a].at[2 * px + py, mine]
                if kind == 'sent':
                    out.append(copy(a, j, x_refs[a].at[mine], o_refs[a].at[s_me, mine], (px, py, c)))
                elif kind == 'arrived':
                    out.append(copy(a, j, part, part, (px, py, c)))
                elif kind == 'passed':
                    out.append(copy(a, 3 + j, part, part, sib))
                else:
                    theirs = o_refs[a].at[2 * px + py, self._rows(a, 1 - c)]
                    out.append(copy(a, 3 + j, theirs, theirs, sib))
        return out

    def start(self, x_refs, o_refs, sems):
        if not self.n:
            return
        for cp in self._copies('own', x_refs, o_refs, sems) + self._copies('sent', x_refs, o_refs, sems):
            cp.start()

    def relay(self, x_refs, o_refs, sems):
        if not self.n:
            return
        for got, fw in zip(self._copies('arrived', x_refs, o_refs, sems),
                           self._copies('passed', x_refs, o_refs, sems)):
            got.wait_recv()
            fw.start()

    def finish(self, x_refs, o_refs, sems):
        if not self.n:
            return
        for cp in self._copies('from_sibling', x_refs, o_refs, sems):
            cp.wait_recv()
        for cp in self._copies('sent', x_refs, o_refs, sems) + self._copies('passed', x_refs, o_refs, sems):
            cp.wait_send()
        for cp in self._copies('own', x_refs, o_refs, sems):
            cp.wait()


def allgather_chips(shards, name):
    gather = ChipGather(shards)
    n = gather.n

    def body(*refs):
        x_refs, o_refs, sems = refs[:n], refs[n:2 * n], refs[2 * n:]
        gather.start(x_refs, o_refs, sems)
        gather.relay(x_refs, o_refs, sems)
        gather.finish(x_refs, o_refs, sems)

    return pl.pallas_call(
        body, name=name, in_specs=gather.in_specs, out_specs=tuple(gather.out_specs),
        out_shape=tuple(gather.out_shape), scratch_shapes=gather.scratch, compiler_params=_params())(*shards)


def sibling_swap(srcs, halves, name):
    n = len(srcs)
    row_axis = [s.ndim - 2 for s in srcs]
    out_shapes = [s.shape[:ax] + (s.shape[ax] // 2,) + s.shape[ax + 1:] if halves else s.shape
                  for s, ax in zip(srcs, row_axis)]

    def body(*refs):
        x_refs, o_refs = refs[:n], refs[n:2 * n]
        send_sems, recv_sems = refs[2 * n:]
        x, y, c = _coords()
        copies = []
        for a in range(n):
            part = x_refs[a]
            if halves:
                rh = srcs[a].shape[row_axis[a]] // 2
                theirs = pl.ds(pl.multiple_of((1 - c) * rh, 16), rh)
                part = part.at[:, theirs] if row_axis[a] == 1 else part.at[theirs]
            cp = pltpu.make_async_remote_copy(src_ref=part, dst_ref=o_refs[a], send_sem=send_sems.at[a],
                                              recv_sem=recv_sems.at[a], device_id=(x, y, 1 - c), device_id_type=MESH)
            cp.start()
            copies.append(cp)
        for cp in copies:
            cp.wait()

    return pl.pallas_call(
        body, name=name, in_specs=[ANY] * n, out_specs=tuple([ANY] * n),
        out_shape=tuple(jax.ShapeDtypeStruct(sh, s.dtype) for sh, s in zip(out_shapes, srcs)),
        scratch_shapes=[pltpu.SemaphoreType.DMA((n,)), pltpu.SemaphoreType.DMA((n,))],
        compiler_params=_params())(*srcs)


def scatter_chips(parts, name):
    n = len(parts)

    def body(*refs):
        x_refs, o_refs = refs[:n], refs[n:2 * n]
        send_sems, recv_sems = refs[2 * n:]
        x, y, c = _coords()
        copies = []
        for j, (px, py) in enumerate(_other_chips(x, y)):
            for a in range(n):
                cp = pltpu.make_async_remote_copy(src_ref=x_refs[a].at[2 * px + py], dst_ref=o_refs[a].at[j],
                                                  send_sem=send_sems.at[3 * a + j], recv_sem=recv_sems.at[3 * a + j],
                                                  device_id=(px, py, c), device_id_type=MESH)
                cp.start()
                copies.append(cp)
        for cp in copies:
            cp.wait()

    return pl.pallas_call(
        body, name=name, in_specs=[ANY] * n, out_specs=tuple([ANY] * n),
        out_shape=tuple(jax.ShapeDtypeStruct((3,) + p.shape[1:], p.dtype) for p in parts),
        scratch_shapes=[pltpu.SemaphoreType.DMA((3 * n,)), pltpu.SemaphoreType.DMA((3 * n,))],
        compiler_params=_params())(*parts)


def _row_block(rows):
    return max(b for b in range(16, 257, 16) if rows % b == 0)


def chip_sum(gpack, recv, core, name):
    n, R, W = gpack.shape
    rh = R // 2
    rb = _row_block(rh)
    nb = rh // rb

    def body(c_ref, g_ref, r_ref, o_ref, ob_ref):
        s = g_ref[...] + r_ref[...]
        o_ref[...] = s
        ob_ref[...] = s.astype(BF16)

    blk = pl.BlockSpec((1, rb, W), lambda i, j, c_ref: (i, j, 0))
    spec = pltpu.PrefetchScalarGridSpec(
        num_scalar_prefetch=1, grid=(n, nb),
        in_specs=[pl.BlockSpec((1, rb, W), lambda i, j, c_ref: (i, c_ref[0] * nb + j, 0)), blk],
        out_specs=(blk, blk))
    return pl.pallas_call(
        body, name=name, grid_spec=spec,
        out_shape=(jax.ShapeDtypeStruct((n, rh, W), F32), jax.ShapeDtypeStruct((n, rh, W), BF16)),
        compiler_params=_params(("arbitrary", "arbitrary")))(core, gpack, recv)


def chip_sum_cols(gate, up, recv_gate, recv_up, core, name):
    R, W = gate.shape
    cw = W // 2
    rh = R // 2
    rb = _row_block(rh)
    nb = rh // rb

    def body(c_ref, g_ref, u_ref, rg_ref, ru_ref, o_ref, ob_ref):
        s = jnp.where(pl.program_id(0) < 2, g_ref[...] + rg_ref[...], u_ref[...] + ru_ref[...])
        o_ref[0] = s
        ob_ref[0] = s.astype(BF16)

    gate_col = lambda s: jnp.minimum(s, 1)
    up_col = lambda s: jnp.maximum(s - 2, 0)
    out = pl.BlockSpec((1, rb, cw), lambda s, j, c_ref: (s, j, 0))
    spec = pltpu.PrefetchScalarGridSpec(
        num_scalar_prefetch=1, grid=(N_CHIPS, nb),
        in_specs=[pl.BlockSpec((rb, cw), lambda s, j, c_ref: (c_ref[0] * nb + j, gate_col(s))),
                  pl.BlockSpec((rb, cw), lambda s, j, c_ref: (c_ref[0] * nb + j, up_col(s))),
                  pl.BlockSpec((rb, cw), lambda s, j, c_ref: (j, gate_col(s))),
                  pl.BlockSpec((rb, cw), lambda s, j, c_ref: (j, up_col(s)))],
        out_specs=(out, out))
    return pl.pallas_call(
        body, name=name, grid_spec=spec,
        out_shape=(jax.ShapeDtypeStruct((N_CHIPS, rh, cw), F32), jax.ShapeDtypeStruct((N_CHIPS, rh, cw), BF16)),
        compiler_params=_params(("arbitrary", "arbitrary")))(core, gate, up, recv_gate, recv_up)


def shard_sum(own, recv, name):
    R, W = own.shape
    rb = _row_block(R)

    def body(a_ref, r_ref, o_ref):
        acc = a_ref[...]
        for j in range(3):
            acc = acc + r_ref[j].astype(F32)
        o_ref[...] = acc

    return pl.pallas_call(
        body, name=name, grid=(R // rb,),
        in_specs=[pl.BlockSpec((rb, W), lambda i: (i, 0)), pl.BlockSpec((3, rb, W), lambda i: (0, i, 0))],
        out_specs=pl.BlockSpec((rb, W), lambda i: (i, 0)), out_shape=jax.ShapeDtypeStruct((R, W), F32),
        compiler_params=_params(("arbitrary",)))(own, recv)


def adamw(w, m, v, g, name):
    L, R, C = w.shape
    rb = max(b for b in range(8, 257, 8) if R % b == 0)
    bc1 = 1.0 - ADAM_B1 ** ADAM_STEP
    bc2 = 1.0 - ADAM_B2 ** ADAM_STEP

    def body(w_ref, m_ref, v_ref, g_ref, d_ref, nm_ref, nv_ref):
        gv = g_ref[...]
        nm = ADAM_B1 * m_ref[...] + (1.0 - ADAM_B1) * gv
        nv = ADAM_B2 * v_ref[...] + (1.0 - ADAM_B2) * (gv * gv)
        d_ref[...] = -ADAM_LR * ((nm / bc1) / (jnp.sqrt(nv / bc2) + ADAM_EPS) + ADAM_WD * w_ref[...])
        nm_ref[...] = nm
        nv_ref[...] = nv

    blk = pl.BlockSpec((1, rb, C), lambda l, i: (l, i, 0))
    sh = jax.ShapeDtypeStruct((L, R, C), F32)
    return pl.pallas_call(body, name=name, grid=(L, R // rb), in_specs=[blk] * 4, out_specs=(blk,) * 3,
                          out_shape=(sh, sh, sh), compiler_params=_params(("arbitrary", "arbitrary")))(w, m, v, g)


SMALL = [n for n in WEIGHTS if n not in BIG]


PACK_TILE = 8 * 128


def _pack(arrays):
    blocks = []
    for a in arrays:
        flat = a.reshape(-1)
        flat = jnp.pad(flat, (0, -flat.size % PACK_TILE))
        blocks.append(flat.reshape(-1, 128))
    rows = sum(b.shape[0] for b in blocks)
    if rows % 16:
        blocks.append(jnp.zeros((8, 128), arrays[0].dtype))
    return jnp.concatenate(blocks, axis=0)


def _unpack(pack, shapes):
    out, row = [], 0
    for shape in shapes:
        size = int(np.prod(shape))
        rows = -(-size // PACK_TILE) * 8
        out.append(pack[row:row + rows].reshape(-1)[:size].reshape(shape))
        row += rows
    return out


def _pad_lanes(a):
    return jnp.pad(a, ((0, 0), (0, -a.shape[1] % 128)))


def _local_shard(full, axis, chip):
    size = full.shape[axis] // N_CHIPS
    return lax.dynamic_slice_in_dim(full, chip * size, size, axis)


def kernel(x, ffn1_norm, ffn1_wi, ffn1_wo, mix_norm, w_in, w_out, lru_conv_w, lru_conv_b, lru_gate_a_w, lru_gate_a_b, lru_gate_x_w, lru_gate_x_b, lru_lambda, lru_out_norm, rwkv_mu, rwkv_w_up, rwkv_w_bias, rwkv_a_up, rwkv_a_bias, rwkv_g_up, rwkv_k_k, rwkv_k_a, rwkv_r_k, rwkv_ln_g, rwkv_ln_b, rwkv_vres_w1, rwkv_vres_w2, rwkv_vres_b, gdn_conv_w, gdn_a_log, gdn_dt_bias, gdn_norm, ffn2_norm, ffn2_wi, ffn2_wo, final_norm, loss_target, m_ffn1_norm, m_ffn1_wi, m_ffn1_wo, m_mix_norm, m_w_in, m_w_out, m_lru_conv_w, m_lru_conv_b, m_lru_gate_a_w, m_lru_gate_a_b, m_lru_gate_x_w, m_lru_gate_x_b, m_lru_lambda, m_lru_out_norm, m_rwkv_mu, m_rwkv_w_up, m_rwkv_w_bias, m_rwkv_a_up, m_rwkv_a_bias, m_rwkv_g_up, m_rwkv_k_k, m_rwkv_k_a, m_rwkv_r_k, m_rwkv_ln_g, m_rwkv_ln_b, m_rwkv_vres_w1, m_rwkv_vres_w2, m_rwkv_vres_b, m_gdn_conv_w, m_gdn_a_log, m_gdn_dt_bias, m_gdn_norm, m_ffn2_norm, m_ffn2_wi, m_ffn2_wo, m_final_norm, v_ffn1_norm, v_ffn1_wi, v_ffn1_wo, v_mix_norm, v_w_in, v_w_out, v_lru_conv_w, v_lru_conv_b, v_lru_gate_a_w, v_lru_gate_a_b, v_lru_gate_x_w, v_lru_gate_x_b, v_lru_lambda, v_lru_out_norm, v_rwkv_mu, v_rwkv_w_up, v_rwkv_w_bias, v_rwkv_a_up, v_rwkv_a_bias, v_rwkv_g_up, v_rwkv_k_k, v_rwkv_k_a, v_rwkv_r_k, v_rwkv_ln_g, v_rwkv_ln_b, v_rwkv_vres_w1, v_rwkv_vres_w2, v_rwkv_vres_b, v_gdn_conv_w, v_gdn_a_log, v_gdn_dt_bias, v_gdn_norm, v_ffn2_norm, v_ffn2_wi, v_ffn2_wo, v_final_norm):
    args = locals()
    w_loc = {n: args[n] for n in WEIGHTS}
    m_loc = {n: args['m_' + n] for n in WEIGHTS}
    v_loc = {n: args['v_' + n] for n in WEIGHTS}
    chip = 2 * lax.axis_index("x") + lax.axis_index("y")
    core = lax.axis_index("c")

    big = [(n, l) for n in BIG for l in range(N_LAYERS)]
    shards = {(n, l): _pad_lanes(w_loc[n][l].astype(BF16)) for n, l in big}
    first = [('ffn1_wi', 0), ('ffn1_wo', 0)]
    wb = {n: [None] * N_LAYERS for n in BIG}
    for (n, l), g in zip(first, allgather_chips([shards[k] for k in first], "allgather_first")):
        wb[n][l] = _natural(n, g)

    sm_names = list(SMALL_SHARDED)
    placed = []
    for n in sm_names:
        ax = SMALL_SHARDED[n]
        full_shape = w_loc[n].shape[:ax] + (N_CHIPS * w_loc[n].shape[ax],) + w_loc[n].shape[ax + 1:]
        src = w_loc[n] * (core == 0).astype(F32)
        placed.append(lax.dynamic_update_slice_in_dim(jnp.zeros(full_shape, F32), src, chip * w_loc[n].shape[ax], ax))
    summed = allreduce_small(_pack(placed), "allgather_small")
    w_full = dict(w_loc)
    w_full.update(zip(sm_names, _unpack(summed, [p.shape for p in placed])))

    loss, dx, grads = local_step(x[0], loss_target[0], w_full, wb, shards)
    loss = lax.psum(loss, ("x", "y", "c"))

    gsum = allreduce_small(_pack([grads[n] for n in SMALL]), "allreduce_small")
    g_loc = {}
    for n, g in zip(SMALL, _unpack(gsum, [grads[n].shape for n in SMALL])):
        g_loc[n] = _local_shard(g, SMALL_SHARDED[n], chip) if n in SMALL_SHARDED else g

    parts = []
    for n, l in big:
        parts += list(grads[n][l]) if isinstance(grads[n][l], tuple) else [grads[n][l]]
    swapped = iter(zip(parts, sibling_swap(parts, True, "grad_swap_cores")))
    core_arg = core.reshape(1).astype(jnp.int32)
    sums = []
    for n, l in big:
        if isinstance(grads[n][l], tuple):
            (dwg, from_g), (dwu, from_u) = next(swapped), next(swapped)
            sums.append(chip_sum_cols(dwg, dwu, from_g, from_u, core_arg, f"grad_chip_sum_{n}{l}"))
        else:
            p, r = next(swapped)
            sums.append(chip_sum(p, r, core_arg, f"grad_chip_sum_{n}{l}"))
    from_chips = scatter_chips([s[1] for s in sums], "grad_scatter")
    halves = [shard_sum(lax.dynamic_index_in_dim(s[0], chip, 0, keepdims=False), r, f"grad_shard_sum_{n}{l}")
              for (n, l), s, r in zip(big, sums, from_chips)]
    others = sibling_swap(halves, False, "grad_share_cores")
    rows = {n: [None] * N_LAYERS for n in BIG}
    for (n, l), half, other in zip(big, halves, others):
        lower = jnp.where(core == 0, half, other)
        upper = jnp.where(core == 0, other, half)
        rows[n][l] = jnp.concatenate([lower, upper], axis=0)[:, :w_loc[n].shape[-1]]
    big_names = list(BIG)
    for n in big_names:
        g_loc[n] = jnp.stack(rows[n])

    delta, new_m, new_v = {}, {}, {}
    for n in big_names:
        delta[n], new_m[n], new_v[n] = adamw(w_loc[n], m_loc[n], v_loc[n], g_loc[n], f"adamw_{n}")
    pack = lambda d: _pack([d[n] for n in SMALL])[None]
    res = adamw(pack(w_loc), pack(m_loc), pack(v_loc), pack(g_loc), "adamw_small")
    for dst, r in zip((delta, new_m, new_v), res):
        dst.update(zip(SMALL, _unpack(r[0], [w_loc[n].shape for n in SMALL])))

    return (loss, dx[None], *[g_loc[n] for n in WEIGHTS], *[delta[n] for n in WEIGHTS],
            *[new_m[n] for n in WEIGHTS], *[new_v[n] for n in WEIGHTS])
```

```python
import functools

import numpy as np
import jax
import jax.numpy as jnp
from jax import lax
from jax.experimental import pallas as pl
from jax.experimental.pallas import tpu as pltpu

F32 = jnp.float32
BF16 = jnp.bfloat16
MESH = pl.DeviceIdType.MESH

D_MODEL = 1024
D_FF = 2816
N_LAYERS = 2
HEADS = 6
HEAD_DIM = 64
MIX_W = HEADS * HEAD_DIM
LRU_W = 256
LRU_BLOCKS = 4
D_IN = 3468
D_IN_PAD = 3584
NORM_EPS = 1e-6
GN_EPS = 64e-5
LRU_C = 8.0
CHUNK = 64
ROWS = 256
FF_CHUNK = 256
IN_CHUNK = 512
VMEM_LIMIT = 56 * 1024 * 1024

ADAM_LR, ADAM_B1, ADAM_B2, ADAM_EPS, ADAM_WD, ADAM_STEP = 0.001, 0.9, 0.999, 1e-08, 0.01, 10

WEIGHTS = ['ffn1_norm', 'ffn1_wi', 'ffn1_wo', 'mix_norm', 'w_in', 'w_out', 'lru_conv_w', 'lru_conv_b',
           'lru_gate_a_w', 'lru_gate_a_b', 'lru_gate_x_w', 'lru_gate_x_b', 'lru_lambda', 'lru_out_norm',
           'rwkv_mu', 'rwkv_w_up', 'rwkv_w_bias', 'rwkv_a_up', 'rwkv_a_bias', 'rwkv_g_up', 'rwkv_k_k',
           'rwkv_k_a', 'rwkv_r_k', 'rwkv_ln_g', 'rwkv_ln_b', 'rwkv_vres_w1', 'rwkv_vres_w2', 'rwkv_vres_b',
           'gdn_conv_w', 'gdn_a_log', 'gdn_dt_bias', 'gdn_norm', 'ffn2_norm', 'ffn2_wi', 'ffn2_wo', 'final_norm']
BIG = {'ffn1_wi': 2, 'ffn1_wo': 1, 'w_in': 2, 'w_out': 1, 'ffn2_wi': 2, 'ffn2_wo': 1}
SMALL_SHARDED = {'lru_conv_w': 2, 'rwkv_w_up': 2, 'rwkv_a_up': 2, 'rwkv_g_up': 2, 'rwkv_vres_w1': 1,
                 'rwkv_vres_w2': 2, 'gdn_conv_w': 2}
N_CHIPS = 4


def _params(sem=None):
    kw = dict(vmem_limit_bytes=VMEM_LIMIT)
    if sem is not None:
        kw['dimension_semantics'] = sem
    return pltpu.CompilerParams(**kw)


def _bdot(a, b, dims=(((1,), (0,)), ((), ()))):
    return lax.dot_general(a.astype(BF16), b.astype(BF16), dims, preferred_element_type=F32)


def _bdot_nt(a, b):
    return _bdot(a, b, (((1,), (1,)), ((), ())))


def _bdot_tn(a, b):
    return _bdot(a, b, (((0,), (0,)), ((), ())))


_DIMS = {'nn': (((1,), (0,)), ((), ())), 'nt': (((1,), (1,)), ((), ())), 'tn': (((0,), (0,)), ((), ()))}


def _split(a, terms):
    parts = []
    for _ in range(terms - 1):
        hi = a.astype(BF16)
        parts.append(hi)
        a = a - hi.astype(F32)
    parts.append(a.astype(BF16))
    return parts


_BATCH_DIMS = {'nn': (((2,), (1,)), ((0,), (0,))), 'nt': (((2,), (2,)), ((0,), (0,))),
               'tn': (((1,), (1,)), ((0,), (0,)))}


def _dot3(a, b, kind):
    ah, al = _split(a, 2)
    bh, bl = _split(b, 2)
    dims = _BATCH_DIMS[kind] if a.ndim == 3 else _DIMS[kind]
    d = lambda p, q: lax.dot_general(p, q, dims, preferred_element_type=F32)
    return d(ah, bh) + (d(ah, bl) + d(al, bh))


@functools.partial(jax.custom_vjp, nondiff_argnums=(2,))
def _cdot_k(a, b, kind):
    return _dot3(a, b, kind)


def _cdot_k_fwd(a, b, kind):
    return _dot3(a, b, kind), (a, b)


def _cdot_k_bwd(kind, res, ct):
    a, b = res
    if kind == 'nn':
        return _dot3(ct, b, 'nt'), _dot3(a, ct, 'tn')
    if kind == 'nt':
        return _dot3(ct, b, 'nn'), _dot3(ct, a, 'tn')
    return _dot3(b, ct, 'nt'), _dot3(a, ct, 'nn')


_cdot_k.defvjp(_cdot_k_fwd, _cdot_k_bwd)


def _dot1(a, b, kind):
    dims = _BATCH_DIMS[kind] if a.ndim == 3 else _DIMS[kind]
    return lax.dot_general(a.astype(BF16), b.astype(BF16), dims, preferred_element_type=F32)


@functools.partial(jax.custom_vjp, nondiff_argnums=(2,))
def _cdot1_k(a, b, kind):
    return _dot1(a, b, kind)


def _cdot1_k_fwd(a, b, kind):
    return _dot1(a, b, kind), (a, b)


def _cdot1_k_bwd(kind, res, ct):
    a, b = res
    if kind == 'nn':
        return _dot1(ct, b, 'nt'), _dot1(a, ct, 'tn')
    if kind == 'nt':
        return _dot1(ct, b, 'nn'), _dot1(ct, a, 'tn')
    return _dot1(b, ct, 'nt'), _dot1(a, ct, 'nn')


_cdot1_k.defvjp(_cdot1_k_fwd, _cdot1_k_bwd)


def _cdot(a, b):
    return _cdot1_k(a, b, 'nn')


def _cdot_nt(a, b):
    return _cdot1_k(a, b, 'nt')


def _cdot_tn(a, b):
    return _cdot1_k(a, b, 'tn')


def _hdot(a, b):
    return _cdot_k(a, b, 'nn')


def _dot_exact(x, m01, kind):
    d = lambda p: lax.dot_general(p, m01.astype(BF16), _DIMS[kind], preferred_element_type=F32)
    hi, mid, lo = _split(x, 3)
    return d(hi) + (d(mid) + d(lo))


@functools.partial(jax.custom_vjp, nondiff_argnums=(1,))
def _xdot(x, make_m):
    return _dot_exact(x, make_m(), 'nn')


def _xdot_fwd(x, make_m):
    return _dot_exact(x, make_m(), 'nn'), None


def _xdot_bwd(make_m, _, ct):
    return (_dot_exact(ct, make_m(), 'nt'),)


_xdot.defvjp(_xdot_fwd, _xdot_bwd)


def _iota2(n, m):
    return lax.broadcasted_iota(jnp.int32, (n, m), 0), lax.broadcasted_iota(jnp.int32, (n, m), 1)


def _head_blocks(w):
    ri, ci = _iota2(w, w)
    return (ri // HEAD_DIM == ci // HEAD_DIM).astype(F32)


def _segsum(x):
    return _xdot(x, functools.partial(_head_blocks, x.shape[-1]))


def _cumsum_rows(x):
    return _cumsum_k(x, x.shape[0])


@functools.partial(jax.custom_vjp, nondiff_argnums=(1,))
def _cumsum_k(x, n):
    return _lower_dot(x, n, False)


def _lower_dot(x, n, transpose):
    ri, ci = _iota2(n, n)
    m = ((ri <= ci) if transpose else (ri >= ci)).astype(BF16)
    d = lambda p: lax.dot_general(m, p, _DIMS['nn'], preferred_element_type=F32)
    hi, mid, lo = _split(x, 3)
    return d(hi) + (d(mid) + d(lo))


def _cumsum_k_fwd(x, n):
    return _lower_dot(x, n, False), None


def _cumsum_k_bwd(n, _, ct):
    return (_lower_dot(ct, n, True),)


_cumsum_k.defvjp(_cumsum_k_fwd, _cumsum_k_bwd)


def _rms(x, g):
    return x * lax.rsqrt(jnp.mean(x * x, axis=-1, keepdims=True) + NORM_EPS) * g


DENSE_ROWS = 1024


def _row_loop(n_rows, fn):
    rows = min(DENSE_ROWS, n_rows)

    def step(i, c):
        fn(pl.ds(pl.multiple_of(i * rows, rows), rows))
        return c
    lax.fori_loop(0, n_rows // rows, step, 0)


def ffn_fwd(x, g, wi, wo, name, hosted=()):
    T = x.shape[0]
    nj = D_FF // FF_CHUNK
    gather = ChipGather(list(hosted))
    n = gather.n

    def body(*refs):
        x_ref, g_ref, wg_ref, wu_ref, wo_ref = refs[:5]
        hx, o_ref, ho = refs[5:5 + n], refs[5 + n], refs[6 + n:6 + 2 * n]
        h_ref, acc_ref = refs[6 + 2 * n:8 + 2 * n]
        sems = refs[8 + 2 * n:]
        j = pl.program_id(0)

        @pl.when(j == 0)
        def _():
            gather.start(hx, ho, sems)

            def init(r):
                h_ref[r, :] = _rms(x_ref[r, :], g_ref[...]).astype(BF16)
                acc_ref[r, :] = jnp.zeros((r.size, D_MODEL), F32)
            _row_loop(T, init)

        def blk(r):
            hb = h_ref[r, :]
            gate = jnp.dot(hb, wg_ref[...], preferred_element_type=F32)
            up = jnp.dot(hb, wu_ref[...], preferred_element_type=F32)
            a = (gate * jax.nn.sigmoid(gate) * up).astype(BF16)
            acc_ref[r, :] += jnp.dot(a, wo_ref[...], preferred_element_type=F32)
        _row_loop(T, blk)

        @pl.when(j == nj - 2)
        def _():
            gather.relay(hx, ho, sems)

        @pl.when(j == nj - 1)
        def _():
            def fin(r):
                o_ref[r, :] = x_ref[r, :] + 0.5 * acc_ref[r, :]
            _row_loop(T, fin)
            gather.finish(hx, ho, sems)

    full = pl.BlockSpec((T, D_MODEL), lambda j: (0, 0))
    res = pl.pallas_call(
        body, name=name, grid=(nj,),
        in_specs=[full, pl.BlockSpec((1, D_MODEL), lambda j: (0, 0)),
                  pl.BlockSpec((D_MODEL, FF_CHUNK), lambda j: (0, j)),
                  pl.BlockSpec((D_MODEL, FF_CHUNK), lambda j: (0, j + nj)),
                  pl.BlockSpec((FF_CHUNK, D_MODEL), lambda j: (j, 0))] + gather.in_specs,
        out_specs=tuple([full] + gather.out_specs),
        out_shape=tuple([jax.ShapeDtypeStruct((T, D_MODEL), F32)] + gather.out_shape),
        scratch_shapes=[pltpu.VMEM((T, D_MODEL), BF16), pltpu.VMEM((T, D_MODEL), F32)] + gather.scratch,
        compiler_params=_params(("arbitrary",)))(x, g, wi, wi, wo, *hosted)
    return res[0], list(res[1:])


def _norm_bwd_rows(x, g, dh, dres):
    rstd = lax.rsqrt(jnp.mean(x * x, axis=-1, keepdims=True) + NORM_EPS)
    xh = x * rstd
    dxh = dh * g
    dx = rstd * (dxh - xh * jnp.mean(dxh * xh, axis=-1, keepdims=True))
    return dres + dx, jnp.sum(dh * xh, axis=0, keepdims=True)


def ffn_bwd(x, dy, g, wi, wo, name):
    T = x.shape[0]
    nj = D_FF // FF_CHUNK

    def body(x_ref, dy_ref, g_ref, wg_ref, wu_ref, wo_ref, dx_ref, dg_ref, dwg_ref, dwu_ref, dwo_ref,
             h_ref, da_ref, dh_ref):
        j = pl.program_id(0)

        @pl.when(j == 0)
        def _():
            def init(r):
                h_ref[r, :] = _rms(x_ref[r, :], g_ref[...]).astype(BF16)
                da_ref[r, :] = (0.5 * dy_ref[r, :]).astype(BF16)
                dh_ref[r, :] = jnp.zeros((r.size, D_MODEL), F32)
            _row_loop(T, init)

        dwg_ref[...] = jnp.zeros_like(dwg_ref)
        dwu_ref[...] = jnp.zeros_like(dwu_ref)
        dwo_ref[...] = jnp.zeros_like(dwo_ref)

        def blk(r):
            hb = h_ref[r, :]
            db = da_ref[r, :]
            gate = jnp.dot(hb, wg_ref[...], preferred_element_type=F32)
            up = jnp.dot(hb, wu_ref[...], preferred_element_type=F32)
            sg = jax.nn.sigmoid(gate)
            sl = gate * sg
            da = _bdot_nt(db, wo_ref[...])
            dup = (da * sl).astype(BF16)
            dgate = (da * up * (sg * (1.0 + gate * (1.0 - sg)))).astype(BF16)
            dwo_ref[...] += _bdot_tn((sl * up).astype(BF16), db)
            dwg_ref[...] += _bdot_tn(hb, dgate)
            dwu_ref[...] += _bdot_tn(hb, dup)
            dh_ref[r, :] += _bdot_nt(dgate, wg_ref[...]) + _bdot_nt(dup, wu_ref[...])
        _row_loop(T, blk)

        @pl.when(j == nj - 1)
        def _():
            dg_ref[...] = jnp.zeros_like(dg_ref)

            def fin(r):
                dx, dg = _norm_bwd_rows(x_ref[r, :], g_ref[...], dh_ref[r, :], dy_ref[r, :])
                dx_ref[r, :] = dx
                dg_ref[...] += dg
            _row_loop(T, fin)

    full = pl.BlockSpec((T, D_MODEL), lambda j: (0, 0))
    vec = pl.BlockSpec((1, D_MODEL), lambda j: (0, 0))
    return pl.pallas_call(
        body, name=name, grid=(nj,),
        in_specs=[full, full, vec,
                  pl.BlockSpec((D_MODEL, FF_CHUNK), lambda j: (0, j)),
                  pl.BlockSpec((D_MODEL, FF_CHUNK), lambda j: (0, j + nj)),
                  pl.BlockSpec((FF_CHUNK, D_MODEL), lambda j: (j, 0))],
        out_specs=(full, vec,
                   pl.BlockSpec((D_MODEL, FF_CHUNK), lambda j: (0, j)),
                   pl.BlockSpec((D_MODEL, FF_CHUNK), lambda j: (0, j)),
                   pl.BlockSpec((FF_CHUNK, D_MODEL), lambda j: (j, 0))),
        out_shape=(jax.ShapeDtypeStruct((T, D_MODEL), F32), jax.ShapeDtypeStruct((1, D_MODEL), F32),
                   jax.ShapeDtypeStruct((D_MODEL, D_FF), F32), jax.ShapeDtypeStruct((D_MODEL, D_FF), F32),
                   jax.ShapeDtypeStruct((D_FF, D_MODEL), F32)),
        scratch_shapes=[pltpu.VMEM((T, D_MODEL), BF16), pltpu.VMEM((T, D_MODEL), BF16),
                        pltpu.VMEM((T, D_MODEL), F32)],
        compiler_params=_params(("arbitrary",)))(x, dy, g, wi, wi, wo)


def proj_fwd(x, g, w, name):
    T = x.shape[0]
    nj = D_IN_PAD // IN_CHUNK

    def body(x_ref, g_ref, w_ref, o_ref, h_ref):
        @pl.when(pl.program_id(0) == 0)
        def _():
            def init(r):
                h_ref[r, :] = _rms(x_ref[r, :], g_ref[...]).astype(BF16)
            _row_loop(T, init)

        def blk(r):
            o_ref[r, :] = jnp.dot(h_ref[r, :], w_ref[...], preferred_element_type=F32)
        _row_loop(T, blk)

    return pl.pallas_call(
        body, name=name, grid=(nj,),
        in_specs=[pl.BlockSpec((T, D_MODEL), lambda j: (0, 0)), pl.BlockSpec((1, D_MODEL), lambda j: (0, 0)),
                  pl.BlockSpec((D_MODEL, IN_CHUNK), lambda j: (0, j))],
        out_specs=pl.BlockSpec((T, IN_CHUNK), lambda j: (0, j)),
        out_shape=jax.ShapeDtypeStruct((T, D_IN_PAD), F32),
        scratch_shapes=[pltpu.VMEM((T, D_MODEL), BF16)],
        compiler_params=_params(("arbitrary",)))(x, g, w)


def proj_bwd(x, dres, g, w, dp, name):
    T = x.shape[0]
    nj = D_IN_PAD // IN_CHUNK

    def body(x_ref, dres_ref, g_ref, w_ref, dp_ref, dx_ref, dg_ref, dw_ref, h_ref, dh_ref):
        j = pl.program_id(0)

        @pl.when(j == 0)
        def _():
            def init(r):
                h_ref[r, :] = _rms(x_ref[r, :], g_ref[...]).astype(BF16)
                dh_ref[r, :] = jnp.zeros((r.size, D_MODEL), F32)
            _row_loop(T, init)

        dw_ref[...] = jnp.zeros_like(dw_ref)

        def blk(r):
            dpb = dp_ref[r, :].astype(BF16)
            dw_ref[...] += _bdot_tn(h_ref[r, :], dpb)
            dh_ref[r, :] += _bdot_nt(dpb, w_ref[...])
        _row_loop(T, blk)

        @pl.when(j == nj - 1)
        def _():
            dg_ref[...] = jnp.zeros_like(dg_ref)

            def fin(r):
                dx, dg = _norm_bwd_rows(x_ref[r, :], g_ref[...], dh_ref[r, :], dres_ref[r, :])
                dx_ref[r, :] = dx
                dg_ref[...] += dg
            _row_loop(T, fin)

    full = pl.BlockSpec((T, D_MODEL), lambda j: (0, 0))
    vec = pl.BlockSpec((1, D_MODEL), lambda j: (0, 0))
    return pl.pallas_call(
        body, name=name, grid=(nj,),
        in_specs=[full, full, vec, pl.BlockSpec((D_MODEL, IN_CHUNK), lambda j: (0, j)),
                  pl.BlockSpec((T, IN_CHUNK), lambda j: (0, j))],
        out_specs=(full, vec, pl.BlockSpec((D_MODEL, IN_CHUNK), lambda j: (0, j))),
        out_shape=(jax.ShapeDtypeStruct((T, D_MODEL), F32), jax.ShapeDtypeStruct((1, D_MODEL), F32),
                   jax.ShapeDtypeStruct((D_MODEL, D_IN_PAD), F32)),
        scratch_shapes=[pltpu.VMEM((T, D_MODEL), BF16), pltpu.VMEM((T, D_MODEL), F32)],
        compiler_params=_params(("arbitrary",)))(x, dres, g, w, dp)


def out_fwd(mixed, w, x, name):
    T = x.shape[0]

    def body(m_ref, w_ref, x_ref, o_ref):
        o_ref[...] = x_ref[...] + jnp.dot(m_ref[...].astype(BF16), w_ref[...], preferred_element_type=F32)

    blk = pl.BlockSpec((ROWS, D_MODEL), lambda i: (i, 0))
    return pl.pallas_call(
        body, name=name, grid=(T // ROWS,),
        in_specs=[blk, pl.BlockSpec((D_MODEL, D_MODEL), lambda i: (0, 0)), blk],
        out_specs=blk, out_shape=jax.ShapeDtypeStruct((T, D_MODEL), F32),
        compiler_params=_params(("arbitrary",)))(mixed, w, x)


def out_bwd(mixed, w, dy, name):
    T = dy.shape[0]

    def body(m_ref, w_ref, dy_ref, dm_ref, dw_ref):
        @pl.when(pl.program_id(0) == 0)
        def _():
            dw_ref[...] = jnp.zeros_like(dw_ref)
        dyb = dy_ref[...].astype(BF16)
        dm_ref[...] = _bdot_nt(dyb, w_ref[...])
        dw_ref[...] += _bdot_tn(m_ref[...].astype(BF16), dyb)

    blk = pl.BlockSpec((ROWS, D_MODEL), lambda i: (i, 0))
    sq = pl.BlockSpec((D_MODEL, D_MODEL), lambda i: (0, 0))
    return pl.pallas_call(
        body, name=name, grid=(T // ROWS,),
        in_specs=[blk, sq, blk], out_specs=(blk, sq),
        out_shape=(jax.ShapeDtypeStruct((T, D_MODEL), F32), jax.ShapeDtypeStruct((D_MODEL, D_MODEL), F32)),
        compiler_params=_params(("arbitrary",)))(mixed, w, dy)


def loss_head(x, g, target, name):
    T = x.shape[0]

    def body(x_ref, g_ref, t_ref, loss_ref, dx_ref, dg_ref):
        @pl.when(pl.program_id(0) == 0)
        def _():
            loss_ref[...] = jnp.zeros_like(loss_ref)
            dg_ref[...] = jnp.zeros_like(dg_ref)
        xb = x_ref[...]
        rstd = lax.rsqrt(jnp.mean(xb * xb, axis=-1, keepdims=True) + NORM_EPS)
        xh = xb * rstd
        err = xh * g_ref[...] - t_ref[...]
        loss_ref[...] += 0.5 * jnp.sum(jnp.mean(err * err, axis=-1, keepdims=True), axis=0, keepdims=True)
        dy = err * (1.0 / D_MODEL)
        dg_ref[...] += jnp.sum(dy * xh, axis=0, keepdims=True)
        dxh = dy * g_ref[...]
        dx_ref[...] = rstd * (dxh - xh * jnp.mean(dxh * xh, axis=-1, keepdims=True))

    blk = pl.BlockSpec((ROWS, D_MODEL), lambda i: (i, 0))
    vec = pl.BlockSpec((1, D_MODEL), lambda i: (0, 0))
    return pl.pallas_call(
        body, name=name, grid=(T // ROWS,),
        in_specs=[blk, vec, blk], out_specs=(pl.BlockSpec((1, 1), lambda i: (0, 0)), blk, vec),
        out_shape=(jax.ShapeDtypeStruct((1, 1), F32), jax.ShapeDtypeStruct((T, D_MODEL), F32),
                   jax.ShapeDtypeStruct((1, D_MODEL), F32)),
        compiler_params=_params(("arbitrary",)))(x, g, target)


def rowwise_fwd(fn, rows, shared, out_widths, name):
    T = rows[0].shape[0]
    n_in = len(rows) + len(shared)

    def body(*refs):
        res = fn(*[r[...] for r in refs[:n_in]])
        for o, v in zip(refs[n_in:], res):
            o[...] = v

    in_specs = ([pl.BlockSpec((ROWS, a.shape[1]), lambda i: (i, 0)) for a in rows]
                + [pl.BlockSpec(a.shape, lambda i: (0, 0)) for a in shared])
    return pl.pallas_call(
        body, name=name, grid=(T // ROWS,), in_specs=in_specs,
        out_specs=tuple(pl.BlockSpec((ROWS, w), lambda i: (i, 0)) for w in out_widths),
        out_shape=tuple(jax.ShapeDtypeStruct((T, w), F32) for w in out_widths),
        compiler_params=_params(("arbitrary",)))(*rows, *shared)


def rowwise_bwd(fn, rows, shared, cts, name, ct_fn=None):
    T = rows[0].shape[0]
    nr, ns, nc = len(rows), len(shared), len(cts)

    def body(*refs):
        ins = [r[...] for r in refs[:nr + ns]]
        ctv = tuple(r[...] for r in refs[nr + ns:nr + ns + nc])
        outs = refs[nr + ns + nc:]
        _, vjp = jax.vjp(fn, *ins)
        grads = vjp(ct_fn(*ctv) if ct_fn is not None else ctv)
        for k in range(nr):
            outs[k][...] = grads[k]

        @pl.when(pl.program_id(0) == 0)
        def _():
            for k in range(ns):
                outs[nr + k][...] = jnp.zeros_like(outs[nr + k])
        for k in range(ns):
            outs[nr + k][...] += grads[nr + k]

    row_spec = lambda a: pl.BlockSpec((ROWS, a.shape[1]), lambda i: (i, 0))
    sh_spec = lambda a: pl.BlockSpec(a.shape, lambda i: (0, 0))
    return pl.pallas_call(
        body, name=name, grid=(T // ROWS,),
        in_specs=[row_spec(a) for a in rows] + [sh_spec(a) for a in shared] + [row_spec(a) for a in cts],
        out_specs=tuple([row_spec(a) for a in rows] + [sh_spec(a) for a in shared]),
        out_shape=tuple(jax.ShapeDtypeStruct(a.shape, F32) for a in list(rows) + list(shared)),
        compiler_params=_params(("arbitrary",)))(*rows, *shared, *cts)


def shift_rows(x, s):
    return jnp.pad(x, ((s, 0), (0, 0)))[:x.shape[0]]


def unshift_rows(x, s):
    return jnp.pad(x, ((0, s), (0, 0)))[s:]


def _neg_expm1(y):
    series = -(y * (1.0 + y * (0.5 + y * (1.0 / 6.0 + y * (1.0 / 24.0)))))
    return jnp.where(y > -0.05, series, 1.0 - jnp.exp(y))


def lru_pre_fn(x0, x1, x2, x3, first, w0, w1, w2, w3, cb, ga, gab, gx, gxb, lam):
    xc = w3 * x0 + w2 * x1 + w1 * x2 + w0 * x3 + cb
    r = jax.nn.sigmoid(_hdot(xc, ga) + gab)
    i = jax.nn.sigmoid(_hdot(xc, gx) + gxb)
    log_a = -LRU_C * r * jax.nn.softplus(-lam)
    a = jnp.exp(log_a)
    mult = jnp.where(first > 0.5, 1.0, jnp.sqrt(_neg_expm1(2.0 * log_a)))
    return a, mult * i * xc


def lru_post_fn(h, py, og):
    return (_rms(h * jax.nn.gelu(py), og),)


def lru_scan(a, b, reverse, name):
    T, C = a.shape
    nb = T // 8

    def body(a_ref, b_ref, h_ref):
        rows = lax.broadcasted_iota(jnp.int32, (8, C), 0)

        def blk(i, carry):
            j = nb - 1 - i if reverse else i
            r = pl.ds(pl.multiple_of(j * 8, 8), 8)
            A = a_ref[r, :]
            B = b_ref[r, :]
            for s in (1, 2, 4):
                if reverse:
                    keep = rows < 8 - s
                    sh = 8 - s
                else:
                    keep = rows >= s
                    sh = s
                Bs = jnp.where(keep, pltpu.roll(B, sh, 0), 0.0)
                As = jnp.where(keep, pltpu.roll(A, sh, 0), 1.0)
                B = B + A * Bs
                A = A * As
            hb = B + A * carry
            h_ref[r, :] = hb
            edge = 0 if reverse else 7
            return jnp.sum(jnp.where(rows == edge, hb, 0.0), axis=0, keepdims=True)

        lax.fori_loop(0, nb, blk, jnp.zeros((1, C), F32))

    full = pl.BlockSpec((T, C), lambda: (0, 0))
    return pl.pallas_call(body, name=name, in_specs=[full, full], out_specs=full,
                          out_shape=jax.ShapeDtypeStruct((T, C), F32), compiler_params=_params())(a, b)


def make_rwkv_pre_fn(has_vres):
    def fn(p, pp, *rest):
        if has_vres:
            vf, mu, w_up, w_b, a_up, a_b, g_up, kk_w, ka_w, vw1, vw2, vb = rest
        else:
            mu, w_up, w_b, a_up, a_b, g_up, kk_w, ka_w = rest
        xm = p + (pp - p) * mu
        r, k, v = xm[:, 0:384], xm[:, 384:768], xm[:, 768:1152]
        xw, xa, xg = xm[:, 1152:1216], xm[:, 1216:1280], xm[:, 1280:1408]
        w_log = -jax.nn.softplus(-(w_b + _hdot(jnp.tanh(xw), w_up))) - 0.5
        lw = -jnp.exp(w_log)
        a = jax.nn.sigmoid(a_b + _hdot(xa, a_up))
        g = _hdot(jax.nn.sigmoid(xg), g_up)
        if has_vres:
            v = v + (vf - v) * jax.nn.sigmoid(vb + _hdot(_hdot(v, vw1), vw2))
        kkx = k * kk_w
        kk = kkx * lax.rsqrt(_segsum(kkx * kkx) + 1e-6)
        k2 = k * (1.0 + (a - 1.0) * ka_w)
        return r, lw, k2, v, kk, a, g
    return fn


def rwkv_post_fn(y, r, k2, v, g, ln_g, ln_b, r_k):
    mean = _segsum(y) * (1.0 / HEAD_DIM)
    yc = y - mean
    var = _segsum(yc * yc) * (1.0 / HEAD_DIM)
    yn = yc * lax.rsqrt(var + GN_EPS) * ln_g + ln_b
    bonus = _segsum(r * k2 * r_k) * v
    return ((yn + bonus) * g,)


def _head_expander(first_lane):
    ri, ci = _iota2(128, MIX_W)
    return (ri == ci // HEAD_DIM + first_lane).astype(F32)


def gdn_pre_fn(x0, x1, x2, x3, ab, w0, w1, w2, w3, alog, dtb):
    qkv = jax.nn.silu(w3 * x0 + w2 * x1 + w1 * x2 + w0 * x3)
    q, k, v = qkv[:, 0:384], qkv[:, 384:768], qkv[:, 768:1152]
    q = q * lax.rsqrt(_segsum(q * q) + 1e-6) * (HEAD_DIM ** -0.5)
    k = k * lax.rsqrt(_segsum(k * k) + 1e-6)
    g = -jnp.exp(alog) * jax.nn.softplus(ab + dtb)
    beta = jax.nn.sigmoid(ab)
    ge = _xdot(g, functools.partial(_head_expander, 0))
    be = _xdot(beta, functools.partial(_head_expander, HEADS))
    return q, k, v, ge, be


def gdn_post_fn(o, z, ng):
    ms = _segsum(o * o) * (1.0 / HEAD_DIM)
    return (o * lax.rsqrt(ms + NORM_EPS) * ng * jax.nn.silu(z),)


def _neumann_inv(m):
    n = m.shape[-1]
    ri, ci = _iota2(n, n)
    eye = (ri == ci).astype(F32)
    md = jnp.where(ri // 16 == ci // 16, m, 0.0)
    mo = m - md
    t0 = eye + md
    p2 = _hdot(md, md)
    t0 = t0 + _hdot(t0, p2)
    p4 = _hdot(p2, p2)
    t0 = t0 + _hdot(t0, p4)
    p8 = _hdot(p4, p4)
    t0 = t0 + _hdot(t0, p8)
    nn = _hdot(t0, mo)
    n2 = _hdot(nn, nn)
    t1 = eye + nn + n2 + _hdot(nn, n2)
    return _hdot(t1, t0)


@jax.custom_vjp
def _inv_saved(m, t_saved):
    return t_saved


def _inv_saved_fwd(m, t_saved):
    return t_saved, t_saved


def _inv_saved_bwd(t_saved, dt):
    tt = jnp.swapaxes(t_saved, -1, -2)
    return _hdot(_hdot(tt, dt), tt), jnp.zeros_like(t_saved)


_inv_saved.defvjp(_inv_saved_fwd, _inv_saved_bwd)


def _heads(x):
    return jnp.concatenate([x[None, :, h * HEAD_DIM:(h + 1) * HEAD_DIM] for h in range(HEADS)], axis=0)


def _unheads(y):
    return jnp.concatenate([lax.index_in_dim(y, h, 0, keepdims=False) for h in range(HEADS)], axis=1)


def rwkv_heads(s0, r, lw, k2, v, kk, a, inv):
    n = r.shape[0]
    ri, ci = _iota2(n, n)
    low, strict = ri >= ci, ri > ci
    cs = _cumsum_rows(lw)
    cl = jnp.sum(lw, axis=0, keepdims=True)
    p_in, p_prev, p_inv = jnp.exp(cs), jnp.exp(cs - lw), jnp.exp(-cs)
    p_rest, p_all = jnp.exp(cl - cs), jnp.exp(cl)
    bd = kk * a
    at, rt = _heads(-kk * p_prev), _heads(r * p_in)
    bh, kh = _heads(bd * p_inv), _heads(k2 * p_inv)
    vh = _heads(v)
    m_ab = jnp.where(strict, _cdot_nt(at, bh), 0.0)
    m_ak = jnp.where(strict, _cdot_nt(at, kh), 0.0)
    m_rb = jnp.where(low, _cdot_nt(rt, bh), 0.0)
    m_rk = jnp.where(low, _cdot_nt(rt, kh), 0.0)
    sa = _cdot(inv(m_ab), _cdot_nt(at, s0) + _cdot(m_ak, vh))
    y = _cdot_nt(rt, s0) + _cdot(m_rb, sa) + _cdot(m_rk, vh)
    s1 = s0 * _heads(p_all) + _cdot_tn(sa, _heads(bd * p_rest)) + _cdot_tn(vh, _heads(k2 * p_rest))
    return _unheads(y), s1


def gdn_heads(s0, q, k, v, ge, be, inv):
    n = q.shape[0]
    ri, ci = _iota2(n, n)
    low, strict = ri >= ci, ri > ci
    gc = _cumsum_rows(ge)
    gl = jnp.sum(ge, axis=0, keepdims=True)
    gch = _heads(gc)
    decay = jnp.where(low, jnp.exp(jnp.where(low, gch - jnp.swapaxes(gch, 1, 2), 0.0)), 0.0)
    kb = k * be
    e = jnp.exp(gc)
    kh = _heads(k)
    m = -jnp.where(strict, _cdot_nt(_heads(kb), kh) * decay, 0.0)
    mr = jnp.where(low, _cdot_nt(_heads(q), kh) * decay, 0.0)
    u = _cdot(inv(m), _heads(v * be) - _cdot_nt(_heads(kb * e), s0))
    y = _cdot_nt(_heads(q * e), s0) + _cdot(mr, u)
    s1 = s0 * _heads(jnp.exp(gl)) + _cdot_tn(u, _heads(k * jnp.exp(gl - gc)))
    return _unheads(y), s1


def core_fwd(heads_fn, ins, name, hosted=()):
    T = ins[0].shape[0]
    nc = T // CHUNK
    n = len(ins)
    gather = ChipGather(list(hosted))
    ng = gather.n

    def body(*refs):
        hx = refs[n:n + ng]
        y_ref, s0_ref, t_ref = refs[n + ng:n + ng + 3]
        ho = refs[n + ng + 3:n + 2 * ng + 3]
        s_ref = refs[n + 2 * ng + 3]
        sems = refs[n + 2 * ng + 4:]
        c = pl.program_id(0)

        @pl.when(c == 0)
        def _():
            gather.start(hx, ho, sems)
            s_ref[...] = jnp.zeros_like(s_ref)

        s0 = s_ref[...]
        kept = []

        def inv(m):
            kept.append(_neumann_inv(m))
            return kept[0]

        y, s1 = heads_fn(s0, *[r[...] for r in refs[:n]], inv)
        y_ref[...] = y
        s0_ref[0] = s0
        t_ref[0] = kept[0]
        s_ref[...] = s1

        @pl.when(c == nc - 4)
        def _():
            gather.relay(hx, ho, sems)

        @pl.when(c == nc - 1)
        def _():
            gather.finish(hx, ho, sems)

    row = pl.BlockSpec((CHUNK, MIX_W), lambda c: (c, 0))
    st_shape = (HEADS, HEAD_DIM, HEAD_DIM)
    st = pl.BlockSpec((1,) + st_shape, lambda c: (c, 0, 0, 0))
    res = pl.pallas_call(
        body, name=name, grid=(nc,), in_specs=[row] * n + gather.in_specs,
        out_specs=tuple([row, st, st] + gather.out_specs),
        out_shape=tuple([jax.ShapeDtypeStruct((T, MIX_W), F32), jax.ShapeDtypeStruct((nc,) + st_shape, F32),
                         jax.ShapeDtypeStruct((nc,) + st_shape, F32)] + gather.out_shape),
        scratch_shapes=[pltpu.VMEM(st_shape, F32)] + gather.scratch,
        compiler_params=_params(("arbitrary",)))(*ins, *hosted)
    return res[0], res[1], res[2], list(res[3:])


def core_bwd(heads_fn, ins, s0_all, t_all, dy, name):
    T = ins[0].shape[0]
    nc = T // CHUNK
    n = len(ins)

    def body(*refs):
        s0_ref, t_ref, dy_ref = refs[n:n + 3]
        outs = refs[n + 3:n + 3 + n]
        ds_ref = refs[n + 3 + n]

        @pl.when(pl.program_id(0) == 0)
        def _():
            ds_ref[...] = jnp.zeros_like(ds_ref)

        t_saved = t_ref[0]
        f = lambda s0, *xs: heads_fn(s0, *xs, lambda m: _inv_saved(m, t_saved))
        _, vjp = jax.vjp(f, s0_ref[0], *[r[...] for r in refs[:n]])
        grads = vjp((dy_ref[...], ds_ref[...]))
        ds_ref[...] = grads[0]
        for k in range(n):
            outs[k][...] = grads[1 + k]

    row = pl.BlockSpec((CHUNK, MIX_W), lambda c: (nc - 1 - c, 0))
    st_shape = (HEADS, HEAD_DIM, HEAD_DIM)
    st = pl.BlockSpec((1,) + st_shape, lambda c: (nc - 1 - c, 0, 0, 0))
    return pl.pallas_call(
        body, name=name, grid=(nc,), in_specs=[row] * n + [st, st, row], out_specs=tuple([row] * n),
        out_shape=tuple(jax.ShapeDtypeStruct((T, MIX_W), F32) for _ in range(n)),
        scratch_shapes=[pltpu.VMEM(st_shape, F32)],
        compiler_params=_params(("arbitrary",)))(*ins, s0_all, t_all, dy)


def _block_diag(w):
    out = jnp.zeros((LRU_W, LRU_W), w.dtype)
    for n in range(LRU_BLOCKS):
        out = lax.dynamic_update_slice(out, w[n], (n * 64, n * 64))
    return out


def _block_diag_grad(g):
    return jnp.stack([g[n * 64:(n + 1) * 64, n * 64:(n + 1) * 64] for n in range(LRU_BLOCKS)])


def _row(v):
    return v.reshape(1, -1)


def _pad128(v):
    return jnp.pad(v.reshape(1, -1), ((0, 0), (0, 128 - v.size)))


def _layer_shared(w, l):
    cw = w['lru_conv_w'][l]
    lru_pre = [_row(cw[0]), _row(cw[1]), _row(cw[2]), _row(cw[3]), _row(w['lru_conv_b'][l]),
               _block_diag(w['lru_gate_a_w'][l]), _row(w['lru_gate_a_b'][l]),
               _block_diag(w['lru_gate_x_w'][l]), _row(w['lru_gate_x_b'][l]), _row(w['lru_lambda'][l])]
    rw_pre = [_row(w['rwkv_mu'][l]), w['rwkv_w_up'][l], _row(w['rwkv_w_bias'][l]), w['rwkv_a_up'][l],
              _row(w['rwkv_a_bias'][l]), w['rwkv_g_up'][l], _row(w['rwkv_k_k'][l]), _row(w['rwkv_k_a'][l])]
    if l > 0:
        rw_pre += [w['rwkv_vres_w1'][l - 1], w['rwkv_vres_w2'][l - 1], _row(w['rwkv_vres_b'][l - 1])]
    rw_post = [_row(w['rwkv_ln_g'][l]), _row(w['rwkv_ln_b'][l]), _row(w['rwkv_r_k'][l])]
    gw = w['gdn_conv_w'][l]
    gdn_pre = [_row(gw[0]), _row(gw[1]), _row(gw[2]), _row(gw[3]), _pad128(w['gdn_a_log'][l]),
               _pad128(w['gdn_dt_bias'][l])]
    gdn_post = [_row(jnp.tile(w['gdn_norm'][l], HEADS))]
    return dict(lru_pre=lru_pre, lru_post=[_row(w['lru_out_norm'][l])], rw_pre=rw_pre, rw_post=rw_post,
                gdn_pre=gdn_pre, gdn_post=gdn_post)


def _mixer_fwd(p, sh, l, v_first, host_rwkv=(), host_gdn=()):
    T = p.shape[0]
    lx, ly = p[:, 0:256], p[:, 256:512]
    prw, qkv, z, ab = p[:, 512:1920], p[:, 1920:3072], p[:, 3072:3456], p[:, 3456:3584]
    first = jnp.zeros((T, LRU_W), F32).at[0].set(1.0)
    lru_rows = [lx, shift_rows(lx, 1), shift_rows(lx, 2), shift_rows(lx, 3), first]
    a, b = rowwise_fwd(lru_pre_fn, lru_rows, sh['lru_pre'], (LRU_W, LRU_W), f"lru_pre_fwd{l}")
    hseq = lru_scan(a, b, False, f"lru_scan_fwd{l}")
    (y_lru,) = rowwise_fwd(lru_post_fn, [hseq, ly], sh['lru_post'], (LRU_W,), f"lru_post_fwd{l}")

    rw_rows = [prw, shift_rows(prw, 1)] + ([v_first] if l > 0 else [])
    rw = rowwise_fwd(make_rwkv_pre_fn(l > 0), rw_rows, sh['rw_pre'], (MIX_W,) * 7, f"rwkv_pre_fwd{l}")
    r, lw, k2, v, kk, ar, g = rw
    y_raw, rs0, rt, got_rwkv = core_fwd(rwkv_heads, [r, lw, k2, v, kk, ar], f"rwkv_core_fwd{l}", host_rwkv)
    (y_rw,) = rowwise_fwd(rwkv_post_fn, [y_raw, r, k2, v, g], sh['rw_post'], (MIX_W,), f"rwkv_post_fwd{l}")

    gdn_rows = [qkv, shift_rows(qkv, 1), shift_rows(qkv, 2), shift_rows(qkv, 3), ab]
    gd = rowwise_fwd(gdn_pre_fn, gdn_rows, sh['gdn_pre'], (MIX_W,) * 5, f"gdn_pre_fwd{l}")
    o_raw, gs0, gt, got_gdn = core_fwd(gdn_heads, list(gd), f"gdn_core_fwd{l}", host_gdn)
    (y_gdn,) = rowwise_fwd(gdn_post_fn, [o_raw, z], sh['gdn_post'], (MIX_W,), f"gdn_post_fwd{l}")

    mixed = jnp.concatenate([y_lru, y_rw, y_gdn], axis=1)
    saved = dict(lru_rows=lru_rows, a=a, hseq=hseq, ly=ly, rw_rows=rw_rows, rw=rw, y_raw=y_raw, rs0=rs0, rt=rt,
                 gdn_rows=gdn_rows, gd=gd, o_raw=o_raw, gs0=gs0, gt=gt, z=z)
    v_layer0 = v if l == 0 else None
    return mixed, saved, v_layer0, got_rwkv, got_gdn


def _mixer_bwd(dmixed, sv, sh, l, dv_first):
    d_lru, d_rw, d_gdn = dmixed[:, 0:256], dmixed[:, 256:640], dmixed[:, 640:1024]
    gw = {}

    dh, dly, d_og = rowwise_bwd(lru_post_fn, [sv['hseq'], sv['ly']], sh['lru_post'], [d_lru], f"lru_post_bwd{l}")
    gscan = lru_scan(unshift_rows(sv['a'], 1), dh, True, f"lru_scan_bwd{l}")
    res = rowwise_bwd(lru_pre_fn, sv['lru_rows'], sh['lru_pre'], [gscan, shift_rows(sv['hseq'], 1)],
                      f"lru_pre_bwd{l}", ct_fn=lambda gs, hp: (gs * hp, gs))
    dlx = res[0] + unshift_rows(res[1], 1) + unshift_rows(res[2], 2) + unshift_rows(res[3], 3)
    dw0, dw1, dw2, dw3, dcb, dga, dgab, dgx, dgxb, dlam = res[5:]
    gw['lru_conv_w'] = jnp.concatenate([dw0, dw1, dw2, dw3], axis=0)
    gw['lru_conv_b'] = dcb[0]
    gw['lru_gate_a_w'] = _block_diag_grad(dga)
    gw['lru_gate_a_b'] = dgab.reshape(LRU_BLOCKS, 64)
    gw['lru_gate_x_w'] = _block_diag_grad(dgx)
    gw['lru_gate_x_b'] = dgxb.reshape(LRU_BLOCKS, 64)
    gw['lru_lambda'] = dlam[0]
    gw['lru_out_norm'] = d_og[0]

    r, lw, k2, v, kk, ar, g = sv['rw']
    res = rowwise_bwd(rwkv_post_fn, [sv['y_raw'], r, k2, v, g], sh['rw_post'], [d_rw], f"rwkv_post_bwd{l}")
    dy_raw, dr_p, dk2_p, dv_p, dg = res[:5]
    gw['rwkv_ln_g'], gw['rwkv_ln_b'], gw['rwkv_r_k'] = res[5][0], res[6][0], res[7].reshape(HEADS, HEAD_DIM)
    dr_c, dlw, dk2_c, dv_c, dkk, dar = core_bwd(rwkv_heads, [r, lw, k2, v, kk, ar], sv['rs0'], sv['rt'], dy_raw,
                                                 f"rwkv_core_bwd{l}")
    cts = [dr_p, dr_c, dlw, dk2_p, dk2_c, dv_p, dv_c, dkk, dar, dg]
    if l == 0:
        cts.append(dv_first)
        ct_fn = lambda a1, a2, b, c1, c2, d1, d2, e, f, gg, vf: (a1 + a2, b, c1 + c2, d1 + d2 + vf, e, f, gg)
    else:
        ct_fn = lambda a1, a2, b, c1, c2, d1, d2, e, f, gg: (a1 + a2, b, c1 + c2, d1 + d2, e, f, gg)
    res = rowwise_bwd(make_rwkv_pre_fn(l > 0), sv['rw_rows'], sh['rw_pre'], cts, f"rwkv_pre_bwd{l}", ct_fn=ct_fn)
    dprw = res[0] + unshift_rows(res[1], 1)
    nrow = len(sv['rw_rows'])
    dv_first_out = res[2] if l > 0 else None
    sg = res[nrow:]
    gw['rwkv_mu'], gw['rwkv_w_up'], gw['rwkv_w_bias'], gw['rwkv_a_up'] = sg[0][0], sg[1], sg[2][0], sg[3]
    gw['rwkv_a_bias'], gw['rwkv_g_up'], gw['rwkv_k_k'], gw['rwkv_k_a'] = sg[4][0], sg[5], sg[6][0], sg[7][0]
    if l > 0:
        gw['rwkv_vres_w1'], gw['rwkv_vres_w2'], gw['rwkv_vres_b'] = sg[8], sg[9], sg[10][0]

    do_raw, dz, d_ng = rowwise_bwd(gdn_post_fn, [sv['o_raw'], sv['z']], sh['gdn_post'], [d_gdn], f"gdn_post_bwd{l}")
    gw['gdn_norm'] = jnp.sum(d_ng.reshape(HEADS, HEAD_DIM), axis=0)
    dgd = core_bwd(gdn_heads, list(sv['gd']), sv['gs0'], sv['gt'], do_raw, f"gdn_core_bwd{l}")
    res = rowwise_bwd(gdn_pre_fn, sv['gdn_rows'], sh['gdn_pre'], list(dgd), f"gdn_pre_bwd{l}")
    dqkv = res[0] + unshift_rows(res[1], 1) + unshift_rows(res[2], 2) + unshift_rows(res[3], 3)
    dab = res[4]
    gw['gdn_conv_w'] = jnp.concatenate(res[5:9], axis=0)
    gw['gdn_a_log'], gw['gdn_dt_bias'] = res[9][0, :HEADS], res[10][0, :HEADS]

    dp = jnp.concatenate([dlx, dly, dprw, dqkv, dz, dab], axis=1)
    return dp, gw, dv_first_out


IN_SHARD = D_IN // N_CHIPS
IN_SHARD_PAD = D_IN_PAD // N_CHIPS


def _cols_to_chips(g, n=N_CHIPS):
    r = g.shape[0]
    return jnp.transpose(g.reshape(r, n, -1), (1, 0, 2))


def _cols_from_chips(g):
    return jnp.transpose(g, (1, 0, 2)).reshape(g.shape[1], -1)


def _w_in_from_chips(g):
    nat = _cols_from_chips(g[:, :, :IN_SHARD])
    return jnp.pad(nat, ((0, 0), (0, D_IN_PAD - D_IN)))


def _w_in_to_chips(g):
    return jnp.pad(_cols_to_chips(g[:, :D_IN]), ((0, 0), (0, 0), (0, IN_SHARD_PAD - IN_SHARD)))


def _natural(name, g):
    if name == 'w_in':
        return _w_in_from_chips(g)
    if BIG[name] == 2:
        return _cols_from_chips(g)
    return g.reshape(-1, g.shape[2])


def local_step(x, target, w, wb, shards=None):
    def hosted(keys):
        return [shards[k] for k in keys] if shards is not None else []

    def arrived(keys, gathered):
        for (name, layer), g in zip(keys if shards is not None else [], gathered):
            wb[name][layer] = _natural(name, g)

    saved = []
    v_first = None
    for l in range(N_LAYERS):
        sh = _layer_shared(w, l)
        for_mixer = [('w_in', l), ('w_out', l)]
        for_ffn2 = [('ffn2_wi', l), ('ffn2_wo', l)]
        for_next = [('ffn1_wi', l + 1), ('ffn1_wo', l + 1)] if l + 1 < N_LAYERS else []
        x1, got = ffn_fwd(x, _row(w['ffn1_norm'][l]), wb['ffn1_wi'][l], wb['ffn1_wo'][l], f"ffn1_fwd{l}",
                          hosted(for_mixer))
        arrived(for_mixer, got)
        p = proj_fwd(x1, _row(w['mix_norm'][l]), wb['w_in'][l], f"proj_fwd{l}")
        mixed, sv, v0, got_ffn2, got_next = _mixer_fwd(p, sh, l, v_first, hosted(for_ffn2), hosted(for_next))
        arrived(for_ffn2, got_ffn2)
        arrived(for_next, got_next)
        if l == 0:
            v_first = v0
        x2 = out_fwd(mixed, wb['w_out'][l], x1, f"out_fwd{l}")
        x3, _ = ffn_fwd(x2, _row(w['ffn2_norm'][l]), wb['ffn2_wi'][l], wb['ffn2_wo'][l], f"ffn2_fwd{l}")
        saved.append(dict(x0=x, x1=x1, x2=x2, mixed=mixed, sv=sv, sh=sh))
        x = x3

    loss, dx, dgf = loss_head(x, _row(w['final_norm']), target, "loss_head")
    per_layer = [None] * N_LAYERS
    dv_first = jnp.zeros((x.shape[0], MIX_W), F32)
    for l in reversed(range(N_LAYERS)):
        s = saved[l]
        gw = {}
        dx, dg2, dwg, dwu, dwo = ffn_bwd(s['x2'], dx, _row(w['ffn2_norm'][l]), wb['ffn2_wi'][l], wb['ffn2_wo'][l],
                                         f"ffn2_bwd{l}")
        wi_parts = lambda dwg, dwu: (dwg, dwu)
        row_parts = lambda dw: dw.reshape(N_CHIPS, -1, dw.shape[1])
        gw['ffn2_norm'], gw['ffn2_wi'], gw['ffn2_wo'] = dg2[0], wi_parts(dwg, dwu), row_parts(dwo)
        dmixed, dw_out = out_bwd(s['mixed'], wb['w_out'][l], dx, f"out_bwd{l}")
        gw['w_out'] = row_parts(dw_out)
        dp, gmix, dvf = _mixer_bwd(dmixed, s['sv'], s['sh'], l, dv_first)
        if l > 0:
            dv_first = dvf
        gw.update(gmix)
        dx, dgm, dwin = proj_bwd(s['x1'], dx, _row(w['mix_norm'][l]), wb['w_in'][l], dp, f"proj_bwd{l}")
        gw['mix_norm'], gw['w_in'] = dgm[0], _w_in_to_chips(dwin)
        dx, dg1, dwg, dwu, dwo = ffn_bwd(s['x0'], dx, _row(w['ffn1_norm'][l]), wb['ffn1_wi'][l], wb['ffn1_wo'][l],
                                         f"ffn1_bwd{l}")
        gw['ffn1_norm'], gw['ffn1_wi'], gw['ffn1_wo'] = dg1[0], wi_parts(dwg, dwu), row_parts(dwo)
        per_layer[l] = gw

    grads = {'final_norm': dgf[0]}
    for name in WEIGHTS:
        if name == 'final_norm':
            continue
        if name in BIG:
            grads[name] = [per_layer[l][name] for l in range(N_LAYERS)]
        elif name.startswith('rwkv_vres'):
            grads[name] = per_layer[1][name][None]
        else:
            grads[name] = jnp.stack([per_layer[l][name] for l in range(N_LAYERS)])
    return loss[0, 0], dx, grads


ANY = pl.BlockSpec(memory_space=pl.ANY)


def _coords():
    return lax.axis_index("x"), lax.axis_index("y"), lax.axis_index("c")


def _other_chips(x, y):
    return [((x + 1) % 2, y), (x, (y + 1) % 2), ((x + 1) % 2, (y + 1) % 2)]


def allreduce_small(pack, name):
    R = pack.shape[0]
    rh = R // 2

    def body(x_ref, o_ref, sib_ref, chip_ref, parts_ref, send_sems, recv_sems):
        x, y, c = _coords()
        sib = (x, y, 1 - c)

        def copy(k, src, dst, to):
            return pltpu.make_async_remote_copy(src_ref=src, dst_ref=dst, send_sem=send_sems.at[k],
                                                recv_sem=recv_sems.at[k], device_id=to, device_id_type=MESH)

        swap = copy(0, x_ref, sib_ref, sib)
        swap.start()
        swap.wait()
        chip_ref[...] = jnp.where(c == 0, x_ref[...], sib_ref[...]) + jnp.where(c == 0, sib_ref[...], x_ref[...])

        mine = pl.ds(pl.multiple_of(c * rh, 8), rh)
        sends = [copy(1 + j, chip_ref.at[mine], parts_ref.at[j], (px, py, c))
                 for j, (px, py) in enumerate(_other_chips(x, y))]
        for cp in sends:
            cp.start()
        for cp in sends:
            cp.wait()
        s = 2 * x + y
        own = chip_ref[mine, :]
        from_chip = {2: parts_ref[0], 1: parts_ref[1], 3: parts_ref[2]}
        terms = []
        for k in range(N_CHIPS):
            t = own
            for d, part in from_chip.items():
                t = jnp.where(jnp.bitwise_xor(s, d) == k, part, t)
            terms.append(t)
        o_ref[mine, :] = ((terms[0] + terms[1]) + terms[2]) + terms[3]

        share = copy(4, o_ref.at[mine], o_ref.at[mine], sib)
        share.start()
        share.wait()

    vm = pl.BlockSpec(memory_space=pltpu.VMEM)
    return pl.pallas_call(
        body, name=name, in_specs=[vm], out_specs=vm, out_shape=jax.ShapeDtypeStruct((R, 128), F32),
        scratch_shapes=[pltpu.VMEM((R, 128), F32), pltpu.VMEM((R, 128), F32), pltpu.VMEM((3, rh, 128), F32),
                        pltpu.SemaphoreType.DMA((5,)), pltpu.SemaphoreType.DMA((5,))],
        compiler_params=_params())(pack)


class ChipGather:
    def __init__(self, shards):
        self.shapes = [s.shape for s in shards]
        self.n = len(shards)
        self.in_specs = [ANY] * self.n
        self.out_specs = [ANY] * self.n
        self.out_shape = [jax.ShapeDtypeStruct((N_CHIPS,) + s.shape, s.dtype) for s in shards]
        self.scratch = [pltpu.SemaphoreType.DMA((6 * self.n,)), pltpu.SemaphoreType.DMA((6 * self.n,)),
                        pltpu.SemaphoreType.DMA((self.n,))] if self.n else []

    def _rows(self, a, core):
        rh = self.shapes[a][0] // 2
        return pl.ds(pl.multiple_of(core * rh, 16), rh)

    def _copies(self, kind, x_refs, o_refs, sems):
        send_sems, recv_sems, local_sems = sems
        x, y, c = _coords()
        s_me = 2 * x + y
        sib = (x, y, 1 - c)

        def copy(a, k, src, dst, to):
            return pltpu.make_async_remote_copy(src_ref=src, dst_ref=dst, send_sem=send_sems.at[6 * a + k],
                                                recv_sem=recv_sems.at[6 * a + k], device_id=to, device_id_type=MESH)

        if kind == 'own':
            return [pltpu.make_async_copy(x_refs[a], o_refs[a].at[s_me], local_sems.at[a]) for a in range(self.n)]
        out = []
        for j, (px, py) in enumerate(_other_chips(x, y)):
            for a in range(self.n):
                mine = self._rows(a, c)
                part = o_refs[a].at[2 * px + py, mine]
                if kind == 'sent':
                    out.append(copy(a, j, x_refs[a].at[mine], o_refs[a].at[s_me, mine], (px, py, c)))
                elif kind == 'arrived':
                    out.append(copy(a, j, part, part, (px, py, c)))
                elif kind == 'passed':
                    out.append(copy(a, 3 + j, part, part, sib))
                else:
                    theirs = o_refs[a].at[2 * px + py, self._rows(a, 1 - c)]
                    out.append(copy(a, 3 + j, theirs, theirs, sib))
        return out

    def start(self, x_refs, o_refs, sems):
        if not self.n:
            return
        for cp in self._copies('own', x_refs, o_refs, sems) + self._copies('sent', x_refs, o_refs, sems):
            cp.start()

    def relay(self, x_refs, o_refs, sems):
        if not self.n:
            return
        for got, fw in zip(self._copies('arrived', x_refs, o_refs, sems),
                           self._copies('passed', x_refs, o_refs, sems)):
            got.wait_recv()
            fw.start()

    def finish(self, x_refs, o_refs, sems):
        if not self.n:
            return
        for cp in self._copies('from_sibling', x_refs, o_refs, sems):
            cp.wait_recv()
        for cp in self._copies('sent', x_refs, o_refs, sems) + self._copies('passed', x_refs, o_refs, sems):
            cp.wait_send()
        for cp in self._copies('own', x_refs, o_refs, sems):
            cp.wait()


def allgather_chips(shards, name):
    gather = ChipGather(shards)
    n = gather.n

    def body(*refs):
        x_refs, o_refs, sems = refs[:n], refs[n:2 * n], refs[2 * n:]
        gather.start(x_refs, o_refs, sems)
        gather.relay(x_refs, o_refs, sems)
        gather.finish(x_refs, o_refs, sems)

    return pl.pallas_call(
        body, name=name, in_specs=gather.in_specs, out_specs=tuple(gather.out_specs),
        out_shape=tuple(gather.out_shape), scratch_shapes=gather.scratch, compiler_params=_params())(*shards)


def sibling_swap(srcs, halves, name):
    n = len(srcs)
    row_axis = [s.ndim - 2 for s in srcs]
    out_shapes = [s.shape[:ax] + (s.shape[ax] // 2,) + s.shape[ax + 1:] if halves else s.shape
                  for s, ax in zip(srcs, row_axis)]

    def body(*refs):
        x_refs, o_refs = refs[:n], refs[n:2 * n]
        send_sems, recv_sems = refs[2 * n:]
        x, y, c = _coords()
        copies = []
        for a in range(n):
            part = x_refs[a]
            if halves:
                rh = srcs[a].shape[row_axis[a]] // 2
                theirs = pl.ds(pl.multiple_of((1 - c) * rh, 16), rh)
                part = part.at[:, theirs] if row_axis[a] == 1 else part.at[theirs]
            cp = pltpu.make_async_remote_copy(src_ref=part, dst_ref=o_refs[a], send_sem=send_sems.at[a],
                                              recv_sem=recv_sems.at[a], device_id=(x, y, 1 - c), device_id_type=MESH)
            cp.start()
            copies.append(cp)
        for cp in copies:
            cp.wait()

    return pl.pallas_call(
        body, name=name, in_specs=[ANY] * n, out_specs=tuple([ANY] * n),
        out_shape=tuple(jax.ShapeDtypeStruct(sh, s.dtype) for sh, s in zip(out_shapes, srcs)),
        scratch_shapes=[pltpu.SemaphoreType.DMA((n,)), pltpu.SemaphoreType.DMA((n,))],
        compiler_params=_params())(*srcs)


def scatter_chips(parts, name):
    n = len(parts)

    def body(*refs):
        x_refs, o_refs = refs[:n], refs[n:2 * n]
        send_sems, recv_sems = refs[2 * n:]
        x, y, c = _coords()
        copies = []
        for j, (px, py) in enumerate(_other_chips(x, y)):
            for a in range(n):
                cp = pltpu.make_async_remote_copy(src_ref=x_refs[a].at[2 * px + py], dst_ref=o_refs[a].at[j],
                                                  send_sem=send_sems.at[3 * a + j], recv_sem=recv_sems.at[3 * a + j],
                                                  device_id=(px, py, c), device_id_type=MESH)
                cp.start()
                copies.append(cp)
        for cp in copies:
            cp.wait()

    return pl.pallas_call(
        body, name=name, in_specs=[ANY] * n, out_specs=tuple([ANY] * n),
        out_shape=tuple(jax.ShapeDtypeStruct((3,) + p.shape[1:], p.dtype) for p in parts),
        scratch_shapes=[pltpu.SemaphoreType.DMA((3 * n,)), pltpu.SemaphoreType.DMA((3 * n,))],
        compiler_params=_params())(*parts)


def _row_block(rows):
    return max(b for b in range(16, 257, 16) if rows % b == 0)


def chip_sum(gpack, recv, core, name):
    n, R, W = gpack.shape
    rh = R // 2
    rb = _row_block(rh)
    nb = rh // rb

    def body(c_ref, g_ref, r_ref, o_ref, ob_ref):
        s = g_ref[...] + r_ref[...]
        o_ref[...] = s
        ob_ref[...] = s.astype(BF16)

    blk = pl.BlockSpec((1, rb, W), lambda i, j, c_ref: (i, j, 0))
    spec = pltpu.PrefetchScalarGridSpec(
        num_scalar_prefetch=1, grid=(n, nb),
        in_specs=[pl.BlockSpec((1, rb, W), lambda i, j, c_ref: (i, c_ref[0] * nb + j, 0)), blk],
        out_specs=(blk, blk))
    return pl.pallas_call(
        body, name=name, grid_spec=spec,
        out_shape=(jax.ShapeDtypeStruct((n, rh, W), F32), jax.ShapeDtypeStruct((n, rh, W), BF16)),
        compiler_params=_params(("arbitrary", "arbitrary")))(core, gpack, recv)


def chip_sum_cols(gate, up, recv_gate, recv_up, core, name):
    R, W = gate.shape
    cw = W // 2
    rh = R // 2
    rb = _row_block(rh)
    nb = rh // rb

    def body(c_ref, g_ref, u_ref, rg_ref, ru_ref, o_ref, ob_ref):
        s = jnp.where(pl.program_id(0) < 2, g_ref[...] + rg_ref[...], u_ref[...] + ru_ref[...])
        o_ref[0] = s
        ob_ref[0] = s.astype(BF16)

    gate_col = lambda s: jnp.minimum(s, 1)
    up_col = lambda s: jnp.maximum(s - 2, 0)
    out = pl.BlockSpec((1, rb, cw), lambda s, j, c_ref: (s, j, 0))
    spec = pltpu.PrefetchScalarGridSpec(
        num_scalar_prefetch=1, grid=(N_CHIPS, nb),
        in_specs=[pl.BlockSpec((rb, cw), lambda s, j, c_ref: (c_ref[0] * nb + j, gate_col(s))),
                  pl.BlockSpec((rb, cw), lambda s, j, c_ref: (c_ref[0] * nb + j, up_col(s))),
                  pl.BlockSpec((rb, cw), lambda s, j, c_ref: (j, gate_col(s))),
                  pl.BlockSpec((rb, cw), lambda s, j, c_ref: (j, up_col(s)))],
        out_specs=(out, out))
    return pl.pallas_call(
        body, name=name, grid_spec=spec,
        out_shape=(jax.ShapeDtypeStruct((N_CHIPS, rh, cw), F32), jax.ShapeDtypeStruct((N_CHIPS, rh, cw), BF16)),
        compiler_params=_params(("arbitrary", "arbitrary")))(core, gate, up, recv_gate, recv_up)


def shard_sum(own, recv, name):
    R, W = own.shape
    rb = _row_block(R)

    def body(a_ref, r_ref, o_ref):
        acc = a_ref[...]
        for j in range(3):
            acc = acc + r_ref[j].astype(F32)
        o_ref[...] = acc

    return pl.pallas_call(
        body, name=name, grid=(R // rb,),
        in_specs=[pl.BlockSpec((rb, W), lambda i: (i, 0)), pl.BlockSpec((3, rb, W), lambda i: (0, i, 0))],
        out_specs=pl.BlockSpec((rb, W), lambda i: (i, 0)), out_shape=jax.ShapeDtypeStruct((R, W), F32),
        compiler_params=_params(("arbitrary",)))(own, recv)


def adamw(w, m, v, g, name):
    L, R, C = w.shape
    rb = max(b for b in range(8, 257, 8) if R % b == 0)
    bc1 = 1.0 - ADAM_B1 ** ADAM_STEP
    bc2 = 1.0 - ADAM_B2 ** ADAM_STEP

    def body(w_ref, m_ref, v_ref, g_ref, d_ref, nm_ref, nv_ref):
        gv = g_ref[...]
        nm = ADAM_B1 * m_ref[...] + (1.0 - ADAM_B1) * gv
        nv = ADAM_B2 * v_ref[...] + (1.0 - ADAM_B2) * (gv * gv)
        d_ref[...] = -ADAM_LR * ((nm / bc1) / (jnp.sqrt(nv / bc2) + ADAM_EPS) + ADAM_WD * w_ref[...])
        nm_ref[...] = nm
        nv_ref[...] = nv

    blk = pl.BlockSpec((1, rb, C), lambda l, i: (l, i, 0))
    sh = jax.ShapeDtypeStruct((L, R, C), F32)
    return pl.pallas_call(body, name=name, grid=(L, R // rb), in_specs=[blk] * 4, out_specs=(blk,) * 3,
                          out_shape=(sh, sh, sh), compiler_params=_params(("arbitrary", "arbitrary")))(w, m, v, g)


SMALL = [n for n in WEIGHTS if n not in BIG]


PACK_TILE = 8 * 128


def _pack(arrays):
    blocks = []
    for a in arrays:
        flat = a.reshape(-1)
        flat = jnp.pad(flat, (0, -flat.size % PACK_TILE))
        blocks.append(flat.reshape(-1, 128))
    rows = sum(b.shape[0] for b in blocks)
    if rows % 16:
        blocks.append(jnp.zeros((8, 128), arrays[0].dtype))
    return jnp.concatenate(blocks, axis=0)


def _unpack(pack, shapes):
    out, row = [], 0
    for shape in shapes:
        size = int(np.prod(shape))
        rows = -(-size // PACK_TILE) * 8
        out.append(pack[row:row + rows].reshape(-1)[:size].reshape(shape))
        row += rows
    return out


def _pad_lanes(a):
    return jnp.pad(a, ((0, 0), (0, -a.shape[1] % 128)))


def _local_shard(full, axis, chip):
    size = full.shape[axis] // N_CHIPS
    return lax.dynamic_slice_in_dim(full, chip * size, size, axis)


def kernel(x, ffn1_norm, ffn1_wi, ffn1_wo, mix_norm, w_in, w_out, lru_conv_w, lru_conv_b, lru_gate_a_w, lru_gate_a_b, lru_gate_x_w, lru_gate_x_b, lru_lambda, lru_out_norm, rwkv_mu, rwkv_w_up, rwkv_w_bias, rwkv_a_up, rwkv_a_bias, rwkv_g_up, rwkv_k_k, rwkv_k_a, rwkv_r_k, rwkv_ln_g, rwkv_ln_b, rwkv_vres_w1, rwkv_vres_w2, rwkv_vres_b, gdn_conv_w, gdn_a_log, gdn_dt_bias, gdn_norm, ffn2_norm, ffn2_wi, ffn2_wo, final_norm, loss_target, m_ffn1_norm, m_ffn1_wi, m_ffn1_wo, m_mix_norm, m_w_in, m_w_out, m_lru_conv_w, m_lru_conv_b, m_lru_gate_a_w, m_lru_gate_a_b, m_lru_gate_x_w, m_lru_gate_x_b, m_lru_lambda, m_lru_out_norm, m_rwkv_mu, m_rwkv_w_up, m_rwkv_w_bias, m_rwkv_a_up, m_rwkv_a_bias, m_rwkv_g_up, m_rwkv_k_k, m_rwkv_k_a, m_rwkv_r_k, m_rwkv_ln_g, m_rwkv_ln_b, m_rwkv_vres_w1, m_rwkv_vres_w2, m_rwkv_vres_b, m_gdn_conv_w, m_gdn_a_log, m_gdn_dt_bias, m_gdn_norm, m_ffn2_norm, m_ffn2_wi, m_ffn2_wo, m_final_norm, v_ffn1_norm, v_ffn1_wi, v_ffn1_wo, v_mix_norm, v_w_in, v_w_out, v_lru_conv_w, v_lru_conv_b, v_lru_gate_a_w, v_lru_gate_a_b, v_lru_gate_x_w, v_lru_gate_x_b, v_lru_lambda, v_lru_out_norm, v_rwkv_mu, v_rwkv_w_up, v_rwkv_w_bias, v_rwkv_a_up, v_rwkv_a_bias, v_rwkv_g_up, v_rwkv_k_k, v_rwkv_k_a, v_rwkv_r_k, v_rwkv_ln_g, v_rwkv_ln_b, v_rwkv_vres_w1, v_rwkv_vres_w2, v_rwkv_vres_b, v_gdn_conv_w, v_gdn_a_log, v_gdn_dt_bias, v_gdn_norm, v_ffn2_norm, v_ffn2_wi, v_ffn2_wo, v_final_norm):
    args = locals()
    w_loc = {n: args[n] for n in WEIGHTS}
    m_loc = {n: args['m_' + n] for n in WEIGHTS}
    v_loc = {n: args['v_' + n] for n in WEIGHTS}
    chip = 2 * lax.axis_index("x") + lax.axis_index("y")
    core = lax.axis_index("c")

    big = [(n, l) for n in BIG for l in range(N_LAYERS)]
    shards = {(n, l): _pad_lanes(w_loc[n][l].astype(BF16)) for n, l in big}
    first = [('ffn1_wi', 0), ('ffn1_wo', 0)]
    wb = {n: [None] * N_LAYERS for n in BIG}
    for (n, l), g in zip(first, allgather_chips([shards[k] for k in first], "allgather_first")):
        wb[n][l] = _natural(n, g)

    sm_names = list(SMALL_SHARDED)
    placed = []
    for n in sm_names:
        ax = SMALL_SHARDED[n]
        full_shape = w_loc[n].shape[:ax] + (N_CHIPS * w_loc[n].shape[ax],) + w_loc[n].shape[ax + 1:]
        src = w_loc[n] * (core == 0).astype(F32)
        placed.append(lax.dynamic_update_slice_in_dim(jnp.zeros(full_shape, F32), src, chip * w_loc[n].shape[ax], ax))
    summed = allreduce_small(_pack(placed), "allgather_small")
    w_full = dict(w_loc)
    w_full.update(zip(sm_names, _unpack(summed, [p.shape for p in placed])))

    loss, dx, grads = local_step(x[0], loss_target[0], w_full, wb, shards)
    loss = lax.psum(loss, ("x", "y", "c"))

    gsum = allreduce_small(_pack([grads[n] for n in SMALL]), "allreduce_small")
    g_loc = {}
    for n, g in zip(SMALL, _unpack(gsum, [grads[n].shape for n in SMALL])):
        g_loc[n] = _local_shard(g, SMALL_SHARDED[n], chip) if n in SMALL_SHARDED else g

    parts = []
    for n, l in big:
        parts += list(grads[n][l]) if isinstance(grads[n][l], tuple) else [grads[n][l]]
    swapped = iter(zip(parts, sibling_swap(parts, True, "grad_swap_cores")))
    core_arg = core.reshape(1).astype(jnp.int32)
    sums = []
    for n, l in big:
        if isinstance(grads[n][l], tuple):
            (dwg, from_g), (dwu, from_u) = next(swapped), next(swapped)
            sums.append(chip_sum_cols(dwg, dwu, from_g, from_u, core_arg, f"grad_chip_sum_{n}{l}"))
        else:
            p, r = next(swapped)
            sums.append(chip_sum(p, r, core_arg, f"grad_chip_sum_{n}{l}"))
    from_chips = scatter_chips([s[1] for s in sums], "grad_scatter")
    halves = [shard_sum(lax.dynamic_index_in_dim(s[0], chip, 0, keepdims=False), r, f"grad_shard_sum_{n}{l}")
              for (n, l), s, r in zip(big, sums, from_chips)]
    others = sibling_swap(halves, False, "grad_share_cores")
    rows = {n: [None] * N_LAYERS for n in BIG}
    for (n, l), half, other in zip(big, halves, others):
        lower = jnp.where(core == 0, half, other)
        upper = jnp.where(core == 0, other, half)
        rows[n][l] = jnp.concatenate([lower, upper], axis=0)[:, :w_loc[n].shape[-1]]
    big_names = list(BIG)
    for n in big_names:
        g_loc[n] = jnp.stack(rows[n])

    delta, new_m, new_v = {}, {}, {}
    for n in big_names:
        delta[n], new_m[n], new_v[n] = adamw(w_loc[n], m_loc[n], v_loc[n], g_loc[n], f"adamw_{n}")
    pack = lambda d: _pack([d[n] for n in SMALL])[None]
    res = adamw(pack(w_loc), pack(m_loc), pack(v_loc), pack(g_loc), "adamw_small")
    for dst, r in zip((delta, new_m, new_v), res):
        dst.update(zip(SMALL, _unpack(r[0], [w_loc[n].shape for n in SMALL])))

    return (loss, dx[None], *[g_loc[n] for n in WEIGHTS], *[delta[n] for n in WEIGHTS],
            *[new_m[n] for n in WEIGHTS], *[new_v[n] for n in WEIGHTS])
```

```python
import functools

import numpy as np
import jax
import jax.numpy as jnp
from jax import lax
from jax.experimental import pallas as pl
from jax.experimental.pallas import tpu as pltpu

F32 = jnp.float32
BF16 = jnp.bfloat16
MESH = pl.DeviceIdType.MESH

D_MODEL = 1024
D_FF = 2816
N_LAYERS = 2
HEADS = 6
HEAD_DIM = 64
MIX_W = HEADS * HEAD_DIM
LRU_W = 256
LRU_BLOCKS = 4
D_IN = 3468
D_IN_PAD = 3584
NORM_EPS = 1e-6
GN_EPS = 64e-5
LRU_C = 8.0
CHUNK = 64
ROWS = 256
FF_CHUNK = 256
IN_CHUNK = 512
VMEM_LIMIT = 56 * 1024 * 1024

ADAM_LR, ADAM_B1, ADAM_B2, ADAM_EPS, ADAM_WD, ADAM_STEP = 0.001, 0.9, 0.999, 1e-08, 0.01, 10

WEIGHTS = ['ffn1_norm', 'ffn1_wi', 'ffn1_wo', 'mix_norm', 'w_in', 'w_out', 'lru_conv_w', 'lru_conv_b',
           'lru_gate_a_w', 'lru_gate_a_b', 'lru_gate_x_w', 'lru_gate_x_b', 'lru_lambda', 'lru_out_norm',
           'rwkv_mu', 'rwkv_w_up', 'rwkv_w_bias', 'rwkv_a_up', 'rwkv_a_bias', 'rwkv_g_up', 'rwkv_k_k',
           'rwkv_k_a', 'rwkv_r_k', 'rwkv_ln_g', 'rwkv_ln_b', 'rwkv_vres_w1', 'rwkv_vres_w2', 'rwkv_vres_b',
           'gdn_conv_w', 'gdn_a_log', 'gdn_dt_bias', 'gdn_norm', 'ffn2_norm', 'ffn2_wi', 'ffn2_wo', 'final_norm']
BIG = {'ffn1_wi': 2, 'ffn1_wo': 1, 'w_in': 2, 'w_out': 1, 'ffn2_wi': 2, 'ffn2_wo': 1}
SMALL_SHARDED = {'lru_conv_w': 2, 'rwkv_w_up': 2, 'rwkv_a_up': 2, 'rwkv_g_up': 2, 'rwkv_vres_w1': 1,
                 'rwkv_vres_w2': 2, 'gdn_conv_w': 2}
N_CHIPS = 4


def _params(sem=None):
    kw = dict(vmem_limit_bytes=VMEM_LIMIT)
    if sem is not None:
        kw['dimension_semantics'] = sem
    return pltpu.CompilerParams(**kw)


def _bdot(a, b, dims=(((1,), (0,)), ((), ()))):
    return lax.dot_general(a.astype(BF16), b.astype(BF16), dims, preferred_element_type=F32)


def _bdot_nt(a, b):
    return _bdot(a, b, (((1,), (1,)), ((), ())))


def _bdot_tn(a, b):
    return _bdot(a, b, (((0,), (0,)), ((), ())))


_DIMS = {'nn': (((1,), (0,)), ((), ())), 'nt': (((1,), (1,)), ((), ())), 'tn': (((0,), (0,)), ((), ()))}


def _split(a, terms):
    parts = []
    for _ in range(terms - 1):
        hi = a.astype(BF16)
        parts.append(hi)
        a = a - hi.astype(F32)
    parts.append(a.astype(BF16))
    return parts


_BATCH_DIMS = {'nn': (((2,), (1,)), ((0,), (0,))), 'nt': (((2,), (2,)), ((0,), (0,))),
               'tn': (((1,), (1,)), ((0,), (0,)))}


def _dot3(a, b, kind):
    ah, al = _split(a, 2)
    bh, bl = _split(b, 2)
    dims = _BATCH_DIMS[kind] if a.ndim == 3 else _DIMS[kind]
    d = lambda p, q: lax.dot_general(p, q, dims, preferred_element_type=F32)
    return d(ah, bh) + (d(ah, bl) + d(al, bh))


@functools.partial(jax.custom_vjp, nondiff_argnums=(2,))
def _cdot_k(a, b, kind):
    return _dot3(a, b, kind)


def _cdot_k_fwd(a, b, kind):
    return _dot3(a, b, kind), (a, b)


def _cdot_k_bwd(kind, res, ct):
    a, b = res
    if kind == 'nn':
        return _dot3(ct, b, 'nt'), _dot3(a, ct, 'tn')
    if kind == 'nt':
        return _dot3(ct, b, 'nn'), _dot3(ct, a, 'tn')
    return _dot3(b, ct, 'nt'), _dot3(a, ct, 'nn')


_cdot_k.defvjp(_cdot_k_fwd, _cdot_k_bwd)


def _dot1(a, b, kind):
    dims = _BATCH_DIMS[kind] if a.ndim == 3 else _DIMS[kind]
    return lax.dot_general(a.astype(BF16), b.astype(BF16), dims, preferred_element_type=F32)


@functools.partial(jax.custom_vjp, nondiff_argnums=(2,))
def _cdot1_k(a, b, kind):
    return _dot1(a, b, kind)


def _cdot1_k_fwd(a, b, kind):
    return _dot1(a, b, kind), (a, b)


def _cdot1_k_bwd(kind, res, ct):
    a, b = res
    if kind == 'nn':
        return _dot1(ct, b, 'nt'), _dot1(a, ct, 'tn')
    if kind == 'nt':
        return _dot1(ct, b, 'nn'), _dot1(ct, a, 'tn')
    return _dot1(b, ct, 'nt'), _dot1(a, ct, 'nn')


_cdot1_k.defvjp(_cdot1_k_fwd, _cdot1_k_bwd)


def _cdot(a, b):
    return _cdot1_k(a, b, 'nn')


def _cdot_nt(a, b):
    return _cdot1_k(a, b, 'nt')


def _cdot_tn(a, b):
    return _cdot1_k(a, b, 'tn')


def _hdot(a, b):
    return _cdot_k(a, b, 'nn')


def _dot_exact(x, m01, kind):
    d = lambda p: lax.dot_general(p, m01.astype(BF16), _DIMS[kind], preferred_element_type=F32)
    hi, mid, lo = _split(x, 3)
    return d(hi) + (d(mid) + d(lo))


@functools.partial(jax.custom_vjp, nondiff_argnums=(1,))
def _xdot(x, make_m):
    return _dot_exact(x, make_m(), 'nn')


def _xdot_fwd(x, make_m):
    return _dot_exact(x, make_m(), 'nn'), None


def _xdot_bwd(make_m, _, ct):
    return (_dot_exact(ct, make_m(), 'nt'),)


_xdot.defvjp(_xdot_fwd, _xdot_bwd)


def _iota2(n, m):
    return lax.broadcasted_iota(jnp.int32, (n, m), 0), lax.broadcasted_iota(jnp.int32, (n, m), 1)


def _head_blocks(w):
    ri, ci = _iota2(w, w)
    return (ri // HEAD_DIM == ci // HEAD_DIM).astype(F32)


def _segsum(x):
    return _xdot(x, functools.partial(_head_blocks, x.shape[-1]))


def _cumsum_rows(x):
    return _cumsum_k(x, x.shape[0])


@functools.partial(jax.custom_vjp, nondiff_argnums=(1,))
def _cumsum_k(x, n):
    return _lower_dot(x, n, False)


def _lower_dot(x, n, transpose):
    ri, ci = _iota2(n, n)
    m = ((ri <= ci) if transpose else (ri >= ci)).astype(BF16)
    d = lambda p: lax.dot_general(m, p, _DIMS['nn'], preferred_element_type=F32)
    hi, mid, lo = _split(x, 3)
    return d(hi) + (d(mid) + d(lo))


def _cumsum_k_fwd(x, n):
    return _lower_dot(x, n, False), None


def _cumsum_k_bwd(n, _, ct):
    return (_lower_dot(ct, n, True),)


_cumsum_k.defvjp(_cumsum_k_fwd, _cumsum_k_bwd)


def _rms(x, g):
    return x * lax.rsqrt(jnp.mean(x * x, axis=-1, keepdims=True) + NORM_EPS) * g


DENSE_ROWS = 1024


def _row_loop(n_rows, fn):
    rows = min(DENSE_ROWS, n_rows)

    def step(i, c):
        fn(pl.ds(pl.multiple_of(i * rows, rows), rows))
        return c
    lax.fori_loop(0, n_rows // rows, step, 0)


def ffn_fwd(x, g, wi, wo, name, hosted=()):
    T = x.shape[0]
    nj = D_FF // FF_CHUNK
    gather = ChipGather(list(hosted))
    n = gather.n

    def body(*refs):
        x_ref, g_ref, wg_ref, wu_ref, wo_ref = refs[:5]
        hx, o_ref, ho = refs[5:5 + n], refs[5 + n], refs[6 + n:6 + 2 * n]
        h_ref, acc_ref = refs[6 + 2 * n:8 + 2 * n]
        sems = refs[8 + 2 * n:]
        j = pl.program_id(0)

        @pl.when(j == 0)
        def _():
            gather.start(hx, ho, sems)

            def init(r):
                h_ref[r, :] = _rms(x_ref[r, :], g_ref[...]).astype(BF16)
                acc_ref[r, :] = jnp.zeros((r.size, D_MODEL), F32)
            _row_loop(T, init)

        def blk(r):
            hb = h_ref[r, :]
            gate = jnp.dot(hb, wg_ref[...], preferred_element_type=F32)
            up = jnp.dot(hb, wu_ref[...], preferred_element_type=F32)
            a = (gate * jax.nn.sigmoid(gate) * up).astype(BF16)
            acc_ref[r, :] += jnp.dot(a, wo_ref[...], preferred_element_type=F32)
        _row_loop(T, blk)

        @pl.when(j == nj - 2)
        def _():
            gather.relay(hx, ho, sems)

        @pl.when(j == nj - 1)
        def _():
            def fin(r):
                o_ref[r, :] = x_ref[r, :] + 0.5 * acc_ref[r, :]
            _row_loop(T, fin)
            gather.finish(hx, ho, sems)

    full = pl.BlockSpec((T, D_MODEL), lambda j: (0, 0))
    res = pl.pallas_call(
        body, name=name, grid=(nj,),
        in_specs=[full, pl.BlockSpec((1, D_MODEL), lambda j: (0, 0)),
                  pl.BlockSpec((D_MODEL, FF_CHUNK), lambda j: (0, j)),
                  pl.BlockSpec((D_MODEL, FF_CHUNK), lambda j: (0, j + nj)),
                  pl.BlockSpec((FF_CHUNK, D_MODEL), lambda j: (j, 0))] + gather.in_specs,
        out_specs=tuple([full] + gather.out_specs),
        out_shape=tuple([jax.ShapeDtypeStruct((T, D_MODEL), F32)] + gather.out_shape),
        scratch_shapes=[pltpu.VMEM((T, D_MODEL), BF16), pltpu.VMEM((T, D_MODEL), F32)] + gather.scratch,
        compiler_params=_params(("arbitrary",)))(x, g, wi, wi, wo, *hosted)
    return res[0], list(res[1:])


def _norm_bwd_rows(x, g, dh, dres):
    rstd = lax.rsqrt(jnp.mean(x * x, axis=-1, keepdims=True) + NORM_EPS)
    xh = x * rstd
    dxh = dh * g
    dx = rstd * (dxh - xh * jnp.mean(dxh * xh, axis=-1, keepdims=True))
    return dres + dx, jnp.sum(dh * xh, axis=0, keepdims=True)


def ffn_bwd(x, dy, g, wi, wo, name, hosted=()):
    T = x.shape[0]
    nj = D_FF // FF_CHUNK
    scatter = ChipScatter(list(hosted))
    n = scatter.n

    def body(*refs):
        x_ref, dy_ref, g_ref, wg_ref, wu_ref, wo_ref = refs[:6]
        hx = refs[6:6 + n]
        dx_ref, dg_ref, dwg_ref, dwu_ref, dwo_ref = refs[6 + n:11 + n]
        ho = refs[11 + n:11 + 2 * n]
        h_ref, da_ref, dh_ref = refs[11 + 2 * n:14 + 2 * n]
        sems = refs[14 + 2 * n:]
        j = pl.program_id(0)

        @pl.when(j == 0)
        def _():
            scatter.start(hx, ho, sems)

            def init(r):
                h_ref[r, :] = _rms(x_ref[r, :], g_ref[...]).astype(BF16)
                da_ref[r, :] = (0.5 * dy_ref[r, :]).astype(BF16)
                dh_ref[r, :] = jnp.zeros((r.size, D_MODEL), F32)
            _row_loop(T, init)

        dwg_ref[...] = jnp.zeros_like(dwg_ref)
        dwu_ref[...] = jnp.zeros_like(dwu_ref)
        dwo_ref[...] = jnp.zeros_like(dwo_ref)

        def blk(r):
            hb = h_ref[r, :]
            db = da_ref[r, :]
            gate = jnp.dot(hb, wg_ref[...], preferred_element_type=F32)
            up = jnp.dot(hb, wu_ref[...], preferred_element_type=F32)
            sg = jax.nn.sigmoid(gate)
            sl = gate * sg
            da = _bdot_nt(db, wo_ref[...])
            dup = (da * sl).astype(BF16)
            dgate = (da * up * (sg * (1.0 + gate * (1.0 - sg)))).astype(BF16)
            dwo_ref[...] += _bdot_tn((sl * up).astype(BF16), db)
            dwg_ref[...] += _bdot_tn(hb, dgate)
            dwu_ref[...] += _bdot_tn(hb, dup)
            dh_ref[r, :] += _bdot_nt(dgate, wg_ref[...]) + _bdot_nt(dup, wu_ref[...])
        _row_loop(T, blk)

        @pl.when(j == nj - 1)
        def _():
            dg_ref[...] = jnp.zeros_like(dg_ref)

            def fin(r):
                dx, dg = _norm_bwd_rows(x_ref[r, :], g_ref[...], dh_ref[r, :], dy_ref[r, :])
                dx_ref[r, :] = dx
                dg_ref[...] += dg
            _row_loop(T, fin)
            scatter.finish(hx, ho, sems)

    full = pl.BlockSpec((T, D_MODEL), lambda j: (0, 0))
    vec = pl.BlockSpec((1, D_MODEL), lambda j: (0, 0))
    res = pl.pallas_call(
        body, name=name, grid=(nj,),
        in_specs=[full, full, vec,
                  pl.BlockSpec((D_MODEL, FF_CHUNK), lambda j: (0, j)),
                  pl.BlockSpec((D_MODEL, FF_CHUNK), lambda j: (0, j + nj)),
                  pl.BlockSpec((FF_CHUNK, D_MODEL), lambda j: (j, 0))] + scatter.in_specs,
        out_specs=tuple([full, vec,
                         pl.BlockSpec((D_MODEL, FF_CHUNK), lambda j: (0, j)),
                         pl.BlockSpec((D_MODEL, FF_CHUNK), lambda j: (0, j)),
                         pl.BlockSpec((FF_CHUNK, D_MODEL), lambda j: (j, 0))] + scatter.out_specs),
        out_shape=tuple([jax.ShapeDtypeStruct((T, D_MODEL), F32), jax.ShapeDtypeStruct((1, D_MODEL), F32),
                         jax.ShapeDtypeStruct((D_MODEL, D_FF), F32), jax.ShapeDtypeStruct((D_MODEL, D_FF), F32),
                         jax.ShapeDtypeStruct((D_FF, D_MODEL), F32)] + scatter.out_shape),
        scratch_shapes=[pltpu.VMEM((T, D_MODEL), BF16), pltpu.VMEM((T, D_MODEL), BF16),
                        pltpu.VMEM((T, D_MODEL), F32)] + scatter.scratch,
        compiler_params=_params(("arbitrary",)))(x, dy, g, wi, wi, wo, *hosted)
    return res[0], res[1], res[2], res[3], res[4], list(res[5:])


def proj_fwd(x, g, w, name):
    T = x.shape[0]
    nj = D_IN_PAD // IN_CHUNK

    def body(x_ref, g_ref, w_ref, o_ref, h_ref):
        @pl.when(pl.program_id(0) == 0)
        def _():
            def init(r):
                h_ref[r, :] = _rms(x_ref[r, :], g_ref[...]).astype(BF16)
            _row_loop(T, init)

        def blk(r):
            o_ref[r, :] = jnp.dot(h_ref[r, :], w_ref[...], preferred_element_type=F32)
        _row_loop(T, blk)

    return pl.pallas_call(
        body, name=name, grid=(nj,),
        in_specs=[pl.BlockSpec((T, D_MODEL), lambda j: (0, 0)), pl.BlockSpec((1, D_MODEL), lambda j: (0, 0)),
                  pl.BlockSpec((D_MODEL, IN_CHUNK), lambda j: (0, j))],
        out_specs=pl.BlockSpec((T, IN_CHUNK), lambda j: (0, j)),
        out_shape=jax.ShapeDtypeStruct((T, D_IN_PAD), F32),
        scratch_shapes=[pltpu.VMEM((T, D_MODEL), BF16)],
        compiler_params=_params(("arbitrary",)))(x, g, w)


def proj_bwd(x, dres, g, w, dp, name):
    T = x.shape[0]
    nj = D_IN_PAD // IN_CHUNK

    def body(x_ref, dres_ref, g_ref, w_ref, dp_ref, dx_ref, dg_ref, dw_ref, h_ref, dh_ref):
        j = pl.program_id(0)

        @pl.when(j == 0)
        def _():
            def init(r):
                h_ref[r, :] = _rms(x_ref[r, :], g_ref[...]).astype(BF16)
                dh_ref[r, :] = jnp.zeros((r.size, D_MODEL), F32)
            _row_loop(T, init)

        dw_ref[...] = jnp.zeros_like(dw_ref)

        def blk(r):
            dpb = dp_ref[r, :].astype(BF16)
            dw_ref[...] += _bdot_tn(h_ref[r, :], dpb)
            dh_ref[r, :] += _bdot_nt(dpb, w_ref[...])
        _row_loop(T, blk)

        @pl.when(j == nj - 1)
        def _():
            dg_ref[...] = jnp.zeros_like(dg_ref)

            def fin(r):
                dx, dg = _norm_bwd_rows(x_ref[r, :], g_ref[...], dh_ref[r, :], dres_ref[r, :])
                dx_ref[r, :] = dx
                dg_ref[...] += dg
            _row_loop(T, fin)

    full = pl.BlockSpec((T, D_MODEL), lambda j: (0, 0))
    vec = pl.BlockSpec((1, D_MODEL), lambda j: (0, 0))
    return pl.pallas_call(
        body, name=name, grid=(nj,),
        in_specs=[full, full, vec, pl.BlockSpec((D_MODEL, IN_CHUNK), lambda j: (0, j)),
                  pl.BlockSpec((T, IN_CHUNK), lambda j: (0, j))],
        out_specs=(full, vec, pl.BlockSpec((D_MODEL, IN_CHUNK), lambda j: (0, j))),
        out_shape=(jax.ShapeDtypeStruct((T, D_MODEL), F32), jax.ShapeDtypeStruct((1, D_MODEL), F32),
                   jax.ShapeDtypeStruct((D_MODEL, D_IN_PAD), F32)),
        scratch_shapes=[pltpu.VMEM((T, D_MODEL), BF16), pltpu.VMEM((T, D_MODEL), F32)],
        compiler_params=_params(("arbitrary",)))(x, dres, g, w, dp)


def out_fwd(mixed, w, x, name):
    T = x.shape[0]

    def body(m_ref, w_ref, x_ref, o_ref):
        o_ref[...] = x_ref[...] + jnp.dot(m_ref[...].astype(BF16), w_ref[...], preferred_element_type=F32)

    blk = pl.BlockSpec((ROWS, D_MODEL), lambda i: (i, 0))
    return pl.pallas_call(
        body, name=name, grid=(T // ROWS,),
        in_specs=[blk, pl.BlockSpec((D_MODEL, D_MODEL), lambda i: (0, 0)), blk],
        out_specs=blk, out_shape=jax.ShapeDtypeStruct((T, D_MODEL), F32),
        compiler_params=_params(("arbitrary",)))(mixed, w, x)


def out_bwd(mixed, w, dy, name):
    T = dy.shape[0]

    def body(m_ref, w_ref, dy_ref, dm_ref, dw_ref):
        @pl.when(pl.program_id(0) == 0)
        def _():
            dw_ref[...] = jnp.zeros_like(dw_ref)
        dyb = dy_ref[...].astype(BF16)
        dm_ref[...] = _bdot_nt(dyb, w_ref[...])
        dw_ref[...] += _bdot_tn(m_ref[...].astype(BF16), dyb)

    blk = pl.BlockSpec((ROWS, D_MODEL), lambda i: (i, 0))
    sq = pl.BlockSpec((D_MODEL, D_MODEL), lambda i: (0, 0))
    return pl.pallas_call(
        body, name=name, grid=(T // ROWS,),
        in_specs=[blk, sq, blk], out_specs=(blk, sq),
        out_shape=(jax.ShapeDtypeStruct((T, D_MODEL), F32), jax.ShapeDtypeStruct((D_MODEL, D_MODEL), F32)),
        compiler_params=_params(("arbitrary",)))(mixed, w, dy)


def loss_head(x, g, target, name):
    T = x.shape[0]

    def body(x_ref, g_ref, t_ref, loss_ref, dx_ref, dg_ref):
        @pl.when(pl.program_id(0) == 0)
        def _():
            loss_ref[...] = jnp.zeros_like(loss_ref)
            dg_ref[...] = jnp.zeros_like(dg_ref)
        xb = x_ref[...]
        rstd = lax.rsqrt(jnp.mean(xb * xb, axis=-1, keepdims=True) + NORM_EPS)
        xh = xb * rstd
        err = xh * g_ref[...] - t_ref[...]
        loss_ref[...] += 0.5 * jnp.sum(jnp.mean(err * err, axis=-1, keepdims=True), axis=0, keepdims=True)
        dy = err * (1.0 / D_MODEL)
        dg_ref[...] += jnp.sum(dy * xh, axis=0, keepdims=True)
        dxh = dy * g_ref[...]
        dx_ref[...] = rstd * (dxh - xh * jnp.mean(dxh * xh, axis=-1, keepdims=True))

    blk = pl.BlockSpec((ROWS, D_MODEL), lambda i: (i, 0))
    vec = pl.BlockSpec((1, D_MODEL), lambda i: (0, 0))
    return pl.pallas_call(
        body, name=name, grid=(T // ROWS,),
        in_specs=[blk, vec, blk], out_specs=(pl.BlockSpec((1, 1), lambda i: (0, 0)), blk, vec),
        out_shape=(jax.ShapeDtypeStruct((1, 1), F32), jax.ShapeDtypeStruct((T, D_MODEL), F32),
                   jax.ShapeDtypeStruct((1, D_MODEL), F32)),
        compiler_params=_params(("arbitrary",)))(x, g, target)


def rowwise_fwd(fn, rows, shared, out_widths, name):
    T = rows[0].shape[0]
    n_in = len(rows) + len(shared)

    def body(*refs):
        res = fn(*[r[...] for r in refs[:n_in]])
        for o, v in zip(refs[n_in:], res):
            o[...] = v

    in_specs = ([pl.BlockSpec((ROWS, a.shape[1]), lambda i: (i, 0)) for a in rows]
                + [pl.BlockSpec(a.shape, lambda i: (0, 0)) for a in shared])
    return pl.pallas_call(
        body, name=name, grid=(T // ROWS,), in_specs=in_specs,
        out_specs=tuple(pl.BlockSpec((ROWS, w), lambda i: (i, 0)) for w in out_widths),
        out_shape=tuple(jax.ShapeDtypeStruct((T, w), F32) for w in out_widths),
        compiler_params=_params(("arbitrary",)))(*rows, *shared)


def rowwise_bwd(fn, rows, shared, cts, name, ct_fn=None):
    T = rows[0].shape[0]
    nr, ns, nc = len(rows), len(shared), len(cts)

    def body(*refs):
        ins = [r[...] for r in refs[:nr + ns]]
        ctv = tuple(r[...] for r in refs[nr + ns:nr + ns + nc])
        outs = refs[nr + ns + nc:]
        _, vjp = jax.vjp(fn, *ins)
        grads = vjp(ct_fn(*ctv) if ct_fn is not None else ctv)
        for k in range(nr):
            outs[k][...] = grads[k]

        @pl.when(pl.program_id(0) == 0)
        def _():
            for k in range(ns):
                outs[nr + k][...] = jnp.zeros_like(outs[nr + k])
        for k in range(ns):
            outs[nr + k][...] += grads[nr + k]

    row_spec = lambda a: pl.BlockSpec((ROWS, a.shape[1]), lambda i: (i, 0))
    sh_spec = lambda a: pl.BlockSpec(a.shape, lambda i: (0, 0))
    return pl.pallas_call(
        body, name=name, grid=(T // ROWS,),
        in_specs=[row_spec(a) for a in rows] + [sh_spec(a) for a in shared] + [row_spec(a) for a in cts],
        out_specs=tuple([row_spec(a) for a in rows] + [sh_spec(a) for a in shared]),
        out_shape=tuple(jax.ShapeDtypeStruct(a.shape, F32) for a in list(rows) + list(shared)),
        compiler_params=_params(("arbitrary",)))(*rows, *shared, *cts)


def shift_rows(x, s):
    return jnp.pad(x, ((s, 0), (0, 0)))[:x.shape[0]]


def unshift_rows(x, s):
    return jnp.pad(x, ((0, s), (0, 0)))[s:]


def _neg_expm1(y):
    series = -(y * (1.0 + y * (0.5 + y * (1.0 / 6.0 + y * (1.0 / 24.0)))))
    return jnp.where(y > -0.05, series, 1.0 - jnp.exp(y))


def lru_pre_fn(x0, x1, x2, x3, first, w0, w1, w2, w3, cb, ga, gab, gx, gxb, lam):
    xc = w3 * x0 + w2 * x1 + w1 * x2 + w0 * x3 + cb
    r = jax.nn.sigmoid(_hdot(xc, ga) + gab)
    i = jax.nn.sigmoid(_hdot(xc, gx) + gxb)
    log_a = -LRU_C * r * jax.nn.softplus(-lam)
    a = jnp.exp(log_a)
    mult = jnp.where(first > 0.5, 1.0, jnp.sqrt(_neg_expm1(2.0 * log_a)))
    return a, mult * i * xc


def lru_post_fn(h, py, og):
    return (_rms(h * jax.nn.gelu(py), og),)


def lru_scan(a, b, reverse, name):
    T, C = a.shape
    nb = T // 8

    def body(a_ref, b_ref, h_ref):
        rows = lax.broadcasted_iota(jnp.int32, (8, C), 0)

        def blk(i, carry):
            j = nb - 1 - i if reverse else i
            r = pl.ds(pl.multiple_of(j * 8, 8), 8)
            A = a_ref[r, :]
            B = b_ref[r, :]
            for s in (1, 2, 4):
                if reverse:
                    keep = rows < 8 - s
                    sh = 8 - s
                else:
                    keep = rows >= s
                    sh = s
                Bs = jnp.where(keep, pltpu.roll(B, sh, 0), 0.0)
                As = jnp.where(keep, pltpu.roll(A, sh, 0), 1.0)
                B = B + A * Bs
                A = A * As
            hb = B + A * carry
            h_ref[r, :] = hb
            edge = 0 if reverse else 7
            return jnp.sum(jnp.where(rows == edge, hb, 0.0), axis=0, keepdims=True)

        lax.fori_loop(0, nb, blk, jnp.zeros((1, C), F32))

    full = pl.BlockSpec((T, C), lambda: (0, 0))
    return pl.pallas_call(body, name=name, in_specs=[full, full], out_specs=full,
                          out_shape=jax.ShapeDtypeStruct((T, C), F32), compiler_params=_params())(a, b)


def make_rwkv_pre_fn(has_vres):
    def fn(p, pp, *rest):
        if has_vres:
            vf, mu, w_up, w_b, a_up, a_b, g_up, kk_w, ka_w, vw1, vw2, vb = rest
        else:
            mu, w_up, w_b, a_up, a_b, g_up, kk_w, ka_w = rest
        xm = p + (pp - p) * mu
        r, k, v = xm[:, 0:384], xm[:, 384:768], xm[:, 768:1152]
        xw, xa, xg = xm[:, 1152:1216], xm[:, 1216:1280], xm[:, 1280:1408]
        w_log = -jax.nn.softplus(-(w_b + _hdot(jnp.tanh(xw), w_up))) - 0.5
        lw = -jnp.exp(w_log)
        a = jax.nn.sigmoid(a_b + _hdot(xa, a_up))
        g = _hdot(jax.nn.sigmoid(xg), g_up)
        if has_vres:
            v = v + (vf - v) * jax.nn.sigmoid(vb + _hdot(_hdot(v, vw1), vw2))
        kkx = k * kk_w
        kk = kkx * lax.rsqrt(_segsum(kkx * kkx) + 1e-6)
        k2 = k * (1.0 + (a - 1.0) * ka_w)
        return r, lw, k2, v, kk, a, g
    return fn


def rwkv_post_fn(y, r, k2, v, g, ln_g, ln_b, r_k):
    mean = _segsum(y) * (1.0 / HEAD_DIM)
    yc = y - mean
    var = _segsum(yc * yc) * (1.0 / HEAD_DIM)
    yn = yc * lax.rsqrt(var + GN_EPS) * ln_g + ln_b
    bonus = _segsum(r * k2 * r_k) * v
    return ((yn + bonus) * g,)


def _head_expander(first_lane):
    ri, ci = _iota2(128, MIX_W)
    return (ri == ci // HEAD_DIM + first_lane).astype(F32)


def gdn_pre_fn(x0, x1, x2, x3, ab, w0, w1, w2, w3, alog, dtb):
    qkv = jax.nn.silu(w3 * x0 + w2 * x1 + w1 * x2 + w0 * x3)
    q, k, v = qkv[:, 0:384], qkv[:, 384:768], qkv[:, 768:1152]
    q = q * lax.rsqrt(_segsum(q * q) + 1e-6) * (HEAD_DIM ** -0.5)
    k = k * lax.rsqrt(_segsum(k * k) + 1e-6)
    g = -jnp.exp(alog) * jax.nn.softplus(ab + dtb)
    beta = jax.nn.sigmoid(ab)
    ge = _xdot(g, functools.partial(_head_expander, 0))
    be = _xdot(beta, functools.partial(_head_expander, HEADS))
    return q, k, v, ge, be


def gdn_post_fn(o, z, ng):
    ms = _segsum(o * o) * (1.0 / HEAD_DIM)
    return (o * lax.rsqrt(ms + NORM_EPS) * ng * jax.nn.silu(z),)


def _neumann_inv(m):
    n = m.shape[-1]
    ri, ci = _iota2(n, n)
    eye = (ri == ci).astype(F32)
    md = jnp.where(ri // 16 == ci // 16, m, 0.0)
    mo = m - md
    t0 = eye + md
    p2 = _hdot(md, md)
    t0 = t0 + _hdot(t0, p2)
    p4 = _hdot(p2, p2)
    t0 = t0 + _hdot(t0, p4)
    p8 = _hdot(p4, p4)
    t0 = t0 + _hdot(t0, p8)
    nn = _hdot(t0, mo)
    n2 = _hdot(nn, nn)
    t1 = eye + nn + n2 + _hdot(nn, n2)
    return _hdot(t1, t0)


@jax.custom_vjp
def _inv_saved(m, t_saved):
    return t_saved


def _inv_saved_fwd(m, t_saved):
    return t_saved, t_saved


def _inv_saved_bwd(t_saved, dt):
    tt = jnp.swapaxes(t_saved, -1, -2)
    return _hdot(_hdot(tt, dt), tt), jnp.zeros_like(t_saved)


_inv_saved.defvjp(_inv_saved_fwd, _inv_saved_bwd)


def _heads(x):
    return jnp.concatenate([x[None, :, h * HEAD_DIM:(h + 1) * HEAD_DIM] for h in range(HEADS)], axis=0)


def _unheads(y):
    return jnp.concatenate([lax.index_in_dim(y, h, 0, keepdims=False) for h in range(HEADS)], axis=1)


def rwkv_heads(s0, r, lw, k2, v, kk, a, inv):
    n = r.shape[0]
    ri, ci = _iota2(n, n)
    low, strict = ri >= ci, ri > ci
    cs = _cumsum_rows(lw)
    cl = jnp.sum(lw, axis=0, keepdims=True)
    p_in, p_prev, p_inv = jnp.exp(cs), jnp.exp(cs - lw), jnp.exp(-cs)
    p_rest, p_all = jnp.exp(cl - cs), jnp.exp(cl)
    bd = kk * a
    at, rt = _heads(-kk * p_prev), _heads(r * p_in)
    bh, kh = _heads(bd * p_inv), _heads(k2 * p_inv)
    vh = _heads(v)
    m_ab = jnp.where(strict, _cdot_nt(at, bh), 0.0)
    m_ak = jnp.where(strict, _cdot_nt(at, kh), 0.0)
    m_rb = jnp.where(low, _cdot_nt(rt, bh), 0.0)
    m_rk = jnp.where(low, _cdot_nt(rt, kh), 0.0)
    sa = _cdot(inv(m_ab), _cdot_nt(at, s0) + _cdot(m_ak, vh))
    y = _cdot_nt(rt, s0) + _cdot(m_rb, sa) + _cdot(m_rk, vh)
    s1 = s0 * _heads(p_all) + _cdot_tn(sa, _heads(bd * p_rest)) + _cdot_tn(vh, _heads(k2 * p_rest))
    return _unheads(y), s1


def gdn_heads(s0, q, k, v, ge, be, inv):
    n = q.shape[0]
    ri, ci = _iota2(n, n)
    low, strict = ri >= ci, ri > ci
    gc = _cumsum_rows(ge)
    gl = jnp.sum(ge, axis=0, keepdims=True)
    gch = _heads(gc)
    decay = jnp.where(low, jnp.exp(jnp.where(low, gch - jnp.swapaxes(gch, 1, 2), 0.0)), 0.0)
    kb = k * be
    e = jnp.exp(gc)
    kh = _heads(k)
    m = -jnp.where(strict, _cdot_nt(_heads(kb), kh) * decay, 0.0)
    mr = jnp.where(low, _cdot_nt(_heads(q), kh) * decay, 0.0)
    u = _cdot(inv(m), _heads(v * be) - _cdot_nt(_heads(kb * e), s0))
    y = _cdot_nt(_heads(q * e), s0) + _cdot(mr, u)
    s1 = s0 * _heads(jnp.exp(gl)) + _cdot_tn(u, _heads(k * jnp.exp(gl - gc)))
    return _unheads(y), s1


def core_fwd(heads_fn, ins, name, hosted=()):
    T = ins[0].shape[0]
    nc = T // CHUNK
    n = len(ins)
    gather = ChipGather(list(hosted))
    ng = gather.n

    def body(*refs):
        hx = refs[n:n + ng]
        y_ref, s0_ref, t_ref = refs[n + ng:n + ng + 3]
        ho = refs[n + ng + 3:n + 2 * ng + 3]
        s_ref = refs[n + 2 * ng + 3]
        sems = refs[n + 2 * ng + 4:]
        c = pl.program_id(0)

        @pl.when(c == 0)
        def _():
            gather.start(hx, ho, sems)
            s_ref[...] = jnp.zeros_like(s_ref)

        s0 = s_ref[...]
        kept = []

        def inv(m):
            kept.append(_neumann_inv(m))
            return kept[0]

        y, s1 = heads_fn(s0, *[r[...] for r in refs[:n]], inv)
        y_ref[...] = y
        s0_ref[0] = s0
        t_ref[0] = kept[0]
        s_ref[...] = s1

        @pl.when(c == nc - 4)
        def _():
            gather.relay(hx, ho, sems)

        @pl.when(c == nc - 1)
        def _():
            gather.finish(hx, ho, sems)

    row = pl.BlockSpec((CHUNK, MIX_W), lambda c: (c, 0))
    st_shape = (HEADS, HEAD_DIM, HEAD_DIM)
    st = pl.BlockSpec((1,) + st_shape, lambda c: (c, 0, 0, 0))
    res = pl.pallas_call(
        body, name=name, grid=(nc,), in_specs=[row] * n + gather.in_specs,
        out_specs=tuple([row, st, st] + gather.out_specs),
        out_shape=tuple([jax.ShapeDtypeStruct((T, MIX_W), F32), jax.ShapeDtypeStruct((nc,) + st_shape, F32),
                         jax.ShapeDtypeStruct((nc,) + st_shape, F32)] + gather.out_shape),
        scratch_shapes=[pltpu.VMEM(st_shape, F32)] + gather.scratch,
        compiler_params=_params(("arbitrary",)))(*ins, *hosted)
    return res[0], res[1], res[2], list(res[3:])


def core_bwd(heads_fn, ins, s0_all, t_all, dy, name):
    T = ins[0].shape[0]
    nc = T // CHUNK
    n = len(ins)

    def body(*refs):
        s0_ref, t_ref, dy_ref = refs[n:n + 3]
        outs = refs[n + 3:n + 3 + n]
        ds_ref = refs[n + 3 + n]

        @pl.when(pl.program_id(0) == 0)
        def _():
            ds_ref[...] = jnp.zeros_like(ds_ref)

        t_saved = t_ref[0]
        f = lambda s0, *xs: heads_fn(s0, *xs, lambda m: _inv_saved(m, t_saved))
        _, vjp = jax.vjp(f, s0_ref[0], *[r[...] for r in refs[:n]])
        grads = vjp((dy_ref[...], ds_ref[...]))
        ds_ref[...] = grads[0]
        for k in range(n):
            outs[k][...] = grads[1 + k]

    row = pl.BlockSpec((CHUNK, MIX_W), lambda c: (nc - 1 - c, 0))
    st_shape = (HEADS, HEAD_DIM, HEAD_DIM)
    st = pl.BlockSpec((1,) + st_shape, lambda c: (nc - 1 - c, 0, 0, 0))
    return pl.pallas_call(
        body, name=name, grid=(nc,), in_specs=[row] * n + [st, st, row], out_specs=tuple([row] * n),
        out_shape=tuple(jax.ShapeDtypeStruct((T, MIX_W), F32) for _ in range(n)),
        scratch_shapes=[pltpu.VMEM(st_shape, F32)],
        compiler_params=_params(("arbitrary",)))(*ins, s0_all, t_all, dy)


def _block_diag(w):
    out = jnp.zeros((LRU_W, LRU_W), w.dtype)
    for n in range(LRU_BLOCKS):
        out = lax.dynamic_update_slice(out, w[n], (n * 64, n * 64))
    return out


def _block_diag_grad(g):
    return jnp.stack([g[n * 64:(n + 1) * 64, n * 64:(n + 1) * 64] for n in range(LRU_BLOCKS)])


def _row(v):
    return v.reshape(1, -1)


def _pad128(v):
    return jnp.pad(v.reshape(1, -1), ((0, 0), (0, 128 - v.size)))


def _layer_shared(w, l):
    cw = w['lru_conv_w'][l]
    lru_pre = [_row(cw[0]), _row(cw[1]), _row(cw[2]), _row(cw[3]), _row(w['lru_conv_b'][l]),
               _block_diag(w['lru_gate_a_w'][l]), _row(w['lru_gate_a_b'][l]),
               _block_diag(w['lru_gate_x_w'][l]), _row(w['lru_gate_x_b'][l]), _row(w['lru_lambda'][l])]
    rw_pre = [_row(w['rwkv_mu'][l]), w['rwkv_w_up'][l], _row(w['rwkv_w_bias'][l]), w['rwkv_a_up'][l],
              _row(w['rwkv_a_bias'][l]), w['rwkv_g_up'][l], _row(w['rwkv_k_k'][l]), _row(w['rwkv_k_a'][l])]
    if l > 0:
        rw_pre += [w['rwkv_vres_w1'][l - 1], w['rwkv_vres_w2'][l - 1], _row(w['rwkv_vres_b'][l - 1])]
    rw_post = [_row(w['rwkv_ln_g'][l]), _row(w['rwkv_ln_b'][l]), _row(w['rwkv_r_k'][l])]
    gw = w['gdn_conv_w'][l]
    gdn_pre = [_row(gw[0]), _row(gw[1]), _row(gw[2]), _row(gw[3]), _pad128(w['gdn_a_log'][l]),
               _pad128(w['gdn_dt_bias'][l])]
    gdn_post = [_row(jnp.tile(w['gdn_norm'][l], HEADS))]
    return dict(lru_pre=lru_pre, lru_post=[_row(w['lru_out_norm'][l])], rw_pre=rw_pre, rw_post=rw_post,
                gdn_pre=gdn_pre, gdn_post=gdn_post)


def _mixer_fwd(p, sh, l, v_first, host_rwkv=(), host_gdn=()):
    T = p.shape[0]
    lx, ly = p[:, 0:256], p[:, 256:512]
    prw, qkv, z, ab = p[:, 512:1920], p[:, 1920:3072], p[:, 3072:3456], p[:, 3456:3584]
    first = jnp.zeros((T, LRU_W), F32).at[0].set(1.0)
    lru_rows = [lx, shift_rows(lx, 1), shift_rows(lx, 2), shift_rows(lx, 3), first]
    a, b = rowwise_fwd(lru_pre_fn, lru_rows, sh['lru_pre'], (LRU_W, LRU_W), f"lru_pre_fwd{l}")
    hseq = lru_scan(a, b, False, f"lru_scan_fwd{l}")
    (y_lru,) = rowwise_fwd(lru_post_fn, [hseq, ly], sh['lru_post'], (LRU_W,), f"lru_post_fwd{l}")

    rw_rows = [prw, shift_rows(prw, 1)] + ([v_first] if l > 0 else [])
    rw = rowwise_fwd(make_rwkv_pre_fn(l > 0), rw_rows, sh['rw_pre'], (MIX_W,) * 7, f"rwkv_pre_fwd{l}")
    r, lw, k2, v, kk, ar, g = rw
    y_raw, rs0, rt, got_rwkv = core_fwd(rwkv_heads, [r, lw, k2, v, kk, ar], f"rwkv_core_fwd{l}", host_rwkv)
    (y_rw,) = rowwise_fwd(rwkv_post_fn, [y_raw, r, k2, v, g], sh['rw_post'], (MIX_W,), f"rwkv_post_fwd{l}")

    gdn_rows = [qkv, shift_rows(qkv, 1), shift_rows(qkv, 2), shift_rows(qkv, 3), ab]
    gd = rowwise_fwd(gdn_pre_fn, gdn_rows, sh['gdn_pre'], (MIX_W,) * 5, f"gdn_pre_fwd{l}")
    o_raw, gs0, gt, got_gdn = core_fwd(gdn_heads, list(gd), f"gdn_core_fwd{l}", host_gdn)
    (y_gdn,) = rowwise_fwd(gdn_post_fn, [o_raw, z], sh['gdn_post'], (MIX_W,), f"gdn_post_fwd{l}")

    mixed = jnp.concatenate([y_lru, y_rw, y_gdn], axis=1)
    saved = dict(lru_rows=lru_rows, a=a, hseq=hseq, ly=ly, rw_rows=rw_rows, rw=rw, y_raw=y_raw, rs0=rs0, rt=rt,
                 gdn_rows=gdn_rows, gd=gd, o_raw=o_raw, gs0=gs0, gt=gt, z=z)
    v_layer0 = v if l == 0 else None
    return mixed, saved, v_layer0, got_rwkv, got_gdn


def _mixer_bwd(dmixed, sv, sh, l, dv_first):
    d_lru, d_rw, d_gdn = dmixed[:, 0:256], dmixed[:, 256:640], dmixed[:, 640:1024]
    gw = {}

    dh, dly, d_og = rowwise_bwd(lru_post_fn, [sv['hseq'], sv['ly']], sh['lru_post'], [d_lru], f"lru_post_bwd{l}")
    gscan = lru_scan(unshift_rows(sv['a'], 1), dh, True, f"lru_scan_bwd{l}")
    res = rowwise_bwd(lru_pre_fn, sv['lru_rows'], sh['lru_pre'], [gscan, shift_rows(sv['hseq'], 1)],
                      f"lru_pre_bwd{l}", ct_fn=lambda gs, hp: (gs * hp, gs))
    dlx = res[0] + unshift_rows(res[1], 1) + unshift_rows(res[2], 2) + unshift_rows(res[3], 3)
    dw0, dw1, dw2, dw3, dcb, dga, dgab, dgx, dgxb, dlam = res[5:]
    gw['lru_conv_w'] = jnp.concatenate([dw0, dw1, dw2, dw3], axis=0)
    gw['lru_conv_b'] = dcb[0]
    gw['lru_gate_a_w'] = _block_diag_grad(dga)
    gw['lru_gate_a_b'] = dgab.reshape(LRU_BLOCKS, 64)
    gw['lru_gate_x_w'] = _block_diag_grad(dgx)
    gw['lru_gate_x_b'] = dgxb.reshape(LRU_BLOCKS, 64)
    gw['lru_lambda'] = dlam[0]
    gw['lru_out_norm'] = d_og[0]

    r, lw, k2, v, kk, ar, g = sv['rw']
    res = rowwise_bwd(rwkv_post_fn, [sv['y_raw'], r, k2, v, g], sh['rw_post'], [d_rw], f"rwkv_post_bwd{l}")
    dy_raw, dr_p, dk2_p, dv_p, dg = res[:5]
    gw['rwkv_ln_g'], gw['rwkv_ln_b'], gw['rwkv_r_k'] = res[5][0], res[6][0], res[7].reshape(HEADS, HEAD_DIM)
    dr_c, dlw, dk2_c, dv_c, dkk, dar = core_bwd(rwkv_heads, [r, lw, k2, v, kk, ar], sv['rs0'], sv['rt'], dy_raw,
                                                 f"rwkv_core_bwd{l}")
    cts = [dr_p, dr_c, dlw, dk2_p, dk2_c, dv_p, dv_c, dkk, dar, dg]
    if l == 0:
        cts.append(dv_first)
        ct_fn = lambda a1, a2, b, c1, c2, d1, d2, e, f, gg, vf: (a1 + a2, b, c1 + c2, d1 + d2 + vf, e, f, gg)
    else:
        ct_fn = lambda a1, a2, b, c1, c2, d1, d2, e, f, gg: (a1 + a2, b, c1 + c2, d1 + d2, e, f, gg)
    res = rowwise_bwd(make_rwkv_pre_fn(l > 0), sv['rw_rows'], sh['rw_pre'], cts, f"rwkv_pre_bwd{l}", ct_fn=ct_fn)
    dprw = res[0] + unshift_rows(res[1], 1)
    nrow = len(sv['rw_rows'])
    dv_first_out = res[2] if l > 0 else None
    sg = res[nrow:]
    gw['rwkv_mu'], gw['rwkv_w_up'], gw['rwkv_w_bias'], gw['rwkv_a_up'] = sg[0][0], sg[1], sg[2][0], sg[3]
    gw['rwkv_a_bias'], gw['rwkv_g_up'], gw['rwkv_k_k'], gw['rwkv_k_a'] = sg[4][0], sg[5], sg[6][0], sg[7][0]
    if l > 0:
        gw['rwkv_vres_w1'], gw['rwkv_vres_w2'], gw['rwkv_vres_b'] = sg[8], sg[9], sg[10][0]

    do_raw, dz, d_ng = rowwise_bwd(gdn_post_fn, [sv['o_raw'], sv['z']], sh['gdn_post'], [d_gdn], f"gdn_post_bwd{l}")
    gw['gdn_norm'] = jnp.sum(d_ng.reshape(HEADS, HEAD_DIM), axis=0)
    dgd = core_bwd(gdn_heads, list(sv['gd']), sv['gs0'], sv['gt'], do_raw, f"gdn_core_bwd{l}")
    res = rowwise_bwd(gdn_pre_fn, sv['gdn_rows'], sh['gdn_pre'], list(dgd), f"gdn_pre_bwd{l}")
    dqkv = res[0] + unshift_rows(res[1], 1) + unshift_rows(res[2], 2) + unshift_rows(res[3], 3)
    dab = res[4]
    gw['gdn_conv_w'] = jnp.concatenate(res[5:9], axis=0)
    gw['gdn_a_log'], gw['gdn_dt_bias'] = res[9][0, :HEADS], res[10][0, :HEADS]

    dp = jnp.concatenate([dlx, dly, dprw, dqkv, dz, dab], axis=1)
    return dp, gw, dv_first_out


IN_SHARD = D_IN // N_CHIPS
IN_SHARD_PAD = D_IN_PAD // N_CHIPS


def _cols_to_chips(g, n=N_CHIPS):
    r = g.shape[0]
    return jnp.transpose(g.reshape(r, n, -1), (1, 0, 2))


def _cols_from_chips(g):
    return jnp.transpose(g, (1, 0, 2)).reshape(g.shape[1], -1)


def _w_in_from_chips(g):
    nat = _cols_from_chips(g[:, :, :IN_SHARD])
    return jnp.pad(nat, ((0, 0), (0, D_IN_PAD - D_IN)))


def _w_in_to_chips(g):
    return jnp.pad(_cols_to_chips(g[:, :D_IN]), ((0, 0), (0, 0), (0, IN_SHARD_PAD - IN_SHARD)))


def _natural(name, g):
    if name == 'w_in':
        return _w_in_from_chips(g)
    if BIG[name] == 2:
        return _cols_from_chips(g)
    return g.reshape(-1, g.shape[2])


def local_step(x, target, w, wb, shards=None):
    def hosted(keys):
        return [shards[k] for k in keys] if shards is not None else []

    def arrived(keys, gathered):
        for (name, layer), g in zip(keys if shards is not None else [], gathered):
            wb[name][layer] = _natural(name, g)

    saved = []
    v_first = None
    for l in range(N_LAYERS):
        sh = _layer_shared(w, l)
        for_mixer = [('w_in', l), ('w_out', l)]
        for_ffn2 = [('ffn2_wi', l), ('ffn2_wo', l)]
        for_next = [('ffn1_wi', l + 1), ('ffn1_wo', l + 1)] if l + 1 < N_LAYERS else []
        x1, got = ffn_fwd(x, _row(w['ffn1_norm'][l]), wb['ffn1_wi'][l], wb['ffn1_wo'][l], f"ffn1_fwd{l}",
                          hosted(for_mixer))
        arrived(for_mixer, got)
        p = proj_fwd(x1, _row(w['mix_norm'][l]), wb['w_in'][l], f"proj_fwd{l}")
        mixed, sv, v0, got_ffn2, got_next = _mixer_fwd(p, sh, l, v_first, hosted(for_ffn2), hosted(for_next))
        arrived(for_ffn2, got_ffn2)
        arrived(for_next, got_next)
        if l == 0:
            v_first = v0
        x2 = out_fwd(mixed, wb['w_out'][l], x1, f"out_fwd{l}")
        x3, _ = ffn_fwd(x2, _row(w['ffn2_norm'][l]), wb['ffn2_wi'][l], wb['ffn2_wo'][l], f"ffn2_fwd{l}")
        saved.append(dict(x0=x, x1=x1, x2=x2, mixed=mixed, sv=sv, sh=sh))
        x = x3

    loss, dx, dgf = loss_head(x, _row(w['final_norm']), target, "loss_head")
    per_layer = [dict() for _ in range(N_LAYERS)]
    dv_first = jnp.zeros((x.shape[0], MIX_W), F32)

    waiting, chip_sums, arrived_parts = [], {}, {}

    def reduce_now(keys, tag):
        if shards is None:
            return
        sums = chip_sums_of([(n, k, per_layer[k][n]) for n, k in keys], lax.axis_index("c"), tag)
        for key, (total, total_bf) in zip(keys, sums):
            chip_sums[key] = total
            waiting.append((key, total_bf))

    def take_waiting():
        keys, parts = [k for k, _ in waiting], [p for _, p in waiting]
        waiting.clear()
        return keys, parts

    for l in reversed(range(N_LAYERS)):
        s = saved[l]
        gw = per_layer[l]
        keys, parts = take_waiting()
        dx, dg2, dwg, dwu, dwo, got = ffn_bwd(s['x2'], dx, _row(w['ffn2_norm'][l]), wb['ffn2_wi'][l],
                                              wb['ffn2_wo'][l], f"ffn2_bwd{l}", parts)
        arrived_parts.update(zip(keys, got))
        wi_parts = lambda dwg, dwu: (dwg, dwu)
        row_parts = lambda dw: dw.reshape(N_CHIPS, -1, dw.shape[1])
        gw['ffn2_norm'], gw['ffn2_wi'], gw['ffn2_wo'] = dg2[0], wi_parts(dwg, dwu), row_parts(dwo)
        if l == N_LAYERS - 1:
            reduce_now([('ffn2_wi', l), ('ffn2_wo', l)], f"ffn2_{l}")
        dmixed, dw_out = out_bwd(s['mixed'], wb['w_out'][l], dx, f"out_bwd{l}")
        gw['w_out'] = row_parts(dw_out)
        dp, gmix, dvf = _mixer_bwd(dmixed, s['sv'], s['sh'], l, dv_first)
        if l > 0:
            dv_first = dvf
        gw.update(gmix)
        dx, dgm, dwin = proj_bwd(s['x1'], dx, _row(w['mix_norm'][l]), wb['w_in'][l], dp, f"proj_bwd{l}")
        gw['mix_norm'], gw['w_in'] = dgm[0], _w_in_to_chips(dwin)
        if l < N_LAYERS - 1:
            reduce_now([('ffn2_wi', l), ('ffn2_wo', l), ('w_in', l), ('w_out', l)], f"mix_{l}")
        keys, parts = take_waiting()
        dx, dg1, dwg, dwu, dwo, got = ffn_bwd(s['x0'], dx, _row(w['ffn1_norm'][l]), wb['ffn1_wi'][l],
                                              wb['ffn1_wo'][l], f"ffn1_bwd{l}", parts)
        arrived_parts.update(zip(keys, got))
        gw['ffn1_norm'], gw['ffn1_wi'], gw['ffn1_wo'] = dg1[0], wi_parts(dwg, dwu), row_parts(dwo)
        if l == N_LAYERS - 1:
            reduce_now([('w_in', l), ('w_out', l), ('ffn1_wi', l), ('ffn1_wo', l)], f"ffn1_{l}")
        else:
            reduce_now([('ffn1_wi', l), ('ffn1_wo', l)], f"ffn1_{l}")
    if shards is not None:
        keys, parts = take_waiting()
        arrived_parts.update(zip(keys, scatter_chips(parts, "grad_scatter_last")))

    grads = {'final_norm': dgf[0]}
    for name in WEIGHTS:
        if name == 'final_norm':
            continue
        if name in BIG:
            if shards is None:
                grads[name] = [per_layer[l][name] for l in range(N_LAYERS)]
            else:
                grads[name] = [(chip_sums[(name, l)], arrived_parts[(name, l)]) for l in range(N_LAYERS)]
        elif name.startswith('rwkv_vres'):
            grads[name] = per_layer[1][name][None]
        else:
            grads[name] = jnp.stack([per_layer[l][name] for l in range(N_LAYERS)])
    return loss[0, 0], dx, grads


ANY = pl.BlockSpec(memory_space=pl.ANY)


def _coords():
    return lax.axis_index("x"), lax.axis_index("y"), lax.axis_index("c")


def _other_chips(x, y):
    return [((x + 1) % 2, y), (x, (y + 1) % 2), ((x + 1) % 2, (y + 1) % 2)]


def allreduce_small(pack, name):
    R = pack.shape[0]
    rh = R // 2

    def body(x_ref, o_ref, sib_ref, chip_ref, parts_ref, send_sems, recv_sems):
        x, y, c = _coords()
        sib = (x, y, 1 - c)

        def copy(k, src, dst, to):
            return pltpu.make_async_remote_copy(src_ref=src, dst_ref=dst, send_sem=send_sems.at[k],
                                                recv_sem=recv_sems.at[k], device_id=to, device_id_type=MESH)

        swap = copy(0, x_ref, sib_ref, sib)
        swap.start()
        swap.wait()
        chip_ref[...] = jnp.where(c == 0, x_ref[...], sib_ref[...]) + jnp.where(c == 0, sib_ref[...], x_ref[...])

        mine = pl.ds(pl.multiple_of(c * rh, 8), rh)
        sends = [copy(1 + j, chip_ref.at[mine], parts_ref.at[j], (px, py, c))
                 for j, (px, py) in enumerate(_other_chips(x, y))]
        for cp in sends:
            cp.start()
        for cp in sends:
            cp.wait()
        s = 2 * x + y
        own = chip_ref[mine, :]
        from_chip = {2: parts_ref[0], 1: parts_ref[1], 3: parts_ref[2]}
        terms = []
        for k in range(N_CHIPS):
            t = own
            for d, part in from_chip.items():
                t = jnp.where(jnp.bitwise_xor(s, d) == k, part, t)
            terms.append(t)
        o_ref[mine, :] = ((terms[0] + terms[1]) + terms[2]) + terms[3]

        share = copy(4, o_ref.at[mine], o_ref.at[mine], sib)
        share.start()
        share.wait()

    vm = pl.BlockSpec(memory_space=pltpu.VMEM)
    return pl.pallas_call(
        body, name=name, in_specs=[vm], out_specs=vm, out_shape=jax.ShapeDtypeStruct((R, 128), F32),
        scratch_shapes=[pltpu.VMEM((R, 128), F32), pltpu.VMEM((R, 128), F32), pltpu.VMEM((3, rh, 128), F32),
                        pltpu.SemaphoreType.DMA((5,)), pltpu.SemaphoreType.DMA((5,))],
        compiler_params=_params())(pack)


class ChipGather:
    def __init__(self, shards):
        self.shapes = [s.shape for s in shards]
        self.n = len(shards)
        self.in_specs = [ANY] * self.n
        self.out_specs = [ANY] * self.n
        self.out_shape = [jax.ShapeDtypeStruct((N_CHIPS,) + s.shape, s.dtype) for s in shards]
        self.scratch = [pltpu.SemaphoreType.DMA((6 * self.n,)), pltpu.SemaphoreType.DMA((6 * self.n,)),
                        pltpu.SemaphoreType.DMA((self.n,))] if self.n else []

    def _rows(self, a, core):
        rh = self.shapes[a][0] // 2
        return pl.ds(pl.multiple_of(core * rh, 16), rh)

    def _copies(self, kind, x_refs, o_refs, sems):
        send_sems, recv_sems, local_sems = sems
        x, y, c = _coords()
        s_me = 2 * x + y
        sib = (x, y, 1 - c)

        def copy(a, k, src, dst, to):
            return pltpu.make_async_remote_copy(src_ref=src, dst_ref=dst, send_sem=send_sems.at[6 * a + k],
                                                recv_sem=recv_sems.at[6 * a + k], device_id=to, device_id_type=MESH)

        if kind == 'own':
            return [pltpu.make_async_copy(x_refs[a], o_refs[a].at[s_me], local_sems.at[a]) for a in range(self.n)]
        out = []
        for j, (px, py) in enumerate(_other_chips(x, y)):
            for a in range(self.n):
                mine = self._rows(a, c)
                part = o_refs[a].at[2 * px + py, mine]
                if kind == 'sent':
                    out.append(copy(a, j, x_refs[a].at[mine], o_refs[a].at[s_me, mine], (px, py, c)))
                elif kind == 'arrived':
                    out.append(copy(a, j, part, part, (px, py, c)))
                elif kind == 'passed':
                    out.append(copy(a, 3 + j, part, part, sib))
                else:
                    theirs = o_refs[a].at[2 * px + py, self._rows(a, 1 - c)]
                    out.append(copy(a, 3 + j, theirs, theirs, sib))
        return out

    def start(self, x_refs, o_refs, sems):
        if not self.n:
            return
        for cp in self._copies('own', x_refs, o_refs, sems) + self._copies('sent', x_refs, o_refs, sems):
            cp.start()

    def relay(self, x_refs, o_refs, sems):
        if not self.n:
            return
        for got, fw in zip(self._copies('arrived', x_refs, o_refs, sems),
                           self._copies('passed', x_refs, o_refs, sems)):
            got.wait_recv()
            fw.start()

    def finish(self, x_refs, o_refs, sems):
        if not self.n:
            return
        for cp in self._copies('from_sibling', x_refs, o_refs, sems):
            cp.wait_recv()
        for cp in self._copies('sent', x_refs, o_refs, sems) + self._copies('passed', x_refs, o_refs, sems):
            cp.wait_send()
        for cp in self._copies('own', x_refs, o_refs, sems):
            cp.wait()


def allgather_chips(shards, name):
    gather = ChipGather(shards)
    n = gather.n

    def body(*refs):
        x_refs, o_refs, sems = refs[:n], refs[n:2 * n], refs[2 * n:]
        gather.start(x_refs, o_refs, sems)
        gather.relay(x_refs, o_refs, sems)
        gather.finish(x_refs, o_refs, sems)

    return pl.pallas_call(
        body, name=name, in_specs=gather.in_specs, out_specs=tuple(gather.out_specs),
        out_shape=tuple(gather.out_shape), scratch_shapes=gather.scratch, compiler_params=_params())(*shards)


def sibling_swap(srcs, halves, name):
    n = len(srcs)
    row_axis = [s.ndim - 2 for s in srcs]
    out_shapes = [s.shape[:ax] + (s.shape[ax] // 2,) + s.shape[ax + 1:] if halves else s.shape
                  for s, ax in zip(srcs, row_axis)]

    def body(*refs):
        x_refs, o_refs = refs[:n], refs[n:2 * n]
        send_sems, recv_sems = refs[2 * n:]
        x, y, c = _coords()
        copies = []
        for a in range(n):
            part = x_refs[a]
            if halves:
                rh = srcs[a].shape[row_axis[a]] // 2
                theirs = pl.ds(pl.multiple_of((1 - c) * rh, 16), rh)
                part = part.at[:, theirs] if row_axis[a] == 1 else part.at[theirs]
            cp = pltpu.make_async_remote_copy(src_ref=part, dst_ref=o_refs[a], send_sem=send_sems.at[a],
                                              recv_sem=recv_sems.at[a], device_id=(x, y, 1 - c), device_id_type=MESH)
            cp.start()
            copies.append(cp)
        for cp in copies:
            cp.wait()

    return pl.pallas_call(
        body, name=name, in_specs=[ANY] * n, out_specs=tuple([ANY] * n),
        out_shape=tuple(jax.ShapeDtypeStruct(sh, s.dtype) for sh, s in zip(out_shapes, srcs)),
        scratch_shapes=[pltpu.SemaphoreType.DMA((n,)), pltpu.SemaphoreType.DMA((n,))],
        compiler_params=_params())(*srcs)


class ChipScatter:
    def __init__(self, parts):
        self.n = len(parts)
        self.in_specs = [ANY] * self.n
        self.out_specs = [ANY] * self.n
        self.out_shape = [jax.ShapeDtypeStruct((3,) + p.shape[1:], p.dtype) for p in parts]
        self.scratch = [pltpu.SemaphoreType.DMA((3 * self.n,)), pltpu.SemaphoreType.DMA((3 * self.n,))] if self.n else []

    def _copies(self, x_refs, o_refs, sems):
        send_sems, recv_sems = sems
        x, y, c = _coords()
        return [pltpu.make_async_remote_copy(src_ref=x_refs[a].at[2 * px + py], dst_ref=o_refs[a].at[j],
                                             send_sem=send_sems.at[3 * a + j], recv_sem=recv_sems.at[3 * a + j],
                                             device_id=(px, py, c), device_id_type=MESH)
                for j, (px, py) in enumerate(_other_chips(x, y)) for a in range(self.n)]

    def start(self, x_refs, o_refs, sems):
        if self.n:
            for cp in self._copies(x_refs, o_refs, sems):
                cp.start()

    def finish(self, x_refs, o_refs, sems):
        if self.n:
            for cp in self._copies(x_refs, o_refs, sems):
                cp.wait()


def scatter_chips(parts, name):
    scatter = ChipScatter(parts)
    n = scatter.n

    def body(*refs):
        x_refs, o_refs, sems = refs[:n], refs[n:2 * n], refs[2 * n:]
        scatter.start(x_refs, o_refs, sems)
        scatter.finish(x_refs, o_refs, sems)

    return pl.pallas_call(
        body, name=name, in_specs=scatter.in_specs, out_specs=tuple(scatter.out_specs),
        out_shape=tuple(scatter.out_shape), scratch_shapes=scatter.scratch, compiler_params=_params())(*parts)


def _row_block(rows):
    return max(b for b in range(16, 257, 16) if rows % b == 0)


def chip_sum(gpack, recv, core, name):
    n, R, W = gpack.shape
    rh = R // 2
    rb = _row_block(rh)
    nb = rh // rb

    def body(c_ref, g_ref, r_ref, o_ref, ob_ref):
        s = g_ref[...] + r_ref[...]
        o_ref[...] = s
        ob_ref[...] = s.astype(BF16)

    blk = pl.BlockSpec((1, rb, W), lambda i, j, c_ref: (i, j, 0))
    spec = pltpu.PrefetchScalarGridSpec(
        num_scalar_prefetch=1, grid=(n, nb),
        in_specs=[pl.BlockSpec((1, rb, W), lambda i, j, c_ref: (i, c_ref[0] * nb + j, 0)), blk],
        out_specs=(blk, blk))
    return pl.pallas_call(
        body, name=name, grid_spec=spec,
        out_shape=(jax.ShapeDtypeStruct((n, rh, W), F32), jax.ShapeDtypeStruct((n, rh, W), BF16)),
        compiler_params=_params(("arbitrary", "arbitrary")))(core, gpack, recv)


def chip_sum_cols(gate, up, recv_gate, recv_up, core, name):
    R, W = gate.shape
    cw = W // 2
    rh = R // 2
    rb = _row_block(rh)
    nb = rh // rb

    def body(c_ref, g_ref, u_ref, rg_ref, ru_ref, o_ref, ob_ref):
        s = jnp.where(pl.program_id(0) < 2, g_ref[...] + rg_ref[...], u_ref[...] + ru_ref[...])
        o_ref[0] = s
        ob_ref[0] = s.astype(BF16)

    gate_col = lambda s: jnp.minimum(s, 1)
    up_col = lambda s: jnp.maximum(s - 2, 0)
    out = pl.BlockSpec((1, rb, cw), lambda s, j, c_ref: (s, j, 0))
    spec = pltpu.PrefetchScalarGridSpec(
        num_scalar_prefetch=1, grid=(N_CHIPS, nb),
        in_specs=[pl.BlockSpec((rb, cw), lambda s, j, c_ref: (c_ref[0] * nb + j, gate_col(s))),
                  pl.BlockSpec((rb, cw), lambda s, j, c_ref: (c_ref[0] * nb + j, up_col(s))),
                  pl.BlockSpec((rb, cw), lambda s, j, c_ref: (j, gate_col(s))),
                  pl.BlockSpec((rb, cw), lambda s, j, c_ref: (j, up_col(s)))],
        out_specs=(out, out))
    return pl.pallas_call(
        body, name=name, grid_spec=spec,
        out_shape=(jax.ShapeDtypeStruct((N_CHIPS, rh, cw), F32), jax.ShapeDtypeStruct((N_CHIPS, rh, cw), BF16)),
        compiler_params=_params(("arbitrary", "arbitrary")))(core, gate, up, recv_gate, recv_up)


def chip_sums_of(items, core, tag):
    parts = []
    for _, _, g in items:
        parts += list(g) if isinstance(g, tuple) else [g]
    swapped = iter(zip(parts, sibling_swap(parts, True, f"grad_swap_cores_{tag}")))
    core_arg = core.reshape(1).astype(jnp.int32)
    sums = []
    for n, l, g in items:
        if isinstance(g, tuple):
            (dwg, from_g), (dwu, from_u) = next(swapped), next(swapped)
            sums.append(chip_sum_cols(dwg, dwu, from_g, from_u, core_arg, f"grad_chip_sum_{n}{l}"))
        else:
            p, r = next(swapped)
            sums.append(chip_sum(p, r, core_arg, f"grad_chip_sum_{n}{l}"))
    return sums


def shard_sum(own, recv, name):
    R, W = own.shape
    rb = _row_block(R)

    def body(a_ref, r_ref, o_ref):
        acc = a_ref[...]
        for j in range(3):
            acc = acc + r_ref[j].astype(F32)
        o_ref[...] = acc

    return pl.pallas_call(
        body, name=name, grid=(R // rb,),
        in_specs=[pl.BlockSpec((rb, W), lambda i: (i, 0)), pl.BlockSpec((3, rb, W), lambda i: (0, i, 0))],
        out_specs=pl.BlockSpec((rb, W), lambda i: (i, 0)), out_shape=jax.ShapeDtypeStruct((R, W), F32),
        compiler_params=_params(("arbitrary",)))(own, recv)


def adamw(w, m, v, g, name):
    L, R, C = w.shape
    rb = max(b for b in range(8, 257, 8) if R % b == 0)
    bc1 = 1.0 - ADAM_B1 ** ADAM_STEP
    bc2 = 1.0 - ADAM_B2 ** ADAM_STEP

    def body(w_ref, m_ref, v_ref, g_ref, d_ref, nm_ref, nv_ref):
        gv = g_ref[...]
        nm = ADAM_B1 * m_ref[...] + (1.0 - ADAM_B1) * gv
        nv = ADAM_B2 * v_ref[...] + (1.0 - ADAM_B2) * (gv * gv)
        d_ref[...] = -ADAM_LR * ((nm / bc1) / (jnp.sqrt(nv / bc2) + ADAM_EPS) + ADAM_WD * w_ref[...])
        nm_ref[...] = nm
        nv_ref[...] = nv

    blk = pl.BlockSpec((1, rb, C), lambda l, i: (l, i, 0))
    sh = jax.ShapeDtypeStruct((L, R, C), F32)
    return pl.pallas_call(body, name=name, grid=(L, R // rb), in_specs=[blk] * 4, out_specs=(blk,) * 3,
                          out_shape=(sh, sh, sh), compiler_params=_params(("arbitrary", "arbitrary")))(w, m, v, g)


SMALL = [n for n in WEIGHTS if n not in BIG]


PACK_TILE = 8 * 128


def _pack(arrays):
    blocks = []
    for a in arrays:
        flat = a.reshape(-1)
        flat = jnp.pad(flat, (0, -flat.size % PACK_TILE))
        blocks.append(flat.reshape(-1, 128))
    rows = sum(b.shape[0] for b in blocks)
    if rows % 16:
        blocks.append(jnp.zeros((8, 128), arrays[0].dtype))
    return jnp.concatenate(blocks, axis=0)


def _unpack(pack, shapes):
    out, row = [], 0
    for shape in shapes:
        size = int(np.prod(shape))
        rows = -(-size // PACK_TILE) * 8
        out.append(pack[row:row + rows].reshape(-1)[:size].reshape(shape))
        row += rows
    return out


def _pad_lanes(a):
    return jnp.pad(a, ((0, 0), (0, -a.shape[1] % 128)))


def _local_shard(full, axis, chip):
    size = full.shape[axis] // N_CHIPS
    return lax.dynamic_slice_in_dim(full, chip * size, size, axis)


def kernel(x, ffn1_norm, ffn1_wi, ffn1_wo, mix_norm, w_in, w_out, lru_conv_w, lru_conv_b, lru_gate_a_w, lru_gate_a_b, lru_gate_x_w, lru_gate_x_b, lru_lambda, lru_out_norm, rwkv_mu, rwkv_w_up, rwkv_w_bias, rwkv_a_up, rwkv_a_bias, rwkv_g_up, rwkv_k_k, rwkv_k_a, rwkv_r_k, rwkv_ln_g, rwkv_ln_b, rwkv_vres_w1, rwkv_vres_w2, rwkv_vres_b, gdn_conv_w, gdn_a_log, gdn_dt_bias, gdn_norm, ffn2_norm, ffn2_wi, ffn2_wo, final_norm, loss_target, m_ffn1_norm, m_ffn1_wi, m_ffn1_wo, m_mix_norm, m_w_in, m_w_out, m_lru_conv_w, m_lru_conv_b, m_lru_gate_a_w, m_lru_gate_a_b, m_lru_gate_x_w, m_lru_gate_x_b, m_lru_lambda, m_lru_out_norm, m_rwkv_mu, m_rwkv_w_up, m_rwkv_w_bias, m_rwkv_a_up, m_rwkv_a_bias, m_rwkv_g_up, m_rwkv_k_k, m_rwkv_k_a, m_rwkv_r_k, m_rwkv_ln_g, m_rwkv_ln_b, m_rwkv_vres_w1, m_rwkv_vres_w2, m_rwkv_vres_b, m_gdn_conv_w, m_gdn_a_log, m_gdn_dt_bias, m_gdn_norm, m_ffn2_norm, m_ffn2_wi, m_ffn2_wo, m_final_norm, v_ffn1_norm, v_ffn1_wi, v_ffn1_wo, v_mix_norm, v_w_in, v_w_out, v_lru_conv_w, v_lru_conv_b, v_lru_gate_a_w, v_lru_gate_a_b, v_lru_gate_x_w, v_lru_gate_x_b, v_lru_lambda, v_lru_out_norm, v_rwkv_mu, v_rwkv_w_up, v_rwkv_w_bias, v_rwkv_a_up, v_rwkv_a_bias, v_rwkv_g_up, v_rwkv_k_k, v_rwkv_k_a, v_rwkv_r_k, v_rwkv_ln_g, v_rwkv_ln_b, v_rwkv_vres_w1, v_rwkv_vres_w2, v_rwkv_vres_b, v_gdn_conv_w, v_gdn_a_log, v_gdn_dt_bias, v_gdn_norm, v_ffn2_norm, v_ffn2_wi, v_ffn2_wo, v_final_norm):
    args = locals()
    w_loc = {n: args[n] for n in WEIGHTS}
    m_loc = {n: args['m_' + n] for n in WEIGHTS}
    v_loc = {n: args['v_' + n] for n in WEIGHTS}
    chip = 2 * lax.axis_index("x") + lax.axis_index("y")
    core = lax.axis_index("c")

    big = [(n, l) for n in BIG for l in range(N_LAYERS)]
    shards = {(n, l): _pad_lanes(w_loc[n][l].astype(BF16)) for n, l in big}
    first = [('ffn1_wi', 0), ('ffn1_wo', 0)]
    wb = {n: [None] * N_LAYERS for n in BIG}
    for (n, l), g in zip(first, allgather_chips([shards[k] for k in first], "allgather_first")):
        wb[n][l] = _natural(n, g)

    sm_names = list(SMALL_SHARDED)
    placed = []
    for n in sm_names:
        ax = SMALL_SHARDED[n]
        full_shape = w_loc[n].shape[:ax] + (N_CHIPS * w_loc[n].shape[ax],) + w_loc[n].shape[ax + 1:]
        src = w_loc[n] * (core == 0).astype(F32)
        placed.append(lax.dynamic_update_slice_in_dim(jnp.zeros(full_shape, F32), src, chip * w_loc[n].shape[ax], ax))
    summed = allreduce_small(_pack(placed), "allgather_small")
    w_full = dict(w_loc)
    w_full.update(zip(sm_names, _unpack(summed, [p.shape for p in placed])))

    loss, dx, grads = local_step(x[0], loss_target[0], w_full, wb, shards)
    loss = lax.psum(loss, ("x", "y", "c"))

    gsum = allreduce_small(_pack([grads[n] for n in SMALL]), "allreduce_small")
    g_loc = {}
    for n, g in zip(SMALL, _unpack(gsum, [grads[n].shape for n in SMALL])):
        g_loc[n] = _local_shard(g, SMALL_SHARDED[n], chip) if n in SMALL_SHARDED else g

    halves =[shard_sum(lax.dynamic_index_in_dim(grads[n][l][0], chip, 0, keepdims=False), grads[n][l][1],
                        f"grad_shard_sum_{n}{l}") for n, l in big]
    others = sibling_swap(halves, False, "grad_share_cores")
    rows = {n: [None] * N_LAYERS for n in BIG}
    for (n, l), half, other in zip(big, halves, others):
        lower = jnp.where(core == 0, half, other)
        upper = jnp.where(core == 0, other, half)
        rows[n][l] = jnp.concatenate([lower, upper], axis=0)[:, :w_loc[n].shape[-1]]
    big_names = list(BIG)
    for n in big_names:
        g_loc[n] = jnp.stack(rows[n])

    delta, new_m, new_v = {}, {}, {}
    for n in big_names:
        delta[n], new_m[n], new_v[n] = adamw(w_loc[n], m_loc[n], v_loc[n], g_loc[n], f"adamw_{n}")
    pack = lambda d: _pack([d[n] for n in SMALL])[None]
    res = adamw(pack(w_loc), pack(m_loc), pack(v_loc), pack(g_loc), "adamw_small")
    for dst, r in zip((delta, new_m, new_v), res):
        dst.update(zip(SMALL, _unpack(r[0], [w_loc[n].shape for n in SMALL])))

    return (loss, dx[None], *[g_loc[n] for n in WEIGHTS], *[delta[n] for n in WEIGHTS],
            *[new_m[n] for n in WEIGHTS], *[new_v[n] for n in WEIGHTS])
```

```python
import functools

import numpy as np
import jax
import jax.numpy as jnp
from jax import lax
from jax.experimental import pallas as pl
from jax.experimental.pallas import tpu as pltpu

F32 = jnp.float32
BF16 = jnp.bfloat16
MESH = pl.DeviceIdType.MESH

D_MODEL = 1024
D_FF = 2816
N_LAYERS = 2
HEADS = 6
HEAD_DIM = 64
MIX_W = HEADS * HEAD_DIM
LRU_W = 256
LRU_BLOCKS = 4
D_IN = 3468
D_IN_PAD = 3584
NORM_EPS = 1e-6
GN_EPS = 64e-5
LRU_C = 8.0
CHUNK = 64
ROWS = 256
FF_CHUNK = 256
IN_CHUNK = 512
VMEM_LIMIT = 56 * 1024 * 1024

ADAM_LR, ADAM_B1, ADAM_B2, ADAM_EPS, ADAM_WD, ADAM_STEP = 0.001, 0.9, 0.999, 1e-08, 0.01, 10

WEIGHTS = ['ffn1_norm', 'ffn1_wi', 'ffn1_wo', 'mix_norm', 'w_in', 'w_out', 'lru_conv_w', 'lru_conv_b',
           'lru_gate_a_w', 'lru_gate_a_b', 'lru_gate_x_w', 'lru_gate_x_b', 'lru_lambda', 'lru_out_norm',
           'rwkv_mu', 'rwkv_w_up', 'rwkv_w_bias', 'rwkv_a_up', 'rwkv_a_bias', 'rwkv_g_up', 'rwkv_k_k',
           'rwkv_k_a', 'rwkv_r_k', 'rwkv_ln_g', 'rwkv_ln_b', 'rwkv_vres_w1', 'rwkv_vres_w2', 'rwkv_vres_b',
           'gdn_conv_w', 'gdn_a_log', 'gdn_dt_bias', 'gdn_norm', 'ffn2_norm', 'ffn2_wi', 'ffn2_wo', 'final_norm']
BIG = {'ffn1_wi': 2, 'ffn1_wo': 1, 'w_in': 2, 'w_out': 1, 'ffn2_wi': 2, 'ffn2_wo': 1}
SMALL_SHARDED = {'lru_conv_w': 2, 'rwkv_w_up': 2, 'rwkv_a_up': 2, 'rwkv_g_up': 2, 'rwkv_vres_w1': 1,
                 'rwkv_vres_w2': 2, 'gdn_conv_w': 2}
N_CHIPS = 4


def _params(sem=None):
    kw = dict(vmem_limit_bytes=VMEM_LIMIT)
    if sem is not None:
        kw['dimension_semantics'] = sem
    return pltpu.CompilerParams(**kw)


def _bdot(a, b, dims=(((1,), (0,)), ((), ()))):
    return lax.dot_general(a.astype(BF16), b.astype(BF16), dims, preferred_element_type=F32)


def _bdot_nt(a, b):
    return _bdot(a, b, (((1,), (1,)), ((), ())))


def _bdot_tn(a, b):
    return _bdot(a, b, (((0,), (0,)), ((), ())))


_DIMS = {'nn': (((1,), (0,)), ((), ())), 'nt': (((1,), (1,)), ((), ())), 'tn': (((0,), (0,)), ((), ()))}


def _split(a, terms):
    parts = []
    for _ in range(terms - 1):
        hi = a.astype(BF16)
        parts.append(hi)
        a = a - hi.astype(F32)
    parts.append(a.astype(BF16))
    return parts


_BATCH_DIMS = {'nn': (((2,), (1,)), ((0,), (0,))), 'nt': (((2,), (2,)), ((0,), (0,))),
               'tn': (((1,), (1,)), ((0,), (0,)))}


def _dot3(a, b, kind):
    ah, al = _split(a, 2)
    bh, bl = _split(b, 2)
    dims = _BATCH_DIMS[kind] if a.ndim == 3 else _DIMS[kind]
    d = lambda p, q: lax.dot_general(p, q, dims, preferred_element_type=F32)
    return d(ah, bh) + (d(ah, bl) + d(al, bh))


@functools.partial(jax.custom_vjp, nondiff_argnums=(2,))
def _cdot_k(a, b, kind):
    return _dot3(a, b, kind)


def _cdot_k_fwd(a, b, kind):
    return _dot3(a, b, kind), (a, b)


def _cdot_k_bwd(kind, res, ct):
    a, b = res
    if kind == 'nn':
        return _dot3(ct, b, 'nt'), _dot3(a, ct, 'tn')
    if kind == 'nt':
        return _dot3(ct, b, 'nn'), _dot3(ct, a, 'tn')
    return _dot3(b, ct, 'nt'), _dot3(a, ct, 'nn')


_cdot_k.defvjp(_cdot_k_fwd, _cdot_k_bwd)


def _dot1(a, b, kind):
    dims = _BATCH_DIMS[kind] if a.ndim == 3 else _DIMS[kind]
    return lax.dot_general(a.astype(BF16), b.astype(BF16), dims, preferred_element_type=F32)


@functools.partial(jax.custom_vjp, nondiff_argnums=(2,))
def _cdot1_k(a, b, kind):
    return _dot1(a, b, kind)


def _cdot1_k_fwd(a, b, kind):
    return _dot1(a, b, kind), (a, b)


def _cdot1_k_bwd(kind, res, ct):
    a, b = res
    if kind == 'nn':
        return _dot1(ct, b, 'nt'), _dot1(a, ct, 'tn')
    if kind == 'nt':
        return _dot1(ct, b, 'nn'), _dot1(ct, a, 'tn')
    return _dot1(b, ct, 'nt'), _dot1(a, ct, 'nn')


_cdot1_k.defvjp(_cdot1_k_fwd, _cdot1_k_bwd)


def _cdot(a, b):
    return _cdot1_k(a, b, 'nn')


def _cdot_nt(a, b):
    return _cdot1_k(a, b, 'nt')


def _cdot_tn(a, b):
    return _cdot1_k(a, b, 'tn')


def _hdot(a, b):
    return _cdot_k(a, b, 'nn')


def _dot_exact(x, m01, kind):
    d = lambda p: lax.dot_general(p, m01.astype(BF16), _DIMS[kind], preferred_element_type=F32)
    hi, mid, lo = _split(x, 3)
    return d(hi) + (d(mid) + d(lo))


@functools.partial(jax.custom_vjp, nondiff_argnums=(1,))
def _xdot(x, make_m):
    return _dot_exact(x, make_m(), 'nn')


def _xdot_fwd(x, make_m):
    return _dot_exact(x, make_m(), 'nn'), None


def _xdot_bwd(make_m, _, ct):
    return (_dot_exact(ct, make_m(), 'nt'),)


_xdot.defvjp(_xdot_fwd, _xdot_bwd)


def _iota2(n, m):
    return lax.broadcasted_iota(jnp.int32, (n, m), 0), lax.broadcasted_iota(jnp.int32, (n, m), 1)


def _head_blocks(w):
    ri, ci = _iota2(w, w)
    return (ri // HEAD_DIM == ci // HEAD_DIM).astype(F32)


def _segsum(x):
    return _xdot(x, functools.partial(_head_blocks, x.shape[-1]))


def _cumsum_rows(x):
    return _cumsum_k(x, x.shape[0])


@functools.partial(jax.custom_vjp, nondiff_argnums=(1,))
def _cumsum_k(x, n):
    return _lower_dot(x, n, False)


def _lower_dot(x, n, transpose):
    ri, ci = _iota2(n, n)
    m = ((ri <= ci) if transpose else (ri >= ci)).astype(BF16)
    d = lambda p: lax.dot_general(m, p, _DIMS['nn'], preferred_element_type=F32)
    hi, mid, lo = _split(x, 3)
    return d(hi) + (d(mid) + d(lo))


def _cumsum_k_fwd(x, n):
    return _lower_dot(x, n, False), None


def _cumsum_k_bwd(n, _, ct):
    return (_lower_dot(ct, n, True),)


_cumsum_k.defvjp(_cumsum_k_fwd, _cumsum_k_bwd)


def _rms(x, g):
    return x * lax.rsqrt(jnp.mean(x * x, axis=-1, keepdims=True) + NORM_EPS) * g


DENSE_ROWS = 1024


def _row_loop(n_rows, fn):
    rows = min(DENSE_ROWS, n_rows)

    def step(i, c):
        fn(pl.ds(pl.multiple_of(i * rows, rows), rows))
        return c
    lax.fori_loop(0, n_rows // rows, step, 0)


def ffn_fwd(x, g, wi, wo, name, hosted=()):
    T = x.shape[0]
    nj = D_FF // FF_CHUNK
    gather = ChipGather(list(hosted))
    n = gather.n

    def body(*refs):
        x_ref, g_ref, wg_ref, wu_ref, wo_ref = refs[:5]
        hx, o_ref, ho = refs[5:5 + n], refs[5 + n], refs[6 + n:6 + 2 * n]
        h_ref, acc_ref = refs[6 + 2 * n:8 + 2 * n]
        sems = refs[8 + 2 * n:]
        j = pl.program_id(0)

        @pl.when(j == 0)
        def _():
            gather.start(hx, ho, sems)

            def init(r):
                h_ref[r, :] = _rms(x_ref[r, :], g_ref[...]).astype(BF16)
                acc_ref[r, :] = jnp.zeros((r.size, D_MODEL), F32)
            _row_loop(T, init)

        def blk(r):
            hb = h_ref[r, :]
            gate = jnp.dot(hb, wg_ref[...], preferred_element_type=F32)
            up = jnp.dot(hb, wu_ref[...], preferred_element_type=F32)
            a = (gate * jax.nn.sigmoid(gate) * up).astype(BF16)
            acc_ref[r, :] += jnp.dot(a, wo_ref[...], preferred_element_type=F32)
        _row_loop(T, blk)

        @pl.when(j == nj - 2)
        def _():
            gather.relay(hx, ho, sems)

        @pl.when(j == nj - 1)
        def _():
            def fin(r):
                o_ref[r, :] = x_ref[r, :] + 0.5 * acc_ref[r, :]
            _row_loop(T, fin)
            gather.finish(hx, ho, sems)

    full = pl.BlockSpec((T, D_MODEL), lambda j: (0, 0))
    res = pl.pallas_call(
        body, name=name, grid=(nj,),
        in_specs=[full, pl.BlockSpec((1, D_MODEL), lambda j: (0, 0)),
                  pl.BlockSpec((D_MODEL, FF_CHUNK), lambda j: (0, j)),
                  pl.BlockSpec((D_MODEL, FF_CHUNK), lambda j: (0, j + nj)),
                  pl.BlockSpec((FF_CHUNK, D_MODEL), lambda j: (j, 0))] + gather.in_specs,
        out_specs=tuple([full] + gather.out_specs),
        out_shape=tuple([jax.ShapeDtypeStruct((T, D_MODEL), F32)] + gather.out_shape),
        scratch_shapes=[pltpu.VMEM((T, D_MODEL), BF16), pltpu.VMEM((T, D_MODEL), F32)] + gather.scratch,
        compiler_params=_params(("arbitrary",)))(x, g, wi, wi, wo, *hosted)
    return res[0], list(res[1:])


def _norm_bwd_rows(x, g, dh, dres):
    rstd = lax.rsqrt(jnp.mean(x * x, axis=-1, keepdims=True) + NORM_EPS)
    xh = x * rstd
    dxh = dh * g
    dx = rstd * (dxh - xh * jnp.mean(dxh * xh, axis=-1, keepdims=True))
    return dres + dx, jnp.sum(dh * xh, axis=0, keepdims=True)


def ffn_bwd(x, dy, g, wi, wo, name, hosted=()):
    T = x.shape[0]
    nj = D_FF // FF_CHUNK
    scatter = ChipScatter(list(hosted))
    n = scatter.n

    def body(*refs):
        x_ref, dy_ref, g_ref, wg_ref, wu_ref, wo_ref = refs[:6]
        hx = refs[6:6 + n]
        dx_ref, dg_ref, dwg_ref, dwu_ref, dwo_ref = refs[6 + n:11 + n]
        ho = refs[11 + n:11 + 2 * n]
        h_ref, da_ref, dh_ref = refs[11 + 2 * n:14 + 2 * n]
        sems = refs[14 + 2 * n:]
        j = pl.program_id(0)

        @pl.when(j == 0)
        def _():
            scatter.start(hx, ho, sems)

            def init(r):
                h_ref[r, :] = _rms(x_ref[r, :], g_ref[...]).astype(BF16)
                da_ref[r, :] = (0.5 * dy_ref[r, :]).astype(BF16)
                dh_ref[r, :] = jnp.zeros((r.size, D_MODEL), F32)
            _row_loop(T, init)

        dwg_ref[...] = jnp.zeros_like(dwg_ref)
        dwu_ref[...] = jnp.zeros_like(dwu_ref)
        dwo_ref[...] = jnp.zeros_like(dwo_ref)

        def blk(r):
            hb = h_ref[r, :]
            db = da_ref[r, :]
            gate = jnp.dot(hb, wg_ref[...], preferred_element_type=F32)
            up = jnp.dot(hb, wu_ref[...], preferred_element_type=F32)
            sg = jax.nn.sigmoid(gate)
            sl = gate * sg
            da = _bdot_nt(db, wo_ref[...])
            dup = (da * sl).astype(BF16)
            dgate = (da * up * (sg * (1.0 + gate * (1.0 - sg)))).astype(BF16)
            dwo_ref[...] += _bdot_tn((sl * up).astype(BF16), db)
            dwg_ref[...] += _bdot_tn(hb, dgate)
            dwu_ref[...] += _bdot_tn(hb, dup)
            dh_ref[r, :] += _bdot_nt(dgate, wg_ref[...]) + _bdot_nt(dup, wu_ref[...])
        _row_loop(T, blk)

        @pl.when(j == nj - 1)
        def _():
            dg_ref[...] = jnp.zeros_like(dg_ref)

            def fin(r):
                dx, dg = _norm_bwd_rows(x_ref[r, :], g_ref[...], dh_ref[r, :], dy_ref[r, :])
                dx_ref[r, :] = dx
                dg_ref[...] += dg
            _row_loop(T, fin)
            scatter.finish(hx, ho, sems)

    full = pl.BlockSpec((T, D_MODEL), lambda j: (0, 0))
    vec = pl.BlockSpec((1, D_MODEL), lambda j: (0, 0))
    res = pl.pallas_call(
        body, name=name, grid=(nj,),
        in_specs=[full, full, vec,
                  pl.BlockSpec((D_MODEL, FF_CHUNK), lambda j: (0, j)),
                  pl.BlockSpec((D_MODEL, FF_CHUNK), lambda j: (0, j + nj)),
                  pl.BlockSpec((FF_CHUNK, D_MODEL), lambda j: (j, 0))] + scatter.in_specs,
        out_specs=tuple([full, vec,
                         pl.BlockSpec((D_MODEL, FF_CHUNK), lambda j: (0, j)),
                         pl.BlockSpec((D_MODEL, FF_CHUNK), lambda j: (0, j)),
                         pl.BlockSpec((FF_CHUNK, D_MODEL), lambda j: (j, 0))] + scatter.out_specs),
        out_shape=tuple([jax.ShapeDtypeStruct((T, D_MODEL), F32), jax.ShapeDtypeStruct((1, D_MODEL), F32),
                         jax.ShapeDtypeStruct((D_MODEL, D_FF), F32), jax.ShapeDtypeStruct((D_MODEL, D_FF), F32),
                         jax.ShapeDtypeStruct((D_FF, D_MODEL), F32)] + scatter.out_shape),
        scratch_shapes=[pltpu.VMEM((T, D_MODEL), BF16), pltpu.VMEM((T, D_MODEL), BF16),
                        pltpu.VMEM((T, D_MODEL), F32)] + scatter.scratch,
        compiler_params=_params(("arbitrary",)))(x, dy, g, wi, wi, wo, *hosted)
    return res[0], res[1], res[2], res[3], res[4], list(res[5:])


def proj_fwd(x, g, w, name):
    T = x.shape[0]
    nj = D_IN_PAD // IN_CHUNK

    def body(x_ref, g_ref, w_ref, o_ref, h_ref):
        @pl.when(pl.program_id(0) == 0)
        def _():
            def init(r):
                h_ref[r, :] = _rms(x_ref[r, :], g_ref[...]).astype(BF16)
            _row_loop(T, init)

        def blk(r):
            o_ref[r, :] = jnp.dot(h_ref[r, :], w_ref[...], preferred_element_type=F32)
        _row_loop(T, blk)

    return pl.pallas_call(
        body, name=name, grid=(nj,),
        in_specs=[pl.BlockSpec((T, D_MODEL), lambda j: (0, 0)), pl.BlockSpec((1, D_MODEL), lambda j: (0, 0)),
                  pl.BlockSpec((D_MODEL, IN_CHUNK), lambda j: (0, j))],
        out_specs=pl.BlockSpec((T, IN_CHUNK), lambda j: (0, j)),
        out_shape=jax.ShapeDtypeStruct((T, D_IN_PAD), F32),
        scratch_shapes=[pltpu.VMEM((T, D_MODEL), BF16)],
        compiler_params=_params(("arbitrary",)))(x, g, w)


def proj_bwd(x, dres, g, w, dp, name):
    T = x.shape[0]
    nj = D_IN_PAD // IN_CHUNK

    def body(x_ref, dres_ref, g_ref, w_ref, dp_ref, dx_ref, dg_ref, dw_ref, h_ref, dh_ref):
        j = pl.program_id(0)

        @pl.when(j == 0)
        def _():
            def init(r):
                h_ref[r, :] = _rms(x_ref[r, :], g_ref[...]).astype(BF16)
                dh_ref[r, :] = jnp.zeros((r.size, D_MODEL), F32)
            _row_loop(T, init)

        dw_ref[...] = jnp.zeros_like(dw_ref)

        def blk(r):
            dpb = dp_ref[r, :].astype(BF16)
            dw_ref[...] += _bdot_tn(h_ref[r, :], dpb)
            dh_ref[r, :] += _bdot_nt(dpb, w_ref[...])
        _row_loop(T, blk)

        @pl.when(j == nj - 1)
        def _():
            dg_ref[...] = jnp.zeros_like(dg_ref)

            def fin(r):
                dx, dg = _norm_bwd_rows(x_ref[r, :], g_ref[...], dh_ref[r, :], dres_ref[r, :])
                dx_ref[r, :] = dx
                dg_ref[...] += dg
            _row_loop(T, fin)

    full = pl.BlockSpec((T, D_MODEL), lambda j: (0, 0))
    vec = pl.BlockSpec((1, D_MODEL), lambda j: (0, 0))
    return pl.pallas_call(
        body, name=name, grid=(nj,),
        in_specs=[full, full, vec, pl.BlockSpec((D_MODEL, IN_CHUNK), lambda j: (0, j)),
                  pl.BlockSpec((T, IN_CHUNK), lambda j: (0, j))],
        out_specs=(full, vec, pl.BlockSpec((D_MODEL, IN_CHUNK), lambda j: (0, j))),
        out_shape=(jax.ShapeDtypeStruct((T, D_MODEL), F32), jax.ShapeDtypeStruct((1, D_MODEL), F32),
                   jax.ShapeDtypeStruct((D_MODEL, D_IN_PAD), F32)),
        scratch_shapes=[pltpu.VMEM((T, D_MODEL), BF16), pltpu.VMEM((T, D_MODEL), F32)],
        compiler_params=_params(("arbitrary",)))(x, dres, g, w, dp)


def out_fwd(mixed, w, x, name):
    T = x.shape[0]

    def body(m_ref, w_ref, x_ref, o_ref):
        o_ref[...] = x_ref[...] + jnp.dot(m_ref[...].astype(BF16), w_ref[...], preferred_element_type=F32)

    blk = pl.BlockSpec((ROWS, D_MODEL), lambda i: (i, 0))
    return pl.pallas_call(
        body, name=name, grid=(T // ROWS,),
        in_specs=[blk, pl.BlockSpec((D_MODEL, D_MODEL), lambda i: (0, 0)), blk],
        out_specs=blk, out_shape=jax.ShapeDtypeStruct((T, D_MODEL), F32),
        compiler_params=_params(("arbitrary",)))(mixed, w, x)


def out_bwd(mixed, w, dy, name):
    T = dy.shape[0]

    def body(m_ref, w_ref, dy_ref, dm_ref, dw_ref):
        @pl.when(pl.program_id(0) == 0)
        def _():
            dw_ref[...] = jnp.zeros_like(dw_ref)
        dyb = dy_ref[...].astype(BF16)
        dm_ref[...] = _bdot_nt(dyb, w_ref[...])
        dw_ref[...] += _bdot_tn(m_ref[...].astype(BF16), dyb)

    blk = pl.BlockSpec((ROWS, D_MODEL), lambda i: (i, 0))
    sq = pl.BlockSpec((D_MODEL, D_MODEL), lambda i: (0, 0))
    return pl.pallas_call(
        body, name=name, grid=(T // ROWS,),
        in_specs=[blk, sq, blk], out_specs=(blk, sq),
        out_shape=(jax.ShapeDtypeStruct((T, D_MODEL), F32), jax.ShapeDtypeStruct((D_MODEL, D_MODEL), F32)),
        compiler_params=_params(("arbitrary",)))(mixed, w, dy)


def loss_head(x, g, target, name):
    T = x.shape[0]

    def body(x_ref, g_ref, t_ref, loss_ref, dx_ref, dg_ref):
        @pl.when(pl.program_id(0) == 0)
        def _():
            loss_ref[...] = jnp.zeros_like(loss_ref)
            dg_ref[...] = jnp.zeros_like(dg_ref)
        xb = x_ref[...]
        rstd = lax.rsqrt(jnp.mean(xb * xb, axis=-1, keepdims=True) + NORM_EPS)
        xh = xb * rstd
        err = xh * g_ref[...] - t_ref[...]
        loss_ref[...] += 0.5 * jnp.sum(jnp.mean(err * err, axis=-1, keepdims=True), axis=0, keepdims=True)
        dy = err * (1.0 / D_MODEL)
        dg_ref[...] += jnp.sum(dy * xh, axis=0, keepdims=True)
        dxh = dy * g_ref[...]
        dx_ref[...] = rstd * (dxh - xh * jnp.mean(dxh * xh, axis=-1, keepdims=True))

    blk = pl.BlockSpec((ROWS, D_MODEL), lambda i: (i, 0))
    vec = pl.BlockSpec((1, D_MODEL), lambda i: (0, 0))
    return pl.pallas_call(
        body, name=name, grid=(T // ROWS,),
        in_specs=[blk, vec, blk], out_specs=(pl.BlockSpec((1, 1), lambda i: (0, 0)), blk, vec),
        out_shape=(jax.ShapeDtypeStruct((1, 1), F32), jax.ShapeDtypeStruct((T, D_MODEL), F32),
                   jax.ShapeDtypeStruct((1, D_MODEL), F32)),
        compiler_params=_params(("arbitrary",)))(x, g, target)


def rowwise_fwd(fn, rows, shared, out_widths, name):
    T = rows[0].shape[0]
    n_in = len(rows) + len(shared)

    def body(*refs):
        res = fn(*[r[...] for r in refs[:n_in]])
        for o, v in zip(refs[n_in:], res):
            o[...] = v

    in_specs = ([pl.BlockSpec((ROWS, a.shape[1]), lambda i: (i, 0)) for a in rows]
                + [pl.BlockSpec(a.shape, lambda i: (0, 0)) for a in shared])
    return pl.pallas_call(
        body, name=name, grid=(T // ROWS,), in_specs=in_specs,
        out_specs=tuple(pl.BlockSpec((ROWS, w), lambda i: (i, 0)) for w in out_widths),
        out_shape=tuple(jax.ShapeDtypeStruct((T, w), F32) for w in out_widths),
        compiler_params=_params(("arbitrary",)))(*rows, *shared)


def rowwise_bwd(fn, rows, shared, cts, name, ct_fn=None):
    T = rows[0].shape[0]
    nr, ns, nc = len(rows), len(shared), len(cts)

    def body(*refs):
        ins = [r[...] for r in refs[:nr + ns]]
        ctv = tuple(r[...] for r in refs[nr + ns:nr + ns + nc])
        outs = refs[nr + ns + nc:]
        _, vjp = jax.vjp(fn, *ins)
        grads = vjp(ct_fn(*ctv) if ct_fn is not None else ctv)
        for k in range(nr):
            outs[k][...] = grads[k]

        @pl.when(pl.program_id(0) == 0)
        def _():
            for k in range(ns):
                outs[nr + k][...] = jnp.zeros_like(outs[nr + k])
        for k in range(ns):
            outs[nr + k][...] += grads[nr + k]

    row_spec = lambda a: pl.BlockSpec((ROWS, a.shape[1]), lambda i: (i, 0))
    sh_spec = lambda a: pl.BlockSpec(a.shape, lambda i: (0, 0))
    return pl.pallas_call(
        body, name=name, grid=(T // ROWS,),
        in_specs=[row_spec(a) for a in rows] + [sh_spec(a) for a in shared] + [row_spec(a) for a in cts],
        out_specs=tuple([row_spec(a) for a in rows] + [sh_spec(a) for a in shared]),
        out_shape=tuple(jax.ShapeDtypeStruct(a.shape, F32) for a in list(rows) + list(shared)),
        compiler_params=_params(("arbitrary",)))(*rows, *shared, *cts)


def shift_rows(x, s):
    return jnp.pad(x, ((s, 0), (0, 0)))[:x.shape[0]]


def unshift_rows(x, s):
    return jnp.pad(x, ((0, s), (0, 0)))[s:]


def _neg_expm1(y):
    series = -(y * (1.0 + y * (0.5 + y * (1.0 / 6.0 + y * (1.0 / 24.0)))))
    return jnp.where(y > -0.05, series, 1.0 - jnp.exp(y))


def lru_pre_fn(x0, x1, x2, x3, first, w0, w1, w2, w3, cb, ga, gab, gx, gxb, lam):
    xc = w3 * x0 + w2 * x1 + w1 * x2 + w0 * x3 + cb
    r = jax.nn.sigmoid(_hdot(xc, ga) + gab)
    i = jax.nn.sigmoid(_hdot(xc, gx) + gxb)
    log_a = -LRU_C * r * jax.nn.softplus(-lam)
    a = jnp.exp(log_a)
    mult = jnp.where(first > 0.5, 1.0, jnp.sqrt(_neg_expm1(2.0 * log_a)))
    return a, mult * i * xc


def lru_post_fn(h, py, og):
    return (_rms(h * jax.nn.gelu(py), og),)


def lru_scan(a, b, reverse, name):
    T, C = a.shape
    nb = T // 8

    def body(a_ref, b_ref, h_ref):
        rows = lax.broadcasted_iota(jnp.int32, (8, C), 0)

        def blk(i, carry):
            j = nb - 1 - i if reverse else i
            r = pl.ds(pl.multiple_of(j * 8, 8), 8)
            A = a_ref[r, :]
            B = b_ref[r, :]
            for s in (1, 2, 4):
                if reverse:
                    keep = rows < 8 - s
                    sh = 8 - s
                else:
                    keep = rows >= s
                    sh = s
                Bs = jnp.where(keep, pltpu.roll(B, sh, 0), 0.0)
                As = jnp.where(keep, pltpu.roll(A, sh, 0), 1.0)
                B = B + A * Bs
                A = A * As
            hb = B + A * carry
            h_ref[r, :] = hb
            edge = 0 if reverse else 7
            return jnp.sum(jnp.where(rows == edge, hb, 0.0), axis=0, keepdims=True)

        lax.fori_loop(0, nb, blk, jnp.zeros((1, C), F32))

    full = pl.BlockSpec((T, C), lambda: (0, 0))
    return pl.pallas_call(body, name=name, in_specs=[full, full], out_specs=full,
                          out_shape=jax.ShapeDtypeStruct((T, C), F32), compiler_params=_params())(a, b)


def make_rwkv_pre_fn(has_vres):
    def fn(p, pp, *rest):
        if has_vres:
            vf, mu, w_up, w_b, a_up, a_b, g_up, kk_w, ka_w, vw1, vw2, vb = rest
        else:
            mu, w_up, w_b, a_up, a_b, g_up, kk_w, ka_w = rest
        xm = p + (pp - p) * mu
        r, k, v = xm[:, 0:384], xm[:, 384:768], xm[:, 768:1152]
        xw, xa, xg = xm[:, 1152:1216], xm[:, 1216:1280], xm[:, 1280:1408]
        w_log = -jax.nn.softplus(-(w_b + _hdot(jnp.tanh(xw), w_up))) - 0.5
        lw = -jnp.exp(w_log)
        a = jax.nn.sigmoid(a_b + _hdot(xa, a_up))
        g = _hdot(jax.nn.sigmoid(xg), g_up)
        if has_vres:
            v = v + (vf - v) * jax.nn.sigmoid(vb + _hdot(_hdot(v, vw1), vw2))
        kkx = k * kk_w
        kk = kkx * lax.rsqrt(_segsum(kkx * kkx) + 1e-6)
        k2 = k * (1.0 + (a - 1.0) * ka_w)
        return r, lw, k2, v, kk, a, g
    return fn


def rwkv_post_fn(y, r, k2, v, g, ln_g, ln_b, r_k):
    mean = _segsum(y) * (1.0 / HEAD_DIM)
    yc = y - mean
    var = _segsum(yc * yc) * (1.0 / HEAD_DIM)
    yn = yc * lax.rsqrt(var + GN_EPS) * ln_g + ln_b
    bonus = _segsum(r * k2 * r_k) * v
    return ((yn + bonus) * g,)


def _head_expander(first_lane):
    ri, ci = _iota2(128, MIX_W)
    return (ri == ci // HEAD_DIM + first_lane).astype(F32)


def gdn_pre_fn(x0, x1, x2, x3, ab, w0, w1, w2, w3, alog, dtb):
    qkv = jax.nn.silu(w3 * x0 + w2 * x1 + w1 * x2 + w0 * x3)
    q, k, v = qkv[:, 0:384], qkv[:, 384:768], qkv[:, 768:1152]
    q = q * lax.rsqrt(_segsum(q * q) + 1e-6) * (HEAD_DIM ** -0.5)
    k = k * lax.rsqrt(_segsum(k * k) + 1e-6)
    g = -jnp.exp(alog) * jax.nn.softplus(ab + dtb)
    beta = jax.nn.sigmoid(ab)
    ge = _xdot(g, functools.partial(_head_expander, 0))
    be = _xdot(beta, functools.partial(_head_expander, HEADS))
    return q, k, v, ge, be


def gdn_post_fn(o, z, ng):
    ms = _segsum(o * o) * (1.0 / HEAD_DIM)
    return (o * lax.rsqrt(ms + NORM_EPS) * ng * jax.nn.silu(z),)


def _neumann_inv(m):
    n = m.shape[-1]
    ri, ci = _iota2(n, n)
    eye = (ri == ci).astype(F32)
    md = jnp.where(ri // 16 == ci // 16, m, 0.0)
    mo = m - md
    t0 = eye + md
    p2 = _hdot(md, md)
    t0 = t0 + _hdot(t0, p2)
    p4 = _hdot(p2, p2)
    t0 = t0 + _hdot(t0, p4)
    p8 = _hdot(p4, p4)
    t0 = t0 + _hdot(t0, p8)
    nn = _hdot(t0, mo)
    n2 = _hdot(nn, nn)
    t1 = eye + nn + n2 + _hdot(nn, n2)
    return _hdot(t1, t0)


@jax.custom_vjp
def _inv_saved(m, t_saved):
    return t_saved


def _inv_saved_fwd(m, t_saved):
    return t_saved, t_saved


def _inv_saved_bwd(t_saved, dt):
    tt = jnp.swapaxes(t_saved, -1, -2)
    return _hdot(_hdot(tt, dt), tt), jnp.zeros_like(t_saved)


_inv_saved.defvjp(_inv_saved_fwd, _inv_saved_bwd)


def _heads(x):
    return jnp.concatenate([x[None, :, h * HEAD_DIM:(h + 1) * HEAD_DIM] for h in range(HEADS)], axis=0)


def _unheads(y):
    return jnp.concatenate([lax.index_in_dim(y, h, 0, keepdims=False) for h in range(HEADS)], axis=1)


def rwkv_heads(s0, r, lw, k2, v, kk, a, inv):
    n = r.shape[0]
    ri, ci = _iota2(n, n)
    low, strict = ri >= ci, ri > ci
    cs = _cumsum_rows(lw)
    cl = jnp.sum(lw, axis=0, keepdims=True)
    p_in, p_prev, p_inv = jnp.exp(cs), jnp.exp(cs - lw), jnp.exp(-cs)
    p_rest, p_all = jnp.exp(cl - cs), jnp.exp(cl)
    bd = kk * a
    at, rt = _heads(-kk * p_prev), _heads(r * p_in)
    bh, kh = _heads(bd * p_inv), _heads(k2 * p_inv)
    vh = _heads(v)
    m_ab = jnp.where(strict, _cdot_nt(at, bh), 0.0)
    m_ak = jnp.where(strict, _cdot_nt(at, kh), 0.0)
    m_rb = jnp.where(low, _cdot_nt(rt, bh), 0.0)
    m_rk = jnp.where(low, _cdot_nt(rt, kh), 0.0)
    sa = _cdot(inv(m_ab), _cdot_nt(at, s0) + _cdot(m_ak, vh))
    y = _cdot_nt(rt, s0) + _cdot(m_rb, sa) + _cdot(m_rk, vh)
    s1 = s0 * _heads(p_all) + _cdot_tn(sa, _heads(bd * p_rest)) + _cdot_tn(vh, _heads(k2 * p_rest))
    return _unheads(y), s1


def gdn_heads(s0, q, k, v, ge, be, inv):
    n = q.shape[0]
    ri, ci = _iota2(n, n)
    low, strict = ri >= ci, ri > ci
    gc = _cumsum_rows(ge)
    gl = jnp.sum(ge, axis=0, keepdims=True)
    gch = _heads(gc)
    decay = jnp.where(low, jnp.exp(jnp.where(low, gch - jnp.swapaxes(gch, 1, 2), 0.0)), 0.0)
    kb = k * be
    e = jnp.exp(gc)
    kh = _heads(k)
    m = -jnp.where(strict, _cdot_nt(_heads(kb), kh) * decay, 0.0)
    mr = jnp.where(low, _cdot_nt(_heads(q), kh) * decay, 0.0)
    u = _cdot(inv(m), _heads(v * be) - _cdot_nt(_heads(kb * e), s0))
    y = _cdot_nt(_heads(q * e), s0) + _cdot(mr, u)
    s1 = s0 * _heads(jnp.exp(gl)) + _cdot_tn(u, _heads(k * jnp.exp(gl - gc)))
    return _unheads(y), s1


def core_fwd(heads_fn, ins, name, hosted=()):
    T = ins[0].shape[0]
    nc = T // CHUNK
    n = len(ins)
    gather = ChipGather(list(hosted))
    ng = gather.n

    def body(*refs):
        hx = refs[n:n + ng]
        y_ref, s0_ref, t_ref = refs[n + ng:n + ng + 3]
        ho = refs[n + ng + 3:n + 2 * ng + 3]
        s_ref = refs[n + 2 * ng + 3]
        sems = refs[n + 2 * ng + 4:]
        c = pl.program_id(0)

        @pl.when(c == 0)
        def _():
            gather.start(hx, ho, sems)
            s_ref[...] = jnp.zeros_like(s_ref)

        s0 = s_ref[...]
        kept = []

        def inv(m):
            kept.append(_neumann_inv(m))
            return kept[0]

        y, s1 = heads_fn(s0, *[r[...] for r in refs[:n]], inv)
        y_ref[...] = y
        s0_ref[0] = s0
        t_ref[0] = kept[0]
        s_ref[...] = s1

        @pl.when(c == nc - 4)
        def _():
            gather.relay(hx, ho, sems)

        @pl.when(c == nc - 1)
        def _():
            gather.finish(hx, ho, sems)

    row = pl.BlockSpec((CHUNK, MIX_W), lambda c: (c, 0))
    st_shape = (HEADS, HEAD_DIM, HEAD_DIM)
    st = pl.BlockSpec((1,) + st_shape, lambda c: (c, 0, 0, 0))
    res = pl.pallas_call(
        body, name=name, grid=(nc,), in_specs=[row] * n + gather.in_specs,
        out_specs=tuple([row, st, st] + gather.out_specs),
        out_shape=tuple([jax.ShapeDtypeStruct((T, MIX_W), F32), jax.ShapeDtypeStruct((nc,) + st_shape, F32),
                         jax.ShapeDtypeStruct((nc,) + st_shape, F32)] + gather.out_shape),
        scratch_shapes=[pltpu.VMEM(st_shape, F32)] + gather.scratch,
        compiler_params=_params(("arbitrary",)))(*ins, *hosted)
    return res[0], res[1], res[2], list(res[3:])


def core_bwd(heads_fn, ins, s0_all, t_all, dy, name):
    T = ins[0].shape[0]
    nc = T // CHUNK
    n = len(ins)

    def body(*refs):
        s0_ref, t_ref, dy_ref = refs[n:n + 3]
        outs = refs[n + 3:n + 3 + n]
        ds_ref = refs[n + 3 + n]

        @pl.when(pl.program_id(0) == 0)
        def _():
            ds_ref[...] = jnp.zeros_like(ds_ref)

        t_saved = t_ref[0]
        f = lambda s0, *xs: heads_fn(s0, *xs, lambda m: _inv_saved(m, t_saved))
        _, vjp = jax.vjp(f, s0_ref[0], *[r[...] for r in refs[:n]])
        grads = vjp((dy_ref[...], ds_ref[...]))
        ds_ref[...] = grads[0]
        for k in range(n):
            outs[k][...] = grads[1 + k]

    row = pl.BlockSpec((CHUNK, MIX_W), lambda c: (nc - 1 - c, 0))
    st_shape = (HEADS, HEAD_DIM, HEAD_DIM)
    st = pl.BlockSpec((1,) + st_shape, lambda c: (nc - 1 - c, 0, 0, 0))
    return pl.pallas_call(
        body, name=name, grid=(nc,), in_specs=[row] * n + [st, st, row], out_specs=tuple([row] * n),
        out_shape=tuple(jax.ShapeDtypeStruct((T, MIX_W), F32) for _ in range(n)),
        scratch_shapes=[pltpu.VMEM(st_shape, F32)],
        compiler_params=_params(("arbitrary",)))(*ins, s0_all, t_all, dy)


def _block_diag(w):
    out = jnp.zeros((LRU_W, LRU_W), w.dtype)
    for n in range(LRU_BLOCKS):
        out = lax.dynamic_update_slice(out, w[n], (n * 64, n * 64))
    return out


def _block_diag_grad(g):
    return jnp.stack([g[n * 64:(n + 1) * 64, n * 64:(n + 1) * 64] for n in range(LRU_BLOCKS)])


def _row(v):
    return v.reshape(1, -1)


def _pad128(v):
    return jnp.pad(v.reshape(1, -1), ((0, 0), (0, 128 - v.size)))


def _layer_shared(w, l):
    cw = w['lru_conv_w'][l]
    lru_pre = [_row(cw[0]), _row(cw[1]), _row(cw[2]), _row(cw[3]), _row(w['lru_conv_b'][l]),
               _block_diag(w['lru_gate_a_w'][l]), _row(w['lru_gate_a_b'][l]),
               _block_diag(w['lru_gate_x_w'][l]), _row(w['lru_gate_x_b'][l]), _row(w['lru_lambda'][l])]
    rw_pre = [_row(w['rwkv_mu'][l]), w['rwkv_w_up'][l], _row(w['rwkv_w_bias'][l]), w['rwkv_a_up'][l],
              _row(w['rwkv_a_bias'][l]), w['rwkv_g_up'][l], _row(w['rwkv_k_k'][l]), _row(w['rwkv_k_a'][l])]
    if l > 0:
        rw_pre += [w['rwkv_vres_w1'][l - 1], w['rwkv_vres_w2'][l - 1], _row(w['rwkv_vres_b'][l - 1])]
    rw_post = [_row(w['rwkv_ln_g'][l]), _row(w['rwkv_ln_b'][l]), _row(w['rwkv_r_k'][l])]
    gw = w['gdn_conv_w'][l]
    gdn_pre = [_row(gw[0]), _row(gw[1]), _row(gw[2]), _row(gw[3]), _pad128(w['gdn_a_log'][l]),
               _pad128(w['gdn_dt_bias'][l])]
    gdn_post = [_row(jnp.tile(w['gdn_norm'][l], HEADS))]
    return dict(lru_pre=lru_pre, lru_post=[_row(w['lru_out_norm'][l])], rw_pre=rw_pre, rw_post=rw_post,
                gdn_pre=gdn_pre, gdn_post=gdn_post)


def _mixer_fwd(p, sh, l, v_first, host_rwkv=(), host_gdn=()):
    T = p.shape[0]
    lx, ly = p[:, 0:256], p[:, 256:512]
    prw, qkv, z, ab = p[:, 512:1920], p[:, 1920:3072], p[:, 3072:3456], p[:, 3456:3584]
    first = (lax.broadcasted_iota(jnp.int32, (T, LRU_W), 0) == 0).astype(F32)
    lru_rows = [lx, shift_rows(lx, 1), shift_rows(lx, 2), shift_rows(lx, 3), first]
    a, b = rowwise_fwd(lru_pre_fn, lru_rows, sh['lru_pre'], (LRU_W, LRU_W), f"lru_pre_fwd{l}")
    hseq = lru_scan(a, b, False, f"lru_scan_fwd{l}")
    (y_lru,) = rowwise_fwd(lru_post_fn, [hseq, ly], sh['lru_post'], (LRU_W,), f"lru_post_fwd{l}")

    rw_rows = [prw, shift_rows(prw, 1)] + ([v_first] if l > 0 else [])
    rw = rowwise_fwd(make_rwkv_pre_fn(l > 0), rw_rows, sh['rw_pre'], (MIX_W,) * 7, f"rwkv_pre_fwd{l}")
    r, lw, k2, v, kk, ar, g = rw
    y_raw, rs0, rt, got_rwkv = core_fwd(rwkv_heads, [r, lw, k2, v, kk, ar], f"rwkv_core_fwd{l}", host_rwkv)
    (y_rw,) = rowwise_fwd(rwkv_post_fn, [y_raw, r, k2, v, g], sh['rw_post'], (MIX_W,), f"rwkv_post_fwd{l}")

    gdn_rows = [qkv, shift_rows(qkv, 1), shift_rows(qkv, 2), shift_rows(qkv, 3), ab]
    gd = rowwise_fwd(gdn_pre_fn, gdn_rows, sh['gdn_pre'], (MIX_W,) * 5, f"gdn_pre_fwd{l}")
    o_raw, gs0, gt, got_gdn = core_fwd(gdn_heads, list(gd), f"gdn_core_fwd{l}", host_gdn)
    (y_gdn,) = rowwise_fwd(gdn_post_fn, [o_raw, z], sh['gdn_post'], (MIX_W,), f"gdn_post_fwd{l}")

    mixed = jnp.concatenate([y_lru, y_rw, y_gdn], axis=1)
    saved = dict(lru_rows=lru_rows, a=a, hseq=hseq, ly=ly, rw_rows=rw_rows, rw=rw, y_raw=y_raw, rs0=rs0, rt=rt,
                 gdn_rows=gdn_rows, gd=gd, o_raw=o_raw, gs0=gs0, gt=gt, z=z)
    v_layer0 = v if l == 0 else None
    return mixed, saved, v_layer0, got_rwkv, got_gdn


def _mixer_bwd(dmixed, sv, sh, l, dv_first):
    d_lru, d_rw, d_gdn = dmixed[:, 0:256], dmixed[:, 256:640], dmixed[:, 640:1024]
    gw = {}

    dh, dly, d_og = rowwise_bwd(lru_post_fn, [sv['hseq'], sv['ly']], sh['lru_post'], [d_lru], f"lru_post_bwd{l}")
    gscan = lru_scan(unshift_rows(sv['a'], 1), dh, True, f"lru_scan_bwd{l}")
    res = rowwise_bwd(lru_pre_fn, sv['lru_rows'], sh['lru_pre'], [gscan, shift_rows(sv['hseq'], 1)],
                      f"lru_pre_bwd{l}", ct_fn=lambda gs, hp: (gs * hp, gs))
    dlx = res[0] + unshift_rows(res[1], 1) + unshift_rows(res[2], 2) + unshift_rows(res[3], 3)
    dw0, dw1, dw2, dw3, dcb, dga, dgab, dgx, dgxb, dlam = res[5:]
    gw['lru_conv_w'] = jnp.concatenate([dw0, dw1, dw2, dw3], axis=0)
    gw['lru_conv_b'] = dcb[0]
    gw['lru_gate_a_w'] = _block_diag_grad(dga)
    gw['lru_gate_a_b'] = dgab.reshape(LRU_BLOCKS, 64)
    gw['lru_gate_x_w'] = _block_diag_grad(dgx)
    gw['lru_gate_x_b'] = dgxb.reshape(LRU_BLOCKS, 64)
    gw['lru_lambda'] = dlam[0]
    gw['lru_out_norm'] = d_og[0]

    r, lw, k2, v, kk, ar, g = sv['rw']
    res = rowwise_bwd(rwkv_post_fn, [sv['y_raw'], r, k2, v, g], sh['rw_post'], [d_rw], f"rwkv_post_bwd{l}")
    dy_raw, dr_p, dk2_p, dv_p, dg = res[:5]
    gw['rwkv_ln_g'], gw['rwkv_ln_b'], gw['rwkv_r_k'] = res[5][0], res[6][0], res[7].reshape(HEADS, HEAD_DIM)
    dr_c, dlw, dk2_c, dv_c, dkk, dar = core_bwd(rwkv_heads, [r, lw, k2, v, kk, ar], sv['rs0'], sv['rt'], dy_raw,
                                                 f"rwkv_core_bwd{l}")
    cts = [dr_p, dr_c, dlw, dk2_p, dk2_c, dv_p, dv_c, dkk, dar, dg]
    if l == 0:
        cts.append(dv_first)
        ct_fn = lambda a1, a2, b, c1, c2, d1, d2, e, f, gg, vf: (a1 + a2, b, c1 + c2, d1 + d2 + vf, e, f, gg)
    else:
        ct_fn = lambda a1, a2, b, c1, c2, d1, d2, e, f, gg: (a1 + a2, b, c1 + c2, d1 + d2, e, f, gg)
    res = rowwise_bwd(make_rwkv_pre_fn(l > 0), sv['rw_rows'], sh['rw_pre'], cts, f"rwkv_pre_bwd{l}", ct_fn=ct_fn)
    dprw = res[0] + unshift_rows(res[1], 1)
    nrow = len(sv['rw_rows'])
    dv_first_out = res[2] if l > 0 else None
    sg = res[nrow:]
    gw['rwkv_mu'], gw['rwkv_w_up'], gw['rwkv_w_bias'], gw['rwkv_a_up'] = sg[0][0], sg[1], sg[2][0], sg[3]
    gw['rwkv_a_bias'], gw['rwkv_g_up'], gw['rwkv_k_k'], gw['rwkv_k_a'] = sg[4][0], sg[5], sg[6][0], sg[7][0]
    if l > 0:
        gw['rwkv_vres_w1'], gw['rwkv_vres_w2'], gw['rwkv_vres_b'] = sg[8], sg[9], sg[10][0]

    do_raw, dz, d_ng = rowwise_bwd(gdn_post_fn, [sv['o_raw'], sv['z']], sh['gdn_post'], [d_gdn], f"gdn_post_bwd{l}")
    gw['gdn_norm'] = jnp.sum(d_ng.reshape(HEADS, HEAD_DIM), axis=0)
    dgd = core_bwd(gdn_heads, list(sv['gd']), sv['gs0'], sv['gt'], do_raw, f"gdn_core_bwd{l}")
    res = rowwise_bwd(gdn_pre_fn, sv['gdn_rows'], sh['gdn_pre'], list(dgd), f"gdn_pre_bwd{l}")
    dqkv = res[0] + unshift_rows(res[1], 1) + unshift_rows(res[2], 2) + unshift_rows(res[3], 3)
    dab = res[4]
    gw['gdn_conv_w'] = jnp.concatenate(res[5:9], axis=0)
    gw['gdn_a_log'], gw['gdn_dt_bias'] = res[9][0, :HEADS], res[10][0, :HEADS]

    dp = jnp.concatenate([dlx, dly, dprw, dqkv, dz, dab], axis=1)
    return dp, gw, dv_first_out


IN_SHARD = D_IN // N_CHIPS
IN_SHARD_PAD = D_IN_PAD // N_CHIPS


def _cols_to_chips(g, n=N_CHIPS):
    r = g.shape[0]
    return jnp.transpose(g.reshape(r, n, -1), (1, 0, 2))


def _cols_from_chips(g):
    return jnp.transpose(g, (1, 0, 2)).reshape(g.shape[1], -1)


def _w_in_from_chips(g):
    nat = _cols_from_chips(g[:, :, :IN_SHARD])
    return jnp.pad(nat, ((0, 0), (0, D_IN_PAD - D_IN)))


def _w_in_to_chips(g):
    return jnp.pad(_cols_to_chips(g[:, :D_IN]), ((0, 0), (0, 0), (0, IN_SHARD_PAD - IN_SHARD)))


def _natural(name, g):
    if name == 'w_in':
        return _w_in_from_chips(g)
    if BIG[name] == 2:
        return _cols_from_chips(g)
    return g.reshape(-1, g.shape[2])


def local_step(x, target, w, wb, shards=None):
    def hosted(keys):
        return [shards[k] for k in keys] if shards is not None else []

    def arrived(keys, gathered):
        for (name, layer), g in zip(keys if shards is not None else [], gathered):
            wb[name][layer] = _natural(name, g)

    saved = []
    v_first = None
    for l in range(N_LAYERS):
        sh = _layer_shared(w, l)
        for_mixer = [('w_in', l), ('w_out', l)]
        for_ffn2 = [('ffn2_wi', l), ('ffn2_wo', l)]
        for_next = [('ffn1_wi', l + 1), ('ffn1_wo', l + 1)] if l + 1 < N_LAYERS else []
        x1, got = ffn_fwd(x, _row(w['ffn1_norm'][l]), wb['ffn1_wi'][l], wb['ffn1_wo'][l], f"ffn1_fwd{l}",
                          hosted(for_mixer))
        arrived(for_mixer, got)
        p = proj_fwd(x1, _row(w['mix_norm'][l]), wb['w_in'][l], f"proj_fwd{l}")
        mixed, sv, v0, got_ffn2, got_next = _mixer_fwd(p, sh, l, v_first, hosted(for_ffn2), hosted(for_next))
        arrived(for_ffn2, got_ffn2)
        arrived(for_next, got_next)
        if l == 0:
            v_first = v0
        x2 = out_fwd(mixed, wb['w_out'][l], x1, f"out_fwd{l}")
        x3, _ = ffn_fwd(x2, _row(w['ffn2_norm'][l]), wb['ffn2_wi'][l], wb['ffn2_wo'][l], f"ffn2_fwd{l}")
        saved.append(dict(x0=x, x1=x1, x2=x2, mixed=mixed, sv=sv, sh=sh))
        x = x3

    loss, dx, dgf = loss_head(x, _row(w['final_norm']), target, "loss_head")
    per_layer = [dict() for _ in range(N_LAYERS)]
    dv_first = jnp.zeros((x.shape[0], MIX_W), F32)

    waiting, chip_sums, arrived_parts = [], {}, {}

    def reduce_now(keys, tag):
        if shards is None:
            return
        sums = chip_sums_of([(n, k, per_layer[k][n]) for n, k in keys], lax.axis_index("c"), tag)
        for key, (total, total_bf) in zip(keys, sums):
            chip_sums[key] = total
            waiting.append((key, total_bf))

    def take_waiting():
        keys, parts = [k for k, _ in waiting], [p for _, p in waiting]
        waiting.clear()
        return keys, parts

    for l in reversed(range(N_LAYERS)):
        s = saved[l]
        gw = per_layer[l]
        keys, parts = take_waiting()
        dx, dg2, dwg, dwu, dwo, got = ffn_bwd(s['x2'], dx, _row(w['ffn2_norm'][l]), wb['ffn2_wi'][l],
                                              wb['ffn2_wo'][l], f"ffn2_bwd{l}", parts)
        arrived_parts.update(zip(keys, got))
        wi_parts = lambda dwg, dwu: (dwg, dwu)
        row_parts = lambda dw: dw.reshape(N_CHIPS, -1, dw.shape[1])
        gw['ffn2_norm'], gw['ffn2_wi'], gw['ffn2_wo'] = dg2[0], wi_parts(dwg, dwu), row_parts(dwo)
        if l == N_LAYERS - 1:
            reduce_now([('ffn2_wi', l), ('ffn2_wo', l)], f"ffn2_{l}")
        dmixed, dw_out = out_bwd(s['mixed'], wb['w_out'][l], dx, f"out_bwd{l}")
        gw['w_out'] = row_parts(dw_out)
        dp, gmix, dvf = _mixer_bwd(dmixed, s['sv'], s['sh'], l, dv_first)
        if l > 0:
            dv_first = dvf
        gw.update(gmix)
        dx, dgm, dwin = proj_bwd(s['x1'], dx, _row(w['mix_norm'][l]), wb['w_in'][l], dp, f"proj_bwd{l}")
        gw['mix_norm'], gw['w_in'] = dgm[0], _w_in_to_chips(dwin)
        if l < N_LAYERS - 1:
            reduce_now([('ffn2_wi', l), ('ffn2_wo', l), ('w_in', l), ('w_out', l)], f"mix_{l}")
        keys, parts = take_waiting()
        dx, dg1, dwg, dwu, dwo, got = ffn_bwd(s['x0'], dx, _row(w['ffn1_norm'][l]), wb['ffn1_wi'][l],
                                              wb['ffn1_wo'][l], f"ffn1_bwd{l}", parts)
        arrived_parts.update(zip(keys, got))
        gw['ffn1_norm'], gw['ffn1_wi'], gw['ffn1_wo'] = dg1[0], wi_parts(dwg, dwu), row_parts(dwo)
        if l == N_LAYERS - 1:
            reduce_now([('w_in', l), ('w_out', l), ('ffn1_wi', l), ('ffn1_wo', l)], f"ffn1_{l}")
        else:
            reduce_now([('ffn1_wi', l), ('ffn1_wo', l)], f"ffn1_{l}")
    if shards is not None:
        keys, parts = take_waiting()
        arrived_parts.update(zip(keys, scatter_chips(parts, "grad_scatter_last")))

    grads = {'final_norm': dgf[0]}
    for name in WEIGHTS:
        if name == 'final_norm':
            continue
        if name in BIG:
            if shards is None:
                grads[name] = [per_layer[l][name] for l in range(N_LAYERS)]
            else:
                grads[name] = [(chip_sums[(name, l)], arrived_parts[(name, l)]) for l in range(N_LAYERS)]
        elif name.startswith('rwkv_vres'):
            grads[name] = per_layer[1][name][None]
        else:
            grads[name] = jnp.stack([per_layer[l][name] for l in range(N_LAYERS)])
    return loss[0, 0], dx, grads


ANY = pl.BlockSpec(memory_space=pl.ANY)


def _coords():
    return lax.axis_index("x"), lax.axis_index("y"), lax.axis_index("c")


def _other_chips(x, y):
    return [((x + 1) % 2, y), (x, (y + 1) % 2), ((x + 1) % 2, (y + 1) % 2)]


def allreduce_small(pack, name):
    R = pack.shape[0]
    rh = R // 2

    def body(x_ref, o_ref, sib_ref, chip_ref, parts_ref, send_sems, recv_sems):
        x, y, c = _coords()
        sib = (x, y, 1 - c)

        def copy(k, src, dst, to):
            return pltpu.make_async_remote_copy(src_ref=src, dst_ref=dst, send_sem=send_sems.at[k],
                                                recv_sem=recv_sems.at[k], device_id=to, device_id_type=MESH)

        swap = copy(0, x_ref, sib_ref, sib)
        swap.start()
        swap.wait()
        chip_ref[...] = jnp.where(c == 0, x_ref[...], sib_ref[...]) + jnp.where(c == 0, sib_ref[...], x_ref[...])

        mine = pl.ds(pl.multiple_of(c * rh, 8), rh)
        sends = [copy(1 + j, chip_ref.at[mine], parts_ref.at[j], (px, py, c))
                 for j, (px, py) in enumerate(_other_chips(x, y))]
        for cp in sends:
            cp.start()
        for cp in sends:
            cp.wait()
        s = 2 * x + y
        own = chip_ref[mine, :]
        from_chip = {2: parts_ref[0], 1: parts_ref[1], 3: parts_ref[2]}
        terms = []
        for k in range(N_CHIPS):
            t = own
            for d, part in from_chip.items():
                t = jnp.where(jnp.bitwise_xor(s, d) == k, part, t)
            terms.append(t)
        o_ref[mine, :] = ((terms[0] + terms[1]) + terms[2]) + terms[3]

        share = copy(4, o_ref.at[mine], o_ref.at[mine], sib)
        share.start()
        share.wait()

    vm = pl.BlockSpec(memory_space=pltpu.VMEM)
    return pl.pallas_call(
        body, name=name, in_specs=[vm], out_specs=vm, out_shape=jax.ShapeDtypeStruct((R, 128), F32),
        scratch_shapes=[pltpu.VMEM((R, 128), F32), pltpu.VMEM((R, 128), F32), pltpu.VMEM((3, rh, 128), F32),
                        pltpu.SemaphoreType.DMA((5,)), pltpu.SemaphoreType.DMA((5,))],
        compiler_params=_params())(pack)


class ChipGather:
    def __init__(self, shards):
        self.shapes = [s.shape for s in shards]
        self.n = len(shards)
        self.in_specs = [ANY] * self.n
        self.out_specs = [ANY] * self.n
        self.out_shape = [jax.ShapeDtypeStruct((N_CHIPS,) + s.shape, s.dtype) for s in shards]
        self.scratch = [pltpu.SemaphoreType.DMA((6 * self.n,)), pltpu.SemaphoreType.DMA((6 * self.n,)),
                        pltpu.SemaphoreType.DMA((self.n,))] if self.n else []

    def _rows(self, a, core):
        rh = self.shapes[a][0] // 2
        return pl.ds(pl.multiple_of(core * rh, 16), rh)

    def _copies(self, kind, x_refs, o_refs, sems):
        send_sems, recv_sems, local_sems = sems
        x, y, c = _coords()
        s_me = 2 * x + y
        sib = (x, y, 1 - c)

        def copy(a, k, src, dst, to):
            return pltpu.make_async_remote_copy(src_ref=src, dst_ref=dst, send_sem=send_sems.at[6 * a + k],
                                                recv_sem=recv_sems.at[6 * a + k], device_id=to, device_id_type=MESH)

        if kind == 'own':
            return [pltpu.make_async_copy(x_refs[a], o_refs[a].at[s_me], local_sems.at[a]) for a in range(self.n)]
        out = []
        for j, (px, py) in enumerate(_other_chips(x, y)):
            for a in range(self.n):
                mine = self._rows(a, c)
                part = o_refs[a].at[2 * px + py, mine]
                if kind == 'sent':
                    out.append(copy(a, j, x_refs[a].at[mine], o_refs[a].at[s_me, mine], (px, py, c)))
                elif kind == 'arrived':
                    out.append(copy(a, j, part, part, (px, py, c)))
                elif kind == 'passed':
                    out.append(copy(a, 3 + j, part, part, sib))
                else:
                    theirs = o_refs[a].at[2 * px + py, self._rows(a, 1 - c)]
                    out.append(copy(a, 3 + j, theirs, theirs, sib))
        return out

    def start(self, x_refs, o_refs, sems):
        if not self.n:
            return
        for cp in self._copies('own', x_refs, o_refs, sems) + self._copies('sent', x_refs, o_refs, sems):
            cp.start()

    def relay(self, x_refs, o_refs, sems):
        if not self.n:
            return
        for got, fw in zip(self._copies('arrived', x_refs, o_refs, sems),
                           self._copies('passed', x_refs, o_refs, sems)):
            got.wait_recv()
            fw.start()

    def finish(self, x_refs, o_refs, sems):
        if not self.n:
            return
        for cp in self._copies('from_sibling', x_refs, o_refs, sems):
            cp.wait_recv()
        for cp in self._copies('sent', x_refs, o_refs, sems) + self._copies('passed', x_refs, o_refs, sems):
            cp.wait_send()
        for cp in self._copies('own', x_refs, o_refs, sems):
            cp.wait()


def allgather_chips(shards, name):
    gather = ChipGather(shards)
    n = gather.n

    def body(*refs):
        x_refs, o_refs, sems = refs[:n], refs[n:2 * n], refs[2 * n:]
        gather.start(x_refs, o_refs, sems)
        gather.relay(x_refs, o_refs, sems)
        gather.finish(x_refs, o_refs, sems)

    return pl.pallas_call(
        body, name=name, in_specs=gather.in_specs, out_specs=tuple(gather.out_specs),
        out_shape=tuple(gather.out_shape), scratch_shapes=gather.scratch, compiler_params=_params())(*shards)


def sibling_swap(srcs, halves, name):
    n = len(srcs)
    row_axis = [s.ndim - 2 for s in srcs]
    out_shapes = [s.shape[:ax] + (s.shape[ax] // 2,) + s.shape[ax + 1:] if halves else s.shape
                  for s, ax in zip(srcs, row_axis)]

    def body(*refs):
        x_refs, o_refs = refs[:n], refs[n:2 * n]
        send_sems, recv_sems = refs[2 * n:]
        x, y, c = _coords()
        copies = []
        for a in range(n):
            part = x_refs[a]
            if halves:
                rh = srcs[a].shape[row_axis[a]] // 2
                theirs = pl.ds(pl.multiple_of((1 - c) * rh, 16), rh)
                part = part.at[:, theirs] if row_axis[a] == 1 else part.at[theirs]
            cp = pltpu.make_async_remote_copy(src_ref=part, dst_ref=o_refs[a], send_sem=send_sems.at[a],
                                              recv_sem=recv_sems.at[a], device_id=(x, y, 1 - c), device_id_type=MESH)
            cp.start()
            copies.append(cp)
        for cp in copies:
            cp.wait()

    return pl.pallas_call(
        body, name=name, in_specs=[ANY] * n, out_specs=tuple([ANY] * n),
        out_shape=tuple(jax.ShapeDtypeStruct(sh, s.dtype) for sh, s in zip(out_shapes, srcs)),
        scratch_shapes=[pltpu.SemaphoreType.DMA((n,)), pltpu.SemaphoreType.DMA((n,))],
        compiler_params=_params())(*srcs)


class ChipScatter:
    def __init__(self, parts):
        self.n = len(parts)
        self.in_specs = [ANY] * self.n
        self.out_specs = [ANY] * self.n
        self.out_shape = [jax.ShapeDtypeStruct((3,) + p.shape[1:], p.dtype) for p in parts]
        self.scratch = [pltpu.SemaphoreType.DMA((3 * self.n,)), pltpu.SemaphoreType.DMA((3 * self.n,))] if self.n else []

    def _copies(self, x_refs, o_refs, sems):
        send_sems, recv_sems = sems
        x, y, c = _coords()
        return [pltpu.make_async_remote_copy(src_ref=x_refs[a].at[2 * px + py], dst_ref=o_refs[a].at[j],
                                             send_sem=send_sems.at[3 * a + j], recv_sem=recv_sems.at[3 * a + j],
                                             device_id=(px, py, c), device_id_type=MESH)
                for j, (px, py) in enumerate(_other_chips(x, y)) for a in range(self.n)]

    def start(self, x_refs, o_refs, sems):
        if self.n:
            for cp in self._copies(x_refs, o_refs, sems):
                cp.start()

    def finish(self, x_refs, o_refs, sems):
        if self.n:
            for cp in self._copies(x_refs, o_refs, sems):
                cp.wait()


def scatter_chips(parts, name):
    scatter = ChipScatter(parts)
    n = scatter.n

    def body(*refs):
        x_refs, o_refs, sems = refs[:n], refs[n:2 * n], refs[2 * n:]
        scatter.start(x_refs, o_refs, sems)
        scatter.finish(x_refs, o_refs, sems)

    return pl.pallas_call(
        body, name=name, in_specs=scatter.in_specs, out_specs=tuple(scatter.out_specs),
        out_shape=tuple(scatter.out_shape), scratch_shapes=scatter.scratch, compiler_params=_params())(*parts)


def _row_block(rows):
    return max(b for b in range(16, 257, 16) if rows % b == 0)


def chip_sum(gpack, recv, core, name):
    n, R, W = gpack.shape
    rh = R // 2
    rb = _row_block(rh)
    nb = rh // rb

    def body(c_ref, g_ref, r_ref, o_ref, ob_ref):
        s = g_ref[...] + r_ref[...]
        o_ref[...] = s
        ob_ref[...] = s.astype(BF16)

    blk = pl.BlockSpec((1, rb, W), lambda i, j, c_ref: (i, j, 0))
    spec = pltpu.PrefetchScalarGridSpec(
        num_scalar_prefetch=1, grid=(n, nb),
        in_specs=[pl.BlockSpec((1, rb, W), lambda i, j, c_ref: (i, c_ref[0] * nb + j, 0)), blk],
        out_specs=(blk, blk))
    return pl.pallas_call(
        body, name=name, grid_spec=spec,
        out_shape=(jax.ShapeDtypeStruct((n, rh, W), F32), jax.ShapeDtypeStruct((n, rh, W), BF16)),
        compiler_params=_params(("arbitrary", "arbitrary")))(core, gpack, recv)


def chip_sum_cols(gate, up, recv_gate, recv_up, core, name):
    R, W = gate.shape
    cw = W // 2
    rh = R // 2
    rb = _row_block(rh)
    nb = rh // rb

    def body(c_ref, g_ref, u_ref, rg_ref, ru_ref, o_ref, ob_ref):
        s = jnp.where(pl.program_id(0) < 2, g_ref[...] + rg_ref[...], u_ref[...] + ru_ref[...])
        o_ref[0] = s
        ob_ref[0] = s.astype(BF16)

    gate_col = lambda s: jnp.minimum(s, 1)
    up_col = lambda s: jnp.maximum(s - 2, 0)
    out = pl.BlockSpec((1, rb, cw), lambda s, j, c_ref: (s, j, 0))
    spec = pltpu.PrefetchScalarGridSpec(
        num_scalar_prefetch=1, grid=(N_CHIPS, nb),
        in_specs=[pl.BlockSpec((rb, cw), lambda s, j, c_ref: (c_ref[0] * nb + j, gate_col(s))),
                  pl.BlockSpec((rb, cw), lambda s, j, c_ref: (c_ref[0] * nb + j, up_col(s))),
                  pl.BlockSpec((rb, cw), lambda s, j, c_ref: (j, gate_col(s))),
                  pl.BlockSpec((rb, cw), lambda s, j, c_ref: (j, up_col(s)))],
        out_specs=(out, out))
    return pl.pallas_call(
        body, name=name, grid_spec=spec,
        out_shape=(jax.ShapeDtypeStruct((N_CHIPS, rh, cw), F32), jax.ShapeDtypeStruct((N_CHIPS, rh, cw), BF16)),
        compiler_params=_params(("arbitrary", "arbitrary")))(core, gate, up, recv_gate, recv_up)


def chip_sums_of(items, core, tag):
    parts = []
    for _, _, g in items:
        parts += list(g) if isinstance(g, tuple) else [g]
    swapped = iter(zip(parts, sibling_swap(parts, True, f"grad_swap_cores_{tag}")))
    core_arg = core.reshape(1).astype(jnp.int32)
    sums = []
    for n, l, g in items:
        if isinstance(g, tuple):
            (dwg, from_g), (dwu, from_u) = next(swapped), next(swapped)
            sums.append(chip_sum_cols(dwg, dwu, from_g, from_u, core_arg, f"grad_chip_sum_{n}{l}"))
        else:
            p, r = next(swapped)
            sums.append(chip_sum(p, r, core_arg, f"grad_chip_sum_{n}{l}"))
    return sums


def shard_sum(own, recv, name):
    R, W = own.shape
    rb = _row_block(R)

    def body(a_ref, r_ref, o_ref):
        acc = a_ref[...]
        for j in range(3):
            acc = acc + r_ref[j].astype(F32)
        o_ref[...] = acc

    return pl.pallas_call(
        body, name=name, grid=(R // rb,),
        in_specs=[pl.BlockSpec((rb, W), lambda i: (i, 0)), pl.BlockSpec((3, rb, W), lambda i: (0, i, 0))],
        out_specs=pl.BlockSpec((rb, W), lambda i: (i, 0)), out_shape=jax.ShapeDtypeStruct((R, W), F32),
        compiler_params=_params(("arbitrary",)))(own, recv)


def adamw(w, m, v, g, name):
    L, R, C = w.shape
    rb = max(b for b in range(8, 257, 8) if R % b == 0)
    bc1 = 1.0 - ADAM_B1 ** ADAM_STEP
    bc2 = 1.0 - ADAM_B2 ** ADAM_STEP

    def body(w_ref, m_ref, v_ref, g_ref, d_ref, nm_ref, nv_ref):
        gv = g_ref[...]
        nm = ADAM_B1 * m_ref[...] + (1.0 - ADAM_B1) * gv
        nv = ADAM_B2 * v_ref[...] + (1.0 - ADAM_B2) * (gv * gv)
        d_ref[...] = -ADAM_LR * ((nm / bc1) / (jnp.sqrt(nv / bc2) + ADAM_EPS) + ADAM_WD * w_ref[...])
        nm_ref[...] = nm
        nv_ref[...] = nv

    blk = pl.BlockSpec((1, rb, C), lambda l, i: (l, i, 0))
    sh = jax.ShapeDtypeStruct((L, R, C), F32)
    return pl.pallas_call(body, name=name, grid=(L, R // rb), in_specs=[blk] * 4, out_specs=(blk,) * 3,
                          out_shape=(sh, sh, sh), compiler_params=_params(("arbitrary", "arbitrary")))(w, m, v, g)


SMALL = [n for n in WEIGHTS if n not in BIG]


PACK_TILE = 8 * 128


def _pack(arrays):
    blocks = []
    for a in arrays:
        flat = a.reshape(-1)
        flat = jnp.pad(flat, (0, -flat.size % PACK_TILE))
        blocks.append(flat.reshape(-1, 128))
    rows = sum(b.shape[0] for b in blocks)
    if rows % 16:
        blocks.append(jnp.zeros((8, 128), arrays[0].dtype))
    return jnp.concatenate(blocks, axis=0)


def _unpack(pack, shapes):
    out, row = [], 0
    for shape in shapes:
        size = int(np.prod(shape))
        rows = -(-size // PACK_TILE) * 8
        out.append(pack[row:row + rows].reshape(-1)[:size].reshape(shape))
        row += rows
    return out


def _pad_lanes(a):
    return jnp.pad(a, ((0, 0), (0, -a.shape[1] % 128)))


def _local_shard(full, axis, chip):
    size = full.shape[axis] // N_CHIPS
    return lax.dynamic_slice_in_dim(full, chip * size, size, axis)


def kernel(x, ffn1_norm, ffn1_wi, ffn1_wo, mix_norm, w_in, w_out, lru_conv_w, lru_conv_b, lru_gate_a_w, lru_gate_a_b, lru_gate_x_w, lru_gate_x_b, lru_lambda, lru_out_norm, rwkv_mu, rwkv_w_up, rwkv_w_bias, rwkv_a_up, rwkv_a_bias, rwkv_g_up, rwkv_k_k, rwkv_k_a, rwkv_r_k, rwkv_ln_g, rwkv_ln_b, rwkv_vres_w1, rwkv_vres_w2, rwkv_vres_b, gdn_conv_w, gdn_a_log, gdn_dt_bias, gdn_norm, ffn2_norm, ffn2_wi, ffn2_wo, final_norm, loss_target, m_ffn1_norm, m_ffn1_wi, m_ffn1_wo, m_mix_norm, m_w_in, m_w_out, m_lru_conv_w, m_lru_conv_b, m_lru_gate_a_w, m_lru_gate_a_b, m_lru_gate_x_w, m_lru_gate_x_b, m_lru_lambda, m_lru_out_norm, m_rwkv_mu, m_rwkv_w_up, m_rwkv_w_bias, m_rwkv_a_up, m_rwkv_a_bias, m_rwkv_g_up, m_rwkv_k_k, m_rwkv_k_a, m_rwkv_r_k, m_rwkv_ln_g, m_rwkv_ln_b, m_rwkv_vres_w1, m_rwkv_vres_w2, m_rwkv_vres_b, m_gdn_conv_w, m_gdn_a_log, m_gdn_dt_bias, m_gdn_norm, m_ffn2_norm, m_ffn2_wi, m_ffn2_wo, m_final_norm, v_ffn1_norm, v_ffn1_wi, v_ffn1_wo, v_mix_norm, v_w_in, v_w_out, v_lru_conv_w, v_lru_conv_b, v_lru_gate_a_w, v_lru_gate_a_b, v_lru_gate_x_w, v_lru_gate_x_b, v_lru_lambda, v_lru_out_norm, v_rwkv_mu, v_rwkv_w_up, v_rwkv_w_bias, v_rwkv_a_up, v_rwkv_a_bias, v_rwkv_g_up, v_rwkv_k_k, v_rwkv_k_a, v_rwkv_r_k, v_rwkv_ln_g, v_rwkv_ln_b, v_rwkv_vres_w1, v_rwkv_vres_w2, v_rwkv_vres_b, v_gdn_conv_w, v_gdn_a_log, v_gdn_dt_bias, v_gdn_norm, v_ffn2_norm, v_ffn2_wi, v_ffn2_wo, v_final_norm):
    args = locals()
    w_loc = {n: args[n] for n in WEIGHTS}
    m_loc = {n: args['m_' + n] for n in WEIGHTS}
    v_loc = {n: args['v_' + n] for n in WEIGHTS}
    chip = 2 * lax.axis_index("x") + lax.axis_index("y")
    core = lax.axis_index("c")

    big = [(n, l) for n in BIG for l in range(N_LAYERS)]
    shards = {(n, l): _pad_lanes(w_loc[n][l].astype(BF16)) for n, l in big}
    first = [('ffn1_wi', 0), ('ffn1_wo', 0)]
    wb = {n: [None] * N_LAYERS for n in BIG}
    for (n, l), g in zip(first, allgather_chips([shards[k] for k in first], "allgather_first")):
        wb[n][l] = _natural(n, g)

    sm_names = list(SMALL_SHARDED)
    placed = []
    for n in sm_names:
        mine = [jnp.where((chip == s) & (core == 0), w_loc[n], 0.0) for s in range(N_CHIPS)]
        placed.append(jnp.concatenate(mine, axis=SMALL_SHARDED[n]))
    summed = allreduce_small(_pack(placed), "allgather_small")
    w_full = dict(w_loc)
    w_full.update(zip(sm_names, _unpack(summed, [p.shape for p in placed])))

    loss, dx, grads = local_step(x[0], loss_target[0], w_full, wb, shards)
    loss = lax.psum(loss, ("x", "y", "c"))

    gsum = allreduce_small(_pack([grads[n] for n in SMALL]), "allreduce_small")
    g_loc = {}
    for n, g in zip(SMALL, _unpack(gsum, [grads[n].shape for n in SMALL])):
        g_loc[n] = _local_shard(g, SMALL_SHARDED[n], chip) if n in SMALL_SHARDED else g

    halves =[shard_sum(lax.dynamic_index_in_dim(grads[n][l][0], chip, 0, keepdims=False), grads[n][l][1],
                        f"grad_shard_sum_{n}{l}") for n, l in big]
    others = sibling_swap(halves, False, "grad_share_cores")
    rows = {n: [None] * N_LAYERS for n in BIG}
    for (n, l), half, other in zip(big, halves, others):
        lower = jnp.where(core == 0, half, other)
        upper = jnp.where(core == 0, other, half)
        rows[n][l] = jnp.concatenate([lower, upper], axis=0)[:, :w_loc[n].shape[-1]]
    big_names = list(BIG)
    for n in big_names:
        g_loc[n] = jnp.stack(rows[n])

    delta, new_m, new_v = {}, {}, {}
    for n in big_names:
        delta[n], new_m[n], new_v[n] = adamw(w_loc[n], m_loc[n], v_loc[n], g_loc[n], f"adamw_{n}")
    pack = lambda d: _pack([d[n] for n in SMALL])[None]
    res = adamw(pack(w_loc), pack(m_loc), pack(v_loc), pack(g_loc), "adamw_small")
    for dst, r in zip((delta, new_m, new_v), res):
        dst.update(zip(SMALL, _unpack(r[0], [w_loc[n].shape for n in SMALL])))

    return (loss, dx[None], *[g_loc[n] for n in WEIGHTS], *[delta[n] for n in WEIGHTS],
            *[new_m[n] for n in WEIGHTS], *[new_v[n] for n in WEIGHTS])
```

```python
import functools

import numpy as np
import jax
import jax.numpy as jnp
from jax import lax
from jax.experimental import pallas as pl
from jax.experimental.pallas import tpu as pltpu

F32 = jnp.float32
BF16 = jnp.bfloat16
MESH = pl.DeviceIdType.MESH

D_MODEL = 1024
D_FF = 2816
N_LAYERS = 2
HEADS = 6
HEAD_DIM = 64
MIX_W = HEADS * HEAD_DIM
LRU_W = 256
LRU_BLOCKS = 4
D_IN = 3468
D_IN_PAD = 3584
NORM_EPS = 1e-6
GN_EPS = 64e-5
LRU_C = 8.0
CHUNK = 64
ROWS = 256
FF_CHUNK = 256
IN_CHUNK = 512
VMEM_LIMIT = 56 * 1024 * 1024

ADAM_LR, ADAM_B1, ADAM_B2, ADAM_EPS, ADAM_WD, ADAM_STEP = 0.001, 0.9, 0.999, 1e-08, 0.01, 10

WEIGHTS = ['ffn1_norm', 'ffn1_wi', 'ffn1_wo', 'mix_norm', 'w_in', 'w_out', 'lru_conv_w', 'lru_conv_b',
           'lru_gate_a_w', 'lru_gate_a_b', 'lru_gate_x_w', 'lru_gate_x_b', 'lru_lambda', 'lru_out_norm',
           'rwkv_mu', 'rwkv_w_up', 'rwkv_w_bias', 'rwkv_a_up', 'rwkv_a_bias', 'rwkv_g_up', 'rwkv_k_k',
           'rwkv_k_a', 'rwkv_r_k', 'rwkv_ln_g', 'rwkv_ln_b', 'rwkv_vres_w1', 'rwkv_vres_w2', 'rwkv_vres_b',
           'gdn_conv_w', 'gdn_a_log', 'gdn_dt_bias', 'gdn_norm', 'ffn2_norm', 'ffn2_wi', 'ffn2_wo', 'final_norm']
BIG = {'ffn1_wi': 2, 'ffn1_wo': 1, 'w_in': 2, 'w_out': 1, 'ffn2_wi': 2, 'ffn2_wo': 1}
SMALL_SHARDED = {'lru_conv_w': 2, 'rwkv_w_up': 2, 'rwkv_a_up': 2, 'rwkv_g_up': 2, 'rwkv_vres_w1': 1,
                 'rwkv_vres_w2': 2, 'gdn_conv_w': 2}
N_CHIPS = 4


def _params(sem=None):
    kw = dict(vmem_limit_bytes=VMEM_LIMIT)
    if sem is not None:
        kw['dimension_semantics'] = sem
    return pltpu.CompilerParams(**kw)


def _bdot(a, b, dims=(((1,), (0,)), ((), ()))):
    return lax.dot_general(a.astype(BF16), b.astype(BF16), dims, preferred_element_type=F32)


def _bdot_nt(a, b):
    return _bdot(a, b, (((1,), (1,)), ((), ())))


def _bdot_tn(a, b):
    return _bdot(a, b, (((0,), (0,)), ((), ())))


_DIMS = {'nn': (((1,), (0,)), ((), ())), 'nt': (((1,), (1,)), ((), ())), 'tn': (((0,), (0,)), ((), ()))}


def _split(a, terms):
    parts = []
    for _ in range(terms - 1):
        hi = a.astype(BF16)
        parts.append(hi)
        a = a - hi.astype(F32)
    parts.append(a.astype(BF16))
    return parts


_BATCH_DIMS = {'nn': (((2,), (1,)), ((0,), (0,))), 'nt': (((2,), (2,)), ((0,), (0,))),
               'tn': (((1,), (1,)), ((0,), (0,)))}


def _dot3(a, b, kind):
    ah, al = _split(a, 2)
    bh, bl = _split(b, 2)
    dims = _BATCH_DIMS[kind] if a.ndim == 3 else _DIMS[kind]
    d = lambda p, q: lax.dot_general(p, q, dims, preferred_element_type=F32)
    return d(ah, bh) + (d(ah, bl) + d(al, bh))


@functools.partial(jax.custom_vjp, nondiff_argnums=(2,))
def _cdot_k(a, b, kind):
    return _dot3(a, b, kind)


def _cdot_k_fwd(a, b, kind):
    return _dot3(a, b, kind), (a, b)


def _cdot_k_bwd(kind, res, ct):
    a, b = res
    if kind == 'nn':
        return _dot3(ct, b, 'nt'), _dot3(a, ct, 'tn')
    if kind == 'nt':
        return _dot3(ct, b, 'nn'), _dot3(ct, a, 'tn')
    return _dot3(b, ct, 'nt'), _dot3(a, ct, 'nn')


_cdot_k.defvjp(_cdot_k_fwd, _cdot_k_bwd)


def _dot1(a, b, kind):
    dims = _BATCH_DIMS[kind] if a.ndim == 3 else _DIMS[kind]
    return lax.dot_general(a.astype(BF16), b.astype(BF16), dims, preferred_element_type=F32)


@functools.partial(jax.custom_vjp, nondiff_argnums=(2,))
def _cdot1_k(a, b, kind):
    return _dot1(a, b, kind)


def _cdot1_k_fwd(a, b, kind):
    return _dot1(a, b, kind), (a, b)


def _cdot1_k_bwd(kind, res, ct):
    a, b = res
    if kind == 'nn':
        return _dot1(ct, b, 'nt'), _dot1(a, ct, 'tn')
    if kind == 'nt':
        return _dot1(ct, b, 'nn'), _dot1(ct, a, 'tn')
    return _dot1(b, ct, 'nt'), _dot1(a, ct, 'nn')


_cdot1_k.defvjp(_cdot1_k_fwd, _cdot1_k_bwd)


def _cdot(a, b):
    return _cdot1_k(a, b, 'nn')


def _cdot_nt(a, b):
    return _cdot1_k(a, b, 'nt')


def _cdot_tn(a, b):
    return _cdot1_k(a, b, 'tn')


def _hdot(a, b):
    return _cdot_k(a, b, 'nn')


def _dot_exact(x, m01, kind):
    d = lambda p: lax.dot_general(p, m01.astype(BF16), _DIMS[kind], preferred_element_type=F32)
    hi, mid, lo = _split(x, 3)
    return d(hi) + (d(mid) + d(lo))


@functools.partial(jax.custom_vjp, nondiff_argnums=(1,))
def _xdot(x, make_m):
    return _dot_exact(x, make_m(), 'nn')


def _xdot_fwd(x, make_m):
    return _dot_exact(x, make_m(), 'nn'), None


def _xdot_bwd(make_m, _, ct):
    return (_dot_exact(ct, make_m(), 'nt'),)


_xdot.defvjp(_xdot_fwd, _xdot_bwd)


def _iota2(n, m):
    return lax.broadcasted_iota(jnp.int32, (n, m), 0), lax.broadcasted_iota(jnp.int32, (n, m), 1)


def _head_blocks(w):
    ri, ci = _iota2(w, w)
    return (ri // HEAD_DIM == ci // HEAD_DIM).astype(F32)


def _segsum(x):
    return _xdot(x, functools.partial(_head_blocks, x.shape[-1]))


def _cumsum_rows(x):
    return _cumsum_k(x, x.shape[0])


@functools.partial(jax.custom_vjp, nondiff_argnums=(1,))
def _cumsum_k(x, n):
    return _lower_dot(x, n, False)


def _lower_dot(x, n, transpose):
    ri, ci = _iota2(n, n)
    m = ((ri <= ci) if transpose else (ri >= ci)).astype(BF16)
    d = lambda p: lax.dot_general(m, p, _DIMS['nn'], preferred_element_type=F32)
    hi, mid, lo = _split(x, 3)
    return d(hi) + (d(mid) + d(lo))


def _cumsum_k_fwd(x, n):
    return _lower_dot(x, n, False), None


def _cumsum_k_bwd(n, _, ct):
    return (_lower_dot(ct, n, True),)


_cumsum_k.defvjp(_cumsum_k_fwd, _cumsum_k_bwd)


def _rms(x, g):
    return x * lax.rsqrt(jnp.mean(x * x, axis=-1, keepdims=True) + NORM_EPS) * g


DENSE_ROWS = 1024


def _row_loop(n_rows, fn):
    rows = min(DENSE_ROWS, n_rows)

    def step(i, c):
        fn(pl.ds(pl.multiple_of(i * rows, rows), rows))
        return c
    lax.fori_loop(0, n_rows // rows, step, 0)


def ffn_fwd(x, g, wi, wo, name, hosted=()):
    T = x.shape[0]
    nj = D_FF // FF_CHUNK
    gather = ChipGather(list(hosted))
    n = gather.n

    def body(*refs):
        x_ref, g_ref, wg_ref, wu_ref, wo_ref = refs[:5]
        hx, o_ref, ho = refs[5:5 + n], refs[5 + n], refs[6 + n:6 + 2 * n]
        h_ref, acc_ref = refs[6 + 2 * n:8 + 2 * n]
        sems = refs[8 + 2 * n:]
        j = pl.program_id(0)

        @pl.when(j == 0)
        def _():
            gather.start(hx, ho, sems)

            def init(r):
                h_ref[r, :] = _rms(x_ref[r, :], g_ref[...]).astype(BF16)
                acc_ref[r, :] = jnp.zeros((r.size, D_MODEL), F32)
            _row_loop(T, init)

        def blk(r):
            hb = h_ref[r, :]
            gate = jnp.dot(hb, wg_ref[...], preferred_element_type=F32)
            up = jnp.dot(hb, wu_ref[...], preferred_element_type=F32)
            a = (gate * jax.nn.sigmoid(gate) * up).astype(BF16)
            acc_ref[r, :] += jnp.dot(a, wo_ref[...], preferred_element_type=F32)
        _row_loop(T, blk)

        @pl.when(j == nj - 2)
        def _():
            gather.relay(hx, ho, sems)

        @pl.when(j == nj - 1)
        def _():
            def fin(r):
                o_ref[r, :] = x_ref[r, :] + 0.5 * acc_ref[r, :]
            _row_loop(T, fin)
            gather.finish(hx, ho, sems)

    full = pl.BlockSpec((T, D_MODEL), lambda j: (0, 0))
    res = pl.pallas_call(
        body, name=name, grid=(nj,),
        in_specs=[full, pl.BlockSpec((1, D_MODEL), lambda j: (0, 0)),
                  pl.BlockSpec((D_MODEL, FF_CHUNK), lambda j: (0, j)),
                  pl.BlockSpec((D_MODEL, FF_CHUNK), lambda j: (0, j + nj)),
                  pl.BlockSpec((FF_CHUNK, D_MODEL), lambda j: (j, 0))] + gather.in_specs,
        out_specs=tuple([full] + gather.out_specs),
        out_shape=tuple([jax.ShapeDtypeStruct((T, D_MODEL), F32)] + gather.out_shape),
        scratch_shapes=[pltpu.VMEM((T, D_MODEL), BF16), pltpu.VMEM((T, D_MODEL), F32)] + gather.scratch,
        compiler_params=_params(("arbitrary",)))(x, g, wi, wi, wo, *hosted)
    return res[0], list(res[1:])


def _norm_bwd_rows(x, g, dh, dres):
    rstd = lax.rsqrt(jnp.mean(x * x, axis=-1, keepdims=True) + NORM_EPS)
    xh = x * rstd
    dxh = dh * g
    dx = rstd * (dxh - xh * jnp.mean(dxh * xh, axis=-1, keepdims=True))
    return dres + dx, jnp.sum(dh * xh, axis=0, keepdims=True)


def ffn_bwd(x, dy, g, wi, wo, name, hosted=()):
    T = x.shape[0]
    nj = D_FF // FF_CHUNK
    scatter = ChipScatter(list(hosted))
    n = scatter.n

    def body(*refs):
        x_ref, dy_ref, g_ref, wg_ref, wu_ref, wo_ref = refs[:6]
        hx = refs[6:6 + n]
        dx_ref, dg_ref, dwg_ref, dwu_ref, dwo_ref = refs[6 + n:11 + n]
        ho = refs[11 + n:11 + 2 * n]
        h_ref, da_ref, dh_ref = refs[11 + 2 * n:14 + 2 * n]
        sems = refs[14 + 2 * n:]
        j = pl.program_id(0)

        @pl.when(j == 0)
        def _():
            scatter.start(hx, ho, sems)

            def init(r):
                h_ref[r, :] = _rms(x_ref[r, :], g_ref[...]).astype(BF16)
                da_ref[r, :] = (0.5 * dy_ref[r, :]).astype(BF16)
                dh_ref[r, :] = jnp.zeros((r.size, D_MODEL), F32)
            _row_loop(T, init)

        dwg_ref[...] = jnp.zeros_like(dwg_ref)
        dwu_ref[...] = jnp.zeros_like(dwu_ref)
        dwo_ref[...] = jnp.zeros_like(dwo_ref)

        def blk(r):
            hb = h_ref[r, :]
            db = da_ref[r, :]
            gate = jnp.dot(hb, wg_ref[...], preferred_element_type=F32)
            up = jnp.dot(hb, wu_ref[...], preferred_element_type=F32)
            sg = jax.nn.sigmoid(gate)
            sl = gate * sg
            da = _bdot_nt(db, wo_ref[...])
            dup = (da * sl).astype(BF16)
            dgate = (da * up * (sg * (1.0 + gate * (1.0 - sg)))).astype(BF16)
            dwo_ref[...] += _bdot_tn((sl * up).astype(BF16), db)
            dwg_ref[...] += _bdot_tn(hb, dgate)
            dwu_ref[...] += _bdot_tn(hb, dup)
            dh_ref[r, :] += _bdot_nt(dgate, wg_ref[...]) + _bdot_nt(dup, wu_ref[...])
        _row_loop(T, blk)

        @pl.when(j == nj - 1)
        def _():
            dg_ref[...] = jnp.zeros_like(dg_ref)

            def fin(r):
                dx, dg = _norm_bwd_rows(x_ref[r, :], g_ref[...], dh_ref[r, :], dy_ref[r, :])
                dx_ref[r, :] = dx
                dg_ref[...] += dg
            _row_loop(T, fin)
            scatter.finish(hx, ho, sems)

    full = pl.BlockSpec((T, D_MODEL), lambda j: (0, 0))
    vec = pl.BlockSpec((1, D_MODEL), lambda j: (0, 0))
    res = pl.pallas_call(
        body, name=name, grid=(nj,),
        in_specs=[full, full, vec,
                  pl.BlockSpec((D_MODEL, FF_CHUNK), lambda j: (0, j)),
                  pl.BlockSpec((D_MODEL, FF_CHUNK), lambda j: (0, j + nj)),
                  pl.BlockSpec((FF_CHUNK, D_MODEL), lambda j: (j, 0))] + scatter.in_specs,
        out_specs=tuple([full, vec,
                         pl.BlockSpec((D_MODEL, FF_CHUNK), lambda j: (0, j)),
                         pl.BlockSpec((D_MODEL, FF_CHUNK), lambda j: (0, j)),
                         pl.BlockSpec((FF_CHUNK, D_MODEL), lambda j: (j, 0))] + scatter.out_specs),
        out_shape=tuple([jax.ShapeDtypeStruct((T, D_MODEL), F32), jax.ShapeDtypeStruct((1, D_MODEL), F32),
                         jax.ShapeDtypeStruct((D_MODEL, D_FF), F32), jax.ShapeDtypeStruct((D_MODEL, D_FF), F32),
                         jax.ShapeDtypeStruct((D_FF, D_MODEL), F32)] + scatter.out_shape),
        scratch_shapes=[pltpu.VMEM((T, D_MODEL), BF16), pltpu.VMEM((T, D_MODEL), BF16),
                        pltpu.VMEM((T, D_MODEL), F32)] + scatter.scratch,
        compiler_params=_params(("arbitrary",)))(x, dy, g, wi, wi, wo, *hosted)
    return res[0], res[1], res[2], res[3], res[4], list(res[5:])


def proj_fwd(x, g, w, name):
    T = x.shape[0]
    nj = D_IN_PAD // IN_CHUNK

    def body(x_ref, g_ref, w_ref, o_ref, h_ref):
        @pl.when(pl.program_id(0) == 0)
        def _():
            def init(r):
                h_ref[r, :] = _rms(x_ref[r, :], g_ref[...]).astype(BF16)
            _row_loop(T, init)

        def blk(r):
            o_ref[r, :] = jnp.dot(h_ref[r, :], w_ref[...], preferred_element_type=F32)
        _row_loop(T, blk)

    return pl.pallas_call(
        body, name=name, grid=(nj,),
        in_specs=[pl.BlockSpec((T, D_MODEL), lambda j: (0, 0)), pl.BlockSpec((1, D_MODEL), lambda j: (0, 0)),
                  pl.BlockSpec((D_MODEL, IN_CHUNK), lambda j: (0, j))],
        out_specs=pl.BlockSpec((T, IN_CHUNK), lambda j: (0, j)),
        out_shape=jax.ShapeDtypeStruct((T, D_IN_PAD), F32),
        scratch_shapes=[pltpu.VMEM((T, D_MODEL), BF16)],
        compiler_params=_params(("arbitrary",)))(x, g, w)


def proj_bwd(x, dres, g, w, dp, name):
    T = x.shape[0]
    nj = D_IN_PAD // IN_CHUNK

    def body(x_ref, dres_ref, g_ref, w_ref, dp_ref, dx_ref, dg_ref, dw_ref, h_ref, dh_ref):
        j = pl.program_id(0)

        @pl.when(j == 0)
        def _():
            def init(r):
                h_ref[r, :] = _rms(x_ref[r, :], g_ref[...]).astype(BF16)
                dh_ref[r, :] = jnp.zeros((r.size, D_MODEL), F32)
            _row_loop(T, init)

        dw_ref[...] = jnp.zeros_like(dw_ref)

        def blk(r):
            dpb = dp_ref[r, :].astype(BF16)
            dw_ref[...] += _bdot_tn(h_ref[r, :], dpb)
            dh_ref[r, :] += _bdot_nt(dpb, w_ref[...])
        _row_loop(T, blk)

        @pl.when(j == nj - 1)
        def _():
            dg_ref[...] = jnp.zeros_like(dg_ref)

            def fin(r):
                dx, dg = _norm_bwd_rows(x_ref[r, :], g_ref[...], dh_ref[r, :], dres_ref[r, :])
                dx_ref[r, :] = dx
                dg_ref[...] += dg
            _row_loop(T, fin)

    full = pl.BlockSpec((T, D_MODEL), lambda j: (0, 0))
    vec = pl.BlockSpec((1, D_MODEL), lambda j: (0, 0))
    return pl.pallas_call(
        body, name=name, grid=(nj,),
        in_specs=[full, full, vec, pl.BlockSpec((D_MODEL, IN_CHUNK), lambda j: (0, j)),
                  pl.BlockSpec((T, IN_CHUNK), lambda j: (0, j))],
        out_specs=(full, vec, pl.BlockSpec((D_MODEL, IN_CHUNK), lambda j: (0, j))),
        out_shape=(jax.ShapeDtypeStruct((T, D_MODEL), F32), jax.ShapeDtypeStruct((1, D_MODEL), F32),
                   jax.ShapeDtypeStruct((D_MODEL, D_IN_PAD), F32)),
        scratch_shapes=[pltpu.VMEM((T, D_MODEL), BF16), pltpu.VMEM((T, D_MODEL), F32)],
        compiler_params=_params(("arbitrary",)))(x, dres, g, w, dp)


def out_fwd(mixed, w, x, name):
    T = x.shape[0]

    def body(m_ref, w_ref, x_ref, o_ref):
        o_ref[...] = x_ref[...] + jnp.dot(m_ref[...].astype(BF16), w_ref[...], preferred_element_type=F32)

    blk = pl.BlockSpec((ROWS, D_MODEL), lambda i: (i, 0))
    return pl.pallas_call(
        body, name=name, grid=(T // ROWS,),
        in_specs=[blk, pl.BlockSpec((D_MODEL, D_MODEL), lambda i: (0, 0)), blk],
        out_specs=blk, out_shape=jax.ShapeDtypeStruct((T, D_MODEL), F32),
        compiler_params=_params(("arbitrary",)))(mixed, w, x)


def out_bwd(mixed, w, dy, name):
    T = dy.shape[0]

    def body(m_ref, w_ref, dy_ref, dm_ref, dw_ref):
        @pl.when(pl.program_id(0) == 0)
        def _():
            dw_ref[...] = jnp.zeros_like(dw_ref)
        dyb = dy_ref[...].astype(BF16)
        dm_ref[...] = _bdot_nt(dyb, w_ref[...])
        dw_ref[...] += _bdot_tn(m_ref[...].astype(BF16), dyb)

    blk = pl.BlockSpec((ROWS, D_MODEL), lambda i: (i, 0))
    sq = pl.BlockSpec((D_MODEL, D_MODEL), lambda i: (0, 0))
    return pl.pallas_call(
        body, name=name, grid=(T // ROWS,),
        in_specs=[blk, sq, blk], out_specs=(blk, sq),
        out_shape=(jax.ShapeDtypeStruct((T, D_MODEL), F32), jax.ShapeDtypeStruct((D_MODEL, D_MODEL), F32)),
        compiler_params=_params(("arbitrary",)))(mixed, w, dy)


def loss_head(x, g, target, name):
    T = x.shape[0]

    def body(x_ref, g_ref, t_ref, loss_ref, dx_ref, dg_ref):
        @pl.when(pl.program_id(0) == 0)
        def _():
            loss_ref[...] = jnp.zeros_like(loss_ref)
            dg_ref[...] = jnp.zeros_like(dg_ref)
        xb = x_ref[...]
        rstd = lax.rsqrt(jnp.mean(xb * xb, axis=-1, keepdims=True) + NORM_EPS)
        xh = xb * rstd
        err = xh * g_ref[...] - t_ref[...]
        loss_ref[...] += 0.5 * jnp.sum(jnp.mean(err * err, axis=-1, keepdims=True), axis=0, keepdims=True)
        dy = err * (1.0 / D_MODEL)
        dg_ref[...] += jnp.sum(dy * xh, axis=0, keepdims=True)
        dxh = dy * g_ref[...]
        dx_ref[...] = rstd * (dxh - xh * jnp.mean(dxh * xh, axis=-1, keepdims=True))

    blk = pl.BlockSpec((ROWS, D_MODEL), lambda i: (i, 0))
    vec = pl.BlockSpec((1, D_MODEL), lambda i: (0, 0))
    return pl.pallas_call(
        body, name=name, grid=(T // ROWS,),
        in_specs=[blk, vec, blk], out_specs=(pl.BlockSpec((1, 1), lambda i: (0, 0)), blk, vec),
        out_shape=(jax.ShapeDtypeStruct((1, 1), F32), jax.ShapeDtypeStruct((T, D_MODEL), F32),
                   jax.ShapeDtypeStruct((1, D_MODEL), F32)),
        compiler_params=_params(("arbitrary",)))(x, g, target)


def rowwise_fwd(fn, rows, shared, out_widths, name):
    T = rows[0].shape[0]
    n_in = len(rows) + len(shared)

    def body(*refs):
        res = fn(*[r[...] for r in refs[:n_in]])
        for o, v in zip(refs[n_in:], res):
            o[...] = v

    in_specs = ([pl.BlockSpec((ROWS, a.shape[1]), lambda i: (i, 0)) for a in rows]
                + [pl.BlockSpec(a.shape, lambda i: (0, 0)) for a in shared])
    return pl.pallas_call(
        body, name=name, grid=(T // ROWS,), in_specs=in_specs,
        out_specs=tuple(pl.BlockSpec((ROWS, w), lambda i: (i, 0)) for w in out_widths),
        out_shape=tuple(jax.ShapeDtypeStruct((T, w), F32) for w in out_widths),
        compiler_params=_params(("arbitrary",)))(*rows, *shared)


def rowwise_bwd(fn, rows, shared, cts, name, ct_fn=None):
    T = rows[0].shape[0]
    nr, ns, nc = len(rows), len(shared), len(cts)

    def body(*refs):
        ins = [r[...] for r in refs[:nr + ns]]
        ctv = tuple(r[...] for r in refs[nr + ns:nr + ns + nc])
        outs = refs[nr + ns + nc:]
        _, vjp = jax.vjp(fn, *ins)
        grads = vjp(ct_fn(*ctv) if ct_fn is not None else ctv)
        for k in range(nr):
            outs[k][...] = grads[k]

        @pl.when(pl.program_id(0) == 0)
        def _():
            for k in range(ns):
                outs[nr + k][...] = jnp.zeros_like(outs[nr + k])
        for k in range(ns):
            outs[nr + k][...] += grads[nr + k]

    row_spec = lambda a: pl.BlockSpec((ROWS, a.shape[1]), lambda i: (i, 0))
    sh_spec = lambda a: pl.BlockSpec(a.shape, lambda i: (0, 0))
    return pl.pallas_call(
        body, name=name, grid=(T // ROWS,),
        in_specs=[row_spec(a) for a in rows] + [sh_spec(a) for a in shared] + [row_spec(a) for a in cts],
        out_specs=tuple([row_spec(a) for a in rows] + [sh_spec(a) for a in shared]),
        out_shape=tuple(jax.ShapeDtypeStruct(a.shape, F32) for a in list(rows) + list(shared)),
        compiler_params=_params(("arbitrary",)))(*rows, *shared, *cts)


def shift_rows(x, s):
    return jnp.pad(x, ((s, 0), (0, 0)))[:x.shape[0]]


def unshift_rows(x, s):
    return jnp.pad(x, ((0, s), (0, 0)))[s:]


def _neg_expm1(y):
    series = -(y * (1.0 + y * (0.5 + y * (1.0 / 6.0 + y * (1.0 / 24.0)))))
    return jnp.where(y > -0.05, series, 1.0 - jnp.exp(y))


def lru_pre_fn(x0, x1, x2, x3, first, w0, w1, w2, w3, cb, ga, gab, gx, gxb, lam):
    xc = w3 * x0 + w2 * x1 + w1 * x2 + w0 * x3 + cb
    r = jax.nn.sigmoid(_hdot(xc, ga) + gab)
    i = jax.nn.sigmoid(_hdot(xc, gx) + gxb)
    log_a = -LRU_C * r * jax.nn.softplus(-lam)
    a = jnp.exp(log_a)
    mult = jnp.where(first > 0.5, 1.0, jnp.sqrt(_neg_expm1(2.0 * log_a)))
    return a, mult * i * xc


def lru_post_fn(h, py, og):
    return (_rms(h * jax.nn.gelu(py), og),)


def lru_scan(a, b, reverse, name):
    T, C = a.shape
    nb = T // 8

    def body(a_ref, b_ref, h_ref):
        rows = lax.broadcasted_iota(jnp.int32, (8, C), 0)

        def blk(i, carry):
            j = nb - 1 - i if reverse else i
            r = pl.ds(pl.multiple_of(j * 8, 8), 8)
            A = a_ref[r, :]
            B = b_ref[r, :]
            for s in (1, 2, 4):
                if reverse:
                    keep = rows < 8 - s
                    sh = 8 - s
                else:
                    keep = rows >= s
                    sh = s
                Bs = jnp.where(keep, pltpu.roll(B, sh, 0), 0.0)
                As = jnp.where(keep, pltpu.roll(A, sh, 0), 1.0)
                B = B + A * Bs
                A = A * As
            hb = B + A * carry
            h_ref[r, :] = hb
            edge = 0 if reverse else 7
            return jnp.sum(jnp.where(rows == edge, hb, 0.0), axis=0, keepdims=True)

        lax.fori_loop(0, nb, blk, jnp.zeros((1, C), F32))

    full = pl.BlockSpec((T, C), lambda: (0, 0))
    return pl.pallas_call(body, name=name, in_specs=[full, full], out_specs=full,
                          out_shape=jax.ShapeDtypeStruct((T, C), F32), compiler_params=_params())(a, b)


def make_rwkv_pre_fn(has_vres):
    def fn(p, pp, *rest):
        if has_vres:
            vf, mu, w_up, w_b, a_up, a_b, g_up, kk_w, ka_w, vw1, vw2, vb = rest
        else:
            mu, w_up, w_b, a_up, a_b, g_up, kk_w, ka_w = rest
        xm = p + (pp - p) * mu
        r, k, v = xm[:, 0:384], xm[:, 384:768], xm[:, 768:1152]
        xw, xa, xg = xm[:, 1152:1216], xm[:, 1216:1280], xm[:, 1280:1408]
        w_log = -jax.nn.softplus(-(w_b + _hdot(jnp.tanh(xw), w_up))) - 0.5
        lw = -jnp.exp(w_log)
        a = jax.nn.sigmoid(a_b + _hdot(xa, a_up))
        g = _hdot(jax.nn.sigmoid(xg), g_up)
        if has_vres:
            v = v + (vf - v) * jax.nn.sigmoid(vb + _hdot(_hdot(v, vw1), vw2))
        kkx = k * kk_w
        kk = kkx * lax.rsqrt(_segsum(kkx * kkx) + 1e-6)
        k2 = k * (1.0 + (a - 1.0) * ka_w)
        return r, lw, k2, v, kk, a, g
    return fn


def rwkv_post_fn(y, r, k2, v, g, ln_g, ln_b, r_k):
    mean = _segsum(y) * (1.0 / HEAD_DIM)
    yc = y - mean
    var = _segsum(yc * yc) * (1.0 / HEAD_DIM)
    yn = yc * lax.rsqrt(var + GN_EPS) * ln_g + ln_b
    bonus = _segsum(r * k2 * r_k) * v
    return ((yn + bonus) * g,)


def _head_expander(first_lane):
    ri, ci = _iota2(128, MIX_W)
    return (ri == ci // HEAD_DIM + first_lane).astype(F32)


def gdn_pre_fn(x0, x1, x2, x3, ab, w0, w1, w2, w3, alog, dtb):
    qkv = jax.nn.silu(w3 * x0 + w2 * x1 + w1 * x2 + w0 * x3)
    q, k, v = qkv[:, 0:384], qkv[:, 384:768], qkv[:, 768:1152]
    q = q * lax.rsqrt(_segsum(q * q) + 1e-6) * (HEAD_DIM ** -0.5)
    k = k * lax.rsqrt(_segsum(k * k) + 1e-6)
    g = -jnp.exp(alog) * jax.nn.softplus(ab + dtb)
    beta = jax.nn.sigmoid(ab)
    ge = _xdot(g, functools.partial(_head_expander, 0))
    be = _xdot(beta, functools.partial(_head_expander, HEADS))
    return q, k, v, ge, be


def gdn_post_fn(o, z, ng):
    ms = _segsum(o * o) * (1.0 / HEAD_DIM)
    return (o * lax.rsqrt(ms + NORM_EPS) * ng * jax.nn.silu(z),)


def _neumann_inv(m):
    n = m.shape[-1]
    ri, ci = _iota2(n, n)
    eye = (ri == ci).astype(F32)
    md = jnp.where(ri // 16 == ci // 16, m, 0.0)
    mo = m - md
    t0 = eye + md
    p2 = _hdot(md, md)
    t0 = t0 + _hdot(t0, p2)
    p4 = _hdot(p2, p2)
    t0 = t0 + _hdot(t0, p4)
    p8 = _hdot(p4, p4)
    t0 = t0 + _hdot(t0, p8)
    nn = _hdot(t0, mo)
    n2 = _hdot(nn, nn)
    t1 = eye + nn + n2 + _hdot(nn, n2)
    return _hdot(t1, t0)


@jax.custom_vjp
def _inv_saved(m, t_saved):
    return t_saved


def _inv_saved_fwd(m, t_saved):
    return t_saved, t_saved


def _inv_saved_bwd(t_saved, dt):
    tt = jnp.swapaxes(t_saved, -1, -2)
    return _hdot(_hdot(tt, dt), tt), jnp.zeros_like(t_saved)


_inv_saved.defvjp(_inv_saved_fwd, _inv_saved_bwd)


def _heads(x):
    return jnp.concatenate([x[None, :, h * HEAD_DIM:(h + 1) * HEAD_DIM] for h in range(HEADS)], axis=0)


def _unheads(y):
    return jnp.concatenate([lax.index_in_dim(y, h, 0, keepdims=False) for h in range(HEADS)], axis=1)


def rwkv_heads(s0, r, lw, k2, v, kk, a, inv):
    n = r.shape[0]
    ri, ci = _iota2(n, n)
    low, strict = ri >= ci, ri > ci
    cs = _cumsum_rows(lw)
    cl = jnp.sum(lw, axis=0, keepdims=True)
    p_in, p_prev, p_inv = jnp.exp(cs), jnp.exp(cs - lw), jnp.exp(-cs)
    p_rest, p_all = jnp.exp(cl - cs), jnp.exp(cl)
    bd = kk * a
    at, rt = _heads(-kk * p_prev), _heads(r * p_in)
    bh, kh = _heads(bd * p_inv), _heads(k2 * p_inv)
    vh = _heads(v)
    m_ab = jnp.where(strict, _cdot_nt(at, bh), 0.0)
    m_ak = jnp.where(strict, _cdot_nt(at, kh), 0.0)
    m_rb = jnp.where(low, _cdot_nt(rt, bh), 0.0)
    m_rk = jnp.where(low, _cdot_nt(rt, kh), 0.0)
    sa = _cdot(inv(m_ab), _cdot_nt(at, s0) + _cdot(m_ak, vh))
    y = _cdot_nt(rt, s0) + _cdot(m_rb, sa) + _cdot(m_rk, vh)
    s1 = s0 * _heads(p_all) + _cdot_tn(sa, _heads(bd * p_rest)) + _cdot_tn(vh, _heads(k2 * p_rest))
    return _unheads(y), s1


def gdn_heads(s0, q, k, v, ge, be, inv):
    n = q.shape[0]
    ri, ci = _iota2(n, n)
    low, strict = ri >= ci, ri > ci
    gc = _cumsum_rows(ge)
    gl = jnp.sum(ge, axis=0, keepdims=True)
    gch = _heads(gc)
    decay = jnp.where(low, jnp.exp(jnp.where(low, gch - jnp.swapaxes(gch, 1, 2), 0.0)), 0.0)
    kb = k * be
    e = jnp.exp(gc)
    kh = _heads(k)
    m = -jnp.where(strict, _cdot_nt(_heads(kb), kh) * decay, 0.0)
    mr = jnp.where(low, _cdot_nt(_heads(q), kh) * decay, 0.0)
    u = _cdot(inv(m), _heads(v * be) - _cdot_nt(_heads(kb * e), s0))
    y = _cdot_nt(_heads(q * e), s0) + _cdot(mr, u)
    s1 = s0 * _heads(jnp.exp(gl)) + _cdot_tn(u, _heads(k * jnp.exp(gl - gc)))
    return _unheads(y), s1


def core_fwd(heads_fn, ins, name, hosted=()):
    T = ins[0].shape[0]
    nc = T // CHUNK
    n = len(ins)
    gather = ChipGather(list(hosted))
    ng = gather.n

    def body(*refs):
        hx = refs[n:n + ng]
        y_ref, s0_ref, t_ref = refs[n + ng:n + ng + 3]
        ho = refs[n + ng + 3:n + 2 * ng + 3]
        s_ref = refs[n + 2 * ng + 3]
        sems = refs[n + 2 * ng + 4:]
        c = pl.program_id(0)

        @pl.when(c == 0)
        def _():
            gather.start(hx, ho, sems)
            s_ref[...] = jnp.zeros_like(s_ref)

        s0 = s_ref[...]
        kept = []

        def inv(m):
            kept.append(_neumann_inv(m))
            return kept[0]

        y, s1 = heads_fn(s0, *[r[...] for r in refs[:n]], inv)
        y_ref[...] = y
        s0_ref[0] = s0
        t_ref[0] = kept[0]
        s_ref[...] = s1

        @pl.when(c == nc - 4)
        def _():
            gather.relay(hx, ho, sems)

        @pl.when(c == nc - 1)
        def _():
            gather.finish(hx, ho, sems)

    row = pl.BlockSpec((CHUNK, MIX_W), lambda c: (c, 0))
    st_shape = (HEADS, HEAD_DIM, HEAD_DIM)
    st = pl.BlockSpec((1,) + st_shape, lambda c: (c, 0, 0, 0))
    res = pl.pallas_call(
        body, name=name, grid=(nc,), in_specs=[row] * n + gather.in_specs,
        out_specs=tuple([row, st, st] + gather.out_specs),
        out_shape=tuple([jax.ShapeDtypeStruct((T, MIX_W), F32), jax.ShapeDtypeStruct((nc,) + st_shape, F32),
                         jax.ShapeDtypeStruct((nc,) + st_shape, F32)] + gather.out_shape),
        scratch_shapes=[pltpu.VMEM(st_shape, F32)] + gather.scratch,
        compiler_params=_params(("arbitrary",)))(*ins, *hosted)
    return res[0], res[1], res[2], list(res[3:])


def core_bwd(heads_fn, ins, s0_all, t_all, dy, name):
    T = ins[0].shape[0]
    nc = T // CHUNK
    n = len(ins)

    def body(*refs):
        s0_ref, t_ref, dy_ref = refs[n:n + 3]
        outs = refs[n + 3:n + 3 + n]
        ds_ref = refs[n + 3 + n]

        @pl.when(pl.program_id(0) == 0)
        def _():
            ds_ref[...] = jnp.zeros_like(ds_ref)

        t_saved = t_ref[0]
        f = lambda s0, *xs: heads_fn(s0, *xs, lambda m: _inv_saved(m, t_saved))
        _, vjp = jax.vjp(f, s0_ref[0], *[r[...] for r in refs[:n]])
        grads = vjp((dy_ref[...], ds_ref[...]))
        ds_ref[...] = grads[0]
        for k in range(n):
            outs[k][...] = grads[1 + k]

    row = pl.BlockSpec((CHUNK, MIX_W), lambda c: (nc - 1 - c, 0))
    st_shape = (HEADS, HEAD_DIM, HEAD_DIM)
    st = pl.BlockSpec((1,) + st_shape, lambda c: (nc - 1 - c, 0, 0, 0))
    return pl.pallas_call(
        body, name=name, grid=(nc,), in_specs=[row] * n + [st, st, row], out_specs=tuple([row] * n),
        out_shape=tuple(jax.ShapeDtypeStruct((T, MIX_W), F32) for _ in range(n)),
        scratch_shapes=[pltpu.VMEM(st_shape, F32)],
        compiler_params=_params(("arbitrary",)))(*ins, s0_all, t_all, dy)


def _block_diag(w):
    out = jnp.zeros((LRU_W, LRU_W), w.dtype)
    for n in range(LRU_BLOCKS):
        out = lax.dynamic_update_slice(out, w[n], (n * 64, n * 64))
    return out


def _block_diag_grad(g):
    return jnp.stack([g[n * 64:(n + 1) * 64, n * 64:(n + 1) * 64] for n in range(LRU_BLOCKS)])


def _row(v):
    return v.reshape(1, -1)


def _pad128(v):
    return jnp.pad(v.reshape(1, -1), ((0, 0), (0, 128 - v.size)))


def _layer_shared(w, l):
    cw = w['lru_conv_w'][l]
    lru_pre = [_row(cw[0]), _row(cw[1]), _row(cw[2]), _row(cw[3]), _row(w['lru_conv_b'][l]),
               _block_diag(w['lru_gate_a_w'][l]), _row(w['lru_gate_a_b'][l]),
               _block_diag(w['lru_gate_x_w'][l]), _row(w['lru_gate_x_b'][l]), _row(w['lru_lambda'][l])]
    rw_pre = [_row(w['rwkv_mu'][l]), w['rwkv_w_up'][l], _row(w['rwkv_w_bias'][l]), w['rwkv_a_up'][l],
              _row(w['rwkv_a_bias'][l]), w['rwkv_g_up'][l], _row(w['rwkv_k_k'][l]), _row(w['rwkv_k_a'][l])]
    if l > 0:
        rw_pre += [w['rwkv_vres_w1'][l - 1], w['rwkv_vres_w2'][l - 1], _row(w['rwkv_vres_b'][l - 1])]
    rw_post = [_row(w['rwkv_ln_g'][l]), _row(w['rwkv_ln_b'][l]), _row(w['rwkv_r_k'][l])]
    gw = w['gdn_conv_w'][l]
    gdn_pre = [_row(gw[0]), _row(gw[1]), _row(gw[2]), _row(gw[3]), _pad128(w['gdn_a_log'][l]),
               _pad128(w['gdn_dt_bias'][l])]
    gdn_post = [_row(jnp.tile(w['gdn_norm'][l], HEADS))]
    return dict(lru_pre=lru_pre, lru_post=[_row(w['lru_out_norm'][l])], rw_pre=rw_pre, rw_post=rw_post,
                gdn_pre=gdn_pre, gdn_post=gdn_post)


def _mixer_fwd(p, sh, l, v_first, host_rwkv=(), host_gdn=()):
    T = p.shape[0]
    lx, ly = p[:, 0:256], p[:, 256:512]
    prw, qkv, z, ab = p[:, 512:1920], p[:, 1920:3072], p[:, 3072:3456], p[:, 3456:3584]
    first = (lax.broadcasted_iota(jnp.int32, (T, LRU_W), 0) == 0).astype(F32)
    lru_rows = [lx, shift_rows(lx, 1), shift_rows(lx, 2), shift_rows(lx, 3), first]
    a, b = rowwise_fwd(lru_pre_fn, lru_rows, sh['lru_pre'], (LRU_W, LRU_W), f"lru_pre_fwd{l}")
    hseq = lru_scan(a, b, False, f"lru_scan_fwd{l}")
    (y_lru,) = rowwise_fwd(lru_post_fn, [hseq, ly], sh['lru_post'], (LRU_W,), f"lru_post_fwd{l}")

    rw_rows = [prw, shift_rows(prw, 1)] + ([v_first] if l > 0 else [])
    rw = rowwise_fwd(make_rwkv_pre_fn(l > 0), rw_rows, sh['rw_pre'], (MIX_W,) * 7, f"rwkv_pre_fwd{l}")
    r, lw, k2, v, kk, ar, g = rw
    y_raw, rs0, rt, got_rwkv = core_fwd(rwkv_heads, [r, lw, k2, v, kk, ar], f"rwkv_core_fwd{l}", host_rwkv)
    (y_rw,) = rowwise_fwd(rwkv_post_fn, [y_raw, r, k2, v, g], sh['rw_post'], (MIX_W,), f"rwkv_post_fwd{l}")

    gdn_rows = [qkv, shift_rows(qkv, 1), shift_rows(qkv, 2), shift_rows(qkv, 3), ab]
    gd = rowwise_fwd(gdn_pre_fn, gdn_rows, sh['gdn_pre'], (MIX_W,) * 5, f"gdn_pre_fwd{l}")
    o_raw, gs0, gt, got_gdn = core_fwd(gdn_heads, list(gd), f"gdn_core_fwd{l}", host_gdn)
    (y_gdn,) = rowwise_fwd(gdn_post_fn, [o_raw, z], sh['gdn_post'], (MIX_W,), f"gdn_post_fwd{l}")

    mixed = jnp.concatenate([y_lru, y_rw, y_gdn], axis=1)
    saved = dict(lru_rows=lru_rows, a=a, hseq=hseq, ly=ly, rw_rows=rw_rows, rw=rw, y_raw=y_raw, rs0=rs0, rt=rt,
                 gdn_rows=gdn_rows, gd=gd, o_raw=o_raw, gs0=gs0, gt=gt, z=z)
    v_layer0 = v if l == 0 else None
    return mixed, saved, v_layer0, got_rwkv, got_gdn


def _mixer_bwd(dmixed, sv, sh, l, dv_first):
    d_lru, d_rw, d_gdn = dmixed[:, 0:256], dmixed[:, 256:640], dmixed[:, 640:1024]
    gw = {}

    dh, dly, d_og = rowwise_bwd(lru_post_fn, [sv['hseq'], sv['ly']], sh['lru_post'], [d_lru], f"lru_post_bwd{l}")
    gscan = lru_scan(unshift_rows(sv['a'], 1), dh, True, f"lru_scan_bwd{l}")
    res = rowwise_bwd(lru_pre_fn, sv['lru_rows'], sh['lru_pre'], [gscan, shift_rows(sv['hseq'], 1)],
                      f"lru_pre_bwd{l}", ct_fn=lambda gs, hp: (gs * hp, gs))
    dlx = res[0] + unshift_rows(res[1], 1) + unshift_rows(res[2], 2) + unshift_rows(res[3], 3)
    dw0, dw1, dw2, dw3, dcb, dga, dgab, dgx, dgxb, dlam = res[5:]
    gw['lru_conv_w'] = jnp.concatenate([dw0, dw1, dw2, dw3], axis=0)
    gw['lru_conv_b'] = dcb[0]
    gw['lru_gate_a_w'] = _block_diag_grad(dga)
    gw['lru_gate_a_b'] = dgab.reshape(LRU_BLOCKS, 64)
    gw['lru_gate_x_w'] = _block_diag_grad(dgx)
    gw['lru_gate_x_b'] = dgxb.reshape(LRU_BLOCKS, 64)
    gw['lru_lambda'] = dlam[0]
    gw['lru_out_norm'] = d_og[0]

    r, lw, k2, v, kk, ar, g = sv['rw']
    res = rowwise_bwd(rwkv_post_fn, [sv['y_raw'], r, k2, v, g], sh['rw_post'], [d_rw], f"rwkv_post_bwd{l}")
    dy_raw, dr_p, dk2_p, dv_p, dg = res[:5]
    gw['rwkv_ln_g'], gw['rwkv_ln_b'], gw['rwkv_r_k'] = res[5][0], res[6][0], res[7].reshape(HEADS, HEAD_DIM)
    dr_c, dlw, dk2_c, dv_c, dkk, dar = core_bwd(rwkv_heads, [r, lw, k2, v, kk, ar], sv['rs0'], sv['rt'], dy_raw,
                                                 f"rwkv_core_bwd{l}")
    cts = [dr_p, dr_c, dlw, dk2_p, dk2_c, dv_p, dv_c, dkk, dar, dg]
    if l == 0:
        cts.append(dv_first)
        ct_fn = lambda a1, a2, b, c1, c2, d1, d2, e, f, gg, vf: (a1 + a2, b, c1 + c2, d1 + d2 + vf, e, f, gg)
    else:
        ct_fn = lambda a1, a2, b, c1, c2, d1, d2, e, f, gg: (a1 + a2, b, c1 + c2, d1 + d2, e, f, gg)
    res = rowwise_bwd(make_rwkv_pre_fn(l > 0), sv['rw_rows'], sh['rw_pre'], cts, f"rwkv_pre_bwd{l}", ct_fn=ct_fn)
    dprw = res[0] + unshift_rows(res[1], 1)
    nrow = len(sv['rw_rows'])
    dv_first_out = res[2] if l > 0 else None
    sg = res[nrow:]
    gw['rwkv_mu'], gw['rwkv_w_up'], gw['rwkv_w_bias'], gw['rwkv_a_up'] = sg[0][0], sg[1], sg[2][0], sg[3]
    gw['rwkv_a_bias'], gw['rwkv_g_up'], gw['rwkv_k_k'], gw['rwkv_k_a'] = sg[4][0], sg[5], sg[6][0], sg[7][0]
    if l > 0:
        gw['rwkv_vres_w1'], gw['rwkv_vres_w2'], gw['rwkv_vres_b'] = sg[8], sg[9], sg[10][0]

    do_raw, dz, d_ng = rowwise_bwd(gdn_post_fn, [sv['o_raw'], sv['z']], sh['gdn_post'], [d_gdn], f"gdn_post_bwd{l}")
    gw['gdn_norm'] = jnp.sum(d_ng.reshape(HEADS, HEAD_DIM), axis=0)
    dgd = core_bwd(gdn_heads, list(sv['gd']), sv['gs0'], sv['gt'], do_raw, f"gdn_core_bwd{l}")
    res = rowwise_bwd(gdn_pre_fn, sv['gdn_rows'], sh['gdn_pre'], list(dgd), f"gdn_pre_bwd{l}")
    dqkv = res[0] + unshift_rows(res[1], 1) + unshift_rows(res[2], 2) + unshift_rows(res[3], 3)
    dab = res[4]
    gw['gdn_conv_w'] = jnp.concatenate(res[5:9], axis=0)
    gw['gdn_a_log'], gw['gdn_dt_bias'] = res[9][0, :HEADS], res[10][0, :HEADS]

    dp = jnp.concatenate([dlx, dly, dprw, dqkv, dz, dab], axis=1)
    return dp, gw, dv_first_out


IN_SHARD = D_IN // N_CHIPS
IN_SHARD_PAD = D_IN_PAD // N_CHIPS


def _cols_to_chips(g, n=N_CHIPS):
    r = g.shape[0]
    return jnp.transpose(g.reshape(r, n, -1), (1, 0, 2))


def _cols_from_chips(g):
    return jnp.transpose(g, (1, 0, 2)).reshape(g.shape[1], -1)


def _w_in_from_chips(g):
    nat = _cols_from_chips(g[:, :, :IN_SHARD])
    return jnp.pad(nat, ((0, 0), (0, D_IN_PAD - D_IN)))


def _w_in_to_chips(g):
    return jnp.pad(_cols_to_chips(g[:, :D_IN]), ((0, 0), (0, 0), (0, IN_SHARD_PAD - IN_SHARD)))


def _natural(name, g):
    if name == 'w_in':
        return _w_in_from_chips(g)
    if BIG[name] == 2:
        return _cols_from_chips(g)
    return g.reshape(-1, g.shape[2])


def local_step(x, target, w, wb, shards=None):
    def hosted(keys):
        return [shards[k] for k in keys] if shards is not None else []

    def arrived(keys, gathered):
        for (name, layer), g in zip(keys if shards is not None else [], gathered):
            wb[name][layer] = _natural(name, g)

    saved = []
    v_first = None
    for l in range(N_LAYERS):
        sh = _layer_shared(w, l)
        for_mixer = [('w_in', l), ('w_out', l)]
        for_ffn2 = [('ffn2_wi', l), ('ffn2_wo', l)]
        for_next = [('ffn1_wi', l + 1), ('ffn1_wo', l + 1)] if l + 1 < N_LAYERS else []
        x1, got = ffn_fwd(x, _row(w['ffn1_norm'][l]), wb['ffn1_wi'][l], wb['ffn1_wo'][l], f"ffn1_fwd{l}",
                          hosted(for_mixer))
        arrived(for_mixer, got)
        p = proj_fwd(x1, _row(w['mix_norm'][l]), wb['w_in'][l], f"proj_fwd{l}")
        mixed, sv, v0, got_ffn2, got_next = _mixer_fwd(p, sh, l, v_first, hosted(for_ffn2), hosted(for_next))
        arrived(for_ffn2, got_ffn2)
        arrived(for_next, got_next)
        if l == 0:
            v_first = v0
        x2 = out_fwd(mixed, wb['w_out'][l], x1, f"out_fwd{l}")
        x3, _ = ffn_fwd(x2, _row(w['ffn2_norm'][l]), wb['ffn2_wi'][l], wb['ffn2_wo'][l], f"ffn2_fwd{l}")
        saved.append(dict(x0=x, x1=x1, x2=x2, mixed=mixed, sv=sv, sh=sh))
        x = x3

    loss, dx, dgf = loss_head(x, _row(w['final_norm']), target, "loss_head")
    per_layer = [dict() for _ in range(N_LAYERS)]
    dv_first = jnp.zeros((x.shape[0], MIX_W), F32)

    waiting, chip_sums, arrived_parts = [], {}, {}

    def reduce_now(keys, tag):
        if shards is None:
            return
        sums = chip_sums_of([(n, k, per_layer[k][n]) for n, k in keys], lax.axis_index("c"), tag)
        for key, (total, total_bf) in zip(keys, sums):
            chip_sums[key] = total
            waiting.append((key, total_bf))

    def take_waiting():
        keys, parts = [k for k, _ in waiting], [p for _, p in waiting]
        waiting.clear()
        return keys, parts

    for l in reversed(range(N_LAYERS)):
        s = saved[l]
        gw = per_layer[l]
        keys, parts = take_waiting()
        dx, dg2, dwg, dwu, dwo, got = ffn_bwd(s['x2'], dx, _row(w['ffn2_norm'][l]), wb['ffn2_wi'][l],
                                              wb['ffn2_wo'][l], f"ffn2_bwd{l}", parts)
        arrived_parts.update(zip(keys, got))
        wi_parts = lambda dwg, dwu: (dwg, dwu)
        row_parts = lambda dw: dw.reshape(N_CHIPS, -1, dw.shape[1])
        gw['ffn2_norm'], gw['ffn2_wi'], gw['ffn2_wo'] = dg2[0], wi_parts(dwg, dwu), row_parts(dwo)
        if l == N_LAYERS - 1:
            reduce_now([('ffn2_wi', l), ('ffn2_wo', l)], f"ffn2_{l}")
        dmixed, dw_out = out_bwd(s['mixed'], wb['w_out'][l], dx, f"out_bwd{l}")
        gw['w_out'] = row_parts(dw_out)
        dp, gmix, dvf = _mixer_bwd(dmixed, s['sv'], s['sh'], l, dv_first)
        if l > 0:
            dv_first = dvf
        gw.update(gmix)
        dx, dgm, dwin = proj_bwd(s['x1'], dx, _row(w['mix_norm'][l]), wb['w_in'][l], dp, f"proj_bwd{l}")
        gw['mix_norm'], gw['w_in'] = dgm[0], _w_in_to_chips(dwin)
        if l < N_LAYERS - 1:
            reduce_now([('ffn2_wi', l), ('ffn2_wo', l), ('w_in', l), ('w_out', l)], f"mix_{l}")
        keys, parts = take_waiting()
        dx, dg1, dwg, dwu, dwo, got = ffn_bwd(s['x0'], dx, _row(w['ffn1_norm'][l]), wb['ffn1_wi'][l],
                                              wb['ffn1_wo'][l], f"ffn1_bwd{l}", parts)
        arrived_parts.update(zip(keys, got))
        gw['ffn1_norm'], gw['ffn1_wi'], gw['ffn1_wo'] = dg1[0], wi_parts(dwg, dwu), row_parts(dwo)
        if l == N_LAYERS - 1:
            reduce_now([('w_in', l), ('w_out', l), ('ffn1_wi', l), ('ffn1_wo', l)], f"ffn1_{l}")
        else:
            reduce_now([('ffn1_wi', l), ('ffn1_wo', l)], f"ffn1_{l}")
    grads = {'final_norm': dgf[0]}
    if shards is not None:
        grads['last_round'] = take_waiting()
        arrived_parts.update({key: None for key in grads['last_round'][0]})
    for name in WEIGHTS:
        if name == 'final_norm':
            continue
        if name in BIG:
            if shards is None:
                grads[name] = [per_layer[l][name] for l in range(N_LAYERS)]
            else:
                grads[name] = [(chip_sums[(name, l)], arrived_parts[(name, l)]) for l in range(N_LAYERS)]
        elif name.startswith('rwkv_vres'):
            grads[name] = per_layer[1][name][None]
        else:
            grads[name] = jnp.stack([per_layer[l][name] for l in range(N_LAYERS)])
    return loss[0, 0], dx, grads


ANY = pl.BlockSpec(memory_space=pl.ANY)


def _coords():
    return lax.axis_index("x"), lax.axis_index("y"), lax.axis_index("c")


def _other_chips(x, y):
    return [((x + 1) % 2, y), (x, (y + 1) % 2), ((x + 1) % 2, (y + 1) % 2)]


def allreduce_small(pack, name):
    R = pack.shape[0]
    rh = R // 2

    def body(x_ref, o_ref, sib_ref, chip_ref, parts_ref, send_sems, recv_sems):
        x, y, c = _coords()
        sib = (x, y, 1 - c)

        def copy(k, src, dst, to):
            return pltpu.make_async_remote_copy(src_ref=src, dst_ref=dst, send_sem=send_sems.at[k],
                                                recv_sem=recv_sems.at[k], device_id=to, device_id_type=MESH)

        swap = copy(0, x_ref, sib_ref, sib)
        swap.start()
        swap.wait()
        chip_ref[...] = jnp.where(c == 0, x_ref[...], sib_ref[...]) + jnp.where(c == 0, sib_ref[...], x_ref[...])

        mine = pl.ds(pl.multiple_of(c * rh, 8), rh)
        sends = [copy(1 + j, chip_ref.at[mine], parts_ref.at[j], (px, py, c))
                 for j, (px, py) in enumerate(_other_chips(x, y))]
        for cp in sends:
            cp.start()
        for cp in sends:
            cp.wait()
        s = 2 * x + y
        own = chip_ref[mine, :]
        from_chip = {2: parts_ref[0], 1: parts_ref[1], 3: parts_ref[2]}
        terms = []
        for k in range(N_CHIPS):
            t = own
            for d, part in from_chip.items():
                t = jnp.where(jnp.bitwise_xor(s, d) == k, part, t)
            terms.append(t)
        o_ref[mine, :] = ((terms[0] + terms[1]) + terms[2]) + terms[3]

        share = copy(4, o_ref.at[mine], o_ref.at[mine], sib)
        share.start()
        share.wait()

    vm = pl.BlockSpec(memory_space=pltpu.VMEM)
    return pl.pallas_call(
        body, name=name, in_specs=[vm], out_specs=vm, out_shape=jax.ShapeDtypeStruct((R, 128), F32),
        scratch_shapes=[pltpu.VMEM((R, 128), F32), pltpu.VMEM((R, 128), F32), pltpu.VMEM((3, rh, 128), F32),
                        pltpu.SemaphoreType.DMA((5,)), pltpu.SemaphoreType.DMA((5,))],
        compiler_params=_params())(pack)


class ChipGather:
    def __init__(self, shards):
        self.shapes = [s.shape for s in shards]
        self.n = len(shards)
        self.in_specs = [ANY] * self.n
        self.out_specs = [ANY] * self.n
        self.out_shape = [jax.ShapeDtypeStruct((N_CHIPS,) + s.shape, s.dtype) for s in shards]
        self.scratch = [pltpu.SemaphoreType.DMA((6 * self.n,)), pltpu.SemaphoreType.DMA((6 * self.n,)),
                        pltpu.SemaphoreType.DMA((self.n,))] if self.n else []

    def _rows(self, a, core):
        rh = self.shapes[a][0] // 2
        return pl.ds(pl.multiple_of(core * rh, 16), rh)

    def _copies(self, kind, x_refs, o_refs, sems):
        send_sems, recv_sems, local_sems = sems
        x, y, c = _coords()
        s_me = 2 * x + y
        sib = (x, y, 1 - c)

        def copy(a, k, src, dst, to):
            return pltpu.make_async_remote_copy(src_ref=src, dst_ref=dst, send_sem=send_sems.at[6 * a + k],
                                                recv_sem=recv_sems.at[6 * a + k], device_id=to, device_id_type=MESH)

        if kind == 'own':
            return [pltpu.make_async_copy(x_refs[a], o_refs[a].at[s_me], local_sems.at[a]) for a in range(self.n)]
        out = []
        for j, (px, py) in enumerate(_other_chips(x, y)):
            for a in range(self.n):
                mine = self._rows(a, c)
                part = o_refs[a].at[2 * px + py, mine]
                if kind == 'sent':
                    out.append(copy(a, j, x_refs[a].at[mine], o_refs[a].at[s_me, mine], (px, py, c)))
                elif kind == 'arrived':
                    out.append(copy(a, j, part, part, (px, py, c)))
                elif kind == 'passed':
                    out.append(copy(a, 3 + j, part, part, sib))
                else:
                    theirs = o_refs[a].at[2 * px + py, self._rows(a, 1 - c)]
                    out.append(copy(a, 3 + j, theirs, theirs, sib))
        return out

    def start(self, x_refs, o_refs, sems):
        if not self.n:
            return
        for cp in self._copies('own', x_refs, o_refs, sems) + self._copies('sent', x_refs, o_refs, sems):
            cp.start()

    def relay(self, x_refs, o_refs, sems):
        if not self.n:
            return
        for got, fw in zip(self._copies('arrived', x_refs, o_refs, sems),
                           self._copies('passed', x_refs, o_refs, sems)):
            got.wait_recv()
            fw.start()

    def finish(self, x_refs, o_refs, sems):
        if not self.n:
            return
        for cp in self._copies('from_sibling', x_refs, o_refs, sems):
            cp.wait_recv()
        for cp in self._copies('sent', x_refs, o_refs, sems) + self._copies('passed', x_refs, o_refs, sems):
            cp.wait_send()
        for cp in self._copies('own', x_refs, o_refs, sems):
            cp.wait()


def allgather_chips(shards, name):
    gather = ChipGather(shards)
    n = gather.n

    def body(*refs):
        x_refs, o_refs, sems = refs[:n], refs[n:2 * n], refs[2 * n:]
        gather.start(x_refs, o_refs, sems)
        gather.relay(x_refs, o_refs, sems)
        gather.finish(x_refs, o_refs, sems)

    return pl.pallas_call(
        body, name=name, in_specs=gather.in_specs, out_specs=tuple(gather.out_specs),
        out_shape=tuple(gather.out_shape), scratch_shapes=gather.scratch, compiler_params=_params())(*shards)


def sibling_swap(srcs, halves, name):
    n = len(srcs)
    row_axis = [s.ndim - 2 for s in srcs]
    out_shapes = [s.shape[:ax] + (s.shape[ax] // 2,) + s.shape[ax + 1:] if halves else s.shape
                  for s, ax in zip(srcs, row_axis)]

    def body(*refs):
        x_refs, o_refs = refs[:n], refs[n:2 * n]
        send_sems, recv_sems = refs[2 * n:]
        x, y, c = _coords()
        copies = []
        for a in range(n):
            part = x_refs[a]
            if halves:
                rh = srcs[a].shape[row_axis[a]] // 2
                theirs = pl.ds(pl.multiple_of((1 - c) * rh, 16), rh)
                part = part.at[:, theirs] if row_axis[a] == 1 else part.at[theirs]
            cp = pltpu.make_async_remote_copy(src_ref=part, dst_ref=o_refs[a], send_sem=send_sems.at[a],
                                              recv_sem=recv_sems.at[a], device_id=(x, y, 1 - c), device_id_type=MESH)
            cp.start()
            copies.append(cp)
        for cp in copies:
            cp.wait()

    return pl.pallas_call(
        body, name=name, in_specs=[ANY] * n, out_specs=tuple([ANY] * n),
        out_shape=tuple(jax.ShapeDtypeStruct(sh, s.dtype) for sh, s in zip(out_shapes, srcs)),
        scratch_shapes=[pltpu.SemaphoreType.DMA((n,)), pltpu.SemaphoreType.DMA((n,))],
        compiler_params=_params())(*srcs)


class ChipScatter:
    def __init__(self, parts):
        self.n = len(parts)
        self.in_specs = [ANY] * self.n
        self.out_specs = [ANY] * self.n
        self.out_shape = [jax.ShapeDtypeStruct((3,) + p.shape[1:], p.dtype) for p in parts]
        self.scratch = [pltpu.SemaphoreType.DMA((3 * self.n,)), pltpu.SemaphoreType.DMA((3 * self.n,))] if self.n else []

    def _copies(self, x_refs, o_refs, sems):
        send_sems, recv_sems = sems
        x, y, c = _coords()
        return [pltpu.make_async_remote_copy(src_ref=x_refs[a].at[2 * px + py], dst_ref=o_refs[a].at[j],
                                             send_sem=send_sems.at[3 * a + j], recv_sem=recv_sems.at[3 * a + j],
                                             device_id=(px, py, c), device_id_type=MESH)
                for j, (px, py) in enumerate(_other_chips(x, y)) for a in range(self.n)]

    def start(self, x_refs, o_refs, sems):
        if self.n:
            for cp in self._copies(x_refs, o_refs, sems):
                cp.start()

    def finish(self, x_refs, o_refs, sems):
        if self.n:
            for cp in self._copies(x_refs, o_refs, sems):
                cp.wait()


def scatter_chips(parts, name):
    scatter = ChipScatter(parts)
    n = scatter.n

    def body(*refs):
        x_refs, o_refs, sems = refs[:n], refs[n:2 * n], refs[2 * n:]
        scatter.start(x_refs, o_refs, sems)
        scatter.finish(x_refs, o_refs, sems)

    return pl.pallas_call(
        body, name=name, in_specs=scatter.in_specs, out_specs=tuple(scatter.out_specs),
        out_shape=tuple(scatter.out_shape), scratch_shapes=scatter.scratch, compiler_params=_params())(*parts)


HBM = pl.BlockSpec(memory_space=pltpu.HBM)
SEM = pl.BlockSpec(memory_space=pltpu.SEMAPHORE)
SIDE_EFFECT = pltpu.SideEffectType.DATAFLOW_SIDE_EFFECTING


def _scatter_copies(x_refs, land_refs, send_sems, recv_sems):
    x, y, c = _coords()
    n = len(x_refs)
    return [pltpu.make_async_remote_copy(src_ref=x_refs[a].at[2 * px + py], dst_ref=land_refs[a].at[j],
                                         send_sem=send_sems[3 * a + j], recv_sem=recv_sems[3 * a + j],
                                         device_id=(px, py, c), device_id_type=MESH)
            for j, (px, py) in enumerate(_other_chips(x, y)) for a in range(n)]


def scatter_start(parts, name):
    n = len(parts)
    k = 3 * n
    lands = [lax.empty((3,) + p.shape[1:], p.dtype) for p in parts]

    def body(*refs):
        x_refs, land_refs = refs[:n], refs[n:2 * n]
        send_sems, recv_sems = refs[2 * n:2 * n + k], refs[2 * n + k:2 * n + 2 * k]
        token = refs[-1]
        for cp in _scatter_copies(x_refs, land_refs, send_sems, recv_sems):
            cp.start()
        token[...] = jnp.zeros_like(token)

    hbm = lambda a: pltpu.HBM(a.shape, a.dtype)
    res = pl.pallas_call(
        body, name=name, in_specs=[HBM] * (2 * n),
        out_specs=tuple([SEM] * (2 * k) + [HBM] * (2 * n) + [pl.BlockSpec(memory_space=pltpu.VMEM)]),
        out_shape=tuple([pltpu.SemaphoreType.DMA(())] * (2 * k) + [hbm(p) for p in parts] + [hbm(b) for b in lands]
                        + [jax.ShapeDtypeStruct((8, 128), F32)]),
        input_output_aliases={i: 2 * k + i for i in range(2 * n)},
        compiler_params=pltpu.CompilerParams(has_side_effects=SIDE_EFFECT, vmem_limit_bytes=VMEM_LIMIT))(
            *[pltpu.with_memory_space_constraint(a, pltpu.HBM) for a in list(parts) + lands])
    return list(res[:2 * k]), list(res[2 * k:2 * k + n]), list(res[2 * k + n:2 * k + 2 * n]), res[-1]


def scatter_wait(sems, parts_thru, lands_thru, after, name):
    n = len(parts_thru)
    k = 3 * n

    def body(*refs):
        x_refs, land_refs = refs[:n], refs[n:2 * n]
        send_sems, recv_sems = refs[2 * n:2 * n + k], refs[2 * n + k:2 * n + 2 * k]
        for cp in _scatter_copies(x_refs, land_refs, send_sems, recv_sems):
            cp.wait_send()
            cp.wait_recv()

    hbm = lambda a: pltpu.HBM(a.shape, a.dtype)
    res = pl.pallas_call(
        body, name=name, in_specs=[HBM] * (2 * n) + [SEM] * (2 * k) + [ANY],
        out_specs=tuple([HBM] * (2 * n)), out_shape=tuple(hbm(a) for a in list(parts_thru) + list(lands_thru)),
        input_output_aliases={i: i for i in range(2 * n)},
        compiler_params=pltpu.CompilerParams(has_side_effects=SIDE_EFFECT, vmem_limit_bytes=VMEM_LIMIT))(
            *parts_thru, *lands_thru, *sems, after)
    return list(res[n:])


def _row_block(rows):
    return max(b for b in range(16, 257, 16) if rows % b == 0)


def chip_sum(gpack, recv, core, name):
    n, R, W = gpack.shape
    rh = R // 2
    rb = _row_block(rh)
    nb = rh // rb

    def body(c_ref, g_ref, r_ref, o_ref, ob_ref):
        s = g_ref[...] + r_ref[...]
        o_ref[...] = s
        ob_ref[...] = s.astype(BF16)

    blk = pl.BlockSpec((1, rb, W), lambda i, j, c_ref: (i, j, 0))
    spec = pltpu.PrefetchScalarGridSpec(
        num_scalar_prefetch=1, grid=(n, nb),
        in_specs=[pl.BlockSpec((1, rb, W), lambda i, j, c_ref: (i, c_ref[0] * nb + j, 0)), blk],
        out_specs=(blk, blk))
    return pl.pallas_call(
        body, name=name, grid_spec=spec,
        out_shape=(jax.ShapeDtypeStruct((n, rh, W), F32), jax.ShapeDtypeStruct((n, rh, W), BF16)),
        compiler_params=_params(("arbitrary", "arbitrary")))(core, gpack, recv)


def chip_sum_cols(gate, up, recv_gate, recv_up, core, name):
    R, W = gate.shape
    cw = W // 2
    rh = R // 2
    rb = _row_block(rh)
    nb = rh // rb

    def body(c_ref, g_ref, u_ref, rg_ref, ru_ref, o_ref, ob_ref):
        s = jnp.where(pl.program_id(0) < 2, g_ref[...] + rg_ref[...], u_ref[...] + ru_ref[...])
        o_ref[0] = s
        ob_ref[0] = s.astype(BF16)

    gate_col = lambda s: jnp.minimum(s, 1)
    up_col = lambda s: jnp.maximum(s - 2, 0)
    out = pl.BlockSpec((1, rb, cw), lambda s, j, c_ref: (s, j, 0))
    spec = pltpu.PrefetchScalarGridSpec(
        num_scalar_prefetch=1, grid=(N_CHIPS, nb),
        in_specs=[pl.BlockSpec((rb, cw), lambda s, j, c_ref: (c_ref[0] * nb + j, gate_col(s))),
                  pl.BlockSpec((rb, cw), lambda s, j, c_ref: (c_ref[0] * nb + j, up_col(s))),
                  pl.BlockSpec((rb, cw), lambda s, j, c_ref: (j, gate_col(s))),
                  pl.BlockSpec((rb, cw), lambda s, j, c_ref: (j, up_col(s)))],
        out_specs=(out, out))
    return pl.pallas_call(
        body, name=name, grid_spec=spec,
        out_shape=(jax.ShapeDtypeStruct((N_CHIPS, rh, cw), F32), jax.ShapeDtypeStruct((N_CHIPS, rh, cw), BF16)),
        compiler_params=_params(("arbitrary", "arbitrary")))(core, gate, up, recv_gate, recv_up)


def chip_sums_of(items, core, tag):
    parts = []
    for _, _, g in items:
        parts += list(g) if isinstance(g, tuple) else [g]
    swapped = iter(zip(parts, sibling_swap(parts, True, f"grad_swap_cores_{tag}")))
    core_arg = core.reshape(1).astype(jnp.int32)
    sums = []
    for n, l, g in items:
        if isinstance(g, tuple):
            (dwg, from_g), (dwu, from_u) = next(swapped), next(swapped)
            sums.append(chip_sum_cols(dwg, dwu, from_g, from_u, core_arg, f"grad_chip_sum_{n}{l}"))
        else:
            p, r = next(swapped)
            sums.append(chip_sum(p, r, core_arg, f"grad_chip_sum_{n}{l}"))
    return sums


def shard_sum(own, recv, name):
    R, W = own.shape
    rb = _row_block(R)

    def body(a_ref, r_ref, o_ref):
        acc = a_ref[...]
        for j in range(3):
            acc = acc + r_ref[j].astype(F32)
        o_ref[...] = acc

    return pl.pallas_call(
        body, name=name, grid=(R // rb,),
        in_specs=[pl.BlockSpec((rb, W), lambda i: (i, 0)), pl.BlockSpec((3, rb, W), lambda i: (0, i, 0))],
        out_specs=pl.BlockSpec((rb, W), lambda i: (i, 0)), out_shape=jax.ShapeDtypeStruct((R, W), F32),
        compiler_params=_params(("arbitrary",)))(own, recv)


def adamw(w, m, v, g, name):
    L, R, C = w.shape
    rb = max(b for b in range(8, 257, 8) if R % b == 0)
    bc1 = 1.0 - ADAM_B1 ** ADAM_STEP
    bc2 = 1.0 - ADAM_B2 ** ADAM_STEP

    def body(w_ref, m_ref, v_ref, g_ref, d_ref, nm_ref, nv_ref):
        gv = g_ref[...]
        nm = ADAM_B1 * m_ref[...] + (1.0 - ADAM_B1) * gv
        nv = ADAM_B2 * v_ref[...] + (1.0 - ADAM_B2) * (gv * gv)
        d_ref[...] = -ADAM_LR * ((nm / bc1) / (jnp.sqrt(nv / bc2) + ADAM_EPS) + ADAM_WD * w_ref[...])
        nm_ref[...] = nm
        nv_ref[...] = nv

    blk = pl.BlockSpec((1, rb, C), lambda l, i: (l, i, 0))
    sh = jax.ShapeDtypeStruct((L, R, C), F32)
    return pl.pallas_call(body, name=name, grid=(L, R // rb), in_specs=[blk] * 4, out_specs=(blk,) * 3,
                          out_shape=(sh, sh, sh), compiler_params=_params(("arbitrary", "arbitrary")))(w, m, v, g)


SMALL = [n for n in WEIGHTS if n not in BIG]


PACK_TILE = 8 * 128


def _pack(arrays):
    blocks = []
    for a in arrays:
        flat = a.reshape(-1)
        flat = jnp.pad(flat, (0, -flat.size % PACK_TILE))
        blocks.append(flat.reshape(-1, 128))
    rows = sum(b.shape[0] for b in blocks)
    if rows % 16:
        blocks.append(jnp.zeros((8, 128), arrays[0].dtype))
    return jnp.concatenate(blocks, axis=0)


def _unpack(pack, shapes):
    out, row = [], 0
    for shape in shapes:
        size = int(np.prod(shape))
        rows = -(-size // PACK_TILE) * 8
        out.append(pack[row:row + rows].reshape(-1)[:size].reshape(shape))
        row += rows
    return out


def _pad_lanes(a):
    return jnp.pad(a, ((0, 0), (0, -a.shape[1] % 128)))


def _local_shard(full, axis, chip):
    size = full.shape[axis] // N_CHIPS
    return lax.dynamic_slice_in_dim(full, chip * size, size, axis)


def kernel(x, ffn1_norm, ffn1_wi, ffn1_wo, mix_norm, w_in, w_out, lru_conv_w, lru_conv_b, lru_gate_a_w, lru_gate_a_b, lru_gate_x_w, lru_gate_x_b, lru_lambda, lru_out_norm, rwkv_mu, rwkv_w_up, rwkv_w_bias, rwkv_a_up, rwkv_a_bias, rwkv_g_up, rwkv_k_k, rwkv_k_a, rwkv_r_k, rwkv_ln_g, rwkv_ln_b, rwkv_vres_w1, rwkv_vres_w2, rwkv_vres_b, gdn_conv_w, gdn_a_log, gdn_dt_bias, gdn_norm, ffn2_norm, ffn2_wi, ffn2_wo, final_norm, loss_target, m_ffn1_norm, m_ffn1_wi, m_ffn1_wo, m_mix_norm, m_w_in, m_w_out, m_lru_conv_w, m_lru_conv_b, m_lru_gate_a_w, m_lru_gate_a_b, m_lru_gate_x_w, m_lru_gate_x_b, m_lru_lambda, m_lru_out_norm, m_rwkv_mu, m_rwkv_w_up, m_rwkv_w_bias, m_rwkv_a_up, m_rwkv_a_bias, m_rwkv_g_up, m_rwkv_k_k, m_rwkv_k_a, m_rwkv_r_k, m_rwkv_ln_g, m_rwkv_ln_b, m_rwkv_vres_w1, m_rwkv_vres_w2, m_rwkv_vres_b, m_gdn_conv_w, m_gdn_a_log, m_gdn_dt_bias, m_gdn_norm, m_ffn2_norm, m_ffn2_wi, m_ffn2_wo, m_final_norm, v_ffn1_norm, v_ffn1_wi, v_ffn1_wo, v_mix_norm, v_w_in, v_w_out, v_lru_conv_w, v_lru_conv_b, v_lru_gate_a_w, v_lru_gate_a_b, v_lru_gate_x_w, v_lru_gate_x_b, v_lru_lambda, v_lru_out_norm, v_rwkv_mu, v_rwkv_w_up, v_rwkv_w_bias, v_rwkv_a_up, v_rwkv_a_bias, v_rwkv_g_up, v_rwkv_k_k, v_rwkv_k_a, v_rwkv_r_k, v_rwkv_ln_g, v_rwkv_ln_b, v_rwkv_vres_w1, v_rwkv_vres_w2, v_rwkv_vres_b, v_gdn_conv_w, v_gdn_a_log, v_gdn_dt_bias, v_gdn_norm, v_ffn2_norm, v_ffn2_wi, v_ffn2_wo, v_final_norm):
    args = locals()
    w_loc = {n: args[n] for n in WEIGHTS}
    m_loc = {n: args['m_' + n] for n in WEIGHTS}
    v_loc = {n: args['v_' + n] for n in WEIGHTS}
    chip = 2 * lax.axis_index("x") + lax.axis_index("y")
    core = lax.axis_index("c")

    big = [(n, l) for n in BIG for l in range(N_LAYERS)]
    shards = {(n, l): _pad_lanes(w_loc[n][l].astype(BF16)) for n, l in big}
    first = [('ffn1_wi', 0), ('ffn1_wo', 0)]
    wb = {n: [None] * N_LAYERS for n in BIG}
    for (n, l), g in zip(first, allgather_chips([shards[k] for k in first], "allgather_first")):
        wb[n][l] = _natural(n, g)

    sm_names = list(SMALL_SHARDED)
    placed = []
    for n in sm_names:
        mine = [jnp.where((chip == s) & (core == 0), w_loc[n], 0.0) for s in range(N_CHIPS)]
        placed.append(jnp.concatenate(mine, axis=SMALL_SHARDED[n]))
    summed = allreduce_small(_pack(placed), "allgather_small")
    w_full = dict(w_loc)
    w_full.update(zip(sm_names, _unpack(summed, [p.shape for p in placed])))

    loss, dx, grads = local_step(x[0], loss_target[0], w_full, wb, shards)
    loss = lax.psum(loss, ("x", "y", "c"))

    gsum = allreduce_small(_pack([grads[n] for n in SMALL]), "allreduce_small")
    g_loc = {}
    for n, g in zip(SMALL, _unpack(gsum, [grads[n].shape for n in SMALL])):
        g_loc[n] = _local_shard(g, SMALL_SHARDED[n], chip) if n in SMALL_SHARDED else g

    last_keys, last_parts = grads['last_round']
    sems, parts_thru, lands_thru, token = scatter_start(last_parts, "grad_scatter_last_start")
    chip_after_start = chip + token[0, 0].astype(chip.dtype)
    rows = {n: [None] * N_LAYERS for n in BIG}
    delta, new_m, new_v = {}, {}, {}

    def finish(keys, arrived, which_chip, tag):
        halves = [shard_sum(lax.dynamic_index_in_dim(grads[n][l][0], which_chip, 0, keepdims=False), got,
                            f"grad_shard_sum_{n}{l}") for (n, l), got in zip(keys, arrived)]
        others = sibling_swap(halves, False, f"grad_share_cores_{tag}")
        for (n, l), half, other in zip(keys, halves, others):
            lower = jnp.where(core == 0, half, other)
            upper = jnp.where(core == 0, other, half)
            rows[n][l] = jnp.concatenate([lower, upper], axis=0)[:, :w_loc[n].shape[-1]]
        for n in BIG:
            if n not in delta and all(r is not None for r in rows[n]):
                g_loc[n] = jnp.stack(rows[n])
                delta[n], new_m[n], new_v[n] = adamw(w_loc[n], m_loc[n], v_loc[n], g_loc[n], f"adamw_{n}")

    early = [k for k in big if k not in last_keys]
    finish(early, [grads[n][l][1] for n, l in early], chip_after_start, "early")
    pack = lambda d: _pack([d[n] for n in SMALL])[None]
    res = adamw(pack(w_loc), pack(m_loc), pack(v_loc), pack(g_loc), "adamw_small")
    for dst, r in zip((delta, new_m, new_v), res):
        dst.update(zip(SMALL, _unpack(r[0], [w_loc[n].shape for n in SMALL])))
    arrived_last = scatter_wait(sems, parts_thru, lands_thru, delta['ffn2_wi'], "grad_scatter_last_wait")
    finish(last_keys, arrived_last, chip, "last")

    return (loss, dx[None], *[g_loc[n] for n in WEIGHTS], *[delta[n] for n in WEIGHTS],
            *[new_m[n] for n in WEIGHTS], *[new_v[n] for n in WEIGHTS])
```

```python
import functools

import numpy as np
import jax
import jax.numpy as jnp
from jax import lax
from jax.experimental import pallas as pl
from jax.experimental.pallas import tpu as pltpu

F32 = jnp.float32
BF16 = jnp.bfloat16
MESH = pl.DeviceIdType.MESH

D_MODEL = 1024
D_FF = 2816
N_LAYERS = 2
HEADS = 6
HEAD_DIM = 64
MIX_W = HEADS * HEAD_DIM
LRU_W = 256
LRU_BLOCKS = 4
D_IN = 3468
D_IN_PAD = 3584
NORM_EPS = 1e-6
GN_EPS = 64e-5
LRU_C = 8.0
CHUNK = 64
ROWS = 512
FF_CHUNK = 256
IN_CHUNK = 512
VMEM_LIMIT = 56 * 1024 * 1024

ADAM_LR, ADAM_B1, ADAM_B2, ADAM_EPS, ADAM_WD, ADAM_STEP = 0.001, 0.9, 0.999, 1e-08, 0.01, 10

WEIGHTS = ['ffn1_norm', 'ffn1_wi', 'ffn1_wo', 'mix_norm', 'w_in', 'w_out', 'lru_conv_w', 'lru_conv_b',
           'lru_gate_a_w', 'lru_gate_a_b', 'lru_gate_x_w', 'lru_gate_x_b', 'lru_lambda', 'lru_out_norm',
           'rwkv_mu', 'rwkv_w_up', 'rwkv_w_bias', 'rwkv_a_up', 'rwkv_a_bias', 'rwkv_g_up', 'rwkv_k_k',
           'rwkv_k_a', 'rwkv_r_k', 'rwkv_ln_g', 'rwkv_ln_b', 'rwkv_vres_w1', 'rwkv_vres_w2', 'rwkv_vres_b',
           'gdn_conv_w', 'gdn_a_log', 'gdn_dt_bias', 'gdn_norm', 'ffn2_norm', 'ffn2_wi', 'ffn2_wo', 'final_norm']
BIG = {'ffn1_wi': 2, 'ffn1_wo': 1, 'w_in': 2, 'w_out': 1, 'ffn2_wi': 2, 'ffn2_wo': 1}
SMALL_SHARDED = {'lru_conv_w': 2, 'rwkv_w_up': 2, 'rwkv_a_up': 2, 'rwkv_g_up': 2, 'rwkv_vres_w1': 1,
                 'rwkv_vres_w2': 2, 'gdn_conv_w': 2}
N_CHIPS = 4


def _params(sem=None):
    kw = dict(vmem_limit_bytes=VMEM_LIMIT)
    if sem is not None:
        kw['dimension_semantics'] = sem
    return pltpu.CompilerParams(**kw)


def _bdot(a, b, dims=(((1,), (0,)), ((), ()))):
    return lax.dot_general(a.astype(BF16), b.astype(BF16), dims, preferred_element_type=F32)


def _bdot_nt(a, b):
    return _bdot(a, b, (((1,), (1,)), ((), ())))


def _bdot_tn(a, b):
    return _bdot(a, b, (((0,), (0,)), ((), ())))


_DIMS = {'nn': (((1,), (0,)), ((), ())), 'nt': (((1,), (1,)), ((), ())), 'tn': (((0,), (0,)), ((), ()))}


def _split(a, terms):
    parts = []
    for _ in range(terms - 1):
        hi = a.astype(BF16)
        parts.append(hi)
        a = a - hi.astype(F32)
    parts.append(a.astype(BF16))
    return parts


_BATCH_DIMS = {'nn': (((2,), (1,)), ((0,), (0,))), 'nt': (((2,), (2,)), ((0,), (0,))),
               'tn': (((1,), (1,)), ((0,), (0,)))}


def _dot3(a, b, kind):
    ah, al = _split(a, 2)
    bh, bl = _split(b, 2)
    dims = _BATCH_DIMS[kind] if a.ndim == 3 else _DIMS[kind]
    d = lambda p, q: lax.dot_general(p, q, dims, preferred_element_type=F32)
    return d(ah, bh) + (d(ah, bl) + d(al, bh))


@functools.partial(jax.custom_vjp, nondiff_argnums=(2,))
def _cdot_k(a, b, kind):
    return _dot3(a, b, kind)


def _cdot_k_fwd(a, b, kind):
    return _dot3(a, b, kind), (a, b)


def _cdot_k_bwd(kind, res, ct):
    a, b = res
    if kind == 'nn':
        return _dot3(ct, b, 'nt'), _dot3(a, ct, 'tn')
    if kind == 'nt':
        return _dot3(ct, b, 'nn'), _dot3(ct, a, 'tn')
    return _dot3(b, ct, 'nt'), _dot3(a, ct, 'nn')


_cdot_k.defvjp(_cdot_k_fwd, _cdot_k_bwd)


def _dot1(a, b, kind):
    dims = _BATCH_DIMS[kind] if a.ndim == 3 else _DIMS[kind]
    return lax.dot_general(a.astype(BF16), b.astype(BF16), dims, preferred_element_type=F32)


@functools.partial(jax.custom_vjp, nondiff_argnums=(2,))
def _cdot1_k(a, b, kind):
    return _dot1(a, b, kind)


def _cdot1_k_fwd(a, b, kind):
    return _dot1(a, b, kind), (a, b)


def _cdot1_k_bwd(kind, res, ct):
    a, b = res
    if kind == 'nn':
        return _dot1(ct, b, 'nt'), _dot1(a, ct, 'tn')
    if kind == 'nt':
        return _dot1(ct, b, 'nn'), _dot1(ct, a, 'tn')
    return _dot1(b, ct, 'nt'), _dot1(a, ct, 'nn')


_cdot1_k.defvjp(_cdot1_k_fwd, _cdot1_k_bwd)


def _cdot(a, b):
    return _cdot1_k(a, b, 'nn')


def _cdot_nt(a, b):
    return _cdot1_k(a, b, 'nt')


def _cdot_tn(a, b):
    return _cdot1_k(a, b, 'tn')


def _hdot(a, b):
    return _cdot_k(a, b, 'nn')


def _dot_exact(x, m01, kind):
    d = lambda p: lax.dot_general(p, m01.astype(BF16), _DIMS[kind], preferred_element_type=F32)
    hi, mid, lo = _split(x, 3)
    return d(hi) + (d(mid) + d(lo))


@functools.partial(jax.custom_vjp, nondiff_argnums=(1,))
def _xdot(x, make_m):
    return _dot_exact(x, make_m(), 'nn')


def _xdot_fwd(x, make_m):
    return _dot_exact(x, make_m(), 'nn'), None


def _xdot_bwd(make_m, _, ct):
    return (_dot_exact(ct, make_m(), 'nt'),)


_xdot.defvjp(_xdot_fwd, _xdot_bwd)


def _iota2(n, m):
    return lax.broadcasted_iota(jnp.int32, (n, m), 0), lax.broadcasted_iota(jnp.int32, (n, m), 1)


def _head_blocks(w):
    ri, ci = _iota2(w, w)
    return (ri // HEAD_DIM == ci // HEAD_DIM).astype(F32)


def _segsum(x):
    return _xdot(x, functools.partial(_head_blocks, x.shape[-1]))


def _cumsum_rows(x):
    return _cumsum_k(x, x.shape[0])


@functools.partial(jax.custom_vjp, nondiff_argnums=(1,))
def _cumsum_k(x, n):
    return _lower_dot(x, n, False)


def _lower_dot(x, n, transpose):
    ri, ci = _iota2(n, n)
    m = ((ri <= ci) if transpose else (ri >= ci)).astype(BF16)
    d = lambda p: lax.dot_general(m, p, _DIMS['nn'], preferred_element_type=F32)
    hi, mid, lo = _split(x, 3)
    return d(hi) + (d(mid) + d(lo))


def _cumsum_k_fwd(x, n):
    return _lower_dot(x, n, False), None


def _cumsum_k_bwd(n, _, ct):
    return (_lower_dot(ct, n, True),)


_cumsum_k.defvjp(_cumsum_k_fwd, _cumsum_k_bwd)


def _rms(x, g):
    return x * lax.rsqrt(jnp.mean(x * x, axis=-1, keepdims=True) + NORM_EPS) * g


DENSE_ROWS = 1024


def _row_loop(n_rows, fn):
    rows = min(DENSE_ROWS, n_rows)

    def step(i, c):
        fn(pl.ds(pl.multiple_of(i * rows, rows), rows))
        return c
    lax.fori_loop(0, n_rows // rows, step, 0)


def ffn_fwd(x, g, wi, wo, name, hosted=()):
    T = x.shape[0]
    nj = D_FF // FF_CHUNK
    gather = ChipGather(list(hosted))
    n = gather.n

    def body(*refs):
        x_ref, g_ref, wg_ref, wu_ref, wo_ref = refs[:5]
        hx, o_ref, ho = refs[5:5 + n], refs[5 + n], refs[6 + n:6 + 2 * n]
        h_ref, acc_ref = refs[6 + 2 * n:8 + 2 * n]
        sems = refs[8 + 2 * n:]
        j = pl.program_id(0)

        @pl.when(j == 0)
        def _():
            gather.start(hx, ho, sems)

            def init(r):
                h_ref[r, :] = _rms(x_ref[r, :], g_ref[...]).astype(BF16)
                acc_ref[r, :] = jnp.zeros((r.size, D_MODEL), F32)
            _row_loop(T, init)

        def blk(r):
            hb = h_ref[r, :]
            gate = jnp.dot(hb, wg_ref[...], preferred_element_type=F32)
            up = jnp.dot(hb, wu_ref[...], preferred_element_type=F32)
            a = (gate * jax.nn.sigmoid(gate) * up).astype(BF16)
            acc_ref[r, :] += jnp.dot(a, wo_ref[...], preferred_element_type=F32)
        _row_loop(T, blk)

        @pl.when(j == nj - 2)
        def _():
            gather.relay(hx, ho, sems)

        @pl.when(j == nj - 1)
        def _():
            def fin(r):
                o_ref[r, :] = x_ref[r, :] + 0.5 * acc_ref[r, :]
            _row_loop(T, fin)
            gather.finish(hx, ho, sems)

    full = pl.BlockSpec((T, D_MODEL), lambda j: (0, 0))
    res = pl.pallas_call(
        body, name=name, grid=(nj,),
        in_specs=[full, pl.BlockSpec((1, D_MODEL), lambda j: (0, 0)),
                  pl.BlockSpec((D_MODEL, FF_CHUNK), lambda j: (0, j)),
                  pl.BlockSpec((D_MODEL, FF_CHUNK), lambda j: (0, j + nj)),
                  pl.BlockSpec((FF_CHUNK, D_MODEL), lambda j: (j, 0))] + gather.in_specs,
        out_specs=tuple([full] + gather.out_specs),
        out_shape=tuple([jax.ShapeDtypeStruct((T, D_MODEL), F32)] + gather.out_shape),
        scratch_shapes=[pltpu.VMEM((T, D_MODEL), BF16), pltpu.VMEM((T, D_MODEL), F32)] + gather.scratch,
        compiler_params=_params(("arbitrary",)))(x, g, wi, wi, wo, *hosted)
    return res[0], list(res[1:])


def _norm_bwd_rows(x, g, dh, dres):
    rstd = lax.rsqrt(jnp.mean(x * x, axis=-1, keepdims=True) + NORM_EPS)
    xh = x * rstd
    dxh = dh * g
    dx = rstd * (dxh - xh * jnp.mean(dxh * xh, axis=-1, keepdims=True))
    return dres + dx, jnp.sum(dh * xh, axis=0, keepdims=True)


def ffn_bwd(x, dy, g, wi, wo, name, hosted=()):
    T = x.shape[0]
    nj = D_FF // FF_CHUNK
    scatter = ChipScatter(list(hosted))
    n = scatter.n

    def body(*refs):
        x_ref, dy_ref, g_ref, wg_ref, wu_ref, wo_ref = refs[:6]
        hx = refs[6:6 + n]
        dx_ref, dg_ref, dwg_ref, dwu_ref, dwo_ref = refs[6 + n:11 + n]
        ho = refs[11 + n:11 + 2 * n]
        h_ref, da_ref, dh_ref = refs[11 + 2 * n:14 + 2 * n]
        sems = refs[14 + 2 * n:]
        j = pl.program_id(0)

        @pl.when(j == 0)
        def _():
            scatter.start(hx, ho, sems)

            def init(r):
                h_ref[r, :] = _rms(x_ref[r, :], g_ref[...]).astype(BF16)
                da_ref[r, :] = (0.5 * dy_ref[r, :]).astype(BF16)
                dh_ref[r, :] = jnp.zeros((r.size, D_MODEL), F32)
            _row_loop(T, init)

        dwg_ref[...] = jnp.zeros_like(dwg_ref)
        dwu_ref[...] = jnp.zeros_like(dwu_ref)
        dwo_ref[...] = jnp.zeros_like(dwo_ref)

        def blk(r):
            hb = h_ref[r, :]
            db = da_ref[r, :]
            gate = jnp.dot(hb, wg_ref[...], preferred_element_type=F32)
            up = jnp.dot(hb, wu_ref[...], preferred_element_type=F32)
            sg = jax.nn.sigmoid(gate)
            sl = gate * sg
            da = _bdot_nt(db, wo_ref[...])
            dup = (da * sl).astype(BF16)
            dgate = (da * up * (sg * (1.0 + gate * (1.0 - sg)))).astype(BF16)
            dwo_ref[...] += _bdot_tn((sl * up).astype(BF16), db)
            dwg_ref[...] += _bdot_tn(hb, dgate)
            dwu_ref[...] += _bdot_tn(hb, dup)
            dh_ref[r, :] += _bdot_nt(dgate, wg_ref[...]) + _bdot_nt(dup, wu_ref[...])
        _row_loop(T, blk)

        @pl.when(j == nj - 1)
        def _():
            dg_ref[...] = jnp.zeros_like(dg_ref)

            def fin(r):
                dx, dg = _norm_bwd_rows(x_ref[r, :], g_ref[...], dh_ref[r, :], dy_ref[r, :])
                dx_ref[r, :] = dx
                dg_ref[...] += dg
            _row_loop(T, fin)
            scatter.finish(hx, ho, sems)

    full = pl.BlockSpec((T, D_MODEL), lambda j: (0, 0))
    vec = pl.BlockSpec((1, D_MODEL), lambda j: (0, 0))
    res = pl.pallas_call(
        body, name=name, grid=(nj,),
        in_specs=[full, full, vec,
                  pl.BlockSpec((D_MODEL, FF_CHUNK), lambda j: (0, j)),
                  pl.BlockSpec((D_MODEL, FF_CHUNK), lambda j: (0, j + nj)),
                  pl.BlockSpec((FF_CHUNK, D_MODEL), lambda j: (j, 0))] + scatter.in_specs,
        out_specs=tuple([full, vec,
                         pl.BlockSpec((D_MODEL, FF_CHUNK), lambda j: (0, j)),
                         pl.BlockSpec((D_MODEL, FF_CHUNK), lambda j: (0, j)),
                         pl.BlockSpec((FF_CHUNK, D_MODEL), lambda j: (j, 0))] + scatter.out_specs),
        out_shape=tuple([jax.ShapeDtypeStruct((T, D_MODEL), F32), jax.ShapeDtypeStruct((1, D_MODEL), F32),
                         jax.ShapeDtypeStruct((D_MODEL, D_FF), F32), jax.ShapeDtypeStruct((D_MODEL, D_FF), F32),
                         jax.ShapeDtypeStruct((D_FF, D_MODEL), F32)] + scatter.out_shape),
        scratch_shapes=[pltpu.VMEM((T, D_MODEL), BF16), pltpu.VMEM((T, D_MODEL), BF16),
                        pltpu.VMEM((T, D_MODEL), F32)] + scatter.scratch,
        compiler_params=_params(("arbitrary",)))(x, dy, g, wi, wi, wo, *hosted)
    return res[0], res[1], res[2], res[3], res[4], list(res[5:])


def proj_fwd(x, g, w, name):
    T = x.shape[0]
    nj = D_IN_PAD // IN_CHUNK

    def body(x_ref, g_ref, w_ref, o_ref, h_ref):
        @pl.when(pl.program_id(0) == 0)
        def _():
            def init(r):
                h_ref[r, :] = _rms(x_ref[r, :], g_ref[...]).astype(BF16)
            _row_loop(T, init)

        def blk(r):
            o_ref[r, :] = jnp.dot(h_ref[r, :], w_ref[...], preferred_element_type=F32)
        _row_loop(T, blk)

    return pl.pallas_call(
        body, name=name, grid=(nj,),
        in_specs=[pl.BlockSpec((T, D_MODEL), lambda j: (0, 0)), pl.BlockSpec((1, D_MODEL), lambda j: (0, 0)),
                  pl.BlockSpec((D_MODEL, IN_CHUNK), lambda j: (0, j))],
        out_specs=pl.BlockSpec((T, IN_CHUNK), lambda j: (0, j)),
        out_shape=jax.ShapeDtypeStruct((T, D_IN_PAD), F32),
        scratch_shapes=[pltpu.VMEM((T, D_MODEL), BF16)],
        compiler_params=_params(("arbitrary",)))(x, g, w)


def proj_bwd(x, dres, g, w, dp, name):
    T = x.shape[0]
    nj = D_IN_PAD // IN_CHUNK

    def body(x_ref, dres_ref, g_ref, w_ref, dp_ref, dx_ref, dg_ref, dw_ref, h_ref, dh_ref):
        j = pl.program_id(0)

        @pl.when(j == 0)
        def _():
            def init(r):
                h_ref[r, :] = _rms(x_ref[r, :], g_ref[...]).astype(BF16)
                dh_ref[r, :] = jnp.zeros((r.size, D_MODEL), F32)
            _row_loop(T, init)

        dw_ref[...] = jnp.zeros_like(dw_ref)

        def blk(r):
            dpb = dp_ref[r, :].astype(BF16)
            dw_ref[...] += _bdot_tn(h_ref[r, :], dpb)
            dh_ref[r, :] += _bdot_nt(dpb, w_ref[...])
        _row_loop(T, blk)

        @pl.when(j == nj - 1)
        def _():
            dg_ref[...] = jnp.zeros_like(dg_ref)

            def fin(r):
                dx, dg = _norm_bwd_rows(x_ref[r, :], g_ref[...], dh_ref[r, :], dres_ref[r, :])
                dx_ref[r, :] = dx
                dg_ref[...] += dg
            _row_loop(T, fin)

    full = pl.BlockSpec((T, D_MODEL), lambda j: (0, 0))
    vec = pl.BlockSpec((1, D_MODEL), lambda j: (0, 0))
    return pl.pallas_call(
        body, name=name, grid=(nj,),
        in_specs=[full, full, vec, pl.BlockSpec((D_MODEL, IN_CHUNK), lambda j: (0, j)),
                  pl.BlockSpec((T, IN_CHUNK), lambda j: (0, j))],
        out_specs=(full, vec, pl.BlockSpec((D_MODEL, IN_CHUNK), lambda j: (0, j))),
        out_shape=(jax.ShapeDtypeStruct((T, D_MODEL), F32), jax.ShapeDtypeStruct((1, D_MODEL), F32),
                   jax.ShapeDtypeStruct((D_MODEL, D_IN_PAD), F32)),
        scratch_shapes=[pltpu.VMEM((T, D_MODEL), BF16), pltpu.VMEM((T, D_MODEL), F32)],
        compiler_params=_params(("arbitrary",)))(x, dres, g, w, dp)


def out_fwd(mixed, w, x, name):
    T = x.shape[0]

    def body(m_ref, w_ref, x_ref, o_ref):
        o_ref[...] = x_ref[...] + jnp.dot(m_ref[...].astype(BF16), w_ref[...], preferred_element_type=F32)

    blk = pl.BlockSpec((ROWS, D_MODEL), lambda i: (i, 0))
    return pl.pallas_call(
        body, name=name, grid=(T // ROWS,),
        in_specs=[blk, pl.BlockSpec((D_MODEL, D_MODEL), lambda i: (0, 0)), blk],
        out_specs=blk, out_shape=jax.ShapeDtypeStruct((T, D_MODEL), F32),
        compiler_params=_params(("arbitrary",)))(mixed, w, x)


def out_bwd(mixed, w, dy, name):
    T = dy.shape[0]

    def body(m_ref, w_ref, dy_ref, dm_ref, dw_ref):
        @pl.when(pl.program_id(0) == 0)
        def _():
            dw_ref[...] = jnp.zeros_like(dw_ref)
        dyb = dy_ref[...].astype(BF16)
        dm_ref[...] = _bdot_nt(dyb, w_ref[...])
        dw_ref[...] += _bdot_tn(m_ref[...].astype(BF16), dyb)

    blk = pl.BlockSpec((ROWS, D_MODEL), lambda i: (i, 0))
    sq = pl.BlockSpec((D_MODEL, D_MODEL), lambda i: (0, 0))
    return pl.pallas_call(
        body, name=name, grid=(T // ROWS,),
        in_specs=[blk, sq, blk], out_specs=(blk, sq),
        out_shape=(jax.ShapeDtypeStruct((T, D_MODEL), F32), jax.ShapeDtypeStruct((D_MODEL, D_MODEL), F32)),
        compiler_params=_params(("arbitrary",)))(mixed, w, dy)


def loss_head(x, g, target, name):
    T = x.shape[0]

    def body(x_ref, g_ref, t_ref, loss_ref, dx_ref, dg_ref):
        @pl.when(pl.program_id(0) == 0)
        def _():
            loss_ref[...] = jnp.zeros_like(loss_ref)
            dg_ref[...] = jnp.zeros_like(dg_ref)
        xb = x_ref[...]
        rstd = lax.rsqrt(jnp.mean(xb * xb, axis=-1, keepdims=True) + NORM_EPS)
        xh = xb * rstd
        err = xh * g_ref[...] - t_ref[...]
        loss_ref[...] += 0.5 * jnp.sum(jnp.mean(err * err, axis=-1, keepdims=True), axis=0, keepdims=True)
        dy = err * (1.0 / D_MODEL)
        dg_ref[...] += jnp.sum(dy * xh, axis=0, keepdims=True)
        dxh = dy * g_ref[...]
        dx_ref[...] = rstd * (dxh - xh * jnp.mean(dxh * xh, axis=-1, keepdims=True))

    blk = pl.BlockSpec((ROWS, D_MODEL), lambda i: (i, 0))
    vec = pl.BlockSpec((1, D_MODEL), lambda i: (0, 0))
    return pl.pallas_call(
        body, name=name, grid=(T // ROWS,),
        in_specs=[blk, vec, blk], out_specs=(pl.BlockSpec((1, 1), lambda i: (0, 0)), blk, vec),
        out_shape=(jax.ShapeDtypeStruct((1, 1), F32), jax.ShapeDtypeStruct((T, D_MODEL), F32),
                   jax.ShapeDtypeStruct((1, D_MODEL), F32)),
        compiler_params=_params(("arbitrary",)))(x, g, target)


def rowwise_fwd(fn, rows, shared, out_widths, name):
    T = rows[0].shape[0]
    n_in = len(rows) + len(shared)

    def body(*refs):
        res = fn(*[r[...] for r in refs[:n_in]])
        for o, v in zip(refs[n_in:], res):
            o[...] = v

    in_specs = ([pl.BlockSpec((ROWS, a.shape[1]), lambda i: (i, 0)) for a in rows]
                + [pl.BlockSpec(a.shape, lambda i: (0, 0)) for a in shared])
    return pl.pallas_call(
        body, name=name, grid=(T // ROWS,), in_specs=in_specs,
        out_specs=tuple(pl.BlockSpec((ROWS, w), lambda i: (i, 0)) for w in out_widths),
        out_shape=tuple(jax.ShapeDtypeStruct((T, w), F32) for w in out_widths),
        compiler_params=_params(("arbitrary",)))(*rows, *shared)


def rowwise_bwd(fn, rows, shared, cts, name, ct_fn=None):
    T = rows[0].shape[0]
    nr, ns, nc = len(rows), len(shared), len(cts)

    def body(*refs):
        ins = [r[...] for r in refs[:nr + ns]]
        ctv = tuple(r[...] for r in refs[nr + ns:nr + ns + nc])
        outs = refs[nr + ns + nc:]
        _, vjp = jax.vjp(fn, *ins)
        grads = vjp(ct_fn(*ctv) if ct_fn is not None else ctv)
        for k in range(nr):
            outs[k][...] = grads[k]

        @pl.when(pl.program_id(0) == 0)
        def _():
            for k in range(ns):
                outs[nr + k][...] = jnp.zeros_like(outs[nr + k])
        for k in range(ns):
            outs[nr + k][...] += grads[nr + k]

    row_spec = lambda a: pl.BlockSpec((ROWS, a.shape[1]), lambda i: (i, 0))
    sh_spec = lambda a: pl.BlockSpec(a.shape, lambda i: (0, 0))
    return pl.pallas_call(
        body, name=name, grid=(T // ROWS,),
        in_specs=[row_spec(a) for a in rows] + [sh_spec(a) for a in shared] + [row_spec(a) for a in cts],
        out_specs=tuple([row_spec(a) for a in rows] + [sh_spec(a) for a in shared]),
        out_shape=tuple(jax.ShapeDtypeStruct(a.shape, F32) for a in list(rows) + list(shared)),
        compiler_params=_params(("arbitrary",)))(*rows, *shared, *cts)


def shift_rows(x, s):
    return jnp.pad(x, ((s, 0), (0, 0)))[:x.shape[0]]


def unshift_rows(x, s):
    return jnp.pad(x, ((0, s), (0, 0)))[s:]


def _neg_expm1(y):
    series = -(y * (1.0 + y * (0.5 + y * (1.0 / 6.0 + y * (1.0 / 24.0)))))
    return jnp.where(y > -0.05, series, 1.0 - jnp.exp(y))


def lru_pre_fn(x0, x1, x2, x3, first, w0, w1, w2, w3, cb, ga, gab, gx, gxb, lam):
    xc = w3 * x0 + w2 * x1 + w1 * x2 + w0 * x3 + cb
    r = jax.nn.sigmoid(_hdot(xc, ga) + gab)
    i = jax.nn.sigmoid(_hdot(xc, gx) + gxb)
    log_a = -LRU_C * r * jax.nn.softplus(-lam)
    a = jnp.exp(log_a)
    mult = jnp.where(first > 0.5, 1.0, jnp.sqrt(_neg_expm1(2.0 * log_a)))
    return a, mult * i * xc


def lru_post_fn(h, py, og):
    return (_rms(h * jax.nn.gelu(py), og),)


def lru_scan(a, b, reverse, name):
    T, C = a.shape
    nb = T // 8

    def body(a_ref, b_ref, h_ref):
        rows = lax.broadcasted_iota(jnp.int32, (8, C), 0)

        def blk(i, carry):
            j = nb - 1 - i if reverse else i
            r = pl.ds(pl.multiple_of(j * 8, 8), 8)
            A = a_ref[r, :]
            B = b_ref[r, :]
            for s in (1, 2, 4):
                if reverse:
                    keep = rows < 8 - s
                    sh = 8 - s
                else:
                    keep = rows >= s
                    sh = s
                Bs = jnp.where(keep, pltpu.roll(B, sh, 0), 0.0)
                As = jnp.where(keep, pltpu.roll(A, sh, 0), 1.0)
                B = B + A * Bs
                A = A * As
            hb = B + A * carry
            h_ref[r, :] = hb
            edge = 0 if reverse else 7
            return jnp.sum(jnp.where(rows == edge, hb, 0.0), axis=0, keepdims=True)

        lax.fori_loop(0, nb, blk, jnp.zeros((1, C), F32))

    full = pl.BlockSpec((T, C), lambda: (0, 0))
    return pl.pallas_call(body, name=name, in_specs=[full, full], out_specs=full,
                          out_shape=jax.ShapeDtypeStruct((T, C), F32), compiler_params=_params())(a, b)


def make_rwkv_pre_fn(has_vres):
    def fn(p, pp, *rest):
        if has_vres:
            vf, mu, w_up, w_b, a_up, a_b, g_up, kk_w, ka_w, vw1, vw2, vb = rest
        else:
            mu, w_up, w_b, a_up, a_b, g_up, kk_w, ka_w = rest
        xm = p + (pp - p) * mu
        r, k, v = xm[:, 0:384], xm[:, 384:768], xm[:, 768:1152]
        xw, xa, xg = xm[:, 1152:1216], xm[:, 1216:1280], xm[:, 1280:1408]
        w_log = -jax.nn.softplus(-(w_b + _hdot(jnp.tanh(xw), w_up))) - 0.5
        lw = -jnp.exp(w_log)
        a = jax.nn.sigmoid(a_b + _hdot(xa, a_up))
        g = _hdot(jax.nn.sigmoid(xg), g_up)
        if has_vres:
            v = v + (vf - v) * jax.nn.sigmoid(vb + _hdot(_hdot(v, vw1), vw2))
        kkx = k * kk_w
        kk = kkx * lax.rsqrt(_segsum(kkx * kkx) + 1e-6)
        k2 = k * (1.0 + (a - 1.0) * ka_w)
        return r, lw, k2, v, kk, a, g
    return fn


def rwkv_post_fn(y, r, k2, v, g, ln_g, ln_b, r_k):
    mean = _segsum(y) * (1.0 / HEAD_DIM)
    yc = y - mean
    var = _segsum(yc * yc) * (1.0 / HEAD_DIM)
    yn = yc * lax.rsqrt(var + GN_EPS) * ln_g + ln_b
    bonus = _segsum(r * k2 * r_k) * v
    return ((yn + bonus) * g,)


def _head_expander(first_lane):
    ri, ci = _iota2(128, MIX_W)
    return (ri == ci // HEAD_DIM + first_lane).astype(F32)


def gdn_pre_fn(x0, x1, x2, x3, ab, w0, w1, w2, w3, alog, dtb):
    qkv = jax.nn.silu(w3 * x0 + w2 * x1 + w1 * x2 + w0 * x3)
    q, k, v = qkv[:, 0:384], qkv[:, 384:768], qkv[:, 768:1152]
    q = q * lax.rsqrt(_segsum(q * q) + 1e-6) * (HEAD_DIM ** -0.5)
    k = k * lax.rsqrt(_segsum(k * k) + 1e-6)
    g = -jnp.exp(alog) * jax.nn.softplus(ab + dtb)
    beta = jax.nn.sigmoid(ab)
    ge = _xdot(g, functools.partial(_head_expander, 0))
    be = _xdot(beta, functools.partial(_head_expander, HEADS))
    return q, k, v, ge, be


def gdn_post_fn(o, z, ng):
    ms = _segsum(o * o) * (1.0 / HEAD_DIM)
    return (o * lax.rsqrt(ms + NORM_EPS) * ng * jax.nn.silu(z),)


def _neumann_inv(m):
    n = m.shape[-1]
    ri, ci = _iota2(n, n)
    eye = (ri == ci).astype(F32)
    md = jnp.where(ri // 16 == ci // 16, m, 0.0)
    mo = m - md
    t0 = eye + md
    p2 = _hdot(md, md)
    t0 = t0 + _hdot(t0, p2)
    p4 = _hdot(p2, p2)
    t0 = t0 + _hdot(t0, p4)
    p8 = _hdot(p4, p4)
    t0 = t0 + _hdot(t0, p8)
    nn = _hdot(t0, mo)
    n2 = _hdot(nn, nn)
    t1 = eye + nn + n2 + _hdot(nn, n2)
    return _hdot(t1, t0)


@jax.custom_vjp
def _inv_saved(m, t_saved):
    return t_saved


def _inv_saved_fwd(m, t_saved):
    return t_saved, t_saved


def _inv_saved_bwd(t_saved, dt):
    tt = jnp.swapaxes(t_saved, -1, -2)
    return _hdot(_hdot(tt, dt), tt), jnp.zeros_like(t_saved)


_inv_saved.defvjp(_inv_saved_fwd, _inv_saved_bwd)


def _heads(x):
    return jnp.concatenate([x[None, :, h * HEAD_DIM:(h + 1) * HEAD_DIM] for h in range(HEADS)], axis=0)


def _unheads(y):
    return jnp.concatenate([lax.index_in_dim(y, h, 0, keepdims=False) for h in range(HEADS)], axis=1)


def rwkv_heads(s0, r, lw, k2, v, kk, a, inv):
    n = r.shape[0]
    ri, ci = _iota2(n, n)
    low, strict = ri >= ci, ri > ci
    cs = _cumsum_rows(lw)
    cl = jnp.sum(lw, axis=0, keepdims=True)
    p_in, p_prev, p_inv = jnp.exp(cs), jnp.exp(cs - lw), jnp.exp(-cs)
    p_rest, p_all = jnp.exp(cl - cs), jnp.exp(cl)
    bd = kk * a
    at, rt = _heads(-kk * p_prev), _heads(r * p_in)
    bh, kh = _heads(bd * p_inv), _heads(k2 * p_inv)
    vh = _heads(v)
    m_ab = jnp.where(strict, _cdot_nt(at, bh), 0.0)
    m_ak = jnp.where(strict, _cdot_nt(at, kh), 0.0)
    m_rb = jnp.where(low, _cdot_nt(rt, bh), 0.0)
    m_rk = jnp.where(low, _cdot_nt(rt, kh), 0.0)
    sa = _cdot(inv(m_ab), _cdot_nt(at, s0) + _cdot(m_ak, vh))
    y = _cdot_nt(rt, s0) + _cdot(m_rb, sa) + _cdot(m_rk, vh)
    s1 = s0 * _heads(p_all) + _cdot_tn(sa, _heads(bd * p_rest)) + _cdot_tn(vh, _heads(k2 * p_rest))
    return _unheads(y), s1


def gdn_heads(s0, q, k, v, ge, be, inv):
    n = q.shape[0]
    ri, ci = _iota2(n, n)
    low, strict = ri >= ci, ri > ci
    gc = _cumsum_rows(ge)
    gl = jnp.sum(ge, axis=0, keepdims=True)
    gch = _heads(gc)
    decay = jnp.where(low, jnp.exp(jnp.where(low, gch - jnp.swapaxes(gch, 1, 2), 0.0)), 0.0)
    kb = k * be
    e = jnp.exp(gc)
    kh = _heads(k)
    m = -jnp.where(strict, _cdot_nt(_heads(kb), kh) * decay, 0.0)
    mr = jnp.where(low, _cdot_nt(_heads(q), kh) * decay, 0.0)
    u = _cdot(inv(m), _heads(v * be) - _cdot_nt(_heads(kb * e), s0))
    y = _cdot_nt(_heads(q * e), s0) + _cdot(mr, u)
    s1 = s0 * _heads(jnp.exp(gl)) + _cdot_tn(u, _heads(k * jnp.exp(gl - gc)))
    return _unheads(y), s1


def core_fwd(heads_fn, ins, name, hosted=()):
    T = ins[0].shape[0]
    nc = T // CHUNK
    n = len(ins)
    gather = ChipGather(list(hosted))
    ng = gather.n

    def body(*refs):
        hx = refs[n:n + ng]
        y_ref, s0_ref, t_ref = refs[n + ng:n + ng + 3]
        ho = refs[n + ng + 3:n + 2 * ng + 3]
        s_ref = refs[n + 2 * ng + 3]
        sems = refs[n + 2 * ng + 4:]
        c = pl.program_id(0)

        @pl.when(c == 0)
        def _():
            gather.start(hx, ho, sems)
            s_ref[...] = jnp.zeros_like(s_ref)

        s0 = s_ref[...]
        kept = []

        def inv(m):
            kept.append(_neumann_inv(m))
            return kept[0]

        y, s1 = heads_fn(s0, *[r[...] for r in refs[:n]], inv)
        y_ref[...] = y
        s0_ref[0] = s0
        t_ref[0] = kept[0]
        s_ref[...] = s1

        @pl.when(c == nc - 4)
        def _():
            gather.relay(hx, ho, sems)

        @pl.when(c == nc - 1)
        def _():
            gather.finish(hx, ho, sems)

    row = pl.BlockSpec((CHUNK, MIX_W), lambda c: (c, 0))
    st_shape = (HEADS, HEAD_DIM, HEAD_DIM)
    st = pl.BlockSpec((1,) + st_shape, lambda c: (c, 0, 0, 0))
    res = pl.pallas_call(
        body, name=name, grid=(nc,), in_specs=[row] * n + gather.in_specs,
        out_specs=tuple([row, st, st] + gather.out_specs),
        out_shape=tuple([jax.ShapeDtypeStruct((T, MIX_W), F32), jax.ShapeDtypeStruct((nc,) + st_shape, F32),
                         jax.ShapeDtypeStruct((nc,) + st_shape, F32)] + gather.out_shape),
        scratch_shapes=[pltpu.VMEM(st_shape, F32)] + gather.scratch,
        compiler_params=_params(("arbitrary",)))(*ins, *hosted)
    return res[0], res[1], res[2], list(res[3:])


def core_bwd(heads_fn, ins, s0_all, t_all, dy, name):
    T = ins[0].shape[0]
    nc = T // CHUNK
    n = len(ins)

    def body(*refs):
        s0_ref, t_ref, dy_ref = refs[n:n + 3]
        outs = refs[n + 3:n + 3 + n]
        ds_ref = refs[n + 3 + n]

        @pl.when(pl.program_id(0) == 0)
        def _():
            ds_ref[...] = jnp.zeros_like(ds_ref)

        t_saved = t_ref[0]
        f = lambda s0, *xs: heads_fn(s0, *xs, lambda m: _inv_saved(m, t_saved))
        _, vjp = jax.vjp(f, s0_ref[0], *[r[...] for r in refs[:n]])
        grads = vjp((dy_ref[...], ds_ref[...]))
        ds_ref[...] = grads[0]
        for k in range(n):
            outs[k][...] = grads[1 + k]

    row = pl.BlockSpec((CHUNK, MIX_W), lambda c: (nc - 1 - c, 0))
    st_shape = (HEADS, HEAD_DIM, HEAD_DIM)
    st = pl.BlockSpec((1,) + st_shape, lambda c: (nc - 1 - c, 0, 0, 0))
    return pl.pallas_call(
        body, name=name, grid=(nc,), in_specs=[row] * n + [st, st, row], out_specs=tuple([row] * n),
        out_shape=tuple(jax.ShapeDtypeStruct((T, MIX_W), F32) for _ in range(n)),
        scratch_shapes=[pltpu.VMEM(st_shape, F32)],
        compiler_params=_params(("arbitrary",)))(*ins, s0_all, t_all, dy)


def _block_diag(w):
    out = jnp.zeros((LRU_W, LRU_W), w.dtype)
    for n in range(LRU_BLOCKS):
        out = lax.dynamic_update_slice(out, w[n], (n * 64, n * 64))
    return out


def _block_diag_grad(g):
    return jnp.stack([g[n * 64:(n + 1) * 64, n * 64:(n + 1) * 64] for n in range(LRU_BLOCKS)])


def _row(v):
    return v.reshape(1, -1)


def _pad128(v):
    return jnp.pad(v.reshape(1, -1), ((0, 0), (0, 128 - v.size)))


def _layer_shared(w, l):
    cw = w['lru_conv_w'][l]
    lru_pre = [_row(cw[0]), _row(cw[1]), _row(cw[2]), _row(cw[3]), _row(w['lru_conv_b'][l]),
               _block_diag(w['lru_gate_a_w'][l]), _row(w['lru_gate_a_b'][l]),
               _block_diag(w['lru_gate_x_w'][l]), _row(w['lru_gate_x_b'][l]), _row(w['lru_lambda'][l])]
    rw_pre = [_row(w['rwkv_mu'][l]), w['rwkv_w_up'][l], _row(w['rwkv_w_bias'][l]), w['rwkv_a_up'][l],
              _row(w['rwkv_a_bias'][l]), w['rwkv_g_up'][l], _row(w['rwkv_k_k'][l]), _row(w['rwkv_k_a'][l])]
    if l > 0:
        rw_pre += [w['rwkv_vres_w1'][l - 1], w['rwkv_vres_w2'][l - 1], _row(w['rwkv_vres_b'][l - 1])]
    rw_post = [_row(w['rwkv_ln_g'][l]), _row(w['rwkv_ln_b'][l]), _row(w['rwkv_r_k'][l])]
    gw = w['gdn_conv_w'][l]
    gdn_pre = [_row(gw[0]), _row(gw[1]), _row(gw[2]), _row(gw[3]), _pad128(w['gdn_a_log'][l]),
               _pad128(w['gdn_dt_bias'][l])]
    gdn_post = [_row(jnp.tile(w['gdn_norm'][l], HEADS))]
    return dict(lru_pre=lru_pre, lru_post=[_row(w['lru_out_norm'][l])], rw_pre=rw_pre, rw_post=rw_post,
                gdn_pre=gdn_pre, gdn_post=gdn_post)


def _mixer_fwd(p, sh, l, v_first, host_rwkv=(), host_gdn=()):
    T = p.shape[0]
    lx, ly = p[:, 0:256], p[:, 256:512]
    prw, qkv, z, ab = p[:, 512:1920], p[:, 1920:3072], p[:, 3072:3456], p[:, 3456:3584]
    first = (lax.broadcasted_iota(jnp.int32, (T, LRU_W), 0) == 0).astype(F32)
    lru_rows = [lx, shift_rows(lx, 1), shift_rows(lx, 2), shift_rows(lx, 3), first]
    a, b = rowwise_fwd(lru_pre_fn, lru_rows, sh['lru_pre'], (LRU_W, LRU_W), f"lru_pre_fwd{l}")
    hseq = lru_scan(a, b, False, f"lru_scan_fwd{l}")
    (y_lru,) = rowwise_fwd(lru_post_fn, [hseq, ly], sh['lru_post'], (LRU_W,), f"lru_post_fwd{l}")

    rw_rows = [prw, shift_rows(prw, 1)] + ([v_first] if l > 0 else [])
    rw = rowwise_fwd(make_rwkv_pre_fn(l > 0), rw_rows, sh['rw_pre'], (MIX_W,) * 7, f"rwkv_pre_fwd{l}")
    r, lw, k2, v, kk, ar, g = rw
    y_raw, rs0, rt, got_rwkv = core_fwd(rwkv_heads, [r, lw, k2, v, kk, ar], f"rwkv_core_fwd{l}", host_rwkv)
    (y_rw,) = rowwise_fwd(rwkv_post_fn, [y_raw, r, k2, v, g], sh['rw_post'], (MIX_W,), f"rwkv_post_fwd{l}")

    gdn_rows = [qkv, shift_rows(qkv, 1), shift_rows(qkv, 2), shift_rows(qkv, 3), ab]
    gd = rowwise_fwd(gdn_pre_fn, gdn_rows, sh['gdn_pre'], (MIX_W,) * 5, f"gdn_pre_fwd{l}")
    o_raw, gs0, gt, got_gdn = core_fwd(gdn_heads, list(gd), f"gdn_core_fwd{l}", host_gdn)
    (y_gdn,) = rowwise_fwd(gdn_post_fn, [o_raw, z], sh['gdn_post'], (MIX_W,), f"gdn_post_fwd{l}")

    mixed = jnp.concatenate([y_lru, y_rw, y_gdn], axis=1)
    saved = dict(lru_rows=lru_rows, a=a, hseq=hseq, ly=ly, rw_rows=rw_rows, rw=rw, y_raw=y_raw, rs0=rs0, rt=rt,
                 gdn_rows=gdn_rows, gd=gd, o_raw=o_raw, gs0=gs0, gt=gt, z=z)
    v_layer0 = v if l == 0 else None
    return mixed, saved, v_layer0, got_rwkv, got_gdn


def _mixer_bwd(dmixed, sv, sh, l, dv_first):
    d_lru, d_rw, d_gdn = dmixed[:, 0:256], dmixed[:, 256:640], dmixed[:, 640:1024]
    gw = {}

    dh, dly, d_og = rowwise_bwd(lru_post_fn, [sv['hseq'], sv['ly']], sh['lru_post'], [d_lru], f"lru_post_bwd{l}")
    gscan = lru_scan(unshift_rows(sv['a'], 1), dh, True, f"lru_scan_bwd{l}")
    res = rowwise_bwd(lru_pre_fn, sv['lru_rows'], sh['lru_pre'], [gscan, shift_rows(sv['hseq'], 1)],
                      f"lru_pre_bwd{l}", ct_fn=lambda gs, hp: (gs * hp, gs))
    dlx = res[0] + unshift_rows(res[1], 1) + unshift_rows(res[2], 2) + unshift_rows(res[3], 3)
    dw0, dw1, dw2, dw3, dcb, dga, dgab, dgx, dgxb, dlam = res[5:]
    gw['lru_conv_w'] = jnp.concatenate([dw0, dw1, dw2, dw3], axis=0)
    gw['lru_conv_b'] = dcb[0]
    gw['lru_gate_a_w'] = _block_diag_grad(dga)
    gw['lru_gate_a_b'] = dgab.reshape(LRU_BLOCKS, 64)
    gw['lru_gate_x_w'] = _block_diag_grad(dgx)
    gw['lru_gate_x_b'] = dgxb.reshape(LRU_BLOCKS, 64)
    gw['lru_lambda'] = dlam[0]
    gw['lru_out_norm'] = d_og[0]

    r, lw, k2, v, kk, ar, g = sv['rw']
    res = rowwise_bwd(rwkv_post_fn, [sv['y_raw'], r, k2, v, g], sh['rw_post'], [d_rw], f"rwkv_post_bwd{l}")
    dy_raw, dr_p, dk2_p, dv_p, dg = res[:5]
    gw['rwkv_ln_g'], gw['rwkv_ln_b'], gw['rwkv_r_k'] = res[5][0], res[6][0], res[7].reshape(HEADS, HEAD_DIM)
    dr_c, dlw, dk2_c, dv_c, dkk, dar = core_bwd(rwkv_heads, [r, lw, k2, v, kk, ar], sv['rs0'], sv['rt'], dy_raw,
                                                 f"rwkv_core_bwd{l}")
    cts = [dr_p, dr_c, dlw, dk2_p, dk2_c, dv_p, dv_c, dkk, dar, dg]
    if l == 0:
        cts.append(dv_first)
        ct_fn = lambda a1, a2, b, c1, c2, d1, d2, e, f, gg, vf: (a1 + a2, b, c1 + c2, d1 + d2 + vf, e, f, gg)
    else:
        ct_fn = lambda a1, a2, b, c1, c2, d1, d2, e, f, gg: (a1 + a2, b, c1 + c2, d1 + d2, e, f, gg)
    res = rowwise_bwd(make_rwkv_pre_fn(l > 0), sv['rw_rows'], sh['rw_pre'], cts, f"rwkv_pre_bwd{l}", ct_fn=ct_fn)
    dprw = res[0] + unshift_rows(res[1], 1)
    nrow = len(sv['rw_rows'])
    dv_first_out = res[2] if l > 0 else None
    sg = res[nrow:]
    gw['rwkv_mu'], gw['rwkv_w_up'], gw['rwkv_w_bias'], gw['rwkv_a_up'] = sg[0][0], sg[1], sg[2][0], sg[3]
    gw['rwkv_a_bias'], gw['rwkv_g_up'], gw['rwkv_k_k'], gw['rwkv_k_a'] = sg[4][0], sg[5], sg[6][0], sg[7][0]
    if l > 0:
        gw['rwkv_vres_w1'], gw['rwkv_vres_w2'], gw['rwkv_vres_b'] = sg[8], sg[9], sg[10][0]

    do_raw, dz, d_ng = rowwise_bwd(gdn_post_fn, [sv['o_raw'], sv['z']], sh['gdn_post'], [d_gdn], f"gdn_post_bwd{l}")
    gw['gdn_norm'] = jnp.sum(d_ng.reshape(HEADS, HEAD_DIM), axis=0)
    dgd = core_bwd(gdn_heads, list(sv['gd']), sv['gs0'], sv['gt'], do_raw, f"gdn_core_bwd{l}")
    res = rowwise_bwd(gdn_pre_fn, sv['gdn_rows'], sh['gdn_pre'], list(dgd), f"gdn_pre_bwd{l}")
    dqkv = res[0] + unshift_rows(res[1], 1) + unshift_rows(res[2], 2) + unshift_rows(res[3], 3)
    dab = res[4]
    gw['gdn_conv_w'] = jnp.concatenate(res[5:9], axis=0)
    gw['gdn_a_log'], gw['gdn_dt_bias'] = res[9][0, :HEADS], res[10][0, :HEADS]

    dp = jnp.concatenate([dlx, dly, dprw, dqkv, dz, dab], axis=1)
    return dp, gw, dv_first_out


IN_SHARD = D_IN // N_CHIPS
IN_SHARD_PAD = D_IN_PAD // N_CHIPS


def _cols_to_chips(g, n=N_CHIPS):
    r = g.shape[0]
    return jnp.transpose(g.reshape(r, n, -1), (1, 0, 2))


def _cols_from_chips(g):
    return jnp.transpose(g, (1, 0, 2)).reshape(g.shape[1], -1)


def _w_in_from_chips(g):
    nat = _cols_from_chips(g[:, :, :IN_SHARD])
    return jnp.pad(nat, ((0, 0), (0, D_IN_PAD - D_IN)))


def _w_in_to_chips(g):
    return jnp.pad(_cols_to_chips(g[:, :D_IN]), ((0, 0), (0, 0), (0, IN_SHARD_PAD - IN_SHARD)))


def _natural(name, g):
    if name == 'w_in':
        return _w_in_from_chips(g)
    if BIG[name] == 2:
        return _cols_from_chips(g)
    return g.reshape(-1, g.shape[2])


def local_step(x, target, w, wb, shards=None):
    def hosted(keys):
        return [shards[k] for k in keys] if shards is not None else []

    def arrived(keys, gathered):
        for (name, layer), g in zip(keys if shards is not None else [], gathered):
            wb[name][layer] = _natural(name, g)

    saved = []
    v_first = None
    for l in range(N_LAYERS):
        sh = _layer_shared(w, l)
        for_mixer = [('w_in', l), ('w_out', l)]
        for_ffn2 = [('ffn2_wi', l), ('ffn2_wo', l)]
        for_next = [('ffn1_wi', l + 1), ('ffn1_wo', l + 1)] if l + 1 < N_LAYERS else []
        x1, got = ffn_fwd(x, _row(w['ffn1_norm'][l]), wb['ffn1_wi'][l], wb['ffn1_wo'][l], f"ffn1_fwd{l}",
                          hosted(for_mixer))
        arrived(for_mixer, got)
        p = proj_fwd(x1, _row(w['mix_norm'][l]), wb['w_in'][l], f"proj_fwd{l}")
        mixed, sv, v0, got_ffn2, got_next = _mixer_fwd(p, sh, l, v_first, hosted(for_ffn2), hosted(for_next))
        arrived(for_ffn2, got_ffn2)
        arrived(for_next, got_next)
        if l == 0:
            v_first = v0
        x2 = out_fwd(mixed, wb['w_out'][l], x1, f"out_fwd{l}")
        x3, _ = ffn_fwd(x2, _row(w['ffn2_norm'][l]), wb['ffn2_wi'][l], wb['ffn2_wo'][l], f"ffn2_fwd{l}")
        saved.append(dict(x0=x, x1=x1, x2=x2, mixed=mixed, sv=sv, sh=sh))
        x = x3

    loss, dx, dgf = loss_head(x, _row(w['final_norm']), target, "loss_head")
    per_layer = [dict() for _ in range(N_LAYERS)]
    dv_first = jnp.zeros((x.shape[0], MIX_W), F32)

    waiting, chip_sums, arrived_parts = [], {}, {}

    def reduce_now(keys, tag):
        if shards is None:
            return
        sums = chip_sums_of([(n, k, per_layer[k][n]) for n, k in keys], lax.axis_index("c"), tag)
        for key, (total, total_bf) in zip(keys, sums):
            chip_sums[key] = total
            waiting.append((key, total_bf))

    def take_waiting():
        keys, parts = [k for k, _ in waiting], [p for _, p in waiting]
        waiting.clear()
        return keys, parts

    for l in reversed(range(N_LAYERS)):
        s = saved[l]
        gw = per_layer[l]
        keys, parts = take_waiting()
        dx, dg2, dwg, dwu, dwo, got = ffn_bwd(s['x2'], dx, _row(w['ffn2_norm'][l]), wb['ffn2_wi'][l],
                                              wb['ffn2_wo'][l], f"ffn2_bwd{l}", parts)
        arrived_parts.update(zip(keys, got))
        wi_parts = lambda dwg, dwu: (dwg, dwu)
        row_parts = lambda dw: dw.reshape(N_CHIPS, -1, dw.shape[1])
        gw['ffn2_norm'], gw['ffn2_wi'], gw['ffn2_wo'] = dg2[0], wi_parts(dwg, dwu), row_parts(dwo)
        if l == N_LAYERS - 1:
            reduce_now([('ffn2_wi', l), ('ffn2_wo', l)], f"ffn2_{l}")
        dmixed, dw_out = out_bwd(s['mixed'], wb['w_out'][l], dx, f"out_bwd{l}")
        gw['w_out'] = row_parts(dw_out)
        dp, gmix, dvf = _mixer_bwd(dmixed, s['sv'], s['sh'], l, dv_first)
        if l > 0:
            dv_first = dvf
        gw.update(gmix)
        dx, dgm, dwin = proj_bwd(s['x1'], dx, _row(w['mix_norm'][l]), wb['w_in'][l], dp, f"proj_bwd{l}")
        gw['mix_norm'], gw['w_in'] = dgm[0], _w_in_to_chips(dwin)
        if l < N_LAYERS - 1:
            reduce_now([('ffn2_wi', l), ('ffn2_wo', l), ('w_in', l), ('w_out', l)], f"mix_{l}")
        keys, parts = take_waiting()
        dx, dg1, dwg, dwu, dwo, got = ffn_bwd(s['x0'], dx, _row(w['ffn1_norm'][l]), wb['ffn1_wi'][l],
                                              wb['ffn1_wo'][l], f"ffn1_bwd{l}", parts)
        arrived_parts.update(zip(keys, got))
        gw['ffn1_norm'], gw['ffn1_wi'], gw['ffn1_wo'] = dg1[0], wi_parts(dwg, dwu), row_parts(dwo)
        if l == N_LAYERS - 1:
            reduce_now([('w_in', l), ('w_out', l), ('ffn1_wi', l), ('ffn1_wo', l)], f"ffn1_{l}")
        else:
            reduce_now([('ffn1_wi', l), ('ffn1_wo', l)], f"ffn1_{l}")
    grads = {'final_norm': dgf[0]}
    if shards is not None:
        grads['last_round'] = take_waiting()
        arrived_parts.update({key: None for key in grads['last_round'][0]})
    for name in WEIGHTS:
        if name == 'final_norm':
            continue
        if name in BIG:
            if shards is None:
                grads[name] = [per_layer[l][name] for l in range(N_LAYERS)]
            else:
                grads[name] = [(chip_sums[(name, l)], arrived_parts[(name, l)]) for l in range(N_LAYERS)]
        elif name.startswith('rwkv_vres'):
            grads[name] = per_layer[1][name][None]
        else:
            grads[name] = jnp.stack([per_layer[l][name] for l in range(N_LAYERS)])
    return loss[0, 0], dx, grads


ANY = pl.BlockSpec(memory_space=pl.ANY)


def _coords():
    return lax.axis_index("x"), lax.axis_index("y"), lax.axis_index("c")


def _other_chips(x, y):
    return [((x + 1) % 2, y), (x, (y + 1) % 2), ((x + 1) % 2, (y + 1) % 2)]


def allreduce_small(pack, name):
    R = pack.shape[0]
    rh = R // 2

    def body(x_ref, o_ref, sib_ref, chip_ref, parts_ref, send_sems, recv_sems):
        x, y, c = _coords()
        sib = (x, y, 1 - c)

        def copy(k, src, dst, to):
            return pltpu.make_async_remote_copy(src_ref=src, dst_ref=dst, send_sem=send_sems.at[k],
                                                recv_sem=recv_sems.at[k], device_id=to, device_id_type=MESH)

        swap = copy(0, x_ref, sib_ref, sib)
        swap.start()
        swap.wait()
        chip_ref[...] = jnp.where(c == 0, x_ref[...], sib_ref[...]) + jnp.where(c == 0, sib_ref[...], x_ref[...])

        mine = pl.ds(pl.multiple_of(c * rh, 8), rh)
        sends = [copy(1 + j, chip_ref.at[mine], parts_ref.at[j], (px, py, c))
                 for j, (px, py) in enumerate(_other_chips(x, y))]
        for cp in sends:
            cp.start()
        for cp in sends:
            cp.wait()
        s = 2 * x + y
        own = chip_ref[mine, :]
        from_chip = {2: parts_ref[0], 1: parts_ref[1], 3: parts_ref[2]}
        terms = []
        for k in range(N_CHIPS):
            t = own
            for d, part in from_chip.items():
                t = jnp.where(jnp.bitwise_xor(s, d) == k, part, t)
            terms.append(t)
        o_ref[mine, :] = ((terms[0] + terms[1]) + terms[2]) + terms[3]

        share = copy(4, o_ref.at[mine], o_ref.at[mine], sib)
        share.start()
        share.wait()

    vm = pl.BlockSpec(memory_space=pltpu.VMEM)
    return pl.pallas_call(
        body, name=name, in_specs=[vm], out_specs=vm, out_shape=jax.ShapeDtypeStruct((R, 128), F32),
        scratch_shapes=[pltpu.VMEM((R, 128), F32), pltpu.VMEM((R, 128), F32), pltpu.VMEM((3, rh, 128), F32),
                        pltpu.SemaphoreType.DMA((5,)), pltpu.SemaphoreType.DMA((5,))],
        compiler_params=_params())(pack)


class ChipGather:
    def __init__(self, shards):
        self.shapes = [s.shape for s in shards]
        self.n = len(shards)
        self.in_specs = [ANY] * self.n
        self.out_specs = [ANY] * self.n
        self.out_shape = [jax.ShapeDtypeStruct((N_CHIPS,) + s.shape, s.dtype) for s in shards]
        self.scratch = [pltpu.SemaphoreType.DMA((6 * self.n,)), pltpu.SemaphoreType.DMA((6 * self.n,)),
                        pltpu.SemaphoreType.DMA((self.n,))] if self.n else []

    def _rows(self, a, core):
        rh = self.shapes[a][0] // 2
        return pl.ds(pl.multiple_of(core * rh, 16), rh)

    def _copies(self, kind, x_refs, o_refs, sems):
        send_sems, recv_sems, local_sems = sems
        x, y, c = _coords()
        s_me = 2 * x + y
        sib = (x, y, 1 - c)

        def copy(a, k, src, dst, to):
            return pltpu.make_async_remote_copy(src_ref=src, dst_ref=dst, send_sem=send_sems.at[6 * a + k],
                                                recv_sem=recv_sems.at[6 * a + k], device_id=to, device_id_type=MESH)

        if kind == 'own':
            return [pltpu.make_async_copy(x_refs[a], o_refs[a].at[s_me], local_sems.at[a]) for a in range(self.n)]
        out = []
        for j, (px, py) in enumerate(_other_chips(x, y)):
            for a in range(self.n):
                mine = self._rows(a, c)
                part = o_refs[a].at[2 * px + py, mine]
                if kind == 'sent':
                    out.append(copy(a, j, x_refs[a].at[mine], o_refs[a].at[s_me, mine], (px, py, c)))
                elif kind == 'arrived':
                    out.append(copy(a, j, part, part, (px, py, c)))
                elif kind == 'passed':
                    out.append(copy(a, 3 + j, part, part, sib))
                else:
                    theirs = o_refs[a].at[2 * px + py, self._rows(a, 1 - c)]
                    out.append(copy(a, 3 + j, theirs, theirs, sib))
        return out

    def start(self, x_refs, o_refs, sems):
        if not self.n:
            return
        for cp in self._copies('own', x_refs, o_refs, sems) + self._copies('sent', x_refs, o_refs, sems):
            cp.start()

    def relay(self, x_refs, o_refs, sems):
        if not self.n:
            return
        for got, fw in zip(self._copies('arrived', x_refs, o_refs, sems),
                           self._copies('passed', x_refs, o_refs, sems)):
            got.wait_recv()
            fw.start()

    def finish(self, x_refs, o_refs, sems):
        if not self.n:
            return
        for cp in self._copies('from_sibling', x_refs, o_refs, sems):
            cp.wait_recv()
        for cp in self._copies('sent', x_refs, o_refs, sems) + self._copies('passed', x_refs, o_refs, sems):
            cp.wait_send()
        for cp in self._copies('own', x_refs, o_refs, sems):
            cp.wait()


def allgather_chips(shards, name):
    gather = ChipGather(shards)
    n = gather.n

    def body(*refs):
        x_refs, o_refs, sems = refs[:n], refs[n:2 * n], refs[2 * n:]
        gather.start(x_refs, o_refs, sems)
        gather.relay(x_refs, o_refs, sems)
        gather.finish(x_refs, o_refs, sems)

    return pl.pallas_call(
        body, name=name, in_specs=gather.in_specs, out_specs=tuple(gather.out_specs),
        out_shape=tuple(gather.out_shape), scratch_shapes=gather.scratch, compiler_params=_params())(*shards)


def sibling_swap(srcs, halves, name):
    n = len(srcs)
    row_axis = [s.ndim - 2 for s in srcs]
    out_shapes = [s.shape[:ax] + (s.shape[ax] // 2,) + s.shape[ax + 1:] if halves else s.shape
                  for s, ax in zip(srcs, row_axis)]

    def body(*refs):
        x_refs, o_refs = refs[:n], refs[n:2 * n]
        send_sems, recv_sems = refs[2 * n:]
        x, y, c = _coords()
        copies = []
        for a in range(n):
            part = x_refs[a]
            if halves:
                rh = srcs[a].shape[row_axis[a]] // 2
                theirs = pl.ds(pl.multiple_of((1 - c) * rh, 16), rh)
                part = part.at[:, theirs] if row_axis[a] == 1 else part.at[theirs]
            cp = pltpu.make_async_remote_copy(src_ref=part, dst_ref=o_refs[a], send_sem=send_sems.at[a],
                                              recv_sem=recv_sems.at[a], device_id=(x, y, 1 - c), device_id_type=MESH)
            cp.start()
            copies.append(cp)
        for cp in copies:
            cp.wait()

    return pl.pallas_call(
        body, name=name, in_specs=[ANY] * n, out_specs=tuple([ANY] * n),
        out_shape=tuple(jax.ShapeDtypeStruct(sh, s.dtype) for sh, s in zip(out_shapes, srcs)),
        scratch_shapes=[pltpu.SemaphoreType.DMA((n,)), pltpu.SemaphoreType.DMA((n,))],
        compiler_params=_params())(*srcs)


class ChipScatter:
    def __init__(self, parts):
        self.n = len(parts)
        self.in_specs = [ANY] * self.n
        self.out_specs = [ANY] * self.n
        self.out_shape = [jax.ShapeDtypeStruct((3,) + p.shape[1:], p.dtype) for p in parts]
        self.scratch = [pltpu.SemaphoreType.DMA((3 * self.n,)), pltpu.SemaphoreType.DMA((3 * self.n,))] if self.n else []

    def _copies(self, x_refs, o_refs, sems):
        send_sems, recv_sems = sems
        x, y, c = _coords()
        return [pltpu.make_async_remote_copy(src_ref=x_refs[a].at[2 * px + py], dst_ref=o_refs[a].at[j],
                                             send_sem=send_sems.at[3 * a + j], recv_sem=recv_sems.at[3 * a + j],
                                             device_id=(px, py, c), device_id_type=MESH)
                for j, (px, py) in enumerate(_other_chips(x, y)) for a in range(self.n)]

    def start(self, x_refs, o_refs, sems):
        if self.n:
            for cp in self._copies(x_refs, o_refs, sems):
                cp.start()

    def finish(self, x_refs, o_refs, sems):
        if self.n:
            for cp in self._copies(x_refs, o_refs, sems):
                cp.wait()


def scatter_chips(parts, name):
    scatter = ChipScatter(parts)
    n = scatter.n

    def body(*refs):
        x_refs, o_refs, sems = refs[:n], refs[n:2 * n], refs[2 * n:]
        scatter.start(x_refs, o_refs, sems)
        scatter.finish(x_refs, o_refs, sems)

    return pl.pallas_call(
        body, name=name, in_specs=scatter.in_specs, out_specs=tuple(scatter.out_specs),
        out_shape=tuple(scatter.out_shape), scratch_shapes=scatter.scratch, compiler_params=_params())(*parts)


HBM = pl.BlockSpec(memory_space=pltpu.HBM)
SEM = pl.BlockSpec(memory_space=pltpu.SEMAPHORE)
SIDE_EFFECT = pltpu.SideEffectType.DATAFLOW_SIDE_EFFECTING


def _scatter_copies(x_refs, land_refs, send_sems, recv_sems):
    x, y, c = _coords()
    n = len(x_refs)
    return [pltpu.make_async_remote_copy(src_ref=x_refs[a].at[2 * px + py], dst_ref=land_refs[a].at[j],
                                         send_sem=send_sems[3 * a + j], recv_sem=recv_sems[3 * a + j],
                                         device_id=(px, py, c), device_id_type=MESH)
            for j, (px, py) in enumerate(_other_chips(x, y)) for a in range(n)]


def scatter_start(parts, name):
    n = len(parts)
    k = 3 * n
    lands = [lax.empty((3,) + p.shape[1:], p.dtype) for p in parts]

    def body(*refs):
        x_refs, land_refs = refs[:n], refs[n:2 * n]
        send_sems, recv_sems = refs[2 * n:2 * n + k], refs[2 * n + k:2 * n + 2 * k]
        token = refs[-1]
        for cp in _scatter_copies(x_refs, land_refs, send_sems, recv_sems):
            cp.start()
        token[...] = jnp.zeros_like(token)

    hbm = lambda a: pltpu.HBM(a.shape, a.dtype)
    res = pl.pallas_call(
        body, name=name, in_specs=[HBM] * (2 * n),
        out_specs=tuple([SEM] * (2 * k) + [HBM] * (2 * n) + [pl.BlockSpec(memory_space=pltpu.VMEM)]),
        out_shape=tuple([pltpu.SemaphoreType.DMA(())] * (2 * k) + [hbm(p) for p in parts] + [hbm(b) for b in lands]
                        + [jax.ShapeDtypeStruct((8, 128), F32)]),
        input_output_aliases={i: 2 * k + i for i in range(2 * n)},
        compiler_params=pltpu.CompilerParams(has_side_effects=SIDE_EFFECT, vmem_limit_bytes=VMEM_LIMIT))(
            *[pltpu.with_memory_space_constraint(a, pltpu.HBM) for a in list(parts) + lands])
    return list(res[:2 * k]), list(res[2 * k:2 * k + n]), list(res[2 * k + n:2 * k + 2 * n]), res[-1]


def scatter_wait(sems, parts_thru, lands_thru, after, name):
    n = len(parts_thru)
    k = 3 * n

    def body(*refs):
        x_refs, land_refs = refs[:n], refs[n:2 * n]
        send_sems, recv_sems = refs[2 * n:2 * n + k], refs[2 * n + k:2 * n + 2 * k]
        for cp in _scatter_copies(x_refs, land_refs, send_sems, recv_sems):
            cp.wait_send()
            cp.wait_recv()

    hbm = lambda a: pltpu.HBM(a.shape, a.dtype)
    res = pl.pallas_call(
        body, name=name, in_specs=[HBM] * (2 * n) + [SEM] * (2 * k) + [ANY],
        out_specs=tuple([HBM] * (2 * n)), out_shape=tuple(hbm(a) for a in list(parts_thru) + list(lands_thru)),
        input_output_aliases={i: i for i in range(2 * n)},
        compiler_params=pltpu.CompilerParams(has_side_effects=SIDE_EFFECT, vmem_limit_bytes=VMEM_LIMIT))(
            *parts_thru, *lands_thru, *sems, after)
    return list(res[n:])


def _row_block(rows):
    return max(b for b in range(16, 257, 16) if rows % b == 0)


def chip_sum(gpack, recv, core, name):
    n, R, W = gpack.shape
    rh = R // 2
    rb = _row_block(rh)
    nb = rh // rb

    def body(c_ref, g_ref, r_ref, o_ref, ob_ref):
        s = g_ref[...] + r_ref[...]
        o_ref[...] = s
        ob_ref[...] = s.astype(BF16)

    blk = pl.BlockSpec((1, rb, W), lambda i, j, c_ref: (i, j, 0))
    spec = pltpu.PrefetchScalarGridSpec(
        num_scalar_prefetch=1, grid=(n, nb),
        in_specs=[pl.BlockSpec((1, rb, W), lambda i, j, c_ref: (i, c_ref[0] * nb + j, 0)), blk],
        out_specs=(blk, blk))
    return pl.pallas_call(
        body, name=name, grid_spec=spec,
        out_shape=(jax.ShapeDtypeStruct((n, rh, W), F32), jax.ShapeDtypeStruct((n, rh, W), BF16)),
        compiler_params=_params(("arbitrary", "arbitrary")))(core, gpack, recv)


def chip_sum_cols(gate, up, recv_gate, recv_up, core, name):
    R, W = gate.shape
    cw = W // 2
    rh = R // 2
    rb = _row_block(rh)
    nb = rh // rb

    def body(c_ref, g_ref, u_ref, rg_ref, ru_ref, o_ref, ob_ref):
        s = jnp.where(pl.program_id(0) < 2, g_ref[...] + rg_ref[...], u_ref[...] + ru_ref[...])
        o_ref[0] = s
        ob_ref[0] = s.astype(BF16)

    gate_col = lambda s: jnp.minimum(s, 1)
    up_col = lambda s: jnp.maximum(s - 2, 0)
    out = pl.BlockSpec((1, rb, cw), lambda s, j, c_ref: (s, j, 0))
    spec = pltpu.PrefetchScalarGridSpec(
        num_scalar_prefetch=1, grid=(N_CHIPS, nb),
        in_specs=[pl.BlockSpec((rb, cw), lambda s, j, c_ref: (c_ref[0] * nb + j, gate_col(s))),
                  pl.BlockSpec((rb, cw), lambda s, j, c_ref: (c_ref[0] * nb + j, up_col(s))),
                  pl.BlockSpec((rb, cw), lambda s, j, c_ref: (j, gate_col(s))),
                  pl.BlockSpec((rb, cw), lambda s, j, c_ref: (j, up_col(s)))],
        out_specs=(out, out))
    return pl.pallas_call(
        body, name=name, grid_spec=spec,
        out_shape=(jax.ShapeDtypeStruct((N_CHIPS, rh, cw), F32), jax.ShapeDtypeStruct((N_CHIPS, rh, cw), BF16)),
        compiler_params=_params(("arbitrary", "arbitrary")))(core, gate, up, recv_gate, recv_up)


def chip_sums_of(items, core, tag):
    parts = []
    for _, _, g in items:
        parts += list(g) if isinstance(g, tuple) else [g]
    swapped = iter(zip(parts, sibling_swap(parts, True, f"grad_swap_cores_{tag}")))
    core_arg = core.reshape(1).astype(jnp.int32)
    sums = []
    for n, l, g in items:
        if isinstance(g, tuple):
            (dwg, from_g), (dwu, from_u) = next(swapped), next(swapped)
            sums.append(chip_sum_cols(dwg, dwu, from_g, from_u, core_arg, f"grad_chip_sum_{n}{l}"))
        else:
            p, r = next(swapped)
            sums.append(chip_sum(p, r, core_arg, f"grad_chip_sum_{n}{l}"))
    return sums


def shard_sum(own, recv, name):
    R, W = own.shape
    rb = _row_block(R)

    def body(a_ref, r_ref, o_ref):
        acc = a_ref[...]
        for j in range(3):
            acc = acc + r_ref[j].astype(F32)
        o_ref[...] = acc

    return pl.pallas_call(
        body, name=name, grid=(R // rb,),
        in_specs=[pl.BlockSpec((rb, W), lambda i: (i, 0)), pl.BlockSpec((3, rb, W), lambda i: (0, i, 0))],
        out_specs=pl.BlockSpec((rb, W), lambda i: (i, 0)), out_shape=jax.ShapeDtypeStruct((R, W), F32),
        compiler_params=_params(("arbitrary",)))(own, recv)


def adamw(w, m, v, g, name):
    L, R, C = w.shape
    rb = max(b for b in range(8, 257, 8) if R % b == 0)
    bc1 = 1.0 - ADAM_B1 ** ADAM_STEP
    bc2 = 1.0 - ADAM_B2 ** ADAM_STEP

    def body(w_ref, m_ref, v_ref, g_ref, d_ref, nm_ref, nv_ref):
        gv = g_ref[...]
        nm = ADAM_B1 * m_ref[...] + (1.0 - ADAM_B1) * gv
        nv = ADAM_B2 * v_ref[...] + (1.0 - ADAM_B2) * (gv * gv)
        d_ref[...] = -ADAM_LR * ((nm / bc1) / (jnp.sqrt(nv / bc2) + ADAM_EPS) + ADAM_WD * w_ref[...])
        nm_ref[...] = nm
        nv_ref[...] = nv

    blk = pl.BlockSpec((1, rb, C), lambda l, i: (l, i, 0))
    sh = jax.ShapeDtypeStruct((L, R, C), F32)
    return pl.pallas_call(body, name=name, grid=(L, R // rb), in_specs=[blk] * 4, out_specs=(blk,) * 3,
                          out_shape=(sh, sh, sh), compiler_params=_params(("arbitrary", "arbitrary")))(w, m, v, g)


SMALL = [n for n in WEIGHTS if n not in BIG]


PACK_TILE = 8 * 128


def _pack(arrays):
    blocks = []
    for a in arrays:
        flat = a.reshape(-1)
        flat = jnp.pad(flat, (0, -flat.size % PACK_TILE))
        blocks.append(flat.reshape(-1, 128))
    rows = sum(b.shape[0] for b in blocks)
    if rows % 16:
        blocks.append(jnp.zeros((8, 128), arrays[0].dtype))
    return jnp.concatenate(blocks, axis=0)


def _unpack(pack, shapes):
    out, row = [], 0
    for shape in shapes:
        size = int(np.prod(shape))
        rows = -(-size // PACK_TILE) * 8
        out.append(pack[row:row + rows].reshape(-1)[:size].reshape(shape))
        row += rows
    return out


def _pad_lanes(a):
    return jnp.pad(a, ((0, 0), (0, -a.shape[1] % 128)))


def _local_shard(full, axis, chip):
    size = full.shape[axis] // N_CHIPS
    return lax.dynamic_slice_in_dim(full, chip * size, size, axis)


def kernel(x, ffn1_norm, ffn1_wi, ffn1_wo, mix_norm, w_in, w_out, lru_conv_w, lru_conv_b, lru_gate_a_w, lru_gate_a_b, lru_gate_x_w, lru_gate_x_b, lru_lambda, lru_out_norm, rwkv_mu, rwkv_w_up, rwkv_w_bias, rwkv_a_up, rwkv_a_bias, rwkv_g_up, rwkv_k_k, rwkv_k_a, rwkv_r_k, rwkv_ln_g, rwkv_ln_b, rwkv_vres_w1, rwkv_vres_w2, rwkv_vres_b, gdn_conv_w, gdn_a_log, gdn_dt_bias, gdn_norm, ffn2_norm, ffn2_wi, ffn2_wo, final_norm, loss_target, m_ffn1_norm, m_ffn1_wi, m_ffn1_wo, m_mix_norm, m_w_in, m_w_out, m_lru_conv_w, m_lru_conv_b, m_lru_gate_a_w, m_lru_gate_a_b, m_lru_gate_x_w, m_lru_gate_x_b, m_lru_lambda, m_lru_out_norm, m_rwkv_mu, m_rwkv_w_up, m_rwkv_w_bias, m_rwkv_a_up, m_rwkv_a_bias, m_rwkv_g_up, m_rwkv_k_k, m_rwkv_k_a, m_rwkv_r_k, m_rwkv_ln_g, m_rwkv_ln_b, m_rwkv_vres_w1, m_rwkv_vres_w2, m_rwkv_vres_b, m_gdn_conv_w, m_gdn_a_log, m_gdn_dt_bias, m_gdn_norm, m_ffn2_norm, m_ffn2_wi, m_ffn2_wo, m_final_norm, v_ffn1_norm, v_ffn1_wi, v_ffn1_wo, v_mix_norm, v_w_in, v_w_out, v_lru_conv_w, v_lru_conv_b, v_lru_gate_a_w, v_lru_gate_a_b, v_lru_gate_x_w, v_lru_gate_x_b, v_lru_lambda, v_lru_out_norm, v_rwkv_mu, v_rwkv_w_up, v_rwkv_w_bias, v_rwkv_a_up, v_rwkv_a_bias, v_rwkv_g_up, v_rwkv_k_k, v_rwkv_k_a, v_rwkv_r_k, v_rwkv_ln_g, v_rwkv_ln_b, v_rwkv_vres_w1, v_rwkv_vres_w2, v_rwkv_vres_b, v_gdn_conv_w, v_gdn_a_log, v_gdn_dt_bias, v_gdn_norm, v_ffn2_norm, v_ffn2_wi, v_ffn2_wo, v_final_norm):
    args = locals()
    w_loc = {n: args[n] for n in WEIGHTS}
    m_loc = {n: args['m_' + n] for n in WEIGHTS}
    v_loc = {n: args['v_' + n] for n in WEIGHTS}
    chip = 2 * lax.axis_index("x") + lax.axis_index("y")
    core = lax.axis_index("c")

    big = [(n, l) for n in BIG for l in range(N_LAYERS)]
    shards = {(n, l): _pad_lanes(w_loc[n][l].astype(BF16)) for n, l in big}
    first = [('ffn1_wi', 0), ('ffn1_wo', 0)]
    wb = {n: [None] * N_LAYERS for n in BIG}
    for (n, l), g in zip(first, allgather_chips([shards[k] for k in first], "allgather_first")):
        wb[n][l] = _natural(n, g)

    sm_names = list(SMALL_SHARDED)
    placed = []
    for n in sm_names:
        mine = [jnp.where((chip == s) & (core == 0), w_loc[n], 0.0) for s in range(N_CHIPS)]
        placed.append(jnp.concatenate(mine, axis=SMALL_SHARDED[n]))
    summed = allreduce_small(_pack(placed), "allgather_small")
    w_full = dict(w_loc)
    w_full.update(zip(sm_names, _unpack(summed, [p.shape for p in placed])))

    loss, dx, grads = local_step(x[0], loss_target[0], w_full, wb, shards)
    loss = lax.psum(loss, ("x", "y", "c"))

    gsum = allreduce_small(_pack([grads[n] for n in SMALL]), "allreduce_small")
    g_loc = {}
    for n, g in zip(SMALL, _unpack(gsum, [grads[n].shape for n in SMALL])):
        g_loc[n] = _local_shard(g, SMALL_SHARDED[n], chip) if n in SMALL_SHARDED else g

    last_keys, last_parts = grads['last_round']
    sems, parts_thru, lands_thru, token = scatter_start(last_parts, "grad_scatter_last_start")
    chip_after_start = chip + token[0, 0].astype(chip.dtype)
    rows = {n: [None] * N_LAYERS for n in BIG}
    delta, new_m, new_v = {}, {}, {}

    def finish(keys, arrived, which_chip, tag):
        halves = [shard_sum(lax.dynamic_index_in_dim(grads[n][l][0], which_chip, 0, keepdims=False), got,
                            f"grad_shard_sum_{n}{l}") for (n, l), got in zip(keys, arrived)]
        others = sibling_swap(halves, False, f"grad_share_cores_{tag}")
        for (n, l), half, other in zip(keys, halves, others):
            lower = jnp.where(core == 0, half, other)
            upper = jnp.where(core == 0, other, half)
            rows[n][l] = jnp.concatenate([lower, upper], axis=0)[:, :w_loc[n].shape[-1]]
        for n in BIG:
            if n not in delta and all(r is not None for r in rows[n]):
                g_loc[n] = jnp.stack(rows[n])
                delta[n], new_m[n], new_v[n] = adamw(w_loc[n], m_loc[n], v_loc[n], g_loc[n], f"adamw_{n}")

    early = [k for k in big if k not in last_keys]
    finish(early, [grads[n][l][1] for n, l in early], chip_after_start, "early")
    pack = lambda d: _pack([d[n] for n in SMALL])[None]
    res = adamw(pack(w_loc), pack(m_loc), pack(v_loc), pack(g_loc), "adamw_small")
    for dst, r in zip((delta, new_m, new_v), res):
        dst.update(zip(SMALL, _unpack(r[0], [w_loc[n].shape for n in SMALL])))
    arrived_last = scatter_wait(sems, parts_thru, lands_thru, delta['ffn2_wi'], "grad_scatter_last_wait")
    finish(last_keys, arrived_last, chip, "last")

    return (loss, dx[None], *[g_loc[n] for n in WEIGHTS], *[delta[n] for n in WEIGHTS],
            *[new_m[n] for n in WEIGHTS], *[new_v[n] for n in WEIGHTS])
```

```python
import functools

import numpy as np
import jax
import jax.numpy as jnp
from jax import lax
from jax.experimental import pallas as pl
from jax.experimental.pallas import tpu as pltpu

F32 = jnp.float32
BF16 = jnp.bfloat16
MESH = pl.DeviceIdType.MESH

D_MODEL = 1024
D_FF = 2816
N_LAYERS = 2
HEADS = 6
HEAD_DIM = 64
MIX_W = HEADS * HEAD_DIM
LRU_W = 256
LRU_BLOCKS = 4
D_IN = 3468
D_IN_PAD = 3584
NORM_EPS = 1e-6
GN_EPS = 64e-5
LRU_C = 8.0
CHUNK = 64
ROWS = 512
FF_CHUNK = 256
IN_CHUNK = 512
VMEM_LIMIT = 56 * 1024 * 1024

ADAM_LR, ADAM_B1, ADAM_B2, ADAM_EPS, ADAM_WD, ADAM_STEP = 0.001, 0.9, 0.999, 1e-08, 0.01, 10

WEIGHTS = ['ffn1_norm', 'ffn1_wi', 'ffn1_wo', 'mix_norm', 'w_in', 'w_out', 'lru_conv_w', 'lru_conv_b',
           'lru_gate_a_w', 'lru_gate_a_b', 'lru_gate_x_w', 'lru_gate_x_b', 'lru_lambda', 'lru_out_norm',
           'rwkv_mu', 'rwkv_w_up', 'rwkv_w_bias', 'rwkv_a_up', 'rwkv_a_bias', 'rwkv_g_up', 'rwkv_k_k',
           'rwkv_k_a', 'rwkv_r_k', 'rwkv_ln_g', 'rwkv_ln_b', 'rwkv_vres_w1', 'rwkv_vres_w2', 'rwkv_vres_b',
           'gdn_conv_w', 'gdn_a_log', 'gdn_dt_bias', 'gdn_norm', 'ffn2_norm', 'ffn2_wi', 'ffn2_wo', 'final_norm']
BIG = {'ffn1_wi': 2, 'ffn1_wo': 1, 'w_in': 2, 'w_out': 1, 'ffn2_wi': 2, 'ffn2_wo': 1}
SMALL_SHARDED = {'lru_conv_w': 2, 'rwkv_w_up': 2, 'rwkv_a_up': 2, 'rwkv_g_up': 2, 'rwkv_vres_w1': 1,
                 'rwkv_vres_w2': 2, 'gdn_conv_w': 2}
N_CHIPS = 4


def _params(sem=None):
    kw = dict(vmem_limit_bytes=VMEM_LIMIT)
    if sem is not None:
        kw['dimension_semantics'] = sem
    return pltpu.CompilerParams(**kw)


def _bdot(a, b, dims=(((1,), (0,)), ((), ()))):
    return lax.dot_general(a.astype(BF16), b.astype(BF16), dims, preferred_element_type=F32)


def _bdot_nt(a, b):
    return _bdot(a, b, (((1,), (1,)), ((), ())))


def _bdot_tn(a, b):
    return _bdot(a, b, (((0,), (0,)), ((), ())))


_DIMS = {'nn': (((1,), (0,)), ((), ())), 'nt': (((1,), (1,)), ((), ())), 'tn': (((0,), (0,)), ((), ()))}


def _split(a, terms):
    parts = []
    for _ in range(terms - 1):
        hi = a.astype(BF16)
        parts.append(hi)
        a = a - hi.astype(F32)
    parts.append(a.astype(BF16))
    return parts


_BATCH_DIMS = {'nn': (((2,), (1,)), ((0,), (0,))), 'nt': (((2,), (2,)), ((0,), (0,))),
               'tn': (((1,), (1,)), ((0,), (0,)))}


def _dot3(a, b, kind):
    ah, al = _split(a, 2)
    bh, bl = _split(b, 2)
    dims = _BATCH_DIMS[kind] if a.ndim == 3 else _DIMS[kind]
    d = lambda p, q: lax.dot_general(p, q, dims, preferred_element_type=F32)
    return d(ah, bh) + (d(ah, bl) + d(al, bh))


@functools.partial(jax.custom_vjp, nondiff_argnums=(2,))
def _cdot_k(a, b, kind):
    return _dot3(a, b, kind)


def _cdot_k_fwd(a, b, kind):
    return _dot3(a, b, kind), (a, b)


def _cdot_k_bwd(kind, res, ct):
    a, b = res
    if kind == 'nn':
        return _dot3(ct, b, 'nt'), _dot3(a, ct, 'tn')
    if kind == 'nt':
        return _dot3(ct, b, 'nn'), _dot3(ct, a, 'tn')
    return _dot3(b, ct, 'nt'), _dot3(a, ct, 'nn')


_cdot_k.defvjp(_cdot_k_fwd, _cdot_k_bwd)


def _dot1(a, b, kind):
    dims = _BATCH_DIMS[kind] if a.ndim == 3 else _DIMS[kind]
    return lax.dot_general(a.astype(BF16), b.astype(BF16), dims, preferred_element_type=F32)


@functools.partial(jax.custom_vjp, nondiff_argnums=(2,))
def _cdot1_k(a, b, kind):
    return _dot1(a, b, kind)


def _cdot1_k_fwd(a, b, kind):
    return _dot1(a, b, kind), (a, b)


def _cdot1_k_bwd(kind, res, ct):
    a, b = res
    if kind == 'nn':
        return _dot1(ct, b, 'nt'), _dot1(a, ct, 'tn')
    if kind == 'nt':
        return _dot1(ct, b, 'nn'), _dot1(ct, a, 'tn')
    return _dot1(b, ct, 'nt'), _dot1(a, ct, 'nn')


_cdot1_k.defvjp(_cdot1_k_fwd, _cdot1_k_bwd)


def _cdot(a, b):
    return _cdot1_k(a, b, 'nn')


def _cdot_nt(a, b):
    return _cdot1_k(a, b, 'nt')


def _cdot_tn(a, b):
    return _cdot1_k(a, b, 'tn')


def _hdot(a, b):
    return _cdot_k(a, b, 'nn')


def _dot_exact(x, m01, kind):
    d = lambda p: lax.dot_general(p, m01.astype(BF16), _DIMS[kind], preferred_element_type=F32)
    hi, mid, lo = _split(x, 3)
    return d(hi) + (d(mid) + d(lo))


@functools.partial(jax.custom_vjp, nondiff_argnums=(1,))
def _xdot(x, make_m):
    return _dot_exact(x, make_m(), 'nn')


def _xdot_fwd(x, make_m):
    return _dot_exact(x, make_m(), 'nn'), None


def _xdot_bwd(make_m, _, ct):
    return (_dot_exact(ct, make_m(), 'nt'),)


_xdot.defvjp(_xdot_fwd, _xdot_bwd)


def _iota2(n, m):
    return lax.broadcasted_iota(jnp.int32, (n, m), 0), lax.broadcasted_iota(jnp.int32, (n, m), 1)


def _head_blocks(w):
    ri, ci = _iota2(w, w)
    return (ri // HEAD_DIM == ci // HEAD_DIM).astype(F32)


def _segsum(x):
    return _xdot(x, functools.partial(_head_blocks, x.shape[-1]))


def _cumsum_rows(x):
    return _cumsum_k(x, x.shape[0])


@functools.partial(jax.custom_vjp, nondiff_argnums=(1,))
def _cumsum_k(x, n):
    return _lower_dot(x, n, False)


def _lower_dot(x, n, transpose):
    ri, ci = _iota2(n, n)
    m = ((ri <= ci) if transpose else (ri >= ci)).astype(BF16)
    d = lambda p: lax.dot_general(m, p, _DIMS['nn'], preferred_element_type=F32)
    hi, mid, lo = _split(x, 3)
    return d(hi) + (d(mid) + d(lo))


def _cumsum_k_fwd(x, n):
    return _lower_dot(x, n, False), None


def _cumsum_k_bwd(n, _, ct):
    return (_lower_dot(ct, n, True),)


_cumsum_k.defvjp(_cumsum_k_fwd, _cumsum_k_bwd)


def _rms(x, g):
    return x * lax.rsqrt(jnp.mean(x * x, axis=-1, keepdims=True) + NORM_EPS) * g


DENSE_ROWS = 1024


def _row_loop(n_rows, fn):
    rows = min(DENSE_ROWS, n_rows)

    def step(i, c):
        fn(pl.ds(pl.multiple_of(i * rows, rows), rows))
        return c
    lax.fori_loop(0, n_rows // rows, step, 0)


def ffn_fwd(x, g, wi, wo, name, hosted=()):
    T = x.shape[0]
    nj = D_FF // FF_CHUNK
    gather = ChipGather(list(hosted))
    n = gather.n

    def body(*refs):
        x_ref, g_ref, wg_ref, wu_ref, wo_ref = refs[:5]
        hx, o_ref, ho = refs[5:5 + n], refs[5 + n], refs[6 + n:6 + 2 * n]
        h_ref, acc_ref = refs[6 + 2 * n:8 + 2 * n]
        sems = refs[8 + 2 * n:]
        j = pl.program_id(0)

        @pl.when(j == 0)
        def _():
            gather.start(hx, ho, sems)

            def init(r):
                h_ref[r, :] = _rms(x_ref[r, :], g_ref[...]).astype(BF16)
                acc_ref[r, :] = jnp.zeros((r.size, D_MODEL), F32)
            _row_loop(T, init)

        def blk(r):
            hb = h_ref[r, :]
            gate = jnp.dot(hb, wg_ref[...], preferred_element_type=F32)
            up = jnp.dot(hb, wu_ref[...], preferred_element_type=F32)
            a = (gate * jax.nn.sigmoid(gate) * up).astype(BF16)
            acc_ref[r, :] += jnp.dot(a, wo_ref[...], preferred_element_type=F32)
        _row_loop(T, blk)

        @pl.when(j == nj - 2)
        def _():
            gather.relay(hx, ho, sems)

        @pl.when(j == nj - 1)
        def _():
            def fin(r):
                o_ref[r, :] = x_ref[r, :] + 0.5 * acc_ref[r, :]
            _row_loop(T, fin)
            gather.finish(hx, ho, sems)

    full = pl.BlockSpec((T, D_MODEL), lambda j: (0, 0))
    res = pl.pallas_call(
        body, name=name, grid=(nj,),
        in_specs=[full, pl.BlockSpec((1, D_MODEL), lambda j: (0, 0)),
                  pl.BlockSpec((D_MODEL, FF_CHUNK), lambda j: (0, j)),
                  pl.BlockSpec((D_MODEL, FF_CHUNK), lambda j: (0, j + nj)),
                  pl.BlockSpec((FF_CHUNK, D_MODEL), lambda j: (j, 0))] + gather.in_specs,
        out_specs=tuple([full] + gather.out_specs),
        out_shape=tuple([jax.ShapeDtypeStruct((T, D_MODEL), F32)] + gather.out_shape),
        scratch_shapes=[pltpu.VMEM((T, D_MODEL), BF16), pltpu.VMEM((T, D_MODEL), F32)] + gather.scratch,
        compiler_params=_params(("arbitrary",)))(x, g, wi, wi, wo, *hosted)
    return res[0], list(res[1:])


def _norm_bwd_rows(x, g, dh, dres):
    rstd = lax.rsqrt(jnp.mean(x * x, axis=-1, keepdims=True) + NORM_EPS)
    xh = x * rstd
    dxh = dh * g
    dx = rstd * (dxh - xh * jnp.mean(dxh * xh, axis=-1, keepdims=True))
    return dres + dx, jnp.sum(dh * xh, axis=0, keepdims=True)


def ffn_bwd(x, dy, g, wi, wo, name, hosted=()):
    T = x.shape[0]
    nj = D_FF // FF_CHUNK
    scatter = ChipScatter(list(hosted))
    n = scatter.n

    def body(*refs):
        x_ref, dy_ref, g_ref, wg_ref, wu_ref, wo_ref = refs[:6]
        hx = refs[6:6 + n]
        dx_ref, dg_ref, dwg_ref, dwu_ref, dwo_ref = refs[6 + n:11 + n]
        ho = refs[11 + n:11 + 2 * n]
        h_ref, da_ref, dh_ref = refs[11 + 2 * n:14 + 2 * n]
        sems = refs[14 + 2 * n:]
        j = pl.program_id(0)

        @pl.when(j == 0)
        def _():
            scatter.start(hx, ho, sems)

            def init(r):
                h_ref[r, :] = _rms(x_ref[r, :], g_ref[...]).astype(BF16)
                da_ref[r, :] = (0.5 * dy_ref[r, :]).astype(BF16)
                dh_ref[r, :] = jnp.zeros((r.size, D_MODEL), F32)
            _row_loop(T, init)

        dwg_ref[...] = jnp.zeros_like(dwg_ref)
        dwu_ref[...] = jnp.zeros_like(dwu_ref)
        dwo_ref[...] = jnp.zeros_like(dwo_ref)

        def blk(r):
            hb = h_ref[r, :]
            db = da_ref[r, :]
            gate = jnp.dot(hb, wg_ref[...], preferred_element_type=F32)
            up = jnp.dot(hb, wu_ref[...], preferred_element_type=F32)
            sg = jax.nn.sigmoid(gate)
            sl = gate * sg
            da = _bdot_nt(db, wo_ref[...])
            dup = (da * sl).astype(BF16)
            dgate = (da * up * (sg * (1.0 + gate * (1.0 - sg)))).astype(BF16)
            dwo_ref[...] += _bdot_tn((sl * up).astype(BF16), db)
            dwg_ref[...] += _bdot_tn(hb, dgate)
            dwu_ref[...] += _bdot_tn(hb, dup)
            dh_ref[r, :] += _bdot_nt(dgate, wg_ref[...]) + _bdot_nt(dup, wu_ref[...])
        _row_loop(T, blk)

        @pl.when(j == nj - 1)
        def _():
            dg_ref[...] = jnp.zeros_like(dg_ref)

            def fin(r):
                dx, dg = _norm_bwd_rows(x_ref[r, :], g_ref[...], dh_ref[r, :], dy_ref[r, :])
                dx_ref[r, :] = dx
                dg_ref[...] += dg
            _row_loop(T, fin)
            scatter.finish(hx, ho, sems)

    full = pl.BlockSpec((T, D_MODEL), lambda j: (0, 0))
    vec = pl.BlockSpec((1, D_MODEL), lambda j: (0, 0))
    res = pl.pallas_call(
        body, name=name, grid=(nj,),
        in_specs=[full, full, vec,
                  pl.BlockSpec((D_MODEL, FF_CHUNK), lambda j: (0, j)),
                  pl.BlockSpec((D_MODEL, FF_CHUNK), lambda j: (0, j + nj)),
                  pl.BlockSpec((FF_CHUNK, D_MODEL), lambda j: (j, 0))] + scatter.in_specs,
        out_specs=tuple([full, vec,
                         pl.BlockSpec((D_MODEL, FF_CHUNK), lambda j: (0, j)),
                         pl.BlockSpec((D_MODEL, FF_CHUNK), lambda j: (0, j)),
                         pl.BlockSpec((FF_CHUNK, D_MODEL), lambda j: (j, 0))] + scatter.out_specs),
        out_shape=tuple([jax.ShapeDtypeStruct((T, D_MODEL), F32), jax.ShapeDtypeStruct((1, D_MODEL), F32),
                         jax.ShapeDtypeStruct((D_MODEL, D_FF), F32), jax.ShapeDtypeStruct((D_MODEL, D_FF), F32),
                         jax.ShapeDtypeStruct((D_FF, D_MODEL), F32)] + scatter.out_shape),
        scratch_shapes=[pltpu.VMEM((T, D_MODEL), BF16), pltpu.VMEM((T, D_MODEL), BF16),
                        pltpu.VMEM((T, D_MODEL), F32)] + scatter.scratch,
        compiler_params=_params(("arbitrary",)))(x, dy, g, wi, wi, wo, *hosted)
    return res[0], res[1], res[2], res[3], res[4], list(res[5:])


def proj_fwd(x, g, w, name):
    T = x.shape[0]
    nj = D_IN_PAD // IN_CHUNK

    def body(x_ref, g_ref, w_ref, o_ref, h_ref):
        @pl.when(pl.program_id(0) == 0)
        def _():
            def init(r):
                h_ref[r, :] = _rms(x_ref[r, :], g_ref[...]).astype(BF16)
            _row_loop(T, init)

        def blk(r):
            o_ref[r, :] = jnp.dot(h_ref[r, :], w_ref[...], preferred_element_type=F32)
        _row_loop(T, blk)

    return pl.pallas_call(
        body, name=name, grid=(nj,),
        in_specs=[pl.BlockSpec((T, D_MODEL), lambda j: (0, 0)), pl.BlockSpec((1, D_MODEL), lambda j: (0, 0)),
                  pl.BlockSpec((D_MODEL, IN_CHUNK), lambda j: (0, j))],
        out_specs=pl.BlockSpec((T, IN_CHUNK), lambda j: (0, j)),
        out_shape=jax.ShapeDtypeStruct((T, D_IN_PAD), F32),
        scratch_shapes=[pltpu.VMEM((T, D_MODEL), BF16)],
        compiler_params=_params(("arbitrary",)))(x, g, w)


def proj_bwd(x, dres, g, w, dp, name):
    T = x.shape[0]
    nj = D_IN_PAD // IN_CHUNK

    def body(x_ref, dres_ref, g_ref, w_ref, dp_ref, dx_ref, dg_ref, dw_ref, h_ref, dh_ref):
        j = pl.program_id(0)

        @pl.when(j == 0)
        def _():
            def init(r):
                h_ref[r, :] = _rms(x_ref[r, :], g_ref[...]).astype(BF16)
                dh_ref[r, :] = jnp.zeros((r.size, D_MODEL), F32)
            _row_loop(T, init)

        dw_ref[...] = jnp.zeros_like(dw_ref)

        def blk(r):
            dpb = dp_ref[r, :].astype(BF16)
            dw_ref[...] += _bdot_tn(h_ref[r, :], dpb)
            dh_ref[r, :] += _bdot_nt(dpb, w_ref[...])
        _row_loop(T, blk)

        @pl.when(j == nj - 1)
        def _():
            dg_ref[...] = jnp.zeros_like(dg_ref)

            def fin(r):
                dx, dg = _norm_bwd_rows(x_ref[r, :], g_ref[...], dh_ref[r, :], dres_ref[r, :])
                dx_ref[r, :] = dx
                dg_ref[...] += dg
            _row_loop(T, fin)

    full = pl.BlockSpec((T, D_MODEL), lambda j: (0, 0))
    vec = pl.BlockSpec((1, D_MODEL), lambda j: (0, 0))
    return pl.pallas_call(
        body, name=name, grid=(nj,),
        in_specs=[full, full, vec, pl.BlockSpec((D_MODEL, IN_CHUNK), lambda j: (0, j)),
                  pl.BlockSpec((T, IN_CHUNK), lambda j: (0, j))],
        out_specs=(full, vec, pl.BlockSpec((D_MODEL, IN_CHUNK), lambda j: (0, j))),
        out_shape=(jax.ShapeDtypeStruct((T, D_MODEL), F32), jax.ShapeDtypeStruct((1, D_MODEL), F32),
                   jax.ShapeDtypeStruct((D_MODEL, D_IN_PAD), F32)),
        scratch_shapes=[pltpu.VMEM((T, D_MODEL), BF16), pltpu.VMEM((T, D_MODEL), F32)],
        compiler_params=_params(("arbitrary",)))(x, dres, g, w, dp)


def out_fwd(mixed, w, x, name):
    T = x.shape[0]

    def body(m_ref, w_ref, x_ref, o_ref):
        o_ref[...] = x_ref[...] + jnp.dot(m_ref[...].astype(BF16), w_ref[...], preferred_element_type=F32)

    blk = pl.BlockSpec((ROWS, D_MODEL), lambda i: (i, 0))
    return pl.pallas_call(
        body, name=name, grid=(T // ROWS,),
        in_specs=[blk, pl.BlockSpec((D_MODEL, D_MODEL), lambda i: (0, 0)), blk],
        out_specs=blk, out_shape=jax.ShapeDtypeStruct((T, D_MODEL), F32),
        compiler_params=_params(("arbitrary",)))(mixed, w, x)


def out_bwd(mixed, w, dy, name):
    T = dy.shape[0]

    def body(m_ref, w_ref, dy_ref, dm_ref, dw_ref):
        @pl.when(pl.program_id(0) == 0)
        def _():
            dw_ref[...] = jnp.zeros_like(dw_ref)
        dyb = dy_ref[...].astype(BF16)
        dm_ref[...] = _bdot_nt(dyb, w_ref[...])
        dw_ref[...] += _bdot_tn(m_ref[...].astype(BF16), dyb)

    blk = pl.BlockSpec((ROWS, D_MODEL), lambda i: (i, 0))
    sq = pl.BlockSpec((D_MODEL, D_MODEL), lambda i: (0, 0))
    return pl.pallas_call(
        body, name=name, grid=(T // ROWS,),
        in_specs=[blk, sq, blk], out_specs=(blk, sq),
        out_shape=(jax.ShapeDtypeStruct((T, D_MODEL), F32), jax.ShapeDtypeStruct((D_MODEL, D_MODEL), F32)),
        compiler_params=_params(("arbitrary",)))(mixed, w, dy)


def loss_head(x, g, target, name):
    T = x.shape[0]

    def body(x_ref, g_ref, t_ref, loss_ref, dx_ref, dg_ref):
        @pl.when(pl.program_id(0) == 0)
        def _():
            loss_ref[...] = jnp.zeros_like(loss_ref)
            dg_ref[...] = jnp.zeros_like(dg_ref)
        xb = x_ref[...]
        rstd = lax.rsqrt(jnp.mean(xb * xb, axis=-1, keepdims=True) + NORM_EPS)
        xh = xb * rstd
        err = xh * g_ref[...] - t_ref[...]
        loss_ref[...] += 0.5 * jnp.sum(jnp.mean(err * err, axis=-1, keepdims=True), axis=0, keepdims=True)
        dy = err * (1.0 / D_MODEL)
        dg_ref[...] += jnp.sum(dy * xh, axis=0, keepdims=True)
        dxh = dy * g_ref[...]
        dx_ref[...] = rstd * (dxh - xh * jnp.mean(dxh * xh, axis=-1, keepdims=True))

    blk = pl.BlockSpec((ROWS, D_MODEL), lambda i: (i, 0))
    vec = pl.BlockSpec((1, D_MODEL), lambda i: (0, 0))
    return pl.pallas_call(
        body, name=name, grid=(T // ROWS,),
        in_specs=[blk, vec, blk], out_specs=(pl.BlockSpec((1, 1), lambda i: (0, 0)), blk, vec),
        out_shape=(jax.ShapeDtypeStruct((1, 1), F32), jax.ShapeDtypeStruct((T, D_MODEL), F32),
                   jax.ShapeDtypeStruct((1, D_MODEL), F32)),
        compiler_params=_params(("arbitrary",)))(x, g, target)


def rowwise_fwd(fn, rows, shared, out_widths, name):
    T = rows[0].shape[0]
    n_in = len(rows) + len(shared)

    def body(*refs):
        res = fn(*[r[...] for r in refs[:n_in]])
        for o, v in zip(refs[n_in:], res):
            o[...] = v

    in_specs = ([pl.BlockSpec((ROWS, a.shape[1]), lambda i: (i, 0)) for a in rows]
                + [pl.BlockSpec(a.shape, lambda i: (0, 0)) for a in shared])
    return pl.pallas_call(
        body, name=name, grid=(T // ROWS,), in_specs=in_specs,
        out_specs=tuple(pl.BlockSpec((ROWS, w), lambda i: (i, 0)) for w in out_widths),
        out_shape=tuple(jax.ShapeDtypeStruct((T, w), F32) for w in out_widths),
        compiler_params=_params(("arbitrary",)))(*rows, *shared)


def rowwise_bwd(fn, rows, shared, cts, name, ct_fn=None):
    T = rows[0].shape[0]
    nr, ns, nc = len(rows), len(shared), len(cts)

    def body(*refs):
        ins = [r[...] for r in refs[:nr + ns]]
        ctv = tuple(r[...] for r in refs[nr + ns:nr + ns + nc])
        outs = refs[nr + ns + nc:]
        _, vjp = jax.vjp(fn, *ins)
        grads = vjp(ct_fn(*ctv) if ct_fn is not None else ctv)
        for k in range(nr):
            outs[k][...] = grads[k]

        @pl.when(pl.program_id(0) == 0)
        def _():
            for k in range(ns):
                outs[nr + k][...] = jnp.zeros_like(outs[nr + k])
        for k in range(ns):
            outs[nr + k][...] += grads[nr + k]

    row_spec = lambda a: pl.BlockSpec((ROWS, a.shape[1]), lambda i: (i, 0))
    sh_spec = lambda a: pl.BlockSpec(a.shape, lambda i: (0, 0))
    return pl.pallas_call(
        body, name=name, grid=(T // ROWS,),
        in_specs=[row_spec(a) for a in rows] + [sh_spec(a) for a in shared] + [row_spec(a) for a in cts],
        out_specs=tuple([row_spec(a) for a in rows] + [sh_spec(a) for a in shared]),
        out_shape=tuple(jax.ShapeDtypeStruct(a.shape, F32) for a in list(rows) + list(shared)),
        compiler_params=_params(("arbitrary",)))(*rows, *shared, *cts)


def shift_rows(x, s):
    return jnp.pad(x, ((s, 0), (0, 0)))[:x.shape[0]]


def unshift_rows(x, s):
    return jnp.pad(x, ((0, s), (0, 0)))[s:]


def _neg_expm1(y):
    series = -(y * (1.0 + y * (0.5 + y * (1.0 / 6.0 + y * (1.0 / 24.0)))))
    return jnp.where(y > -0.05, series, 1.0 - jnp.exp(y))


def lru_pre_fn(x0, x1, x2, x3, first, w0, w1, w2, w3, cb, ga, gab, gx, gxb, lam):
    xc = w3 * x0 + w2 * x1 + w1 * x2 + w0 * x3 + cb
    r = jax.nn.sigmoid(_hdot(xc, ga) + gab)
    i = jax.nn.sigmoid(_hdot(xc, gx) + gxb)
    log_a = -LRU_C * r * jax.nn.softplus(-lam)
    a = jnp.exp(log_a)
    mult = jnp.where(first > 0.5, 1.0, jnp.sqrt(_neg_expm1(2.0 * log_a)))
    return a, mult * i * xc


def lru_post_fn(h, py, og):
    return (_rms(h * jax.nn.gelu(py), og),)


def lru_scan(a, b, reverse, name):
    T, C = a.shape
    nb = T // 8

    def body(a_ref, b_ref, h_ref):
        rows = lax.broadcasted_iota(jnp.int32, (8, C), 0)

        def blk(i, carry):
            j = nb - 1 - i if reverse else i
            r = pl.ds(pl.multiple_of(j * 8, 8), 8)
            A = a_ref[r, :]
            B = b_ref[r, :]
            for s in (1, 2, 4):
                if reverse:
                    keep = rows < 8 - s
                    sh = 8 - s
                else:
                    keep = rows >= s
                    sh = s
                Bs = jnp.where(keep, pltpu.roll(B, sh, 0), 0.0)
                As = jnp.where(keep, pltpu.roll(A, sh, 0), 1.0)
                B = B + A * Bs
                A = A * As
            hb = B + A * carry
            h_ref[r, :] = hb
            edge = 0 if reverse else 7
            return jnp.sum(jnp.where(rows == edge, hb, 0.0), axis=0, keepdims=True)

        lax.fori_loop(0, nb, blk, jnp.zeros((1, C), F32))

    full = pl.BlockSpec((T, C), lambda: (0, 0))
    return pl.pallas_call(body, name=name, in_specs=[full, full], out_specs=full,
                          out_shape=jax.ShapeDtypeStruct((T, C), F32), compiler_params=_params())(a, b)


def make_rwkv_pre_fn(has_vres):
    def fn(p, pp, *rest):
        if has_vres:
            vf, mu, w_up, w_b, a_up, a_b, g_up, kk_w, ka_w, vw1, vw2, vb = rest
        else:
            mu, w_up, w_b, a_up, a_b, g_up, kk_w, ka_w = rest
        xm = p + (pp - p) * mu
        r, k, v = xm[:, 0:384], xm[:, 384:768], xm[:, 768:1152]
        xw, xa, xg = xm[:, 1152:1216], xm[:, 1216:1280], xm[:, 1280:1408]
        w_log = -jax.nn.softplus(-(w_b + _hdot(jnp.tanh(xw), w_up))) - 0.5
        lw = -jnp.exp(w_log)
        a = jax.nn.sigmoid(a_b + _hdot(xa, a_up))
        g = _hdot(jax.nn.sigmoid(xg), g_up)
        if has_vres:
            v = v + (vf - v) * jax.nn.sigmoid(vb + _hdot(_hdot(v, vw1), vw2))
        kkx = k * kk_w
        kk = kkx * lax.rsqrt(_segsum(kkx * kkx) + 1e-6)
        k2 = k * (1.0 + (a - 1.0) * ka_w)
        return r, lw, k2, v, kk, a, g
    return fn


def rwkv_post_fn(y, r, k2, v, g, ln_g, ln_b, r_k):
    mean = _segsum(y) * (1.0 / HEAD_DIM)
    yc = y - mean
    var = _segsum(yc * yc) * (1.0 / HEAD_DIM)
    yn = yc * lax.rsqrt(var + GN_EPS) * ln_g + ln_b
    bonus = _segsum(r * k2 * r_k) * v
    return ((yn + bonus) * g,)


def _head_expander(first_lane):
    ri, ci = _iota2(128, MIX_W)
    return (ri == ci // HEAD_DIM + first_lane).astype(F32)


def gdn_pre_fn(x0, x1, x2, x3, ab, w0, w1, w2, w3, alog, dtb):
    qkv = jax.nn.silu(w3 * x0 + w2 * x1 + w1 * x2 + w0 * x3)
    q, k, v = qkv[:, 0:384], qkv[:, 384:768], qkv[:, 768:1152]
    q = q * lax.rsqrt(_segsum(q * q) + 1e-6) * (HEAD_DIM ** -0.5)
    k = k * lax.rsqrt(_segsum(k * k) + 1e-6)
    g = -jnp.exp(alog) * jax.nn.softplus(ab + dtb)
    beta = jax.nn.sigmoid(ab)
    ge = _xdot(g, functools.partial(_head_expander, 0))
    be = _xdot(beta, functools.partial(_head_expander, HEADS))
    return q, k, v, ge, be


def gdn_post_fn(o, z, ng):
    ms = _segsum(o * o) * (1.0 / HEAD_DIM)
    return (o * lax.rsqrt(ms + NORM_EPS) * ng * jax.nn.silu(z),)


def _neumann_inv(m):
    n = m.shape[-1]
    ri, ci = _iota2(n, n)
    eye = (ri == ci).astype(F32)
    md = jnp.where(ri // 16 == ci // 16, m, 0.0)
    mo = m - md
    t0 = eye + md
    p2 = _hdot(md, md)
    t0 = t0 + _hdot(t0, p2)
    p4 = _hdot(p2, p2)
    t0 = t0 + _hdot(t0, p4)
    p8 = _hdot(p4, p4)
    t0 = t0 + _hdot(t0, p8)
    nn = _hdot(t0, mo)
    n2 = _hdot(nn, nn)
    t1 = eye + nn + n2 + _hdot(nn, n2)
    return _hdot(t1, t0)


@jax.custom_vjp
def _inv_saved(m, t_saved):
    return t_saved


def _inv_saved_fwd(m, t_saved):
    return t_saved, t_saved


def _inv_saved_bwd(t_saved, dt):
    tt = jnp.swapaxes(t_saved, -1, -2)
    return _hdot(_hdot(tt, dt), tt), jnp.zeros_like(t_saved)


_inv_saved.defvjp(_inv_saved_fwd, _inv_saved_bwd)


def _heads(x):
    return jnp.concatenate([x[None, :, h * HEAD_DIM:(h + 1) * HEAD_DIM] for h in range(HEADS)], axis=0)


def _unheads(y):
    return jnp.concatenate([lax.index_in_dim(y, h, 0, keepdims=False) for h in range(HEADS)], axis=1)


def rwkv_heads(s0, r, lw, k2, v, kk, a, inv):
    n = r.shape[0]
    ri, ci = _iota2(n, n)
    low, strict = ri >= ci, ri > ci
    cs = _cumsum_rows(lw)
    cl = jnp.sum(lw, axis=0, keepdims=True)
    p_in, p_prev, p_inv = jnp.exp(cs), jnp.exp(cs - lw), jnp.exp(-cs)
    p_rest, p_all = jnp.exp(cl - cs), jnp.exp(cl)
    bd = kk * a
    at, rt = _heads(-kk * p_prev), _heads(r * p_in)
    bh, kh = _heads(bd * p_inv), _heads(k2 * p_inv)
    vh = _heads(v)
    m_ab = jnp.where(strict, _cdot_nt(at, bh), 0.0)
    m_ak = jnp.where(strict, _cdot_nt(at, kh), 0.0)
    m_rb = jnp.where(low, _cdot_nt(rt, bh), 0.0)
    m_rk = jnp.where(low, _cdot_nt(rt, kh), 0.0)
    sa = _cdot(inv(m_ab), _cdot_nt(at, s0) + _cdot(m_ak, vh))
    y = _cdot_nt(rt, s0) + _cdot(m_rb, sa) + _cdot(m_rk, vh)
    s1 = s0 * _heads(p_all) + _cdot_tn(sa, _heads(bd * p_rest)) + _cdot_tn(vh, _heads(k2 * p_rest))
    return _unheads(y), s1


def gdn_heads(s0, q, k, v, ge, be, inv):
    n = q.shape[0]
    ri, ci = _iota2(n, n)
    low, strict = ri >= ci, ri > ci
    gc = _cumsum_rows(ge)
    gl = jnp.sum(ge, axis=0, keepdims=True)
    gch = _heads(gc)
    decay = jnp.where(low, jnp.exp(jnp.where(low, gch - jnp.swapaxes(gch, 1, 2), 0.0)), 0.0)
    kb = k * be
    e = jnp.exp(gc)
    kh = _heads(k)
    m = -jnp.where(strict, _cdot_nt(_heads(kb), kh) * decay, 0.0)
    mr = jnp.where(low, _cdot_nt(_heads(q), kh) * decay, 0.0)
    u = _cdot(inv(m), _heads(v * be) - _cdot_nt(_heads(kb * e), s0))
    y = _cdot_nt(_heads(q * e), s0) + _cdot(mr, u)
    s1 = s0 * _heads(jnp.exp(gl)) + _cdot_tn(u, _heads(k * jnp.exp(gl - gc)))
    return _unheads(y), s1


def core_fwd(heads_fn, ins, name, hosted=()):
    T = ins[0].shape[0]
    nc = T // CHUNK
    n = len(ins)
    gather = ChipGather(list(hosted))
    ng = gather.n

    def body(*refs):
        hx = refs[n:n + ng]
        y_ref, s0_ref, t_ref = refs[n + ng:n + ng + 3]
        ho = refs[n + ng + 3:n + 2 * ng + 3]
        s_ref = refs[n + 2 * ng + 3]
        sems = refs[n + 2 * ng + 4:]
        c = pl.program_id(0)

        @pl.when(c == 0)
        def _():
            gather.start(hx, ho, sems)
            s_ref[...] = jnp.zeros_like(s_ref)

        s0 = s_ref[...]
        kept = []

        def inv(m):
            kept.append(_neumann_inv(m))
            return kept[0]

        y, s1 = heads_fn(s0, *[r[...] for r in refs[:n]], inv)
        y_ref[...] = y
        s0_ref[0] = s0
        t_ref[0] = kept[0]
        s_ref[...] = s1

        @pl.when(c == nc - 4)
        def _():
            gather.relay(hx, ho, sems)

        @pl.when(c == nc - 1)
        def _():
            gather.finish(hx, ho, sems)

    row = pl.BlockSpec((CHUNK, MIX_W), lambda c: (c, 0))
    st_shape = (HEADS, HEAD_DIM, HEAD_DIM)
    st = pl.BlockSpec((1,) + st_shape, lambda c: (c, 0, 0, 0))
    res = pl.pallas_call(
        body, name=name, grid=(nc,), in_specs=[row] * n + gather.in_specs,
        out_specs=tuple([row, st, st] + gather.out_specs),
        out_shape=tuple([jax.ShapeDtypeStruct((T, MIX_W), F32), jax.ShapeDtypeStruct((nc,) + st_shape, F32),
                         jax.ShapeDtypeStruct((nc,) + st_shape, F32)] + gather.out_shape),
        scratch_shapes=[pltpu.VMEM(st_shape, F32)] + gather.scratch,
        compiler_params=_params(("arbitrary",)))(*ins, *hosted)
    return res[0], res[1], res[2], list(res[3:])


def core_bwd(heads_fn, ins, s0_all, t_all, dy, name):
    T = ins[0].shape[0]
    nc = T // CHUNK
    n = len(ins)

    def body(*refs):
        s0_ref, t_ref, dy_ref = refs[n:n + 3]
        outs = refs[n + 3:n + 3 + n]
        ds_ref = refs[n + 3 + n]

        @pl.when(pl.program_id(0) == 0)
        def _():
            ds_ref[...] = jnp.zeros_like(ds_ref)

        t_saved = t_ref[0]
        f = lambda s0, *xs: heads_fn(s0, *xs, lambda m: _inv_saved(m, t_saved))
        _, vjp = jax.vjp(f, s0_ref[0], *[r[...] for r in refs[:n]])
        grads = vjp((dy_ref[...], ds_ref[...]))
        ds_ref[...] = grads[0]
        for k in range(n):
            outs[k][...] = grads[1 + k]

    row = pl.BlockSpec((CHUNK, MIX_W), lambda c: (nc - 1 - c, 0))
    st_shape = (HEADS, HEAD_DIM, HEAD_DIM)
    st = pl.BlockSpec((1,) + st_shape, lambda c: (nc - 1 - c, 0, 0, 0))
    return pl.pallas_call(
        body, name=name, grid=(nc,), in_specs=[row] * n + [st, st, row], out_specs=tuple([row] * n),
        out_shape=tuple(jax.ShapeDtypeStruct((T, MIX_W), F32) for _ in range(n)),
        scratch_shapes=[pltpu.VMEM(st_shape, F32)],
        compiler_params=_params(("arbitrary",)))(*ins, s0_all, t_all, dy)


def _block_diag(w):
    out = jnp.zeros((LRU_W, LRU_W), w.dtype)
    for n in range(LRU_BLOCKS):
        out = lax.dynamic_update_slice(out, w[n], (n * 64, n * 64))
    return out


def _block_diag_grad(g):
    return jnp.stack([g[n * 64:(n + 1) * 64, n * 64:(n + 1) * 64] for n in range(LRU_BLOCKS)])


def _row(v):
    return v.reshape(1, -1)


def _pad128(v):
    return jnp.pad(v.reshape(1, -1), ((0, 0), (0, 128 - v.size)))


def _layer_shared(w, l):
    cw = w['lru_conv_w'][l]
    lru_pre = [_row(cw[0]), _row(cw[1]), _row(cw[2]), _row(cw[3]), _row(w['lru_conv_b'][l]),
               _block_diag(w['lru_gate_a_w'][l]), _row(w['lru_gate_a_b'][l]),
               _block_diag(w['lru_gate_x_w'][l]), _row(w['lru_gate_x_b'][l]), _row(w['lru_lambda'][l])]
    rw_pre = [_row(w['rwkv_mu'][l]), w['rwkv_w_up'][l], _row(w['rwkv_w_bias'][l]), w['rwkv_a_up'][l],
              _row(w['rwkv_a_bias'][l]), w['rwkv_g_up'][l], _row(w['rwkv_k_k'][l]), _row(w['rwkv_k_a'][l])]
    if l > 0:
        rw_pre += [w['rwkv_vres_w1'][l - 1], w['rwkv_vres_w2'][l - 1], _row(w['rwkv_vres_b'][l - 1])]
    rw_post = [_row(w['rwkv_ln_g'][l]), _row(w['rwkv_ln_b'][l]), _row(w['rwkv_r_k'][l])]
    gw = w['gdn_conv_w'][l]
    gdn_pre = [_row(gw[0]), _row(gw[1]), _row(gw[2]), _row(gw[3]), _pad128(w['gdn_a_log'][l]),
               _pad128(w['gdn_dt_bias'][l])]
    gdn_post = [_row(jnp.tile(w['gdn_norm'][l], HEADS))]
    return dict(lru_pre=lru_pre, lru_post=[_row(w['lru_out_norm'][l])], rw_pre=rw_pre, rw_post=rw_post,
                gdn_pre=gdn_pre, gdn_post=gdn_post)


def _mixer_fwd(p, sh, l, v_first, host_rwkv=(), host_gdn=()):
    T = p.shape[0]
    lx, ly = p[:, 0:256], p[:, 256:512]
    prw, qkv, z, ab = p[:, 512:1920], p[:, 1920:3072], p[:, 3072:3456], p[:, 3456:3584]
    first = (lax.broadcasted_iota(jnp.int32, (T, LRU_W), 0) == 0).astype(F32)
    lru_rows = [lx, shift_rows(lx, 1), shift_rows(lx, 2), shift_rows(lx, 3), first]
    a, b = rowwise_fwd(lru_pre_fn, lru_rows, sh['lru_pre'], (LRU_W, LRU_W), f"lru_pre_fwd{l}")
    hseq = lru_scan(a, b, False, f"lru_scan_fwd{l}")
    (y_lru,) = rowwise_fwd(lru_post_fn, [hseq, ly], sh['lru_post'], (LRU_W,), f"lru_post_fwd{l}")

    rw_rows = [prw, shift_rows(prw, 1)] + ([v_first] if l > 0 else [])
    rw = rowwise_fwd(make_rwkv_pre_fn(l > 0), rw_rows, sh['rw_pre'], (MIX_W,) * 7, f"rwkv_pre_fwd{l}")
    r, lw, k2, v, kk, ar, g = rw
    y_raw, rs0, rt, got_rwkv = core_fwd(rwkv_heads, [r, lw, k2, v, kk, ar], f"rwkv_core_fwd{l}", host_rwkv)
    (y_rw,) = rowwise_fwd(rwkv_post_fn, [y_raw, r, k2, v, g], sh['rw_post'], (MIX_W,), f"rwkv_post_fwd{l}")

    gdn_rows = [qkv, shift_rows(qkv, 1), shift_rows(qkv, 2), shift_rows(qkv, 3), ab]
    gd = rowwise_fwd(gdn_pre_fn, gdn_rows, sh['gdn_pre'], (MIX_W,) * 5, f"gdn_pre_fwd{l}")
    o_raw, gs0, gt, got_gdn = core_fwd(gdn_heads, list(gd), f"gdn_core_fwd{l}", host_gdn)
    (y_gdn,) = rowwise_fwd(gdn_post_fn, [o_raw, z], sh['gdn_post'], (MIX_W,), f"gdn_post_fwd{l}")

    mixed = jnp.concatenate([y_lru, y_rw, y_gdn], axis=1)
    saved = dict(lru_rows=lru_rows, a=a, hseq=hseq, ly=ly, rw_rows=rw_rows, rw=rw, y_raw=y_raw, rs0=rs0, rt=rt,
                 gdn_rows=gdn_rows, gd=gd, o_raw=o_raw, gs0=gs0, gt=gt, z=z)
    v_layer0 = v if l == 0 else None
    return mixed, saved, v_layer0, got_rwkv, got_gdn


def _mixer_bwd(dmixed, sv, sh, l, dv_first):
    d_lru, d_rw, d_gdn = dmixed[:, 0:256], dmixed[:, 256:640], dmixed[:, 640:1024]
    gw = {}

    dh, dly, d_og = rowwise_bwd(lru_post_fn, [sv['hseq'], sv['ly']], sh['lru_post'], [d_lru], f"lru_post_bwd{l}")
    gscan = lru_scan(unshift_rows(sv['a'], 1), dh, True, f"lru_scan_bwd{l}")
    res = rowwise_bwd(lru_pre_fn, sv['lru_rows'], sh['lru_pre'], [gscan, shift_rows(sv['hseq'], 1)],
                      f"lru_pre_bwd{l}", ct_fn=lambda gs, hp: (gs * hp, gs))
    dlx = res[0] + unshift_rows(res[1], 1) + unshift_rows(res[2], 2) + unshift_rows(res[3], 3)
    dw0, dw1, dw2, dw3, dcb, dga, dgab, dgx, dgxb, dlam = res[5:]
    gw['lru_conv_w'] = jnp.concatenate([dw0, dw1, dw2, dw3], axis=0)
    gw['lru_conv_b'] = dcb[0]
    gw['lru_gate_a_w'] = _block_diag_grad(dga)
    gw['lru_gate_a_b'] = dgab.reshape(LRU_BLOCKS, 64)
    gw['lru_gate_x_w'] = _block_diag_grad(dgx)
    gw['lru_gate_x_b'] = dgxb.reshape(LRU_BLOCKS, 64)
    gw['lru_lambda'] = dlam[0]
    gw['lru_out_norm'] = d_og[0]

    r, lw, k2, v, kk, ar, g = sv['rw']
    res = rowwise_bwd(rwkv_post_fn, [sv['y_raw'], r, k2, v, g], sh['rw_post'], [d_rw], f"rwkv_post_bwd{l}")
    dy_raw, dr_p, dk2_p, dv_p, dg = res[:5]
    gw['rwkv_ln_g'], gw['rwkv_ln_b'], gw['rwkv_r_k'] = res[5][0], res[6][0], res[7].reshape(HEADS, HEAD_DIM)
    dr_c, dlw, dk2_c, dv_c, dkk, dar = core_bwd(rwkv_heads, [r, lw, k2, v, kk, ar], sv['rs0'], sv['rt'], dy_raw,
                                                 f"rwkv_core_bwd{l}")
    cts = [dr_p, dr_c, dlw, dk2_p, dk2_c, dv_p, dv_c, dkk, dar, dg]
    if l == 0:
        cts.append(dv_first)
        ct_fn = lambda a1, a2, b, c1, c2, d1, d2, e, f, gg, vf: (a1 + a2, b, c1 + c2, d1 + d2 + vf, e, f, gg)
    else:
        ct_fn = lambda a1, a2, b, c1, c2, d1, d2, e, f, gg: (a1 + a2, b, c1 + c2, d1 + d2, e, f, gg)
    res = rowwise_bwd(make_rwkv_pre_fn(l > 0), sv['rw_rows'], sh['rw_pre'], cts, f"rwkv_pre_bwd{l}", ct_fn=ct_fn)
    dprw = res[0] + unshift_rows(res[1], 1)
    nrow = len(sv['rw_rows'])
    dv_first_out = res[2] if l > 0 else None
    sg = res[nrow:]
    gw['rwkv_mu'], gw['rwkv_w_up'], gw['rwkv_w_bias'], gw['rwkv_a_up'] = sg[0][0], sg[1], sg[2][0], sg[3]
    gw['rwkv_a_bias'], gw['rwkv_g_up'], gw['rwkv_k_k'], gw['rwkv_k_a'] = sg[4][0], sg[5], sg[6][0], sg[7][0]
    if l > 0:
        gw['rwkv_vres_w1'], gw['rwkv_vres_w2'], gw['rwkv_vres_b'] = sg[8], sg[9], sg[10][0]

    do_raw, dz, d_ng = rowwise_bwd(gdn_post_fn, [sv['o_raw'], sv['z']], sh['gdn_post'], [d_gdn], f"gdn_post_bwd{l}")
    gw['gdn_norm'] = jnp.sum(d_ng.reshape(HEADS, HEAD_DIM), axis=0)
    dgd = core_bwd(gdn_heads, list(sv['gd']), sv['gs0'], sv['gt'], do_raw, f"gdn_core_bwd{l}")
    res = rowwise_bwd(gdn_pre_fn, sv['gdn_rows'], sh['gdn_pre'], list(dgd), f"gdn_pre_bwd{l}")
    dqkv = res[0] + unshift_rows(res[1], 1) + unshift_rows(res[2], 2) + unshift_rows(res[3], 3)
    dab = res[4]
    gw['gdn_conv_w'] = jnp.concatenate(res[5:9], axis=0)
    gw['gdn_a_log'], gw['gdn_dt_bias'] = res[9][0, :HEADS], res[10][0, :HEADS]

    dp = jnp.concatenate([dlx, dly, dprw, dqkv, dz, dab], axis=1)
    return dp, gw, dv_first_out


IN_SHARD = D_IN // N_CHIPS
IN_SHARD_PAD = D_IN_PAD // N_CHIPS


def _cols_to_chips(g, n=N_CHIPS):
    r = g.shape[0]
    return jnp.transpose(g.reshape(r, n, -1), (1, 0, 2))


def _cols_from_chips(g):
    return jnp.transpose(g, (1, 0, 2)).reshape(g.shape[1], -1)


def _w_in_from_chips(g):
    nat = _cols_from_chips(g[:, :, :IN_SHARD])
    return jnp.pad(nat, ((0, 0), (0, D_IN_PAD - D_IN)))


def _w_in_to_chips(g):
    return jnp.pad(_cols_to_chips(g[:, :D_IN]), ((0, 0), (0, 0), (0, IN_SHARD_PAD - IN_SHARD)))


def _natural(name, g):
    if name == 'w_in':
        return _w_in_from_chips(g)
    if BIG[name] == 2:
        return _cols_from_chips(g)
    return g.reshape(-1, g.shape[2])


def local_step(x, target, w, wb, shards=None):
    def hosted(keys):
        return [shards[k] for k in keys] if shards is not None else []

    def arrived(keys, gathered):
        for (name, layer), g in zip(keys if shards is not None else [], gathered):
            wb[name][layer] = _natural(name, g)

    saved = []
    v_first = None
    for l in range(N_LAYERS):
        sh = _layer_shared(w, l)
        more = l + 1 < N_LAYERS
        in_ffn1 = [('w_in', l), ('w_out', l)]
        in_rwkv = [('ffn2_wi', l)] + ([('ffn2_wo', l)] if more else [])
        in_gdn = [('ffn1_wi', l + 1)] if more else [('ffn2_wo', l)]
        in_ffn2 = [('ffn1_wo', l + 1)] if more else []
        x1, got = ffn_fwd(x, _row(w['ffn1_norm'][l]), wb['ffn1_wi'][l], wb['ffn1_wo'][l], f"ffn1_fwd{l}",
                          hosted(in_ffn1))
        arrived(in_ffn1, got)
        p = proj_fwd(x1, _row(w['mix_norm'][l]), wb['w_in'][l], f"proj_fwd{l}")
        mixed, sv, v0, got_rwkv, got_gdn = _mixer_fwd(p, sh, l, v_first, hosted(in_rwkv), hosted(in_gdn))
        arrived(in_rwkv, got_rwkv)
        arrived(in_gdn, got_gdn)
        if l == 0:
            v_first = v0
        x2 = out_fwd(mixed, wb['w_out'][l], x1, f"out_fwd{l}")
        x3, got = ffn_fwd(x2, _row(w['ffn2_norm'][l]), wb['ffn2_wi'][l], wb['ffn2_wo'][l], f"ffn2_fwd{l}",
                          hosted(in_ffn2))
        arrived(in_ffn2, got)
        saved.append(dict(x0=x, x1=x1, x2=x2, mixed=mixed, sv=sv, sh=sh))
        x = x3

    loss, dx, dgf = loss_head(x, _row(w['final_norm']), target, "loss_head")
    per_layer = [dict() for _ in range(N_LAYERS)]
    dv_first = jnp.zeros((x.shape[0], MIX_W), F32)

    waiting, chip_sums, arrived_parts = [], {}, {}

    def reduce_now(keys, tag):
        if shards is None:
            return
        sums = chip_sums_of([(n, k, per_layer[k][n]) for n, k in keys], lax.axis_index("c"), tag)
        for key, (total, total_bf) in zip(keys, sums):
            chip_sums[key] = total
            waiting.append((key, total_bf))

    def take_waiting():
        keys, parts = [k for k, _ in waiting], [p for _, p in waiting]
        waiting.clear()
        return keys, parts

    for l in reversed(range(N_LAYERS)):
        s = saved[l]
        gw = per_layer[l]
        keys, parts = take_waiting()
        dx, dg2, dwg, dwu, dwo, got = ffn_bwd(s['x2'], dx, _row(w['ffn2_norm'][l]), wb['ffn2_wi'][l],
                                              wb['ffn2_wo'][l], f"ffn2_bwd{l}", parts)
        arrived_parts.update(zip(keys, got))
        wi_parts = lambda dwg, dwu: (dwg, dwu)
        row_parts = lambda dw: dw.reshape(N_CHIPS, -1, dw.shape[1])
        gw['ffn2_norm'], gw['ffn2_wi'], gw['ffn2_wo'] = dg2[0], wi_parts(dwg, dwu), row_parts(dwo)
        if l == N_LAYERS - 1:
            reduce_now([('ffn2_wi', l), ('ffn2_wo', l)], f"ffn2_{l}")
        dmixed, dw_out = out_bwd(s['mixed'], wb['w_out'][l], dx, f"out_bwd{l}")
        gw['w_out'] = row_parts(dw_out)
        dp, gmix, dvf = _mixer_bwd(dmixed, s['sv'], s['sh'], l, dv_first)
        if l > 0:
            dv_first = dvf
        gw.update(gmix)
        dx, dgm, dwin = proj_bwd(s['x1'], dx, _row(w['mix_norm'][l]), wb['w_in'][l], dp, f"proj_bwd{l}")
        gw['mix_norm'], gw['w_in'] = dgm[0], _w_in_to_chips(dwin)
        if l < N_LAYERS - 1:
            reduce_now([('ffn2_wi', l), ('ffn2_wo', l), ('w_in', l), ('w_out', l)], f"mix_{l}")
        keys, parts = take_waiting()
        dx, dg1, dwg, dwu, dwo, got = ffn_bwd(s['x0'], dx, _row(w['ffn1_norm'][l]), wb['ffn1_wi'][l],
                                              wb['ffn1_wo'][l], f"ffn1_bwd{l}", parts)
        arrived_parts.update(zip(keys, got))
        gw['ffn1_norm'], gw['ffn1_wi'], gw['ffn1_wo'] = dg1[0], wi_parts(dwg, dwu), row_parts(dwo)
        if l == N_LAYERS - 1:
            reduce_now([('w_in', l), ('w_out', l), ('ffn1_wi', l), ('ffn1_wo', l)], f"ffn1_{l}")
        else:
            reduce_now([('ffn1_wi', l), ('ffn1_wo', l)], f"ffn1_{l}")
    grads = {'final_norm': dgf[0]}
    if shards is not None:
        grads['last_round'] = take_waiting()
        arrived_parts.update({key: None for key in grads['last_round'][0]})
    for name in WEIGHTS:
        if name == 'final_norm':
            continue
        if name in BIG:
            if shards is None:
                grads[name] = [per_layer[l][name] for l in range(N_LAYERS)]
            else:
                grads[name] = [(chip_sums[(name, l)], arrived_parts[(name, l)]) for l in range(N_LAYERS)]
        elif name.startswith('rwkv_vres'):
            grads[name] = per_layer[1][name][None]
        else:
            grads[name] = jnp.stack([per_layer[l][name] for l in range(N_LAYERS)])
    return loss[0, 0], dx, grads


ANY = pl.BlockSpec(memory_space=pl.ANY)


def _coords():
    return lax.axis_index("x"), lax.axis_index("y"), lax.axis_index("c")


def _other_chips(x, y):
    return [((x + 1) % 2, y), (x, (y + 1) % 2), ((x + 1) % 2, (y + 1) % 2)]


def allreduce_small(pack, name):
    R = pack.shape[0]
    rh = R // 2

    def body(x_ref, o_ref, sib_ref, chip_ref, parts_ref, send_sems, recv_sems):
        x, y, c = _coords()
        sib = (x, y, 1 - c)

        def copy(k, src, dst, to):
            return pltpu.make_async_remote_copy(src_ref=src, dst_ref=dst, send_sem=send_sems.at[k],
                                                recv_sem=recv_sems.at[k], device_id=to, device_id_type=MESH)

        swap = copy(0, x_ref, sib_ref, sib)
        swap.start()
        swap.wait()
        chip_ref[...] = jnp.where(c == 0, x_ref[...], sib_ref[...]) + jnp.where(c == 0, sib_ref[...], x_ref[...])

        mine = pl.ds(pl.multiple_of(c * rh, 8), rh)
        sends = [copy(1 + j, chip_ref.at[mine], parts_ref.at[j], (px, py, c))
                 for j, (px, py) in enumerate(_other_chips(x, y))]
        for cp in sends:
            cp.start()
        for cp in sends:
            cp.wait()
        s = 2 * x + y
        own = chip_ref[mine, :]
        from_chip = {2: parts_ref[0], 1: parts_ref[1], 3: parts_ref[2]}
        terms = []
        for k in range(N_CHIPS):
            t = own
            for d, part in from_chip.items():
                t = jnp.where(jnp.bitwise_xor(s, d) == k, part, t)
            terms.append(t)
        o_ref[mine, :] = ((terms[0] + terms[1]) + terms[2]) + terms[3]

        share = copy(4, o_ref.at[mine], o_ref.at[mine], sib)
        share.start()
        share.wait()

    vm = pl.BlockSpec(memory_space=pltpu.VMEM)
    return pl.pallas_call(
        body, name=name, in_specs=[vm], out_specs=vm, out_shape=jax.ShapeDtypeStruct((R, 128), F32),
        scratch_shapes=[pltpu.VMEM((R, 128), F32), pltpu.VMEM((R, 128), F32), pltpu.VMEM((3, rh, 128), F32),
                        pltpu.SemaphoreType.DMA((5,)), pltpu.SemaphoreType.DMA((5,))],
        compiler_params=_params())(pack)


class ChipGather:
    def __init__(self, shards):
        self.shapes = [s.shape for s in shards]
        self.n = len(shards)
        self.in_specs = [ANY] * self.n
        self.out_specs = [ANY] * self.n
        self.out_shape = [jax.ShapeDtypeStruct((N_CHIPS,) + s.shape, s.dtype) for s in shards]
        self.scratch = [pltpu.SemaphoreType.DMA((6 * self.n,)), pltpu.SemaphoreType.DMA((6 * self.n,)),
                        pltpu.SemaphoreType.DMA((self.n,))] if self.n else []

    def _rows(self, a, core):
        rh = self.shapes[a][0] // 2
        return pl.ds(pl.multiple_of(core * rh, 16), rh)

    def _copies(self, kind, x_refs, o_refs, sems):
        send_sems, recv_sems, local_sems = sems
        x, y, c = _coords()
        s_me = 2 * x + y
        sib = (x, y, 1 - c)

        def copy(a, k, src, dst, to):
            return pltpu.make_async_remote_copy(src_ref=src, dst_ref=dst, send_sem=send_sems.at[6 * a + k],
                                                recv_sem=recv_sems.at[6 * a + k], device_id=to, device_id_type=MESH)

        if kind == 'own':
            return [pltpu.make_async_copy(x_refs[a], o_refs[a].at[s_me], local_sems.at[a]) for a in range(self.n)]
        out = []
        for j, (px, py) in enumerate(_other_chips(x, y)):
            for a in range(self.n):
                mine = self._rows(a, c)
                part = o_refs[a].at[2 * px + py, mine]
                if kind == 'sent':
                    out.append(copy(a, j, x_refs[a].at[mine], o_refs[a].at[s_me, mine], (px, py, c)))
                elif kind == 'arrived':
                    out.append(copy(a, j, part, part, (px, py, c)))
                elif kind == 'passed':
                    out.append(copy(a, 3 + j, part, part, sib))
                else:
                    theirs = o_refs[a].at[2 * px + py, self._rows(a, 1 - c)]
                    out.append(copy(a, 3 + j, theirs, theirs, sib))
        return out

    def start(self, x_refs, o_refs, sems):
        if not self.n:
            return
        for cp in self._copies('own', x_refs, o_refs, sems) + self._copies('sent', x_refs, o_refs, sems):
            cp.start()

    def relay(self, x_refs, o_refs, sems):
        if not self.n:
            return
        for got, fw in zip(self._copies('arrived', x_refs, o_refs, sems),
                           self._copies('passed', x_refs, o_refs, sems)):
            got.wait_recv()
            fw.start()

    def finish(self, x_refs, o_refs, sems):
        if not self.n:
            return
        for cp in self._copies('from_sibling', x_refs, o_refs, sems):
            cp.wait_recv()
        for cp in self._copies('sent', x_refs, o_refs, sems) + self._copies('passed', x_refs, o_refs, sems):
            cp.wait_send()
        for cp in self._copies('own', x_refs, o_refs, sems):
            cp.wait()


def allgather_chips(shards, name):
    gather = ChipGather(shards)
    n = gather.n

    def body(*refs):
        x_refs, o_refs, sems = refs[:n], refs[n:2 * n], refs[2 * n:]
        gather.start(x_refs, o_refs, sems)
        gather.relay(x_refs, o_refs, sems)
        gather.finish(x_refs, o_refs, sems)

    return pl.pallas_call(
        body, name=name, in_specs=gather.in_specs, out_specs=tuple(gather.out_specs),
        out_shape=tuple(gather.out_shape), scratch_shapes=gather.scratch, compiler_params=_params())(*shards)


def sibling_swap(srcs, halves, name):
    n = len(srcs)
    row_axis = [s.ndim - 2 for s in srcs]
    out_shapes = [s.shape[:ax] + (s.shape[ax] // 2,) + s.shape[ax + 1:] if halves else s.shape
                  for s, ax in zip(srcs, row_axis)]

    def body(*refs):
        x_refs, o_refs = refs[:n], refs[n:2 * n]
        send_sems, recv_sems = refs[2 * n:]
        x, y, c = _coords()
        copies = []
        for a in range(n):
            part = x_refs[a]
            if halves:
                rh = srcs[a].shape[row_axis[a]] // 2
                theirs = pl.ds(pl.multiple_of((1 - c) * rh, 16), rh)
                part = part.at[:, theirs] if row_axis[a] == 1 else part.at[theirs]
            cp = pltpu.make_async_remote_copy(src_ref=part, dst_ref=o_refs[a], send_sem=send_sems.at[a],
                                              recv_sem=recv_sems.at[a], device_id=(x, y, 1 - c), device_id_type=MESH)
            cp.start()
            copies.append(cp)
        for cp in copies:
            cp.wait()

    return pl.pallas_call(
        body, name=name, in_specs=[ANY] * n, out_specs=tuple([ANY] * n),
        out_shape=tuple(jax.ShapeDtypeStruct(sh, s.dtype) for sh, s in zip(out_shapes, srcs)),
        scratch_shapes=[pltpu.SemaphoreType.DMA((n,)), pltpu.SemaphoreType.DMA((n,))],
        compiler_params=_params())(*srcs)


class ChipScatter:
    def __init__(self, parts):
        self.n = len(parts)
        self.in_specs = [ANY] * self.n
        self.out_specs = [ANY] * self.n
        self.out_shape = [jax.ShapeDtypeStruct((3,) + p.shape[1:], p.dtype) for p in parts]
        self.scratch = [pltpu.SemaphoreType.DMA((3 * self.n,)), pltpu.SemaphoreType.DMA((3 * self.n,))] if self.n else []

    def _copies(self, x_refs, o_refs, sems):
        send_sems, recv_sems = sems
        x, y, c = _coords()
        return [pltpu.make_async_remote_copy(src_ref=x_refs[a].at[2 * px + py], dst_ref=o_refs[a].at[j],
                                             send_sem=send_sems.at[3 * a + j], recv_sem=recv_sems.at[3 * a + j],
                                             device_id=(px, py, c), device_id_type=MESH)
                for j, (px, py) in enumerate(_other_chips(x, y)) for a in range(self.n)]

    def start(self, x_refs, o_refs, sems):
        if self.n:
            for cp in self._copies(x_refs, o_refs, sems):
                cp.start()

    def finish(self, x_refs, o_refs, sems):
        if self.n:
            for cp in self._copies(x_refs, o_refs, sems):
                cp.wait()


def scatter_chips(parts, name):
    scatter = ChipScatter(parts)
    n = scatter.n

    def body(*refs):
        x_refs, o_refs, sems = refs[:n], refs[n:2 * n], refs[2 * n:]
        scatter.start(x_refs, o_refs, sems)
        scatter.finish(x_refs, o_refs, sems)

    return pl.pallas_call(
        body, name=name, in_specs=scatter.in_specs, out_specs=tuple(scatter.out_specs),
        out_shape=tuple(scatter.out_shape), scratch_shapes=scatter.scratch, compiler_params=_params())(*parts)


HBM = pl.BlockSpec(memory_space=pltpu.HBM)
SEM = pl.BlockSpec(memory_space=pltpu.SEMAPHORE)
SIDE_EFFECT = pltpu.SideEffectType.DATAFLOW_SIDE_EFFECTING


def _scatter_copies(x_refs, land_refs, send_sems, recv_sems):
    x, y, c = _coords()
    n = len(x_refs)
    return [pltpu.make_async_remote_copy(src_ref=x_refs[a].at[2 * px + py], dst_ref=land_refs[a].at[j],
                                         send_sem=send_sems[3 * a + j], recv_sem=recv_sems[3 * a + j],
                                         device_id=(px, py, c), device_id_type=MESH)
            for j, (px, py) in enumerate(_other_chips(x, y)) for a in range(n)]


def scatter_start(parts, name):
    n = len(parts)
    k = 3 * n
    lands = [lax.empty((3,) + p.shape[1:], p.dtype) for p in parts]

    def body(*refs):
        x_refs, land_refs = refs[:n], refs[n:2 * n]
        send_sems, recv_sems = refs[2 * n:2 * n + k], refs[2 * n + k:2 * n + 2 * k]
        token = refs[-1]
        for cp in _scatter_copies(x_refs, land_refs, send_sems, recv_sems):
            cp.start()
        token[...] = jnp.zeros_like(token)

    hbm = lambda a: pltpu.HBM(a.shape, a.dtype)
    res = pl.pallas_call(
        body, name=name, in_specs=[HBM] * (2 * n),
        out_specs=tuple([SEM] * (2 * k) + [HBM] * (2 * n) + [pl.BlockSpec(memory_space=pltpu.VMEM)]),
        out_shape=tuple([pltpu.SemaphoreType.DMA(())] * (2 * k) + [hbm(p) for p in parts] + [hbm(b) for b in lands]
                        + [jax.ShapeDtypeStruct((8, 128), F32)]),
        input_output_aliases={i: 2 * k + i for i in range(2 * n)},
        compiler_params=pltpu.CompilerParams(has_side_effects=SIDE_EFFECT, vmem_limit_bytes=VMEM_LIMIT))(
            *[pltpu.with_memory_space_constraint(a, pltpu.HBM) for a in list(parts) + lands])
    return list(res[:2 * k]), list(res[2 * k:2 * k + n]), list(res[2 * k + n:2 * k + 2 * n]), res[-1]


def scatter_wait(sems, parts_thru, lands_thru, after, name):
    n = len(parts_thru)
    k = 3 * n

    def body(*refs):
        x_refs, land_refs = refs[:n], refs[n:2 * n]
        send_sems, recv_sems = refs[2 * n:2 * n + k], refs[2 * n + k:2 * n + 2 * k]
        for cp in _scatter_copies(x_refs, land_refs, send_sems, recv_sems):
            cp.wait_send()
            cp.wait_recv()

    hbm = lambda a: pltpu.HBM(a.shape, a.dtype)
    res = pl.pallas_call(
        body, name=name, in_specs=[HBM] * (2 * n) + [SEM] * (2 * k) + [ANY],
        out_specs=tuple([HBM] * (2 * n)), out_shape=tuple(hbm(a) for a in list(parts_thru) + list(lands_thru)),
        input_output_aliases={i: i for i in range(2 * n)},
        compiler_params=pltpu.CompilerParams(has_side_effects=SIDE_EFFECT, vmem_limit_bytes=VMEM_LIMIT))(
            *parts_thru, *lands_thru, *sems, after)
    return list(res[n:])


def _row_block(rows):
    return max(b for b in range(16, 257, 16) if rows % b == 0)


def chip_sum(gpack, recv, core, name):
    n, R, W = gpack.shape
    rh = R // 2
    rb = _row_block(rh)
    nb = rh // rb

    def body(c_ref, g_ref, r_ref, o_ref, ob_ref):
        s = g_ref[...] + r_ref[...]
        o_ref[...] = s
        ob_ref[...] = s.astype(BF16)

    blk = pl.BlockSpec((1, rb, W), lambda i, j, c_ref: (i, j, 0))
    spec = pltpu.PrefetchScalarGridSpec(
        num_scalar_prefetch=1, grid=(n, nb),
        in_specs=[pl.BlockSpec((1, rb, W), lambda i, j, c_ref: (i, c_ref[0] * nb + j, 0)), blk],
        out_specs=(blk, blk))
    return pl.pallas_call(
        body, name=name, grid_spec=spec,
        out_shape=(jax.ShapeDtypeStruct((n, rh, W), F32), jax.ShapeDtypeStruct((n, rh, W), BF16)),
        compiler_params=_params(("arbitrary", "arbitrary")))(core, gpack, recv)


def chip_sum_cols(gate, up, recv_gate, recv_up, core, name):
    R, W = gate.shape
    cw = W // 2
    rh = R // 2
    rb = _row_block(rh)
    nb = rh // rb

    def body(c_ref, g_ref, u_ref, rg_ref, ru_ref, o_ref, ob_ref):
        s = jnp.where(pl.program_id(0) < 2, g_ref[...] + rg_ref[...], u_ref[...] + ru_ref[...])
        o_ref[0] = s
        ob_ref[0] = s.astype(BF16)

    gate_col = lambda s: jnp.minimum(s, 1)
    up_col = lambda s: jnp.maximum(s - 2, 0)
    out = pl.BlockSpec((1, rb, cw), lambda s, j, c_ref: (s, j, 0))
    spec = pltpu.PrefetchScalarGridSpec(
        num_scalar_prefetch=1, grid=(N_CHIPS, nb),
        in_specs=[pl.BlockSpec((rb, cw), lambda s, j, c_ref: (c_ref[0] * nb + j, gate_col(s))),
                  pl.BlockSpec((rb, cw), lambda s, j, c_ref: (c_ref[0] * nb + j, up_col(s))),
                  pl.BlockSpec((rb, cw), lambda s, j, c_ref: (j, gate_col(s))),
                  pl.BlockSpec((rb, cw), lambda s, j, c_ref: (j, up_col(s)))],
        out_specs=(out, out))
    return pl.pallas_call(
        body, name=name, grid_spec=spec,
        out_shape=(jax.ShapeDtypeStruct((N_CHIPS, rh, cw), F32), jax.ShapeDtypeStruct((N_CHIPS, rh, cw), BF16)),
        compiler_params=_params(("arbitrary", "arbitrary")))(core, gate, up, recv_gate, recv_up)


def chip_sums_of(items, core, tag):
    parts = []
    for _, _, g in items:
        parts += list(g) if isinstance(g, tuple) else [g]
    swapped = iter(zip(parts, sibling_swap(parts, True, f"grad_swap_cores_{tag}")))
    core_arg = core.reshape(1).astype(jnp.int32)
    sums = []
    for n, l, g in items:
        if isinstance(g, tuple):
            (dwg, from_g), (dwu, from_u) = next(swapped), next(swapped)
            sums.append(chip_sum_cols(dwg, dwu, from_g, from_u, core_arg, f"grad_chip_sum_{n}{l}"))
        else:
            p, r = next(swapped)
            sums.append(chip_sum(p, r, core_arg, f"grad_chip_sum_{n}{l}"))
    return sums


def shard_sum(own, recv, name):
    R, W = own.shape
    rb = _row_block(R)

    def body(a_ref, r_ref, o_ref):
        acc = a_ref[...]
        for j in range(3):
            acc = acc + r_ref[j].astype(F32)
        o_ref[...] = acc

    return pl.pallas_call(
        body, name=name, grid=(R // rb,),
        in_specs=[pl.BlockSpec((rb, W), lambda i: (i, 0)), pl.BlockSpec((3, rb, W), lambda i: (0, i, 0))],
        out_specs=pl.BlockSpec((rb, W), lambda i: (i, 0)), out_shape=jax.ShapeDtypeStruct((R, W), F32),
        compiler_params=_params(("arbitrary",)))(own, recv)


def adamw(w, m, v, g, name):
    L, R, C = w.shape
    rb = max(b for b in range(8, 257, 8) if R % b == 0)
    bc1 = 1.0 - ADAM_B1 ** ADAM_STEP
    bc2 = 1.0 - ADAM_B2 ** ADAM_STEP

    def body(w_ref, m_ref, v_ref, g_ref, d_ref, nm_ref, nv_ref):
        gv = g_ref[...]
        nm = ADAM_B1 * m_ref[...] + (1.0 - ADAM_B1) * gv
        nv = ADAM_B2 * v_ref[...] + (1.0 - ADAM_B2) * (gv * gv)
        d_ref[...] = -ADAM_LR * ((nm / bc1) / (jnp.sqrt(nv / bc2) + ADAM_EPS) + ADAM_WD * w_ref[...])
        nm_ref[...] = nm
        nv_ref[...] = nv

    blk = pl.BlockSpec((1, rb, C), lambda l, i: (l, i, 0))
    sh = jax.ShapeDtypeStruct((L, R, C), F32)
    return pl.pallas_call(body, name=name, grid=(L, R // rb), in_specs=[blk] * 4, out_specs=(blk,) * 3,
                          out_shape=(sh, sh, sh), compiler_params=_params(("arbitrary", "arbitrary")))(w, m, v, g)


SMALL = [n for n in WEIGHTS if n not in BIG]


PACK_TILE = 8 * 128


def _pack(arrays):
    blocks = []
    for a in arrays:
        flat = a.reshape(-1)
        flat = jnp.pad(flat, (0, -flat.size % PACK_TILE))
        blocks.append(flat.reshape(-1, 128))
    rows = sum(b.shape[0] for b in blocks)
    if rows % 16:
        blocks.append(jnp.zeros((8, 128), arrays[0].dtype))
    return jnp.concatenate(blocks, axis=0)


def _unpack(pack, shapes):
    out, row = [], 0
    for shape in shapes:
        size = int(np.prod(shape))
        rows = -(-size // PACK_TILE) * 8
        out.append(pack[row:row + rows].reshape(-1)[:size].reshape(shape))
        row += rows
    return out


def _pad_lanes(a):
    return jnp.pad(a, ((0, 0), (0, -a.shape[1] % 128)))


def _local_shard(full, axis, chip):
    size = full.shape[axis] // N_CHIPS
    return lax.dynamic_slice_in_dim(full, chip * size, size, axis)


def kernel(x, ffn1_norm, ffn1_wi, ffn1_wo, mix_norm, w_in, w_out, lru_conv_w, lru_conv_b, lru_gate_a_w, lru_gate_a_b, lru_gate_x_w, lru_gate_x_b, lru_lambda, lru_out_norm, rwkv_mu, rwkv_w_up, rwkv_w_bias, rwkv_a_up, rwkv_a_bias, rwkv_g_up, rwkv_k_k, rwkv_k_a, rwkv_r_k, rwkv_ln_g, rwkv_ln_b, rwkv_vres_w1, rwkv_vres_w2, rwkv_vres_b, gdn_conv_w, gdn_a_log, gdn_dt_bias, gdn_norm, ffn2_norm, ffn2_wi, ffn2_wo, final_norm, loss_target, m_ffn1_norm, m_ffn1_wi, m_ffn1_wo, m_mix_norm, m_w_in, m_w_out, m_lru_conv_w, m_lru_conv_b, m_lru_gate_a_w, m_lru_gate_a_b, m_lru_gate_x_w, m_lru_gate_x_b, m_lru_lambda, m_lru_out_norm, m_rwkv_mu, m_rwkv_w_up, m_rwkv_w_bias, m_rwkv_a_up, m_rwkv_a_bias, m_rwkv_g_up, m_rwkv_k_k, m_rwkv_k_a, m_rwkv_r_k, m_rwkv_ln_g, m_rwkv_ln_b, m_rwkv_vres_w1, m_rwkv_vres_w2, m_rwkv_vres_b, m_gdn_conv_w, m_gdn_a_log, m_gdn_dt_bias, m_gdn_norm, m_ffn2_norm, m_ffn2_wi, m_ffn2_wo, m_final_norm, v_ffn1_norm, v_ffn1_wi, v_ffn1_wo, v_mix_norm, v_w_in, v_w_out, v_lru_conv_w, v_lru_conv_b, v_lru_gate_a_w, v_lru_gate_a_b, v_lru_gate_x_w, v_lru_gate_x_b, v_lru_lambda, v_lru_out_norm, v_rwkv_mu, v_rwkv_w_up, v_rwkv_w_bias, v_rwkv_a_up, v_rwkv_a_bias, v_rwkv_g_up, v_rwkv_k_k, v_rwkv_k_a, v_rwkv_r_k, v_rwkv_ln_g, v_rwkv_ln_b, v_rwkv_vres_w1, v_rwkv_vres_w2, v_rwkv_vres_b, v_gdn_conv_w, v_gdn_a_log, v_gdn_dt_bias, v_gdn_norm, v_ffn2_norm, v_ffn2_wi, v_ffn2_wo, v_final_norm):
    args = locals()
    w_loc = {n: args[n] for n in WEIGHTS}
    m_loc = {n: args['m_' + n] for n in WEIGHTS}
    v_loc = {n: args['v_' + n] for n in WEIGHTS}
    chip = 2 * lax.axis_index("x") + lax.axis_index("y")
    core = lax.axis_index("c")

    big = [(n, l) for n in BIG for l in range(N_LAYERS)]
    shards = {(n, l): _pad_lanes(w_loc[n][l].astype(BF16)) for n, l in big}
    first = [('ffn1_wi', 0), ('ffn1_wo', 0)]
    wb = {n: [None] * N_LAYERS for n in BIG}
    for (n, l), g in zip(first, allgather_chips([shards[k] for k in first], "allgather_first")):
        wb[n][l] = _natural(n, g)

    sm_names = list(SMALL_SHARDED)
    placed = []
    for n in sm_names:
        mine = [jnp.where((chip == s) & (core == 0), w_loc[n], 0.0) for s in range(N_CHIPS)]
        placed.append(jnp.concatenate(mine, axis=SMALL_SHARDED[n]))
    summed = allreduce_small(_pack(placed), "allgather_small")
    w_full = dict(w_loc)
    w_full.update(zip(sm_names, _unpack(summed, [p.shape for p in placed])))

    loss, dx, grads = local_step(x[0], loss_target[0], w_full, wb, shards)
    loss = lax.psum(loss, ("x", "y", "c"))

    gsum = allreduce_small(_pack([grads[n] for n in SMALL]), "allreduce_small")
    g_loc = {}
    for n, g in zip(SMALL, _unpack(gsum, [grads[n].shape for n in SMALL])):
        g_loc[n] = _local_shard(g, SMALL_SHARDED[n], chip) if n in SMALL_SHARDED else g

    last_keys, last_parts = grads['last_round']
    sems, parts_thru, lands_thru, token = scatter_start(last_parts, "grad_scatter_last_start")
    chip_after_start = chip + token[0, 0].astype(chip.dtype)
    rows = {n: [None] * N_LAYERS for n in BIG}
    delta, new_m, new_v = {}, {}, {}

    def finish(keys, arrived, which_chip, tag):
        halves = [shard_sum(lax.dynamic_index_in_dim(grads[n][l][0], which_chip, 0, keepdims=False), got,
                            f"grad_shard_sum_{n}{l}") for (n, l), got in zip(keys, arrived)]
        others = sibling_swap(halves, False, f"grad_share_cores_{tag}")
        for (n, l), half, other in zip(keys, halves, others):
            lower = jnp.where(core == 0, half, other)
            upper = jnp.where(core == 0, other, half)
            rows[n][l] = jnp.concatenate([lower, upper], axis=0)[:, :w_loc[n].shape[-1]]
        for n in BIG:
            if n not in delta and all(r is not None for r in rows[n]):
                g_loc[n] = jnp.stack(rows[n])
                delta[n], new_m[n], new_v[n] = adamw(w_loc[n], m_loc[n], v_loc[n], g_loc[n], f"adamw_{n}")

    early = [k for k in big if k not in last_keys]
    finish(early, [grads[n][l][1] for n, l in early], chip_after_start, "early")
    pack = lambda d: _pack([d[n] for n in SMALL])[None]
    res = adamw(pack(w_loc), pack(m_loc), pack(v_loc), pack(g_loc), "adamw_small")
    for dst, r in zip((delta, new_m, new_v), res):
        dst.update(zip(SMALL, _unpack(r[0], [w_loc[n].shape for n in SMALL])))
    arrived_last = scatter_wait(sems, parts_thru, lands_thru, delta['ffn2_wi'], "grad_scatter_last_wait")
    finish(last_keys, arrived_last, chip, "last")

    return (loss, dx[None], *[g_loc[n] for n in WEIGHTS], *[delta[n] for n in WEIGHTS],
            *[new_m[n] for n in WEIGHTS], *[new_v[n] for n in WEIGHTS])
```

```python
import functools

import numpy as np
import jax
import jax.numpy as jnp
from jax import lax
from jax.experimental import pallas as pl
from jax.experimental.pallas import tpu as pltpu

F32 = jnp.float32
BF16 = jnp.bfloat16
MESH = pl.DeviceIdType.MESH

D_MODEL = 1024
D_FF = 2816
N_LAYERS = 2
HEADS = 6
HEAD_DIM = 64
MIX_W = HEADS * HEAD_DIM
LRU_W = 256
LRU_BLOCKS = 4
D_IN = 3468
D_IN_PAD = 3584
NORM_EPS = 1e-6
GN_EPS = 64e-5
LRU_C = 8.0
CHUNK = 64
ROWS = 512
FF_CHUNK = 256
IN_CHUNK = 512
VMEM_LIMIT = 56 * 1024 * 1024

ADAM_LR, ADAM_B1, ADAM_B2, ADAM_EPS, ADAM_WD, ADAM_STEP = 0.001, 0.9, 0.999, 1e-08, 0.01, 10

WEIGHTS = ['ffn1_norm', 'ffn1_wi', 'ffn1_wo', 'mix_norm', 'w_in', 'w_out', 'lru_conv_w', 'lru_conv_b',
           'lru_gate_a_w', 'lru_gate_a_b', 'lru_gate_x_w', 'lru_gate_x_b', 'lru_lambda', 'lru_out_norm',
           'rwkv_mu', 'rwkv_w_up', 'rwkv_w_bias', 'rwkv_a_up', 'rwkv_a_bias', 'rwkv_g_up', 'rwkv_k_k',
           'rwkv_k_a', 'rwkv_r_k', 'rwkv_ln_g', 'rwkv_ln_b', 'rwkv_vres_w1', 'rwkv_vres_w2', 'rwkv_vres_b',
           'gdn_conv_w', 'gdn_a_log', 'gdn_dt_bias', 'gdn_norm', 'ffn2_norm', 'ffn2_wi', 'ffn2_wo', 'final_norm']
BIG = {'ffn1_wi': 2, 'ffn1_wo': 1, 'w_in': 2, 'w_out': 1, 'ffn2_wi': 2, 'ffn2_wo': 1}
SMALL_SHARDED = {'lru_conv_w': 2, 'rwkv_w_up': 2, 'rwkv_a_up': 2, 'rwkv_g_up': 2, 'rwkv_vres_w1': 1,
                 'rwkv_vres_w2': 2, 'gdn_conv_w': 2}
N_CHIPS = 4


def _params(sem=None):
    kw = dict(vmem_limit_bytes=VMEM_LIMIT)
    if sem is not None:
        kw['dimension_semantics'] = sem
    return pltpu.CompilerParams(**kw)


def _bdot(a, b, dims=(((1,), (0,)), ((), ()))):
    return lax.dot_general(a.astype(BF16), b.astype(BF16), dims, preferred_element_type=F32)


def _bdot_nt(a, b):
    return _bdot(a, b, (((1,), (1,)), ((), ())))


def _bdot_tn(a, b):
    return _bdot(a, b, (((0,), (0,)), ((), ())))


_DIMS = {'nn': (((1,), (0,)), ((), ())), 'nt': (((1,), (1,)), ((), ())), 'tn': (((0,), (0,)), ((), ()))}


def _split(a, terms):
    parts = []
    for _ in range(terms - 1):
        hi = a.astype(BF16)
        parts.append(hi)
        a = a - hi.astype(F32)
    parts.append(a.astype(BF16))
    return parts


_BATCH_DIMS = {'nn': (((2,), (1,)), ((0,), (0,))), 'nt': (((2,), (2,)), ((0,), (0,))),
               'tn': (((1,), (1,)), ((0,), (0,)))}


def _dot3(a, b, kind):
    ah, al = _split(a, 2)
    bh, bl = _split(b, 2)
    dims = _BATCH_DIMS[kind] if a.ndim == 3 else _DIMS[kind]
    d = lambda p, q: lax.dot_general(p, q, dims, preferred_element_type=F32)
    return d(ah, bh) + (d(ah, bl) + d(al, bh))


@functools.partial(jax.custom_vjp, nondiff_argnums=(2,))
def _cdot_k(a, b, kind):
    return _dot3(a, b, kind)


def _cdot_k_fwd(a, b, kind):
    return _dot3(a, b, kind), (a, b)


def _cdot_k_bwd(kind, res, ct):
    a, b = res
    if kind == 'nn':
        return _dot3(ct, b, 'nt'), _dot3(a, ct, 'tn')
    if kind == 'nt':
        return _dot3(ct, b, 'nn'), _dot3(ct, a, 'tn')
    return _dot3(b, ct, 'nt'), _dot3(a, ct, 'nn')


_cdot_k.defvjp(_cdot_k_fwd, _cdot_k_bwd)


def _dot1(a, b, kind):
    dims = _BATCH_DIMS[kind] if a.ndim == 3 else _DIMS[kind]
    return lax.dot_general(a.astype(BF16), b.astype(BF16), dims, preferred_element_type=F32)


@functools.partial(jax.custom_vjp, nondiff_argnums=(2,))
def _cdot1_k(a, b, kind):
    return _dot1(a, b, kind)


def _cdot1_k_fwd(a, b, kind):
    return _dot1(a, b, kind), (a, b)


def _cdot1_k_bwd(kind, res, ct):
    a, b = res
    if kind == 'nn':
        return _dot1(ct, b, 'nt'), _dot1(a, ct, 'tn')
    if kind == 'nt':
        return _dot1(ct, b, 'nn'), _dot1(ct, a, 'tn')
    return _dot1(b, ct, 'nt'), _dot1(a, ct, 'nn')


_cdot1_k.defvjp(_cdot1_k_fwd, _cdot1_k_bwd)


def _cdot(a, b):
    return _cdot1_k(a, b, 'nn')


def _cdot_nt(a, b):
    return _cdot1_k(a, b, 'nt')


def _cdot_tn(a, b):
    return _cdot1_k(a, b, 'tn')


def _hdot(a, b):
    return _cdot_k(a, b, 'nn')


def _dot_exact(x, m01, kind):
    d = lambda p: lax.dot_general(p, m01.astype(BF16), _DIMS[kind], preferred_element_type=F32)
    hi, mid, lo = _split(x, 3)
    return d(hi) + (d(mid) + d(lo))


@functools.partial(jax.custom_vjp, nondiff_argnums=(1,))
def _xdot(x, make_m):
    return _dot_exact(x, make_m(), 'nn')


def _xdot_fwd(x, make_m):
    return _dot_exact(x, make_m(), 'nn'), None


def _xdot_bwd(make_m, _, ct):
    return (_dot_exact(ct, make_m(), 'nt'),)


_xdot.defvjp(_xdot_fwd, _xdot_bwd)


def _iota2(n, m):
    return lax.broadcasted_iota(jnp.int32, (n, m), 0), lax.broadcasted_iota(jnp.int32, (n, m), 1)


def _head_blocks(w):
    ri, ci = _iota2(w, w)
    return (ri // HEAD_DIM == ci // HEAD_DIM).astype(F32)


def _segsum(x):
    return _xdot(x, functools.partial(_head_blocks, x.shape[-1]))


def _cumsum_rows(x):
    return _cumsum_k(x, x.shape[0])


@functools.partial(jax.custom_vjp, nondiff_argnums=(1,))
def _cumsum_k(x, n):
    return _lower_dot(x, n, False)


def _lower_dot(x, n, transpose):
    ri, ci = _iota2(n, n)
    m = ((ri <= ci) if transpose else (ri >= ci)).astype(BF16)
    d = lambda p: lax.dot_general(m, p, _DIMS['nn'], preferred_element_type=F32)
    hi, mid, lo = _split(x, 3)
    return d(hi) + (d(mid) + d(lo))


def _cumsum_k_fwd(x, n):
    return _lower_dot(x, n, False), None


def _cumsum_k_bwd(n, _, ct):
    return (_lower_dot(ct, n, True),)


_cumsum_k.defvjp(_cumsum_k_fwd, _cumsum_k_bwd)


def _rms(x, g):
    return x * lax.rsqrt(jnp.mean(x * x, axis=-1, keepdims=True) + NORM_EPS) * g


DENSE_ROWS = 1024


def _row_loop(n_rows, fn):
    rows = min(DENSE_ROWS, n_rows)

    def step(i, c):
        fn(pl.ds(pl.multiple_of(i * rows, rows), rows))
        return c
    lax.fori_loop(0, n_rows // rows, step, 0)


def ffn_fwd(x, g, wi, wo, name, hosted=()):
    T = x.shape[0]
    nj = D_FF // FF_CHUNK
    gather = ChipGather(list(hosted))
    n = gather.n

    def body(*refs):
        x_ref, g_ref, wg_ref, wu_ref, wo_ref = refs[:5]
        hx, o_ref, ho = refs[5:5 + n], refs[5 + n], refs[6 + n:6 + 2 * n]
        h_ref, acc_ref = refs[6 + 2 * n:8 + 2 * n]
        sems = refs[8 + 2 * n:]
        j = pl.program_id(0)

        @pl.when(j == 0)
        def _():
            gather.start(hx, ho, sems)

            def init(r):
                h_ref[r, :] = _rms(x_ref[r, :], g_ref[...]).astype(BF16)
                acc_ref[r, :] = jnp.zeros((r.size, D_MODEL), F32)
            _row_loop(T, init)

        def blk(r):
            hb = h_ref[r, :]
            gate = jnp.dot(hb, wg_ref[...], preferred_element_type=F32)
            up = jnp.dot(hb, wu_ref[...], preferred_element_type=F32)
            a = (gate * jax.nn.sigmoid(gate) * up).astype(BF16)
            acc_ref[r, :] += jnp.dot(a, wo_ref[...], preferred_element_type=F32)
        _row_loop(T, blk)

        @pl.when(j == nj - 2)
        def _():
            gather.relay(hx, ho, sems)

        @pl.when(j == nj - 1)
        def _():
            def fin(r):
                o_ref[r, :] = x_ref[r, :] + 0.5 * acc_ref[r, :]
            _row_loop(T, fin)
            gather.finish(hx, ho, sems)

    full = pl.BlockSpec((T, D_MODEL), lambda j: (0, 0))
    res = pl.pallas_call(
        body, name=name, grid=(nj,),
        in_specs=[full, pl.BlockSpec((1, D_MODEL), lambda j: (0, 0)),
                  pl.BlockSpec((D_MODEL, FF_CHUNK), lambda j: (0, j)),
                  pl.BlockSpec((D_MODEL, FF_CHUNK), lambda j: (0, j + nj)),
                  pl.BlockSpec((FF_CHUNK, D_MODEL), lambda j: (j, 0))] + gather.in_specs,
        out_specs=tuple([full] + gather.out_specs),
        out_shape=tuple([jax.ShapeDtypeStruct((T, D_MODEL), F32)] + gather.out_shape),
        scratch_shapes=[pltpu.VMEM((T, D_MODEL), BF16), pltpu.VMEM((T, D_MODEL), F32)] + gather.scratch,
        compiler_params=_params(("arbitrary",)))(x, g, wi, wi, wo, *hosted)
    return res[0], list(res[1:])


def _norm_bwd_rows(x, g, dh, dres):
    rstd = lax.rsqrt(jnp.mean(x * x, axis=-1, keepdims=True) + NORM_EPS)
    xh = x * rstd
    dxh = dh * g
    dx = rstd * (dxh - xh * jnp.mean(dxh * xh, axis=-1, keepdims=True))
    return dres + dx, jnp.sum(dh * xh, axis=0, keepdims=True)


def ffn_bwd(x, dy, g, wi, wo, name, hosted=()):
    T = x.shape[0]
    nj = D_FF // FF_CHUNK
    scatter = ChipScatter(list(hosted))
    n = scatter.n

    def body(*refs):
        x_ref, dy_ref, g_ref, wg_ref, wu_ref, wo_ref = refs[:6]
        hx = refs[6:6 + n]
        dx_ref, dg_ref, dwg_ref, dwu_ref, dwo_ref = refs[6 + n:11 + n]
        ho = refs[11 + n:11 + 2 * n]
        h_ref, da_ref, dh_ref = refs[11 + 2 * n:14 + 2 * n]
        sems = refs[14 + 2 * n:]
        j = pl.program_id(0)

        @pl.when(j == 0)
        def _():
            scatter.start(hx, ho, sems)

            def init(r):
                h_ref[r, :] = _rms(x_ref[r, :], g_ref[...]).astype(BF16)
                da_ref[r, :] = (0.5 * dy_ref[r, :]).astype(BF16)
                dh_ref[r, :] = jnp.zeros((r.size, D_MODEL), F32)
            _row_loop(T, init)

        dwg_ref[...] = jnp.zeros_like(dwg_ref)
        dwu_ref[...] = jnp.zeros_like(dwu_ref)
        dwo_ref[...] = jnp.zeros_like(dwo_ref)

        def blk(r):
            hb = h_ref[r, :]
            db = da_ref[r, :]
            gate = jnp.dot(hb, wg_ref[...], preferred_element_type=F32)
            up = jnp.dot(hb, wu_ref[...], preferred_element_type=F32)
            sg = jax.nn.sigmoid(gate)
            sl = gate * sg
            da = _bdot_nt(db, wo_ref[...])
            dup = (da * sl).astype(BF16)
            dgate = (da * up * (sg * (1.0 + gate * (1.0 - sg)))).astype(BF16)
            dwo_ref[...] += _bdot_tn((sl * up).astype(BF16), db)
            dwg_ref[...] += _bdot_tn(hb, dgate)
            dwu_ref[...] += _bdot_tn(hb, dup)
            dh_ref[r, :] += _bdot_nt(dgate, wg_ref[...]) + _bdot_nt(dup, wu_ref[...])
        _row_loop(T, blk)

        @pl.when(j == nj - 1)
        def _():
            dg_ref[...] = jnp.zeros_like(dg_ref)

            def fin(r):
                dx, dg = _norm_bwd_rows(x_ref[r, :], g_ref[...], dh_ref[r, :], dy_ref[r, :])
                dx_ref[r, :] = dx
                dg_ref[...] += dg
            _row_loop(T, fin)
            scatter.finish(hx, ho, sems)

    full = pl.BlockSpec((T, D_MODEL), lambda j: (0, 0))
    vec = pl.BlockSpec((1, D_MODEL), lambda j: (0, 0))
    res = pl.pallas_call(
        body, name=name, grid=(nj,),
        in_specs=[full, full, vec,
                  pl.BlockSpec((D_MODEL, FF_CHUNK), lambda j: (0, j)),
                  pl.BlockSpec((D_MODEL, FF_CHUNK), lambda j: (0, j + nj)),
                  pl.BlockSpec((FF_CHUNK, D_MODEL), lambda j: (j, 0))] + scatter.in_specs,
        out_specs=tuple([full, vec,
                         pl.BlockSpec((D_MODEL, FF_CHUNK), lambda j: (0, j)),
                         pl.BlockSpec((D_MODEL, FF_CHUNK), lambda j: (0, j)),
                         pl.BlockSpec((FF_CHUNK, D_MODEL), lambda j: (j, 0))] + scatter.out_specs),
        out_shape=tuple([jax.ShapeDtypeStruct((T, D_MODEL), F32), jax.ShapeDtypeStruct((1, D_MODEL), F32),
                         jax.ShapeDtypeStruct((D_MODEL, D_FF), F32), jax.ShapeDtypeStruct((D_MODEL, D_FF), F32),
                         jax.ShapeDtypeStruct((D_FF, D_MODEL), F32)] + scatter.out_shape),
        scratch_shapes=[pltpu.VMEM((T, D_MODEL), BF16), pltpu.VMEM((T, D_MODEL), BF16),
                        pltpu.VMEM((T, D_MODEL), F32)] + scatter.scratch,
        compiler_params=_params(("arbitrary",)))(x, dy, g, wi, wi, wo, *hosted)
    return res[0], res[1], res[2], res[3], res[4], list(res[5:])


def proj_fwd(x, g, w, name):
    T = x.shape[0]
    nj = D_IN_PAD // IN_CHUNK

    def body(x_ref, g_ref, w_ref, o_ref, h_ref):
        @pl.when(pl.program_id(0) == 0)
        def _():
            def init(r):
                h_ref[r, :] = _rms(x_ref[r, :], g_ref[...]).astype(BF16)
            _row_loop(T, init)

        def blk(r):
            o_ref[r, :] = jnp.dot(h_ref[r, :], w_ref[...], preferred_element_type=F32)
        _row_loop(T, blk)

    return pl.pallas_call(
        body, name=name, grid=(nj,),
        in_specs=[pl.BlockSpec((T, D_MODEL), lambda j: (0, 0)), pl.BlockSpec((1, D_MODEL), lambda j: (0, 0)),
                  pl.BlockSpec((D_MODEL, IN_CHUNK), lambda j: (0, j))],
        out_specs=pl.BlockSpec((T, IN_CHUNK), lambda j: (0, j)),
        out_shape=jax.ShapeDtypeStruct((T, D_IN_PAD), F32),
        scratch_shapes=[pltpu.VMEM((T, D_MODEL), BF16)],
        compiler_params=_params(("arbitrary",)))(x, g, w)


def proj_bwd(x, dres, g, w, dp, name):
    T = x.shape[0]
    nj = D_IN_PAD // IN_CHUNK

    def body(x_ref, dres_ref, g_ref, w_ref, dp_ref, dx_ref, dg_ref, dw_ref, h_ref, dh_ref):
        j = pl.program_id(0)

        @pl.when(j == 0)
        def _():
            def init(r):
                h_ref[r, :] = _rms(x_ref[r, :], g_ref[...]).astype(BF16)
                dh_ref[r, :] = jnp.zeros((r.size, D_MODEL), F32)
            _row_loop(T, init)

        dw_ref[...] = jnp.zeros_like(dw_ref)

        def blk(r):
            dpb = dp_ref[r, :].astype(BF16)
            dw_ref[...] += _bdot_tn(h_ref[r, :], dpb)
            dh_ref[r, :] += _bdot_nt(dpb, w_ref[...])
        _row_loop(T, blk)

        @pl.when(j == nj - 1)
        def _():
            dg_ref[...] = jnp.zeros_like(dg_ref)

            def fin(r):
                dx, dg = _norm_bwd_rows(x_ref[r, :], g_ref[...], dh_ref[r, :], dres_ref[r, :])
                dx_ref[r, :] = dx
                dg_ref[...] += dg
            _row_loop(T, fin)

    full = pl.BlockSpec((T, D_MODEL), lambda j: (0, 0))
    vec = pl.BlockSpec((1, D_MODEL), lambda j: (0, 0))
    return pl.pallas_call(
        body, name=name, grid=(nj,),
        in_specs=[full, full, vec, pl.BlockSpec((D_MODEL, IN_CHUNK), lambda j: (0, j)),
                  pl.BlockSpec((T, IN_CHUNK), lambda j: (0, j))],
        out_specs=(full, vec, pl.BlockSpec((D_MODEL, IN_CHUNK), lambda j: (0, j))),
        out_shape=(jax.ShapeDtypeStruct((T, D_MODEL), F32), jax.ShapeDtypeStruct((1, D_MODEL), F32),
                   jax.ShapeDtypeStruct((D_MODEL, D_IN_PAD), F32)),
        scratch_shapes=[pltpu.VMEM((T, D_MODEL), BF16), pltpu.VMEM((T, D_MODEL), F32)],
        compiler_params=_params(("arbitrary",)))(x, dres, g, w, dp)


def out_fwd(mixed, w, x, name):
    T = x.shape[0]

    def body(m_ref, w_ref, x_ref, o_ref):
        o_ref[...] = x_ref[...] + jnp.dot(m_ref[...].astype(BF16), w_ref[...], preferred_element_type=F32)

    blk = pl.BlockSpec((ROWS, D_MODEL), lambda i: (i, 0))
    return pl.pallas_call(
        body, name=name, grid=(T // ROWS,),
        in_specs=[blk, pl.BlockSpec((D_MODEL, D_MODEL), lambda i: (0, 0)), blk],
        out_specs=blk, out_shape=jax.ShapeDtypeStruct((T, D_MODEL), F32),
        compiler_params=_params(("arbitrary",)))(mixed, w, x)


def out_bwd(mixed, w, dy, name):
    T = dy.shape[0]

    def body(m_ref, w_ref, dy_ref, dm_ref, dw_ref):
        @pl.when(pl.program_id(0) == 0)
        def _():
            dw_ref[...] = jnp.zeros_like(dw_ref)
        dyb = dy_ref[...].astype(BF16)
        dm_ref[...] = _bdot_nt(dyb, w_ref[...])
        dw_ref[...] += _bdot_tn(m_ref[...].astype(BF16), dyb)

    blk = pl.BlockSpec((ROWS, D_MODEL), lambda i: (i, 0))
    sq = pl.BlockSpec((D_MODEL, D_MODEL), lambda i: (0, 0))
    return pl.pallas_call(
        body, name=name, grid=(T // ROWS,),
        in_specs=[blk, sq, blk], out_specs=(blk, sq),
        out_shape=(jax.ShapeDtypeStruct((T, D_MODEL), F32), jax.ShapeDtypeStruct((D_MODEL, D_MODEL), F32)),
        compiler_params=_params(("arbitrary",)))(mixed, w, dy)


def loss_head(x, g, target, name):
    T = x.shape[0]

    def body(x_ref, g_ref, t_ref, loss_ref, dx_ref, dg_ref):
        @pl.when(pl.program_id(0) == 0)
        def _():
            loss_ref[...] = jnp.zeros_like(loss_ref)
            dg_ref[...] = jnp.zeros_like(dg_ref)
        xb = x_ref[...]
        rstd = lax.rsqrt(jnp.mean(xb * xb, axis=-1, keepdims=True) + NORM_EPS)
        xh = xb * rstd
        err = xh * g_ref[...] - t_ref[...]
        loss_ref[...] += 0.5 * jnp.sum(jnp.mean(err * err, axis=-1, keepdims=True), axis=0, keepdims=True)
        dy = err * (1.0 / D_MODEL)
        dg_ref[...] += jnp.sum(dy * xh, axis=0, keepdims=True)
        dxh = dy * g_ref[...]
        dx_ref[...] = rstd * (dxh - xh * jnp.mean(dxh * xh, axis=-1, keepdims=True))

    blk = pl.BlockSpec((ROWS, D_MODEL), lambda i: (i, 0))
    vec = pl.BlockSpec((1, D_MODEL), lambda i: (0, 0))
    return pl.pallas_call(
        body, name=name, grid=(T // ROWS,),
        in_specs=[blk, vec, blk], out_specs=(pl.BlockSpec((1, 1), lambda i: (0, 0)), blk, vec),
        out_shape=(jax.ShapeDtypeStruct((1, 1), F32), jax.ShapeDtypeStruct((T, D_MODEL), F32),
                   jax.ShapeDtypeStruct((1, D_MODEL), F32)),
        compiler_params=_params(("arbitrary",)))(x, g, target)


def rowwise_fwd(fn, rows, shared, out_widths, name):
    T = rows[0].shape[0]
    n_in = len(rows) + len(shared)

    def body(*refs):
        res = fn(*[r[...] for r in refs[:n_in]])
        for o, v in zip(refs[n_in:], res):
            o[...] = v

    in_specs = ([pl.BlockSpec((ROWS, a.shape[1]), lambda i: (i, 0)) for a in rows]
                + [pl.BlockSpec(a.shape, lambda i: (0, 0)) for a in shared])
    return pl.pallas_call(
        body, name=name, grid=(T // ROWS,), in_specs=in_specs,
        out_specs=tuple(pl.BlockSpec((ROWS, w), lambda i: (i, 0)) for w in out_widths),
        out_shape=tuple(jax.ShapeDtypeStruct((T, w), F32) for w in out_widths),
        compiler_params=_params(("arbitrary",)))(*rows, *shared)


def rowwise_bwd(fn, rows, shared, cts, name, ct_fn=None):
    T = rows[0].shape[0]
    nr, ns, nc = len(rows), len(shared), len(cts)

    def body(*refs):
        ins = [r[...] for r in refs[:nr + ns]]
        ctv = tuple(r[...] for r in refs[nr + ns:nr + ns + nc])
        outs = refs[nr + ns + nc:]
        _, vjp = jax.vjp(fn, *ins)
        grads = vjp(ct_fn(*ctv) if ct_fn is not None else ctv)
        for k in range(nr):
            outs[k][...] = grads[k]

        @pl.when(pl.program_id(0) == 0)
        def _():
            for k in range(ns):
                outs[nr + k][...] = jnp.zeros_like(outs[nr + k])
        for k in range(ns):
            outs[nr + k][...] += grads[nr + k]

    row_spec = lambda a: pl.BlockSpec((ROWS, a.shape[1]), lambda i: (i, 0))
    sh_spec = lambda a: pl.BlockSpec(a.shape, lambda i: (0, 0))
    return pl.pallas_call(
        body, name=name, grid=(T // ROWS,),
        in_specs=[row_spec(a) for a in rows] + [sh_spec(a) for a in shared] + [row_spec(a) for a in cts],
        out_specs=tuple([row_spec(a) for a in rows] + [sh_spec(a) for a in shared]),
        out_shape=tuple(jax.ShapeDtypeStruct(a.shape, F32) for a in list(rows) + list(shared)),
        compiler_params=_params(("arbitrary",)))(*rows, *shared, *cts)


def shift_rows(x, s):
    return jnp.pad(x, ((s, 0), (0, 0)))[:x.shape[0]]


def unshift_rows(x, s):
    return jnp.pad(x, ((0, s), (0, 0)))[s:]


def _neg_expm1(y):
    series = -(y * (1.0 + y * (0.5 + y * (1.0 / 6.0 + y * (1.0 / 24.0)))))
    return jnp.where(y > -0.05, series, 1.0 - jnp.exp(y))


def lru_pre_fn(x0, x1, x2, x3, first, w0, w1, w2, w3, cb, ga, gab, gx, gxb, lam):
    xc = w3 * x0 + w2 * x1 + w1 * x2 + w0 * x3 + cb
    r = jax.nn.sigmoid(_hdot(xc, ga) + gab)
    i = jax.nn.sigmoid(_hdot(xc, gx) + gxb)
    log_a = -LRU_C * r * jax.nn.softplus(-lam)
    a = jnp.exp(log_a)
    mult = jnp.where(first > 0.5, 1.0, jnp.sqrt(_neg_expm1(2.0 * log_a)))
    return a, mult * i * xc


def lru_post_fn(h, py, og):
    return (_rms(h * jax.nn.gelu(py), og),)


def lru_scan(a, b, reverse, name):
    T, C = a.shape
    nb = T // 8

    def body(a_ref, b_ref, h_ref):
        rows = lax.broadcasted_iota(jnp.int32, (8, C), 0)

        def blk(i, carry):
            j = nb - 1 - i if reverse else i
            r = pl.ds(pl.multiple_of(j * 8, 8), 8)
            A = a_ref[r, :]
            B = b_ref[r, :]
            for s in (1, 2, 4):
                if reverse:
                    keep = rows < 8 - s
                    sh = 8 - s
                else:
                    keep = rows >= s
                    sh = s
                Bs = jnp.where(keep, pltpu.roll(B, sh, 0), 0.0)
                As = jnp.where(keep, pltpu.roll(A, sh, 0), 1.0)
                B = B + A * Bs
                A = A * As
            hb = B + A * carry
            h_ref[r, :] = hb
            edge = 0 if reverse else 7
            return jnp.sum(jnp.where(rows == edge, hb, 0.0), axis=0, keepdims=True)

        lax.fori_loop(0, nb, blk, jnp.zeros((1, C), F32))

    full = pl.BlockSpec((T, C), lambda: (0, 0))
    return pl.pallas_call(body, name=name, in_specs=[full, full], out_specs=full,
                          out_shape=jax.ShapeDtypeStruct((T, C), F32), compiler_params=_params())(a, b)


def make_rwkv_pre_fn(has_vres):
    def fn(p, pp, *rest):
        if has_vres:
            vf, mu, w_up, w_b, a_up, a_b, g_up, kk_w, ka_w, vw1, vw2, vb = rest
        else:
            mu, w_up, w_b, a_up, a_b, g_up, kk_w, ka_w = rest
        xm = p + (pp - p) * mu
        r, k, v = xm[:, 0:384], xm[:, 384:768], xm[:, 768:1152]
        xw, xa, xg = xm[:, 1152:1216], xm[:, 1216:1280], xm[:, 1280:1408]
        w_log = -jax.nn.softplus(-(w_b + _hdot(jnp.tanh(xw), w_up))) - 0.5
        lw = -jnp.exp(w_log)
        a = jax.nn.sigmoid(a_b + _hdot(xa, a_up))
        g = _hdot(jax.nn.sigmoid(xg), g_up)
        if has_vres:
            v = v + (vf - v) * jax.nn.sigmoid(vb + _hdot(_hdot(v, vw1), vw2))
        kkx = k * kk_w
        kk = kkx * lax.rsqrt(_segsum(kkx * kkx) + 1e-6)
        k2 = k * (1.0 + (a - 1.0) * ka_w)
        return r, lw, k2, v, kk, a, g
    return fn


def rwkv_post_fn(y, r, k2, v, g, ln_g, ln_b, r_k):
    mean = _segsum(y) * (1.0 / HEAD_DIM)
    yc = y - mean
    var = _segsum(yc * yc) * (1.0 / HEAD_DIM)
    yn = yc * lax.rsqrt(var + GN_EPS) * ln_g + ln_b
    bonus = _segsum(r * k2 * r_k) * v
    return ((yn + bonus) * g,)


def _head_expander(first_lane):
    ri, ci = _iota2(128, MIX_W)
    return (ri == ci // HEAD_DIM + first_lane).astype(F32)


def gdn_pre_fn(x0, x1, x2, x3, ab, w0, w1, w2, w3, alog, dtb):
    qkv = jax.nn.silu(w3 * x0 + w2 * x1 + w1 * x2 + w0 * x3)
    q, k, v = qkv[:, 0:384], qkv[:, 384:768], qkv[:, 768:1152]
    q = q * lax.rsqrt(_segsum(q * q) + 1e-6) * (HEAD_DIM ** -0.5)
    k = k * lax.rsqrt(_segsum(k * k) + 1e-6)
    g = -jnp.exp(alog) * jax.nn.softplus(ab + dtb)
    beta = jax.nn.sigmoid(ab)
    ge = _xdot(g, functools.partial(_head_expander, 0))
    be = _xdot(beta, functools.partial(_head_expander, HEADS))
    return q, k, v, ge, be


def gdn_post_fn(o, z, ng):
    ms = _segsum(o * o) * (1.0 / HEAD_DIM)
    return (o * lax.rsqrt(ms + NORM_EPS) * ng * jax.nn.silu(z),)


def _neumann_inv(m):
    n = m.shape[-1]
    ri, ci = _iota2(n, n)
    eye = (ri == ci).astype(F32)
    md = jnp.where(ri // 16 == ci // 16, m, 0.0)
    mo = m - md
    t0 = eye + md
    p2 = _hdot(md, md)
    t0 = t0 + _hdot(t0, p2)
    p4 = _hdot(p2, p2)
    t0 = t0 + _hdot(t0, p4)
    p8 = _hdot(p4, p4)
    t0 = t0 + _hdot(t0, p8)
    nn = _hdot(t0, mo)
    n2 = _hdot(nn, nn)
    t1 = eye + nn + n2 + _hdot(nn, n2)
    return _hdot(t1, t0)


@jax.custom_vjp
def _inv_saved(m, t_saved):
    return t_saved


def _inv_saved_fwd(m, t_saved):
    return t_saved, t_saved


def _inv_saved_bwd(t_saved, dt):
    tt = jnp.swapaxes(t_saved, -1, -2)
    return _hdot(_hdot(tt, dt), tt), jnp.zeros_like(t_saved)


_inv_saved.defvjp(_inv_saved_fwd, _inv_saved_bwd)


def _heads(x):
    return jnp.concatenate([x[None, :, h * HEAD_DIM:(h + 1) * HEAD_DIM] for h in range(HEADS)], axis=0)


def _unheads(y):
    return jnp.concatenate([lax.index_in_dim(y, h, 0, keepdims=False) for h in range(HEADS)], axis=1)


def rwkv_heads(s0, r, lw, k2, v, kk, a, inv):
    n = r.shape[0]
    ri, ci = _iota2(n, n)
    low, strict = ri >= ci, ri > ci
    cs = _cumsum_rows(lw)
    cl = jnp.sum(lw, axis=0, keepdims=True)
    p_in, p_prev, p_inv = jnp.exp(cs), jnp.exp(cs - lw), jnp.exp(-cs)
    p_rest, p_all = jnp.exp(cl - cs), jnp.exp(cl)
    bd = kk * a
    at, rt = _heads(-kk * p_prev), _heads(r * p_in)
    bh, kh = _heads(bd * p_inv), _heads(k2 * p_inv)
    vh = _heads(v)
    m_ab = jnp.where(strict, _cdot_nt(at, bh), 0.0)
    m_ak = jnp.where(strict, _cdot_nt(at, kh), 0.0)
    m_rb = jnp.where(low, _cdot_nt(rt, bh), 0.0)
    m_rk = jnp.where(low, _cdot_nt(rt, kh), 0.0)
    sa = _cdot(inv(m_ab), _cdot_nt(at, s0) + _cdot(m_ak, vh))
    y = _cdot_nt(rt, s0) + _cdot(m_rb, sa) + _cdot(m_rk, vh)
    s1 = s0 * _heads(p_all) + _cdot_tn(sa, _heads(bd * p_rest)) + _cdot_tn(vh, _heads(k2 * p_rest))
    return _unheads(y), s1


def gdn_heads(s0, q, k, v, ge, be, inv):
    n = q.shape[0]
    ri, ci = _iota2(n, n)
    low, strict = ri >= ci, ri > ci
    gc = _cumsum_rows(ge)
    gl = jnp.sum(ge, axis=0, keepdims=True)
    gch = _heads(gc)
    decay = jnp.where(low, jnp.exp(jnp.where(low, gch - jnp.swapaxes(gch, 1, 2), 0.0)), 0.0)
    kb = k * be
    e = jnp.exp(gc)
    kh = _heads(k)
    m = -jnp.where(strict, _cdot_nt(_heads(kb), kh) * decay, 0.0)
    mr = jnp.where(low, _cdot_nt(_heads(q), kh) * decay, 0.0)
    u = _cdot(inv(m), _heads(v * be) - _cdot_nt(_heads(kb * e), s0))
    y = _cdot_nt(_heads(q * e), s0) + _cdot(mr, u)
    s1 = s0 * _heads(jnp.exp(gl)) + _cdot_tn(u, _heads(k * jnp.exp(gl - gc)))
    return _unheads(y), s1


def core_fwd(heads_fn, ins, name, hosted=()):
    T = ins[0].shape[0]
    nc = T // CHUNK
    n = len(ins)
    gather = ChipGather(list(hosted))
    ng = gather.n

    def body(*refs):
        hx = refs[n:n + ng]
        y_ref, s0_ref, t_ref = refs[n + ng:n + ng + 3]
        ho = refs[n + ng + 3:n + 2 * ng + 3]
        s_ref = refs[n + 2 * ng + 3]
        sems = refs[n + 2 * ng + 4:]
        c = pl.program_id(0)

        @pl.when(c == 0)
        def _():
            gather.start(hx, ho, sems)
            s_ref[...] = jnp.zeros_like(s_ref)

        s0 = s_ref[...]
        kept = []

        def inv(m):
            kept.append(_neumann_inv(m))
            return kept[0]

        y, s1 = heads_fn(s0, *[r[...] for r in refs[:n]], inv)
        y_ref[...] = y
        s0_ref[0] = s0
        t_ref[0] = kept[0]
        s_ref[...] = s1

        @pl.when(c == nc - 4)
        def _():
            gather.relay(hx, ho, sems)

        @pl.when(c == nc - 1)
        def _():
            gather.finish(hx, ho, sems)

    row = pl.BlockSpec((CHUNK, MIX_W), lambda c: (c, 0))
    st_shape = (HEADS, HEAD_DIM, HEAD_DIM)
    st = pl.BlockSpec((1,) + st_shape, lambda c: (c, 0, 0, 0))
    res = pl.pallas_call(
        body, name=name, grid=(nc,), in_specs=[row] * n + gather.in_specs,
        out_specs=tuple([row, st, st] + gather.out_specs),
        out_shape=tuple([jax.ShapeDtypeStruct((T, MIX_W), F32), jax.ShapeDtypeStruct((nc,) + st_shape, F32),
                         jax.ShapeDtypeStruct((nc,) + st_shape, F32)] + gather.out_shape),
        scratch_shapes=[pltpu.VMEM(st_shape, F32)] + gather.scratch,
        compiler_params=_params(("arbitrary",)))(*ins, *hosted)
    return res[0], res[1], res[2], list(res[3:])


def core_bwd(heads_fn, ins, s0_all, t_all, dy, name):
    T = ins[0].shape[0]
    nc = T // CHUNK
    n = len(ins)

    def body(*refs):
        s0_ref, t_ref, dy_ref = refs[n:n + 3]
        outs = refs[n + 3:n + 3 + n]
        ds_ref = refs[n + 3 + n]

        @pl.when(pl.program_id(0) == 0)
        def _():
            ds_ref[...] = jnp.zeros_like(ds_ref)

        t_saved = t_ref[0]
        f = lambda s0, *xs: heads_fn(s0, *xs, lambda m: _inv_saved(m, t_saved))
        _, vjp = jax.vjp(f, s0_ref[0], *[r[...] for r in refs[:n]])
        grads = vjp((dy_ref[...], ds_ref[...]))
        ds_ref[...] = grads[0]
        for k in range(n):
            outs[k][...] = grads[1 + k]

    row = pl.BlockSpec((CHUNK, MIX_W), lambda c: (nc - 1 - c, 0))
    st_shape = (HEADS, HEAD_DIM, HEAD_DIM)
    st = pl.BlockSpec((1,) + st_shape, lambda c: (nc - 1 - c, 0, 0, 0))
    return pl.pallas_call(
        body, name=name, grid=(nc,), in_specs=[row] * n + [st, st, row], out_specs=tuple([row] * n),
        out_shape=tuple(jax.ShapeDtypeStruct((T, MIX_W), F32) for _ in range(n)),
        scratch_shapes=[pltpu.VMEM(st_shape, F32)],
        compiler_params=_params(("arbitrary",)))(*ins, s0_all, t_all, dy)


def _block_diag(w):
    out = jnp.zeros((LRU_W, LRU_W), w.dtype)
    for n in range(LRU_BLOCKS):
        out = lax.dynamic_update_slice(out, w[n], (n * 64, n * 64))
    return out


def _block_diag_grad(g):
    return jnp.stack([g[n * 64:(n + 1) * 64, n * 64:(n + 1) * 64] for n in range(LRU_BLOCKS)])


def _row(v):
    return v.reshape(1, -1)


def _pad128(v):
    return jnp.pad(v.reshape(1, -1), ((0, 0), (0, 128 - v.size)))


def _layer_shared(w, l):
    cw = w['lru_conv_w'][l]
    lru_pre = [_row(cw[0]), _row(cw[1]), _row(cw[2]), _row(cw[3]), _row(w['lru_conv_b'][l]),
               _block_diag(w['lru_gate_a_w'][l]), _row(w['lru_gate_a_b'][l]),
               _block_diag(w['lru_gate_x_w'][l]), _row(w['lru_gate_x_b'][l]), _row(w['lru_lambda'][l])]
    rw_pre = [_row(w['rwkv_mu'][l]), w['rwkv_w_up'][l], _row(w['rwkv_w_bias'][l]), w['rwkv_a_up'][l],
              _row(w['rwkv_a_bias'][l]), w['rwkv_g_up'][l], _row(w['rwkv_k_k'][l]), _row(w['rwkv_k_a'][l])]
    if l > 0:
        rw_pre += [w['rwkv_vres_w1'][l - 1], w['rwkv_vres_w2'][l - 1], _row(w['rwkv_vres_b'][l - 1])]
    rw_post = [_row(w['rwkv_ln_g'][l]), _row(w['rwkv_ln_b'][l]), _row(w['rwkv_r_k'][l])]
    gw = w['gdn_conv_w'][l]
    gdn_pre = [_row(gw[0]), _row(gw[1]), _row(gw[2]), _row(gw[3]), _pad128(w['gdn_a_log'][l]),
               _pad128(w['gdn_dt_bias'][l])]
    gdn_post = [_row(jnp.tile(w['gdn_norm'][l], HEADS))]
    return dict(lru_pre=lru_pre, lru_post=[_row(w['lru_out_norm'][l])], rw_pre=rw_pre, rw_post=rw_post,
                gdn_pre=gdn_pre, gdn_post=gdn_post)


def _mixer_fwd(p, sh, l, v_first, host_rwkv=(), host_gdn=()):
    T = p.shape[0]
    lx, ly = p[:, 0:256], p[:, 256:512]
    prw, qkv, z, ab = p[:, 512:1920], p[:, 1920:3072], p[:, 3072:3456], p[:, 3456:3584]
    first = (lax.broadcasted_iota(jnp.int32, (T, LRU_W), 0) == 0).astype(F32)
    lru_rows = [lx, shift_rows(lx, 1), shift_rows(lx, 2), shift_rows(lx, 3), first]
    a, b = rowwise_fwd(lru_pre_fn, lru_rows, sh['lru_pre'], (LRU_W, LRU_W), f"lru_pre_fwd{l}")
    hseq = lru_scan(a, b, False, f"lru_scan_fwd{l}")
    (y_lru,) = rowwise_fwd(lru_post_fn, [hseq, ly], sh['lru_post'], (LRU_W,), f"lru_post_fwd{l}")

    rw_rows = [prw, shift_rows(prw, 1)] + ([v_first] if l > 0 else [])
    rw = rowwise_fwd(make_rwkv_pre_fn(l > 0), rw_rows, sh['rw_pre'], (MIX_W,) * 7, f"rwkv_pre_fwd{l}")
    r, lw, k2, v, kk, ar, g = rw
    y_raw, rs0, rt, got_rwkv = core_fwd(rwkv_heads, [r, lw, k2, v, kk, ar], f"rwkv_core_fwd{l}", host_rwkv)
    (y_rw,) = rowwise_fwd(rwkv_post_fn, [y_raw, r, k2, v, g], sh['rw_post'], (MIX_W,), f"rwkv_post_fwd{l}")

    gdn_rows = [qkv, shift_rows(qkv, 1), shift_rows(qkv, 2), shift_rows(qkv, 3), ab]
    gd = rowwise_fwd(gdn_pre_fn, gdn_rows, sh['gdn_pre'], (MIX_W,) * 5, f"gdn_pre_fwd{l}")
    o_raw, gs0, gt, got_gdn = core_fwd(gdn_heads, list(gd), f"gdn_core_fwd{l}", host_gdn)
    (y_gdn,) = rowwise_fwd(gdn_post_fn, [o_raw, z], sh['gdn_post'], (MIX_W,), f"gdn_post_fwd{l}")

    mixed = jnp.concatenate([y_lru, y_rw, y_gdn], axis=1)
    saved = dict(lru_rows=lru_rows, a=a, hseq=hseq, ly=ly, rw_rows=rw_rows, rw=rw, y_raw=y_raw, rs0=rs0, rt=rt,
                 gdn_rows=gdn_rows, gd=gd, o_raw=o_raw, gs0=gs0, gt=gt, z=z)
    v_layer0 = v if l == 0 else None
    return mixed, saved, v_layer0, got_rwkv, got_gdn


def _mixer_bwd(dmixed, sv, sh, l, dv_first):
    d_lru, d_rw, d_gdn = dmixed[:, 0:256], dmixed[:, 256:640], dmixed[:, 640:1024]
    gw = {}

    dh, dly, d_og = rowwise_bwd(lru_post_fn, [sv['hseq'], sv['ly']], sh['lru_post'], [d_lru], f"lru_post_bwd{l}")
    gscan = lru_scan(unshift_rows(sv['a'], 1), dh, True, f"lru_scan_bwd{l}")
    res = rowwise_bwd(lru_pre_fn, sv['lru_rows'], sh['lru_pre'], [gscan, shift_rows(sv['hseq'], 1)],
                      f"lru_pre_bwd{l}", ct_fn=lambda gs, hp: (gs * hp, gs))
    dlx = res[0] + unshift_rows(res[1], 1) + unshift_rows(res[2], 2) + unshift_rows(res[3], 3)
    dw0, dw1, dw2, dw3, dcb, dga, dgab, dgx, dgxb, dlam = res[5:]
    gw['lru_conv_w'] = jnp.concatenate([dw0, dw1, dw2, dw3], axis=0)
    gw['lru_conv_b'] = dcb[0]
    gw['lru_gate_a_w'] = _block_diag_grad(dga)
    gw['lru_gate_a_b'] = dgab.reshape(LRU_BLOCKS, 64)
    gw['lru_gate_x_w'] = _block_diag_grad(dgx)
    gw['lru_gate_x_b'] = dgxb.reshape(LRU_BLOCKS, 64)
    gw['lru_lambda'] = dlam[0]
    gw['lru_out_norm'] = d_og[0]

    r, lw, k2, v, kk, ar, g = sv['rw']
    res = rowwise_bwd(rwkv_post_fn, [sv['y_raw'], r, k2, v, g], sh['rw_post'], [d_rw], f"rwkv_post_bwd{l}")
    dy_raw, dr_p, dk2_p, dv_p, dg = res[:5]
    gw['rwkv_ln_g'], gw['rwkv_ln_b'], gw['rwkv_r_k'] = res[5][0], res[6][0], res[7].reshape(HEADS, HEAD_DIM)
    dr_c, dlw, dk2_c, dv_c, dkk, dar = core_bwd(rwkv_heads, [r, lw, k2, v, kk, ar], sv['rs0'], sv['rt'], dy_raw,
                                                 f"rwkv_core_bwd{l}")
    cts = [dr_p, dr_c, dlw, dk2_p, dk2_c, dv_p, dv_c, dkk, dar, dg]
    if l == 0:
        cts.append(dv_first)
        ct_fn = lambda a1, a2, b, c1, c2, d1, d2, e, f, gg, vf: (a1 + a2, b, c1 + c2, d1 + d2 + vf, e, f, gg)
    else:
        ct_fn = lambda a1, a2, b, c1, c2, d1, d2, e, f, gg: (a1 + a2, b, c1 + c2, d1 + d2, e, f, gg)
    res = rowwise_bwd(make_rwkv_pre_fn(l > 0), sv['rw_rows'], sh['rw_pre'], cts, f"rwkv_pre_bwd{l}", ct_fn=ct_fn)
    dprw = res[0] + unshift_rows(res[1], 1)
    nrow = len(sv['rw_rows'])
    dv_first_out = res[2] if l > 0 else None
    sg = res[nrow:]
    gw['rwkv_mu'], gw['rwkv_w_up'], gw['rwkv_w_bias'], gw['rwkv_a_up'] = sg[0][0], sg[1], sg[2][0], sg[3]
    gw['rwkv_a_bias'], gw['rwkv_g_up'], gw['rwkv_k_k'], gw['rwkv_k_a'] = sg[4][0], sg[5], sg[6][0], sg[7][0]
    if l > 0:
        gw['rwkv_vres_w1'], gw['rwkv_vres_w2'], gw['rwkv_vres_b'] = sg[8], sg[9], sg[10][0]

    do_raw, dz, d_ng = rowwise_bwd(gdn_post_fn, [sv['o_raw'], sv['z']], sh['gdn_post'], [d_gdn], f"gdn_post_bwd{l}")
    gw['gdn_norm'] = jnp.sum(d_ng.reshape(HEADS, HEAD_DIM), axis=0)
    dgd = core_bwd(gdn_heads, list(sv['gd']), sv['gs0'], sv['gt'], do_raw, f"gdn_core_bwd{l}")
    res = rowwise_bwd(gdn_pre_fn, sv['gdn_rows'], sh['gdn_pre'], list(dgd), f"gdn_pre_bwd{l}")
    dqkv = res[0] + unshift_rows(res[1], 1) + unshift_rows(res[2], 2) + unshift_rows(res[3], 3)
    dab = res[4]
    gw['gdn_conv_w'] = jnp.concatenate(res[5:9], axis=0)
    gw['gdn_a_log'], gw['gdn_dt_bias'] = res[9][0, :HEADS], res[10][0, :HEADS]

    dp = jnp.concatenate([dlx, dly, dprw, dqkv, dz, dab], axis=1)
    return dp, gw, dv_first_out


IN_SHARD = D_IN // N_CHIPS
IN_SHARD_PAD = D_IN_PAD // N_CHIPS


def _cols_to_chips(g, n=N_CHIPS):
    r = g.shape[0]
    return jnp.transpose(g.reshape(r, n, -1), (1, 0, 2))


def _cols_from_chips(g):
    return jnp.transpose(g, (1, 0, 2)).reshape(g.shape[1], -1)


def _w_in_from_chips(g):
    nat = _cols_from_chips(g[:, :, :IN_SHARD])
    return jnp.pad(nat, ((0, 0), (0, D_IN_PAD - D_IN)))


def _w_in_to_chips(g):
    return jnp.pad(_cols_to_chips(g[:, :D_IN]), ((0, 0), (0, 0), (0, IN_SHARD_PAD - IN_SHARD)))


def _natural(name, g):
    if name == 'w_in':
        return _w_in_from_chips(g)
    if BIG[name] == 2:
        return _cols_from_chips(g)
    return g.reshape(-1, g.shape[2])


def local_step(x, target, w, wb, shards=None):
    def hosted(keys):
        return [shards[k] for k in keys] if shards is not None else []

    def arrived(keys, gathered):
        for (name, layer), g in zip(keys if shards is not None else [], gathered):
            wb[name][layer] = _natural(name, g)

    saved = []
    v_first = None
    for l in range(N_LAYERS):
        sh = _layer_shared(w, l)
        more = l + 1 < N_LAYERS
        in_ffn1 = [('w_in', l)] + ([('w_out', l)] if more else [])
        in_rwkv = [('ffn2_wi', l)] + ([('ffn2_wo', l)] if more else [('w_out', l)])
        in_gdn = [('ffn1_wi', l + 1)] if more else [('ffn2_wo', l)]
        in_ffn2 = [('ffn1_wo', l + 1)] if more else []
        x1, got = ffn_fwd(x, _row(w['ffn1_norm'][l]), wb['ffn1_wi'][l], wb['ffn1_wo'][l], f"ffn1_fwd{l}",
                          hosted(in_ffn1))
        arrived(in_ffn1, got)
        p = proj_fwd(x1, _row(w['mix_norm'][l]), wb['w_in'][l], f"proj_fwd{l}")
        mixed, sv, v0, got_rwkv, got_gdn = _mixer_fwd(p, sh, l, v_first, hosted(in_rwkv), hosted(in_gdn))
        arrived(in_rwkv, got_rwkv)
        arrived(in_gdn, got_gdn)
        if l == 0:
            v_first = v0
        x2 = out_fwd(mixed, wb['w_out'][l], x1, f"out_fwd{l}")
        x3, got = ffn_fwd(x2, _row(w['ffn2_norm'][l]), wb['ffn2_wi'][l], wb['ffn2_wo'][l], f"ffn2_fwd{l}",
                          hosted(in_ffn2))
        arrived(in_ffn2, got)
        saved.append(dict(x0=x, x1=x1, x2=x2, mixed=mixed, sv=sv, sh=sh))
        x = x3

    loss, dx, dgf = loss_head(x, _row(w['final_norm']), target, "loss_head")
    per_layer = [dict() for _ in range(N_LAYERS)]
    dv_first = jnp.zeros((x.shape[0], MIX_W), F32)

    waiting, chip_sums, arrived_parts = [], {}, {}

    def reduce_now(keys, tag):
        if shards is None:
            return
        sums = chip_sums_of([(n, k, per_layer[k][n]) for n, k in keys], lax.axis_index("c"), tag)
        for key, (total, total_bf) in zip(keys, sums):
            chip_sums[key] = total
            waiting.append((key, total_bf))

    def take_waiting():
        keys, parts = [k for k, _ in waiting], [p for _, p in waiting]
        waiting.clear()
        return keys, parts

    for l in reversed(range(N_LAYERS)):
        s = saved[l]
        gw = per_layer[l]
        keys, parts = take_waiting()
        dx, dg2, dwg, dwu, dwo, got = ffn_bwd(s['x2'], dx, _row(w['ffn2_norm'][l]), wb['ffn2_wi'][l],
                                              wb['ffn2_wo'][l], f"ffn2_bwd{l}", parts)
        arrived_parts.update(zip(keys, got))
        wi_parts = lambda dwg, dwu: (dwg, dwu)
        row_parts = lambda dw: dw.reshape(N_CHIPS, -1, dw.shape[1])
        gw['ffn2_norm'], gw['ffn2_wi'], gw['ffn2_wo'] = dg2[0], wi_parts(dwg, dwu), row_parts(dwo)
        if l == N_LAYERS - 1:
            reduce_now([('ffn2_wi', l), ('ffn2_wo', l)], f"ffn2_{l}")
        dmixed, dw_out = out_bwd(s['mixed'], wb['w_out'][l], dx, f"out_bwd{l}")
        gw['w_out'] = row_parts(dw_out)
        dp, gmix, dvf = _mixer_bwd(dmixed, s['sv'], s['sh'], l, dv_first)
        if l > 0:
            dv_first = dvf
        gw.update(gmix)
        dx, dgm, dwin = proj_bwd(s['x1'], dx, _row(w['mix_norm'][l]), wb['w_in'][l], dp, f"proj_bwd{l}")
        gw['mix_norm'], gw['w_in'] = dgm[0], _w_in_to_chips(dwin)
        if l < N_LAYERS - 1:
            reduce_now([('ffn2_wi', l), ('ffn2_wo', l), ('w_in', l), ('w_out', l)], f"mix_{l}")
        keys, parts = take_waiting()
        dx, dg1, dwg, dwu, dwo, got = ffn_bwd(s['x0'], dx, _row(w['ffn1_norm'][l]), wb['ffn1_wi'][l],
                                              wb['ffn1_wo'][l], f"ffn1_bwd{l}", parts)
        arrived_parts.update(zip(keys, got))
        gw['ffn1_norm'], gw['ffn1_wi'], gw['ffn1_wo'] = dg1[0], wi_parts(dwg, dwu), row_parts(dwo)
        if l == N_LAYERS - 1:
            reduce_now([('w_in', l), ('w_out', l), ('ffn1_wi', l), ('ffn1_wo', l)], f"ffn1_{l}")
        else:
            reduce_now([('ffn1_wi', l), ('ffn1_wo', l)], f"ffn1_{l}")
    grads = {'final_norm': dgf[0]}
    if shards is not None:
        grads['last_round'] = take_waiting()
        arrived_parts.update({key: None for key in grads['last_round'][0]})
    for name in WEIGHTS:
        if name == 'final_norm':
            continue
        if name in BIG:
            if shards is None:
                grads[name] = [per_layer[l][name] for l in range(N_LAYERS)]
            else:
                grads[name] = [(chip_sums[(name, l)], arrived_parts[(name, l)]) for l in range(N_LAYERS)]
        elif name.startswith('rwkv_vres'):
            grads[name] = per_layer[1][name][None]
        else:
            grads[name] = jnp.stack([per_layer[l][name] for l in range(N_LAYERS)])
    return loss[0, 0], dx, grads


ANY = pl.BlockSpec(memory_space=pl.ANY)


def _coords():
    return lax.axis_index("x"), lax.axis_index("y"), lax.axis_index("c")


def _other_chips(x, y):
    return [((x + 1) % 2, y), (x, (y + 1) % 2), ((x + 1) % 2, (y + 1) % 2)]


def allreduce_small(pack, name):
    R = pack.shape[0]
    rh = R // 2

    def body(x_ref, o_ref, sib_ref, chip_ref, parts_ref, send_sems, recv_sems):
        x, y, c = _coords()
        sib = (x, y, 1 - c)

        def copy(k, src, dst, to):
            return pltpu.make_async_remote_copy(src_ref=src, dst_ref=dst, send_sem=send_sems.at[k],
                                                recv_sem=recv_sems.at[k], device_id=to, device_id_type=MESH)

        swap = copy(0, x_ref, sib_ref, sib)
        swap.start()
        swap.wait()
        chip_ref[...] = jnp.where(c == 0, x_ref[...], sib_ref[...]) + jnp.where(c == 0, sib_ref[...], x_ref[...])

        mine = pl.ds(pl.multiple_of(c * rh, 8), rh)
        sends = [copy(1 + j, chip_ref.at[mine], parts_ref.at[j], (px, py, c))
                 for j, (px, py) in enumerate(_other_chips(x, y))]
        for cp in sends:
            cp.start()
        for cp in sends:
            cp.wait()
        s = 2 * x + y
        own = chip_ref[mine, :]
        from_chip = {2: parts_ref[0], 1: parts_ref[1], 3: parts_ref[2]}
        terms = []
        for k in range(N_CHIPS):
            t = own
            for d, part in from_chip.items():
                t = jnp.where(jnp.bitwise_xor(s, d) == k, part, t)
            terms.append(t)
        o_ref[mine, :] = ((terms[0] + terms[1]) + terms[2]) + terms[3]

        share = copy(4, o_ref.at[mine], o_ref.at[mine], sib)
        share.start()
        share.wait()

    vm = pl.BlockSpec(memory_space=pltpu.VMEM)
    return pl.pallas_call(
        body, name=name, in_specs=[vm], out_specs=vm, out_shape=jax.ShapeDtypeStruct((R, 128), F32),
        scratch_shapes=[pltpu.VMEM((R, 128), F32), pltpu.VMEM((R, 128), F32), pltpu.VMEM((3, rh, 128), F32),
                        pltpu.SemaphoreType.DMA((5,)), pltpu.SemaphoreType.DMA((5,))],
        compiler_params=_params())(pack)


class ChipGather:
    def __init__(self, shards):
        self.shapes = [s.shape for s in shards]
        self.n = len(shards)
        self.in_specs = [ANY] * self.n
        self.out_specs = [ANY] * self.n
        self.out_shape = [jax.ShapeDtypeStruct((N_CHIPS,) + s.shape, s.dtype) for s in shards]
        self.scratch = [pltpu.SemaphoreType.DMA((6 * self.n,)), pltpu.SemaphoreType.DMA((6 * self.n,)),
                        pltpu.SemaphoreType.DMA((self.n,))] if self.n else []

    def _rows(self, a, core):
        rh = self.shapes[a][0] // 2
        return pl.ds(pl.multiple_of(core * rh, 16), rh)

    def _copies(self, kind, x_refs, o_refs, sems):
        send_sems, recv_sems, local_sems = sems
        x, y, c = _coords()
        s_me = 2 * x + y
        sib = (x, y, 1 - c)

        def copy(a, k, src, dst, to):
            return pltpu.make_async_remote_copy(src_ref=src, dst_ref=dst, send_sem=send_sems.at[6 * a + k],
                                                recv_sem=recv_sems.at[6 * a + k], device_id=to, device_id_type=MESH)

        if kind == 'own':
            return [pltpu.make_async_copy(x_refs[a], o_refs[a].at[s_me], local_sems.at[a]) for a in range(self.n)]
        out = []
        for j, (px, py) in enumerate(_other_chips(x, y)):
            for a in range(self.n):
                mine = self._rows(a, c)
                part = o_refs[a].at[2 * px + py, mine]
                if kind == 'sent':
                    out.append(copy(a, j, x_refs[a].at[mine], o_refs[a].at[s_me, mine], (px, py, c)))
                elif kind == 'arrived':
                    out.append(copy(a, j, part, part, (px, py, c)))
                elif kind == 'passed':
                    out.append(copy(a, 3 + j, part, part, sib))
                else:
                    theirs = o_refs[a].at[2 * px + py, self._rows(a, 1 - c)]
                    out.append(copy(a, 3 + j, theirs, theirs, sib))
        return out

    def start(self, x_refs, o_refs, sems):
        if not self.n:
            return
        for cp in self._copies('own', x_refs, o_refs, sems) + self._copies('sent', x_refs, o_refs, sems):
            cp.start()

    def relay(self, x_refs, o_refs, sems):
        if not self.n:
            return
        for got, fw in zip(self._copies('arrived', x_refs, o_refs, sems),
                           self._copies('passed', x_refs, o_refs, sems)):
            got.wait_recv()
            fw.start()

    def finish(self, x_refs, o_refs, sems):
        if not self.n:
            return
        for cp in self._copies('from_sibling', x_refs, o_refs, sems):
            cp.wait_recv()
        for cp in self._copies('sent', x_refs, o_refs, sems) + self._copies('passed', x_refs, o_refs, sems):
            cp.wait_send()
        for cp in self._copies('own', x_refs, o_refs, sems):
            cp.wait()


def allgather_chips(shards, name):
    gather = ChipGather(shards)
    n = gather.n

    def body(*refs):
        x_refs, o_refs, sems = refs[:n], refs[n:2 * n], refs[2 * n:]
        gather.start(x_refs, o_refs, sems)
        gather.relay(x_refs, o_refs, sems)
        gather.finish(x_refs, o_refs, sems)

    return pl.pallas_call(
        body, name=name, in_specs=gather.in_specs, out_specs=tuple(gather.out_specs),
        out_shape=tuple(gather.out_shape), scratch_shapes=gather.scratch, compiler_params=_params())(*shards)


def sibling_swap(srcs, halves, name):
    n = len(srcs)
    row_axis = [s.ndim - 2 for s in srcs]
    out_shapes = [s.shape[:ax] + (s.shape[ax] // 2,) + s.shape[ax + 1:] if halves else s.shape
                  for s, ax in zip(srcs, row_axis)]

    def body(*refs):
        x_refs, o_refs = refs[:n], refs[n:2 * n]
        send_sems, recv_sems = refs[2 * n:]
        x, y, c = _coords()
        copies = []
        for a in range(n):
            part = x_refs[a]
            if halves:
                rh = srcs[a].shape[row_axis[a]] // 2
                theirs = pl.ds(pl.multiple_of((1 - c) * rh, 16), rh)
                part = part.at[:, theirs] if row_axis[a] == 1 else part.at[theirs]
            cp = pltpu.make_async_remote_copy(src_ref=part, dst_ref=o_refs[a], send_sem=send_sems.at[a],
                                              recv_sem=recv_sems.at[a], device_id=(x, y, 1 - c), device_id_type=MESH)
            cp.start()
            copies.append(cp)
        for cp in copies:
            cp.wait()

    return pl.pallas_call(
        body, name=name, in_specs=[ANY] * n, out_specs=tuple([ANY] * n),
        out_shape=tuple(jax.ShapeDtypeStruct(sh, s.dtype) for sh, s in zip(out_shapes, srcs)),
        scratch_shapes=[pltpu.SemaphoreType.DMA((n,)), pltpu.SemaphoreType.DMA((n,))],
        compiler_params=_params())(*srcs)


class ChipScatter:
    def __init__(self, parts):
        self.n = len(parts)
        self.in_specs = [ANY] * self.n
        self.out_specs = [ANY] * self.n
        self.out_shape = [jax.ShapeDtypeStruct((3,) + p.shape[1:], p.dtype) for p in parts]
        self.scratch = [pltpu.SemaphoreType.DMA((3 * self.n,)), pltpu.SemaphoreType.DMA((3 * self.n,))] if self.n else []

    def _copies(self, x_refs, o_refs, sems):
        send_sems, recv_sems = sems
        x, y, c = _coords()
        return [pltpu.make_async_remote_copy(src_ref=x_refs[a].at[2 * px + py], dst_ref=o_refs[a].at[j],
                                             send_sem=send_sems.at[3 * a + j], recv_sem=recv_sems.at[3 * a + j],
                                             device_id=(px, py, c), device_id_type=MESH)
                for j, (px, py) in enumerate(_other_chips(x, y)) for a in range(self.n)]

    def start(self, x_refs, o_refs, sems):
        if self.n:
            for cp in self._copies(x_refs, o_refs, sems):
                cp.start()

    def finish(self, x_refs, o_refs, sems):
        if self.n:
            for cp in self._copies(x_refs, o_refs, sems):
                cp.wait()


def scatter_chips(parts, name):
    scatter = ChipScatter(parts)
    n = scatter.n

    def body(*refs):
        x_refs, o_refs, sems = refs[:n], refs[n:2 * n], refs[2 * n:]
        scatter.start(x_refs, o_refs, sems)
        scatter.finish(x_refs, o_refs, sems)

    return pl.pallas_call(
        body, name=name, in_specs=scatter.in_specs, out_specs=tuple(scatter.out_specs),
        out_shape=tuple(scatter.out_shape), scratch_shapes=scatter.scratch, compiler_params=_params())(*parts)


HBM = pl.BlockSpec(memory_space=pltpu.HBM)
SEM = pl.BlockSpec(memory_space=pltpu.SEMAPHORE)
SIDE_EFFECT = pltpu.SideEffectType.DATAFLOW_SIDE_EFFECTING


def _scatter_copies(x_refs, land_refs, send_sems, recv_sems):
    x, y, c = _coords()
    n = len(x_refs)
    return [pltpu.make_async_remote_copy(src_ref=x_refs[a].at[2 * px + py], dst_ref=land_refs[a].at[j],
                                         send_sem=send_sems[3 * a + j], recv_sem=recv_sems[3 * a + j],
                                         device_id=(px, py, c), device_id_type=MESH)
            for j, (px, py) in enumerate(_other_chips(x, y)) for a in range(n)]


def scatter_start(parts, name):
    n = len(parts)
    k = 3 * n
    lands = [lax.empty((3,) + p.shape[1:], p.dtype) for p in parts]

    def body(*refs):
        x_refs, land_refs = refs[:n], refs[n:2 * n]
        send_sems, recv_sems = refs[2 * n:2 * n + k], refs[2 * n + k:2 * n + 2 * k]
        token = refs[-1]
        for cp in _scatter_copies(x_refs, land_refs, send_sems, recv_sems):
            cp.start()
        token[...] = jnp.zeros_like(token)

    hbm = lambda a: pltpu.HBM(a.shape, a.dtype)
    res = pl.pallas_call(
        body, name=name, in_specs=[HBM] * (2 * n),
        out_specs=tuple([SEM] * (2 * k) + [HBM] * (2 * n) + [pl.BlockSpec(memory_space=pltpu.VMEM)]),
        out_shape=tuple([pltpu.SemaphoreType.DMA(())] * (2 * k) + [hbm(p) for p in parts] + [hbm(b) for b in lands]
                        + [jax.ShapeDtypeStruct((8, 128), F32)]),
        input_output_aliases={i: 2 * k + i for i in range(2 * n)},
        compiler_params=pltpu.CompilerParams(has_side_effects=SIDE_EFFECT, vmem_limit_bytes=VMEM_LIMIT))(
            *[pltpu.with_memory_space_constraint(a, pltpu.HBM) for a in list(parts) + lands])
    return list(res[:2 * k]), list(res[2 * k:2 * k + n]), list(res[2 * k + n:2 * k + 2 * n]), res[-1]


def scatter_wait(sems, parts_thru, lands_thru, after, name):
    n = len(parts_thru)
    k = 3 * n

    def body(*refs):
        x_refs, land_refs = refs[:n], refs[n:2 * n]
        send_sems, recv_sems = refs[2 * n:2 * n + k], refs[2 * n + k:2 * n + 2 * k]
        for cp in _scatter_copies(x_refs, land_refs, send_sems, recv_sems):
            cp.wait_send()
            cp.wait_recv()

    hbm = lambda a: pltpu.HBM(a.shape, a.dtype)
    res = pl.pallas_call(
        body, name=name, in_specs=[HBM] * (2 * n) + [SEM] * (2 * k) + [ANY],
        out_specs=tuple([HBM] * (2 * n)), out_shape=tuple(hbm(a) for a in list(parts_thru) + list(lands_thru)),
        input_output_aliases={i: i for i in range(2 * n)},
        compiler_params=pltpu.CompilerParams(has_side_effects=SIDE_EFFECT, vmem_limit_bytes=VMEM_LIMIT))(
            *parts_thru, *lands_thru, *sems, after)
    return list(res[n:])


def _row_block(rows):
    return max(b for b in range(16, 257, 16) if rows % b == 0)


def chip_sum(gpack, recv, core, name):
    n, R, W = gpack.shape
    rh = R // 2
    rb = _row_block(rh)
    nb = rh // rb

    def body(c_ref, g_ref, r_ref, o_ref, ob_ref):
        s = g_ref[...] + r_ref[...]
        o_ref[...] = s
        ob_ref[...] = s.astype(BF16)

    blk = pl.BlockSpec((1, rb, W), lambda i, j, c_ref: (i, j, 0))
    spec = pltpu.PrefetchScalarGridSpec(
        num_scalar_prefetch=1, grid=(n, nb),
        in_specs=[pl.BlockSpec((1, rb, W), lambda i, j, c_ref: (i, c_ref[0] * nb + j, 0)), blk],
        out_specs=(blk, blk))
    return pl.pallas_call(
        body, name=name, grid_spec=spec,
        out_shape=(jax.ShapeDtypeStruct((n, rh, W), F32), jax.ShapeDtypeStruct((n, rh, W), BF16)),
        compiler_params=_params(("arbitrary", "arbitrary")))(core, gpack, recv)


def chip_sum_cols(gate, up, recv_gate, recv_up, core, name):
    R, W = gate.shape
    cw = W // 2
    rh = R // 2
    rb = _row_block(rh)
    nb = rh // rb

    def body(c_ref, g_ref, u_ref, rg_ref, ru_ref, o_ref, ob_ref):
        s = jnp.where(pl.program_id(0) < 2, g_ref[...] + rg_ref[...], u_ref[...] + ru_ref[...])
        o_ref[0] = s
        ob_ref[0] = s.astype(BF16)

    gate_blk = lambda s, j: (jnp.where(s < 2, j, nb - 1), jnp.minimum(s, 1))
    up_blk = lambda s, j: (jnp.where(s < 2, 0, j), jnp.maximum(s - 2, 0))
    out = pl.BlockSpec((1, rb, cw), lambda s, j, c_ref: (s, j, 0))

    def own(blk):
        return lambda s, j, c_ref: (c_ref[0] * nb + blk(s, j)[0], blk(s, j)[1])

    def theirs(blk):
        return lambda s, j, c_ref: blk(s, j)

    spec = pltpu.PrefetchScalarGridSpec(
        num_scalar_prefetch=1, grid=(N_CHIPS, nb),
        in_specs=[pl.BlockSpec((rb, cw), own(gate_blk)), pl.BlockSpec((rb, cw), own(up_blk)),
                  pl.BlockSpec((rb, cw), theirs(gate_blk)), pl.BlockSpec((rb, cw), theirs(up_blk))],
        out_specs=(out, out))
    return pl.pallas_call(
        body, name=name, grid_spec=spec,
        out_shape=(jax.ShapeDtypeStruct((N_CHIPS, rh, cw), F32), jax.ShapeDtypeStruct((N_CHIPS, rh, cw), BF16)),
        compiler_params=_params(("arbitrary", "arbitrary")))(core, gate, up, recv_gate, recv_up)


def chip_sums_of(items, core, tag):
    parts = []
    for _, _, g in items:
        parts += list(g) if isinstance(g, tuple) else [g]
    swapped = iter(zip(parts, sibling_swap(parts, True, f"grad_swap_cores_{tag}")))
    core_arg = core.reshape(1).astype(jnp.int32)
    sums = []
    for n, l, g in items:
        if isinstance(g, tuple):
            (dwg, from_g), (dwu, from_u) = next(swapped), next(swapped)
            sums.append(chip_sum_cols(dwg, dwu, from_g, from_u, core_arg, f"grad_chip_sum_{n}{l}"))
        else:
            p, r = next(swapped)
            sums.append(chip_sum(p, r, core_arg, f"grad_chip_sum_{n}{l}"))
    return sums


def shard_sum(own, recv, name):
    R, W = own.shape
    rb = _row_block(R)

    def body(a_ref, r_ref, o_ref):
        acc = a_ref[...]
        for j in range(3):
            acc = acc + r_ref[j].astype(F32)
        o_ref[...] = acc

    return pl.pallas_call(
        body, name=name, grid=(R // rb,),
        in_specs=[pl.BlockSpec((rb, W), lambda i: (i, 0)), pl.BlockSpec((3, rb, W), lambda i: (0, i, 0))],
        out_specs=pl.BlockSpec((rb, W), lambda i: (i, 0)), out_shape=jax.ShapeDtypeStruct((R, W), F32),
        compiler_params=_params(("arbitrary",)))(own, recv)


def adamw(w, m, v, g, name):
    L, R, C = w.shape
    rb = max(b for b in range(8, 257, 8) if R % b == 0)
    bc1 = 1.0 - ADAM_B1 ** ADAM_STEP
    bc2 = 1.0 - ADAM_B2 ** ADAM_STEP

    def body(w_ref, m_ref, v_ref, g_ref, d_ref, nm_ref, nv_ref):
        gv = g_ref[...]
        nm = ADAM_B1 * m_ref[...] + (1.0 - ADAM_B1) * gv
        nv = ADAM_B2 * v_ref[...] + (1.0 - ADAM_B2) * (gv * gv)
        d_ref[...] = -ADAM_LR * ((nm / bc1) / (jnp.sqrt(nv / bc2) + ADAM_EPS) + ADAM_WD * w_ref[...])
        nm_ref[...] = nm
        nv_ref[...] = nv

    blk = pl.BlockSpec((1, rb, C), lambda l, i: (l, i, 0))
    sh = jax.ShapeDtypeStruct((L, R, C), F32)
    return pl.pallas_call(body, name=name, grid=(L, R // rb), in_specs=[blk] * 4, out_specs=(blk,) * 3,
                          out_shape=(sh, sh, sh), compiler_params=_params(("arbitrary", "arbitrary")))(w, m, v, g)


SMALL = [n for n in WEIGHTS if n not in BIG]


PACK_TILE = 8 * 128


def _pack(arrays):
    blocks = []
    for a in arrays:
        flat = a.reshape(-1)
        flat = jnp.pad(flat, (0, -flat.size % PACK_TILE))
        blocks.append(flat.reshape(-1, 128))
    rows = sum(b.shape[0] for b in blocks)
    if rows % 16:
        blocks.append(jnp.zeros((8, 128), arrays[0].dtype))
    return jnp.concatenate(blocks, axis=0)


def _unpack(pack, shapes):
    out, row = [], 0
    for shape in shapes:
        size = int(np.prod(shape))
        rows = -(-size // PACK_TILE) * 8
        out.append(pack[row:row + rows].reshape(-1)[:size].reshape(shape))
        row += rows
    return out


def _pad_lanes(a):
    return jnp.pad(a, ((0, 0), (0, -a.shape[1] % 128)))


def _local_shard(full, axis, chip):
    size = full.shape[axis] // N_CHIPS
    return lax.dynamic_slice_in_dim(full, chip * size, size, axis)


def kernel(x, ffn1_norm, ffn1_wi, ffn1_wo, mix_norm, w_in, w_out, lru_conv_w, lru_conv_b, lru_gate_a_w, lru_gate_a_b, lru_gate_x_w, lru_gate_x_b, lru_lambda, lru_out_norm, rwkv_mu, rwkv_w_up, rwkv_w_bias, rwkv_a_up, rwkv_a_bias, rwkv_g_up, rwkv_k_k, rwkv_k_a, rwkv_r_k, rwkv_ln_g, rwkv_ln_b, rwkv_vres_w1, rwkv_vres_w2, rwkv_vres_b, gdn_conv_w, gdn_a_log, gdn_dt_bias, gdn_norm, ffn2_norm, ffn2_wi, ffn2_wo, final_norm, loss_target, m_ffn1_norm, m_ffn1_wi, m_ffn1_wo, m_mix_norm, m_w_in, m_w_out, m_lru_conv_w, m_lru_conv_b, m_lru_gate_a_w, m_lru_gate_a_b, m_lru_gate_x_w, m_lru_gate_x_b, m_lru_lambda, m_lru_out_norm, m_rwkv_mu, m_rwkv_w_up, m_rwkv_w_bias, m_rwkv_a_up, m_rwkv_a_bias, m_rwkv_g_up, m_rwkv_k_k, m_rwkv_k_a, m_rwkv_r_k, m_rwkv_ln_g, m_rwkv_ln_b, m_rwkv_vres_w1, m_rwkv_vres_w2, m_rwkv_vres_b, m_gdn_conv_w, m_gdn_a_log, m_gdn_dt_bias, m_gdn_norm, m_ffn2_norm, m_ffn2_wi, m_ffn2_wo, m_final_norm, v_ffn1_norm, v_ffn1_wi, v_ffn1_wo, v_mix_norm, v_w_in, v_w_out, v_lru_conv_w, v_lru_conv_b, v_lru_gate_a_w, v_lru_gate_a_b, v_lru_gate_x_w, v_lru_gate_x_b, v_lru_lambda, v_lru_out_norm, v_rwkv_mu, v_rwkv_w_up, v_rwkv_w_bias, v_rwkv_a_up, v_rwkv_a_bias, v_rwkv_g_up, v_rwkv_k_k, v_rwkv_k_a, v_rwkv_r_k, v_rwkv_ln_g, v_rwkv_ln_b, v_rwkv_vres_w1, v_rwkv_vres_w2, v_rwkv_vres_b, v_gdn_conv_w, v_gdn_a_log, v_gdn_dt_bias, v_gdn_norm, v_ffn2_norm, v_ffn2_wi, v_ffn2_wo, v_final_norm):
    args = locals()
    w_loc = {n: args[n] for n in WEIGHTS}
    m_loc = {n: args['m_' + n] for n in WEIGHTS}
    v_loc = {n: args['v_' + n] for n in WEIGHTS}
    chip = 2 * lax.axis_index("x") + lax.axis_index("y")
    core = lax.axis_index("c")

    big = [(n, l) for n in BIG for l in range(N_LAYERS)]
    shards = {(n, l): _pad_lanes(w_loc[n][l].astype(BF16)) for n, l in big}
    first = [('ffn1_wi', 0), ('ffn1_wo', 0)]
    wb = {n: [None] * N_LAYERS for n in BIG}
    for (n, l), g in zip(first, allgather_chips([shards[k] for k in first], "allgather_first")):
        wb[n][l] = _natural(n, g)

    sm_names = list(SMALL_SHARDED)
    placed = []
    for n in sm_names:
        mine = [jnp.where((chip == s) & (core == 0), w_loc[n], 0.0) for s in range(N_CHIPS)]
        placed.append(jnp.concatenate(mine, axis=SMALL_SHARDED[n]))
    summed = allreduce_small(_pack(placed), "allgather_small")
    w_full = dict(w_loc)
    w_full.update(zip(sm_names, _unpack(summed, [p.shape for p in placed])))

    loss, dx, grads = local_step(x[0], loss_target[0], w_full, wb, shards)
    loss = lax.psum(loss, ("x", "y", "c"))

    gsum = allreduce_small(_pack([grads[n] for n in SMALL]), "allreduce_small")
    g_loc = {}
    for n, g in zip(SMALL, _unpack(gsum, [grads[n].shape for n in SMALL])):
        g_loc[n] = _local_shard(g, SMALL_SHARDED[n], chip) if n in SMALL_SHARDED else g

    last_keys, last_parts = grads['last_round']
    sems, parts_thru, lands_thru, token = scatter_start(last_parts, "grad_scatter_last_start")
    chip_after_start = chip + token[0, 0].astype(chip.dtype)
    rows = {n: [None] * N_LAYERS for n in BIG}
    delta, new_m, new_v = {}, {}, {}

    def finish(keys, arrived, which_chip, tag):
        halves = [shard_sum(lax.dynamic_index_in_dim(grads[n][l][0], which_chip, 0, keepdims=False), got,
                            f"grad_shard_sum_{n}{l}") for (n, l), got in zip(keys, arrived)]
        others = sibling_swap(halves, False, f"grad_share_cores_{tag}")
        for (n, l), half, other in zip(keys, halves, others):
            lower = jnp.where(core == 0, half, other)
            upper = jnp.where(core == 0, other, half)
            rows[n][l] = jnp.concatenate([lower, upper], axis=0)[:, :w_loc[n].shape[-1]]
        for n in BIG:
            if n not in delta and all(r is not None for r in rows[n]):
                g_loc[n] = jnp.stack(rows[n])
                delta[n], new_m[n], new_v[n] = adamw(w_loc[n], m_loc[n], v_loc[n], g_loc[n], f"adamw_{n}")

    early = [k for k in big if k not in last_keys]
    finish(early, [grads[n][l][1] for n, l in early], chip_after_start, "early")
    pack = lambda d: _pack([d[n] for n in SMALL])[None]
    res = adamw(pack(w_loc), pack(m_loc), pack(v_loc), pack(g_loc), "adamw_small")
    for dst, r in zip((delta, new_m, new_v), res):
        dst.update(zip(SMALL, _unpack(r[0], [w_loc[n].shape for n in SMALL])))
    arrived_last = scatter_wait(sems, parts_thru, lands_thru, delta['ffn2_wi'], "grad_scatter_last_wait")
    finish(last_keys, arrived_last, chip, "last")

    return (loss, dx[None], *[g_loc[n] for n in WEIGHTS], *[delta[n] for n in WEIGHTS],
            *[new_m[n] for n in WEIGHTS], *[new_v[n] for n in WEIGHTS])
```

```python
import functools

import numpy as np
import jax
import jax.numpy as jnp
from jax import lax
from jax.experimental import pallas as pl
from jax.experimental.pallas import tpu as pltpu

F32 = jnp.float32
BF16 = jnp.bfloat16
MESH = pl.DeviceIdType.MESH

D_MODEL = 1024
D_FF = 2816
N_LAYERS = 2
HEADS = 6
HEAD_DIM = 64
MIX_W = HEADS * HEAD_DIM
LRU_W = 256
LRU_BLOCKS = 4
D_IN = 3468
D_IN_PAD = 3584
NORM_EPS = 1e-6
GN_EPS = 64e-5
LRU_C = 8.0
CHUNK = 64
CHUNKS_PER_STEP = 2
ROWS = 512
FF_CHUNK = 256
IN_CHUNK = 512
VMEM_LIMIT = 56 * 1024 * 1024

ADAM_LR, ADAM_B1, ADAM_B2, ADAM_EPS, ADAM_WD, ADAM_STEP = 0.001, 0.9, 0.999, 1e-08, 0.01, 10

WEIGHTS = ['ffn1_norm', 'ffn1_wi', 'ffn1_wo', 'mix_norm', 'w_in', 'w_out', 'lru_conv_w', 'lru_conv_b',
           'lru_gate_a_w', 'lru_gate_a_b', 'lru_gate_x_w', 'lru_gate_x_b', 'lru_lambda', 'lru_out_norm',
           'rwkv_mu', 'rwkv_w_up', 'rwkv_w_bias', 'rwkv_a_up', 'rwkv_a_bias', 'rwkv_g_up', 'rwkv_k_k',
           'rwkv_k_a', 'rwkv_r_k', 'rwkv_ln_g', 'rwkv_ln_b', 'rwkv_vres_w1', 'rwkv_vres_w2', 'rwkv_vres_b',
           'gdn_conv_w', 'gdn_a_log', 'gdn_dt_bias', 'gdn_norm', 'ffn2_norm', 'ffn2_wi', 'ffn2_wo', 'final_norm']
BIG = {'ffn1_wi': 2, 'ffn1_wo': 1, 'w_in': 2, 'w_out': 1, 'ffn2_wi': 2, 'ffn2_wo': 1}
SMALL_SHARDED = {'lru_conv_w': 2, 'rwkv_w_up': 2, 'rwkv_a_up': 2, 'rwkv_g_up': 2, 'rwkv_vres_w1': 1,
                 'rwkv_vres_w2': 2, 'gdn_conv_w': 2}
N_CHIPS = 4


def _params(sem=None):
    kw = dict(vmem_limit_bytes=VMEM_LIMIT)
    if sem is not None:
        kw['dimension_semantics'] = sem
    return pltpu.CompilerParams(**kw)


def _bdot(a, b, dims=(((1,), (0,)), ((), ()))):
    return lax.dot_general(a.astype(BF16), b.astype(BF16), dims, preferred_element_type=F32)


def _bdot_nt(a, b):
    return _bdot(a, b, (((1,), (1,)), ((), ())))


def _bdot_tn(a, b):
    return _bdot(a, b, (((0,), (0,)), ((), ())))


_DIMS = {'nn': (((1,), (0,)), ((), ())), 'nt': (((1,), (1,)), ((), ())), 'tn': (((0,), (0,)), ((), ()))}


def _split(a, terms):
    parts = []
    for _ in range(terms - 1):
        hi = a.astype(BF16)
        parts.append(hi)
        a = a - hi.astype(F32)
    parts.append(a.astype(BF16))
    return parts


_BATCH_DIMS = {'nn': (((2,), (1,)), ((0,), (0,))), 'nt': (((2,), (2,)), ((0,), (0,))),
               'tn': (((1,), (1,)), ((0,), (0,)))}


def _dot3(a, b, kind):
    ah, al = _split(a, 2)
    bh, bl = _split(b, 2)
    dims = _BATCH_DIMS[kind] if a.ndim == 3 else _DIMS[kind]
    d = lambda p, q: lax.dot_general(p, q, dims, preferred_element_type=F32)
    return d(ah, bh) + (d(ah, bl) + d(al, bh))


@functools.partial(jax.custom_vjp, nondiff_argnums=(2,))
def _cdot_k(a, b, kind):
    return _dot3(a, b, kind)


def _cdot_k_fwd(a, b, kind):
    return _dot3(a, b, kind), (a, b)


def _cdot_k_bwd(kind, res, ct):
    a, b = res
    if kind == 'nn':
        return _dot3(ct, b, 'nt'), _dot3(a, ct, 'tn')
    if kind == 'nt':
        return _dot3(ct, b, 'nn'), _dot3(ct, a, 'tn')
    return _dot3(b, ct, 'nt'), _dot3(a, ct, 'nn')


_cdot_k.defvjp(_cdot_k_fwd, _cdot_k_bwd)


def _dot1(a, b, kind):
    dims = _BATCH_DIMS[kind] if a.ndim == 3 else _DIMS[kind]
    return lax.dot_general(a.astype(BF16), b.astype(BF16), dims, preferred_element_type=F32)


@functools.partial(jax.custom_vjp, nondiff_argnums=(2,))
def _cdot1_k(a, b, kind):
    return _dot1(a, b, kind)


def _cdot1_k_fwd(a, b, kind):
    return _dot1(a, b, kind), (a, b)


def _cdot1_k_bwd(kind, res, ct):
    a, b = res
    if kind == 'nn':
        return _dot1(ct, b, 'nt'), _dot1(a, ct, 'tn')
    if kind == 'nt':
        return _dot1(ct, b, 'nn'), _dot1(ct, a, 'tn')
    return _dot1(b, ct, 'nt'), _dot1(a, ct, 'nn')


_cdot1_k.defvjp(_cdot1_k_fwd, _cdot1_k_bwd)


def _cdot(a, b):
    return _cdot1_k(a, b, 'nn')


def _cdot_nt(a, b):
    return _cdot1_k(a, b, 'nt')


def _cdot_tn(a, b):
    return _cdot1_k(a, b, 'tn')


def _hdot(a, b):
    return _cdot_k(a, b, 'nn')


def _dot_exact(x, m01, kind):
    d = lambda p: lax.dot_general(p, m01.astype(BF16), _DIMS[kind], preferred_element_type=F32)
    hi, mid, lo = _split(x, 3)
    return d(hi) + (d(mid) + d(lo))


@functools.partial(jax.custom_vjp, nondiff_argnums=(1,))
def _xdot(x, make_m):
    return _dot_exact(x, make_m(), 'nn')


def _xdot_fwd(x, make_m):
    return _dot_exact(x, make_m(), 'nn'), None


def _xdot_bwd(make_m, _, ct):
    return (_dot_exact(ct, make_m(), 'nt'),)


_xdot.defvjp(_xdot_fwd, _xdot_bwd)


def _iota2(n, m):
    return lax.broadcasted_iota(jnp.int32, (n, m), 0), lax.broadcasted_iota(jnp.int32, (n, m), 1)


def _head_blocks(w):
    ri, ci = _iota2(w, w)
    return (ri // HEAD_DIM == ci // HEAD_DIM).astype(F32)


def _segsum(x):
    return _xdot(x, functools.partial(_head_blocks, x.shape[-1]))


def _cumsum_rows(x):
    return _cumsum_k(x, x.shape[0])


@functools.partial(jax.custom_vjp, nondiff_argnums=(1,))
def _cumsum_k(x, n):
    return _lower_dot(x, n, False)


def _lower_dot(x, n, transpose):
    ri, ci = _iota2(n, n)
    m = ((ri <= ci) if transpose else (ri >= ci)).astype(BF16)
    d = lambda p: lax.dot_general(m, p, _DIMS['nn'], preferred_element_type=F32)
    hi, mid, lo = _split(x, 3)
    return d(hi) + (d(mid) + d(lo))


def _cumsum_k_fwd(x, n):
    return _lower_dot(x, n, False), None


def _cumsum_k_bwd(n, _, ct):
    return (_lower_dot(ct, n, True),)


_cumsum_k.defvjp(_cumsum_k_fwd, _cumsum_k_bwd)


def _rms(x, g):
    return x * lax.rsqrt(jnp.mean(x * x, axis=-1, keepdims=True) + NORM_EPS) * g


DENSE_ROWS = 1024


def _row_loop(n_rows, fn):
    rows = min(DENSE_ROWS, n_rows)

    def step(i, c):
        fn(pl.ds(pl.multiple_of(i * rows, rows), rows))
        return c
    lax.fori_loop(0, n_rows // rows, step, 0)


def ffn_fwd(x, g, wi, wo, name, hosted=()):
    T = x.shape[0]
    nj = D_FF // FF_CHUNK
    gather = ChipGather(list(hosted))
    n = gather.n

    def body(*refs):
        x_ref, g_ref, wg_ref, wu_ref, wo_ref = refs[:5]
        hx, o_ref, ho = refs[5:5 + n], refs[5 + n], refs[6 + n:6 + 2 * n]
        h_ref, acc_ref = refs[6 + 2 * n:8 + 2 * n]
        sems = refs[8 + 2 * n:]
        j = pl.program_id(0)

        @pl.when(j == 0)
        def _():
            gather.start(hx, ho, sems)

            def init(r):
                h_ref[r, :] = _rms(x_ref[r, :], g_ref[...]).astype(BF16)
                acc_ref[r, :] = jnp.zeros((r.size, D_MODEL), F32)
            _row_loop(T, init)

        def blk(r):
            hb = h_ref[r, :]
            gate = jnp.dot(hb, wg_ref[...], preferred_element_type=F32)
            up = jnp.dot(hb, wu_ref[...], preferred_element_type=F32)
            a = (gate * jax.nn.sigmoid(gate) * up).astype(BF16)
            acc_ref[r, :] += jnp.dot(a, wo_ref[...], preferred_element_type=F32)
        _row_loop(T, blk)

        @pl.when(j == nj - 2)
        def _():
            gather.relay(hx, ho, sems)

        @pl.when(j == nj - 1)
        def _():
            def fin(r):
                o_ref[r, :] = x_ref[r, :] + 0.5 * acc_ref[r, :]
            _row_loop(T, fin)
            gather.finish(hx, ho, sems)

    full = pl.BlockSpec((T, D_MODEL), lambda j: (0, 0))
    res = pl.pallas_call(
        body, name=name, grid=(nj,),
        in_specs=[full, pl.BlockSpec((1, D_MODEL), lambda j: (0, 0)),
                  pl.BlockSpec((D_MODEL, FF_CHUNK), lambda j: (0, j)),
                  pl.BlockSpec((D_MODEL, FF_CHUNK), lambda j: (0, j + nj)),
                  pl.BlockSpec((FF_CHUNK, D_MODEL), lambda j: (j, 0))] + gather.in_specs,
        out_specs=tuple([full] + gather.out_specs),
        out_shape=tuple([jax.ShapeDtypeStruct((T, D_MODEL), F32)] + gather.out_shape),
        scratch_shapes=[pltpu.VMEM((T, D_MODEL), BF16), pltpu.VMEM((T, D_MODEL), F32)] + gather.scratch,
        compiler_params=_params(("arbitrary",)))(x, g, wi, wi, wo, *hosted)
    return res[0], list(res[1:])


def _norm_bwd_rows(x, g, dh, dres):
    rstd = lax.rsqrt(jnp.mean(x * x, axis=-1, keepdims=True) + NORM_EPS)
    xh = x * rstd
    dxh = dh * g
    dx = rstd * (dxh - xh * jnp.mean(dxh * xh, axis=-1, keepdims=True))
    return dres + dx, jnp.sum(dh * xh, axis=0, keepdims=True)


def ffn_bwd(x, dy, g, wi, wo, name, hosted=()):
    T = x.shape[0]
    nj = D_FF // FF_CHUNK
    scatter = ChipScatter(list(hosted))
    n = scatter.n

    def body(*refs):
        x_ref, dy_ref, g_ref, wg_ref, wu_ref, wo_ref = refs[:6]
        hx = refs[6:6 + n]
        dx_ref, dg_ref, dwg_ref, dwu_ref, dwo_ref = refs[6 + n:11 + n]
        ho = refs[11 + n:11 + 2 * n]
        h_ref, da_ref, dh_ref = refs[11 + 2 * n:14 + 2 * n]
        sems = refs[14 + 2 * n:]
        j = pl.program_id(0)

        @pl.when(j == 0)
        def _():
            scatter.start(hx, ho, sems)

            def init(r):
                h_ref[r, :] = _rms(x_ref[r, :], g_ref[...]).astype(BF16)
                da_ref[r, :] = (0.5 * dy_ref[r, :]).astype(BF16)
                dh_ref[r, :] = jnp.zeros((r.size, D_MODEL), F32)
            _row_loop(T, init)

        dwg_ref[...] = jnp.zeros_like(dwg_ref)
        dwu_ref[...] = jnp.zeros_like(dwu_ref)
        dwo_ref[...] = jnp.zeros_like(dwo_ref)

        def blk(r):
            hb = h_ref[r, :]
            db = da_ref[r, :]
            gate = jnp.dot(hb, wg_ref[...], preferred_element_type=F32)
            up = jnp.dot(hb, wu_ref[...], preferred_element_type=F32)
            sg = jax.nn.sigmoid(gate)
            sl = gate * sg
            da = _bdot_nt(db, wo_ref[...])
            dup = (da * sl).astype(BF16)
            dgate = (da * up * (sg * (1.0 + gate * (1.0 - sg)))).astype(BF16)
            dwo_ref[...] += _bdot_tn((sl * up).astype(BF16), db)
            dwg_ref[...] += _bdot_tn(hb, dgate)
            dwu_ref[...] += _bdot_tn(hb, dup)
            dh_ref[r, :] += _bdot_nt(dgate, wg_ref[...]) + _bdot_nt(dup, wu_ref[...])
        _row_loop(T, blk)

        @pl.when(j == nj - 1)
        def _():
            dg_ref[...] = jnp.zeros_like(dg_ref)

            def fin(r):
                dx, dg = _norm_bwd_rows(x_ref[r, :], g_ref[...], dh_ref[r, :], dy_ref[r, :])
                dx_ref[r, :] = dx
                dg_ref[...] += dg
            _row_loop(T, fin)
            scatter.finish(hx, ho, sems)

    full = pl.BlockSpec((T, D_MODEL), lambda j: (0, 0))
    vec = pl.BlockSpec((1, D_MODEL), lambda j: (0, 0))
    res = pl.pallas_call(
        body, name=name, grid=(nj,),
        in_specs=[full, full, vec,
                  pl.BlockSpec((D_MODEL, FF_CHUNK), lambda j: (0, j)),
                  pl.BlockSpec((D_MODEL, FF_CHUNK), lambda j: (0, j + nj)),
                  pl.BlockSpec((FF_CHUNK, D_MODEL), lambda j: (j, 0))] + scatter.in_specs,
        out_specs=tuple([full, vec,
                         pl.BlockSpec((D_MODEL, FF_CHUNK), lambda j: (0, j)),
                         pl.BlockSpec((D_MODEL, FF_CHUNK), lambda j: (0, j)),
                         pl.BlockSpec((FF_CHUNK, D_MODEL), lambda j: (j, 0))] + scatter.out_specs),
        out_shape=tuple([jax.ShapeDtypeStruct((T, D_MODEL), F32), jax.ShapeDtypeStruct((1, D_MODEL), F32),
                         jax.ShapeDtypeStruct((D_MODEL, D_FF), F32), jax.ShapeDtypeStruct((D_MODEL, D_FF), F32),
                         jax.ShapeDtypeStruct((D_FF, D_MODEL), F32)] + scatter.out_shape),
        scratch_shapes=[pltpu.VMEM((T, D_MODEL), BF16), pltpu.VMEM((T, D_MODEL), BF16),
                        pltpu.VMEM((T, D_MODEL), F32)] + scatter.scratch,
        compiler_params=_params(("arbitrary",)))(x, dy, g, wi, wi, wo, *hosted)
    return res[0], res[1], res[2], res[3], res[4], list(res[5:])


def proj_fwd(x, g, w, name):
    T = x.shape[0]
    nj = D_IN_PAD // IN_CHUNK

    def body(x_ref, g_ref, w_ref, o_ref, h_ref):
        @pl.when(pl.program_id(0) == 0)
        def _():
            def init(r):
                h_ref[r, :] = _rms(x_ref[r, :], g_ref[...]).astype(BF16)
            _row_loop(T, init)

        def blk(r):
            o_ref[r, :] = jnp.dot(h_ref[r, :], w_ref[...], preferred_element_type=F32)
        _row_loop(T, blk)

    return pl.pallas_call(
        body, name=name, grid=(nj,),
        in_specs=[pl.BlockSpec((T, D_MODEL), lambda j: (0, 0)), pl.BlockSpec((1, D_MODEL), lambda j: (0, 0)),
                  pl.BlockSpec((D_MODEL, IN_CHUNK), lambda j: (0, j))],
        out_specs=pl.BlockSpec((T, IN_CHUNK), lambda j: (0, j)),
        out_shape=jax.ShapeDtypeStruct((T, D_IN_PAD), F32),
        scratch_shapes=[pltpu.VMEM((T, D_MODEL), BF16)],
        compiler_params=_params(("arbitrary",)))(x, g, w)


def proj_bwd(x, dres, g, w, dp, name):
    T = x.shape[0]
    nj = D_IN_PAD // IN_CHUNK

    def body(x_ref, dres_ref, g_ref, w_ref, dp_ref, dx_ref, dg_ref, dw_ref, h_ref, dh_ref):
        j = pl.program_id(0)

        @pl.when(j == 0)
        def _():
            def init(r):
                h_ref[r, :] = _rms(x_ref[r, :], g_ref[...]).astype(BF16)
                dh_ref[r, :] = jnp.zeros((r.size, D_MODEL), F32)
            _row_loop(T, init)

        dw_ref[...] = jnp.zeros_like(dw_ref)

        def blk(r):
            dpb = dp_ref[r, :].astype(BF16)
            dw_ref[...] += _bdot_tn(h_ref[r, :], dpb)
            dh_ref[r, :] += _bdot_nt(dpb, w_ref[...])
        _row_loop(T, blk)

        @pl.when(j == nj - 1)
        def _():
            dg_ref[...] = jnp.zeros_like(dg_ref)

            def fin(r):
                dx, dg = _norm_bwd_rows(x_ref[r, :], g_ref[...], dh_ref[r, :], dres_ref[r, :])
                dx_ref[r, :] = dx
                dg_ref[...] += dg
            _row_loop(T, fin)

    full = pl.BlockSpec((T, D_MODEL), lambda j: (0, 0))
    vec = pl.BlockSpec((1, D_MODEL), lambda j: (0, 0))
    return pl.pallas_call(
        body, name=name, grid=(nj,),
        in_specs=[full, full, vec, pl.BlockSpec((D_MODEL, IN_CHUNK), lambda j: (0, j)),
                  pl.BlockSpec((T, IN_CHUNK), lambda j: (0, j))],
        out_specs=(full, vec, pl.BlockSpec((D_MODEL, IN_CHUNK), lambda j: (0, j))),
        out_shape=(jax.ShapeDtypeStruct((T, D_MODEL), F32), jax.ShapeDtypeStruct((1, D_MODEL), F32),
                   jax.ShapeDtypeStruct((D_MODEL, D_IN_PAD), F32)),
        scratch_shapes=[pltpu.VMEM((T, D_MODEL), BF16), pltpu.VMEM((T, D_MODEL), F32)],
        compiler_params=_params(("arbitrary",)))(x, dres, g, w, dp)


def out_fwd(mixed, w, x, name):
    T = x.shape[0]

    def body(m_ref, w_ref, x_ref, o_ref):
        o_ref[...] = x_ref[...] + jnp.dot(m_ref[...].astype(BF16), w_ref[...], preferred_element_type=F32)

    blk = pl.BlockSpec((ROWS, D_MODEL), lambda i: (i, 0))
    return pl.pallas_call(
        body, name=name, grid=(T // ROWS,),
        in_specs=[blk, pl.BlockSpec((D_MODEL, D_MODEL), lambda i: (0, 0)), blk],
        out_specs=blk, out_shape=jax.ShapeDtypeStruct((T, D_MODEL), F32),
        compiler_params=_params(("arbitrary",)))(mixed, w, x)


def out_bwd(mixed, w, dy, name):
    T = dy.shape[0]

    def body(m_ref, w_ref, dy_ref, dm_ref, dw_ref):
        @pl.when(pl.program_id(0) == 0)
        def _():
            dw_ref[...] = jnp.zeros_like(dw_ref)
        dyb = dy_ref[...].astype(BF16)
        dm_ref[...] = _bdot_nt(dyb, w_ref[...])
        dw_ref[...] += _bdot_tn(m_ref[...].astype(BF16), dyb)

    blk = pl.BlockSpec((ROWS, D_MODEL), lambda i: (i, 0))
    sq = pl.BlockSpec((D_MODEL, D_MODEL), lambda i: (0, 0))
    return pl.pallas_call(
        body, name=name, grid=(T // ROWS,),
        in_specs=[blk, sq, blk], out_specs=(blk, sq),
        out_shape=(jax.ShapeDtypeStruct((T, D_MODEL), F32), jax.ShapeDtypeStruct((D_MODEL, D_MODEL), F32)),
        compiler_params=_params(("arbitrary",)))(mixed, w, dy)


def loss_head(x, g, target, name):
    T = x.shape[0]

    def body(x_ref, g_ref, t_ref, loss_ref, dx_ref, dg_ref):
        @pl.when(pl.program_id(0) == 0)
        def _():
            loss_ref[...] = jnp.zeros_like(loss_ref)
            dg_ref[...] = jnp.zeros_like(dg_ref)
        xb = x_ref[...]
        rstd = lax.rsqrt(jnp.mean(xb * xb, axis=-1, keepdims=True) + NORM_EPS)
        xh = xb * rstd
        err = xh * g_ref[...] - t_ref[...]
        loss_ref[...] += 0.5 * jnp.sum(jnp.mean(err * err, axis=-1, keepdims=True), axis=0, keepdims=True)
        dy = err * (1.0 / D_MODEL)
        dg_ref[...] += jnp.sum(dy * xh, axis=0, keepdims=True)
        dxh = dy * g_ref[...]
        dx_ref[...] = rstd * (dxh - xh * jnp.mean(dxh * xh, axis=-1, keepdims=True))

    blk = pl.BlockSpec((ROWS, D_MODEL), lambda i: (i, 0))
    vec = pl.BlockSpec((1, D_MODEL), lambda i: (0, 0))
    return pl.pallas_call(
        body, name=name, grid=(T // ROWS,),
        in_specs=[blk, vec, blk], out_specs=(pl.BlockSpec((1, 1), lambda i: (0, 0)), blk, vec),
        out_shape=(jax.ShapeDtypeStruct((1, 1), F32), jax.ShapeDtypeStruct((T, D_MODEL), F32),
                   jax.ShapeDtypeStruct((1, D_MODEL), F32)),
        compiler_params=_params(("arbitrary",)))(x, g, target)


def rowwise_fwd(fn, rows, shared, out_widths, name):
    T = rows[0].shape[0]
    n_in = len(rows) + len(shared)

    def body(*refs):
        res = fn(*[r[...] for r in refs[:n_in]])
        for o, v in zip(refs[n_in:], res):
            o[...] = v

    in_specs = ([pl.BlockSpec((ROWS, a.shape[1]), lambda i: (i, 0)) for a in rows]
                + [pl.BlockSpec(a.shape, lambda i: (0, 0)) for a in shared])
    return pl.pallas_call(
        body, name=name, grid=(T // ROWS,), in_specs=in_specs,
        out_specs=tuple(pl.BlockSpec((ROWS, w), lambda i: (i, 0)) for w in out_widths),
        out_shape=tuple(jax.ShapeDtypeStruct((T, w), F32) for w in out_widths),
        compiler_params=_params(("arbitrary",)))(*rows, *shared)


def rowwise_bwd(fn, rows, shared, cts, name, ct_fn=None):
    T = rows[0].shape[0]
    nr, ns, nc = len(rows), len(shared), len(cts)

    def body(*refs):
        ins = [r[...] for r in refs[:nr + ns]]
        ctv = tuple(r[...] for r in refs[nr + ns:nr + ns + nc])
        outs = refs[nr + ns + nc:]
        _, vjp = jax.vjp(fn, *ins)
        grads = vjp(ct_fn(*ctv) if ct_fn is not None else ctv)
        for k in range(nr):
            outs[k][...] = grads[k]

        @pl.when(pl.program_id(0) == 0)
        def _():
            for k in range(ns):
                outs[nr + k][...] = jnp.zeros_like(outs[nr + k])
        for k in range(ns):
            outs[nr + k][...] += grads[nr + k]

    row_spec = lambda a: pl.BlockSpec((ROWS, a.shape[1]), lambda i: (i, 0))
    sh_spec = lambda a: pl.BlockSpec(a.shape, lambda i: (0, 0))
    return pl.pallas_call(
        body, name=name, grid=(T // ROWS,),
        in_specs=[row_spec(a) for a in rows] + [sh_spec(a) for a in shared] + [row_spec(a) for a in cts],
        out_specs=tuple([row_spec(a) for a in rows] + [sh_spec(a) for a in shared]),
        out_shape=tuple(jax.ShapeDtypeStruct(a.shape, F32) for a in list(rows) + list(shared)),
        compiler_params=_params(("arbitrary",)))(*rows, *shared, *cts)


def shift_rows(x, s):
    return jnp.pad(x, ((s, 0), (0, 0)))[:x.shape[0]]


def unshift_rows(x, s):
    return jnp.pad(x, ((0, s), (0, 0)))[s:]


def _neg_expm1(y):
    series = -(y * (1.0 + y * (0.5 + y * (1.0 / 6.0 + y * (1.0 / 24.0)))))
    return jnp.where(y > -0.05, series, 1.0 - jnp.exp(y))


def lru_pre_fn(x0, x1, x2, x3, first, w0, w1, w2, w3, cb, ga, gab, gx, gxb, lam):
    xc = w3 * x0 + w2 * x1 + w1 * x2 + w0 * x3 + cb
    r = jax.nn.sigmoid(_hdot(xc, ga) + gab)
    i = jax.nn.sigmoid(_hdot(xc, gx) + gxb)
    log_a = -LRU_C * r * jax.nn.softplus(-lam)
    a = jnp.exp(log_a)
    mult = jnp.where(first > 0.5, 1.0, jnp.sqrt(_neg_expm1(2.0 * log_a)))
    return a, mult * i * xc


def lru_post_fn(h, py, og):
    return (_rms(h * jax.nn.gelu(py), og),)


def lru_scan(a, b, reverse, name):
    T, C = a.shape
    nb = T // 8

    def body(a_ref, b_ref, h_ref):
        rows = lax.broadcasted_iota(jnp.int32, (8, C), 0)

        def blk(i, carry):
            j = nb - 1 - i if reverse else i
            r = pl.ds(pl.multiple_of(j * 8, 8), 8)
            A = a_ref[r, :]
            B = b_ref[r, :]
            for s in (1, 2, 4):
                if reverse:
                    keep = rows < 8 - s
                    sh = 8 - s
                else:
                    keep = rows >= s
                    sh = s
                Bs = jnp.where(keep, pltpu.roll(B, sh, 0), 0.0)
                As = jnp.where(keep, pltpu.roll(A, sh, 0), 1.0)
                B = B + A * Bs
                A = A * As
            hb = B + A * carry
            h_ref[r, :] = hb
            edge = 0 if reverse else 7
            return jnp.sum(jnp.where(rows == edge, hb, 0.0), axis=0, keepdims=True)

        lax.fori_loop(0, nb, blk, jnp.zeros((1, C), F32))

    full = pl.BlockSpec((T, C), lambda: (0, 0))
    return pl.pallas_call(body, name=name, in_specs=[full, full], out_specs=full,
                          out_shape=jax.ShapeDtypeStruct((T, C), F32), compiler_params=_params())(a, b)


def make_rwkv_pre_fn(has_vres):
    def fn(p, pp, *rest):
        if has_vres:
            vf, mu, w_up, w_b, a_up, a_b, g_up, kk_w, ka_w, vw1, vw2, vb = rest
        else:
            mu, w_up, w_b, a_up, a_b, g_up, kk_w, ka_w = rest
        xm = p + (pp - p) * mu
        r, k, v = xm[:, 0:384], xm[:, 384:768], xm[:, 768:1152]
        xw, xa, xg = xm[:, 1152:1216], xm[:, 1216:1280], xm[:, 1280:1408]
        w_log = -jax.nn.softplus(-(w_b + _hdot(jnp.tanh(xw), w_up))) - 0.5
        lw = -jnp.exp(w_log)
        a = jax.nn.sigmoid(a_b + _hdot(xa, a_up))
        g = _hdot(jax.nn.sigmoid(xg), g_up)
        if has_vres:
            v = v + (vf - v) * jax.nn.sigmoid(vb + _hdot(_hdot(v, vw1), vw2))
        kkx = k * kk_w
        kk = kkx * lax.rsqrt(_segsum(kkx * kkx) + 1e-6)
        k2 = k * (1.0 + (a - 1.0) * ka_w)
        return r, lw, k2, v, kk, a, g
    return fn


def rwkv_post_fn(y, r, k2, v, g, ln_g, ln_b, r_k):
    mean = _segsum(y) * (1.0 / HEAD_DIM)
    yc = y - mean
    var = _segsum(yc * yc) * (1.0 / HEAD_DIM)
    yn = yc * lax.rsqrt(var + GN_EPS) * ln_g + ln_b
    bonus = _segsum(r * k2 * r_k) * v
    return ((yn + bonus) * g,)


def _head_expander(first_lane):
    ri, ci = _iota2(128, MIX_W)
    return (ri == ci // HEAD_DIM + first_lane).astype(F32)


def gdn_pre_fn(x0, x1, x2, x3, ab, w0, w1, w2, w3, alog, dtb):
    qkv = jax.nn.silu(w3 * x0 + w2 * x1 + w1 * x2 + w0 * x3)
    q, k, v = qkv[:, 0:384], qkv[:, 384:768], qkv[:, 768:1152]
    q = q * lax.rsqrt(_segsum(q * q) + 1e-6) * (HEAD_DIM ** -0.5)
    k = k * lax.rsqrt(_segsum(k * k) + 1e-6)
    g = -jnp.exp(alog) * jax.nn.softplus(ab + dtb)
    beta = jax.nn.sigmoid(ab)
    ge = _xdot(g, functools.partial(_head_expander, 0))
    be = _xdot(beta, functools.partial(_head_expander, HEADS))
    return q, k, v, ge, be


def gdn_post_fn(o, z, ng):
    ms = _segsum(o * o) * (1.0 / HEAD_DIM)
    return (o * lax.rsqrt(ms + NORM_EPS) * ng * jax.nn.silu(z),)


def _neumann_inv(m):
    n = m.shape[-1]
    ri, ci = _iota2(n, n)
    eye = (ri == ci).astype(F32)
    md = jnp.where(ri // 16 == ci // 16, m, 0.0)
    mo = m - md
    t0 = eye + md
    p2 = _hdot(md, md)
    t0 = t0 + _hdot(t0, p2)
    p4 = _hdot(p2, p2)
    t0 = t0 + _hdot(t0, p4)
    p8 = _hdot(p4, p4)
    t0 = t0 + _hdot(t0, p8)
    nn = _hdot(t0, mo)
    n2 = _hdot(nn, nn)
    t1 = eye + nn + n2 + _hdot(nn, n2)
    return _hdot(t1, t0)


@jax.custom_vjp
def _inv_saved(m, t_saved):
    return t_saved


def _inv_saved_fwd(m, t_saved):
    return t_saved, t_saved


def _inv_saved_bwd(t_saved, dt):
    tt = jnp.swapaxes(t_saved, -1, -2)
    return _hdot(_hdot(tt, dt), tt), jnp.zeros_like(t_saved)


_inv_saved.defvjp(_inv_saved_fwd, _inv_saved_bwd)


def _heads(x):
    return jnp.concatenate([x[None, :, h * HEAD_DIM:(h + 1) * HEAD_DIM] for h in range(HEADS)], axis=0)


def _unheads(y):
    return jnp.concatenate([lax.index_in_dim(y, h, 0, keepdims=False) for h in range(HEADS)], axis=1)


def rwkv_heads(s0, r, lw, k2, v, kk, a, inv):
    n = r.shape[0]
    ri, ci = _iota2(n, n)
    low, strict = ri >= ci, ri > ci
    cs = _cumsum_rows(lw)
    cl = jnp.sum(lw, axis=0, keepdims=True)
    p_in, p_prev, p_inv = jnp.exp(cs), jnp.exp(cs - lw), jnp.exp(-cs)
    p_rest, p_all = jnp.exp(cl - cs), jnp.exp(cl)
    bd = kk * a
    at, rt = _heads(-kk * p_prev), _heads(r * p_in)
    bh, kh = _heads(bd * p_inv), _heads(k2 * p_inv)
    vh = _heads(v)
    m_ab = jnp.where(strict, _cdot_nt(at, bh), 0.0)
    m_ak = jnp.where(strict, _cdot_nt(at, kh), 0.0)
    m_rb = jnp.where(low, _cdot_nt(rt, bh), 0.0)
    m_rk = jnp.where(low, _cdot_nt(rt, kh), 0.0)
    sa = _cdot(inv(m_ab), _cdot_nt(at, s0) + _cdot(m_ak, vh))
    y = _cdot_nt(rt, s0) + _cdot(m_rb, sa) + _cdot(m_rk, vh)
    s1 = s0 * _heads(p_all) + _cdot_tn(sa, _heads(bd * p_rest)) + _cdot_tn(vh, _heads(k2 * p_rest))
    return _unheads(y), s1


def gdn_heads(s0, q, k, v, ge, be, inv):
    n = q.shape[0]
    ri, ci = _iota2(n, n)
    low, strict = ri >= ci, ri > ci
    gc = _cumsum_rows(ge)
    gl = jnp.sum(ge, axis=0, keepdims=True)
    gch = _heads(gc)
    decay = jnp.where(low, jnp.exp(jnp.where(low, gch - jnp.swapaxes(gch, 1, 2), 0.0)), 0.0)
    kb = k * be
    e = jnp.exp(gc)
    kh = _heads(k)
    m = -jnp.where(strict, _cdot_nt(_heads(kb), kh) * decay, 0.0)
    mr = jnp.where(low, _cdot_nt(_heads(q), kh) * decay, 0.0)
    u = _cdot(inv(m), _heads(v * be) - _cdot_nt(_heads(kb * e), s0))
    y = _cdot_nt(_heads(q * e), s0) + _cdot(mr, u)
    s1 = s0 * _heads(jnp.exp(gl)) + _cdot_tn(u, _heads(k * jnp.exp(gl - gc)))
    return _unheads(y), s1


def core_fwd(heads_fn, ins, name, hosted=()):
    T = ins[0].shape[0]
    nc = T // CHUNK
    steps = nc // CHUNKS_PER_STEP
    n = len(ins)
    gather = ChipGather(list(hosted))
    ng = gather.n

    def body(*refs):
        hx = refs[n:n + ng]
        y_ref, s0_ref, t_ref = refs[n + ng:n + ng + 3]
        ho = refs[n + ng + 3:n + 2 * ng + 3]
        s_ref = refs[n + 2 * ng + 3]
        sems = refs[n + 2 * ng + 4:]
        c = pl.program_id(0)

        @pl.when(c == 0)
        def _():
            gather.start(hx, ho, sems)
            s_ref[...] = jnp.zeros_like(s_ref)

        state = s_ref[...]
        for u in range(CHUNKS_PER_STEP):
            rows = slice(u * CHUNK, (u + 1) * CHUNK)
            kept = []

            def inv(m):
                kept.append(_neumann_inv(m))
                return kept[0]

            y, after = heads_fn(state, *[r[rows, :] for r in refs[:n]], inv)
            y_ref[rows, :] = y
            s0_ref[u] = state
            t_ref[u] = kept[0]
            state = after
        s_ref[...] = state

        @pl.when(c == max(steps - 2, 0))
        def _():
            gather.relay(hx, ho, sems)

        @pl.when(c == steps - 1)
        def _():
            gather.finish(hx, ho, sems)

    row = pl.BlockSpec((CHUNKS_PER_STEP * CHUNK, MIX_W), lambda c: (c, 0))
    st_shape = (HEADS, HEAD_DIM, HEAD_DIM)
    st = pl.BlockSpec((CHUNKS_PER_STEP,) + st_shape, lambda c: (c, 0, 0, 0))
    res = pl.pallas_call(
        body, name=name, grid=(steps,), in_specs=[row] * n + gather.in_specs,
        out_specs=tuple([row, st, st] + gather.out_specs),
        out_shape=tuple([jax.ShapeDtypeStruct((T, MIX_W), F32), jax.ShapeDtypeStruct((nc,) + st_shape, F32),
                         jax.ShapeDtypeStruct((nc,) + st_shape, F32)] + gather.out_shape),
        scratch_shapes=[pltpu.VMEM(st_shape, F32)] + gather.scratch,
        compiler_params=_params(("arbitrary",)))(*ins, *hosted)
    return res[0], res[1], res[2], list(res[3:])


def core_bwd(heads_fn, ins, s0_all, t_all, dy, name):
    T = ins[0].shape[0]
    nc = T // CHUNK
    steps = nc // CHUNKS_PER_STEP
    n = len(ins)

    def body(*refs):
        s0_ref, t_ref, dy_ref = refs[n:n + 3]
        outs = refs[n + 3:n + 3 + n]
        ds_ref = refs[n + 3 + n]

        @pl.when(pl.program_id(0) == 0)
        def _():
            ds_ref[...] = jnp.zeros_like(ds_ref)

        d_state = ds_ref[...]
        for u in reversed(range(CHUNKS_PER_STEP)):
            rows = slice(u * CHUNK, (u + 1) * CHUNK)
            t_saved = t_ref[u]
            f = lambda s0, *xs: heads_fn(s0, *xs, lambda m: _inv_saved(m, t_saved))
            _, vjp = jax.vjp(f, s0_ref[u], *[r[rows, :] for r in refs[:n]])
            grads = vjp((dy_ref[rows, :], d_state))
            d_state = grads[0]
            for k in range(n):
                outs[k][rows, :] = grads[1 + k]
        ds_ref[...] = d_state

    row = pl.BlockSpec((CHUNKS_PER_STEP * CHUNK, MIX_W), lambda c: (steps - 1 - c, 0))
    st_shape = (HEADS, HEAD_DIM, HEAD_DIM)
    st = pl.BlockSpec((CHUNKS_PER_STEP,) + st_shape, lambda c: (steps - 1 - c, 0, 0, 0))
    return pl.pallas_call(
        body, name=name, grid=(steps,), in_specs=[row] * n + [st, st, row], out_specs=tuple([row] * n),
        out_shape=tuple(jax.ShapeDtypeStruct((T, MIX_W), F32) for _ in range(n)),
        scratch_shapes=[pltpu.VMEM(st_shape, F32)],
        compiler_params=_params(("arbitrary",)))(*ins, s0_all, t_all, dy)


def _block_diag(w):
    out = jnp.zeros((LRU_W, LRU_W), w.dtype)
    for n in range(LRU_BLOCKS):
        out = lax.dynamic_update_slice(out, w[n], (n * 64, n * 64))
    return out


def _block_diag_grad(g):
    return jnp.stack([g[n * 64:(n + 1) * 64, n * 64:(n + 1) * 64] for n in range(LRU_BLOCKS)])


def _row(v):
    return v.reshape(1, -1)


def _pad128(v):
    return jnp.pad(v.reshape(1, -1), ((0, 0), (0, 128 - v.size)))


def _layer_shared(w, l):
    cw = w['lru_conv_w'][l]
    lru_pre = [_row(cw[0]), _row(cw[1]), _row(cw[2]), _row(cw[3]), _row(w['lru_conv_b'][l]),
               _block_diag(w['lru_gate_a_w'][l]), _row(w['lru_gate_a_b'][l]),
               _block_diag(w['lru_gate_x_w'][l]), _row(w['lru_gate_x_b'][l]), _row(w['lru_lambda'][l])]
    rw_pre = [_row(w['rwkv_mu'][l]), w['rwkv_w_up'][l], _row(w['rwkv_w_bias'][l]), w['rwkv_a_up'][l],
              _row(w['rwkv_a_bias'][l]), w['rwkv_g_up'][l], _row(w['rwkv_k_k'][l]), _row(w['rwkv_k_a'][l])]
    if l > 0:
        rw_pre += [w['rwkv_vres_w1'][l - 1], w['rwkv_vres_w2'][l - 1], _row(w['rwkv_vres_b'][l - 1])]
    rw_post = [_row(w['rwkv_ln_g'][l]), _row(w['rwkv_ln_b'][l]), _row(w['rwkv_r_k'][l])]
    gw = w['gdn_conv_w'][l]
    gdn_pre = [_row(gw[0]), _row(gw[1]), _row(gw[2]), _row(gw[3]), _pad128(w['gdn_a_log'][l]),
               _pad128(w['gdn_dt_bias'][l])]
    gdn_post = [_row(jnp.tile(w['gdn_norm'][l], HEADS))]
    return dict(lru_pre=lru_pre, lru_post=[_row(w['lru_out_norm'][l])], rw_pre=rw_pre, rw_post=rw_post,
                gdn_pre=gdn_pre, gdn_post=gdn_post)


def _mixer_fwd(p, sh, l, v_first, host_rwkv=(), host_gdn=()):
    T = p.shape[0]
    lx, ly = p[:, 0:256], p[:, 256:512]
    prw, qkv, z, ab = p[:, 512:1920], p[:, 1920:3072], p[:, 3072:3456], p[:, 3456:3584]
    first = (lax.broadcasted_iota(jnp.int32, (T, LRU_W), 0) == 0).astype(F32)
    lru_rows = [lx, shift_rows(lx, 1), shift_rows(lx, 2), shift_rows(lx, 3), first]
    a, b = rowwise_fwd(lru_pre_fn, lru_rows, sh['lru_pre'], (LRU_W, LRU_W), f"lru_pre_fwd{l}")
    hseq = lru_scan(a, b, False, f"lru_scan_fwd{l}")
    (y_lru,) = rowwise_fwd(lru_post_fn, [hseq, ly], sh['lru_post'], (LRU_W,), f"lru_post_fwd{l}")

    rw_rows = [prw, shift_rows(prw, 1)] + ([v_first] if l > 0 else [])
    rw = rowwise_fwd(make_rwkv_pre_fn(l > 0), rw_rows, sh['rw_pre'], (MIX_W,) * 7, f"rwkv_pre_fwd{l}")
    r, lw, k2, v, kk, ar, g = rw
    y_raw, rs0, rt, got_rwkv = core_fwd(rwkv_heads, [r, lw, k2, v, kk, ar], f"rwkv_core_fwd{l}", host_rwkv)
    (y_rw,) = rowwise_fwd(rwkv_post_fn, [y_raw, r, k2, v, g], sh['rw_post'], (MIX_W,), f"rwkv_post_fwd{l}")

    gdn_rows = [qkv, shift_rows(qkv, 1), shift_rows(qkv, 2), shift_rows(qkv, 3), ab]
    gd = rowwise_fwd(gdn_pre_fn, gdn_rows, sh['gdn_pre'], (MIX_W,) * 5, f"gdn_pre_fwd{l}")
    o_raw, gs0, gt, got_gdn = core_fwd(gdn_heads, list(gd), f"gdn_core_fwd{l}", host_gdn)
    (y_gdn,) = rowwise_fwd(gdn_post_fn, [o_raw, z], sh['gdn_post'], (MIX_W,), f"gdn_post_fwd{l}")

    mixed = jnp.concatenate([y_lru, y_rw, y_gdn], axis=1)
    saved = dict(lru_rows=lru_rows, a=a, hseq=hseq, ly=ly, rw_rows=rw_rows, rw=rw, y_raw=y_raw, rs0=rs0, rt=rt,
                 gdn_rows=gdn_rows, gd=gd, o_raw=o_raw, gs0=gs0, gt=gt, z=z)
    v_layer0 = v if l == 0 else None
    return mixed, saved, v_layer0, got_rwkv, got_gdn


def _mixer_bwd(dmixed, sv, sh, l, dv_first):
    d_lru, d_rw, d_gdn = dmixed[:, 0:256], dmixed[:, 256:640], dmixed[:, 640:1024]
    gw = {}

    dh, dly, d_og = rowwise_bwd(lru_post_fn, [sv['hseq'], sv['ly']], sh['lru_post'], [d_lru], f"lru_post_bwd{l}")
    gscan = lru_scan(unshift_rows(sv['a'], 1), dh, True, f"lru_scan_bwd{l}")
    res = rowwise_bwd(lru_pre_fn, sv['lru_rows'], sh['lru_pre'], [gscan, shift_rows(sv['hseq'], 1)],
                      f"lru_pre_bwd{l}", ct_fn=lambda gs, hp: (gs * hp, gs))
    dlx = res[0] + unshift_rows(res[1], 1) + unshift_rows(res[2], 2) + unshift_rows(res[3], 3)
    dw0, dw1, dw2, dw3, dcb, dga, dgab, dgx, dgxb, dlam = res[5:]
    gw['lru_conv_w'] = jnp.concatenate([dw0, dw1, dw2, dw3], axis=0)
    gw['lru_conv_b'] = dcb[0]
    gw['lru_gate_a_w'] = _block_diag_grad(dga)
    gw['lru_gate_a_b'] = dgab.reshape(LRU_BLOCKS, 64)
    gw['lru_gate_x_w'] = _block_diag_grad(dgx)
    gw['lru_gate_x_b'] = dgxb.reshape(LRU_BLOCKS, 64)
    gw['lru_lambda'] = dlam[0]
    gw['lru_out_norm'] = d_og[0]

    r, lw, k2, v, kk, ar, g = sv['rw']
    res = rowwise_bwd(rwkv_post_fn, [sv['y_raw'], r, k2, v, g], sh['rw_post'], [d_rw], f"rwkv_post_bwd{l}")
    dy_raw, dr_p, dk2_p, dv_p, dg = res[:5]
    gw['rwkv_ln_g'], gw['rwkv_ln_b'], gw['rwkv_r_k'] = res[5][0], res[6][0], res[7].reshape(HEADS, HEAD_DIM)
    dr_c, dlw, dk2_c, dv_c, dkk, dar = core_bwd(rwkv_heads, [r, lw, k2, v, kk, ar], sv['rs0'], sv['rt'], dy_raw,
                                                 f"rwkv_core_bwd{l}")
    cts = [dr_p, dr_c, dlw, dk2_p, dk2_c, dv_p, dv_c, dkk, dar, dg]
    if l == 0:
        cts.append(dv_first)
        ct_fn = lambda a1, a2, b, c1, c2, d1, d2, e, f, gg, vf: (a1 + a2, b, c1 + c2, d1 + d2 + vf, e, f, gg)
    else:
        ct_fn = lambda a1, a2, b, c1, c2, d1, d2, e, f, gg: (a1 + a2, b, c1 + c2, d1 + d2, e, f, gg)
    res = rowwise_bwd(make_rwkv_pre_fn(l > 0), sv['rw_rows'], sh['rw_pre'], cts, f"rwkv_pre_bwd{l}", ct_fn=ct_fn)
    dprw = res[0] + unshift_rows(res[1], 1)
    nrow = len(sv['rw_rows'])
    dv_first_out = res[2] if l > 0 else None
    sg = res[nrow:]
    gw['rwkv_mu'], gw['rwkv_w_up'], gw['rwkv_w_bias'], gw['rwkv_a_up'] = sg[0][0], sg[1], sg[2][0], sg[3]
    gw['rwkv_a_bias'], gw['rwkv_g_up'], gw['rwkv_k_k'], gw['rwkv_k_a'] = sg[4][0], sg[5], sg[6][0], sg[7][0]
    if l > 0:
        gw['rwkv_vres_w1'], gw['rwkv_vres_w2'], gw['rwkv_vres_b'] = sg[8], sg[9], sg[10][0]

    do_raw, dz, d_ng = rowwise_bwd(gdn_post_fn, [sv['o_raw'], sv['z']], sh['gdn_post'], [d_gdn], f"gdn_post_bwd{l}")
    gw['gdn_norm'] = jnp.sum(d_ng.reshape(HEADS, HEAD_DIM), axis=0)
    dgd = core_bwd(gdn_heads, list(sv['gd']), sv['gs0'], sv['gt'], do_raw, f"gdn_core_bwd{l}")
    res = rowwise_bwd(gdn_pre_fn, sv['gdn_rows'], sh['gdn_pre'], list(dgd), f"gdn_pre_bwd{l}")
    dqkv = res[0] + unshift_rows(res[1], 1) + unshift_rows(res[2], 2) + unshift_rows(res[3], 3)
    dab = res[4]
    gw['gdn_conv_w'] = jnp.concatenate(res[5:9], axis=0)
    gw['gdn_a_log'], gw['gdn_dt_bias'] = res[9][0, :HEADS], res[10][0, :HEADS]

    dp = jnp.concatenate([dlx, dly, dprw, dqkv, dz, dab], axis=1)
    return dp, gw, dv_first_out


IN_SHARD = D_IN // N_CHIPS
IN_SHARD_PAD = D_IN_PAD // N_CHIPS


def _cols_to_chips(g, n=N_CHIPS):
    r = g.shape[0]
    return jnp.transpose(g.reshape(r, n, -1), (1, 0, 2))


def _cols_from_chips(g):
    return jnp.transpose(g, (1, 0, 2)).reshape(g.shape[1], -1)


def _w_in_from_chips(g):
    nat = _cols_from_chips(g[:, :, :IN_SHARD])
    return jnp.pad(nat, ((0, 0), (0, D_IN_PAD - D_IN)))


def _w_in_to_chips(g):
    return jnp.pad(_cols_to_chips(g[:, :D_IN]), ((0, 0), (0, 0), (0, IN_SHARD_PAD - IN_SHARD)))


def _natural(name, g):
    if name == 'w_in':
        return _w_in_from_chips(g)
    if BIG[name] == 2:
        return _cols_from_chips(g)
    return g.reshape(-1, g.shape[2])


def local_step(x, target, w, wb, shards=None):
    def hosted(keys):
        return [shards[k] for k in keys] if shards is not None else []

    def arrived(keys, gathered):
        for (name, layer), g in zip(keys if shards is not None else [], gathered):
            wb[name][layer] = _natural(name, g)

    saved = []
    v_first = None
    for l in range(N_LAYERS):
        sh = _layer_shared(w, l)
        more = l + 1 < N_LAYERS
        in_ffn1 = [('w_in', l)] + ([('w_out', l)] if more else [])
        in_rwkv = [('ffn2_wi', l)] + ([('ffn2_wo', l)] if more else [('w_out', l)])
        in_gdn = [('ffn1_wi', l + 1)] if more else [('ffn2_wo', l)]
        in_ffn2 = [('ffn1_wo', l + 1)] if more else []
        x1, got = ffn_fwd(x, _row(w['ffn1_norm'][l]), wb['ffn1_wi'][l], wb['ffn1_wo'][l], f"ffn1_fwd{l}",
                          hosted(in_ffn1))
        arrived(in_ffn1, got)
        p = proj_fwd(x1, _row(w['mix_norm'][l]), wb['w_in'][l], f"proj_fwd{l}")
        mixed, sv, v0, got_rwkv, got_gdn = _mixer_fwd(p, sh, l, v_first, hosted(in_rwkv), hosted(in_gdn))
        arrived(in_rwkv, got_rwkv)
        arrived(in_gdn, got_gdn)
        if l == 0:
            v_first = v0
        x2 = out_fwd(mixed, wb['w_out'][l], x1, f"out_fwd{l}")
        x3, got = ffn_fwd(x2, _row(w['ffn2_norm'][l]), wb['ffn2_wi'][l], wb['ffn2_wo'][l], f"ffn2_fwd{l}",
                          hosted(in_ffn2))
        arrived(in_ffn2, got)
        saved.append(dict(x0=x, x1=x1, x2=x2, mixed=mixed, sv=sv, sh=sh))
        x = x3

    loss, dx, dgf = loss_head(x, _row(w['final_norm']), target, "loss_head")
    per_layer = [dict() for _ in range(N_LAYERS)]
    dv_first = jnp.zeros((x.shape[0], MIX_W), F32)

    waiting, chip_sums, arrived_parts = [], {}, {}

    def reduce_now(keys, tag):
        if shards is None:
            return
        sums = chip_sums_of([(n, k, per_layer[k][n]) for n, k in keys], lax.axis_index("c"), tag)
        for key, (total, total_bf) in zip(keys, sums):
            chip_sums[key] = total
            waiting.append((key, total_bf))

    def take_waiting():
        keys, parts = [k for k, _ in waiting], [p for _, p in waiting]
        waiting.clear()
        return keys, parts

    for l in reversed(range(N_LAYERS)):
        s = saved[l]
        gw = per_layer[l]
        keys, parts = take_waiting()
        dx, dg2, dwg, dwu, dwo, got = ffn_bwd(s['x2'], dx, _row(w['ffn2_norm'][l]), wb['ffn2_wi'][l],
                                              wb['ffn2_wo'][l], f"ffn2_bwd{l}", parts)
        arrived_parts.update(zip(keys, got))
        wi_parts = lambda dwg, dwu: (dwg, dwu)
        row_parts = lambda dw: dw.reshape(N_CHIPS, -1, dw.shape[1])
        gw['ffn2_norm'], gw['ffn2_wi'], gw['ffn2_wo'] = dg2[0], wi_parts(dwg, dwu), row_parts(dwo)
        if l == N_LAYERS - 1:
            reduce_now([('ffn2_wi', l), ('ffn2_wo', l)], f"ffn2_{l}")
        dmixed, dw_out = out_bwd(s['mixed'], wb['w_out'][l], dx, f"out_bwd{l}")
        gw['w_out'] = row_parts(dw_out)
        dp, gmix, dvf = _mixer_bwd(dmixed, s['sv'], s['sh'], l, dv_first)
        if l > 0:
            dv_first = dvf
        gw.update(gmix)
        dx, dgm, dwin = proj_bwd(s['x1'], dx, _row(w['mix_norm'][l]), wb['w_in'][l], dp, f"proj_bwd{l}")
        gw['mix_norm'], gw['w_in'] = dgm[0], _w_in_to_chips(dwin)
        if l < N_LAYERS - 1:
            reduce_now([('ffn2_wi', l), ('ffn2_wo', l), ('w_in', l), ('w_out', l)], f"mix_{l}")
        keys, parts = take_waiting()
        dx, dg1, dwg, dwu, dwo, got = ffn_bwd(s['x0'], dx, _row(w['ffn1_norm'][l]), wb['ffn1_wi'][l],
                                              wb['ffn1_wo'][l], f"ffn1_bwd{l}", parts)
        arrived_parts.update(zip(keys, got))
        gw['ffn1_norm'], gw['ffn1_wi'], gw['ffn1_wo'] = dg1[0], wi_parts(dwg, dwu), row_parts(dwo)
        if l == N_LAYERS - 1:
            reduce_now([('w_in', l), ('w_out', l), ('ffn1_wi', l), ('ffn1_wo', l)], f"ffn1_{l}")
        else:
            reduce_now([('ffn1_wi', l), ('ffn1_wo', l)], f"ffn1_{l}")
    grads = {'final_norm': dgf[0]}
    if shards is not None:
        grads['last_round'] = take_waiting()
        arrived_parts.update({key: None for key in grads['last_round'][0]})
    for name in WEIGHTS:
        if name == 'final_norm':
            continue
        if name in BIG:
            if shards is None:
                grads[name] = [per_layer[l][name] for l in range(N_LAYERS)]
            else:
                grads[name] = [(chip_sums[(name, l)], arrived_parts[(name, l)]) for l in range(N_LAYERS)]
        elif name.startswith('rwkv_vres'):
            grads[name] = per_layer[1][name][None]
        else:
            grads[name] = jnp.stack([per_layer[l][name] for l in range(N_LAYERS)])
    return loss[0, 0], dx, grads


ANY = pl.BlockSpec(memory_space=pl.ANY)


def _coords():
    return lax.axis_index("x"), lax.axis_index("y"), lax.axis_index("c")


def _other_chips(x, y):
    return [((x + 1) % 2, y), (x, (y + 1) % 2), ((x + 1) % 2, (y + 1) % 2)]


def allreduce_small(pack, name):
    R = pack.shape[0]
    rh = R // 2

    def body(x_ref, o_ref, sib_ref, chip_ref, parts_ref, send_sems, recv_sems):
        x, y, c = _coords()
        sib = (x, y, 1 - c)

        def copy(k, src, dst, to):
            return pltpu.make_async_remote_copy(src_ref=src, dst_ref=dst, send_sem=send_sems.at[k],
                                                recv_sem=recv_sems.at[k], device_id=to, device_id_type=MESH)

        swap = copy(0, x_ref, sib_ref, sib)
        swap.start()
        swap.wait()
        chip_ref[...] = jnp.where(c == 0, x_ref[...], sib_ref[...]) + jnp.where(c == 0, sib_ref[...], x_ref[...])

        mine = pl.ds(pl.multiple_of(c * rh, 8), rh)
        sends = [copy(1 + j, chip_ref.at[mine], parts_ref.at[j], (px, py, c))
                 for j, (px, py) in enumerate(_other_chips(x, y))]
        for cp in sends:
            cp.start()
        for cp in sends:
            cp.wait()
        s = 2 * x + y
        own = chip_ref[mine, :]
        from_chip = {2: parts_ref[0], 1: parts_ref[1], 3: parts_ref[2]}
        terms = []
        for k in range(N_CHIPS):
            t = own
            for d, part in from_chip.items():
                t = jnp.where(jnp.bitwise_xor(s, d) == k, part, t)
            terms.append(t)
        o_ref[mine, :] = ((terms[0] + terms[1]) + terms[2]) + terms[3]

        share = copy(4, o_ref.at[mine], o_ref.at[mine], sib)
        share.start()
        share.wait()

    vm = pl.BlockSpec(memory_space=pltpu.VMEM)
    return pl.pallas_call(
        body, name=name, in_specs=[vm], out_specs=vm, out_shape=jax.ShapeDtypeStruct((R, 128), F32),
        scratch_shapes=[pltpu.VMEM((R, 128), F32), pltpu.VMEM((R, 128), F32), pltpu.VMEM((3, rh, 128), F32),
                        pltpu.SemaphoreType.DMA((5,)), pltpu.SemaphoreType.DMA((5,))],
        compiler_params=_params())(pack)


class ChipGather:
    def __init__(self, shards):
        self.shapes = [s.shape for s in shards]
        self.n = len(shards)
        self.in_specs = [ANY] * self.n
        self.out_specs = [ANY] * self.n
        self.out_shape = [jax.ShapeDtypeStruct((N_CHIPS,) + s.shape, s.dtype) for s in shards]
        self.scratch = [pltpu.SemaphoreType.DMA((6 * self.n,)), pltpu.SemaphoreType.DMA((6 * self.n,)),
                        pltpu.SemaphoreType.DMA((self.n,))] if self.n else []

    def _rows(self, a, core):
        rh = self.shapes[a][0] // 2
        return pl.ds(pl.multiple_of(core * rh, 16), rh)

    def _copies(self, kind, x_refs, o_refs, sems):
        send_sems, recv_sems, local_sems = sems
        x, y, c = _coords()
        s_me = 2 * x + y
        sib = (x, y, 1 - c)

        def copy(a, k, src, dst, to):
            return pltpu.make_async_remote_copy(src_ref=src, dst_ref=dst, send_sem=send_sems.at[6 * a + k],
                                                recv_sem=recv_sems.at[6 * a + k], device_id=to, device_id_type=MESH)

        if kind == 'own':
            return [pltpu.make_async_copy(x_refs[a], o_refs[a].at[s_me], local_sems.at[a]) for a in range(self.n)]
        out = []
        for j, (px, py) in enumerate(_other_chips(x, y)):
            for a in range(self.n):
                mine = self._rows(a, c)
                part = o_refs[a].at[2 * px + py, mine]
                if kind == 'sent':
                    out.append(copy(a, j, x_refs[a].at[mine], o_refs[a].at[s_me, mine], (px, py, c)))
                elif kind == 'arrived':
                    out.append(copy(a, j, part, part, (px, py, c)))
                elif kind == 'passed':
                    out.append(copy(a, 3 + j, part, part, sib))
                else:
                    theirs = o_refs[a].at[2 * px + py, self._rows(a, 1 - c)]
                    out.append(copy(a, 3 + j, theirs, theirs, sib))
        return out

    def start(self, x_refs, o_refs, sems):
        if not self.n:
            return
        for cp in self._copies('own', x_refs, o_refs, sems) + self._copies('sent', x_refs, o_refs, sems):
            cp.start()

    def relay(self, x_refs, o_refs, sems):
        if not self.n:
            return
        for got, fw in zip(self._copies('arrived', x_refs, o_refs, sems),
                           self._copies('passed', x_refs, o_refs, sems)):
            got.wait_recv()
            fw.start()

    def finish(self, x_refs, o_refs, sems):
        if not self.n:
            return
        for cp in self._copies('from_sibling', x_refs, o_refs, sems):
            cp.wait_recv()
        for cp in self._copies('sent', x_refs, o_refs, sems) + self._copies('passed', x_refs, o_refs, sems):
            cp.wait_send()
        for cp in self._copies('own', x_refs, o_refs, sems):
            cp.wait()


def allgather_chips(shards, name):
    gather = ChipGather(shards)
    n = gather.n

    def body(*refs):
        x_refs, o_refs, sems = refs[:n], refs[n:2 * n], refs[2 * n:]
        gather.start(x_refs, o_refs, sems)
        gather.relay(x_refs, o_refs, sems)
        gather.finish(x_refs, o_refs, sems)

    return pl.pallas_call(
        body, name=name, in_specs=gather.in_specs, out_specs=tuple(gather.out_specs),
        out_shape=tuple(gather.out_shape), scratch_shapes=gather.scratch, compiler_params=_params())(*shards)


def sibling_swap(srcs, halves, name):
    n = len(srcs)
    row_axis = [s.ndim - 2 for s in srcs]
    out_shapes = [s.shape[:ax] + (s.shape[ax] // 2,) + s.shape[ax + 1:] if halves else s.shape
                  for s, ax in zip(srcs, row_axis)]

    def body(*refs):
        x_refs, o_refs = refs[:n], refs[n:2 * n]
        send_sems, recv_sems = refs[2 * n:]
        x, y, c = _coords()
        copies = []
        for a in range(n):
            part = x_refs[a]
            if halves:
                rh = srcs[a].shape[row_axis[a]] // 2
                theirs = pl.ds(pl.multiple_of((1 - c) * rh, 16), rh)
                part = part.at[:, theirs] if row_axis[a] == 1 else part.at[theirs]
            cp = pltpu.make_async_remote_copy(src_ref=part, dst_ref=o_refs[a], send_sem=send_sems.at[a],
                                              recv_sem=recv_sems.at[a], device_id=(x, y, 1 - c), device_id_type=MESH)
            cp.start()
            copies.append(cp)
        for cp in copies:
            cp.wait()

    return pl.pallas_call(
        body, name=name, in_specs=[ANY] * n, out_specs=tuple([ANY] * n),
        out_shape=tuple(jax.ShapeDtypeStruct(sh, s.dtype) for sh, s in zip(out_shapes, srcs)),
        scratch_shapes=[pltpu.SemaphoreType.DMA((n,)), pltpu.SemaphoreType.DMA((n,))],
        compiler_params=_params())(*srcs)


class ChipScatter:
    def __init__(self, parts):
        self.n = len(parts)
        self.in_specs = [ANY] * self.n
        self.out_specs = [ANY] * self.n
        self.out_shape = [jax.ShapeDtypeStruct((3,) + p.shape[1:], p.dtype) for p in parts]
        self.scratch = [pltpu.SemaphoreType.DMA((3 * self.n,)), pltpu.SemaphoreType.DMA((3 * self.n,))] if self.n else []

    def _copies(self, x_refs, o_refs, sems):
        send_sems, recv_sems = sems
        x, y, c = _coords()
        return [pltpu.make_async_remote_copy(src_ref=x_refs[a].at[2 * px + py], dst_ref=o_refs[a].at[j],
                                             send_sem=send_sems.at[3 * a + j], recv_sem=recv_sems.at[3 * a + j],
                                             device_id=(px, py, c), device_id_type=MESH)
                for j, (px, py) in enumerate(_other_chips(x, y)) for a in range(self.n)]

    def start(self, x_refs, o_refs, sems):
        if self.n:
            for cp in self._copies(x_refs, o_refs, sems):
                cp.start()

    def finish(self, x_refs, o_refs, sems):
        if self.n:
            for cp in self._copies(x_refs, o_refs, sems):
                cp.wait()


HBM = pl.BlockSpec(memory_space=pltpu.HBM)
SEM = pl.BlockSpec(memory_space=pltpu.SEMAPHORE)
SIDE_EFFECT = pltpu.SideEffectType.DATAFLOW_SIDE_EFFECTING


def _scatter_copies(x_refs, land_refs, send_sems, recv_sems):
    x, y, c = _coords()
    n = len(x_refs)
    return [pltpu.make_async_remote_copy(src_ref=x_refs[a].at[2 * px + py], dst_ref=land_refs[a].at[j],
                                         send_sem=send_sems[3 * a + j], recv_sem=recv_sems[3 * a + j],
                                         device_id=(px, py, c), device_id_type=MESH)
            for j, (px, py) in enumerate(_other_chips(x, y)) for a in range(n)]


def scatter_start(parts, name):
    n = len(parts)
    k = 3 * n
    lands = [lax.empty((3,) + p.shape[1:], p.dtype) for p in parts]

    def body(*refs):
        x_refs, land_refs = refs[:n], refs[n:2 * n]
        send_sems, recv_sems = refs[2 * n:2 * n + k], refs[2 * n + k:2 * n + 2 * k]
        token = refs[-1]
        for cp in _scatter_copies(x_refs, land_refs, send_sems, recv_sems):
            cp.start()
        token[...] = jnp.zeros_like(token)

    hbm = lambda a: pltpu.HBM(a.shape, a.dtype)
    res = pl.pallas_call(
        body, name=name, in_specs=[HBM] * (2 * n),
        out_specs=tuple([SEM] * (2 * k) + [HBM] * (2 * n) + [pl.BlockSpec(memory_space=pltpu.VMEM)]),
        out_shape=tuple([pltpu.SemaphoreType.DMA(())] * (2 * k) + [hbm(p) for p in parts] + [hbm(b) for b in lands]
                        + [jax.ShapeDtypeStruct((8, 128), F32)]),
        input_output_aliases={i: 2 * k + i for i in range(2 * n)},
        compiler_params=pltpu.CompilerParams(has_side_effects=SIDE_EFFECT, vmem_limit_bytes=VMEM_LIMIT))(
            *[pltpu.with_memory_space_constraint(a, pltpu.HBM) for a in list(parts) + lands])
    return list(res[:2 * k]), list(res[2 * k:2 * k + n]), list(res[2 * k + n:2 * k + 2 * n]), res[-1]


def scatter_wait(sems, parts_thru, lands_thru, after, name):
    n = len(parts_thru)
    k = 3 * n

    def body(*refs):
        x_refs, land_refs = refs[:n], refs[n:2 * n]
        send_sems, recv_sems = refs[2 * n:2 * n + k], refs[2 * n + k:2 * n + 2 * k]
        for cp in _scatter_copies(x_refs, land_refs, send_sems, recv_sems):
            cp.wait_send()
            cp.wait_recv()

    hbm = lambda a: pltpu.HBM(a.shape, a.dtype)
    res = pl.pallas_call(
        body, name=name, in_specs=[HBM] * (2 * n) + [SEM] * (2 * k) + [ANY],
        out_specs=tuple([HBM] * (2 * n)), out_shape=tuple(hbm(a) for a in list(parts_thru) + list(lands_thru)),
        input_output_aliases={i: i for i in range(2 * n)},
        compiler_params=pltpu.CompilerParams(has_side_effects=SIDE_EFFECT, vmem_limit_bytes=VMEM_LIMIT))(
            *parts_thru, *lands_thru, *sems, after)
    return list(res[n:])


def _row_block(rows):
    return max(b for b in range(16, 257, 16) if rows % b == 0)


def chip_sum(gpack, recv, core, name):
    n, R, W = gpack.shape
    rh = R // 2
    rb = _row_block(rh)
    nb = rh // rb

    def body(c_ref, g_ref, r_ref, o_ref, ob_ref):
        s = g_ref[...] + r_ref[...]
        o_ref[...] = s
        ob_ref[...] = s.astype(BF16)

    blk = pl.BlockSpec((1, rb, W), lambda i, j, c_ref: (i, j, 0))
    spec = pltpu.PrefetchScalarGridSpec(
        num_scalar_prefetch=1, grid=(n, nb),
        in_specs=[pl.BlockSpec((1, rb, W), lambda i, j, c_ref: (i, c_ref[0] * nb + j, 0)), blk],
        out_specs=(blk, blk))
    return pl.pallas_call(
        body, name=name, grid_spec=spec,
        out_shape=(jax.ShapeDtypeStruct((n, rh, W), F32), jax.ShapeDtypeStruct((n, rh, W), BF16)),
        compiler_params=_params(("arbitrary", "arbitrary")))(core, gpack, recv)


def chip_sum_cols(gate, up, recv_gate, recv_up, core, name):
    R, W = gate.shape
    cw = W // 2
    rh = R // 2
    rb = _row_block(rh)
    nb = rh // rb

    def body(c_ref, g_ref, u_ref, rg_ref, ru_ref, o_ref, ob_ref):
        s = jnp.where(pl.program_id(0) < 2, g_ref[...] + rg_ref[...], u_ref[...] + ru_ref[...])
        o_ref[0] = s
        ob_ref[0] = s.astype(BF16)

    gate_blk = lambda s, j: (jnp.where(s < 2, j, nb - 1), jnp.minimum(s, 1))
    up_blk = lambda s, j: (jnp.where(s < 2, 0, j), jnp.maximum(s - 2, 0))
    out = pl.BlockSpec((1, rb, cw), lambda s, j, c_ref: (s, j, 0))

    def own(blk):
        return lambda s, j, c_ref: (c_ref[0] * nb + blk(s, j)[0], blk(s, j)[1])

    def theirs(blk):
        return lambda s, j, c_ref: blk(s, j)

    spec = pltpu.PrefetchScalarGridSpec(
        num_scalar_prefetch=1, grid=(N_CHIPS, nb),
        in_specs=[pl.BlockSpec((rb, cw), own(gate_blk)), pl.BlockSpec((rb, cw), own(up_blk)),
                  pl.BlockSpec((rb, cw), theirs(gate_blk)), pl.BlockSpec((rb, cw), theirs(up_blk))],
        out_specs=(out, out))
    return pl.pallas_call(
        body, name=name, grid_spec=spec,
        out_shape=(jax.ShapeDtypeStruct((N_CHIPS, rh, cw), F32), jax.ShapeDtypeStruct((N_CHIPS, rh, cw), BF16)),
        compiler_params=_params(("arbitrary", "arbitrary")))(core, gate, up, recv_gate, recv_up)


def chip_sums_of(items, core, tag):
    parts = []
    for _, _, g in items:
        parts += list(g) if isinstance(g, tuple) else [g]
    swapped = iter(zip(parts, sibling_swap(parts, True, f"grad_swap_cores_{tag}")))
    core_arg = core.reshape(1).astype(jnp.int32)
    sums = []
    for n, l, g in items:
        if isinstance(g, tuple):
            (dwg, from_g), (dwu, from_u) = next(swapped), next(swapped)
            sums.append(chip_sum_cols(dwg, dwu, from_g, from_u, core_arg, f"grad_chip_sum_{n}{l}"))
        else:
            p, r = next(swapped)
            sums.append(chip_sum(p, r, core_arg, f"grad_chip_sum_{n}{l}"))
    return sums


def shard_sum(own, recv, name):
    R, W = own.shape
    rb = _row_block(R)

    def body(a_ref, r_ref, o_ref):
        acc = a_ref[...]
        for j in range(3):
            acc = acc + r_ref[j].astype(F32)
        o_ref[...] = acc

    return pl.pallas_call(
        body, name=name, grid=(R // rb,),
        in_specs=[pl.BlockSpec((rb, W), lambda i: (i, 0)), pl.BlockSpec((3, rb, W), lambda i: (0, i, 0))],
        out_specs=pl.BlockSpec((rb, W), lambda i: (i, 0)), out_shape=jax.ShapeDtypeStruct((R, W), F32),
        compiler_params=_params(("arbitrary",)))(own, recv)


def adamw(w, m, v, g, name):
    L, R, C = w.shape
    rb = max(b for b in range(8, 257, 8) if R % b == 0)
    bc1 = 1.0 - ADAM_B1 ** ADAM_STEP
    bc2 = 1.0 - ADAM_B2 ** ADAM_STEP

    def body(w_ref, m_ref, v_ref, g_ref, d_ref, nm_ref, nv_ref):
        gv = g_ref[...]
        nm = ADAM_B1 * m_ref[...] + (1.0 - ADAM_B1) * gv
        nv = ADAM_B2 * v_ref[...] + (1.0 - ADAM_B2) * (gv * gv)
        d_ref[...] = -ADAM_LR * ((nm / bc1) / (jnp.sqrt(nv / bc2) + ADAM_EPS) + ADAM_WD * w_ref[...])
        nm_ref[...] = nm
        nv_ref[...] = nv

    blk = pl.BlockSpec((1, rb, C), lambda l, i: (l, i, 0))
    sh = jax.ShapeDtypeStruct((L, R, C), F32)
    return pl.pallas_call(body, name=name, grid=(L, R // rb), in_specs=[blk] * 4, out_specs=(blk,) * 3,
                          out_shape=(sh, sh, sh), compiler_params=_params(("arbitrary", "arbitrary")))(w, m, v, g)


SMALL = [n for n in WEIGHTS if n not in BIG]


PACK_TILE = 8 * 128


def _pack(arrays):
    blocks = []
    for a in arrays:
        flat = a.reshape(-1)
        flat = jnp.pad(flat, (0, -flat.size % PACK_TILE))
        blocks.append(flat.reshape(-1, 128))
    rows = sum(b.shape[0] for b in blocks)
    if rows % 16:
        blocks.append(jnp.zeros((8, 128), arrays[0].dtype))
    return jnp.concatenate(blocks, axis=0)


def _unpack(pack, shapes):
    out, row = [], 0
    for shape in shapes:
        size = int(np.prod(shape))
        rows = -(-size // PACK_TILE) * 8
        out.append(pack[row:row + rows].reshape(-1)[:size].reshape(shape))
        row += rows
    return out


def _pad_lanes(a):
    return jnp.pad(a, ((0, 0), (0, -a.shape[1] % 128)))


def _local_shard(full, axis, chip):
    size = full.shape[axis] // N_CHIPS
    return lax.dynamic_slice_in_dim(full, chip * size, size, axis)


def kernel(x, ffn1_norm, ffn1_wi, ffn1_wo, mix_norm, w_in, w_out, lru_conv_w, lru_conv_b, lru_gate_a_w, lru_gate_a_b, lru_gate_x_w, lru_gate_x_b, lru_lambda, lru_out_norm, rwkv_mu, rwkv_w_up, rwkv_w_bias, rwkv_a_up, rwkv_a_bias, rwkv_g_up, rwkv_k_k, rwkv_k_a, rwkv_r_k, rwkv_ln_g, rwkv_ln_b, rwkv_vres_w1, rwkv_vres_w2, rwkv_vres_b, gdn_conv_w, gdn_a_log, gdn_dt_bias, gdn_norm, ffn2_norm, ffn2_wi, ffn2_wo, final_norm, loss_target, m_ffn1_norm, m_ffn1_wi, m_ffn1_wo, m_mix_norm, m_w_in, m_w_out, m_lru_conv_w, m_lru_conv_b, m_lru_gate_a_w, m_lru_gate_a_b, m_lru_gate_x_w, m_lru_gate_x_b, m_lru_lambda, m_lru_out_norm, m_rwkv_mu, m_rwkv_w_up, m_rwkv_w_bias, m_rwkv_a_up, m_rwkv_a_bias, m_rwkv_g_up, m_rwkv_k_k, m_rwkv_k_a, m_rwkv_r_k, m_rwkv_ln_g, m_rwkv_ln_b, m_rwkv_vres_w1, m_rwkv_vres_w2, m_rwkv_vres_b, m_gdn_conv_w, m_gdn_a_log, m_gdn_dt_bias, m_gdn_norm, m_ffn2_norm, m_ffn2_wi, m_ffn2_wo, m_final_norm, v_ffn1_norm, v_ffn1_wi, v_ffn1_wo, v_mix_norm, v_w_in, v_w_out, v_lru_conv_w, v_lru_conv_b, v_lru_gate_a_w, v_lru_gate_a_b, v_lru_gate_x_w, v_lru_gate_x_b, v_lru_lambda, v_lru_out_norm, v_rwkv_mu, v_rwkv_w_up, v_rwkv_w_bias, v_rwkv_a_up, v_rwkv_a_bias, v_rwkv_g_up, v_rwkv_k_k, v_rwkv_k_a, v_rwkv_r_k, v_rwkv_ln_g, v_rwkv_ln_b, v_rwkv_vres_w1, v_rwkv_vres_w2, v_rwkv_vres_b, v_gdn_conv_w, v_gdn_a_log, v_gdn_dt_bias, v_gdn_norm, v_ffn2_norm, v_ffn2_wi, v_ffn2_wo, v_final_norm):
    args = locals()
    w_loc = {n: args[n] for n in WEIGHTS}
    m_loc = {n: args['m_' + n] for n in WEIGHTS}
    v_loc = {n: args['v_' + n] for n in WEIGHTS}
    chip = 2 * lax.axis_index("x") + lax.axis_index("y")
    core = lax.axis_index("c")

    big = [(n, l) for n in BIG for l in range(N_LAYERS)]
    shards = {(n, l): _pad_lanes(w_loc[n][l].astype(BF16)) for n, l in big}
    first = [('ffn1_wi', 0), ('ffn1_wo', 0)]
    wb = {n: [None] * N_LAYERS for n in BIG}
    for (n, l), g in zip(first, allgather_chips([shards[k] for k in first], "allgather_first")):
        wb[n][l] = _natural(n, g)

    sm_names = list(SMALL_SHARDED)
    placed = []
    for n in sm_names:
        mine = [jnp.where((chip == s) & (core == 0), w_loc[n], 0.0) for s in range(N_CHIPS)]
        placed.append(jnp.concatenate(mine, axis=SMALL_SHARDED[n]))
    summed = allreduce_small(_pack(placed), "allgather_small")
    w_full = dict(w_loc)
    w_full.update(zip(sm_names, _unpack(summed, [p.shape for p in placed])))

    loss, dx, grads = local_step(x[0], loss_target[0], w_full, wb, shards)
    loss = lax.psum(loss, ("x", "y", "c"))

    gsum = allreduce_small(_pack([grads[n] for n in SMALL]), "allreduce_small")
    g_loc = {}
    for n, g in zip(SMALL, _unpack(gsum, [grads[n].shape for n in SMALL])):
        g_loc[n] = _local_shard(g, SMALL_SHARDED[n], chip) if n in SMALL_SHARDED else g

    last_keys, last_parts = grads['last_round']
    sems, parts_thru, lands_thru, token = scatter_start(last_parts, "grad_scatter_last_start")
    chip_after_start = chip + token[0, 0].astype(chip.dtype)
    rows = {n: [None] * N_LAYERS for n in BIG}
    delta, new_m, new_v = {}, {}, {}

    def finish(keys, arrived, which_chip, tag):
        halves = [shard_sum(lax.dynamic_index_in_dim(grads[n][l][0], which_chip, 0, keepdims=False), got,
                            f"grad_shard_sum_{n}{l}") for (n, l), got in zip(keys, arrived)]
        others = sibling_swap(halves, False, f"grad_share_cores_{tag}")
        for (n, l), half, other in zip(keys, halves, others):
            lower = jnp.where(core == 0, half, other)
            upper = jnp.where(core == 0, other, half)
            rows[n][l] = jnp.concatenate([lower, upper], axis=0)[:, :w_loc[n].shape[-1]]
        for n in BIG:
            if n not in delta and all(r is not None for r in rows[n]):
                g_loc[n] = jnp.stack(rows[n])
                delta[n], new_m[n], new_v[n] = adamw(w_loc[n], m_loc[n], v_loc[n], g_loc[n], f"adamw_{n}")

    early = [k for k in big if k not in last_keys]
    finish(early, [grads[n][l][1] for n, l in early], chip_after_start, "early")
    pack = lambda d: _pack([d[n] for n in SMALL])[None]
    res = adamw(pack(w_loc), pack(m_loc), pack(v_loc), pack(g_loc), "adamw_small")
    for dst, r in zip((delta, new_m, new_v), res):
        dst.update(zip(SMALL, _unpack(r[0], [w_loc[n].shape for n in SMALL])))
    arrived_last = scatter_wait(sems, parts_thru, lands_thru, delta['ffn2_wi'], "grad_scatter_last_wait")
    finish(last_keys, arrived_last, chip, "last")

    return (loss, dx[None], *[g_loc[n] for n in WEIGHTS], *[delta[n] for n in WEIGHTS],
            *[new_m[n] for n in WEIGHTS], *[new_v[n] for n in WEIGHTS])
```

```python
import functools

import numpy as np
import jax
import jax.numpy as jnp
from jax import lax
from jax.experimental import pallas as pl
from jax.experimental.pallas import tpu as pltpu

F32 = jnp.float32
BF16 = jnp.bfloat16
MESH = pl.DeviceIdType.MESH

D_MODEL = 1024
D_FF = 2816
N_LAYERS = 2
HEADS = 6
HEAD_DIM = 64
MIX_W = HEADS * HEAD_DIM
LRU_W = 256
LRU_BLOCKS = 4
D_IN = 3468
D_IN_PAD = 3584
NORM_EPS = 1e-6
GN_EPS = 64e-5
LRU_C = 8.0
CHUNK = 64
CHUNKS_PER_STEP = 4
ROWS = 512
FF_CHUNK = 256
IN_CHUNK = 512
VMEM_LIMIT = 56 * 1024 * 1024

ADAM_LR, ADAM_B1, ADAM_B2, ADAM_EPS, ADAM_WD, ADAM_STEP = 0.001, 0.9, 0.999, 1e-08, 0.01, 10

WEIGHTS = ['ffn1_norm', 'ffn1_wi', 'ffn1_wo', 'mix_norm', 'w_in', 'w_out', 'lru_conv_w', 'lru_conv_b',
           'lru_gate_a_w', 'lru_gate_a_b', 'lru_gate_x_w', 'lru_gate_x_b', 'lru_lambda', 'lru_out_norm',
           'rwkv_mu', 'rwkv_w_up', 'rwkv_w_bias', 'rwkv_a_up', 'rwkv_a_bias', 'rwkv_g_up', 'rwkv_k_k',
           'rwkv_k_a', 'rwkv_r_k', 'rwkv_ln_g', 'rwkv_ln_b', 'rwkv_vres_w1', 'rwkv_vres_w2', 'rwkv_vres_b',
           'gdn_conv_w', 'gdn_a_log', 'gdn_dt_bias', 'gdn_norm', 'ffn2_norm', 'ffn2_wi', 'ffn2_wo', 'final_norm']
BIG = {'ffn1_wi': 2, 'ffn1_wo': 1, 'w_in': 2, 'w_out': 1, 'ffn2_wi': 2, 'ffn2_wo': 1}
SMALL_SHARDED = {'lru_conv_w': 2, 'rwkv_w_up': 2, 'rwkv_a_up': 2, 'rwkv_g_up': 2, 'rwkv_vres_w1': 1,
                 'rwkv_vres_w2': 2, 'gdn_conv_w': 2}
N_CHIPS = 4


def _params(sem=None):
    kw = dict(vmem_limit_bytes=VMEM_LIMIT)
    if sem is not None:
        kw['dimension_semantics'] = sem
    return pltpu.CompilerParams(**kw)


def _bdot(a, b, dims=(((1,), (0,)), ((), ()))):
    return lax.dot_general(a.astype(BF16), b.astype(BF16), dims, preferred_element_type=F32)


def _bdot_nt(a, b):
    return _bdot(a, b, (((1,), (1,)), ((), ())))


def _bdot_tn(a, b):
    return _bdot(a, b, (((0,), (0,)), ((), ())))


_DIMS = {'nn': (((1,), (0,)), ((), ())), 'nt': (((1,), (1,)), ((), ())), 'tn': (((0,), (0,)), ((), ()))}


def _split(a, terms):
    parts = []
    for _ in range(terms - 1):
        hi = a.astype(BF16)
        parts.append(hi)
        a = a - hi.astype(F32)
    parts.append(a.astype(BF16))
    return parts


_BATCH_DIMS = {'nn': (((2,), (1,)), ((0,), (0,))), 'nt': (((2,), (2,)), ((0,), (0,))),
               'tn': (((1,), (1,)), ((0,), (0,)))}


def _dot3(a, b, kind):
    ah, al = _split(a, 2)
    bh, bl = _split(b, 2)
    dims = _BATCH_DIMS[kind] if a.ndim == 3 else _DIMS[kind]
    d = lambda p, q: lax.dot_general(p, q, dims, preferred_element_type=F32)
    return d(ah, bh) + (d(ah, bl) + d(al, bh))


@functools.partial(jax.custom_vjp, nondiff_argnums=(2,))
def _cdot_k(a, b, kind):
    return _dot3(a, b, kind)


def _cdot_k_fwd(a, b, kind):
    return _dot3(a, b, kind), (a, b)


def _cdot_k_bwd(kind, res, ct):
    a, b = res
    if kind == 'nn':
        return _dot3(ct, b, 'nt'), _dot3(a, ct, 'tn')
    if kind == 'nt':
        return _dot3(ct, b, 'nn'), _dot3(ct, a, 'tn')
    return _dot3(b, ct, 'nt'), _dot3(a, ct, 'nn')


_cdot_k.defvjp(_cdot_k_fwd, _cdot_k_bwd)


def _dot1(a, b, kind):
    dims = _BATCH_DIMS[kind] if a.ndim == 3 else _DIMS[kind]
    return lax.dot_general(a.astype(BF16), b.astype(BF16), dims, preferred_element_type=F32)


@functools.partial(jax.custom_vjp, nondiff_argnums=(2,))
def _cdot1_k(a, b, kind):
    return _dot1(a, b, kind)


def _cdot1_k_fwd(a, b, kind):
    return _dot1(a, b, kind), (a, b)


def _cdot1_k_bwd(kind, res, ct):
    a, b = res
    if kind == 'nn':
        return _dot1(ct, b, 'nt'), _dot1(a, ct, 'tn')
    if kind == 'nt':
        return _dot1(ct, b, 'nn'), _dot1(ct, a, 'tn')
    return _dot1(b, ct, 'nt'), _dot1(a, ct, 'nn')


_cdot1_k.defvjp(_cdot1_k_fwd, _cdot1_k_bwd)


def _cdot(a, b):
    return _cdot1_k(a, b, 'nn')


def _cdot_nt(a, b):
    return _cdot1_k(a, b, 'nt')


def _cdot_tn(a, b):
    return _cdot1_k(a, b, 'tn')


def _hdot(a, b):
    return _cdot_k(a, b, 'nn')


def _dot_exact(x, m01, kind):
    d = lambda p: lax.dot_general(p, m01.astype(BF16), _DIMS[kind], preferred_element_type=F32)
    hi, mid, lo = _split(x, 3)
    return d(hi) + (d(mid) + d(lo))


@functools.partial(jax.custom_vjp, nondiff_argnums=(1,))
def _xdot(x, make_m):
    return _dot_exact(x, make_m(), 'nn')


def _xdot_fwd(x, make_m):
    return _dot_exact(x, make_m(), 'nn'), None


def _xdot_bwd(make_m, _, ct):
    return (_dot_exact(ct, make_m(), 'nt'),)


_xdot.defvjp(_xdot_fwd, _xdot_bwd)


def _iota2(n, m):
    return lax.broadcasted_iota(jnp.int32, (n, m), 0), lax.broadcasted_iota(jnp.int32, (n, m), 1)


def _head_blocks(w):
    ri, ci = _iota2(w, w)
    return (ri // HEAD_DIM == ci // HEAD_DIM).astype(F32)


def _segsum(x):
    return _xdot(x, functools.partial(_head_blocks, x.shape[-1]))


def _cumsum_rows(x):
    return _cumsum_k(x, x.shape[0])


@functools.partial(jax.custom_vjp, nondiff_argnums=(1,))
def _cumsum_k(x, n):
    return _lower_dot(x, n, False)


def _lower_dot(x, n, transpose):
    ri, ci = _iota2(n, n)
    m = ((ri <= ci) if transpose else (ri >= ci)).astype(BF16)
    d = lambda p: lax.dot_general(m, p, _DIMS['nn'], preferred_element_type=F32)
    hi, mid, lo = _split(x, 3)
    return d(hi) + (d(mid) + d(lo))


def _cumsum_k_fwd(x, n):
    return _lower_dot(x, n, False), None


def _cumsum_k_bwd(n, _, ct):
    return (_lower_dot(ct, n, True),)


_cumsum_k.defvjp(_cumsum_k_fwd, _cumsum_k_bwd)


def _rms(x, g):
    return x * lax.rsqrt(jnp.mean(x * x, axis=-1, keepdims=True) + NORM_EPS) * g


DENSE_ROWS = 1024


def _row_loop(n_rows, fn):
    rows = min(DENSE_ROWS, n_rows)

    def step(i, c):
        fn(pl.ds(pl.multiple_of(i * rows, rows), rows))
        return c
    lax.fori_loop(0, n_rows // rows, step, 0)


def ffn_fwd(x, g, wi, wo, name, hosted=()):
    T = x.shape[0]
    nj = D_FF // FF_CHUNK
    gather = ChipGather(list(hosted))
    n = gather.n

    def body(*refs):
        x_ref, g_ref, wg_ref, wu_ref, wo_ref = refs[:5]
        hx, o_ref, ho = refs[5:5 + n], refs[5 + n], refs[6 + n:6 + 2 * n]
        h_ref, acc_ref = refs[6 + 2 * n:8 + 2 * n]
        sems = refs[8 + 2 * n:]
        j = pl.program_id(0)

        @pl.when(j == 0)
        def _():
            gather.start(hx, ho, sems)

            def init(r):
                h_ref[r, :] = _rms(x_ref[r, :], g_ref[...]).astype(BF16)
                acc_ref[r, :] = jnp.zeros((r.size, D_MODEL), F32)
            _row_loop(T, init)

        def blk(r):
            hb = h_ref[r, :]
            gate = jnp.dot(hb, wg_ref[...], preferred_element_type=F32)
            up = jnp.dot(hb, wu_ref[...], preferred_element_type=F32)
            a = (gate * jax.nn.sigmoid(gate) * up).astype(BF16)
            acc_ref[r, :] += jnp.dot(a, wo_ref[...], preferred_element_type=F32)
        _row_loop(T, blk)

        @pl.when(j == nj - 2)
        def _():
            gather.relay(hx, ho, sems)

        @pl.when(j == nj - 1)
        def _():
            def fin(r):
                o_ref[r, :] = x_ref[r, :] + 0.5 * acc_ref[r, :]
            _row_loop(T, fin)
            gather.finish(hx, ho, sems)

    full = pl.BlockSpec((T, D_MODEL), lambda j: (0, 0))
    res = pl.pallas_call(
        body, name=name, grid=(nj,),
        in_specs=[full, pl.BlockSpec((1, D_MODEL), lambda j: (0, 0)),
                  pl.BlockSpec((D_MODEL, FF_CHUNK), lambda j: (0, j)),
                  pl.BlockSpec((D_MODEL, FF_CHUNK), lambda j: (0, j + nj)),
                  pl.BlockSpec((FF_CHUNK, D_MODEL), lambda j: (j, 0))] + gather.in_specs,
        out_specs=tuple([full] + gather.out_specs),
        out_shape=tuple([jax.ShapeDtypeStruct((T, D_MODEL), F32)] + gather.out_shape),
        scratch_shapes=[pltpu.VMEM((T, D_MODEL), BF16), pltpu.VMEM((T, D_MODEL), F32)] + gather.scratch,
        compiler_params=_params(("arbitrary",)))(x, g, wi, wi, wo, *hosted)
    return res[0], list(res[1:])


def _norm_bwd_rows(x, g, dh, dres):
    rstd = lax.rsqrt(jnp.mean(x * x, axis=-1, keepdims=True) + NORM_EPS)
    xh = x * rstd
    dxh = dh * g
    dx = rstd * (dxh - xh * jnp.mean(dxh * xh, axis=-1, keepdims=True))
    return dres + dx, jnp.sum(dh * xh, axis=0, keepdims=True)


def ffn_bwd(x, dy, g, wi, wo, name, hosted=()):
    T = x.shape[0]
    nj = D_FF // FF_CHUNK
    scatter = ChipScatter(list(hosted))
    n = scatter.n

    def body(*refs):
        x_ref, dy_ref, g_ref, wg_ref, wu_ref, wo_ref = refs[:6]
        hx = refs[6:6 + n]
        dx_ref, dg_ref, dwg_ref, dwu_ref, dwo_ref = refs[6 + n:11 + n]
        ho = refs[11 + n:11 + 2 * n]
        h_ref, da_ref, dh_ref = refs[11 + 2 * n:14 + 2 * n]
        sems = refs[14 + 2 * n:]
        j = pl.program_id(0)

        @pl.when(j == 0)
        def _():
            scatter.start(hx, ho, sems)

            def init(r):
                h_ref[r, :] = _rms(x_ref[r, :], g_ref[...]).astype(BF16)
                da_ref[r, :] = (0.5 * dy_ref[r, :]).astype(BF16)
                dh_ref[r, :] = jnp.zeros((r.size, D_MODEL), F32)
            _row_loop(T, init)

        dwg_ref[...] = jnp.zeros_like(dwg_ref)
        dwu_ref[...] = jnp.zeros_like(dwu_ref)
        dwo_ref[...] = jnp.zeros_like(dwo_ref)

        def blk(r):
            hb = h_ref[r, :]
            db = da_ref[r, :]
            gate = jnp.dot(hb, wg_ref[...], preferred_element_type=F32)
            up = jnp.dot(hb, wu_ref[...], preferred_element_type=F32)
            sg = jax.nn.sigmoid(gate)
            sl = gate * sg
            da = _bdot_nt(db, wo_ref[...])
            dup = (da * sl).astype(BF16)
            dgate = (da * up * (sg * (1.0 + gate * (1.0 - sg)))).astype(BF16)
            dwo_ref[...] += _bdot_tn((sl * up).astype(BF16), db)
            dwg_ref[...] += _bdot_tn(hb, dgate)
            dwu_ref[...] += _bdot_tn(hb, dup)
            dh_ref[r, :] += _bdot_nt(dgate, wg_ref[...]) + _bdot_nt(dup, wu_ref[...])
        _row_loop(T, blk)

        @pl.when(j == nj - 1)
        def _():
            dg_ref[...] = jnp.zeros_like(dg_ref)

            def fin(r):
                dx, dg = _norm_bwd_rows(x_ref[r, :], g_ref[...], dh_ref[r, :], dy_ref[r, :])
                dx_ref[r, :] = dx
                dg_ref[...] += dg
            _row_loop(T, fin)
            scatter.finish(hx, ho, sems)

    full = pl.BlockSpec((T, D_MODEL), lambda j: (0, 0))
    vec = pl.BlockSpec((1, D_MODEL), lambda j: (0, 0))
    res = pl.pallas_call(
        body, name=name, grid=(nj,),
        in_specs=[full, full, vec,
                  pl.BlockSpec((D_MODEL, FF_CHUNK), lambda j: (0, j)),
                  pl.BlockSpec((D_MODEL, FF_CHUNK), lambda j: (0, j + nj)),
                  pl.BlockSpec((FF_CHUNK, D_MODEL), lambda j: (j, 0))] + scatter.in_specs,
        out_specs=tuple([full, vec,
                         pl.BlockSpec((D_MODEL, FF_CHUNK), lambda j: (0, j)),
                         pl.BlockSpec((D_MODEL, FF_CHUNK), lambda j: (0, j)),
                         pl.BlockSpec((FF_CHUNK, D_MODEL), lambda j: (j, 0))] + scatter.out_specs),
        out_shape=tuple([jax.ShapeDtypeStruct((T, D_MODEL), F32), jax.ShapeDtypeStruct((1, D_MODEL), F32),
                         jax.ShapeDtypeStruct((D_MODEL, D_FF), F32), jax.ShapeDtypeStruct((D_MODEL, D_FF), F32),
                         jax.ShapeDtypeStruct((D_FF, D_MODEL), F32)] + scatter.out_shape),
        scratch_shapes=[pltpu.VMEM((T, D_MODEL), BF16), pltpu.VMEM((T, D_MODEL), BF16),
                        pltpu.VMEM((T, D_MODEL), F32)] + scatter.scratch,
        compiler_params=_params(("arbitrary",)))(x, dy, g, wi, wi, wo, *hosted)
    return res[0], res[1], res[2], res[3], res[4], list(res[5:])


def proj_fwd(x, g, w, name):
    T = x.shape[0]
    nj = D_IN_PAD // IN_CHUNK

    def body(x_ref, g_ref, w_ref, o_ref, h_ref):
        @pl.when(pl.program_id(0) == 0)
        def _():
            def init(r):
                h_ref[r, :] = _rms(x_ref[r, :], g_ref[...]).astype(BF16)
            _row_loop(T, init)

        def blk(r):
            o_ref[r, :] = jnp.dot(h_ref[r, :], w_ref[...], preferred_element_type=F32)
        _row_loop(T, blk)

    return pl.pallas_call(
        body, name=name, grid=(nj,),
        in_specs=[pl.BlockSpec((T, D_MODEL), lambda j: (0, 0)), pl.BlockSpec((1, D_MODEL), lambda j: (0, 0)),
                  pl.BlockSpec((D_MODEL, IN_CHUNK), lambda j: (0, j))],
        out_specs=pl.BlockSpec((T, IN_CHUNK), lambda j: (0, j)),
        out_shape=jax.ShapeDtypeStruct((T, D_IN_PAD), F32),
        scratch_shapes=[pltpu.VMEM((T, D_MODEL), BF16)],
        compiler_params=_params(("arbitrary",)))(x, g, w)


def proj_bwd(x, dres, g, w, dp, name):
    T = x.shape[0]
    nj = D_IN_PAD // IN_CHUNK

    def body(x_ref, dres_ref, g_ref, w_ref, dp_ref, dx_ref, dg_ref, dw_ref, h_ref, dh_ref):
        j = pl.program_id(0)

        @pl.when(j == 0)
        def _():
            def init(r):
                h_ref[r, :] = _rms(x_ref[r, :], g_ref[...]).astype(BF16)
                dh_ref[r, :] = jnp.zeros((r.size, D_MODEL), F32)
            _row_loop(T, init)

        dw_ref[...] = jnp.zeros_like(dw_ref)

        def blk(r):
            dpb = dp_ref[r, :].astype(BF16)
            dw_ref[...] += _bdot_tn(h_ref[r, :], dpb)
            dh_ref[r, :] += _bdot_nt(dpb, w_ref[...])
        _row_loop(T, blk)

        @pl.when(j == nj - 1)
        def _():
            dg_ref[...] = jnp.zeros_like(dg_ref)

            def fin(r):
                dx, dg = _norm_bwd_rows(x_ref[r, :], g_ref[...], dh_ref[r, :], dres_ref[r, :])
                dx_ref[r, :] = dx
                dg_ref[...] += dg
            _row_loop(T, fin)

    full = pl.BlockSpec((T, D_MODEL), lambda j: (0, 0))
    vec = pl.BlockSpec((1, D_MODEL), lambda j: (0, 0))
    return pl.pallas_call(
        body, name=name, grid=(nj,),
        in_specs=[full, full, vec, pl.BlockSpec((D_MODEL, IN_CHUNK), lambda j: (0, j)),
                  pl.BlockSpec((T, IN_CHUNK), lambda j: (0, j))],
        out_specs=(full, vec, pl.BlockSpec((D_MODEL, IN_CHUNK), lambda j: (0, j))),
        out_shape=(jax.ShapeDtypeStruct((T, D_MODEL), F32), jax.ShapeDtypeStruct((1, D_MODEL), F32),
                   jax.ShapeDtypeStruct((D_MODEL, D_IN_PAD), F32)),
        scratch_shapes=[pltpu.VMEM((T, D_MODEL), BF16), pltpu.VMEM((T, D_MODEL), F32)],
        compiler_params=_params(("arbitrary",)))(x, dres, g, w, dp)


def out_fwd(mixed, w, x, name):
    T = x.shape[0]

    def body(m_ref, w_ref, x_ref, o_ref):
        o_ref[...] = x_ref[...] + jnp.dot(m_ref[...].astype(BF16), w_ref[...], preferred_element_type=F32)

    blk = pl.BlockSpec((ROWS, D_MODEL), lambda i: (i, 0))
    return pl.pallas_call(
        body, name=name, grid=(T // ROWS,),
        in_specs=[blk, pl.BlockSpec((D_MODEL, D_MODEL), lambda i: (0, 0)), blk],
        out_specs=blk, out_shape=jax.ShapeDtypeStruct((T, D_MODEL), F32),
        compiler_params=_params(("arbitrary",)))(mixed, w, x)


def out_bwd(mixed, w, dy, name):
    T = dy.shape[0]

    def body(m_ref, w_ref, dy_ref, dm_ref, dw_ref):
        @pl.when(pl.program_id(0) == 0)
        def _():
            dw_ref[...] = jnp.zeros_like(dw_ref)
        dyb = dy_ref[...].astype(BF16)
        dm_ref[...] = _bdot_nt(dyb, w_ref[...])
        dw_ref[...] += _bdot_tn(m_ref[...].astype(BF16), dyb)

    blk = pl.BlockSpec((ROWS, D_MODEL), lambda i: (i, 0))
    sq = pl.BlockSpec((D_MODEL, D_MODEL), lambda i: (0, 0))
    return pl.pallas_call(
        body, name=name, grid=(T // ROWS,),
        in_specs=[blk, sq, blk], out_specs=(blk, sq),
        out_shape=(jax.ShapeDtypeStruct((T, D_MODEL), F32), jax.ShapeDtypeStruct((D_MODEL, D_MODEL), F32)),
        compiler_params=_params(("arbitrary",)))(mixed, w, dy)


def loss_head(x, g, target, name):
    T = x.shape[0]

    def body(x_ref, g_ref, t_ref, loss_ref, dx_ref, dg_ref):
        @pl.when(pl.program_id(0) == 0)
        def _():
            loss_ref[...] = jnp.zeros_like(loss_ref)
            dg_ref[...] = jnp.zeros_like(dg_ref)
        xb = x_ref[...]
        rstd = lax.rsqrt(jnp.mean(xb * xb, axis=-1, keepdims=True) + NORM_EPS)
        xh = xb * rstd
        err = xh * g_ref[...] - t_ref[...]
        loss_ref[...] += 0.5 * jnp.sum(jnp.mean(err * err, axis=-1, keepdims=True), axis=0, keepdims=True)
        dy = err * (1.0 / D_MODEL)
        dg_ref[...] += jnp.sum(dy * xh, axis=0, keepdims=True)
        dxh = dy * g_ref[...]
        dx_ref[...] = rstd * (dxh - xh * jnp.mean(dxh * xh, axis=-1, keepdims=True))

    blk = pl.BlockSpec((ROWS, D_MODEL), lambda i: (i, 0))
    vec = pl.BlockSpec((1, D_MODEL), lambda i: (0, 0))
    return pl.pallas_call(
        body, name=name, grid=(T // ROWS,),
        in_specs=[blk, vec, blk], out_specs=(pl.BlockSpec((1, 1), lambda i: (0, 0)), blk, vec),
        out_shape=(jax.ShapeDtypeStruct((1, 1), F32), jax.ShapeDtypeStruct((T, D_MODEL), F32),
                   jax.ShapeDtypeStruct((1, D_MODEL), F32)),
        compiler_params=_params(("arbitrary",)))(x, g, target)


def rowwise_fwd(fn, rows, shared, out_widths, name):
    T = rows[0].shape[0]
    n_in = len(rows) + len(shared)

    def body(*refs):
        res = fn(*[r[...] for r in refs[:n_in]])
        for o, v in zip(refs[n_in:], res):
            o[...] = v

    in_specs = ([pl.BlockSpec((ROWS, a.shape[1]), lambda i: (i, 0)) for a in rows]
                + [pl.BlockSpec(a.shape, lambda i: (0, 0)) for a in shared])
    return pl.pallas_call(
        body, name=name, grid=(T // ROWS,), in_specs=in_specs,
        out_specs=tuple(pl.BlockSpec((ROWS, w), lambda i: (i, 0)) for w in out_widths),
        out_shape=tuple(jax.ShapeDtypeStruct((T, w), F32) for w in out_widths),
        compiler_params=_params(("arbitrary",)))(*rows, *shared)


def rowwise_bwd(fn, rows, shared, cts, name, ct_fn=None):
    T = rows[0].shape[0]
    nr, ns, nc = len(rows), len(shared), len(cts)

    def body(*refs):
        ins = [r[...] for r in refs[:nr + ns]]
        ctv = tuple(r[...] for r in refs[nr + ns:nr + ns + nc])
        outs = refs[nr + ns + nc:]
        _, vjp = jax.vjp(fn, *ins)
        grads = vjp(ct_fn(*ctv) if ct_fn is not None else ctv)
        for k in range(nr):
            outs[k][...] = grads[k]

        @pl.when(pl.program_id(0) == 0)
        def _():
            for k in range(ns):
                outs[nr + k][...] = jnp.zeros_like(outs[nr + k])
        for k in range(ns):
            outs[nr + k][...] += grads[nr + k]

    row_spec = lambda a: pl.BlockSpec((ROWS, a.shape[1]), lambda i: (i, 0))
    sh_spec = lambda a: pl.BlockSpec(a.shape, lambda i: (0, 0))
    return pl.pallas_call(
        body, name=name, grid=(T // ROWS,),
        in_specs=[row_spec(a) for a in rows] + [sh_spec(a) for a in shared] + [row_spec(a) for a in cts],
        out_specs=tuple([row_spec(a) for a in rows] + [sh_spec(a) for a in shared]),
        out_shape=tuple(jax.ShapeDtypeStruct(a.shape, F32) for a in list(rows) + list(shared)),
        compiler_params=_params(("arbitrary",)))(*rows, *shared, *cts)


def shift_rows(x, s):
    return jnp.pad(x, ((s, 0), (0, 0)))[:x.shape[0]]


def unshift_rows(x, s):
    return jnp.pad(x, ((0, s), (0, 0)))[s:]


def _neg_expm1(y):
    series = -(y * (1.0 + y * (0.5 + y * (1.0 / 6.0 + y * (1.0 / 24.0)))))
    return jnp.where(y > -0.05, series, 1.0 - jnp.exp(y))


def lru_pre_fn(x0, x1, x2, x3, first, w0, w1, w2, w3, cb, ga, gab, gx, gxb, lam):
    xc = w3 * x0 + w2 * x1 + w1 * x2 + w0 * x3 + cb
    r = jax.nn.sigmoid(_hdot(xc, ga) + gab)
    i = jax.nn.sigmoid(_hdot(xc, gx) + gxb)
    log_a = -LRU_C * r * jax.nn.softplus(-lam)
    a = jnp.exp(log_a)
    mult = jnp.where(first > 0.5, 1.0, jnp.sqrt(_neg_expm1(2.0 * log_a)))
    return a, mult * i * xc


def lru_post_fn(h, py, og):
    return (_rms(h * jax.nn.gelu(py), og),)


def lru_scan(a, b, reverse, name):
    T, C = a.shape
    nb = T // 8

    def body(a_ref, b_ref, h_ref):
        rows = lax.broadcasted_iota(jnp.int32, (8, C), 0)

        def blk(i, carry):
            j = nb - 1 - i if reverse else i
            r = pl.ds(pl.multiple_of(j * 8, 8), 8)
            A = a_ref[r, :]
            B = b_ref[r, :]
            for s in (1, 2, 4):
                if reverse:
                    keep = rows < 8 - s
                    sh = 8 - s
                else:
                    keep = rows >= s
                    sh = s
                Bs = jnp.where(keep, pltpu.roll(B, sh, 0), 0.0)
                As = jnp.where(keep, pltpu.roll(A, sh, 0), 1.0)
                B = B + A * Bs
                A = A * As
            hb = B + A * carry
            h_ref[r, :] = hb
            edge = 0 if reverse else 7
            return jnp.sum(jnp.where(rows == edge, hb, 0.0), axis=0, keepdims=True)

        lax.fori_loop(0, nb, blk, jnp.zeros((1, C), F32))

    full = pl.BlockSpec((T, C), lambda: (0, 0))
    return pl.pallas_call(body, name=name, in_specs=[full, full], out_specs=full,
                          out_shape=jax.ShapeDtypeStruct((T, C), F32), compiler_params=_params())(a, b)


def make_rwkv_pre_fn(has_vres):
    def fn(p, pp, *rest):
        if has_vres:
            vf, mu, w_up, w_b, a_up, a_b, g_up, kk_w, ka_w, vw1, vw2, vb = rest
        else:
            mu, w_up, w_b, a_up, a_b, g_up, kk_w, ka_w = rest
        xm = p + (pp - p) * mu
        r, k, v = xm[:, 0:384], xm[:, 384:768], xm[:, 768:1152]
        xw, xa, xg = xm[:, 1152:1216], xm[:, 1216:1280], xm[:, 1280:1408]
        w_log = -jax.nn.softplus(-(w_b + _hdot(jnp.tanh(xw), w_up))) - 0.5
        lw = -jnp.exp(w_log)
        a = jax.nn.sigmoid(a_b + _hdot(xa, a_up))
        g = _hdot(jax.nn.sigmoid(xg), g_up)
        if has_vres:
            v = v + (vf - v) * jax.nn.sigmoid(vb + _hdot(_hdot(v, vw1), vw2))
        kkx = k * kk_w
        kk = kkx * lax.rsqrt(_segsum(kkx * kkx) + 1e-6)
        k2 = k * (1.0 + (a - 1.0) * ka_w)
        return r, lw, k2, v, kk, a, g
    return fn


def rwkv_post_fn(y, r, k2, v, g, ln_g, ln_b, r_k):
    mean = _segsum(y) * (1.0 / HEAD_DIM)
    yc = y - mean
    var = _segsum(yc * yc) * (1.0 / HEAD_DIM)
    yn = yc * lax.rsqrt(var + GN_EPS) * ln_g + ln_b
    bonus = _segsum(r * k2 * r_k) * v
    return ((yn + bonus) * g,)


def _head_expander(first_lane):
    ri, ci = _iota2(128, MIX_W)
    return (ri == ci // HEAD_DIM + first_lane).astype(F32)


def gdn_pre_fn(x0, x1, x2, x3, ab, w0, w1, w2, w3, alog, dtb):
    qkv = jax.nn.silu(w3 * x0 + w2 * x1 + w1 * x2 + w0 * x3)
    q, k, v = qkv[:, 0:384], qkv[:, 384:768], qkv[:, 768:1152]
    q = q * lax.rsqrt(_segsum(q * q) + 1e-6) * (HEAD_DIM ** -0.5)
    k = k * lax.rsqrt(_segsum(k * k) + 1e-6)
    g = -jnp.exp(alog) * jax.nn.softplus(ab + dtb)
    beta = jax.nn.sigmoid(ab)
    ge = _xdot(g, functools.partial(_head_expander, 0))
    be = _xdot(beta, functools.partial(_head_expander, HEADS))
    return q, k, v, ge, be


def gdn_post_fn(o, z, ng):
    ms = _segsum(o * o) * (1.0 / HEAD_DIM)
    return (o * lax.rsqrt(ms + NORM_EPS) * ng * jax.nn.silu(z),)


def _neumann_inv(m):
    n = m.shape[-1]
    ri, ci = _iota2(n, n)
    eye = (ri == ci).astype(F32)
    md = jnp.where(ri // 16 == ci // 16, m, 0.0)
    mo = m - md
    t0 = eye + md
    p2 = _hdot(md, md)
    t0 = t0 + _hdot(t0, p2)
    p4 = _hdot(p2, p2)
    t0 = t0 + _hdot(t0, p4)
    p8 = _hdot(p4, p4)
    t0 = t0 + _hdot(t0, p8)
    nn = _hdot(t0, mo)
    n2 = _hdot(nn, nn)
    t1 = eye + nn + n2 + _hdot(nn, n2)
    return _hdot(t1, t0)


@jax.custom_vjp
def _inv_saved(m, t_saved):
    return t_saved


def _inv_saved_fwd(m, t_saved):
    return t_saved, t_saved


def _inv_saved_bwd(t_saved, dt):
    tt = jnp.swapaxes(t_saved, -1, -2)
    return _hdot(_hdot(tt, dt), tt), jnp.zeros_like(t_saved)


_inv_saved.defvjp(_inv_saved_fwd, _inv_saved_bwd)


def _heads(x):
    return jnp.concatenate([x[None, :, h * HEAD_DIM:(h + 1) * HEAD_DIM] for h in range(HEADS)], axis=0)


def _unheads(y):
    return jnp.concatenate([lax.index_in_dim(y, h, 0, keepdims=False) for h in range(HEADS)], axis=1)


def rwkv_heads(s0, r, lw, k2, v, kk, a, inv):
    n = r.shape[0]
    ri, ci = _iota2(n, n)
    low, strict = ri >= ci, ri > ci
    cs = _cumsum_rows(lw)
    cl = jnp.sum(lw, axis=0, keepdims=True)
    p_in, p_prev, p_inv = jnp.exp(cs), jnp.exp(cs - lw), jnp.exp(-cs)
    p_rest, p_all = jnp.exp(cl - cs), jnp.exp(cl)
    bd = kk * a
    at, rt = _heads(-kk * p_prev), _heads(r * p_in)
    bh, kh = _heads(bd * p_inv), _heads(k2 * p_inv)
    vh = _heads(v)
    m_ab = jnp.where(strict, _cdot_nt(at, bh), 0.0)
    m_ak = jnp.where(strict, _cdot_nt(at, kh), 0.0)
    m_rb = jnp.where(low, _cdot_nt(rt, bh), 0.0)
    m_rk = jnp.where(low, _cdot_nt(rt, kh), 0.0)
    sa = _cdot(inv(m_ab), _cdot_nt(at, s0) + _cdot(m_ak, vh))
    y = _cdot_nt(rt, s0) + _cdot(m_rb, sa) + _cdot(m_rk, vh)
    s1 = s0 * _heads(p_all) + _cdot_tn(sa, _heads(bd * p_rest)) + _cdot_tn(vh, _heads(k2 * p_rest))
    return _unheads(y), s1


def gdn_heads(s0, q, k, v, ge, be, inv):
    n = q.shape[0]
    ri, ci = _iota2(n, n)
    low, strict = ri >= ci, ri > ci
    gc = _cumsum_rows(ge)
    gl = jnp.sum(ge, axis=0, keepdims=True)
    gch = _heads(gc)
    decay = jnp.where(low, jnp.exp(jnp.where(low, gch - jnp.swapaxes(gch, 1, 2), 0.0)), 0.0)
    kb = k * be
    e = jnp.exp(gc)
    kh = _heads(k)
    m = -jnp.where(strict, _cdot_nt(_heads(kb), kh) * decay, 0.0)
    mr = jnp.where(low, _cdot_nt(_heads(q), kh) * decay, 0.0)
    u = _cdot(inv(m), _heads(v * be) - _cdot_nt(_heads(kb * e), s0))
    y = _cdot_nt(_heads(q * e), s0) + _cdot(mr, u)
    s1 = s0 * _heads(jnp.exp(gl)) + _cdot_tn(u, _heads(k * jnp.exp(gl - gc)))
    return _unheads(y), s1


def core_fwd(heads_fn, ins, name, hosted=()):
    T = ins[0].shape[0]
    nc = T // CHUNK
    steps = nc // CHUNKS_PER_STEP
    n = len(ins)
    gather = ChipGather(list(hosted))
    ng = gather.n

    def body(*refs):
        hx = refs[n:n + ng]
        y_ref, s0_ref, t_ref = refs[n + ng:n + ng + 3]
        ho = refs[n + ng + 3:n + 2 * ng + 3]
        s_ref = refs[n + 2 * ng + 3]
        sems = refs[n + 2 * ng + 4:]
        c = pl.program_id(0)

        @pl.when(c == 0)
        def _():
            gather.start(hx, ho, sems)
            s_ref[...] = jnp.zeros_like(s_ref)

        state = s_ref[...]
        for u in range(CHUNKS_PER_STEP):
            rows = slice(u * CHUNK, (u + 1) * CHUNK)
            kept = []

            def inv(m):
                kept.append(_neumann_inv(m))
                return kept[0]

            y, after = heads_fn(state, *[r[rows, :] for r in refs[:n]], inv)
            y_ref[rows, :] = y
            s0_ref[u] = state
            t_ref[u] = kept[0]
            state = after
        s_ref[...] = state

        @pl.when(c == max(steps - 2, 0))
        def _():
            gather.relay(hx, ho, sems)

        @pl.when(c == steps - 1)
        def _():
            gather.finish(hx, ho, sems)

    row = pl.BlockSpec((CHUNKS_PER_STEP * CHUNK, MIX_W), lambda c: (c, 0))
    st_shape = (HEADS, HEAD_DIM, HEAD_DIM)
    st = pl.BlockSpec((CHUNKS_PER_STEP,) + st_shape, lambda c: (c, 0, 0, 0))
    res = pl.pallas_call(
        body, name=name, grid=(steps,), in_specs=[row] * n + gather.in_specs,
        out_specs=tuple([row, st, st] + gather.out_specs),
        out_shape=tuple([jax.ShapeDtypeStruct((T, MIX_W), F32), jax.ShapeDtypeStruct((nc,) + st_shape, F32),
                         jax.ShapeDtypeStruct((nc,) + st_shape, F32)] + gather.out_shape),
        scratch_shapes=[pltpu.VMEM(st_shape, F32)] + gather.scratch,
        compiler_params=_params(("arbitrary",)))(*ins, *hosted)
    return res[0], res[1], res[2], list(res[3:])


def core_bwd(heads_fn, ins, s0_all, t_all, dy, name):
    T = ins[0].shape[0]
    nc = T // CHUNK
    steps = nc // CHUNKS_PER_STEP
    n = len(ins)

    def body(*refs):
        s0_ref, t_ref, dy_ref = refs[n:n + 3]
        outs = refs[n + 3:n + 3 + n]
        ds_ref = refs[n + 3 + n]

        @pl.when(pl.program_id(0) == 0)
        def _():
            ds_ref[...] = jnp.zeros_like(ds_ref)

        d_state = ds_ref[...]
        for u in reversed(range(CHUNKS_PER_STEP)):
            rows = slice(u * CHUNK, (u + 1) * CHUNK)
            t_saved = t_ref[u]
            f = lambda s0, *xs: heads_fn(s0, *xs, lambda m: _inv_saved(m, t_saved))
            _, vjp = jax.vjp(f, s0_ref[u], *[r[rows, :] for r in refs[:n]])
            grads = vjp((dy_ref[rows, :], d_state))
            d_state = grads[0]
            for k in range(n):
                outs[k][rows, :] = grads[1 + k]
        ds_ref[...] = d_state

    row = pl.BlockSpec((CHUNKS_PER_STEP * CHUNK, MIX_W), lambda c: (steps - 1 - c, 0))
    st_shape = (HEADS, HEAD_DIM, HEAD_DIM)
    st = pl.BlockSpec((CHUNKS_PER_STEP,) + st_shape, lambda c: (steps - 1 - c, 0, 0, 0))
    return pl.pallas_call(
        body, name=name, grid=(steps,), in_specs=[row] * n + [st, st, row], out_specs=tuple([row] * n),
        out_shape=tuple(jax.ShapeDtypeStruct((T, MIX_W), F32) for _ in range(n)),
        scratch_shapes=[pltpu.VMEM(st_shape, F32)],
        compiler_params=_params(("arbitrary",)))(*ins, s0_all, t_all, dy)


def _block_diag(w):
    out = jnp.zeros((LRU_W, LRU_W), w.dtype)
    for n in range(LRU_BLOCKS):
        out = lax.dynamic_update_slice(out, w[n], (n * 64, n * 64))
    return out


def _block_diag_grad(g):
    return jnp.stack([g[n * 64:(n + 1) * 64, n * 64:(n + 1) * 64] for n in range(LRU_BLOCKS)])


def _row(v):
    return v.reshape(1, -1)


def _pad128(v):
    return jnp.pad(v.reshape(1, -1), ((0, 0), (0, 128 - v.size)))


def _layer_shared(w, l):
    cw = w['lru_conv_w'][l]
    lru_pre = [_row(cw[0]), _row(cw[1]), _row(cw[2]), _row(cw[3]), _row(w['lru_conv_b'][l]),
               _block_diag(w['lru_gate_a_w'][l]), _row(w['lru_gate_a_b'][l]),
               _block_diag(w['lru_gate_x_w'][l]), _row(w['lru_gate_x_b'][l]), _row(w['lru_lambda'][l])]
    rw_pre = [_row(w['rwkv_mu'][l]), w['rwkv_w_up'][l], _row(w['rwkv_w_bias'][l]), w['rwkv_a_up'][l],
              _row(w['rwkv_a_bias'][l]), w['rwkv_g_up'][l], _row(w['rwkv_k_k'][l]), _row(w['rwkv_k_a'][l])]
    if l > 0:
        rw_pre += [w['rwkv_vres_w1'][l - 1], w['rwkv_vres_w2'][l - 1], _row(w['rwkv_vres_b'][l - 1])]
    rw_post = [_row(w['rwkv_ln_g'][l]), _row(w['rwkv_ln_b'][l]), _row(w['rwkv_r_k'][l])]
    gw = w['gdn_conv_w'][l]
    gdn_pre = [_row(gw[0]), _row(gw[1]), _row(gw[2]), _row(gw[3]), _pad128(w['gdn_a_log'][l]),
               _pad128(w['gdn_dt_bias'][l])]
    gdn_post = [_row(jnp.tile(w['gdn_norm'][l], HEADS))]
    return dict(lru_pre=lru_pre, lru_post=[_row(w['lru_out_norm'][l])], rw_pre=rw_pre, rw_post=rw_post,
                gdn_pre=gdn_pre, gdn_post=gdn_post)


def _mixer_fwd(p, sh, l, v_first, host_rwkv=(), host_gdn=()):
    T = p.shape[0]
    lx, ly = p[:, 0:256], p[:, 256:512]
    prw, qkv, z, ab = p[:, 512:1920], p[:, 1920:3072], p[:, 3072:3456], p[:, 3456:3584]
    first = (lax.broadcasted_iota(jnp.int32, (T, LRU_W), 0) == 0).astype(F32)
    lru_rows = [lx, shift_rows(lx, 1), shift_rows(lx, 2), shift_rows(lx, 3), first]
    a, b = rowwise_fwd(lru_pre_fn, lru_rows, sh['lru_pre'], (LRU_W, LRU_W), f"lru_pre_fwd{l}")
    hseq = lru_scan(a, b, False, f"lru_scan_fwd{l}")
    (y_lru,) = rowwise_fwd(lru_post_fn, [hseq, ly], sh['lru_post'], (LRU_W,), f"lru_post_fwd{l}")

    rw_rows = [prw, shift_rows(prw, 1)] + ([v_first] if l > 0 else [])
    rw = rowwise_fwd(make_rwkv_pre_fn(l > 0), rw_rows, sh['rw_pre'], (MIX_W,) * 7, f"rwkv_pre_fwd{l}")
    r, lw, k2, v, kk, ar, g = rw
    y_raw, rs0, rt, got_rwkv = core_fwd(rwkv_heads, [r, lw, k2, v, kk, ar], f"rwkv_core_fwd{l}", host_rwkv)
    (y_rw,) = rowwise_fwd(rwkv_post_fn, [y_raw, r, k2, v, g], sh['rw_post'], (MIX_W,), f"rwkv_post_fwd{l}")

    gdn_rows = [qkv, shift_rows(qkv, 1), shift_rows(qkv, 2), shift_rows(qkv, 3), ab]
    gd = rowwise_fwd(gdn_pre_fn, gdn_rows, sh['gdn_pre'], (MIX_W,) * 5, f"gdn_pre_fwd{l}")
    o_raw, gs0, gt, got_gdn = core_fwd(gdn_heads, list(gd), f"gdn_core_fwd{l}", host_gdn)
    (y_gdn,) = rowwise_fwd(gdn_post_fn, [o_raw, z], sh['gdn_post'], (MIX_W,), f"gdn_post_fwd{l}")

    mixed = jnp.concatenate([y_lru, y_rw, y_gdn], axis=1)
    saved = dict(lru_rows=lru_rows, a=a, hseq=hseq, ly=ly, rw_rows=rw_rows, rw=rw, y_raw=y_raw, rs0=rs0, rt=rt,
                 gdn_rows=gdn_rows, gd=gd, o_raw=o_raw, gs0=gs0, gt=gt, z=z)
    v_layer0 = v if l == 0 else None
    return mixed, saved, v_layer0, got_rwkv, got_gdn


def _mixer_bwd(dmixed, sv, sh, l, dv_first):
    d_lru, d_rw, d_gdn = dmixed[:, 0:256], dmixed[:, 256:640], dmixed[:, 640:1024]
    gw = {}

    dh, dly, d_og = rowwise_bwd(lru_post_fn, [sv['hseq'], sv['ly']], sh['lru_post'], [d_lru], f"lru_post_bwd{l}")
    gscan = lru_scan(unshift_rows(sv['a'], 1), dh, True, f"lru_scan_bwd{l}")
    res = rowwise_bwd(lru_pre_fn, sv['lru_rows'], sh['lru_pre'], [gscan, shift_rows(sv['hseq'], 1)],
                      f"lru_pre_bwd{l}", ct_fn=lambda gs, hp: (gs * hp, gs))
    dlx = res[0] + unshift_rows(res[1], 1) + unshift_rows(res[2], 2) + unshift_rows(res[3], 3)
    dw0, dw1, dw2, dw3, dcb, dga, dgab, dgx, dgxb, dlam = res[5:]
    gw['lru_conv_w'] = jnp.concatenate([dw0, dw1, dw2, dw3], axis=0)
    gw['lru_conv_b'] = dcb[0]
    gw['lru_gate_a_w'] = _block_diag_grad(dga)
    gw['lru_gate_a_b'] = dgab.reshape(LRU_BLOCKS, 64)
    gw['lru_gate_x_w'] = _block_diag_grad(dgx)
    gw['lru_gate_x_b'] = dgxb.reshape(LRU_BLOCKS, 64)
    gw['lru_lambda'] = dlam[0]
    gw['lru_out_norm'] = d_og[0]

    r, lw, k2, v, kk, ar, g = sv['rw']
    res = rowwise_bwd(rwkv_post_fn, [sv['y_raw'], r, k2, v, g], sh['rw_post'], [d_rw], f"rwkv_post_bwd{l}")
    dy_raw, dr_p, dk2_p, dv_p, dg = res[:5]
    gw['rwkv_ln_g'], gw['rwkv_ln_b'], gw['rwkv_r_k'] = res[5][0], res[6][0], res[7].reshape(HEADS, HEAD_DIM)
    dr_c, dlw, dk2_c, dv_c, dkk, dar = core_bwd(rwkv_heads, [r, lw, k2, v, kk, ar], sv['rs0'], sv['rt'], dy_raw,
                                                 f"rwkv_core_bwd{l}")
    cts = [dr_p, dr_c, dlw, dk2_p, dk2_c, dv_p, dv_c, dkk, dar, dg]
    if l == 0:
        cts.append(dv_first)
        ct_fn = lambda a1, a2, b, c1, c2, d1, d2, e, f, gg, vf: (a1 + a2, b, c1 + c2, d1 + d2 + vf, e, f, gg)
    else:
        ct_fn = lambda a1, a2, b, c1, c2, d1, d2, e, f, gg: (a1 + a2, b, c1 + c2, d1 + d2, e, f, gg)
    res = rowwise_bwd(make_rwkv_pre_fn(l > 0), sv['rw_rows'], sh['rw_pre'], cts, f"rwkv_pre_bwd{l}", ct_fn=ct_fn)
    dprw = res[0] + unshift_rows(res[1], 1)
    nrow = len(sv['rw_rows'])
    dv_first_out = res[2] if l > 0 else None
    sg = res[nrow:]
    gw['rwkv_mu'], gw['rwkv_w_up'], gw['rwkv_w_bias'], gw['rwkv_a_up'] = sg[0][0], sg[1], sg[2][0], sg[3]
    gw['rwkv_a_bias'], gw['rwkv_g_up'], gw['rwkv_k_k'], gw['rwkv_k_a'] = sg[4][0], sg[5], sg[6][0], sg[7][0]
    if l > 0:
        gw['rwkv_vres_w1'], gw['rwkv_vres_w2'], gw['rwkv_vres_b'] = sg[8], sg[9], sg[10][0]

    do_raw, dz, d_ng = rowwise_bwd(gdn_post_fn, [sv['o_raw'], sv['z']], sh['gdn_post'], [d_gdn], f"gdn_post_bwd{l}")
    gw['gdn_norm'] = jnp.sum(d_ng.reshape(HEADS, HEAD_DIM), axis=0)
    dgd = core_bwd(gdn_heads, list(sv['gd']), sv['gs0'], sv['gt'], do_raw, f"gdn_core_bwd{l}")
    res = rowwise_bwd(gdn_pre_fn, sv['gdn_rows'], sh['gdn_pre'], list(dgd), f"gdn_pre_bwd{l}")
    dqkv = res[0] + unshift_rows(res[1], 1) + unshift_rows(res[2], 2) + unshift_rows(res[3], 3)
    dab = res[4]
    gw['gdn_conv_w'] = jnp.concatenate(res[5:9], axis=0)
    gw['gdn_a_log'], gw['gdn_dt_bias'] = res[9][0, :HEADS], res[10][0, :HEADS]

    dp = jnp.concatenate([dlx, dly, dprw, dqkv, dz, dab], axis=1)
    return dp, gw, dv_first_out


IN_SHARD = D_IN // N_CHIPS
IN_SHARD_PAD = D_IN_PAD // N_CHIPS


def _cols_to_chips(g, n=N_CHIPS):
    r = g.shape[0]
    return jnp.transpose(g.reshape(r, n, -1), (1, 0, 2))


def _cols_from_chips(g):
    return jnp.transpose(g, (1, 0, 2)).reshape(g.shape[1], -1)


def _w_in_from_chips(g):
    nat = _cols_from_chips(g[:, :, :IN_SHARD])
    return jnp.pad(nat, ((0, 0), (0, D_IN_PAD - D_IN)))


def _w_in_to_chips(g):
    return jnp.pad(_cols_to_chips(g[:, :D_IN]), ((0, 0), (0, 0), (0, IN_SHARD_PAD - IN_SHARD)))


def _natural(name, g):
    if name == 'w_in':
        return _w_in_from_chips(g)
    if BIG[name] == 2:
        return _cols_from_chips(g)
    return g.reshape(-1, g.shape[2])


def local_step(x, target, w, wb, shards=None):
    def hosted(keys):
        return [shards[k] for k in keys] if shards is not None else []

    def arrived(keys, gathered):
        for (name, layer), g in zip(keys if shards is not None else [], gathered):
            wb[name][layer] = _natural(name, g)

    saved = []
    v_first = None
    for l in range(N_LAYERS):
        sh = _layer_shared(w, l)
        more = l + 1 < N_LAYERS
        in_ffn1 = [('w_in', l)] + ([('w_out', l)] if more else [])
        in_rwkv = [('ffn2_wi', l)] + ([('ffn2_wo', l)] if more else [])
        in_gdn = [('ffn1_wi', l + 1)] if more else [('ffn2_wo', l), ('w_out', l)]
        in_ffn2 = [('ffn1_wo', l + 1)] if more else []
        x1, got = ffn_fwd(x, _row(w['ffn1_norm'][l]), wb['ffn1_wi'][l], wb['ffn1_wo'][l], f"ffn1_fwd{l}",
                          hosted(in_ffn1))
        arrived(in_ffn1, got)
        p = proj_fwd(x1, _row(w['mix_norm'][l]), wb['w_in'][l], f"proj_fwd{l}")
        mixed, sv, v0, got_rwkv, got_gdn = _mixer_fwd(p, sh, l, v_first, hosted(in_rwkv), hosted(in_gdn))
        arrived(in_rwkv, got_rwkv)
        arrived(in_gdn, got_gdn)
        if l == 0:
            v_first = v0
        x2 = out_fwd(mixed, wb['w_out'][l], x1, f"out_fwd{l}")
        x3, got = ffn_fwd(x2, _row(w['ffn2_norm'][l]), wb['ffn2_wi'][l], wb['ffn2_wo'][l], f"ffn2_fwd{l}",
                          hosted(in_ffn2))
        arrived(in_ffn2, got)
        saved.append(dict(x0=x, x1=x1, x2=x2, mixed=mixed, sv=sv, sh=sh))
        x = x3

    loss, dx, dgf = loss_head(x, _row(w['final_norm']), target, "loss_head")
    per_layer = [dict() for _ in range(N_LAYERS)]
    dv_first = jnp.zeros((x.shape[0], MIX_W), F32)

    waiting, chip_sums, arrived_parts = [], {}, {}

    def reduce_now(keys, tag):
        if shards is None:
            return
        sums = chip_sums_of([(n, k, per_layer[k][n]) for n, k in keys], lax.axis_index("c"), tag)
        for key, (total, total_bf) in zip(keys, sums):
            chip_sums[key] = total
            waiting.append((key, total_bf))

    def take_waiting():
        keys, parts = [k for k, _ in waiting], [p for _, p in waiting]
        waiting.clear()
        return keys, parts

    for l in reversed(range(N_LAYERS)):
        s = saved[l]
        gw = per_layer[l]
        keys, parts = take_waiting()
        dx, dg2, dwg, dwu, dwo, got = ffn_bwd(s['x2'], dx, _row(w['ffn2_norm'][l]), wb['ffn2_wi'][l],
                                              wb['ffn2_wo'][l], f"ffn2_bwd{l}", parts)
        arrived_parts.update(zip(keys, got))
        wi_parts = lambda dwg, dwu: (dwg, dwu)
        row_parts = lambda dw: dw.reshape(N_CHIPS, -1, dw.shape[1])
        gw['ffn2_norm'], gw['ffn2_wi'], gw['ffn2_wo'] = dg2[0], wi_parts(dwg, dwu), row_parts(dwo)
        if l == N_LAYERS - 1:
            reduce_now([('ffn2_wi', l), ('ffn2_wo', l)], f"ffn2_{l}")
        dmixed, dw_out = out_bwd(s['mixed'], wb['w_out'][l], dx, f"out_bwd{l}")
        gw['w_out'] = row_parts(dw_out)
        dp, gmix, dvf = _mixer_bwd(dmixed, s['sv'], s['sh'], l, dv_first)
        if l > 0:
            dv_first = dvf
        gw.update(gmix)
        dx, dgm, dwin = proj_bwd(s['x1'], dx, _row(w['mix_norm'][l]), wb['w_in'][l], dp, f"proj_bwd{l}")
        gw['mix_norm'], gw['w_in'] = dgm[0], _w_in_to_chips(dwin)
        if l < N_LAYERS - 1:
            reduce_now([('ffn2_wi', l), ('ffn2_wo', l), ('w_in', l), ('w_out', l)], f"mix_{l}")
        keys, parts = take_waiting()
        dx, dg1, dwg, dwu, dwo, got = ffn_bwd(s['x0'], dx, _row(w['ffn1_norm'][l]), wb['ffn1_wi'][l],
                                              wb['ffn1_wo'][l], f"ffn1_bwd{l}", parts)
        arrived_parts.update(zip(keys, got))
        gw['ffn1_norm'], gw['ffn1_wi'], gw['ffn1_wo'] = dg1[0], wi_parts(dwg, dwu), row_parts(dwo)
        if l == N_LAYERS - 1:
            reduce_now([('w_in', l), ('w_out', l), ('ffn1_wi', l), ('ffn1_wo', l)], f"ffn1_{l}")
        else:
            reduce_now([('ffn1_wi', l), ('ffn1_wo', l)], f"ffn1_{l}")
    grads = {'final_norm': dgf[0]}
    if shards is not None:
        grads['last_round'] = take_waiting()
        arrived_parts.update({key: None for key in grads['last_round'][0]})
    for name in WEIGHTS:
        if name == 'final_norm':
            continue
        if name in BIG:
            if shards is None:
                grads[name] = [per_layer[l][name] for l in range(N_LAYERS)]
            else:
                grads[name] = [(chip_sums[(name, l)], arrived_parts[(name, l)]) for l in range(N_LAYERS)]
        elif name.startswith('rwkv_vres'):
            grads[name] = per_layer[1][name][None]
        else:
            grads[name] = jnp.stack([per_layer[l][name] for l in range(N_LAYERS)])
    return loss[0, 0], dx, grads


ANY = pl.BlockSpec(memory_space=pl.ANY)


def _coords():
    return lax.axis_index("x"), lax.axis_index("y"), lax.axis_index("c")


def _other_chips(x, y):
    return [((x + 1) % 2, y), (x, (y + 1) % 2), ((x + 1) % 2, (y + 1) % 2)]


def allreduce_small(pack, name):
    R = pack.shape[0]
    rh = R // 2

    def body(x_ref, o_ref, sib_ref, chip_ref, parts_ref, send_sems, recv_sems):
        x, y, c = _coords()
        sib = (x, y, 1 - c)

        def copy(k, src, dst, to):
            return pltpu.make_async_remote_copy(src_ref=src, dst_ref=dst, send_sem=send_sems.at[k],
                                                recv_sem=recv_sems.at[k], device_id=to, device_id_type=MESH)

        swap = copy(0, x_ref, sib_ref, sib)
        swap.start()
        swap.wait()
        chip_ref[...] = jnp.where(c == 0, x_ref[...], sib_ref[...]) + jnp.where(c == 0, sib_ref[...], x_ref[...])

        mine = pl.ds(pl.multiple_of(c * rh, 8), rh)
        sends = [copy(1 + j, chip_ref.at[mine], parts_ref.at[j], (px, py, c))
                 for j, (px, py) in enumerate(_other_chips(x, y))]
        for cp in sends:
            cp.start()
        for cp in sends:
            cp.wait()
        s = 2 * x + y
        own = chip_ref[mine, :]
        from_chip = {2: parts_ref[0], 1: parts_ref[1], 3: parts_ref[2]}
        terms = []
        for k in range(N_CHIPS):
            t = own
            for d, part in from_chip.items():
                t = jnp.where(jnp.bitwise_xor(s, d) == k, part, t)
            terms.append(t)
        o_ref[mine, :] = ((terms[0] + terms[1]) + terms[2]) + terms[3]

        share = copy(4, o_ref.at[mine], o_ref.at[mine], sib)
        share.start()
        share.wait()

    vm = pl.BlockSpec(memory_space=pltpu.VMEM)
    return pl.pallas_call(
        body, name=name, in_specs=[vm], out_specs=vm, out_shape=jax.ShapeDtypeStruct((R, 128), F32),
        scratch_shapes=[pltpu.VMEM((R, 128), F32), pltpu.VMEM((R, 128), F32), pltpu.VMEM((3, rh, 128), F32),
                        pltpu.SemaphoreType.DMA((5,)), pltpu.SemaphoreType.DMA((5,))],
        compiler_params=_params())(pack)


class ChipGather:
    def __init__(self, shards):
        self.shapes = [s.shape for s in shards]
        self.n = len(shards)
        self.in_specs = [ANY] * self.n
        self.out_specs = [ANY] * self.n
        self.out_shape = [jax.ShapeDtypeStruct((N_CHIPS,) + s.shape, s.dtype) for s in shards]
        self.scratch = [pltpu.SemaphoreType.DMA((6 * self.n,)), pltpu.SemaphoreType.DMA((6 * self.n,)),
                        pltpu.SemaphoreType.DMA((self.n,))] if self.n else []

    def _rows(self, a, core):
        rh = self.shapes[a][0] // 2
        return pl.ds(pl.multiple_of(core * rh, 16), rh)

    def _copies(self, kind, x_refs, o_refs, sems):
        send_sems, recv_sems, local_sems = sems
        x, y, c = _coords()
        s_me = 2 * x + y
        sib = (x, y, 1 - c)

        def copy(a, k, src, dst, to):
            return pltpu.make_async_remote_copy(src_ref=src, dst_ref=dst, send_sem=send_sems.at[6 * a + k],
                                                recv_sem=recv_sems.at[6 * a + k], device_id=to, device_id_type=MESH)

        if kind == 'own':
            return [pltpu.make_async_copy(x_refs[a], o_refs[a].at[s_me], local_sems.at[a]) for a in range(self.n)]
        out = []
        for j, (px, py) in enumerate(_other_chips(x, y)):
            for a in range(self.n):
                mine = self._rows(a, c)
                part = o_refs[a].at[2 * px + py, mine]
                if kind == 'sent':
                    out.append(copy(a, j, x_refs[a].at[mine], o_refs[a].at[s_me, mine], (px, py, c)))
                elif kind == 'arrived':
                    out.append(copy(a, j, part, part, (px, py, c)))
                elif kind == 'passed':
                    out.append(copy(a, 3 + j, part, part, sib))
                else:
                    theirs = o_refs[a].at[2 * px + py, self._rows(a, 1 - c)]
                    out.append(copy(a, 3 + j, theirs, theirs, sib))
        return out

    def start(self, x_refs, o_refs, sems):
        if not self.n:
            return
        for cp in self._copies('own', x_refs, o_refs, sems) + self._copies('sent', x_refs, o_refs, sems):
            cp.start()

    def relay(self, x_refs, o_refs, sems):
        if not self.n:
            return
        for got, fw in zip(self._copies('arrived', x_refs, o_refs, sems),
                           self._copies('passed', x_refs, o_refs, sems)):
            got.wait_recv()
            fw.start()

    def finish(self, x_refs, o_refs, sems):
        if not self.n:
            return
        for cp in self._copies('from_sibling', x_refs, o_refs, sems):
            cp.wait_recv()
        for cp in self._copies('sent', x_refs, o_refs, sems) + self._copies('passed', x_refs, o_refs, sems):
            cp.wait_send()
        for cp in self._copies('own', x_refs, o_refs, sems):
            cp.wait()


def allgather_chips(shards, name):
    gather = ChipGather(shards)
    n = gather.n

    def body(*refs):
        x_refs, o_refs, sems = refs[:n], refs[n:2 * n], refs[2 * n:]
        gather.start(x_refs, o_refs, sems)
        gather.relay(x_refs, o_refs, sems)
        gather.finish(x_refs, o_refs, sems)

    return pl.pallas_call(
        body, name=name, in_specs=gather.in_specs, out_specs=tuple(gather.out_specs),
        out_shape=tuple(gather.out_shape), scratch_shapes=gather.scratch, compiler_params=_params())(*shards)


def sibling_swap(srcs, halves, name):
    n = len(srcs)
    row_axis = [s.ndim - 2 for s in srcs]
    out_shapes = [s.shape[:ax] + (s.shape[ax] // 2,) + s.shape[ax + 1:] if halves else s.shape
                  for s, ax in zip(srcs, row_axis)]

    def body(*refs):
        x_refs, o_refs = refs[:n], refs[n:2 * n]
        send_sems, recv_sems = refs[2 * n:]
        x, y, c = _coords()
        copies = []
        for a in range(n):
            part = x_refs[a]
            if halves:
                rh = srcs[a].shape[row_axis[a]] // 2
                theirs = pl.ds(pl.multiple_of((1 - c) * rh, 16), rh)
                part = part.at[:, theirs] if row_axis[a] == 1 else part.at[theirs]
            cp = pltpu.make_async_remote_copy(src_ref=part, dst_ref=o_refs[a], send_sem=send_sems.at[a],
                                              recv_sem=recv_sems.at[a], device_id=(x, y, 1 - c), device_id_type=MESH)
            cp.start()
            copies.append(cp)
        for cp in copies:
            cp.wait()

    return pl.pallas_call(
        body, name=name, in_specs=[ANY] * n, out_specs=tuple([ANY] * n),
        out_shape=tuple(jax.ShapeDtypeStruct(sh, s.dtype) for sh, s in zip(out_shapes, srcs)),
        scratch_shapes=[pltpu.SemaphoreType.DMA((n,)), pltpu.SemaphoreType.DMA((n,))],
        compiler_params=_params())(*srcs)


class ChipScatter:
    def __init__(self, parts):
        self.n = len(parts)
        self.in_specs = [ANY] * self.n
        self.out_specs = [ANY] * self.n
        self.out_shape = [jax.ShapeDtypeStruct((3,) + p.shape[1:], p.dtype) for p in parts]
        self.scratch = [pltpu.SemaphoreType.DMA((3 * self.n,)), pltpu.SemaphoreType.DMA((3 * self.n,))] if self.n else []

    def _copies(self, x_refs, o_refs, sems):
        send_sems, recv_sems = sems
        x, y, c = _coords()
        return [pltpu.make_async_remote_copy(src_ref=x_refs[a].at[2 * px + py], dst_ref=o_refs[a].at[j],
                                             send_sem=send_sems.at[3 * a + j], recv_sem=recv_sems.at[3 * a + j],
                                             device_id=(px, py, c), device_id_type=MESH)
                for j, (px, py) in enumerate(_other_chips(x, y)) for a in range(self.n)]

    def start(self, x_refs, o_refs, sems):
        if self.n:
            for cp in self._copies(x_refs, o_refs, sems):
                cp.start()

    def finish(self, x_refs, o_refs, sems):
        if self.n:
            for cp in self._copies(x_refs, o_refs, sems):
                cp.wait()


HBM = pl.BlockSpec(memory_space=pltpu.HBM)
SEM = pl.BlockSpec(memory_space=pltpu.SEMAPHORE)
SIDE_EFFECT = pltpu.SideEffectType.DATAFLOW_SIDE_EFFECTING


def _scatter_copies(x_refs, land_refs, send_sems, recv_sems):
    x, y, c = _coords()
    n = len(x_refs)
    return [pltpu.make_async_remote_copy(src_ref=x_refs[a].at[2 * px + py], dst_ref=land_refs[a].at[j],
                                         send_sem=send_sems[3 * a + j], recv_sem=recv_sems[3 * a + j],
                                         device_id=(px, py, c), device_id_type=MESH)
            for j, (px, py) in enumerate(_other_chips(x, y)) for a in range(n)]


def scatter_start(parts, name):
    n = len(parts)
    k = 3 * n
    lands = [lax.empty((3,) + p.shape[1:], p.dtype) for p in parts]

    def body(*refs):
        x_refs, land_refs = refs[:n], refs[n:2 * n]
        send_sems, recv_sems = refs[2 * n:2 * n + k], refs[2 * n + k:2 * n + 2 * k]
        token = refs[-1]
        for cp in _scatter_copies(x_refs, land_refs, send_sems, recv_sems):
            cp.start()
        token[...] = jnp.zeros_like(token)

    hbm = lambda a: pltpu.HBM(a.shape, a.dtype)
    res = pl.pallas_call(
        body, name=name, in_specs=[HBM] * (2 * n),
        out_specs=tuple([SEM] * (2 * k) + [HBM] * (2 * n) + [pl.BlockSpec(memory_space=pltpu.VMEM)]),
        out_shape=tuple([pltpu.SemaphoreType.DMA(())] * (2 * k) + [hbm(p) for p in parts] + [hbm(b) for b in lands]
                        + [jax.ShapeDtypeStruct((8, 128), F32)]),
        input_output_aliases={i: 2 * k + i for i in range(2 * n)},
        compiler_params=pltpu.CompilerParams(has_side_effects=SIDE_EFFECT, vmem_limit_bytes=VMEM_LIMIT))(
            *[pltpu.with_memory_space_constraint(a, pltpu.HBM) for a in list(parts) + lands])
    return list(res[:2 * k]), list(res[2 * k:2 * k + n]), list(res[2 * k + n:2 * k + 2 * n]), res[-1]


def scatter_wait(sems, parts_thru, lands_thru, after, name):
    n = len(parts_thru)
    k = 3 * n

    def body(*refs):
        x_refs, land_refs = refs[:n], refs[n:2 * n]
        send_sems, recv_sems = refs[2 * n:2 * n + k], refs[2 * n + k:2 * n + 2 * k]
        for cp in _scatter_copies(x_refs, land_refs, send_sems, recv_sems):
            cp.wait_send()
            cp.wait_recv()

    hbm = lambda a: pltpu.HBM(a.shape, a.dtype)
    res = pl.pallas_call(
        body, name=name, in_specs=[HBM] * (2 * n) + [SEM] * (2 * k) + [ANY],
        out_specs=tuple([HBM] * (2 * n)), out_shape=tuple(hbm(a) for a in list(parts_thru) + list(lands_thru)),
        input_output_aliases={i: i for i in range(2 * n)},
        compiler_params=pltpu.CompilerParams(has_side_effects=SIDE_EFFECT, vmem_limit_bytes=VMEM_LIMIT))(
            *parts_thru, *lands_thru, *sems, after)
    return list(res[n:])


def _row_block(rows):
    return max(b for b in range(16, 257, 16) if rows % b == 0)


def chip_sum(gpack, recv, core, name):
    n, R, W = gpack.shape
    rh = R // 2
    rb = _row_block(rh)
    nb = rh // rb

    def body(c_ref, g_ref, r_ref, o_ref, ob_ref):
        s = g_ref[...] + r_ref[...]
        o_ref[...] = s
        ob_ref[...] = s.astype(BF16)

    blk = pl.BlockSpec((1, rb, W), lambda i, j, c_ref: (i, j, 0))
    spec = pltpu.PrefetchScalarGridSpec(
        num_scalar_prefetch=1, grid=(n, nb),
        in_specs=[pl.BlockSpec((1, rb, W), lambda i, j, c_ref: (i, c_ref[0] * nb + j, 0)), blk],
        out_specs=(blk, blk))
    return pl.pallas_call(
        body, name=name, grid_spec=spec,
        out_shape=(jax.ShapeDtypeStruct((n, rh, W), F32), jax.ShapeDtypeStruct((n, rh, W), BF16)),
        compiler_params=_params(("arbitrary", "arbitrary")))(core, gpack, recv)


def chip_sum_cols(gate, up, recv_gate, recv_up, core, name):
    R, W = gate.shape
    cw = W // 2
    rh = R // 2
    rb = _row_block(rh)
    nb = rh // rb

    def body(c_ref, g_ref, u_ref, rg_ref, ru_ref, o_ref, ob_ref):
        s = jnp.where(pl.program_id(0) < 2, g_ref[...] + rg_ref[...], u_ref[...] + ru_ref[...])
        o_ref[0] = s
        ob_ref[0] = s.astype(BF16)

    gate_blk = lambda s, j: (jnp.where(s < 2, j, nb - 1), jnp.minimum(s, 1))
    up_blk = lambda s, j: (jnp.where(s < 2, 0, j), jnp.maximum(s - 2, 0))
    out = pl.BlockSpec((1, rb, cw), lambda s, j, c_ref: (s, j, 0))

    def own(blk):
        return lambda s, j, c_ref: (c_ref[0] * nb + blk(s, j)[0], blk(s, j)[1])

    def theirs(blk):
        return lambda s, j, c_ref: blk(s, j)

    spec = pltpu.PrefetchScalarGridSpec(
        num_scalar_prefetch=1, grid=(N_CHIPS, nb),
        in_specs=[pl.BlockSpec((rb, cw), own(gate_blk)), pl.BlockSpec((rb, cw), own(up_blk)),
                  pl.BlockSpec((rb, cw), theirs(gate_blk)), pl.BlockSpec((rb, cw), theirs(up_blk))],
        out_specs=(out, out))
    return pl.pallas_call(
        body, name=name, grid_spec=spec,
        out_shape=(jax.ShapeDtypeStruct((N_CHIPS, rh, cw), F32), jax.ShapeDtypeStruct((N_CHIPS, rh, cw), BF16)),
        compiler_params=_params(("arbitrary", "arbitrary")))(core, gate, up, recv_gate, recv_up)


def chip_sums_of(items, core, tag):
    parts = []
    for _, _, g in items:
        parts += list(g) if isinstance(g, tuple) else [g]
    swapped = iter(zip(parts, sibling_swap(parts, True, f"grad_swap_cores_{tag}")))
    core_arg = core.reshape(1).astype(jnp.int32)
    sums = []
    for n, l, g in items:
        if isinstance(g, tuple):
            (dwg, from_g), (dwu, from_u) = next(swapped), next(swapped)
            sums.append(chip_sum_cols(dwg, dwu, from_g, from_u, core_arg, f"grad_chip_sum_{n}{l}"))
        else:
            p, r = next(swapped)
            sums.append(chip_sum(p, r, core_arg, f"grad_chip_sum_{n}{l}"))
    return sums


def shard_sum(own, recv, name):
    R, W = own.shape
    rb = _row_block(R)

    def body(a_ref, r_ref, o_ref):
        acc = a_ref[...]
        for j in range(3):
            acc = acc + r_ref[j].astype(F32)
        o_ref[...] = acc

    return pl.pallas_call(
        body, name=name, grid=(R // rb,),
        in_specs=[pl.BlockSpec((rb, W), lambda i: (i, 0)), pl.BlockSpec((3, rb, W), lambda i: (0, i, 0))],
        out_specs=pl.BlockSpec((rb, W), lambda i: (i, 0)), out_shape=jax.ShapeDtypeStruct((R, W), F32),
        compiler_params=_params(("arbitrary",)))(own, recv)


def adamw(w, m, v, g, name):
    L, R, C = w.shape
    rb = max(b for b in range(8, 257, 8) if R % b == 0)
    bc1 = 1.0 - ADAM_B1 ** ADAM_STEP
    bc2 = 1.0 - ADAM_B2 ** ADAM_STEP

    def body(w_ref, m_ref, v_ref, g_ref, d_ref, nm_ref, nv_ref):
        gv = g_ref[...]
        nm = ADAM_B1 * m_ref[...] + (1.0 - ADAM_B1) * gv
        nv = ADAM_B2 * v_ref[...] + (1.0 - ADAM_B2) * (gv * gv)
        d_ref[...] = -ADAM_LR * ((nm / bc1) / (jnp.sqrt(nv / bc2) + ADAM_EPS) + ADAM_WD * w_ref[...])
        nm_ref[...] = nm
        nv_ref[...] = nv

    blk = pl.BlockSpec((1, rb, C), lambda l, i: (l, i, 0))
    sh = jax.ShapeDtypeStruct((L, R, C), F32)
    return pl.pallas_call(body, name=name, grid=(L, R // rb), in_specs=[blk] * 4, out_specs=(blk,) * 3,
                          out_shape=(sh, sh, sh), compiler_params=_params(("arbitrary", "arbitrary")))(w, m, v, g)


SMALL = [n for n in WEIGHTS if n not in BIG]


PACK_TILE = 8 * 128


def _pack(arrays):
    blocks = []
    for a in arrays:
        flat = a.reshape(-1)
        flat = jnp.pad(flat, (0, -flat.size % PACK_TILE))
        blocks.append(flat.reshape(-1, 128))
    rows = sum(b.shape[0] for b in blocks)
    if rows % 16:
        blocks.append(jnp.zeros((8, 128), arrays[0].dtype))
    return jnp.concatenate(blocks, axis=0)


def _unpack(pack, shapes):
    out, row = [], 0
    for shape in shapes:
        size = int(np.prod(shape))
        rows = -(-size // PACK_TILE) * 8
        out.append(pack[row:row + rows].reshape(-1)[:size].reshape(shape))
        row += rows
    return out


def _pad_lanes(a):
    return jnp.pad(a, ((0, 0), (0, -a.shape[1] % 128)))


def _local_shard(full, axis, chip):
    size = full.shape[axis] // N_CHIPS
    return lax.dynamic_slice_in_dim(full, chip * size, size, axis)


def kernel(x, ffn1_norm, ffn1_wi, ffn1_wo, mix_norm, w_in, w_out, lru_conv_w, lru_conv_b, lru_gate_a_w, lru_gate_a_b, lru_gate_x_w, lru_gate_x_b, lru_lambda, lru_out_norm, rwkv_mu, rwkv_w_up, rwkv_w_bias, rwkv_a_up, rwkv_a_bias, rwkv_g_up, rwkv_k_k, rwkv_k_a, rwkv_r_k, rwkv_ln_g, rwkv_ln_b, rwkv_vres_w1, rwkv_vres_w2, rwkv_vres_b, gdn_conv_w, gdn_a_log, gdn_dt_bias, gdn_norm, ffn2_norm, ffn2_wi, ffn2_wo, final_norm, loss_target, m_ffn1_norm, m_ffn1_wi, m_ffn1_wo, m_mix_norm, m_w_in, m_w_out, m_lru_conv_w, m_lru_conv_b, m_lru_gate_a_w, m_lru_gate_a_b, m_lru_gate_x_w, m_lru_gate_x_b, m_lru_lambda, m_lru_out_norm, m_rwkv_mu, m_rwkv_w_up, m_rwkv_w_bias, m_rwkv_a_up, m_rwkv_a_bias, m_rwkv_g_up, m_rwkv_k_k, m_rwkv_k_a, m_rwkv_r_k, m_rwkv_ln_g, m_rwkv_ln_b, m_rwkv_vres_w1, m_rwkv_vres_w2, m_rwkv_vres_b, m_gdn_conv_w, m_gdn_a_log, m_gdn_dt_bias, m_gdn_norm, m_ffn2_norm, m_ffn2_wi, m_ffn2_wo, m_final_norm, v_ffn1_norm, v_ffn1_wi, v_ffn1_wo, v_mix_norm, v_w_in, v_w_out, v_lru_conv_w, v_lru_conv_b, v_lru_gate_a_w, v_lru_gate_a_b, v_lru_gate_x_w, v_lru_gate_x_b, v_lru_lambda, v_lru_out_norm, v_rwkv_mu, v_rwkv_w_up, v_rwkv_w_bias, v_rwkv_a_up, v_rwkv_a_bias, v_rwkv_g_up, v_rwkv_k_k, v_rwkv_k_a, v_rwkv_r_k, v_rwkv_ln_g, v_rwkv_ln_b, v_rwkv_vres_w1, v_rwkv_vres_w2, v_rwkv_vres_b, v_gdn_conv_w, v_gdn_a_log, v_gdn_dt_bias, v_gdn_norm, v_ffn2_norm, v_ffn2_wi, v_ffn2_wo, v_final_norm):
    args = locals()
    w_loc = {n: args[n] for n in WEIGHTS}
    m_loc = {n: args['m_' + n] for n in WEIGHTS}
    v_loc = {n: args['v_' + n] for n in WEIGHTS}
    chip = 2 * lax.axis_index("x") + lax.axis_index("y")
    core = lax.axis_index("c")

    big = [(n, l) for n in BIG for l in range(N_LAYERS)]
    shards = {(n, l): _pad_lanes(w_loc[n][l].astype(BF16)) for n, l in big}
    first = [('ffn1_wi', 0), ('ffn1_wo', 0)]
    wb = {n: [None] * N_LAYERS for n in BIG}
    for (n, l), g in zip(first, allgather_chips([shards[k] for k in first], "allgather_first")):
        wb[n][l] = _natural(n, g)

    sm_names = list(SMALL_SHARDED)
    placed = []
    for n in sm_names:
        mine = [jnp.where((chip == s) & (core == 0), w_loc[n], 0.0) for s in range(N_CHIPS)]
        placed.append(jnp.concatenate(mine, axis=SMALL_SHARDED[n]))
    summed = allreduce_small(_pack(placed), "allgather_small")
    w_full = dict(w_loc)
    w_full.update(zip(sm_names, _unpack(summed, [p.shape for p in placed])))

    loss, dx, grads = local_step(x[0], loss_target[0], w_full, wb, shards)
    loss = lax.psum(loss, ("x", "y", "c"))

    gsum = allreduce_small(_pack([grads[n] for n in SMALL]), "allreduce_small")
    g_loc = {}
    for n, g in zip(SMALL, _unpack(gsum, [grads[n].shape for n in SMALL])):
        g_loc[n] = _local_shard(g, SMALL_SHARDED[n], chip) if n in SMALL_SHARDED else g

    last_keys, last_parts = grads['last_round']
    sems, parts_thru, lands_thru, token = scatter_start(last_parts, "grad_scatter_last_start")
    chip_after_start = chip + token[0, 0].astype(chip.dtype)
    rows = {n: [None] * N_LAYERS for n in BIG}
    delta, new_m, new_v = {}, {}, {}

    def finish(keys, arrived, which_chip, tag):
        halves = [shard_sum(lax.dynamic_index_in_dim(grads[n][l][0], which_chip, 0, keepdims=False), got,
                            f"grad_shard_sum_{n}{l}") for (n, l), got in zip(keys, arrived)]
        others = sibling_swap(halves, False, f"grad_share_cores_{tag}")
        for (n, l), half, other in zip(keys, halves, others):
            lower = jnp.where(core == 0, half, other)
            upper = jnp.where(core == 0, other, half)
            rows[n][l] = jnp.concatenate([lower, upper], axis=0)[:, :w_loc[n].shape[-1]]
        for n in BIG:
            if n not in delta and all(r is not None for r in rows[n]):
                g_loc[n] = jnp.stack(rows[n])
                delta[n], new_m[n], new_v[n] = adamw(w_loc[n], m_loc[n], v_loc[n], g_loc[n], f"adamw_{n}")

    early = [k for k in big if k not in last_keys]
    finish(early, [grads[n][l][1] for n, l in early], chip_after_start, "early")
    pack = lambda d: _pack([d[n] for n in SMALL])[None]
    res = adamw(pack(w_loc), pack(m_loc), pack(v_loc), pack(g_loc), "adamw_small")
    for dst, r in zip((delta, new_m, new_v), res):
        dst.update(zip(SMALL, _unpack(r[0], [w_loc[n].shape for n in SMALL])))
    arrived_last = scatter_wait(sems, parts_thru, lands_thru, delta['ffn2_wi'], "grad_scatter_last_wait")
    finish(last_keys, arrived_last, chip, "last")

    return (loss, dx[None], *[g_loc[n] for n in WEIGHTS], *[delta[n] for n in WEIGHTS],
            *[new_m[n] for n in WEIGHTS], *[new_v[n] for n in WEIGHTS])
```

```python
import functools

import numpy as np
import jax
import jax.numpy as jnp
from jax import lax
from jax.experimental import pallas as pl
from jax.experimental.pallas import tpu as pltpu

F32 = jnp.float32
BF16 = jnp.bfloat16
MESH = pl.DeviceIdType.MESH

D_MODEL = 1024
D_FF = 2816
N_LAYERS = 2
HEADS = 6
HEAD_DIM = 64
MIX_W = HEADS * HEAD_DIM
LRU_W = 256
LRU_BLOCKS = 4
D_IN = 3468
D_IN_PAD = 3584
NORM_EPS = 1e-6
GN_EPS = 64e-5
LRU_C = 8.0
CHUNK = 64
CHUNKS_PER_STEP = 4
ROWS = 512
FF_CHUNK = 256
IN_CHUNK = 512
VMEM_LIMIT = 56 * 1024 * 1024

ADAM_LR, ADAM_B1, ADAM_B2, ADAM_EPS, ADAM_WD, ADAM_STEP = 0.001, 0.9, 0.999, 1e-08, 0.01, 10

WEIGHTS = ['ffn1_norm', 'ffn1_wi', 'ffn1_wo', 'mix_norm', 'w_in', 'w_out', 'lru_conv_w', 'lru_conv_b',
           'lru_gate_a_w', 'lru_gate_a_b', 'lru_gate_x_w', 'lru_gate_x_b', 'lru_lambda', 'lru_out_norm',
           'rwkv_mu', 'rwkv_w_up', 'rwkv_w_bias', 'rwkv_a_up', 'rwkv_a_bias', 'rwkv_g_up', 'rwkv_k_k',
           'rwkv_k_a', 'rwkv_r_k', 'rwkv_ln_g', 'rwkv_ln_b', 'rwkv_vres_w1', 'rwkv_vres_w2', 'rwkv_vres_b',
           'gdn_conv_w', 'gdn_a_log', 'gdn_dt_bias', 'gdn_norm', 'ffn2_norm', 'ffn2_wi', 'ffn2_wo', 'final_norm']
BIG = {'ffn1_wi': 2, 'ffn1_wo': 1, 'w_in': 2, 'w_out': 1, 'ffn2_wi': 2, 'ffn2_wo': 1}
SMALL_SHARDED = {'lru_conv_w': 2, 'rwkv_w_up': 2, 'rwkv_a_up': 2, 'rwkv_g_up': 2, 'rwkv_vres_w1': 1,
                 'rwkv_vres_w2': 2, 'gdn_conv_w': 2}
N_CHIPS = 4


def _params(sem=None):
    kw = dict(vmem_limit_bytes=VMEM_LIMIT)
    if sem is not None:
        kw['dimension_semantics'] = sem
    return pltpu.CompilerParams(**kw)


def _bdot(a, b, dims=(((1,), (0,)), ((), ()))):
    return lax.dot_general(a.astype(BF16), b.astype(BF16), dims, preferred_element_type=F32)


def _bdot_nt(a, b):
    return _bdot(a, b, (((1,), (1,)), ((), ())))


def _bdot_tn(a, b):
    return _bdot(a, b, (((0,), (0,)), ((), ())))


_DIMS = {'nn': (((1,), (0,)), ((), ())), 'nt': (((1,), (1,)), ((), ())), 'tn': (((0,), (0,)), ((), ()))}


def _split(a, terms):
    parts = []
    for _ in range(terms - 1):
        hi = a.astype(BF16)
        parts.append(hi)
        a = a - hi.astype(F32)
    parts.append(a.astype(BF16))
    return parts


_BATCH_DIMS = {'nn': (((2,), (1,)), ((0,), (0,))), 'nt': (((2,), (2,)), ((0,), (0,))),
               'tn': (((1,), (1,)), ((0,), (0,)))}


def _dot3(a, b, kind):
    ah, al = _split(a, 2)
    bh, bl = _split(b, 2)
    dims = _BATCH_DIMS[kind] if a.ndim == 3 else _DIMS[kind]
    d = lambda p, q: lax.dot_general(p, q, dims, preferred_element_type=F32)
    return d(ah, bh) + (d(ah, bl) + d(al, bh))


@functools.partial(jax.custom_vjp, nondiff_argnums=(2,))
def _cdot_k(a, b, kind):
    return _dot3(a, b, kind)


def _cdot_k_fwd(a, b, kind):
    return _dot3(a, b, kind), (a, b)


def _cdot_k_bwd(kind, res, ct):
    a, b = res
    if kind == 'nn':
        return _dot3(ct, b, 'nt'), _dot3(a, ct, 'tn')
    if kind == 'nt':
        return _dot3(ct, b, 'nn'), _dot3(ct, a, 'tn')
    return _dot3(b, ct, 'nt'), _dot3(a, ct, 'nn')


_cdot_k.defvjp(_cdot_k_fwd, _cdot_k_bwd)


def _dot1(a, b, kind):
    dims = _BATCH_DIMS[kind] if a.ndim == 3 else _DIMS[kind]
    return lax.dot_general(a.astype(BF16), b.astype(BF16), dims, preferred_element_type=F32)


@functools.partial(jax.custom_vjp, nondiff_argnums=(2,))
def _cdot1_k(a, b, kind):
    return _dot1(a, b, kind)


def _cdot1_k_fwd(a, b, kind):
    return _dot1(a, b, kind), (a, b)


def _cdot1_k_bwd(kind, res, ct):
    a, b = res
    if kind == 'nn':
        return _dot1(ct, b, 'nt'), _dot1(a, ct, 'tn')
    if kind == 'nt':
        return _dot1(ct, b, 'nn'), _dot1(ct, a, 'tn')
    return _dot1(b, ct, 'nt'), _dot1(a, ct, 'nn')


_cdot1_k.defvjp(_cdot1_k_fwd, _cdot1_k_bwd)


def _cdot(a, b):
    return _cdot1_k(a, b, 'nn')


def _cdot_nt(a, b):
    return _cdot1_k(a, b, 'nt')


def _cdot_tn(a, b):
    return _cdot1_k(a, b, 'tn')


def _hdot(a, b):
    return _cdot_k(a, b, 'nn')


def _dot_exact(x, m01, kind):
    d = lambda p: lax.dot_general(p, m01.astype(BF16), _DIMS[kind], preferred_element_type=F32)
    hi, mid, lo = _split(x, 3)
    return d(hi) + (d(mid) + d(lo))


@functools.partial(jax.custom_vjp, nondiff_argnums=(1,))
def _xdot(x, make_m):
    return _dot_exact(x, make_m(), 'nn')


def _xdot_fwd(x, make_m):
    return _dot_exact(x, make_m(), 'nn'), None


def _xdot_bwd(make_m, _, ct):
    return (_dot_exact(ct, make_m(), 'nt'),)


_xdot.defvjp(_xdot_fwd, _xdot_bwd)


def _iota2(n, m):
    return lax.broadcasted_iota(jnp.int32, (n, m), 0), lax.broadcasted_iota(jnp.int32, (n, m), 1)


def _head_blocks(w):
    ri, ci = _iota2(w, w)
    return (ri // HEAD_DIM == ci // HEAD_DIM).astype(F32)


def _segsum(x):
    return _xdot(x, functools.partial(_head_blocks, x.shape[-1]))


def _cumsum_rows(x):
    return _cumsum_k(x, x.shape[0])


@functools.partial(jax.custom_vjp, nondiff_argnums=(1,))
def _cumsum_k(x, n):
    return _lower_dot(x, n, False)


def _lower_dot(x, n, transpose):
    ri, ci = _iota2(n, n)
    m = ((ri <= ci) if transpose else (ri >= ci)).astype(BF16)
    d = lambda p: lax.dot_general(m, p, _DIMS['nn'], preferred_element_type=F32)
    hi, mid, lo = _split(x, 3)
    return d(hi) + (d(mid) + d(lo))


def _cumsum_k_fwd(x, n):
    return _lower_dot(x, n, False), None


def _cumsum_k_bwd(n, _, ct):
    return (_lower_dot(ct, n, True),)


_cumsum_k.defvjp(_cumsum_k_fwd, _cumsum_k_bwd)


def _rms(x, g):
    return x * lax.rsqrt(jnp.mean(x * x, axis=-1, keepdims=True) + NORM_EPS) * g


DENSE_ROWS = 1024


def _row_loop(n_rows, fn):
    rows = min(DENSE_ROWS, n_rows)

    def step(i, c):
        fn(pl.ds(pl.multiple_of(i * rows, rows), rows))
        return c
    lax.fori_loop(0, n_rows // rows, step, 0)


def ffn_fwd(x, g, wi, wo, name, hosted=()):
    T = x.shape[0]
    nj = D_FF // FF_CHUNK
    gather = ChipGather(list(hosted))
    n = gather.n

    def body(*refs):
        x_ref, g_ref, wg_ref, wu_ref, wo_ref = refs[:5]
        hx, o_ref, ho = refs[5:5 + n], refs[5 + n], refs[6 + n:6 + 2 * n]
        h_ref, acc_ref = refs[6 + 2 * n:8 + 2 * n]
        sems = refs[8 + 2 * n:]
        j = pl.program_id(0)

        @pl.when(j == 0)
        def _():
            gather.start(hx, ho, sems)

            def init(r):
                h_ref[r, :] = _rms(x_ref[r, :], g_ref[...]).astype(BF16)
                acc_ref[r, :] = jnp.zeros((r.size, D_MODEL), F32)
            _row_loop(T, init)

        def blk(r):
            hb = h_ref[r, :]
            gate = jnp.dot(hb, wg_ref[...], preferred_element_type=F32)
            up = jnp.dot(hb, wu_ref[...], preferred_element_type=F32)
            a = (gate * jax.nn.sigmoid(gate) * up).astype(BF16)
            acc_ref[r, :] += jnp.dot(a, wo_ref[...], preferred_element_type=F32)
        _row_loop(T, blk)

        @pl.when(j == nj - 2)
        def _():
            gather.relay(hx, ho, sems)

        @pl.when(j == nj - 1)
        def _():
            def fin(r):
                o_ref[r, :] = x_ref[r, :] + 0.5 * acc_ref[r, :]
            _row_loop(T, fin)
            gather.finish(hx, ho, sems)

    full = pl.BlockSpec((T, D_MODEL), lambda j: (0, 0))
    res = pl.pallas_call(
        body, name=name, grid=(nj,),
        in_specs=[full, pl.BlockSpec((1, D_MODEL), lambda j: (0, 0)),
                  pl.BlockSpec((D_MODEL, FF_CHUNK), lambda j: (0, j)),
                  pl.BlockSpec((D_MODEL, FF_CHUNK), lambda j: (0, j + nj)),
                  pl.BlockSpec((FF_CHUNK, D_MODEL), lambda j: (j, 0))] + gather.in_specs,
        out_specs=tuple([full] + gather.out_specs),
        out_shape=tuple([jax.ShapeDtypeStruct((T, D_MODEL), F32)] + gather.out_shape),
        scratch_shapes=[pltpu.VMEM((T, D_MODEL), BF16), pltpu.VMEM((T, D_MODEL), F32)] + gather.scratch,
        compiler_params=_params(("arbitrary",)))(x, g, wi, wi, wo, *hosted)
    return res[0], list(res[1:])


def _norm_bwd_rows(x, g, dh, dres):
    rstd = lax.rsqrt(jnp.mean(x * x, axis=-1, keepdims=True) + NORM_EPS)
    xh = x * rstd
    dxh = dh * g
    dx = rstd * (dxh - xh * jnp.mean(dxh * xh, axis=-1, keepdims=True))
    return dres + dx, jnp.sum(dh * xh, axis=0, keepdims=True)


def ffn_bwd(x, dy, g, wi, wo, name, hosted=()):
    T = x.shape[0]
    nj = D_FF // FF_CHUNK
    scatter = ChipScatter(list(hosted))
    n = scatter.n

    def body(*refs):
        x_ref, dy_ref, g_ref, wg_ref, wu_ref, wo_ref = refs[:6]
        hx = refs[6:6 + n]
        dx_ref, dg_ref, dwg_ref, dwu_ref, dwo_ref = refs[6 + n:11 + n]
        ho = refs[11 + n:11 + 2 * n]
        h_ref, da_ref, dh_ref = refs[11 + 2 * n:14 + 2 * n]
        sems = refs[14 + 2 * n:]
        j = pl.program_id(0)

        @pl.when(j == 0)
        def _():
            scatter.start(hx, ho, sems)

            def init(r):
                h_ref[r, :] = _rms(x_ref[r, :], g_ref[...]).astype(BF16)
                da_ref[r, :] = (0.5 * dy_ref[r, :]).astype(BF16)
                dh_ref[r, :] = jnp.zeros((r.size, D_MODEL), F32)
            _row_loop(T, init)

        dwg_ref[...] = jnp.zeros_like(dwg_ref)
        dwu_ref[...] = jnp.zeros_like(dwu_ref)
        dwo_ref[...] = jnp.zeros_like(dwo_ref)

        def blk(r):
            hb = h_ref[r, :]
            db = da_ref[r, :]
            gate = jnp.dot(hb, wg_ref[...], preferred_element_type=F32)
            up = jnp.dot(hb, wu_ref[...], preferred_element_type=F32)
            sg = jax.nn.sigmoid(gate)
            sl = gate * sg
            da = _bdot_nt(db, wo_ref[...])
            dup = (da * sl).astype(BF16)
            dgate = (da * up * (sg * (1.0 + gate * (1.0 - sg)))).astype(BF16)
            dwo_ref[...] += _bdot_tn((sl * up).astype(BF16), db)
            dwg_ref[...] += _bdot_tn(hb, dgate)
            dwu_ref[...] += _bdot_tn(hb, dup)
            dh_ref[r, :] += _bdot_nt(dgate, wg_ref[...]) + _bdot_nt(dup, wu_ref[...])
        _row_loop(T, blk)

        @pl.when(j == nj - 1)
        def _():
            dg_ref[...] = jnp.zeros_like(dg_ref)

            def fin(r):
                dx, dg = _norm_bwd_rows(x_ref[r, :], g_ref[...], dh_ref[r, :], dy_ref[r, :])
                dx_ref[r, :] = dx
                dg_ref[...] += dg
            _row_loop(T, fin)
            scatter.finish(hx, ho, sems)

    full = pl.BlockSpec((T, D_MODEL), lambda j: (0, 0))
    vec = pl.BlockSpec((1, D_MODEL), lambda j: (0, 0))
    res = pl.pallas_call(
        body, name=name, grid=(nj,),
        in_specs=[full, full, vec,
                  pl.BlockSpec((D_MODEL, FF_CHUNK), lambda j: (0, j)),
                  pl.BlockSpec((D_MODEL, FF_CHUNK), lambda j: (0, j + nj)),
                  pl.BlockSpec((FF_CHUNK, D_MODEL), lambda j: (j, 0))] + scatter.in_specs,
        out_specs=tuple([full, vec,
                         pl.BlockSpec((D_MODEL, FF_CHUNK), lambda j: (0, j)),
                         pl.BlockSpec((D_MODEL, FF_CHUNK), lambda j: (0, j)),
                         pl.BlockSpec((FF_CHUNK, D_MODEL), lambda j: (j, 0))] + scatter.out_specs),
        out_shape=tuple([jax.ShapeDtypeStruct((T, D_MODEL), F32), jax.ShapeDtypeStruct((1, D_MODEL), F32),
                         jax.ShapeDtypeStruct((D_MODEL, D_FF), F32), jax.ShapeDtypeStruct((D_MODEL, D_FF), F32),
                         jax.ShapeDtypeStruct((D_FF, D_MODEL), F32)] + scatter.out_shape),
        scratch_shapes=[pltpu.VMEM((T, D_MODEL), BF16), pltpu.VMEM((T, D_MODEL), BF16),
                        pltpu.VMEM((T, D_MODEL), F32)] + scatter.scratch,
        compiler_params=_params(("arbitrary",)))(x, dy, g, wi, wi, wo, *hosted)
    return res[0], res[1], res[2], res[3], res[4], list(res[5:])


def proj_fwd(x, g, w, name):
    T = x.shape[0]
    nj = D_IN_PAD // IN_CHUNK

    def body(x_ref, g_ref, w_ref, o_ref, h_ref):
        @pl.when(pl.program_id(0) == 0)
        def _():
            def init(r):
                h_ref[r, :] = _rms(x_ref[r, :], g_ref[...]).astype(BF16)
            _row_loop(T, init)

        def blk(r):
            o_ref[r, :] = jnp.dot(h_ref[r, :], w_ref[...], preferred_element_type=F32)
        _row_loop(T, blk)

    return pl.pallas_call(
        body, name=name, grid=(nj,),
        in_specs=[pl.BlockSpec((T, D_MODEL), lambda j: (0, 0)), pl.BlockSpec((1, D_MODEL), lambda j: (0, 0)),
                  pl.BlockSpec((D_MODEL, IN_CHUNK), lambda j: (0, j))],
        out_specs=pl.BlockSpec((T, IN_CHUNK), lambda j: (0, j)),
        out_shape=jax.ShapeDtypeStruct((T, D_IN_PAD), F32),
        scratch_shapes=[pltpu.VMEM((T, D_MODEL), BF16)],
        compiler_params=_params(("arbitrary",)))(x, g, w)


def proj_bwd(x, dres, g, w, dp, name):
    T = x.shape[0]
    nj = D_IN_PAD // IN_CHUNK

    def body(x_ref, dres_ref, g_ref, w_ref, dp_ref, dx_ref, dg_ref, dw_ref, h_ref, dh_ref):
        j = pl.program_id(0)

        @pl.when(j == 0)
        def _():
            def init(r):
                h_ref[r, :] = _rms(x_ref[r, :], g_ref[...]).astype(BF16)
                dh_ref[r, :] = jnp.zeros((r.size, D_MODEL), F32)
            _row_loop(T, init)

        dw_ref[...] = jnp.zeros_like(dw_ref)

        def blk(r):
            dpb = dp_ref[r, :].astype(BF16)
            dw_ref[...] += _bdot_tn(h_ref[r, :], dpb)
            dh_ref[r, :] += _bdot_nt(dpb, w_ref[...])
        _row_loop(T, blk)

        @pl.when(j == nj - 1)
        def _():
            dg_ref[...] = jnp.zeros_like(dg_ref)

            def fin(r):
                dx, dg = _norm_bwd_rows(x_ref[r, :], g_ref[...], dh_ref[r, :], dres_ref[r, :])
                dx_ref[r, :] = dx
                dg_ref[...] += dg
            _row_loop(T, fin)

    full = pl.BlockSpec((T, D_MODEL), lambda j: (0, 0))
    vec = pl.BlockSpec((1, D_MODEL), lambda j: (0, 0))
    return pl.pallas_call(
        body, name=name, grid=(nj,),
        in_specs=[full, full, vec, pl.BlockSpec((D_MODEL, IN_CHUNK), lambda j: (0, j)),
                  pl.BlockSpec((T, IN_CHUNK), lambda j: (0, j))],
        out_specs=(full, vec, pl.BlockSpec((D_MODEL, IN_CHUNK), lambda j: (0, j))),
        out_shape=(jax.ShapeDtypeStruct((T, D_MODEL), F32), jax.ShapeDtypeStruct((1, D_MODEL), F32),
                   jax.ShapeDtypeStruct((D_MODEL, D_IN_PAD), F32)),
        scratch_shapes=[pltpu.VMEM((T, D_MODEL), BF16), pltpu.VMEM((T, D_MODEL), F32)],
        compiler_params=_params(("arbitrary",)))(x, dres, g, w, dp)


def out_fwd(mixed, w, x, name):
    T = x.shape[0]

    def body(m_ref, w_ref, x_ref, o_ref):
        o_ref[...] = x_ref[...] + jnp.dot(m_ref[...].astype(BF16), w_ref[...], preferred_element_type=F32)

    blk = pl.BlockSpec((ROWS, D_MODEL), lambda i: (i, 0))
    return pl.pallas_call(
        body, name=name, grid=(T // ROWS,),
        in_specs=[blk, pl.BlockSpec((D_MODEL, D_MODEL), lambda i: (0, 0)), blk],
        out_specs=blk, out_shape=jax.ShapeDtypeStruct((T, D_MODEL), F32),
        compiler_params=_params(("arbitrary",)))(mixed, w, x)


def out_bwd(mixed, w, dy, name):
    T = dy.shape[0]

    def body(m_ref, w_ref, dy_ref, dm_ref, dw_ref):
        @pl.when(pl.program_id(0) == 0)
        def _():
            dw_ref[...] = jnp.zeros_like(dw_ref)
        dyb = dy_ref[...].astype(BF16)
        dm_ref[...] = _bdot_nt(dyb, w_ref[...])
        dw_ref[...] += _bdot_tn(m_ref[...].astype(BF16), dyb)

    blk = pl.BlockSpec((ROWS, D_MODEL), lambda i: (i, 0))
    sq = pl.BlockSpec((D_MODEL, D_MODEL), lambda i: (0, 0))
    return pl.pallas_call(
        body, name=name, grid=(T // ROWS,),
        in_specs=[blk, sq, blk], out_specs=(blk, sq),
        out_shape=(jax.ShapeDtypeStruct((T, D_MODEL), F32), jax.ShapeDtypeStruct((D_MODEL, D_MODEL), F32)),
        compiler_params=_params(("arbitrary",)))(mixed, w, dy)


def loss_head(x, g, target, name):
    T = x.shape[0]

    def body(x_ref, g_ref, t_ref, loss_ref, dx_ref, dg_ref):
        @pl.when(pl.program_id(0) == 0)
        def _():
            loss_ref[...] = jnp.zeros_like(loss_ref)
            dg_ref[...] = jnp.zeros_like(dg_ref)
        xb = x_ref[...]
        rstd = lax.rsqrt(jnp.mean(xb * xb, axis=-1, keepdims=True) + NORM_EPS)
        xh = xb * rstd
        err = xh * g_ref[...] - t_ref[...]
        loss_ref[...] += 0.5 * jnp.sum(jnp.mean(err * err, axis=-1, keepdims=True), axis=0, keepdims=True)
        dy = err * (1.0 / D_MODEL)
        dg_ref[...] += jnp.sum(dy * xh, axis=0, keepdims=True)
        dxh = dy * g_ref[...]
        dx_ref[...] = rstd * (dxh - xh * jnp.mean(dxh * xh, axis=-1, keepdims=True))

    blk = pl.BlockSpec((ROWS, D_MODEL), lambda i: (i, 0))
    vec = pl.BlockSpec((1, D_MODEL), lambda i: (0, 0))
    return pl.pallas_call(
        body, name=name, grid=(T // ROWS,),
        in_specs=[blk, vec, blk], out_specs=(pl.BlockSpec((1, 1), lambda i: (0, 0)), blk, vec),
        out_shape=(jax.ShapeDtypeStruct((1, 1), F32), jax.ShapeDtypeStruct((T, D_MODEL), F32),
                   jax.ShapeDtypeStruct((1, D_MODEL), F32)),
        compiler_params=_params(("arbitrary",)))(x, g, target)


def rowwise_fwd(fn, rows, shared, out_widths, name):
    T = rows[0].shape[0]
    n_in = len(rows) + len(shared)

    def body(*refs):
        res = fn(*[r[...] for r in refs[:n_in]])
        for o, v in zip(refs[n_in:], res):
            o[...] = v

    in_specs = ([pl.BlockSpec((ROWS, a.shape[1]), lambda i: (i, 0)) for a in rows]
                + [pl.BlockSpec(a.shape, lambda i: (0, 0)) for a in shared])
    return pl.pallas_call(
        body, name=name, grid=(T // ROWS,), in_specs=in_specs,
        out_specs=tuple(pl.BlockSpec((ROWS, w), lambda i: (i, 0)) for w in out_widths),
        out_shape=tuple(jax.ShapeDtypeStruct((T, w), F32) for w in out_widths),
        compiler_params=_params(("arbitrary",)))(*rows, *shared)


def rowwise_bwd(fn, rows, shared, cts, name, ct_fn=None):
    T = rows[0].shape[0]
    nr, ns, nc = len(rows), len(shared), len(cts)

    def body(*refs):
        ins = [r[...] for r in refs[:nr + ns]]
        ctv = tuple(r[...] for r in refs[nr + ns:nr + ns + nc])
        outs = refs[nr + ns + nc:]
        _, vjp = jax.vjp(fn, *ins)
        grads = vjp(ct_fn(*ctv) if ct_fn is not None else ctv)
        for k in range(nr):
            outs[k][...] = grads[k]

        @pl.when(pl.program_id(0) == 0)
        def _():
            for k in range(ns):
                outs[nr + k][...] = jnp.zeros_like(outs[nr + k])
        for k in range(ns):
            outs[nr + k][...] += grads[nr + k]

    row_spec = lambda a: pl.BlockSpec((ROWS, a.shape[1]), lambda i: (i, 0))
    sh_spec = lambda a: pl.BlockSpec(a.shape, lambda i: (0, 0))
    return pl.pallas_call(
        body, name=name, grid=(T // ROWS,),
        in_specs=[row_spec(a) for a in rows] + [sh_spec(a) for a in shared] + [row_spec(a) for a in cts],
        out_specs=tuple([row_spec(a) for a in rows] + [sh_spec(a) for a in shared]),
        out_shape=tuple(jax.ShapeDtypeStruct(a.shape, F32) for a in list(rows) + list(shared)),
        compiler_params=_params(("arbitrary",)))(*rows, *shared, *cts)


def shift_rows(x, s):
    return jnp.pad(x, ((s, 0), (0, 0)))[:x.shape[0]]


def unshift_rows(x, s):
    return jnp.pad(x, ((0, s), (0, 0)))[s:]


def _neg_expm1(y):
    series = -(y * (1.0 + y * (0.5 + y * (1.0 / 6.0 + y * (1.0 / 24.0)))))
    return jnp.where(y > -0.05, series, 1.0 - jnp.exp(y))


def lru_pre_fn(x0, x1, x2, x3, first, w0, w1, w2, w3, cb, ga, gab, gx, gxb, lam):
    xc = w3 * x0 + w2 * x1 + w1 * x2 + w0 * x3 + cb
    r = jax.nn.sigmoid(_hdot(xc, ga) + gab)
    i = jax.nn.sigmoid(_hdot(xc, gx) + gxb)
    log_a = -LRU_C * r * jax.nn.softplus(-lam)
    a = jnp.exp(log_a)
    mult = jnp.where(first > 0.5, 1.0, jnp.sqrt(_neg_expm1(2.0 * log_a)))
    return a, mult * i * xc


def lru_post_fn(h, py, og):
    return (_rms(h * jax.nn.gelu(py), og),)


def lru_scan(a, b, reverse, name):
    T, C = a.shape
    nb = T // 8

    def body(a_ref, b_ref, h_ref):
        rows = lax.broadcasted_iota(jnp.int32, (8, C), 0)

        def blk(i, carry):
            j = nb - 1 - i if reverse else i
            r = pl.ds(pl.multiple_of(j * 8, 8), 8)
            A = a_ref[r, :]
            B = b_ref[r, :]
            for s in (1, 2, 4):
                if reverse:
                    keep = rows < 8 - s
                    sh = 8 - s
                else:
                    keep = rows >= s
                    sh = s
                Bs = jnp.where(keep, pltpu.roll(B, sh, 0), 0.0)
                As = jnp.where(keep, pltpu.roll(A, sh, 0), 1.0)
                B = B + A * Bs
                A = A * As
            hb = B + A * carry
            h_ref[r, :] = hb
            edge = 0 if reverse else 7
            return jnp.sum(jnp.where(rows == edge, hb, 0.0), axis=0, keepdims=True)

        lax.fori_loop(0, nb, blk, jnp.zeros((1, C), F32))

    full = pl.BlockSpec((T, C), lambda: (0, 0))
    return pl.pallas_call(body, name=name, in_specs=[full, full], out_specs=full,
                          out_shape=jax.ShapeDtypeStruct((T, C), F32), compiler_params=_params())(a, b)


def make_rwkv_pre_fn(has_vres):
    def fn(p, pp, *rest):
        if has_vres:
            vf, mu, w_up, w_b, a_up, a_b, g_up, kk_w, ka_w, vw1, vw2, vb = rest
        else:
            mu, w_up, w_b, a_up, a_b, g_up, kk_w, ka_w = rest
        xm = p + (pp - p) * mu
        r, k, v = xm[:, 0:384], xm[:, 384:768], xm[:, 768:1152]
        xw, xa, xg = xm[:, 1152:1216], xm[:, 1216:1280], xm[:, 1280:1408]
        w_log = -jax.nn.softplus(-(w_b + _hdot(jnp.tanh(xw), w_up))) - 0.5
        lw = -jnp.exp(w_log)
        a = jax.nn.sigmoid(a_b + _hdot(xa, a_up))
        g = _hdot(jax.nn.sigmoid(xg), g_up)
        if has_vres:
            v = v + (vf - v) * jax.nn.sigmoid(vb + _hdot(_hdot(v, vw1), vw2))
        kkx = k * kk_w
        kk = kkx * lax.rsqrt(_segsum(kkx * kkx) + 1e-6)
        k2 = k * (1.0 + (a - 1.0) * ka_w)
        return r, lw, k2, v, kk, a, g
    return fn


def rwkv_post_fn(y, r, k2, v, g, ln_g, ln_b, r_k):
    mean = _segsum(y) * (1.0 / HEAD_DIM)
    yc = y - mean
    var = _segsum(yc * yc) * (1.0 / HEAD_DIM)
    yn = yc * lax.rsqrt(var + GN_EPS) * ln_g + ln_b
    bonus = _segsum(r * k2 * r_k) * v
    return ((yn + bonus) * g,)


def _head_expander(first_lane):
    ri, ci = _iota2(128, MIX_W)
    return (ri == ci // HEAD_DIM + first_lane).astype(F32)


def gdn_pre_fn(x0, x1, x2, x3, ab, w0, w1, w2, w3, alog, dtb):
    qkv = jax.nn.silu(w3 * x0 + w2 * x1 + w1 * x2 + w0 * x3)
    q, k, v = qkv[:, 0:384], qkv[:, 384:768], qkv[:, 768:1152]
    q = q * lax.rsqrt(_segsum(q * q) + 1e-6) * (HEAD_DIM ** -0.5)
    k = k * lax.rsqrt(_segsum(k * k) + 1e-6)
    g = -jnp.exp(alog) * jax.nn.softplus(ab + dtb)
    beta = jax.nn.sigmoid(ab)
    ge = _xdot(g, functools.partial(_head_expander, 0))
    be = _xdot(beta, functools.partial(_head_expander, HEADS))
    return q, k, v, ge, be


def gdn_post_fn(o, z, ng):
    ms = _segsum(o * o) * (1.0 / HEAD_DIM)
    return (o * lax.rsqrt(ms + NORM_EPS) * ng * jax.nn.silu(z),)


def _neumann_inv(m):
    n = m.shape[-1]
    ri, ci = _iota2(n, n)
    eye = (ri == ci).astype(F32)
    md = jnp.where(ri // 16 == ci // 16, m, 0.0)
    mo = m - md
    t0 = eye + md
    p2 = _hdot(md, md)
    t0 = t0 + _hdot(t0, p2)
    p4 = _hdot(p2, p2)
    t0 = t0 + _hdot(t0, p4)
    p8 = _hdot(p4, p4)
    t0 = t0 + _hdot(t0, p8)
    nn = _hdot(t0, mo)
    n2 = _hdot(nn, nn)
    t1 = eye + nn + n2 + _hdot(nn, n2)
    return _hdot(t1, t0)


@jax.custom_vjp
def _inv_saved(m, t_saved):
    return t_saved


def _inv_saved_fwd(m, t_saved):
    return t_saved, t_saved


def _inv_saved_bwd(t_saved, dt):
    tt = jnp.swapaxes(t_saved, -1, -2)
    return _hdot(_hdot(tt, dt), tt), jnp.zeros_like(t_saved)


_inv_saved.defvjp(_inv_saved_fwd, _inv_saved_bwd)


def _heads(x):
    return jnp.concatenate([x[None, :, h * HEAD_DIM:(h + 1) * HEAD_DIM] for h in range(HEADS)], axis=0)


def _unheads(y):
    return jnp.concatenate([lax.index_in_dim(y, h, 0, keepdims=False) for h in range(HEADS)], axis=1)


def rwkv_heads(s0, r, lw, k2, v, kk, a, inv):
    n = r.shape[0]
    ri, ci = _iota2(n, n)
    low, strict = ri >= ci, ri > ci
    cs = _cumsum_rows(lw)
    cl = jnp.sum(lw, axis=0, keepdims=True)
    p_in, p_prev, p_inv = jnp.exp(cs), jnp.exp(cs - lw), jnp.exp(-cs)
    p_rest, p_all = jnp.exp(cl - cs), jnp.exp(cl)
    bd = kk * a
    at, rt = _heads(-kk * p_prev), _heads(r * p_in)
    bh, kh = _heads(bd * p_inv), _heads(k2 * p_inv)
    vh = _heads(v)
    m_ab = jnp.where(strict, _cdot_nt(at, bh), 0.0)
    m_ak = jnp.where(strict, _cdot_nt(at, kh), 0.0)
    m_rb = jnp.where(low, _cdot_nt(rt, bh), 0.0)
    m_rk = jnp.where(low, _cdot_nt(rt, kh), 0.0)
    sa = _cdot(inv(m_ab), _cdot_nt(at, s0) + _cdot(m_ak, vh))
    y = _cdot_nt(rt, s0) + _cdot(m_rb, sa) + _cdot(m_rk, vh)
    s1 = s0 * _heads(p_all) + _cdot_tn(sa, _heads(bd * p_rest)) + _cdot_tn(vh, _heads(k2 * p_rest))
    return _unheads(y), s1


def gdn_heads(s0, q, k, v, ge, be, inv):
    n = q.shape[0]
    ri, ci = _iota2(n, n)
    low, strict = ri >= ci, ri > ci
    gc = _cumsum_rows(ge)
    gl = jnp.sum(ge, axis=0, keepdims=True)
    gch = _heads(gc)
    decay = jnp.where(low, jnp.exp(jnp.where(low, gch - jnp.swapaxes(gch, 1, 2), 0.0)), 0.0)
    kb = k * be
    e = jnp.exp(gc)
    kh = _heads(k)
    m = -jnp.where(strict, _cdot_nt(_heads(kb), kh) * decay, 0.0)
    mr = jnp.where(low, _cdot_nt(_heads(q), kh) * decay, 0.0)
    u = _cdot(inv(m), _heads(v * be) - _cdot_nt(_heads(kb * e), s0))
    y = _cdot_nt(_heads(q * e), s0) + _cdot(mr, u)
    s1 = s0 * _heads(jnp.exp(gl)) + _cdot_tn(u, _heads(k * jnp.exp(gl - gc)))
    return _unheads(y), s1


def core_fwd(heads_fn, ins, name, hosted=()):
    T = ins[0].shape[0]
    nc = T // CHUNK
    steps = nc // CHUNKS_PER_STEP
    n = len(ins)
    gather = ChipGather(list(hosted))
    ng = gather.n

    def body(*refs):
        hx = refs[n:n + ng]
        y_ref, s0_ref, t_ref = refs[n + ng:n + ng + 3]
        ho = refs[n + ng + 3:n + 2 * ng + 3]
        s_ref = refs[n + 2 * ng + 3]
        sems = refs[n + 2 * ng + 4:]
        c = pl.program_id(0)

        @pl.when(c == 0)
        def _():
            gather.start(hx, ho, sems)
            s_ref[...] = jnp.zeros_like(s_ref)

        state = s_ref[...]
        for u in range(CHUNKS_PER_STEP):
            rows = slice(u * CHUNK, (u + 1) * CHUNK)
            kept = []

            def inv(m):
                kept.append(_neumann_inv(m))
                return kept[0]

            y, after = heads_fn(state, *[r[rows, :] for r in refs[:n]], inv)
            y_ref[rows, :] = y
            s0_ref[u] = state
            t_ref[u] = kept[0]
            state = after
        s_ref[...] = state

        @pl.when(c == max(steps - 2, 0))
        def _():
            gather.relay(hx, ho, sems)

        @pl.when(c == steps - 1)
        def _():
            gather.finish(hx, ho, sems)

    row = pl.BlockSpec((CHUNKS_PER_STEP * CHUNK, MIX_W), lambda c: (c, 0))
    st_shape = (HEADS, HEAD_DIM, HEAD_DIM)
    st = pl.BlockSpec((CHUNKS_PER_STEP,) + st_shape, lambda c: (c, 0, 0, 0))
    res = pl.pallas_call(
        body, name=name, grid=(steps,), in_specs=[row] * n + gather.in_specs,
        out_specs=tuple([row, st, st] + gather.out_specs),
        out_shape=tuple([jax.ShapeDtypeStruct((T, MIX_W), F32), jax.ShapeDtypeStruct((nc,) + st_shape, F32),
                         jax.ShapeDtypeStruct((nc,) + st_shape, F32)] + gather.out_shape),
        scratch_shapes=[pltpu.VMEM(st_shape, F32)] + gather.scratch,
        compiler_params=_params(("arbitrary",)))(*ins, *hosted)
    return res[0], res[1], res[2], list(res[3:])


def core_bwd(heads_fn, ins, s0_all, t_all, dy, name):
    T = ins[0].shape[0]
    nc = T // CHUNK
    steps = nc // CHUNKS_PER_STEP
    n = len(ins)

    def body(*refs):
        s0_ref, t_ref, dy_ref = refs[n:n + 3]
        outs = refs[n + 3:n + 3 + n]
        ds_ref = refs[n + 3 + n]

        @pl.when(pl.program_id(0) == 0)
        def _():
            ds_ref[...] = jnp.zeros_like(ds_ref)

        d_state = ds_ref[...]
        for u in reversed(range(CHUNKS_PER_STEP)):
            rows = slice(u * CHUNK, (u + 1) * CHUNK)
            t_saved = t_ref[u]
            f = lambda s0, *xs: heads_fn(s0, *xs, lambda m: _inv_saved(m, t_saved))
            _, vjp = jax.vjp(f, s0_ref[u], *[r[rows, :] for r in refs[:n]])
            grads = vjp((dy_ref[rows, :], d_state))
            d_state = grads[0]
            for k in range(n):
                outs[k][rows, :] = grads[1 + k]
        ds_ref[...] = d_state

    row = pl.BlockSpec((CHUNKS_PER_STEP * CHUNK, MIX_W), lambda c: (steps - 1 - c, 0))
    st_shape = (HEADS, HEAD_DIM, HEAD_DIM)
    st = pl.BlockSpec((CHUNKS_PER_STEP,) + st_shape, lambda c: (steps - 1 - c, 0, 0, 0))
    return pl.pallas_call(
        body, name=name, grid=(steps,), in_specs=[row] * n + [st, st, row], out_specs=tuple([row] * n),
        out_shape=tuple(jax.ShapeDtypeStruct((T, MIX_W), F32) for _ in range(n)),
        scratch_shapes=[pltpu.VMEM(st_shape, F32)],
        compiler_params=_params(("arbitrary",)))(*ins, s0_all, t_all, dy)


def _block_diag(w):
    out = jnp.zeros((LRU_W, LRU_W), w.dtype)
    for n in range(LRU_BLOCKS):
        out = lax.dynamic_update_slice(out, w[n], (n * 64, n * 64))
    return out


def _block_diag_grad(g):
    return jnp.stack([g[n * 64:(n + 1) * 64, n * 64:(n + 1) * 64] for n in range(LRU_BLOCKS)])


def _row(v):
    return v.reshape(1, -1)


def _pad128(v):
    return jnp.pad(v.reshape(1, -1), ((0, 0), (0, 128 - v.size)))


def _layer_shared(w, l):
    cw = w['lru_conv_w'][l]
    lru_pre = [_row(cw[0]), _row(cw[1]), _row(cw[2]), _row(cw[3]), _row(w['lru_conv_b'][l]),
               _block_diag(w['lru_gate_a_w'][l]), _row(w['lru_gate_a_b'][l]),
               _block_diag(w['lru_gate_x_w'][l]), _row(w['lru_gate_x_b'][l]), _row(w['lru_lambda'][l])]
    rw_pre = [_row(w['rwkv_mu'][l]), w['rwkv_w_up'][l], _row(w['rwkv_w_bias'][l]), w['rwkv_a_up'][l],
              _row(w['rwkv_a_bias'][l]), w['rwkv_g_up'][l], _row(w['rwkv_k_k'][l]), _row(w['rwkv_k_a'][l])]
    if l > 0:
        rw_pre += [w['rwkv_vres_w1'][l - 1], w['rwkv_vres_w2'][l - 1], _row(w['rwkv_vres_b'][l - 1])]
    rw_post = [_row(w['rwkv_ln_g'][l]), _row(w['rwkv_ln_b'][l]), _row(w['rwkv_r_k'][l])]
    gw = w['gdn_conv_w'][l]
    gdn_pre = [_row(gw[0]), _row(gw[1]), _row(gw[2]), _row(gw[3]), _pad128(w['gdn_a_log'][l]),
               _pad128(w['gdn_dt_bias'][l])]
    gdn_post = [_row(jnp.tile(w['gdn_norm'][l], HEADS))]
    return dict(lru_pre=lru_pre, lru_post=[_row(w['lru_out_norm'][l])], rw_pre=rw_pre, rw_post=rw_post,
                gdn_pre=gdn_pre, gdn_post=gdn_post)


def _mixer_fwd(p, sh, l, v_first, host_rwkv=(), host_gdn=()):
    T = p.shape[0]
    lx, ly = p[:, 0:256], p[:, 256:512]
    prw, qkv, z, ab = p[:, 512:1920], p[:, 1920:3072], p[:, 3072:3456], p[:, 3456:3584]
    first = (lax.broadcasted_iota(jnp.int32, (T, LRU_W), 0) == 0).astype(F32)
    lru_rows = [lx, shift_rows(lx, 1), shift_rows(lx, 2), shift_rows(lx, 3), first]
    a, b = rowwise_fwd(lru_pre_fn, lru_rows, sh['lru_pre'], (LRU_W, LRU_W), f"lru_pre_fwd{l}")
    hseq = lru_scan(a, b, False, f"lru_scan_fwd{l}")
    (y_lru,) = rowwise_fwd(lru_post_fn, [hseq, ly], sh['lru_post'], (LRU_W,), f"lru_post_fwd{l}")

    rw_rows = [prw, shift_rows(prw, 1)] + ([v_first] if l > 0 else [])
    rw = rowwise_fwd(make_rwkv_pre_fn(l > 0), rw_rows, sh['rw_pre'], (MIX_W,) * 7, f"rwkv_pre_fwd{l}")
    r, lw, k2, v, kk, ar, g = rw
    y_raw, rs0, rt, got_rwkv = core_fwd(rwkv_heads, [r, lw, k2, v, kk, ar], f"rwkv_core_fwd{l}", host_rwkv)
    (y_rw,) = rowwise_fwd(rwkv_post_fn, [y_raw, r, k2, v, g], sh['rw_post'], (MIX_W,), f"rwkv_post_fwd{l}")

    gdn_rows = [qkv, shift_rows(qkv, 1), shift_rows(qkv, 2), shift_rows(qkv, 3), ab]
    gd = rowwise_fwd(gdn_pre_fn, gdn_rows, sh['gdn_pre'], (MIX_W,) * 5, f"gdn_pre_fwd{l}")
    o_raw, gs0, gt, got_gdn = core_fwd(gdn_heads, list(gd), f"gdn_core_fwd{l}", host_gdn)
    (y_gdn,) = rowwise_fwd(gdn_post_fn, [o_raw, z], sh['gdn_post'], (MIX_W,), f"gdn_post_fwd{l}")

    mixed = jnp.concatenate([y_lru, y_rw, y_gdn], axis=1)
    saved = dict(lru_rows=lru_rows, a=a, hseq=hseq, ly=ly, rw_rows=rw_rows, rw=rw, y_raw=y_raw, rs0=rs0, rt=rt,
                 gdn_rows=gdn_rows, gd=gd, o_raw=o_raw, gs0=gs0, gt=gt, z=z)
    v_layer0 = v if l == 0 else None
    return mixed, saved, v_layer0, got_rwkv, got_gdn


def _mixer_bwd(dmixed, sv, sh, l, dv_first):
    d_lru, d_rw, d_gdn = dmixed[:, 0:256], dmixed[:, 256:640], dmixed[:, 640:1024]
    gw = {}

    dh, dly, d_og = rowwise_bwd(lru_post_fn, [sv['hseq'], sv['ly']], sh['lru_post'], [d_lru], f"lru_post_bwd{l}")
    gscan = lru_scan(unshift_rows(sv['a'], 1), dh, True, f"lru_scan_bwd{l}")
    res = rowwise_bwd(lru_pre_fn, sv['lru_rows'], sh['lru_pre'], [gscan, shift_rows(sv['hseq'], 1)],
                      f"lru_pre_bwd{l}", ct_fn=lambda gs, hp: (gs * hp, gs))
    dlx = res[0] + unshift_rows(res[1], 1) + unshift_rows(res[2], 2) + unshift_rows(res[3], 3)
    dw0, dw1, dw2, dw3, dcb, dga, dgab, dgx, dgxb, dlam = res[5:]
    gw['lru_conv_w'] = jnp.concatenate([dw0, dw1, dw2, dw3], axis=0)
    gw['lru_conv_b'] = dcb[0]
    gw['lru_gate_a_w'] = _block_diag_grad(dga)
    gw['lru_gate_a_b'] = dgab.reshape(LRU_BLOCKS, 64)
    gw['lru_gate_x_w'] = _block_diag_grad(dgx)
    gw['lru_gate_x_b'] = dgxb.reshape(LRU_BLOCKS, 64)
    gw['lru_lambda'] = dlam[0]
    gw['lru_out_norm'] = d_og[0]

    r, lw, k2, v, kk, ar, g = sv['rw']
    res = rowwise_bwd(rwkv_post_fn, [sv['y_raw'], r, k2, v, g], sh['rw_post'], [d_rw], f"rwkv_post_bwd{l}")
    dy_raw, dr_p, dk2_p, dv_p, dg = res[:5]
    gw['rwkv_ln_g'], gw['rwkv_ln_b'], gw['rwkv_r_k'] = res[5][0], res[6][0], res[7].reshape(HEADS, HEAD_DIM)
    dr_c, dlw, dk2_c, dv_c, dkk, dar = core_bwd(rwkv_heads, [r, lw, k2, v, kk, ar], sv['rs0'], sv['rt'], dy_raw,
                                                 f"rwkv_core_bwd{l}")
    cts = [dr_p, dr_c, dlw, dk2_p, dk2_c, dv_p, dv_c, dkk, dar, dg]
    if l == 0:
        cts.append(dv_first)
        ct_fn = lambda a1, a2, b, c1, c2, d1, d2, e, f, gg, vf: (a1 + a2, b, c1 + c2, d1 + d2 + vf, e, f, gg)
    else:
        ct_fn = lambda a1, a2, b, c1, c2, d1, d2, e, f, gg: (a1 + a2, b, c1 + c2, d1 + d2, e, f, gg)
    res = rowwise_bwd(make_rwkv_pre_fn(l > 0), sv['rw_rows'], sh['rw_pre'], cts, f"rwkv_pre_bwd{l}", ct_fn=ct_fn)
    dprw = res[0] + unshift_rows(res[1], 1)
    nrow = len(sv['rw_rows'])
    dv_first_out = res[2] if l > 0 else None
    sg = res[nrow:]
    gw['rwkv_mu'], gw['rwkv_w_up'], gw['rwkv_w_bias'], gw['rwkv_a_up'] = sg[0][0], sg[1], sg[2][0], sg[3]
    gw['rwkv_a_bias'], gw['rwkv_g_up'], gw['rwkv_k_k'], gw['rwkv_k_a'] = sg[4][0], sg[5], sg[6][0], sg[7][0]
    if l > 0:
        gw['rwkv_vres_w1'], gw['rwkv_vres_w2'], gw['rwkv_vres_b'] = sg[8], sg[9], sg[10][0]

    do_raw, dz, d_ng = rowwise_bwd(gdn_post_fn, [sv['o_raw'], sv['z']], sh['gdn_post'], [d_gdn], f"gdn_post_bwd{l}")
    gw['gdn_norm'] = jnp.sum(d_ng.reshape(HEADS, HEAD_DIM), axis=0)
    dgd = core_bwd(gdn_heads, list(sv['gd']), sv['gs0'], sv['gt'], do_raw, f"gdn_core_bwd{l}")
    res = rowwise_bwd(gdn_pre_fn, sv['gdn_rows'], sh['gdn_pre'], list(dgd), f"gdn_pre_bwd{l}")
    dqkv = res[0] + unshift_rows(res[1], 1) + unshift_rows(res[2], 2) + unshift_rows(res[3], 3)
    dab = res[4]
    gw['gdn_conv_w'] = jnp.concatenate(res[5:9], axis=0)
    gw['gdn_a_log'], gw['gdn_dt_bias'] = res[9][0, :HEADS], res[10][0, :HEADS]

    dp = jnp.concatenate([dlx, dly, dprw, dqkv, dz, dab], axis=1)
    return dp, gw, dv_first_out


IN_SHARD = D_IN // N_CHIPS
IN_SHARD_PAD = D_IN_PAD // N_CHIPS


def _cols_to_chips(g, n=N_CHIPS):
    r = g.shape[0]
    return jnp.transpose(g.reshape(r, n, -1), (1, 0, 2))


def _cols_from_chips(g):
    return jnp.transpose(g, (1, 0, 2)).reshape(g.shape[1], -1)


def _w_in_from_chips(g):
    nat = _cols_from_chips(g[:, :, :IN_SHARD])
    return jnp.pad(nat, ((0, 0), (0, D_IN_PAD - D_IN)))


def _w_in_to_chips(g):
    return jnp.pad(_cols_to_chips(g[:, :D_IN]), ((0, 0), (0, 0), (0, IN_SHARD_PAD - IN_SHARD)))


def _natural(name, g):
    if name == 'w_in':
        return _w_in_from_chips(g)
    if BIG[name] == 2:
        return _cols_from_chips(g)
    return g.reshape(-1, g.shape[2])


def local_step(x, target, w, wb, shards=None):
    def hosted(keys):
        return [shards[k] for k in keys] if shards is not None else []

    def arrived(keys, gathered):
        for (name, layer), g in zip(keys if shards is not None else [], gathered):
            wb[name][layer] = _natural(name, g)

    saved = []
    v_first = None
    for l in range(N_LAYERS):
        sh = _layer_shared(w, l)
        more = l + 1 < N_LAYERS
        in_ffn1 = [('w_in', l)] + ([('w_out', l)] if more else [])
        in_rwkv = [('ffn2_wi', l)] + ([('ffn2_wo', l)] if more else [])
        in_gdn = [('ffn1_wi', l + 1)] if more else [('ffn2_wo', l), ('w_out', l)]
        in_ffn2 = [('ffn1_wo', l + 1)] if more else []
        x1, got = ffn_fwd(x, _row(w['ffn1_norm'][l]), wb['ffn1_wi'][l], wb['ffn1_wo'][l], f"ffn1_fwd{l}",
                          hosted(in_ffn1))
        arrived(in_ffn1, got)
        p = proj_fwd(x1, _row(w['mix_norm'][l]), wb['w_in'][l], f"proj_fwd{l}")
        mixed, sv, v0, got_rwkv, got_gdn = _mixer_fwd(p, sh, l, v_first, hosted(in_rwkv), hosted(in_gdn))
        arrived(in_rwkv, got_rwkv)
        arrived(in_gdn, got_gdn)
        if l == 0:
            v_first = v0
        x2 = out_fwd(mixed, wb['w_out'][l], x1, f"out_fwd{l}")
        x3, got = ffn_fwd(x2, _row(w['ffn2_norm'][l]), wb['ffn2_wi'][l], wb['ffn2_wo'][l], f"ffn2_fwd{l}",
                          hosted(in_ffn2))
        arrived(in_ffn2, got)
        saved.append(dict(x0=x, x1=x1, x2=x2, mixed=mixed, sv=sv, sh=sh))
        x = x3

    loss, dx, dgf = loss_head(x, _row(w['final_norm']), target, "loss_head")
    per_layer = [dict() for _ in range(N_LAYERS)]
    dv_first = jnp.zeros((x.shape[0], MIX_W), F32)

    waiting, chip_sums, arrived_parts = [], {}, {}

    def reduce_now(keys, tag):
        if shards is None:
            return
        sums = chip_sums_of([(n, k, per_layer[k][n]) for n, k in keys], lax.axis_index("c"), tag)
        for key, (total, total_bf) in zip(keys, sums):
            chip_sums[key] = total
            waiting.append((key, total_bf))

    def take_waiting():
        keys, parts = [k for k, _ in waiting], [p for _, p in waiting]
        waiting.clear()
        return keys, parts

    for l in reversed(range(N_LAYERS)):
        s = saved[l]
        gw = per_layer[l]
        keys, parts = take_waiting()
        dx, dg2, dwg, dwu, dwo, got = ffn_bwd(s['x2'], dx, _row(w['ffn2_norm'][l]), wb['ffn2_wi'][l],
                                              wb['ffn2_wo'][l], f"ffn2_bwd{l}", parts)
        arrived_parts.update(zip(keys, got))
        wi_parts = lambda dwg, dwu: (dwg, dwu)
        row_parts = lambda dw: dw.reshape(N_CHIPS, -1, dw.shape[1])
        gw['ffn2_norm'], gw['ffn2_wi'], gw['ffn2_wo'] = dg2[0], wi_parts(dwg, dwu), row_parts(dwo)
        if l == N_LAYERS - 1:
            reduce_now([('ffn2_wi', l), ('ffn2_wo', l)], f"ffn2_{l}")
        dmixed, dw_out = out_bwd(s['mixed'], wb['w_out'][l], dx, f"out_bwd{l}")
        gw['w_out'] = row_parts(dw_out)
        dp, gmix, dvf = _mixer_bwd(dmixed, s['sv'], s['sh'], l, dv_first)
        if l > 0:
            dv_first = dvf
        gw.update(gmix)
        dx, dgm, dwin = proj_bwd(s['x1'], dx, _row(w['mix_norm'][l]), wb['w_in'][l], dp, f"proj_bwd{l}")
        gw['mix_norm'], gw['w_in'] = dgm[0], _w_in_to_chips(dwin)
        if l < N_LAYERS - 1:
            reduce_now([('ffn2_wi', l), ('ffn2_wo', l), ('w_in', l), ('w_out', l)], f"mix_{l}")
        keys, parts = take_waiting()
        dx, dg1, dwg, dwu, dwo, got = ffn_bwd(s['x0'], dx, _row(w['ffn1_norm'][l]), wb['ffn1_wi'][l],
                                              wb['ffn1_wo'][l], f"ffn1_bwd{l}", parts)
        arrived_parts.update(zip(keys, got))
        gw['ffn1_norm'], gw['ffn1_wi'], gw['ffn1_wo'] = dg1[0], wi_parts(dwg, dwu), row_parts(dwo)
        if l == N_LAYERS - 1:
            reduce_now([('w_in', l), ('w_out', l), ('ffn1_wi', l), ('ffn1_wo', l)], f"ffn1_{l}")
        else:
            reduce_now([('ffn1_wi', l), ('ffn1_wo', l)], f"ffn1_{l}")
    grads = {'final_norm': dgf[0]}
    if shards is not None:
        grads['last_round'] = take_waiting()
        arrived_parts.update({key: None for key in grads['last_round'][0]})
    for name in WEIGHTS:
        if name == 'final_norm':
            continue
        if name in BIG:
            if shards is None:
                grads[name] = [per_layer[l][name] for l in range(N_LAYERS)]
            else:
                grads[name] = [(chip_sums[(name, l)], arrived_parts[(name, l)]) for l in range(N_LAYERS)]
        elif name.startswith('rwkv_vres'):
            grads[name] = per_layer[1][name][None]
        else:
            grads[name] = jnp.stack([per_layer[l][name] for l in range(N_LAYERS)])
    return loss[0, 0], dx, grads


ANY = pl.BlockSpec(memory_space=pl.ANY)


def _coords():
    return lax.axis_index("x"), lax.axis_index("y"), lax.axis_index("c")


def _other_chips(x, y):
    return [((x + 1) % 2, y), (x, (y + 1) % 2), ((x + 1) % 2, (y + 1) % 2)]


def allreduce_small(pack, name, hosted=()):
    R = pack.shape[0]
    rh = R // 2
    gather = ChipGather(list(hosted))
    ng = gather.n

    def body(*refs):
        x_ref, hx = refs[0], refs[1:1 + ng]
        o_ref, ho = refs[1 + ng], refs[2 + ng:2 + 2 * ng]
        sib_ref, chip_ref, parts_ref, send_sems, recv_sems = refs[2 + 2 * ng:7 + 2 * ng]
        gather_sems = refs[7 + 2 * ng:]
        gather.start(hx, ho, gather_sems)
        x, y, c = _coords()
        sib = (x, y, 1 - c)

        def copy(k, src, dst, to):
            return pltpu.make_async_remote_copy(src_ref=src, dst_ref=dst, send_sem=send_sems.at[k],
                                                recv_sem=recv_sems.at[k], device_id=to, device_id_type=MESH)

        swap = copy(0, x_ref, sib_ref, sib)
        swap.start()
        swap.wait()
        chip_ref[...] = jnp.where(c == 0, x_ref[...], sib_ref[...]) + jnp.where(c == 0, sib_ref[...], x_ref[...])

        mine = pl.ds(pl.multiple_of(c * rh, 8), rh)
        sends = [copy(1 + j, chip_ref.at[mine], parts_ref.at[j], (px, py, c))
                 for j, (px, py) in enumerate(_other_chips(x, y))]
        for cp in sends:
            cp.start()
        for cp in sends:
            cp.wait()
        s = 2 * x + y
        own = chip_ref[mine, :]
        from_chip = {2: parts_ref[0], 1: parts_ref[1], 3: parts_ref[2]}
        terms = []
        for k in range(N_CHIPS):
            t = own
            for d, part in from_chip.items():
                t = jnp.where(jnp.bitwise_xor(s, d) == k, part, t)
            terms.append(t)
        o_ref[mine, :] = ((terms[0] + terms[1]) + terms[2]) + terms[3]

        share = copy(4, o_ref.at[mine], o_ref.at[mine], sib)
        share.start()
        share.wait()
        gather.relay(hx, ho, gather_sems)
        gather.finish(hx, ho, gather_sems)

    vm = pl.BlockSpec(memory_space=pltpu.VMEM)
    res = pl.pallas_call(
        body, name=name, in_specs=[vm] + gather.in_specs, out_specs=tuple([vm] + gather.out_specs),
        out_shape=tuple([jax.ShapeDtypeStruct((R, 128), F32)] + gather.out_shape),
        scratch_shapes=[pltpu.VMEM((R, 128), F32), pltpu.VMEM((R, 128), F32), pltpu.VMEM((3, rh, 128), F32),
                        pltpu.SemaphoreType.DMA((5,)), pltpu.SemaphoreType.DMA((5,))] + gather.scratch,
        compiler_params=_params())(pack, *hosted)
    return res[0], list(res[1:])


class ChipGather:
    def __init__(self, shards):
        self.shapes = [s.shape for s in shards]
        self.n = len(shards)
        self.in_specs = [ANY] * self.n
        self.out_specs = [ANY] * self.n
        self.out_shape = [jax.ShapeDtypeStruct((N_CHIPS,) + s.shape, s.dtype) for s in shards]
        self.scratch = [pltpu.SemaphoreType.DMA((6 * self.n,)), pltpu.SemaphoreType.DMA((6 * self.n,)),
                        pltpu.SemaphoreType.DMA((self.n,))] if self.n else []

    def _rows(self, a, core):
        rh = self.shapes[a][0] // 2
        return pl.ds(pl.multiple_of(core * rh, 16), rh)

    def _copies(self, kind, x_refs, o_refs, sems):
        send_sems, recv_sems, local_sems = sems
        x, y, c = _coords()
        s_me = 2 * x + y
        sib = (x, y, 1 - c)

        def copy(a, k, src, dst, to):
            return pltpu.make_async_remote_copy(src_ref=src, dst_ref=dst, send_sem=send_sems.at[6 * a + k],
                                                recv_sem=recv_sems.at[6 * a + k], device_id=to, device_id_type=MESH)

        if kind == 'own':
            return [pltpu.make_async_copy(x_refs[a], o_refs[a].at[s_me], local_sems.at[a]) for a in range(self.n)]
        out = []
        for j, (px, py) in enumerate(_other_chips(x, y)):
            for a in range(self.n):
                mine = self._rows(a, c)
                part = o_refs[a].at[2 * px + py, mine]
                if kind == 'sent':
                    out.append(copy(a, j, x_refs[a].at[mine], o_refs[a].at[s_me, mine], (px, py, c)))
                elif kind == 'arrived':
                    out.append(copy(a, j, part, part, (px, py, c)))
                elif kind == 'passed':
                    out.append(copy(a, 3 + j, part, part, sib))
                else:
                    theirs = o_refs[a].at[2 * px + py, self._rows(a, 1 - c)]
                    out.append(copy(a, 3 + j, theirs, theirs, sib))
        return out

    def start(self, x_refs, o_refs, sems):
        if not self.n:
            return
        for cp in self._copies('own', x_refs, o_refs, sems) + self._copies('sent', x_refs, o_refs, sems):
            cp.start()

    def relay(self, x_refs, o_refs, sems):
        if not self.n:
            return
        for got, fw in zip(self._copies('arrived', x_refs, o_refs, sems),
                           self._copies('passed', x_refs, o_refs, sems)):
            got.wait_recv()
            fw.start()

    def finish(self, x_refs, o_refs, sems):
        if not self.n:
            return
        for cp in self._copies('from_sibling', x_refs, o_refs, sems):
            cp.wait_recv()
        for cp in self._copies('sent', x_refs, o_refs, sems) + self._copies('passed', x_refs, o_refs, sems):
            cp.wait_send()
        for cp in self._copies('own', x_refs, o_refs, sems):
            cp.wait()


def sibling_swap(srcs, halves, name):
    n = len(srcs)
    row_axis = [s.ndim - 2 for s in srcs]
    out_shapes = [s.shape[:ax] + (s.shape[ax] // 2,) + s.shape[ax + 1:] if halves else s.shape
                  for s, ax in zip(srcs, row_axis)]

    def body(*refs):
        x_refs, o_refs = refs[:n], refs[n:2 * n]
        send_sems, recv_sems = refs[2 * n:]
        x, y, c = _coords()
        copies = []
        for a in range(n):
            part = x_refs[a]
            if halves:
                rh = srcs[a].shape[row_axis[a]] // 2
                theirs = pl.ds(pl.multiple_of((1 - c) * rh, 16), rh)
                part = part.at[:, theirs] if row_axis[a] == 1 else part.at[theirs]
            cp = pltpu.make_async_remote_copy(src_ref=part, dst_ref=o_refs[a], send_sem=send_sems.at[a],
                                              recv_sem=recv_sems.at[a], device_id=(x, y, 1 - c), device_id_type=MESH)
            cp.start()
            copies.append(cp)
        for cp in copies:
            cp.wait()

    return pl.pallas_call(
        body, name=name, in_specs=[ANY] * n, out_specs=tuple([ANY] * n),
        out_shape=tuple(jax.ShapeDtypeStruct(sh, s.dtype) for sh, s in zip(out_shapes, srcs)),
        scratch_shapes=[pltpu.SemaphoreType.DMA((n,)), pltpu.SemaphoreType.DMA((n,))],
        compiler_params=_params())(*srcs)


class ChipScatter:
    def __init__(self, parts):
        self.n = len(parts)
        self.in_specs = [ANY] * self.n
        self.out_specs = [ANY] * self.n
        self.out_shape = [jax.ShapeDtypeStruct((3,) + p.shape[1:], p.dtype) for p in parts]
        self.scratch = [pltpu.SemaphoreType.DMA((3 * self.n,)), pltpu.SemaphoreType.DMA((3 * self.n,))] if self.n else []

    def _copies(self, x_refs, o_refs, sems):
        send_sems, recv_sems = sems
        x, y, c = _coords()
        return [pltpu.make_async_remote_copy(src_ref=x_refs[a].at[2 * px + py], dst_ref=o_refs[a].at[j],
                                             send_sem=send_sems.at[3 * a + j], recv_sem=recv_sems.at[3 * a + j],
                                             device_id=(px, py, c), device_id_type=MESH)
                for j, (px, py) in enumerate(_other_chips(x, y)) for a in range(self.n)]

    def start(self, x_refs, o_refs, sems):
        if self.n:
            for cp in self._copies(x_refs, o_refs, sems):
                cp.start()

    def finish(self, x_refs, o_refs, sems):
        if self.n:
            for cp in self._copies(x_refs, o_refs, sems):
                cp.wait()


HBM = pl.BlockSpec(memory_space=pltpu.HBM)
SEM = pl.BlockSpec(memory_space=pltpu.SEMAPHORE)
SIDE_EFFECT = pltpu.SideEffectType.DATAFLOW_SIDE_EFFECTING


def _scatter_copies(x_refs, land_refs, send_sems, recv_sems):
    x, y, c = _coords()
    n = len(x_refs)
    return [pltpu.make_async_remote_copy(src_ref=x_refs[a].at[2 * px + py], dst_ref=land_refs[a].at[j],
                                         send_sem=send_sems[3 * a + j], recv_sem=recv_sems[3 * a + j],
                                         device_id=(px, py, c), device_id_type=MESH)
            for j, (px, py) in enumerate(_other_chips(x, y)) for a in range(n)]


def scatter_start(parts, name):
    n = len(parts)
    k = 3 * n
    lands = [lax.empty((3,) + p.shape[1:], p.dtype) for p in parts]

    def body(*refs):
        x_refs, land_refs = refs[:n], refs[n:2 * n]
        send_sems, recv_sems = refs[2 * n:2 * n + k], refs[2 * n + k:2 * n + 2 * k]
        token = refs[-1]
        for cp in _scatter_copies(x_refs, land_refs, send_sems, recv_sems):
            cp.start()
        token[...] = jnp.zeros_like(token)

    hbm = lambda a: pltpu.HBM(a.shape, a.dtype)
    res = pl.pallas_call(
        body, name=name, in_specs=[HBM] * (2 * n),
        out_specs=tuple([SEM] * (2 * k) + [HBM] * (2 * n) + [pl.BlockSpec(memory_space=pltpu.VMEM)]),
        out_shape=tuple([pltpu.SemaphoreType.DMA(())] * (2 * k) + [hbm(p) for p in parts] + [hbm(b) for b in lands]
                        + [jax.ShapeDtypeStruct((8, 128), F32)]),
        input_output_aliases={i: 2 * k + i for i in range(2 * n)},
        compiler_params=pltpu.CompilerParams(has_side_effects=SIDE_EFFECT, vmem_limit_bytes=VMEM_LIMIT))(
            *[pltpu.with_memory_space_constraint(a, pltpu.HBM) for a in list(parts) + lands])
    return list(res[:2 * k]), list(res[2 * k:2 * k + n]), list(res[2 * k + n:2 * k + 2 * n]), res[-1]


def scatter_wait(sems, parts_thru, lands_thru, after, name):
    n = len(parts_thru)
    k = 3 * n

    def body(*refs):
        x_refs, land_refs = refs[:n], refs[n:2 * n]
        send_sems, recv_sems = refs[2 * n:2 * n + k], refs[2 * n + k:2 * n + 2 * k]
        for cp in _scatter_copies(x_refs, land_refs, send_sems, recv_sems):
            cp.wait_send()
            cp.wait_recv()

    hbm = lambda a: pltpu.HBM(a.shape, a.dtype)
    res = pl.pallas_call(
        body, name=name, in_specs=[HBM] * (2 * n) + [SEM] * (2 * k) + [ANY],
        out_specs=tuple([HBM] * (2 * n)), out_shape=tuple(hbm(a) for a in list(parts_thru) + list(lands_thru)),
        input_output_aliases={i: i for i in range(2 * n)},
        compiler_params=pltpu.CompilerParams(has_side_effects=SIDE_EFFECT, vmem_limit_bytes=VMEM_LIMIT))(
            *parts_thru, *lands_thru, *sems, after)
    return list(res[n:])


def _row_block(rows):
    return max(b for b in range(16, 257, 16) if rows % b == 0)


def chip_sum(gpack, recv, core, name):
    n, R, W = gpack.shape
    rh = R // 2
    rb = _row_block(rh)
    nb = rh // rb

    def body(c_ref, g_ref, r_ref, o_ref, ob_ref):
        s = g_ref[...] + r_ref[...]
        o_ref[...] = s
        ob_ref[...] = s.astype(BF16)

    blk = pl.BlockSpec((1, rb, W), lambda i, j, c_ref: (i, j, 0))
    spec = pltpu.PrefetchScalarGridSpec(
        num_scalar_prefetch=1, grid=(n, nb),
        in_specs=[pl.BlockSpec((1, rb, W), lambda i, j, c_ref: (i, c_ref[0] * nb + j, 0)), blk],
        out_specs=(blk, blk))
    return pl.pallas_call(
        body, name=name, grid_spec=spec,
        out_shape=(jax.ShapeDtypeStruct((n, rh, W), F32), jax.ShapeDtypeStruct((n, rh, W), BF16)),
        compiler_params=_params(("arbitrary", "arbitrary")))(core, gpack, recv)


def chip_sum_cols(gate, up, recv_gate, recv_up, core, name):
    R, W = gate.shape
    cw = W // 2
    rh = R // 2
    rb = _row_block(rh)
    nb = rh // rb

    def body(c_ref, g_ref, u_ref, rg_ref, ru_ref, o_ref, ob_ref):
        s = jnp.where(pl.program_id(0) < 2, g_ref[...] + rg_ref[...], u_ref[...] + ru_ref[...])
        o_ref[0] = s
        ob_ref[0] = s.astype(BF16)

    gate_blk = lambda s, j: (jnp.where(s < 2, j, nb - 1), jnp.minimum(s, 1))
    up_blk = lambda s, j: (jnp.where(s < 2, 0, j), jnp.maximum(s - 2, 0))
    out = pl.BlockSpec((1, rb, cw), lambda s, j, c_ref: (s, j, 0))

    def own(blk):
        return lambda s, j, c_ref: (c_ref[0] * nb + blk(s, j)[0], blk(s, j)[1])

    def theirs(blk):
        return lambda s, j, c_ref: blk(s, j)

    spec = pltpu.PrefetchScalarGridSpec(
        num_scalar_prefetch=1, grid=(N_CHIPS, nb),
        in_specs=[pl.BlockSpec((rb, cw), own(gate_blk)), pl.BlockSpec((rb, cw), own(up_blk)),
                  pl.BlockSpec((rb, cw), theirs(gate_blk)), pl.BlockSpec((rb, cw), theirs(up_blk))],
        out_specs=(out, out))
    return pl.pallas_call(
        body, name=name, grid_spec=spec,
        out_shape=(jax.ShapeDtypeStruct((N_CHIPS, rh, cw), F32), jax.ShapeDtypeStruct((N_CHIPS, rh, cw), BF16)),
        compiler_params=_params(("arbitrary", "arbitrary")))(core, gate, up, recv_gate, recv_up)


def chip_sums_of(items, core, tag):
    parts = []
    for _, _, g in items:
        parts += list(g) if isinstance(g, tuple) else [g]
    swapped = iter(zip(parts, sibling_swap(parts, True, f"grad_swap_cores_{tag}")))
    core_arg = core.reshape(1).astype(jnp.int32)
    sums = []
    for n, l, g in items:
        if isinstance(g, tuple):
            (dwg, from_g), (dwu, from_u) = next(swapped), next(swapped)
            sums.append(chip_sum_cols(dwg, dwu, from_g, from_u, core_arg, f"grad_chip_sum_{n}{l}"))
        else:
            p, r = next(swapped)
            sums.append(chip_sum(p, r, core_arg, f"grad_chip_sum_{n}{l}"))
    return sums


def shard_sum(own, recv, name):
    R, W = own.shape
    rb = _row_block(R)

    def body(a_ref, r_ref, o_ref):
        acc = a_ref[...]
        for j in range(3):
            acc = acc + r_ref[j].astype(F32)
        o_ref[...] = acc

    return pl.pallas_call(
        body, name=name, grid=(R // rb,),
        in_specs=[pl.BlockSpec((rb, W), lambda i: (i, 0)), pl.BlockSpec((3, rb, W), lambda i: (0, i, 0))],
        out_specs=pl.BlockSpec((rb, W), lambda i: (i, 0)), out_shape=jax.ShapeDtypeStruct((R, W), F32),
        compiler_params=_params(("arbitrary",)))(own, recv)


def adamw(w, m, v, g, name):
    L, R, C = w.shape
    rb = max(b for b in range(8, 257, 8) if R % b == 0)
    bc1 = 1.0 - ADAM_B1 ** ADAM_STEP
    bc2 = 1.0 - ADAM_B2 ** ADAM_STEP

    def body(w_ref, m_ref, v_ref, g_ref, d_ref, nm_ref, nv_ref):
        gv = g_ref[...]
        nm = ADAM_B1 * m_ref[...] + (1.0 - ADAM_B1) * gv
        nv = ADAM_B2 * v_ref[...] + (1.0 - ADAM_B2) * (gv * gv)
        d_ref[...] = -ADAM_LR * ((nm / bc1) / (jnp.sqrt(nv / bc2) + ADAM_EPS) + ADAM_WD * w_ref[...])
        nm_ref[...] = nm
        nv_ref[...] = nv

    blk = pl.BlockSpec((1, rb, C), lambda l, i: (l, i, 0))
    sh = jax.ShapeDtypeStruct((L, R, C), F32)
    return pl.pallas_call(body, name=name, grid=(L, R // rb), in_specs=[blk] * 4, out_specs=(blk,) * 3,
                          out_shape=(sh, sh, sh), compiler_params=_params(("arbitrary", "arbitrary")))(w, m, v, g)


SMALL = [n for n in WEIGHTS if n not in BIG]


PACK_TILE = 8 * 128


def _pack(arrays):
    blocks = []
    for a in arrays:
        flat = a.reshape(-1)
        flat = jnp.pad(flat, (0, -flat.size % PACK_TILE))
        blocks.append(flat.reshape(-1, 128))
    rows = sum(b.shape[0] for b in blocks)
    if rows % 16:
        blocks.append(jnp.zeros((8, 128), arrays[0].dtype))
    return jnp.concatenate(blocks, axis=0)


def _unpack(pack, shapes):
    out, row = [], 0
    for shape in shapes:
        size = int(np.prod(shape))
        rows = -(-size // PACK_TILE) * 8
        out.append(pack[row:row + rows].reshape(-1)[:size].reshape(shape))
        row += rows
    return out


def _pad_lanes(a):
    return jnp.pad(a, ((0, 0), (0, -a.shape[1] % 128)))


def _local_shard(full, axis, chip):
    size = full.shape[axis] // N_CHIPS
    return lax.dynamic_slice_in_dim(full, chip * size, size, axis)


def kernel(x, ffn1_norm, ffn1_wi, ffn1_wo, mix_norm, w_in, w_out, lru_conv_w, lru_conv_b, lru_gate_a_w, lru_gate_a_b, lru_gate_x_w, lru_gate_x_b, lru_lambda, lru_out_norm, rwkv_mu, rwkv_w_up, rwkv_w_bias, rwkv_a_up, rwkv_a_bias, rwkv_g_up, rwkv_k_k, rwkv_k_a, rwkv_r_k, rwkv_ln_g, rwkv_ln_b, rwkv_vres_w1, rwkv_vres_w2, rwkv_vres_b, gdn_conv_w, gdn_a_log, gdn_dt_bias, gdn_norm, ffn2_norm, ffn2_wi, ffn2_wo, final_norm, loss_target, m_ffn1_norm, m_ffn1_wi, m_ffn1_wo, m_mix_norm, m_w_in, m_w_out, m_lru_conv_w, m_lru_conv_b, m_lru_gate_a_w, m_lru_gate_a_b, m_lru_gate_x_w, m_lru_gate_x_b, m_lru_lambda, m_lru_out_norm, m_rwkv_mu, m_rwkv_w_up, m_rwkv_w_bias, m_rwkv_a_up, m_rwkv_a_bias, m_rwkv_g_up, m_rwkv_k_k, m_rwkv_k_a, m_rwkv_r_k, m_rwkv_ln_g, m_rwkv_ln_b, m_rwkv_vres_w1, m_rwkv_vres_w2, m_rwkv_vres_b, m_gdn_conv_w, m_gdn_a_log, m_gdn_dt_bias, m_gdn_norm, m_ffn2_norm, m_ffn2_wi, m_ffn2_wo, m_final_norm, v_ffn1_norm, v_ffn1_wi, v_ffn1_wo, v_mix_norm, v_w_in, v_w_out, v_lru_conv_w, v_lru_conv_b, v_lru_gate_a_w, v_lru_gate_a_b, v_lru_gate_x_w, v_lru_gate_x_b, v_lru_lambda, v_lru_out_norm, v_rwkv_mu, v_rwkv_w_up, v_rwkv_w_bias, v_rwkv_a_up, v_rwkv_a_bias, v_rwkv_g_up, v_rwkv_k_k, v_rwkv_k_a, v_rwkv_r_k, v_rwkv_ln_g, v_rwkv_ln_b, v_rwkv_vres_w1, v_rwkv_vres_w2, v_rwkv_vres_b, v_gdn_conv_w, v_gdn_a_log, v_gdn_dt_bias, v_gdn_norm, v_ffn2_norm, v_ffn2_wi, v_ffn2_wo, v_final_norm):
    args = locals()
    w_loc = {n: args[n] for n in WEIGHTS}
    m_loc = {n: args['m_' + n] for n in WEIGHTS}
    v_loc = {n: args['v_' + n] for n in WEIGHTS}
    chip = 2 * lax.axis_index("x") + lax.axis_index("y")
    core = lax.axis_index("c")

    big = [(n, l) for n in BIG for l in range(N_LAYERS)]
    shards = {(n, l): _pad_lanes(w_loc[n][l].astype(BF16)) for n, l in big}
    sm_names = list(SMALL_SHARDED)
    placed = []
    for n in sm_names:
        mine = [jnp.where((chip == s) & (core == 0), w_loc[n], 0.0) for s in range(N_CHIPS)]
        placed.append(jnp.concatenate(mine, axis=SMALL_SHARDED[n]))
    first = [('ffn1_wi', 0), ('ffn1_wo', 0)]
    summed, gathered = allreduce_small(_pack(placed), "allgather_first", [shards[k] for k in first])
    wb = {n: [None] * N_LAYERS for n in BIG}
    for (n, l), g in zip(first, gathered):
        wb[n][l] = _natural(n, g)
    w_full = dict(w_loc)
    w_full.update(zip(sm_names, _unpack(summed, [p.shape for p in placed])))

    loss, dx, grads = local_step(x[0], loss_target[0], w_full, wb, shards)
    loss = lax.psum(loss, ("x", "y", "c"))

    gsum, _ = allreduce_small(_pack([grads[n] for n in SMALL]), "allreduce_small")
    g_loc = {}
    for n, g in zip(SMALL, _unpack(gsum, [grads[n].shape for n in SMALL])):
        g_loc[n] = _local_shard(g, SMALL_SHARDED[n], chip) if n in SMALL_SHARDED else g

    last_keys, last_parts = grads['last_round']
    sems, parts_thru, lands_thru, token = scatter_start(last_parts, "grad_scatter_last_start")
    chip_after_start = chip + token[0, 0].astype(chip.dtype)
    rows = {n: [None] * N_LAYERS for n in BIG}
    delta, new_m, new_v = {}, {}, {}

    def finish(keys, arrived, which_chip, tag):
        halves = [shard_sum(lax.dynamic_index_in_dim(grads[n][l][0], which_chip, 0, keepdims=False), got,
                            f"grad_shard_sum_{n}{l}") for (n, l), got in zip(keys, arrived)]
        others = sibling_swap(halves, False, f"grad_share_cores_{tag}")
        for (n, l), half, other in zip(keys, halves, others):
            lower = jnp.where(core == 0, half, other)
            upper = jnp.where(core == 0, other, half)
            rows[n][l] = jnp.concatenate([lower, upper], axis=0)[:, :w_loc[n].shape[-1]]
        for n in BIG:
            if n not in delta and all(r is not None for r in rows[n]):
                g_loc[n] = jnp.stack(rows[n])
                delta[n], new_m[n], new_v[n] = adamw(w_loc[n], m_loc[n], v_loc[n], g_loc[n], f"adamw_{n}")

    early = [k for k in big if k not in last_keys]
    finish(early, [grads[n][l][1] for n, l in early], chip_after_start, "early")
    pack = lambda d: _pack([d[n] for n in SMALL])[None]
    res = adamw(pack(w_loc), pack(m_loc), pack(v_loc), pack(g_loc), "adamw_small")
    for dst, r in zip((delta, new_m, new_v), res):
        dst.update(zip(SMALL, _unpack(r[0], [w_loc[n].shape for n in SMALL])))
    arrived_last = scatter_wait(sems, parts_thru, lands_thru, delta['ffn2_wi'], "grad_scatter_last_wait")
    finish(last_keys, arrived_last, chip, "last")

    return (loss, dx[None], *[g_loc[n] for n in WEIGHTS], *[delta[n] for n in WEIGHTS],
            *[new_m[n] for n in WEIGHTS], *[new_v[n] for n in WEIGHTS])
```

```python
import functools

import numpy as np
import jax
import jax.numpy as jnp
from jax import lax
from jax.experimental import pallas as pl
from jax.experimental.pallas import tpu as pltpu

F32 = jnp.float32
BF16 = jnp.bfloat16
MESH = pl.DeviceIdType.MESH

D_MODEL = 1024
D_FF = 2816
N_LAYERS = 2
HEADS = 6
HEAD_DIM = 64
MIX_W = HEADS * HEAD_DIM
LRU_W = 256
LRU_BLOCKS = 4
D_IN = 3468
D_IN_PAD = 3584
NORM_EPS = 1e-6
GN_EPS = 64e-5
LRU_C = 8.0
CHUNK = 64
CHUNKS_PER_STEP = 4
ROWS = 512
FF_CHUNK = 256
IN_CHUNK = 512
VMEM_LIMIT = 56 * 1024 * 1024

ADAM_LR, ADAM_B1, ADAM_B2, ADAM_EPS, ADAM_WD, ADAM_STEP = 0.001, 0.9, 0.999, 1e-08, 0.01, 10

WEIGHTS = ['ffn1_norm', 'ffn1_wi', 'ffn1_wo', 'mix_norm', 'w_in', 'w_out', 'lru_conv_w', 'lru_conv_b',
           'lru_gate_a_w', 'lru_gate_a_b', 'lru_gate_x_w', 'lru_gate_x_b', 'lru_lambda', 'lru_out_norm',
           'rwkv_mu', 'rwkv_w_up', 'rwkv_w_bias', 'rwkv_a_up', 'rwkv_a_bias', 'rwkv_g_up', 'rwkv_k_k',
           'rwkv_k_a', 'rwkv_r_k', 'rwkv_ln_g', 'rwkv_ln_b', 'rwkv_vres_w1', 'rwkv_vres_w2', 'rwkv_vres_b',
           'gdn_conv_w', 'gdn_a_log', 'gdn_dt_bias', 'gdn_norm', 'ffn2_norm', 'ffn2_wi', 'ffn2_wo', 'final_norm']
BIG = {'ffn1_wi': 2, 'ffn1_wo': 1, 'w_in': 2, 'w_out': 1, 'ffn2_wi': 2, 'ffn2_wo': 1}
SMALL_SHARDED = {'lru_conv_w': 2, 'rwkv_w_up': 2, 'rwkv_a_up': 2, 'rwkv_g_up': 2, 'rwkv_vres_w1': 1,
                 'rwkv_vres_w2': 2, 'gdn_conv_w': 2}
N_CHIPS = 4


def _params(sem=None):
    kw = dict(vmem_limit_bytes=VMEM_LIMIT)
    if sem is not None:
        kw['dimension_semantics'] = sem
    return pltpu.CompilerParams(**kw)


def _bdot(a, b, dims=(((1,), (0,)), ((), ()))):
    return lax.dot_general(a.astype(BF16), b.astype(BF16), dims, preferred_element_type=F32)


def _bdot_nt(a, b):
    return _bdot(a, b, (((1,), (1,)), ((), ())))


def _bdot_tn(a, b):
    return _bdot(a, b, (((0,), (0,)), ((), ())))


_DIMS = {'nn': (((1,), (0,)), ((), ())), 'nt': (((1,), (1,)), ((), ())), 'tn': (((0,), (0,)), ((), ()))}


def _split(a, terms):
    parts = []
    for _ in range(terms - 1):
        hi = a.astype(BF16)
        parts.append(hi)
        a = a - hi.astype(F32)
    parts.append(a.astype(BF16))
    return parts


_BATCH_DIMS = {'nn': (((2,), (1,)), ((0,), (0,))), 'nt': (((2,), (2,)), ((0,), (0,))),
               'tn': (((1,), (1,)), ((0,), (0,)))}


def _dot3(a, b, kind):
    ah, al = _split(a, 2)
    bh, bl = _split(b, 2)
    dims = _BATCH_DIMS[kind] if a.ndim == 3 else _DIMS[kind]
    d = lambda p, q: lax.dot_general(p, q, dims, preferred_element_type=F32)
    return d(ah, bh) + (d(ah, bl) + d(al, bh))


@functools.partial(jax.custom_vjp, nondiff_argnums=(2,))
def _cdot_k(a, b, kind):
    return _dot3(a, b, kind)


def _cdot_k_fwd(a, b, kind):
    return _dot3(a, b, kind), (a, b)


def _cdot_k_bwd(kind, res, ct):
    a, b = res
    if kind == 'nn':
        return _dot3(ct, b, 'nt'), _dot3(a, ct, 'tn')
    if kind == 'nt':
        return _dot3(ct, b, 'nn'), _dot3(ct, a, 'tn')
    return _dot3(b, ct, 'nt'), _dot3(a, ct, 'nn')


_cdot_k.defvjp(_cdot_k_fwd, _cdot_k_bwd)


def _dot1(a, b, kind):
    dims = _BATCH_DIMS[kind] if a.ndim == 3 else _DIMS[kind]
    return lax.dot_general(a.astype(BF16), b.astype(BF16), dims, preferred_element_type=F32)


@functools.partial(jax.custom_vjp, nondiff_argnums=(2,))
def _cdot1_k(a, b, kind):
    return _dot1(a, b, kind)


def _cdot1_k_fwd(a, b, kind):
    return _dot1(a, b, kind), (a, b)


def _cdot1_k_bwd(kind, res, ct):
    a, b = res
    if kind == 'nn':
        return _dot1(ct, b, 'nt'), _dot1(a, ct, 'tn')
    if kind == 'nt':
        return _dot1(ct, b, 'nn'), _dot1(ct, a, 'tn')
    return _dot1(b, ct, 'nt'), _dot1(a, ct, 'nn')


_cdot1_k.defvjp(_cdot1_k_fwd, _cdot1_k_bwd)


def _cdot(a, b):
    return _cdot1_k(a, b, 'nn')


def _cdot_nt(a, b):
    return _cdot1_k(a, b, 'nt')


def _cdot_tn(a, b):
    return _cdot1_k(a, b, 'tn')


def _hdot(a, b):
    return _cdot_k(a, b, 'nn')


def _dot_exact(x, m01, kind):
    d = lambda p: lax.dot_general(p, m01.astype(BF16), _DIMS[kind], preferred_element_type=F32)
    hi, mid, lo = _split(x, 3)
    return d(hi) + (d(mid) + d(lo))


@functools.partial(jax.custom_vjp, nondiff_argnums=(1,))
def _xdot(x, make_m):
    return _dot_exact(x, make_m(), 'nn')


def _xdot_fwd(x, make_m):
    return _dot_exact(x, make_m(), 'nn'), None


def _xdot_bwd(make_m, _, ct):
    return (_dot_exact(ct, make_m(), 'nt'),)


_xdot.defvjp(_xdot_fwd, _xdot_bwd)


def _iota2(n, m):
    return lax.broadcasted_iota(jnp.int32, (n, m), 0), lax.broadcasted_iota(jnp.int32, (n, m), 1)


def _head_blocks(w):
    ri, ci = _iota2(w, w)
    return (ri // HEAD_DIM == ci // HEAD_DIM).astype(F32)


def _segsum(x):
    return _xdot(x, functools.partial(_head_blocks, x.shape[-1]))


def _cumsum_rows(x):
    return _cumsum_k(x, x.shape[0])


@functools.partial(jax.custom_vjp, nondiff_argnums=(1,))
def _cumsum_k(x, n):
    return _lower_dot(x, n, False)


def _lower_dot(x, n, transpose):
    ri, ci = _iota2(n, n)
    m = ((ri <= ci) if transpose else (ri >= ci)).astype(BF16)
    d = lambda p: lax.dot_general(m, p, _DIMS['nn'], preferred_element_type=F32)
    hi, mid, lo = _split(x, 3)
    return d(hi) + (d(mid) + d(lo))


def _cumsum_k_fwd(x, n):
    return _lower_dot(x, n, False), None


def _cumsum_k_bwd(n, _, ct):
    return (_lower_dot(ct, n, True),)


_cumsum_k.defvjp(_cumsum_k_fwd, _cumsum_k_bwd)


def _rms(x, g):
    return x * lax.rsqrt(jnp.mean(x * x, axis=-1, keepdims=True) + NORM_EPS) * g


DENSE_ROWS = 1024


def _row_loop(n_rows, fn):
    rows = min(DENSE_ROWS, n_rows)

    def step(i, c):
        fn(pl.ds(pl.multiple_of(i * rows, rows), rows))
        return c
    lax.fori_loop(0, n_rows // rows, step, 0)


def ffn_fwd(x, g, wi, wo, name, hosted=()):
    T = x.shape[0]
    nj = D_FF // FF_CHUNK
    gather = ChipGather(list(hosted))
    n = gather.n

    def body(*refs):
        x_ref, g_ref, wg_ref, wu_ref, wo_ref = refs[:5]
        hx, o_ref, ho = refs[5:5 + n], refs[5 + n], refs[6 + n:6 + 2 * n]
        h_ref, acc_ref = refs[6 + 2 * n:8 + 2 * n]
        sems = refs[8 + 2 * n:]
        j = pl.program_id(0)

        @pl.when(j == 0)
        def _():
            gather.start(hx, ho, sems)

            def init(r):
                h_ref[r, :] = _rms(x_ref[r, :], g_ref[...]).astype(BF16)
                acc_ref[r, :] = jnp.zeros((r.size, D_MODEL), F32)
            _row_loop(T, init)

        def blk(r):
            hb = h_ref[r, :]
            gate = jnp.dot(hb, wg_ref[...], preferred_element_type=F32)
            up = jnp.dot(hb, wu_ref[...], preferred_element_type=F32)
            a = (gate * jax.nn.sigmoid(gate) * up).astype(BF16)
            acc_ref[r, :] += jnp.dot(a, wo_ref[...], preferred_element_type=F32)
        _row_loop(T, blk)

        @pl.when(j == nj - 2)
        def _():
            gather.relay(hx, ho, sems)

        @pl.when(j == nj - 1)
        def _():
            def fin(r):
                o_ref[r, :] = x_ref[r, :] + 0.5 * acc_ref[r, :]
            _row_loop(T, fin)
            gather.finish(hx, ho, sems)

    full = pl.BlockSpec((T, D_MODEL), lambda j: (0, 0))
    res = pl.pallas_call(
        body, name=name, grid=(nj,),
        in_specs=[full, pl.BlockSpec((1, D_MODEL), lambda j: (0, 0)),
                  pl.BlockSpec((D_MODEL, FF_CHUNK), lambda j: (0, j)),
                  pl.BlockSpec((D_MODEL, FF_CHUNK), lambda j: (0, j + nj)),
                  pl.BlockSpec((FF_CHUNK, D_MODEL), lambda j: (j, 0))] + gather.in_specs,
        out_specs=tuple([full] + gather.out_specs),
        out_shape=tuple([jax.ShapeDtypeStruct((T, D_MODEL), F32)] + gather.out_shape),
        scratch_shapes=[pltpu.VMEM((T, D_MODEL), BF16), pltpu.VMEM((T, D_MODEL), F32)] + gather.scratch,
        compiler_params=_params(("arbitrary",)))(x, g, wi, wi, wo, *hosted)
    return res[0], list(res[1:])


def _norm_bwd_rows(x, g, dh, dres):
    rstd = lax.rsqrt(jnp.mean(x * x, axis=-1, keepdims=True) + NORM_EPS)
    xh = x * rstd
    dxh = dh * g
    dx = rstd * (dxh - xh * jnp.mean(dxh * xh, axis=-1, keepdims=True))
    return dres + dx, jnp.sum(dh * xh, axis=0, keepdims=True)


def ffn_bwd(x, dy, g, wi, wo, name, hosted=()):
    T = x.shape[0]
    nj = D_FF // FF_CHUNK
    scatter = ChipScatter(list(hosted))
    n = scatter.n

    def body(*refs):
        x_ref, dy_ref, g_ref, wg_ref, wu_ref, wo_ref = refs[:6]
        hx = refs[6:6 + n]
        dx_ref, dg_ref, dwg_ref, dwu_ref, dwo_ref = refs[6 + n:11 + n]
        ho = refs[11 + n:11 + 2 * n]
        h_ref, da_ref, dh_ref = refs[11 + 2 * n:14 + 2 * n]
        sems = refs[14 + 2 * n:]
        j = pl.program_id(0)

        @pl.when(j == 0)
        def _():
            scatter.start(hx, ho, sems)

            def init(r):
                h_ref[r, :] = _rms(x_ref[r, :], g_ref[...]).astype(BF16)
                da_ref[r, :] = (0.5 * dy_ref[r, :]).astype(BF16)
                dh_ref[r, :] = jnp.zeros((r.size, D_MODEL), F32)
            _row_loop(T, init)

        dwg_ref[...] = jnp.zeros_like(dwg_ref)
        dwu_ref[...] = jnp.zeros_like(dwu_ref)
        dwo_ref[...] = jnp.zeros_like(dwo_ref)

        def blk(r):
            hb = h_ref[r, :]
            db = da_ref[r, :]
            gate = jnp.dot(hb, wg_ref[...], preferred_element_type=F32)
            up = jnp.dot(hb, wu_ref[...], preferred_element_type=F32)
            sg = jax.nn.sigmoid(gate)
            sl = gate * sg
            da = _bdot_nt(db, wo_ref[...])
            dup = (da * sl).astype(BF16)
            dgate = (da * up * (sg * (1.0 + gate * (1.0 - sg)))).astype(BF16)
            dwo_ref[...] += _bdot_tn((sl * up).astype(BF16), db)
            dwg_ref[...] += _bdot_tn(hb, dgate)
            dwu_ref[...] += _bdot_tn(hb, dup)
            dh_ref[r, :] += _bdot_nt(dgate, wg_ref[...]) + _bdot_nt(dup, wu_ref[...])
        _row_loop(T, blk)

        @pl.when(j == nj - 1)
        def _():
            dg_ref[...] = jnp.zeros_like(dg_ref)

            def fin(r):
                dx, dg = _norm_bwd_rows(x_ref[r, :], g_ref[...], dh_ref[r, :], dy_ref[r, :])
                dx_ref[r, :] = dx
                dg_ref[...] += dg
            _row_loop(T, fin)
            scatter.finish(hx, ho, sems)

    full = pl.BlockSpec((T, D_MODEL), lambda j: (0, 0))
    vec = pl.BlockSpec((1, D_MODEL), lambda j: (0, 0))
    res = pl.pallas_call(
        body, name=name, grid=(nj,),
        in_specs=[full, full, vec,
                  pl.BlockSpec((D_MODEL, FF_CHUNK), lambda j: (0, j)),
                  pl.BlockSpec((D_MODEL, FF_CHUNK), lambda j: (0, j + nj)),
                  pl.BlockSpec((FF_CHUNK, D_MODEL), lambda j: (j, 0))] + scatter.in_specs,
        out_specs=tuple([full, vec,
                         pl.BlockSpec((D_MODEL, FF_CHUNK), lambda j: (0, j)),
                         pl.BlockSpec((D_MODEL, FF_CHUNK), lambda j: (0, j)),
                         pl.BlockSpec((FF_CHUNK, D_MODEL), lambda j: (j, 0))] + scatter.out_specs),
        out_shape=tuple([jax.ShapeDtypeStruct((T, D_MODEL), F32), jax.ShapeDtypeStruct((1, D_MODEL), F32),
                         jax.ShapeDtypeStruct((D_MODEL, D_FF), F32), jax.ShapeDtypeStruct((D_MODEL, D_FF), F32),
                         jax.ShapeDtypeStruct((D_FF, D_MODEL), F32)] + scatter.out_shape),
        scratch_shapes=[pltpu.VMEM((T, D_MODEL), BF16), pltpu.VMEM((T, D_MODEL), BF16),
                        pltpu.VMEM((T, D_MODEL), F32)] + scatter.scratch,
        compiler_params=_params(("arbitrary",)))(x, dy, g, wi, wi, wo, *hosted)
    return res[0], res[1], res[2], res[3], res[4], list(res[5:])


def proj_fwd(x, g, w, name):
    T = x.shape[0]
    nj = D_IN_PAD // IN_CHUNK

    def body(x_ref, g_ref, w_ref, o_ref, h_ref):
        @pl.when(pl.program_id(0) == 0)
        def _():
            def init(r):
                h_ref[r, :] = _rms(x_ref[r, :], g_ref[...]).astype(BF16)
            _row_loop(T, init)

        def blk(r):
            o_ref[r, :] = jnp.dot(h_ref[r, :], w_ref[...], preferred_element_type=F32)
        _row_loop(T, blk)

    return pl.pallas_call(
        body, name=name, grid=(nj,),
        in_specs=[pl.BlockSpec((T, D_MODEL), lambda j: (0, 0)), pl.BlockSpec((1, D_MODEL), lambda j: (0, 0)),
                  pl.BlockSpec((D_MODEL, IN_CHUNK), lambda j: (0, j))],
        out_specs=pl.BlockSpec((T, IN_CHUNK), lambda j: (0, j)),
        out_shape=jax.ShapeDtypeStruct((T, D_IN_PAD), F32),
        scratch_shapes=[pltpu.VMEM((T, D_MODEL), BF16)],
        compiler_params=_params(("arbitrary",)))(x, g, w)


def proj_bwd(x, dres, g, w, dp, name):
    T = x.shape[0]
    nj = D_IN_PAD // IN_CHUNK

    def body(x_ref, dres_ref, g_ref, w_ref, dp_ref, dx_ref, dg_ref, dw_ref, h_ref, dh_ref):
        j = pl.program_id(0)

        @pl.when(j == 0)
        def _():
            def init(r):
                h_ref[r, :] = _rms(x_ref[r, :], g_ref[...]).astype(BF16)
                dh_ref[r, :] = jnp.zeros((r.size, D_MODEL), F32)
            _row_loop(T, init)

        dw_ref[...] = jnp.zeros_like(dw_ref)

        def blk(r):
            dpb = dp_ref[r, :].astype(BF16)
            dw_ref[...] += _bdot_tn(h_ref[r, :], dpb)
            dh_ref[r, :] += _bdot_nt(dpb, w_ref[...])
        _row_loop(T, blk)

        @pl.when(j == nj - 1)
        def _():
            dg_ref[...] = jnp.zeros_like(dg_ref)

            def fin(r):
                dx, dg = _norm_bwd_rows(x_ref[r, :], g_ref[...], dh_ref[r, :], dres_ref[r, :])
                dx_ref[r, :] = dx
                dg_ref[...] += dg
            _row_loop(T, fin)

    full = pl.BlockSpec((T, D_MODEL), lambda j: (0, 0))
    vec = pl.BlockSpec((1, D_MODEL), lambda j: (0, 0))
    return pl.pallas_call(
        body, name=name, grid=(nj,),
        in_specs=[full, full, vec, pl.BlockSpec((D_MODEL, IN_CHUNK), lambda j: (0, j)),
                  pl.BlockSpec((T, IN_CHUNK), lambda j: (0, j))],
        out_specs=(full, vec, pl.BlockSpec((D_MODEL, IN_CHUNK), lambda j: (0, j))),
        out_shape=(jax.ShapeDtypeStruct((T, D_MODEL), F32), jax.ShapeDtypeStruct((1, D_MODEL), F32),
                   jax.ShapeDtypeStruct((D_MODEL, D_IN_PAD), F32)),
        scratch_shapes=[pltpu.VMEM((T, D_MODEL), BF16), pltpu.VMEM((T, D_MODEL), F32)],
        compiler_params=_params(("arbitrary",)))(x, dres, g, w, dp)


def out_fwd(mixed, w, x, name):
    T = x.shape[0]

    def body(m_ref, w_ref, x_ref, o_ref):
        o_ref[...] = x_ref[...] + jnp.dot(m_ref[...].astype(BF16), w_ref[...], preferred_element_type=F32)

    blk = pl.BlockSpec((ROWS, D_MODEL), lambda i: (i, 0))
    return pl.pallas_call(
        body, name=name, grid=(T // ROWS,),
        in_specs=[blk, pl.BlockSpec((D_MODEL, D_MODEL), lambda i: (0, 0)), blk],
        out_specs=blk, out_shape=jax.ShapeDtypeStruct((T, D_MODEL), F32),
        compiler_params=_params(("arbitrary",)))(mixed, w, x)


def out_bwd(mixed, w, dy, name):
    T = dy.shape[0]

    def body(m_ref, w_ref, dy_ref, dm_ref, dw_ref):
        @pl.when(pl.program_id(0) == 0)
        def _():
            dw_ref[...] = jnp.zeros_like(dw_ref)
        dyb = dy_ref[...].astype(BF16)
        dm_ref[...] = _bdot_nt(dyb, w_ref[...])
        dw_ref[...] += _bdot_tn(m_ref[...].astype(BF16), dyb)

    blk = pl.BlockSpec((ROWS, D_MODEL), lambda i: (i, 0))
    sq = pl.BlockSpec((D_MODEL, D_MODEL), lambda i: (0, 0))
    return pl.pallas_call(
        body, name=name, grid=(T // ROWS,),
        in_specs=[blk, sq, blk], out_specs=(blk, sq),
        out_shape=(jax.ShapeDtypeStruct((T, D_MODEL), F32), jax.ShapeDtypeStruct((D_MODEL, D_MODEL), F32)),
        compiler_params=_params(("arbitrary",)))(mixed, w, dy)


def loss_head(x, g, target, name):
    T = x.shape[0]

    def body(x_ref, g_ref, t_ref, loss_ref, dx_ref, dg_ref):
        @pl.when(pl.program_id(0) == 0)
        def _():
            loss_ref[...] = jnp.zeros_like(loss_ref)
            dg_ref[...] = jnp.zeros_like(dg_ref)
        xb = x_ref[...]
        rstd = lax.rsqrt(jnp.mean(xb * xb, axis=-1, keepdims=True) + NORM_EPS)
        xh = xb * rstd
        err = xh * g_ref[...] - t_ref[...]
        loss_ref[...] += 0.5 * jnp.sum(jnp.mean(err * err, axis=-1, keepdims=True), axis=0, keepdims=True)
        dy = err * (1.0 / D_MODEL)
        dg_ref[...] += jnp.sum(dy * xh, axis=0, keepdims=True)
        dxh = dy * g_ref[...]
        dx_ref[...] = rstd * (dxh - xh * jnp.mean(dxh * xh, axis=-1, keepdims=True))

    blk = pl.BlockSpec((ROWS, D_MODEL), lambda i: (i, 0))
    vec = pl.BlockSpec((1, D_MODEL), lambda i: (0, 0))
    return pl.pallas_call(
        body, name=name, grid=(T // ROWS,),
        in_specs=[blk, vec, blk], out_specs=(pl.BlockSpec((1, 1), lambda i: (0, 0)), blk, vec),
        out_shape=(jax.ShapeDtypeStruct((1, 1), F32), jax.ShapeDtypeStruct((T, D_MODEL), F32),
                   jax.ShapeDtypeStruct((1, D_MODEL), F32)),
        compiler_params=_params(("arbitrary",)))(x, g, target)


def rowwise_fwd(fn, rows, shared, out_widths, name):
    T = rows[0].shape[0]
    n_in = len(rows) + len(shared)

    def body(*refs):
        res = fn(*[r[...] for r in refs[:n_in]])
        for o, v in zip(refs[n_in:], res):
            o[...] = v

    in_specs = ([pl.BlockSpec((ROWS, a.shape[1]), lambda i: (i, 0)) for a in rows]
                + [pl.BlockSpec(a.shape, lambda i: (0, 0)) for a in shared])
    return pl.pallas_call(
        body, name=name, grid=(T // ROWS,), in_specs=in_specs,
        out_specs=tuple(pl.BlockSpec((ROWS, w), lambda i: (i, 0)) for w in out_widths),
        out_shape=tuple(jax.ShapeDtypeStruct((T, w), F32) for w in out_widths),
        compiler_params=_params(("arbitrary",)))(*rows, *shared)


def rowwise_bwd(fn, rows, shared, cts, name, ct_fn=None):
    T = rows[0].shape[0]
    nr, ns, nc = len(rows), len(shared), len(cts)

    def body(*refs):
        ins = [r[...] for r in refs[:nr + ns]]
        ctv = tuple(r[...] for r in refs[nr + ns:nr + ns + nc])
        outs = refs[nr + ns + nc:]
        _, vjp = jax.vjp(fn, *ins)
        grads = vjp(ct_fn(*ctv) if ct_fn is not None else ctv)
        for k in range(nr):
            outs[k][...] = grads[k]

        @pl.when(pl.program_id(0) == 0)
        def _():
            for k in range(ns):
                outs[nr + k][...] = jnp.zeros_like(outs[nr + k])
        for k in range(ns):
            outs[nr + k][...] += grads[nr + k]

    row_spec = lambda a: pl.BlockSpec((ROWS, a.shape[1]), lambda i: (i, 0))
    sh_spec = lambda a: pl.BlockSpec(a.shape, lambda i: (0, 0))
    return pl.pallas_call(
        body, name=name, grid=(T // ROWS,),
        in_specs=[row_spec(a) for a in rows] + [sh_spec(a) for a in shared] + [row_spec(a) for a in cts],
        out_specs=tuple([row_spec(a) for a in rows] + [sh_spec(a) for a in shared]),
        out_shape=tuple(jax.ShapeDtypeStruct(a.shape, F32) for a in list(rows) + list(shared)),
        compiler_params=_params(("arbitrary",)))(*rows, *shared, *cts)


def shift_rows(x, s):
    return jnp.pad(x, ((s, 0), (0, 0)))[:x.shape[0]]


def unshift_rows(x, s):
    return jnp.pad(x, ((0, s), (0, 0)))[s:]


def _neg_expm1(y):
    series = -(y * (1.0 + y * (0.5 + y * (1.0 / 6.0 + y * (1.0 / 24.0)))))
    return jnp.where(y > -0.05, series, 1.0 - jnp.exp(y))


def lru_pre_fn(x0, x1, x2, x3, first, w0, w1, w2, w3, cb, ga, gab, gx, gxb, lam):
    xc = w3 * x0 + w2 * x1 + w1 * x2 + w0 * x3 + cb
    r = jax.nn.sigmoid(_hdot(xc, ga) + gab)
    i = jax.nn.sigmoid(_hdot(xc, gx) + gxb)
    log_a = -LRU_C * r * jax.nn.softplus(-lam)
    a = jnp.exp(log_a)
    mult = jnp.where(first > 0.5, 1.0, jnp.sqrt(_neg_expm1(2.0 * log_a)))
    return a, mult * i * xc


def lru_post_fn(h, py, og):
    return (_rms(h * jax.nn.gelu(py), og),)


def lru_scan(a, b, reverse, name):
    T, C = a.shape
    nb = T // 8

    def body(a_ref, b_ref, h_ref):
        rows = lax.broadcasted_iota(jnp.int32, (8, C), 0)

        def blk(i, carry):
            j = nb - 1 - i if reverse else i
            r = pl.ds(pl.multiple_of(j * 8, 8), 8)
            A = a_ref[r, :]
            B = b_ref[r, :]
            for s in (1, 2, 4):
                if reverse:
                    keep = rows < 8 - s
                    sh = 8 - s
                else:
                    keep = rows >= s
                    sh = s
                Bs = jnp.where(keep, pltpu.roll(B, sh, 0), 0.0)
                As = jnp.where(keep, pltpu.roll(A, sh, 0), 1.0)
                B = B + A * Bs
                A = A * As
            hb = B + A * carry
            h_ref[r, :] = hb
            edge = 0 if reverse else 7
            return jnp.sum(jnp.where(rows == edge, hb, 0.0), axis=0, keepdims=True)

        lax.fori_loop(0, nb, blk, jnp.zeros((1, C), F32))

    full = pl.BlockSpec((T, C), lambda: (0, 0))
    return pl.pallas_call(body, name=name, in_specs=[full, full], out_specs=full,
                          out_shape=jax.ShapeDtypeStruct((T, C), F32), compiler_params=_params())(a, b)


def make_rwkv_pre_fn(has_vres):
    def fn(p, pp, *rest):
        if has_vres:
            vf, mu, w_up, w_b, a_up, a_b, g_up, kk_w, ka_w, vw1, vw2, vb = rest
        else:
            mu, w_up, w_b, a_up, a_b, g_up, kk_w, ka_w = rest
        xm = p + (pp - p) * mu
        r, k, v = xm[:, 0:384], xm[:, 384:768], xm[:, 768:1152]
        xw, xa, xg = xm[:, 1152:1216], xm[:, 1216:1280], xm[:, 1280:1408]
        w_log = -jax.nn.softplus(-(w_b + _hdot(jnp.tanh(xw), w_up))) - 0.5
        lw = -jnp.exp(w_log)
        a = jax.nn.sigmoid(a_b + _hdot(xa, a_up))
        g = _hdot(jax.nn.sigmoid(xg), g_up)
        if has_vres:
            v = v + (vf - v) * jax.nn.sigmoid(vb + _hdot(_hdot(v, vw1), vw2))
        kkx = k * kk_w
        kk = kkx * lax.rsqrt(_segsum(kkx * kkx) + 1e-6)
        k2 = k * (1.0 + (a - 1.0) * ka_w)
        return r, lw, k2, v, kk, a, g
    return fn


def rwkv_post_fn(y, r, k2, v, g, ln_g, ln_b, r_k):
    mean = _segsum(y) * (1.0 / HEAD_DIM)
    yc = y - mean
    var = _segsum(yc * yc) * (1.0 / HEAD_DIM)
    yn = yc * lax.rsqrt(var + GN_EPS) * ln_g + ln_b
    bonus = _segsum(r * k2 * r_k) * v
    return ((yn + bonus) * g,)


def _head_expander(first_lane):
    ri, ci = _iota2(128, MIX_W)
    return (ri == ci // HEAD_DIM + first_lane).astype(F32)


def gdn_pre_fn(x0, x1, x2, x3, ab, w0, w1, w2, w3, alog, dtb):
    qkv = jax.nn.silu(w3 * x0 + w2 * x1 + w1 * x2 + w0 * x3)
    q, k, v = qkv[:, 0:384], qkv[:, 384:768], qkv[:, 768:1152]
    q = q * lax.rsqrt(_segsum(q * q) + 1e-6) * (HEAD_DIM ** -0.5)
    k = k * lax.rsqrt(_segsum(k * k) + 1e-6)
    g = -jnp.exp(alog) * jax.nn.softplus(ab + dtb)
    beta = jax.nn.sigmoid(ab)
    ge = _xdot(g, functools.partial(_head_expander, 0))
    be = _xdot(beta, functools.partial(_head_expander, HEADS))
    return q, k, v, ge, be


def gdn_post_fn(o, z, ng):
    ms = _segsum(o * o) * (1.0 / HEAD_DIM)
    return (o * lax.rsqrt(ms + NORM_EPS) * ng * jax.nn.silu(z),)


def _neumann_inv(m):
    n = m.shape[-1]
    ri, ci = _iota2(n, n)
    eye = (ri == ci).astype(F32)
    md = jnp.where(ri // 16 == ci // 16, m, 0.0)
    mo = m - md
    t0 = eye + md
    p2 = _hdot(md, md)
    t0 = t0 + _hdot(t0, p2)
    p4 = _hdot(p2, p2)
    t0 = t0 + _hdot(t0, p4)
    p8 = _hdot(p4, p4)
    t0 = t0 + _hdot(t0, p8)
    nn = _hdot(t0, mo)
    n2 = _hdot(nn, nn)
    t1 = eye + nn + n2 + _hdot(nn, n2)
    return _hdot(t1, t0)


@jax.custom_vjp
def _inv_saved(m, t_saved):
    return t_saved


def _inv_saved_fwd(m, t_saved):
    return t_saved, t_saved


def _inv_saved_bwd(t_saved, dt):
    tt = jnp.swapaxes(t_saved, -1, -2)
    return _hdot(_hdot(tt, dt), tt), jnp.zeros_like(t_saved)


_inv_saved.defvjp(_inv_saved_fwd, _inv_saved_bwd)


def _heads(x):
    return jnp.concatenate([x[None, :, h * HEAD_DIM:(h + 1) * HEAD_DIM] for h in range(HEADS)], axis=0)


def _unheads(y):
    return jnp.concatenate([lax.index_in_dim(y, h, 0, keepdims=False) for h in range(HEADS)], axis=1)


def rwkv_heads(s0, r, lw, k2, v, kk, a, inv):
    n = r.shape[0]
    ri, ci = _iota2(n, n)
    low, strict = ri >= ci, ri > ci
    cs = _cumsum_rows(lw)
    cl = jnp.sum(lw, axis=0, keepdims=True)
    p_in, p_prev, p_inv = jnp.exp(cs), jnp.exp(cs - lw), jnp.exp(-cs)
    p_rest, p_all = jnp.exp(cl - cs), jnp.exp(cl)
    bd = kk * a
    at, rt = _heads(-kk * p_prev), _heads(r * p_in)
    bh, kh = _heads(bd * p_inv), _heads(k2 * p_inv)
    vh = _heads(v)
    m_ab = jnp.where(strict, _cdot_nt(at, bh), 0.0)
    m_ak = jnp.where(strict, _cdot_nt(at, kh), 0.0)
    m_rb = jnp.where(low, _cdot_nt(rt, bh), 0.0)
    m_rk = jnp.where(low, _cdot_nt(rt, kh), 0.0)
    sa = _cdot(inv(m_ab), _cdot_nt(at, s0) + _cdot(m_ak, vh))
    y = _cdot_nt(rt, s0) + _cdot(m_rb, sa) + _cdot(m_rk, vh)
    s1 = s0 * _heads(p_all) + _cdot_tn(sa, _heads(bd * p_rest)) + _cdot_tn(vh, _heads(k2 * p_rest))
    return _unheads(y), s1


def gdn_heads(s0, q, k, v, ge, be, inv):
    n = q.shape[0]
    ri, ci = _iota2(n, n)
    low, strict = ri >= ci, ri > ci
    gc = _cumsum_rows(ge)
    gl = jnp.sum(ge, axis=0, keepdims=True)
    gch = _heads(gc)
    decay = jnp.where(low, jnp.exp(jnp.where(low, gch - jnp.swapaxes(gch, 1, 2), 0.0)), 0.0)
    kb = k * be
    e = jnp.exp(gc)
    kh = _heads(k)
    m = -jnp.where(strict, _cdot_nt(_heads(kb), kh) * decay, 0.0)
    mr = jnp.where(low, _cdot_nt(_heads(q), kh) * decay, 0.0)
    u = _cdot(inv(m), _heads(v * be) - _cdot_nt(_heads(kb * e), s0))
    y = _cdot_nt(_heads(q * e), s0) + _cdot(mr, u)
    s1 = s0 * _heads(jnp.exp(gl)) + _cdot_tn(u, _heads(k * jnp.exp(gl - gc)))
    return _unheads(y), s1


def core_fwd(heads_fn, ins, name, hosted=()):
    T = ins[0].shape[0]
    nc = T // CHUNK
    steps = nc // CHUNKS_PER_STEP
    n = len(ins)
    gather = ChipGather(list(hosted))
    ng = gather.n

    def body(*refs):
        hx = refs[n:n + ng]
        y_ref, s0_ref, t_ref = refs[n + ng:n + ng + 3]
        ho = refs[n + ng + 3:n + 2 * ng + 3]
        s_ref = refs[n + 2 * ng + 3]
        sems = refs[n + 2 * ng + 4:]
        c = pl.program_id(0)

        @pl.when(c == 0)
        def _():
            gather.start(hx, ho, sems)
            s_ref[...] = jnp.zeros_like(s_ref)

        state = s_ref[...]
        for u in range(CHUNKS_PER_STEP):
            rows = slice(u * CHUNK, (u + 1) * CHUNK)
            kept = []

            def inv(m):
                kept.append(_neumann_inv(m))
                return kept[0]

            y, after = heads_fn(state, *[r[rows, :] for r in refs[:n]], inv)
            y_ref[rows, :] = y
            s0_ref[u] = state
            t_ref[u] = kept[0]
            state = after
        s_ref[...] = state

        @pl.when(c == max(steps - 2, 0))
        def _():
            gather.relay(hx, ho, sems)

        @pl.when(c == steps - 1)
        def _():
            gather.finish(hx, ho, sems)

    row = pl.BlockSpec((CHUNKS_PER_STEP * CHUNK, MIX_W), lambda c: (c, 0))
    st_shape = (HEADS, HEAD_DIM, HEAD_DIM)
    st = pl.BlockSpec((CHUNKS_PER_STEP,) + st_shape, lambda c: (c, 0, 0, 0))
    res = pl.pallas_call(
        body, name=name, grid=(steps,), in_specs=[row] * n + gather.in_specs,
        out_specs=tuple([row, st, st] + gather.out_specs),
        out_shape=tuple([jax.ShapeDtypeStruct((T, MIX_W), F32), jax.ShapeDtypeStruct((nc,) + st_shape, F32),
                         jax.ShapeDtypeStruct((nc,) + st_shape, F32)] + gather.out_shape),
        scratch_shapes=[pltpu.VMEM(st_shape, F32)] + gather.scratch,
        compiler_params=_params(("arbitrary",)))(*ins, *hosted)
    return res[0], res[1], res[2], list(res[3:])


def core_bwd(heads_fn, ins, s0_all, t_all, dy, name):
    T = ins[0].shape[0]
    nc = T // CHUNK
    steps = nc // CHUNKS_PER_STEP
    n = len(ins)

    def body(*refs):
        s0_ref, t_ref, dy_ref = refs[n:n + 3]
        outs = refs[n + 3:n + 3 + n]
        ds_ref = refs[n + 3 + n]

        @pl.when(pl.program_id(0) == 0)
        def _():
            ds_ref[...] = jnp.zeros_like(ds_ref)

        d_state = ds_ref[...]
        for u in reversed(range(CHUNKS_PER_STEP)):
            rows = slice(u * CHUNK, (u + 1) * CHUNK)
            t_saved = t_ref[u]
            f = lambda s0, *xs: heads_fn(s0, *xs, lambda m: _inv_saved(m, t_saved))
            _, vjp = jax.vjp(f, s0_ref[u], *[r[rows, :] for r in refs[:n]])
            grads = vjp((dy_ref[rows, :], d_state))
            d_state = grads[0]
            for k in range(n):
                outs[k][rows, :] = grads[1 + k]
        ds_ref[...] = d_state

    row = pl.BlockSpec((CHUNKS_PER_STEP * CHUNK, MIX_W), lambda c: (steps - 1 - c, 0))
    st_shape = (HEADS, HEAD_DIM, HEAD_DIM)
    st = pl.BlockSpec((CHUNKS_PER_STEP,) + st_shape, lambda c: (steps - 1 - c, 0, 0, 0))
    return pl.pallas_call(
        body, name=name, grid=(steps,), in_specs=[row] * n + [st, st, row], out_specs=tuple([row] * n),
        out_shape=tuple(jax.ShapeDtypeStruct((T, MIX_W), F32) for _ in range(n)),
        scratch_shapes=[pltpu.VMEM(st_shape, F32)],
        compiler_params=_params(("arbitrary",)))(*ins, s0_all, t_all, dy)


def _block_diag(w):
    out = jnp.zeros((LRU_W, LRU_W), w.dtype)
    for n in range(LRU_BLOCKS):
        out = lax.dynamic_update_slice(out, w[n], (n * 64, n * 64))
    return out


def _block_diag_grad(g):
    return jnp.stack([g[n * 64:(n + 1) * 64, n * 64:(n + 1) * 64] for n in range(LRU_BLOCKS)])


def _row(v):
    return v.reshape(1, -1)


def _pad128(v):
    return jnp.pad(v.reshape(1, -1), ((0, 0), (0, 128 - v.size)))


def _layer_shared(w, l):
    cw = w['lru_conv_w'][l]
    lru_pre = [_row(cw[0]), _row(cw[1]), _row(cw[2]), _row(cw[3]), _row(w['lru_conv_b'][l]),
               _block_diag(w['lru_gate_a_w'][l]), _row(w['lru_gate_a_b'][l]),
               _block_diag(w['lru_gate_x_w'][l]), _row(w['lru_gate_x_b'][l]), _row(w['lru_lambda'][l])]
    rw_pre = [_row(w['rwkv_mu'][l]), w['rwkv_w_up'][l], _row(w['rwkv_w_bias'][l]), w['rwkv_a_up'][l],
              _row(w['rwkv_a_bias'][l]), w['rwkv_g_up'][l], _row(w['rwkv_k_k'][l]), _row(w['rwkv_k_a'][l])]
    if l > 0:
        rw_pre += [w['rwkv_vres_w1'][l - 1], w['rwkv_vres_w2'][l - 1], _row(w['rwkv_vres_b'][l - 1])]
    rw_post = [_row(w['rwkv_ln_g'][l]), _row(w['rwkv_ln_b'][l]), _row(w['rwkv_r_k'][l])]
    gw = w['gdn_conv_w'][l]
    gdn_pre = [_row(gw[0]), _row(gw[1]), _row(gw[2]), _row(gw[3]), _pad128(w['gdn_a_log'][l]),
               _pad128(w['gdn_dt_bias'][l])]
    gdn_post = [_row(jnp.tile(w['gdn_norm'][l], HEADS))]
    return dict(lru_pre=lru_pre, lru_post=[_row(w['lru_out_norm'][l])], rw_pre=rw_pre, rw_post=rw_post,
                gdn_pre=gdn_pre, gdn_post=gdn_post)


def _mixer_fwd(p, sh, l, v_first, host_rwkv=(), host_gdn=()):
    T = p.shape[0]
    lx, ly = p[:, 0:256], p[:, 256:512]
    prw, qkv, z, ab = p[:, 512:1920], p[:, 1920:3072], p[:, 3072:3456], p[:, 3456:3584]
    first = (lax.broadcasted_iota(jnp.int32, (T, LRU_W), 0) == 0).astype(F32)
    lru_rows = [lx, shift_rows(lx, 1), shift_rows(lx, 2), shift_rows(lx, 3), first]
    a, b = rowwise_fwd(lru_pre_fn, lru_rows, sh['lru_pre'], (LRU_W, LRU_W), f"lru_pre_fwd{l}")
    hseq = lru_scan(a, b, False, f"lru_scan_fwd{l}")
    (y_lru,) = rowwise_fwd(lru_post_fn, [hseq, ly], sh['lru_post'], (LRU_W,), f"lru_post_fwd{l}")

    rw_rows = [prw, shift_rows(prw, 1)] + ([v_first] if l > 0 else [])
    rw = rowwise_fwd(make_rwkv_pre_fn(l > 0), rw_rows, sh['rw_pre'], (MIX_W,) * 7, f"rwkv_pre_fwd{l}")
    r, lw, k2, v, kk, ar, g = rw
    y_raw, rs0, rt, got_rwkv = core_fwd(rwkv_heads, [r, lw, k2, v, kk, ar], f"rwkv_core_fwd{l}", host_rwkv)
    (y_rw,) = rowwise_fwd(rwkv_post_fn, [y_raw, r, k2, v, g], sh['rw_post'], (MIX_W,), f"rwkv_post_fwd{l}")

    gdn_rows = [qkv, shift_rows(qkv, 1), shift_rows(qkv, 2), shift_rows(qkv, 3), ab]
    gd = rowwise_fwd(gdn_pre_fn, gdn_rows, sh['gdn_pre'], (MIX_W,) * 5, f"gdn_pre_fwd{l}")
    o_raw, gs0, gt, got_gdn = core_fwd(gdn_heads, list(gd), f"gdn_core_fwd{l}", host_gdn)
    (y_gdn,) = rowwise_fwd(gdn_post_fn, [o_raw, z], sh['gdn_post'], (MIX_W,), f"gdn_post_fwd{l}")

    mixed = jnp.concatenate([y_lru, y_rw, y_gdn], axis=1)
    saved = dict(lru_rows=lru_rows, a=a, hseq=hseq, ly=ly, rw_rows=rw_rows, rw=rw, y_raw=y_raw, rs0=rs0, rt=rt,
                 gdn_rows=gdn_rows, gd=gd, o_raw=o_raw, gs0=gs0, gt=gt, z=z)
    v_layer0 = v if l == 0 else None
    return mixed, saved, v_layer0, got_rwkv, got_gdn


def _mixer_bwd(dmixed, sv, sh, l, dv_first):
    d_lru, d_rw, d_gdn = dmixed[:, 0:256], dmixed[:, 256:640], dmixed[:, 640:1024]
    gw = {}

    dh, dly, d_og = rowwise_bwd(lru_post_fn, [sv['hseq'], sv['ly']], sh['lru_post'], [d_lru], f"lru_post_bwd{l}")
    gscan = lru_scan(unshift_rows(sv['a'], 1), dh, True, f"lru_scan_bwd{l}")
    res = rowwise_bwd(lru_pre_fn, sv['lru_rows'], sh['lru_pre'], [gscan, shift_rows(sv['hseq'], 1)],
                      f"lru_pre_bwd{l}", ct_fn=lambda gs, hp: (gs * hp, gs))
    dlx = res[0] + unshift_rows(res[1], 1) + unshift_rows(res[2], 2) + unshift_rows(res[3], 3)
    dw0, dw1, dw2, dw3, dcb, dga, dgab, dgx, dgxb, dlam = res[5:]
    gw['lru_conv_w'] = jnp.concatenate([dw0, dw1, dw2, dw3], axis=0)
    gw['lru_conv_b'] = dcb[0]
    gw['lru_gate_a_w'] = _block_diag_grad(dga)
    gw['lru_gate_a_b'] = dgab.reshape(LRU_BLOCKS, 64)
    gw['lru_gate_x_w'] = _block_diag_grad(dgx)
    gw['lru_gate_x_b'] = dgxb.reshape(LRU_BLOCKS, 64)
    gw['lru_lambda'] = dlam[0]
    gw['lru_out_norm'] = d_og[0]

    r, lw, k2, v, kk, ar, g = sv['rw']
    res = rowwise_bwd(rwkv_post_fn, [sv['y_raw'], r, k2, v, g], sh['rw_post'], [d_rw], f"rwkv_post_bwd{l}")
    dy_raw, dr_p, dk2_p, dv_p, dg = res[:5]
    gw['rwkv_ln_g'], gw['rwkv_ln_b'], gw['rwkv_r_k'] = res[5][0], res[6][0], res[7].reshape(HEADS, HEAD_DIM)
    dr_c, dlw, dk2_c, dv_c, dkk, dar = core_bwd(rwkv_heads, [r, lw, k2, v, kk, ar], sv['rs0'], sv['rt'], dy_raw,
                                                 f"rwkv_core_bwd{l}")
    cts = [dr_p, dr_c, dlw, dk2_p, dk2_c, dv_p, dv_c, dkk, dar, dg]
    if l == 0:
        cts.append(dv_first)
        ct_fn = lambda a1, a2, b, c1, c2, d1, d2, e, f, gg, vf: (a1 + a2, b, c1 + c2, d1 + d2 + vf, e, f, gg)
    else:
        ct_fn = lambda a1, a2, b, c1, c2, d1, d2, e, f, gg: (a1 + a2, b, c1 + c2, d1 + d2, e, f, gg)
    res = rowwise_bwd(make_rwkv_pre_fn(l > 0), sv['rw_rows'], sh['rw_pre'], cts, f"rwkv_pre_bwd{l}", ct_fn=ct_fn)
    dprw = res[0] + unshift_rows(res[1], 1)
    nrow = len(sv['rw_rows'])
    dv_first_out = res[2] if l > 0 else None
    sg = res[nrow:]
    gw['rwkv_mu'], gw['rwkv_w_up'], gw['rwkv_w_bias'], gw['rwkv_a_up'] = sg[0][0], sg[1], sg[2][0], sg[3]
    gw['rwkv_a_bias'], gw['rwkv_g_up'], gw['rwkv_k_k'], gw['rwkv_k_a'] = sg[4][0], sg[5], sg[6][0], sg[7][0]
    if l > 0:
        gw['rwkv_vres_w1'], gw['rwkv_vres_w2'], gw['rwkv_vres_b'] = sg[8], sg[9], sg[10][0]

    do_raw, dz, d_ng = rowwise_bwd(gdn_post_fn, [sv['o_raw'], sv['z']], sh['gdn_post'], [d_gdn], f"gdn_post_bwd{l}")
    gw['gdn_norm'] = jnp.sum(d_ng.reshape(HEADS, HEAD_DIM), axis=0)
    dgd = core_bwd(gdn_heads, list(sv['gd']), sv['gs0'], sv['gt'], do_raw, f"gdn_core_bwd{l}")
    res = rowwise_bwd(gdn_pre_fn, sv['gdn_rows'], sh['gdn_pre'], list(dgd), f"gdn_pre_bwd{l}")
    dqkv = res[0] + unshift_rows(res[1], 1) + unshift_rows(res[2], 2) + unshift_rows(res[3], 3)
    dab = res[4]
    gw['gdn_conv_w'] = jnp.concatenate(res[5:9], axis=0)
    gw['gdn_a_log'], gw['gdn_dt_bias'] = res[9][0, :HEADS], res[10][0, :HEADS]

    dp = jnp.concatenate([dlx, dly, dprw, dqkv, dz, dab], axis=1)
    return dp, gw, dv_first_out


IN_SHARD = D_IN // N_CHIPS
IN_SHARD_PAD = D_IN_PAD // N_CHIPS


def _cols_to_chips(g, n=N_CHIPS):
    r = g.shape[0]
    return jnp.transpose(g.reshape(r, n, -1), (1, 0, 2))


def _cols_from_chips(g):
    return jnp.transpose(g, (1, 0, 2)).reshape(g.shape[1], -1)


def _w_in_from_chips(g):
    nat = _cols_from_chips(g[:, :, :IN_SHARD])
    return jnp.pad(nat, ((0, 0), (0, D_IN_PAD - D_IN)))


def _w_in_to_chips(g):
    return jnp.pad(_cols_to_chips(g[:, :D_IN]), ((0, 0), (0, 0), (0, IN_SHARD_PAD - IN_SHARD)))


def _natural(name, g):
    if name == 'w_in':
        return _w_in_from_chips(g)
    if BIG[name] == 2:
        return _cols_from_chips(g)
    return g.reshape(-1, g.shape[2])


def local_step(x, target, w, wb, shards=None):
    def hosted(keys):
        return [shards[k] for k in keys] if shards is not None else []

    def arrived(keys, gathered):
        for (name, layer), g in zip(keys if shards is not None else [], gathered):
            wb[name][layer] = _natural(name, g)

    saved = []
    v_first = None
    for l in range(N_LAYERS):
        sh = _layer_shared(w, l)
        more = l + 1 < N_LAYERS
        in_ffn1 = [('w_in', l)] + ([('w_out', l)] if more else [])
        in_rwkv = [('ffn2_wi', l)] + ([('ffn2_wo', l)] if more else [])
        in_gdn = [('ffn1_wi', l + 1)] if more else [('ffn2_wo', l), ('w_out', l)]
        in_ffn2 = [('ffn1_wo', l + 1)] if more else []
        x1, got = ffn_fwd(x, _row(w['ffn1_norm'][l]), wb['ffn1_wi'][l], wb['ffn1_wo'][l], f"ffn1_fwd{l}",
                          hosted(in_ffn1))
        arrived(in_ffn1, got)
        p = proj_fwd(x1, _row(w['mix_norm'][l]), wb['w_in'][l], f"proj_fwd{l}")
        mixed, sv, v0, got_rwkv, got_gdn = _mixer_fwd(p, sh, l, v_first, hosted(in_rwkv), hosted(in_gdn))
        arrived(in_rwkv, got_rwkv)
        arrived(in_gdn, got_gdn)
        if l == 0:
            v_first = v0
        x2 = out_fwd(mixed, wb['w_out'][l], x1, f"out_fwd{l}")
        x3, got = ffn_fwd(x2, _row(w['ffn2_norm'][l]), wb['ffn2_wi'][l], wb['ffn2_wo'][l], f"ffn2_fwd{l}",
                          hosted(in_ffn2))
        arrived(in_ffn2, got)
        saved.append(dict(x0=x, x1=x1, x2=x2, mixed=mixed, sv=sv, sh=sh))
        x = x3

    loss, dx, dgf = loss_head(x, _row(w['final_norm']), target, "loss_head")
    per_layer = [dict() for _ in range(N_LAYERS)]
    dv_first = jnp.zeros((x.shape[0], MIX_W), F32)

    waiting, chip_sums, arrived_parts = [], {}, {}

    def reduce_now(keys, tag):
        if shards is None:
            return
        sums = chip_sums_of([(n, k, per_layer[k][n]) for n, k in keys], lax.axis_index("c"), tag)
        for key, (total, total_bf) in zip(keys, sums):
            chip_sums[key] = total
            waiting.append((key, total_bf))

    def take_waiting():
        keys, parts = [k for k, _ in waiting], [p for _, p in waiting]
        waiting.clear()
        return keys, parts

    for l in reversed(range(N_LAYERS)):
        s = saved[l]
        gw = per_layer[l]
        keys, parts = take_waiting()
        dx, dg2, dwg, dwu, dwo, got = ffn_bwd(s['x2'], dx, _row(w['ffn2_norm'][l]), wb['ffn2_wi'][l],
                                              wb['ffn2_wo'][l], f"ffn2_bwd{l}", parts)
        arrived_parts.update(zip(keys, got))
        wi_parts = lambda dwg, dwu: (dwg, dwu)
        row_parts = lambda dw: dw.reshape(N_CHIPS, -1, dw.shape[1])
        gw['ffn2_norm'], gw['ffn2_wi'], gw['ffn2_wo'] = dg2[0], wi_parts(dwg, dwu), row_parts(dwo)
        if l == N_LAYERS - 1:
            reduce_now([('ffn2_wi', l), ('ffn2_wo', l)], f"ffn2_{l}")
        dmixed, dw_out = out_bwd(s['mixed'], wb['w_out'][l], dx, f"out_bwd{l}")
        gw['w_out'] = row_parts(dw_out)
        dp, gmix, dvf = _mixer_bwd(dmixed, s['sv'], s['sh'], l, dv_first)
        if l > 0:
            dv_first = dvf
        gw.update(gmix)
        dx, dgm, dwin = proj_bwd(s['x1'], dx, _row(w['mix_norm'][l]), wb['w_in'][l], dp, f"proj_bwd{l}")
        gw['mix_norm'], gw['w_in'] = dgm[0], _w_in_to_chips(dwin)
        if l < N_LAYERS - 1:
            reduce_now([('ffn2_wi', l), ('ffn2_wo', l), ('w_in', l), ('w_out', l)], f"mix_{l}")
        keys, parts = take_waiting()
        dx, dg1, dwg, dwu, dwo, got = ffn_bwd(s['x0'], dx, _row(w['ffn1_norm'][l]), wb['ffn1_wi'][l],
                                              wb['ffn1_wo'][l], f"ffn1_bwd{l}", parts)
        arrived_parts.update(zip(keys, got))
        gw['ffn1_norm'], gw['ffn1_wi'], gw['ffn1_wo'] = dg1[0], wi_parts(dwg, dwu), row_parts(dwo)
        if l == N_LAYERS - 1:
            reduce_now([('w_in', l), ('w_out', l), ('ffn1_wi', l), ('ffn1_wo', l)], f"ffn1_{l}")
        else:
            reduce_now([('ffn1_wi', l), ('ffn1_wo', l)], f"ffn1_{l}")
    grads = {'final_norm': dgf[0]}
    if shards is not None:
        grads['last_round'] = take_waiting()
        arrived_parts.update({key: None for key in grads['last_round'][0]})
    for name in WEIGHTS:
        if name == 'final_norm':
            continue
        if name in BIG:
            if shards is None:
                grads[name] = [per_layer[l][name] for l in range(N_LAYERS)]
            else:
                grads[name] = [(chip_sums[(name, l)], arrived_parts[(name, l)]) for l in range(N_LAYERS)]
        elif name.startswith('rwkv_vres'):
            grads[name] = per_layer[1][name][None]
        else:
            grads[name] = jnp.stack([per_layer[l][name] for l in range(N_LAYERS)])
    return loss[0, 0], dx, grads


ANY = pl.BlockSpec(memory_space=pl.ANY)


def _coords():
    return lax.axis_index("x"), lax.axis_index("y"), lax.axis_index("c")


def _other_chips(x, y):
    return [((x + 1) % 2, y), (x, (y + 1) % 2), ((x + 1) % 2, (y + 1) % 2)]


def allreduce_small(pack, name, hosted=()):
    R = pack.shape[0]
    rh = R // 2
    gather = ChipGather(list(hosted))
    ng = gather.n

    def body(*refs):
        x_ref, hx = refs[0], refs[1:1 + ng]
        o_ref, ho = refs[1 + ng], refs[2 + ng:2 + 2 * ng]
        sib_ref, chip_ref, parts_ref, send_sems, recv_sems = refs[2 + 2 * ng:7 + 2 * ng]
        gather_sems = refs[7 + 2 * ng:]
        gather.start(hx, ho, gather_sems)
        x, y, c = _coords()
        sib = (x, y, 1 - c)

        def copy(k, src, dst, to):
            return pltpu.make_async_remote_copy(src_ref=src, dst_ref=dst, send_sem=send_sems.at[k],
                                                recv_sem=recv_sems.at[k], device_id=to, device_id_type=MESH)

        swap = copy(0, x_ref, sib_ref, sib)
        swap.start()
        swap.wait()
        chip_ref[...] = jnp.where(c == 0, x_ref[...], sib_ref[...]) + jnp.where(c == 0, sib_ref[...], x_ref[...])

        mine = pl.ds(pl.multiple_of(c * rh, 8), rh)
        sends = [copy(1 + j, chip_ref.at[mine], parts_ref.at[j], (px, py, c))
                 for j, (px, py) in enumerate(_other_chips(x, y))]
        for cp in sends:
            cp.start()
        for cp in sends:
            cp.wait()
        s = 2 * x + y
        own = chip_ref[mine, :]
        from_chip = {2: parts_ref[0], 1: parts_ref[1], 3: parts_ref[2]}
        terms = []
        for k in range(N_CHIPS):
            t = own
            for d, part in from_chip.items():
                t = jnp.where(jnp.bitwise_xor(s, d) == k, part, t)
            terms.append(t)
        o_ref[mine, :] = ((terms[0] + terms[1]) + terms[2]) + terms[3]

        share = copy(4, o_ref.at[mine], o_ref.at[mine], sib)
        share.start()
        share.wait()
        gather.relay(hx, ho, gather_sems)
        gather.finish(hx, ho, gather_sems)

    vm = pl.BlockSpec(memory_space=pltpu.VMEM)
    res = pl.pallas_call(
        body, name=name, in_specs=[vm] + gather.in_specs, out_specs=tuple([vm] + gather.out_specs),
        out_shape=tuple([jax.ShapeDtypeStruct((R, 128), F32)] + gather.out_shape),
        scratch_shapes=[pltpu.VMEM((R, 128), F32), pltpu.VMEM((R, 128), F32), pltpu.VMEM((3, rh, 128), F32),
                        pltpu.SemaphoreType.DMA((5,)), pltpu.SemaphoreType.DMA((5,))] + gather.scratch,
        compiler_params=_params())(pack, *hosted)
    return res[0], list(res[1:])


class ChipGather:
    def __init__(self, shards):
        self.shapes = [s.shape for s in shards]
        self.n = len(shards)
        self.in_specs = [ANY] * self.n
        self.out_specs = [ANY] * self.n
        self.out_shape = [jax.ShapeDtypeStruct((N_CHIPS,) + s.shape, s.dtype) for s in shards]
        self.scratch = [pltpu.SemaphoreType.DMA((6 * self.n,)), pltpu.SemaphoreType.DMA((6 * self.n,)),
                        pltpu.SemaphoreType.DMA((self.n,))] if self.n else []

    def _rows(self, a, core):
        rh = self.shapes[a][0] // 2
        return pl.ds(pl.multiple_of(core * rh, 16), rh)

    def _copies(self, kind, x_refs, o_refs, sems):
        send_sems, recv_sems, local_sems = sems
        x, y, c = _coords()
        s_me = 2 * x + y
        sib = (x, y, 1 - c)

        def copy(a, k, src, dst, to):
            return pltpu.make_async_remote_copy(src_ref=src, dst_ref=dst, send_sem=send_sems.at[6 * a + k],
                                                recv_sem=recv_sems.at[6 * a + k], device_id=to, device_id_type=MESH)

        if kind == 'own':
            return [pltpu.make_async_copy(x_refs[a], o_refs[a].at[s_me], local_sems.at[a]) for a in range(self.n)]
        out = []
        for j, (px, py) in enumerate(_other_chips(x, y)):
            for a in range(self.n):
                mine = self._rows(a, c)
                part = o_refs[a].at[2 * px + py, mine]
                if kind == 'sent':
                    out.append(copy(a, j, x_refs[a].at[mine], o_refs[a].at[s_me, mine], (px, py, c)))
                elif kind == 'arrived':
                    out.append(copy(a, j, part, part, (px, py, c)))
                elif kind == 'passed':
                    out.append(copy(a, 3 + j, part, part, sib))
                else:
                    theirs = o_refs[a].at[2 * px + py, self._rows(a, 1 - c)]
                    out.append(copy(a, 3 + j, theirs, theirs, sib))
        return out

    def start(self, x_refs, o_refs, sems):
        if not self.n:
            return
        for cp in self._copies('own', x_refs, o_refs, sems) + self._copies('sent', x_refs, o_refs, sems):
            cp.start()

    def relay(self, x_refs, o_refs, sems):
        if not self.n:
            return
        for got, fw in zip(self._copies('arrived', x_refs, o_refs, sems),
                           self._copies('passed', x_refs, o_refs, sems)):
            got.wait_recv()
            fw.start()

    def finish(self, x_refs, o_refs, sems):
        if not self.n:
            return
        for cp in self._copies('from_sibling', x_refs, o_refs, sems):
            cp.wait_recv()
        for cp in self._copies('sent', x_refs, o_refs, sems) + self._copies('passed', x_refs, o_refs, sems):
            cp.wait_send()
        for cp in self._copies('own', x_refs, o_refs, sems):
            cp.wait()


def sibling_swap(srcs, halves, name):
    n = len(srcs)
    row_axis = [s.ndim - 2 for s in srcs]
    out_shapes = [s.shape[:ax] + (s.shape[ax] // 2,) + s.shape[ax + 1:] if halves else s.shape
                  for s, ax in zip(srcs, row_axis)]

    def body(*refs):
        x_refs, o_refs = refs[:n], refs[n:2 * n]
        send_sems, recv_sems = refs[2 * n:]
        x, y, c = _coords()
        copies = []
        for a in range(n):
            part = x_refs[a]
            if halves:
                rh = srcs[a].shape[row_axis[a]] // 2
                theirs = pl.ds(pl.multiple_of((1 - c) * rh, 16), rh)
                part = part.at[:, theirs] if row_axis[a] == 1 else part.at[theirs]
            cp = pltpu.make_async_remote_copy(src_ref=part, dst_ref=o_refs[a], send_sem=send_sems.at[a],
                                              recv_sem=recv_sems.at[a], device_id=(x, y, 1 - c), device_id_type=MESH)
            cp.start()
            copies.append(cp)
        for cp in copies:
            cp.wait()

    return pl.pallas_call(
        body, name=name, in_specs=[ANY] * n, out_specs=tuple([ANY] * n),
        out_shape=tuple(jax.ShapeDtypeStruct(sh, s.dtype) for sh, s in zip(out_shapes, srcs)),
        scratch_shapes=[pltpu.SemaphoreType.DMA((n,)), pltpu.SemaphoreType.DMA((n,))],
        compiler_params=_params())(*srcs)


class ChipScatter:
    def __init__(self, parts):
        self.n = len(parts)
        self.in_specs = [ANY] * self.n
        self.out_specs = [ANY] * self.n
        self.out_shape = [jax.ShapeDtypeStruct((3,) + p.shape[1:], p.dtype) for p in parts]
        self.scratch = [pltpu.SemaphoreType.DMA((3 * self.n,)), pltpu.SemaphoreType.DMA((3 * self.n,))] if self.n else []

    def _copies(self, x_refs, o_refs, sems):
        send_sems, recv_sems = sems
        x, y, c = _coords()
        return [pltpu.make_async_remote_copy(src_ref=x_refs[a].at[2 * px + py], dst_ref=o_refs[a].at[j],
                                             send_sem=send_sems.at[3 * a + j], recv_sem=recv_sems.at[3 * a + j],
                                             device_id=(px, py, c), device_id_type=MESH)
                for j, (px, py) in enumerate(_other_chips(x, y)) for a in range(self.n)]

    def start(self, x_refs, o_refs, sems):
        if self.n:
            for cp in self._copies(x_refs, o_refs, sems):
                cp.start()

    def finish(self, x_refs, o_refs, sems):
        if self.n:
            for cp in self._copies(x_refs, o_refs, sems):
                cp.wait()


HBM = pl.BlockSpec(memory_space=pltpu.HBM)
SEM = pl.BlockSpec(memory_space=pltpu.SEMAPHORE)
SIDE_EFFECT = pltpu.SideEffectType.DATAFLOW_SIDE_EFFECTING


def _scatter_copies(x_refs, land_refs, send_sems, recv_sems):
    x, y, c = _coords()
    n = len(x_refs)
    return [pltpu.make_async_remote_copy(src_ref=x_refs[a].at[2 * px + py], dst_ref=land_refs[a].at[j],
                                         send_sem=send_sems[3 * a + j], recv_sem=recv_sems[3 * a + j],
                                         device_id=(px, py, c), device_id_type=MESH)
            for j, (px, py) in enumerate(_other_chips(x, y)) for a in range(n)]


def scatter_start(parts, name):
    n = len(parts)
    k = 3 * n
    lands = [lax.empty((3,) + p.shape[1:], p.dtype) for p in parts]

    def body(*refs):
        x_refs, land_refs = refs[:n], refs[n:2 * n]
        send_sems, recv_sems = refs[2 * n:2 * n + k], refs[2 * n + k:2 * n + 2 * k]
        token = refs[-1]
        for cp in _scatter_copies(x_refs, land_refs, send_sems, recv_sems):
            cp.start()
        token[...] = jnp.zeros_like(token)

    hbm = lambda a: pltpu.HBM(a.shape, a.dtype)
    res = pl.pallas_call(
        body, name=name, in_specs=[HBM] * (2 * n),
        out_specs=tuple([SEM] * (2 * k) + [HBM] * (2 * n) + [pl.BlockSpec(memory_space=pltpu.VMEM)]),
        out_shape=tuple([pltpu.SemaphoreType.DMA(())] * (2 * k) + [hbm(p) for p in parts] + [hbm(b) for b in lands]
                        + [jax.ShapeDtypeStruct((8, 128), F32)]),
        input_output_aliases={i: 2 * k + i for i in range(2 * n)},
        compiler_params=pltpu.CompilerParams(has_side_effects=SIDE_EFFECT, vmem_limit_bytes=VMEM_LIMIT))(
            *[pltpu.with_memory_space_constraint(a, pltpu.HBM) for a in list(parts) + lands])
    return list(res[:2 * k]), list(res[2 * k:2 * k + n]), list(res[2 * k + n:2 * k + 2 * n]), res[-1]


def scatter_wait(sems, parts_thru, lands_thru, after, name):
    n = len(parts_thru)
    k = 3 * n

    def body(*refs):
        x_refs, land_refs = refs[:n], refs[n:2 * n]
        send_sems, recv_sems = refs[2 * n:2 * n + k], refs[2 * n + k:2 * n + 2 * k]
        for cp in _scatter_copies(x_refs, land_refs, send_sems, recv_sems):
            cp.wait_send()
            cp.wait_recv()

    hbm = lambda a: pltpu.HBM(a.shape, a.dtype)
    res = pl.pallas_call(
        body, name=name, in_specs=[HBM] * (2 * n) + [SEM] * (2 * k) + [ANY],
        out_specs=tuple([HBM] * (2 * n)), out_shape=tuple(hbm(a) for a in list(parts_thru) + list(lands_thru)),
        input_output_aliases={i: i for i in range(2 * n)},
        compiler_params=pltpu.CompilerParams(has_side_effects=SIDE_EFFECT, vmem_limit_bytes=VMEM_LIMIT))(
            *parts_thru, *lands_thru, *sems, after)
    return list(res[n:])


def _row_block(rows):
    return max(b for b in range(16, 257, 16) if rows % b == 0)


def chip_sum(gpack, recv, core, name):
    n, R, W = gpack.shape
    rh = R // 2
    rb = _row_block(rh)
    nb = rh // rb

    def body(c_ref, g_ref, r_ref, o_ref, ob_ref):
        s = g_ref[...] + r_ref[...]
        o_ref[...] = s
        ob_ref[...] = s.astype(BF16)

    blk = pl.BlockSpec((1, rb, W), lambda i, j, c_ref: (i, j, 0))
    spec = pltpu.PrefetchScalarGridSpec(
        num_scalar_prefetch=1, grid=(n, nb),
        in_specs=[pl.BlockSpec((1, rb, W), lambda i, j, c_ref: (i, c_ref[0] * nb + j, 0)), blk],
        out_specs=(blk, blk))
    return pl.pallas_call(
        body, name=name, grid_spec=spec,
        out_shape=(jax.ShapeDtypeStruct((n, rh, W), F32), jax.ShapeDtypeStruct((n, rh, W), BF16)),
        compiler_params=_params(("arbitrary", "arbitrary")))(core, gpack, recv)


def chip_sum_cols(gate, up, recv_gate, recv_up, core, name):
    R, W = gate.shape
    cw = W // 2
    rh = R // 2
    rb = _row_block(rh)
    nb = rh // rb

    def body(c_ref, g_ref, u_ref, rg_ref, ru_ref, o_ref, ob_ref):
        s = jnp.where(pl.program_id(0) < 2, g_ref[...] + rg_ref[...], u_ref[...] + ru_ref[...])
        o_ref[0] = s
        ob_ref[0] = s.astype(BF16)

    gate_blk = lambda s, j: (jnp.where(s < 2, j, nb - 1), jnp.minimum(s, 1))
    up_blk = lambda s, j: (jnp.where(s < 2, 0, j), jnp.maximum(s - 2, 0))
    out = pl.BlockSpec((1, rb, cw), lambda s, j, c_ref: (s, j, 0))

    def own(blk):
        return lambda s, j, c_ref: (c_ref[0] * nb + blk(s, j)[0], blk(s, j)[1])

    def theirs(blk):
        return lambda s, j, c_ref: blk(s, j)

    spec = pltpu.PrefetchScalarGridSpec(
        num_scalar_prefetch=1, grid=(N_CHIPS, nb),
        in_specs=[pl.BlockSpec((rb, cw), own(gate_blk)), pl.BlockSpec((rb, cw), own(up_blk)),
                  pl.BlockSpec((rb, cw), theirs(gate_blk)), pl.BlockSpec((rb, cw), theirs(up_blk))],
        out_specs=(out, out))
    return pl.pallas_call(
        body, name=name, grid_spec=spec,
        out_shape=(jax.ShapeDtypeStruct((N_CHIPS, rh, cw), F32), jax.ShapeDtypeStruct((N_CHIPS, rh, cw), BF16)),
        compiler_params=_params(("arbitrary", "arbitrary")))(core, gate, up, recv_gate, recv_up)


def chip_sums_of(items, core, tag):
    parts = []
    for _, _, g in items:
        parts += list(g) if isinstance(g, tuple) else [g]
    swapped = iter(zip(parts, sibling_swap(parts, True, f"grad_swap_cores_{tag}")))
    core_arg = core.reshape(1).astype(jnp.int32)
    sums = []
    for n, l, g in items:
        if isinstance(g, tuple):
            (dwg, from_g), (dwu, from_u) = next(swapped), next(swapped)
            sums.append(chip_sum_cols(dwg, dwu, from_g, from_u, core_arg, f"grad_chip_sum_{n}{l}"))
        else:
            p, r = next(swapped)
            sums.append(chip_sum(p, r, core_arg, f"grad_chip_sum_{n}{l}"))
    return sums


def shard_sum(own, recv, name):
    R, W = own.shape
    rb = _row_block(R)

    def body(a_ref, r_ref, o_ref):
        acc = a_ref[...]
        for j in range(3):
            acc = acc + r_ref[j].astype(F32)
        o_ref[...] = acc

    return pl.pallas_call(
        body, name=name, grid=(R // rb,),
        in_specs=[pl.BlockSpec((rb, W), lambda i: (i, 0)), pl.BlockSpec((3, rb, W), lambda i: (0, i, 0))],
        out_specs=pl.BlockSpec((rb, W), lambda i: (i, 0)), out_shape=jax.ShapeDtypeStruct((R, W), F32),
        compiler_params=_params(("arbitrary",)))(own, recv)


def adamw(w, m, v, g, name):
    L, R, C = w.shape
    rb = max(b for b in range(8, 257, 8) if R % b == 0)
    bc1 = 1.0 - ADAM_B1 ** ADAM_STEP
    bc2 = 1.0 - ADAM_B2 ** ADAM_STEP

    def body(w_ref, m_ref, v_ref, g_ref, d_ref, nm_ref, nv_ref):
        gv = g_ref[...]
        nm = ADAM_B1 * m_ref[...] + (1.0 - ADAM_B1) * gv
        nv = ADAM_B2 * v_ref[...] + (1.0 - ADAM_B2) * (gv * gv)
        d_ref[...] = -ADAM_LR * ((nm / bc1) / (jnp.sqrt(nv / bc2) + ADAM_EPS) + ADAM_WD * w_ref[...])
        nm_ref[...] = nm
        nv_ref[...] = nv

    blk = pl.BlockSpec((1, rb, C), lambda l, i: (l, i, 0))
    sh = jax.ShapeDtypeStruct((L, R, C), F32)
    return pl.pallas_call(body, name=name, grid=(L, R // rb), in_specs=[blk] * 4, out_specs=(blk,) * 3,
                          out_shape=(sh, sh, sh), compiler_params=_params(("arbitrary", "arbitrary")))(w, m, v, g)


SMALL = [n for n in WEIGHTS if n not in BIG]


PACK_TILE = 8 * 128


def _pack(arrays):
    blocks = []
    for a in arrays:
        flat = a.reshape(-1)
        flat = jnp.pad(flat, (0, -flat.size % PACK_TILE))
        blocks.append(flat.reshape(-1, 128))
    rows = sum(b.shape[0] for b in blocks)
    if rows % 16:
        blocks.append(jnp.zeros((8, 128), arrays[0].dtype))
    return jnp.concatenate(blocks, axis=0)


def _unpack(pack, shapes):
    out, row = [], 0
    for shape in shapes:
        size = int(np.prod(shape))
        rows = -(-size // PACK_TILE) * 8
        out.append(pack[row:row + rows].reshape(-1)[:size].reshape(shape))
        row += rows
    return out


def _pad_lanes(a):
    return jnp.pad(a, ((0, 0), (0, -a.shape[1] % 128)))


def _local_shard(full, axis, chip):
    size = full.shape[axis] // N_CHIPS
    return lax.dynamic_slice_in_dim(full, chip * size, size, axis)


def kernel(x, ffn1_norm, ffn1_wi, ffn1_wo, mix_norm, w_in, w_out, lru_conv_w, lru_conv_b, lru_gate_a_w, lru_gate_a_b, lru_gate_x_w, lru_gate_x_b, lru_lambda, lru_out_norm, rwkv_mu, rwkv_w_up, rwkv_w_bias, rwkv_a_up, rwkv_a_bias, rwkv_g_up, rwkv_k_k, rwkv_k_a, rwkv_r_k, rwkv_ln_g, rwkv_ln_b, rwkv_vres_w1, rwkv_vres_w2, rwkv_vres_b, gdn_conv_w, gdn_a_log, gdn_dt_bias, gdn_norm, ffn2_norm, ffn2_wi, ffn2_wo, final_norm, loss_target, m_ffn1_norm, m_ffn1_wi, m_ffn1_wo, m_mix_norm, m_w_in, m_w_out, m_lru_conv_w, m_lru_conv_b, m_lru_gate_a_w, m_lru_gate_a_b, m_lru_gate_x_w, m_lru_gate_x_b, m_lru_lambda, m_lru_out_norm, m_rwkv_mu, m_rwkv_w_up, m_rwkv_w_bias, m_rwkv_a_up, m_rwkv_a_bias, m_rwkv_g_up, m_rwkv_k_k, m_rwkv_k_a, m_rwkv_r_k, m_rwkv_ln_g, m_rwkv_ln_b, m_rwkv_vres_w1, m_rwkv_vres_w2, m_rwkv_vres_b, m_gdn_conv_w, m_gdn_a_log, m_gdn_dt_bias, m_gdn_norm, m_ffn2_norm, m_ffn2_wi, m_ffn2_wo, m_final_norm, v_ffn1_norm, v_ffn1_wi, v_ffn1_wo, v_mix_norm, v_w_in, v_w_out, v_lru_conv_w, v_lru_conv_b, v_lru_gate_a_w, v_lru_gate_a_b, v_lru_gate_x_w, v_lru_gate_x_b, v_lru_lambda, v_lru_out_norm, v_rwkv_mu, v_rwkv_w_up, v_rwkv_w_bias, v_rwkv_a_up, v_rwkv_a_bias, v_rwkv_g_up, v_rwkv_k_k, v_rwkv_k_a, v_rwkv_r_k, v_rwkv_ln_g, v_rwkv_ln_b, v_rwkv_vres_w1, v_rwkv_vres_w2, v_rwkv_vres_b, v_gdn_conv_w, v_gdn_a_log, v_gdn_dt_bias, v_gdn_norm, v_ffn2_norm, v_ffn2_wi, v_ffn2_wo, v_final_norm):
    args = locals()
    w_loc = {n: args[n] for n in WEIGHTS}
    m_loc = {n: args['m_' + n] for n in WEIGHTS}
    v_loc = {n: args['v_' + n] for n in WEIGHTS}
    chip = 2 * lax.axis_index("x") + lax.axis_index("y")
    core = lax.axis_index("c")

    big = [(n, l) for n in BIG for l in range(N_LAYERS)]
    shards = {(n, l): _pad_lanes(w_loc[n][l].astype(BF16)) for n, l in big}
    sm_names = list(SMALL_SHARDED)
    placed = []
    for n in sm_names:
        mine = [jnp.where((chip == s) & (core == 0), w_loc[n], 0.0) for s in range(N_CHIPS)]
        placed.append(jnp.concatenate(mine, axis=SMALL_SHARDED[n]))
    first = [('ffn1_wi', 0), ('ffn1_wo', 0)]
    summed, gathered = allreduce_small(_pack(placed), "allgather_first", [shards[k] for k in first])
    wb = {n: [None] * N_LAYERS for n in BIG}
    for (n, l), g in zip(first, gathered):
        wb[n][l] = _natural(n, g)
    w_full = dict(w_loc)
    w_full.update(zip(sm_names, _unpack(summed, [p.shape for p in placed])))

    loss, dx, grads = local_step(x[0], loss_target[0], w_full, wb, shards)

    gsum, _ = allreduce_small(_pack([grads[n] for n in SMALL] + [loss.reshape(1)]), "allreduce_small")
    *small_sums, loss_sum = _unpack(gsum, [grads[n].shape for n in SMALL] + [(1,)])
    loss = loss_sum[0]
    g_loc = {}
    for n, g in zip(SMALL, small_sums):
        g_loc[n] = _local_shard(g, SMALL_SHARDED[n], chip) if n in SMALL_SHARDED else g

    last_keys, last_parts = grads['last_round']
    sems, parts_thru, lands_thru, token = scatter_start(last_parts, "grad_scatter_last_start")
    chip_after_start = chip + token[0, 0].astype(chip.dtype)
    rows = {n: [None] * N_LAYERS for n in BIG}
    delta, new_m, new_v = {}, {}, {}

    def finish(keys, arrived, which_chip, tag):
        halves = [shard_sum(lax.dynamic_index_in_dim(grads[n][l][0], which_chip, 0, keepdims=False), got,
                            f"grad_shard_sum_{n}{l}") for (n, l), got in zip(keys, arrived)]
        others = sibling_swap(halves, False, f"grad_share_cores_{tag}")
        for (n, l), half, other in zip(keys, halves, others):
            lower = jnp.where(core == 0, half, other)
            upper = jnp.where(core == 0, other, half)
            rows[n][l] = jnp.concatenate([lower, upper], axis=0)[:, :w_loc[n].shape[-1]]
        for n in BIG:
            if n not in delta and all(r is not None for r in rows[n]):
                g_loc[n] = jnp.stack(rows[n])
                delta[n], new_m[n], new_v[n] = adamw(w_loc[n], m_loc[n], v_loc[n], g_loc[n], f"adamw_{n}")

    early = [k for k in big if k not in last_keys]
    finish(early, [grads[n][l][1] for n, l in early], chip_after_start, "early")
    pack = lambda d: _pack([d[n] for n in SMALL])[None]
    res = adamw(pack(w_loc), pack(m_loc), pack(v_loc), pack(g_loc), "adamw_small")
    for dst, r in zip((delta, new_m, new_v), res):
        dst.update(zip(SMALL, _unpack(r[0], [w_loc[n].shape for n in SMALL])))
    arrived_last = scatter_wait(sems, parts_thru, lands_thru, delta['ffn2_wi'], "grad_scatter_last_wait")
    finish(last_keys, arrived_last, chip, "last")

    return (loss, dx[None], *[g_loc[n] for n in WEIGHTS], *[delta[n] for n in WEIGHTS],
            *[new_m[n] for n in WEIGHTS], *[new_v[n] for n in WEIGHTS])
```

```python
import functools

import numpy as np
import jax
import jax.numpy as jnp
from jax import lax
from jax.experimental import pallas as pl
from jax.experimental.pallas import tpu as pltpu

F32 = jnp.float32
BF16 = jnp.bfloat16
MESH = pl.DeviceIdType.MESH

D_MODEL = 1024
D_FF = 2816
N_LAYERS = 2
HEADS = 6
HEAD_DIM = 64
MIX_W = HEADS * HEAD_DIM
LRU_W = 256
LRU_BLOCKS = 4
D_IN = 3468
D_IN_PAD = 3584
NORM_EPS = 1e-6
GN_EPS = 64e-5
LRU_C = 8.0
CHUNK = 64
CHUNKS_PER_STEP = 4
ROWS = 512
FF_CHUNK = 256
IN_CHUNK = 512
VMEM_LIMIT = 56 * 1024 * 1024

ADAM_LR, ADAM_B1, ADAM_B2, ADAM_EPS, ADAM_WD, ADAM_STEP = 0.001, 0.9, 0.999, 1e-08, 0.01, 10

WEIGHTS = ['ffn1_norm', 'ffn1_wi', 'ffn1_wo', 'mix_norm', 'w_in', 'w_out', 'lru_conv_w', 'lru_conv_b',
           'lru_gate_a_w', 'lru_gate_a_b', 'lru_gate_x_w', 'lru_gate_x_b', 'lru_lambda', 'lru_out_norm',
           'rwkv_mu', 'rwkv_w_up', 'rwkv_w_bias', 'rwkv_a_up', 'rwkv_a_bias', 'rwkv_g_up', 'rwkv_k_k',
           'rwkv_k_a', 'rwkv_r_k', 'rwkv_ln_g', 'rwkv_ln_b', 'rwkv_vres_w1', 'rwkv_vres_w2', 'rwkv_vres_b',
           'gdn_conv_w', 'gdn_a_log', 'gdn_dt_bias', 'gdn_norm', 'ffn2_norm', 'ffn2_wi', 'ffn2_wo', 'final_norm']
BIG = {'ffn1_wi': 2, 'ffn1_wo': 1, 'w_in': 2, 'w_out': 1, 'ffn2_wi': 2, 'ffn2_wo': 1}
SMALL_SHARDED = {'lru_conv_w': 2, 'rwkv_w_up': 2, 'rwkv_a_up': 2, 'rwkv_g_up': 2, 'rwkv_vres_w1': 1,
                 'rwkv_vres_w2': 2, 'gdn_conv_w': 2}
N_CHIPS = 4


def _params(sem=None):
    kw = dict(vmem_limit_bytes=VMEM_LIMIT)
    if sem is not None:
        kw['dimension_semantics'] = sem
    return pltpu.CompilerParams(**kw)


def _bdot(a, b, dims=(((1,), (0,)), ((), ()))):
    return lax.dot_general(a.astype(BF16), b.astype(BF16), dims, preferred_element_type=F32)


def _bdot_nt(a, b):
    return _bdot(a, b, (((1,), (1,)), ((), ())))


def _bdot_tn(a, b):
    return _bdot(a, b, (((0,), (0,)), ((), ())))


_DIMS = {'nn': (((1,), (0,)), ((), ())), 'nt': (((1,), (1,)), ((), ())), 'tn': (((0,), (0,)), ((), ()))}


def _split(a, terms):
    parts = []
    for _ in range(terms - 1):
        hi = a.astype(BF16)
        parts.append(hi)
        a = a - hi.astype(F32)
    parts.append(a.astype(BF16))
    return parts


_BATCH_DIMS = {'nn': (((2,), (1,)), ((0,), (0,))), 'nt': (((2,), (2,)), ((0,), (0,))),
               'tn': (((1,), (1,)), ((0,), (0,)))}


def _dot3(a, b, kind):
    ah, al = _split(a, 2)
    bh, bl = _split(b, 2)
    dims = _BATCH_DIMS[kind] if a.ndim == 3 else _DIMS[kind]
    d = lambda p, q: lax.dot_general(p, q, dims, preferred_element_type=F32)
    return d(ah, bh) + (d(ah, bl) + d(al, bh))


@functools.partial(jax.custom_vjp, nondiff_argnums=(2,))
def _cdot_k(a, b, kind):
    return _dot3(a, b, kind)


def _cdot_k_fwd(a, b, kind):
    return _dot3(a, b, kind), (a, b)


def _cdot_k_bwd(kind, res, ct):
    a, b = res
    if kind == 'nn':
        return _dot3(ct, b, 'nt'), _dot3(a, ct, 'tn')
    if kind == 'nt':
        return _dot3(ct, b, 'nn'), _dot3(ct, a, 'tn')
    return _dot3(b, ct, 'nt'), _dot3(a, ct, 'nn')


_cdot_k.defvjp(_cdot_k_fwd, _cdot_k_bwd)


def _dot1(a, b, kind):
    dims = _BATCH_DIMS[kind] if a.ndim == 3 else _DIMS[kind]
    return lax.dot_general(a.astype(BF16), b.astype(BF16), dims, preferred_element_type=F32)


@functools.partial(jax.custom_vjp, nondiff_argnums=(2,))
def _cdot1_k(a, b, kind):
    return _dot1(a, b, kind)


def _cdot1_k_fwd(a, b, kind):
    return _dot1(a, b, kind), (a, b)


def _cdot1_k_bwd(kind, res, ct):
    a, b = res
    if kind == 'nn':
        return _dot1(ct, b, 'nt'), _dot1(a, ct, 'tn')
    if kind == 'nt':
        return _dot1(ct, b, 'nn'), _dot1(ct, a, 'tn')
    return _dot1(b, ct, 'nt'), _dot1(a, ct, 'nn')


_cdot1_k.defvjp(_cdot1_k_fwd, _cdot1_k_bwd)


def _cdot(a, b):
    return _cdot1_k(a, b, 'nn')


def _cdot_nt(a, b):
    return _cdot1_k(a, b, 'nt')


def _cdot_tn(a, b):
    return _cdot1_k(a, b, 'tn')


def _hdot(a, b):
    return _cdot_k(a, b, 'nn')


def _dot_exact(x, m01, kind):
    d = lambda p: lax.dot_general(p, m01.astype(BF16), _DIMS[kind], preferred_element_type=F32)
    hi, mid, lo = _split(x, 3)
    return d(hi) + (d(mid) + d(lo))


@functools.partial(jax.custom_vjp, nondiff_argnums=(1,))
def _xdot(x, make_m):
    return _dot_exact(x, make_m(), 'nn')


def _xdot_fwd(x, make_m):
    return _dot_exact(x, make_m(), 'nn'), None


def _xdot_bwd(make_m, _, ct):
    return (_dot_exact(ct, make_m(), 'nt'),)


_xdot.defvjp(_xdot_fwd, _xdot_bwd)


def _iota2(n, m):
    return lax.broadcasted_iota(jnp.int32, (n, m), 0), lax.broadcasted_iota(jnp.int32, (n, m), 1)


def _head_blocks(w):
    ri, ci = _iota2(w, w)
    return (ri // HEAD_DIM == ci // HEAD_DIM).astype(F32)


def _segsum(x):
    return _xdot(x, functools.partial(_head_blocks, x.shape[-1]))


def _cumsum_rows(x):
    return _cumsum_k(x, x.shape[0])


@functools.partial(jax.custom_vjp, nondiff_argnums=(1,))
def _cumsum_k(x, n):
    return _lower_dot(x, n, False)


def _lower_dot(x, n, transpose):
    ri, ci = _iota2(n, n)
    m = ((ri <= ci) if transpose else (ri >= ci)).astype(BF16)
    d = lambda p: lax.dot_general(m, p, _DIMS['nn'], preferred_element_type=F32)
    hi, mid, lo = _split(x, 3)
    return d(hi) + (d(mid) + d(lo))


def _cumsum_k_fwd(x, n):
    return _lower_dot(x, n, False), None


def _cumsum_k_bwd(n, _, ct):
    return (_lower_dot(ct, n, True),)


_cumsum_k.defvjp(_cumsum_k_fwd, _cumsum_k_bwd)


def _rms(x, g):
    return x * lax.rsqrt(jnp.mean(x * x, axis=-1, keepdims=True) + NORM_EPS) * g


DENSE_ROWS = 1024


def _row_loop(n_rows, fn):
    rows = min(DENSE_ROWS, n_rows)

    def step(i, c):
        fn(pl.ds(pl.multiple_of(i * rows, rows), rows))
        return c
    lax.fori_loop(0, n_rows // rows, step, 0)


def ffn_fwd(x, g, wi, wo, name, hosted=()):
    T = x.shape[0]
    nj = D_FF // FF_CHUNK
    gather = ChipGather(list(hosted))
    n = gather.n

    def body(*refs):
        x_ref, g_ref, wg_ref, wu_ref, wo_ref = refs[:5]
        hx, o_ref, ho = refs[5:5 + n], refs[5 + n], refs[6 + n:6 + 2 * n]
        h_ref, acc_ref = refs[6 + 2 * n:8 + 2 * n]
        sems = refs[8 + 2 * n:]
        j = pl.program_id(0)

        @pl.when(j == 0)
        def _():
            gather.start(hx, ho, sems)

            def init(r):
                h_ref[r, :] = _rms(x_ref[r, :], g_ref[...]).astype(BF16)
                acc_ref[r, :] = jnp.zeros((r.size, D_MODEL), F32)
            _row_loop(T, init)

        def blk(r):
            hb = h_ref[r, :]
            gate = jnp.dot(hb, wg_ref[...], preferred_element_type=F32)
            up = jnp.dot(hb, wu_ref[...], preferred_element_type=F32)
            a = (gate * jax.nn.sigmoid(gate) * up).astype(BF16)
            acc_ref[r, :] += jnp.dot(a, wo_ref[...], preferred_element_type=F32)
        _row_loop(T, blk)

        @pl.when(j == nj - 2)
        def _():
            gather.relay(hx, ho, sems)

        @pl.when(j == nj - 1)
        def _():
            def fin(r):
                o_ref[r, :] = x_ref[r, :] + 0.5 * acc_ref[r, :]
            _row_loop(T, fin)
            gather.finish(hx, ho, sems)

    full = pl.BlockSpec((T, D_MODEL), lambda j: (0, 0))
    res = pl.pallas_call(
        body, name=name, grid=(nj,),
        in_specs=[full, pl.BlockSpec((1, D_MODEL), lambda j: (0, 0)),
                  pl.BlockSpec((D_MODEL, FF_CHUNK), lambda j: (0, j)),
                  pl.BlockSpec((D_MODEL, FF_CHUNK), lambda j: (0, j + nj)),
                  pl.BlockSpec((FF_CHUNK, D_MODEL), lambda j: (j, 0))] + gather.in_specs,
        out_specs=tuple([full] + gather.out_specs),
        out_shape=tuple([jax.ShapeDtypeStruct((T, D_MODEL), F32)] + gather.out_shape),
        scratch_shapes=[pltpu.VMEM((T, D_MODEL), BF16), pltpu.VMEM((T, D_MODEL), F32)] + gather.scratch,
        compiler_params=_params(("arbitrary",)))(x, g, wi, wi, wo, *hosted)
    return res[0], list(res[1:])


def _norm_bwd_rows(x, g, dh, dres):
    rstd = lax.rsqrt(jnp.mean(x * x, axis=-1, keepdims=True) + NORM_EPS)
    xh = x * rstd
    dxh = dh * g
    dx = rstd * (dxh - xh * jnp.mean(dxh * xh, axis=-1, keepdims=True))
    return dres + dx, jnp.sum(dh * xh, axis=0, keepdims=True)


def ffn_bwd(x, dy, g, wi, wo, name, hosted=()):
    T = x.shape[0]
    nj = D_FF // FF_CHUNK
    scatter = ChipScatter(list(hosted))
    n = scatter.n

    def body(*refs):
        x_ref, dy_ref, g_ref, wg_ref, wu_ref, wo_ref = refs[:6]
        hx = refs[6:6 + n]
        dx_ref, dg_ref, dwg_ref, dwu_ref, dwo_ref = refs[6 + n:11 + n]
        ho = refs[11 + n:11 + 2 * n]
        h_ref, da_ref, dh_ref = refs[11 + 2 * n:14 + 2 * n]
        sems = refs[14 + 2 * n:]
        j = pl.program_id(0)

        @pl.when(j == 0)
        def _():
            scatter.start(hx, ho, sems)

            def init(r):
                h_ref[r, :] = _rms(x_ref[r, :], g_ref[...]).astype(BF16)
                da_ref[r, :] = (0.5 * dy_ref[r, :]).astype(BF16)
                dh_ref[r, :] = jnp.zeros((r.size, D_MODEL), F32)
            _row_loop(T, init)

        dwg_ref[...] = jnp.zeros_like(dwg_ref)
        dwu_ref[...] = jnp.zeros_like(dwu_ref)
        dwo_ref[...] = jnp.zeros_like(dwo_ref)

        def blk(r):
            hb = h_ref[r, :]
            db = da_ref[r, :]
            gate = jnp.dot(hb, wg_ref[...], preferred_element_type=F32)
            up = jnp.dot(hb, wu_ref[...], preferred_element_type=F32)
            sg = jax.nn.sigmoid(gate)
            sl = gate * sg
            da = _bdot_nt(db, wo_ref[...])
            dup = (da * sl).astype(BF16)
            dgate = (da * up * (sg * (1.0 + gate * (1.0 - sg)))).astype(BF16)
            dwo_ref[...] += _bdot_tn((sl * up).astype(BF16), db)
            dwg_ref[...] += _bdot_tn(hb, dgate)
            dwu_ref[...] += _bdot_tn(hb, dup)
            dh_ref[r, :] += _bdot_nt(dgate, wg_ref[...]) + _bdot_nt(dup, wu_ref[...])
        _row_loop(T, blk)

        @pl.when(j == nj - 1)
        def _():
            dg_ref[...] = jnp.zeros_like(dg_ref)

            def fin(r):
                dx, dg = _norm_bwd_rows(x_ref[r, :], g_ref[...], dh_ref[r, :], dy_ref[r, :])
                dx_ref[r, :] = dx
                dg_ref[...] += dg
            _row_loop(T, fin)
            scatter.finish(hx, ho, sems)

    full = pl.BlockSpec((T, D_MODEL), lambda j: (0, 0))
    vec = pl.BlockSpec((1, D_MODEL), lambda j: (0, 0))
    res = pl.pallas_call(
        body, name=name, grid=(nj,),
        in_specs=[full, full, vec,
                  pl.BlockSpec((D_MODEL, FF_CHUNK), lambda j: (0, j)),
                  pl.BlockSpec((D_MODEL, FF_CHUNK), lambda j: (0, j + nj)),
                  pl.BlockSpec((FF_CHUNK, D_MODEL), lambda j: (j, 0))] + scatter.in_specs,
        out_specs=tuple([full, vec,
                         pl.BlockSpec((D_MODEL, FF_CHUNK), lambda j: (0, j)),
                         pl.BlockSpec((D_MODEL, FF_CHUNK), lambda j: (0, j)),
                         pl.BlockSpec((FF_CHUNK, D_MODEL), lambda j: (j, 0))] + scatter.out_specs),
        out_shape=tuple([jax.ShapeDtypeStruct((T, D_MODEL), F32), jax.ShapeDtypeStruct((1, D_MODEL), F32),
                         jax.ShapeDtypeStruct((D_MODEL, D_FF), F32), jax.ShapeDtypeStruct((D_MODEL, D_FF), F32),
                         jax.ShapeDtypeStruct((D_FF, D_MODEL), F32)] + scatter.out_shape),
        scratch_shapes=[pltpu.VMEM((T, D_MODEL), BF16), pltpu.VMEM((T, D_MODEL), BF16),
                        pltpu.VMEM((T, D_MODEL), F32)] + scatter.scratch,
        compiler_params=_params(("arbitrary",)))(x, dy, g, wi, wi, wo, *hosted)
    return res[0], res[1], res[2], res[3], res[4], list(res[5:])


def proj_fwd(x, g, w, name):
    T = x.shape[0]
    nj = D_IN_PAD // IN_CHUNK

    def body(x_ref, g_ref, w_ref, o_ref, h_ref):
        @pl.when(pl.program_id(0) == 0)
        def _():
            def init(r):
                h_ref[r, :] = _rms(x_ref[r, :], g_ref[...]).astype(BF16)
            _row_loop(T, init)

        def blk(r):
            o_ref[r, :] = jnp.dot(h_ref[r, :], w_ref[...], preferred_element_type=F32)
        _row_loop(T, blk)

    return pl.pallas_call(
        body, name=name, grid=(nj,),
        in_specs=[pl.BlockSpec((T, D_MODEL), lambda j: (0, 0)), pl.BlockSpec((1, D_MODEL), lambda j: (0, 0)),
                  pl.BlockSpec((D_MODEL, IN_CHUNK), lambda j: (0, j))],
        out_specs=pl.BlockSpec((T, IN_CHUNK), lambda j: (0, j)),
        out_shape=jax.ShapeDtypeStruct((T, D_IN_PAD), F32),
        scratch_shapes=[pltpu.VMEM((T, D_MODEL), BF16)],
        compiler_params=_params(("arbitrary",)))(x, g, w)


def proj_bwd(x, dres, g, w, dp, name):
    T = x.shape[0]
    nj = D_IN_PAD // IN_CHUNK

    def body(x_ref, dres_ref, g_ref, w_ref, dp_ref, dx_ref, dg_ref, dw_ref, h_ref, dh_ref):
        j = pl.program_id(0)

        @pl.when(j == 0)
        def _():
            def init(r):
                h_ref[r, :] = _rms(x_ref[r, :], g_ref[...]).astype(BF16)
                dh_ref[r, :] = jnp.zeros((r.size, D_MODEL), F32)
            _row_loop(T, init)

        dw_ref[...] = jnp.zeros_like(dw_ref)

        def blk(r):
            dpb = dp_ref[r, :].astype(BF16)
            dw_ref[...] += _bdot_tn(h_ref[r, :], dpb)
            dh_ref[r, :] += _bdot_nt(dpb, w_ref[...])
        _row_loop(T, blk)

        @pl.when(j == nj - 1)
        def _():
            dg_ref[...] = jnp.zeros_like(dg_ref)

            def fin(r):
                dx, dg = _norm_bwd_rows(x_ref[r, :], g_ref[...], dh_ref[r, :], dres_ref[r, :])
                dx_ref[r, :] = dx
                dg_ref[...] += dg
            _row_loop(T, fin)

    full = pl.BlockSpec((T, D_MODEL), lambda j: (0, 0))
    vec = pl.BlockSpec((1, D_MODEL), lambda j: (0, 0))
    return pl.pallas_call(
        body, name=name, grid=(nj,),
        in_specs=[full, full, vec, pl.BlockSpec((D_MODEL, IN_CHUNK), lambda j: (0, j)),
                  pl.BlockSpec((T, IN_CHUNK), lambda j: (0, j))],
        out_specs=(full, vec, pl.BlockSpec((D_MODEL, IN_CHUNK), lambda j: (0, j))),
        out_shape=(jax.ShapeDtypeStruct((T, D_MODEL), F32), jax.ShapeDtypeStruct((1, D_MODEL), F32),
                   jax.ShapeDtypeStruct((D_MODEL, D_IN_PAD), F32)),
        scratch_shapes=[pltpu.VMEM((T, D_MODEL), BF16), pltpu.VMEM((T, D_MODEL), F32)],
        compiler_params=_params(("arbitrary",)))(x, dres, g, w, dp)


def out_fwd(mixed, w, x, name):
    T = x.shape[0]

    def body(m_ref, w_ref, x_ref, o_ref):
        o_ref[...] = x_ref[...] + jnp.dot(m_ref[...].astype(BF16), w_ref[...], preferred_element_type=F32)

    blk = pl.BlockSpec((ROWS, D_MODEL), lambda i: (i, 0))
    return pl.pallas_call(
        body, name=name, grid=(T // ROWS,),
        in_specs=[blk, pl.BlockSpec((D_MODEL, D_MODEL), lambda i: (0, 0)), blk],
        out_specs=blk, out_shape=jax.ShapeDtypeStruct((T, D_MODEL), F32),
        compiler_params=_params(("arbitrary",)))(mixed, w, x)


def out_bwd(mixed, w, dy, name):
    T = dy.shape[0]

    def body(m_ref, w_ref, dy_ref, dm_ref, dw_ref):
        @pl.when(pl.program_id(0) == 0)
        def _():
            dw_ref[...] = jnp.zeros_like(dw_ref)
        dyb = dy_ref[...].astype(BF16)
        dm_ref[...] = _bdot_nt(dyb, w_ref[...])
        dw_ref[...] += _bdot_tn(m_ref[...].astype(BF16), dyb)

    blk = pl.BlockSpec((ROWS, D_MODEL), lambda i: (i, 0))
    sq = pl.BlockSpec((D_MODEL, D_MODEL), lambda i: (0, 0))
    return pl.pallas_call(
        body, name=name, grid=(T // ROWS,),
        in_specs=[blk, sq, blk], out_specs=(blk, sq),
        out_shape=(jax.ShapeDtypeStruct((T, D_MODEL), F32), jax.ShapeDtypeStruct((D_MODEL, D_MODEL), F32)),
        compiler_params=_params(("arbitrary",)))(mixed, w, dy)


def loss_head(x, g, target, name):
    T = x.shape[0]

    def body(x_ref, g_ref, t_ref, loss_ref, dx_ref, dg_ref):
        @pl.when(pl.program_id(0) == 0)
        def _():
            loss_ref[...] = jnp.zeros_like(loss_ref)
            dg_ref[...] = jnp.zeros_like(dg_ref)
        xb = x_ref[...]
        rstd = lax.rsqrt(jnp.mean(xb * xb, axis=-1, keepdims=True) + NORM_EPS)
        xh = xb * rstd
        err = xh * g_ref[...] - t_ref[...]
        loss_ref[...] += 0.5 * jnp.sum(jnp.mean(err * err, axis=-1, keepdims=True), axis=0, keepdims=True)
        dy = err * (1.0 / D_MODEL)
        dg_ref[...] += jnp.sum(dy * xh, axis=0, keepdims=True)
        dxh = dy * g_ref[...]
        dx_ref[...] = rstd * (dxh - xh * jnp.mean(dxh * xh, axis=-1, keepdims=True))

    blk = pl.BlockSpec((ROWS, D_MODEL), lambda i: (i, 0))
    vec = pl.BlockSpec((1, D_MODEL), lambda i: (0, 0))
    return pl.pallas_call(
        body, name=name, grid=(T // ROWS,),
        in_specs=[blk, vec, blk], out_specs=(pl.BlockSpec((1, 1), lambda i: (0, 0)), blk, vec),
        out_shape=(jax.ShapeDtypeStruct((1, 1), F32), jax.ShapeDtypeStruct((T, D_MODEL), F32),
                   jax.ShapeDtypeStruct((1, D_MODEL), F32)),
        compiler_params=_params(("arbitrary",)))(x, g, target)


def rowwise_fwd(fn, rows, shared, out_widths, name):
    T = rows[0].shape[0]
    n_in = len(rows) + len(shared)

    def body(*refs):
        res = fn(*[r[...] for r in refs[:n_in]])
        for o, v in zip(refs[n_in:], res):
            o[...] = v

    in_specs = ([pl.BlockSpec((ROWS, a.shape[1]), lambda i: (i, 0)) for a in rows]
                + [pl.BlockSpec(a.shape, lambda i: (0, 0)) for a in shared])
    return pl.pallas_call(
        body, name=name, grid=(T // ROWS,), in_specs=in_specs,
        out_specs=tuple(pl.BlockSpec((ROWS, w), lambda i: (i, 0)) for w in out_widths),
        out_shape=tuple(jax.ShapeDtypeStruct((T, w), F32) for w in out_widths),
        compiler_params=_params(("arbitrary",)))(*rows, *shared)


def rowwise_bwd(fn, rows, shared, cts, name, ct_fn=None):
    T = rows[0].shape[0]
    nr, ns, nc = len(rows), len(shared), len(cts)

    def body(*refs):
        ins = [r[...] for r in refs[:nr + ns]]
        ctv = tuple(r[...] for r in refs[nr + ns:nr + ns + nc])
        outs = refs[nr + ns + nc:]
        _, vjp = jax.vjp(fn, *ins)
        grads = vjp(ct_fn(*ctv) if ct_fn is not None else ctv)
        for k in range(nr):
            outs[k][...] = grads[k]

        @pl.when(pl.program_id(0) == 0)
        def _():
            for k in range(ns):
                outs[nr + k][...] = jnp.zeros_like(outs[nr + k])
        for k in range(ns):
            outs[nr + k][...] += grads[nr + k]

    row_spec = lambda a: pl.BlockSpec((ROWS, a.shape[1]), lambda i: (i, 0))
    sh_spec = lambda a: pl.BlockSpec(a.shape, lambda i: (0, 0))
    return pl.pallas_call(
        body, name=name, grid=(T // ROWS,),
        in_specs=[row_spec(a) for a in rows] + [sh_spec(a) for a in shared] + [row_spec(a) for a in cts],
        out_specs=tuple([row_spec(a) for a in rows] + [sh_spec(a) for a in shared]),
        out_shape=tuple(jax.ShapeDtypeStruct(a.shape, F32) for a in list(rows) + list(shared)),
        compiler_params=_params(("arbitrary",)))(*rows, *shared, *cts)


def shift_rows(x, s):
    return jnp.pad(x, ((s, 0), (0, 0)))[:x.shape[0]]


def unshift_rows(x, s):
    return jnp.pad(x, ((0, s), (0, 0)))[s:]


def _neg_expm1(y):
    series = -(y * (1.0 + y * (0.5 + y * (1.0 / 6.0 + y * (1.0 / 24.0)))))
    return jnp.where(y > -0.05, series, 1.0 - jnp.exp(y))


def lru_pre_fn(x0, x1, x2, x3, first, w0, w1, w2, w3, cb, ga, gab, gx, gxb, lam):
    xc = w3 * x0 + w2 * x1 + w1 * x2 + w0 * x3 + cb
    r = jax.nn.sigmoid(_hdot(xc, ga) + gab)
    i = jax.nn.sigmoid(_hdot(xc, gx) + gxb)
    log_a = -LRU_C * r * jax.nn.softplus(-lam)
    a = jnp.exp(log_a)
    mult = jnp.where(first > 0.5, 1.0, jnp.sqrt(_neg_expm1(2.0 * log_a)))
    return a, mult * i * xc


def lru_post_fn(h, py, og):
    return (_rms(h * jax.nn.gelu(py), og),)


def lru_scan(a, b, reverse, name):
    T, C = a.shape
    nb = T // 8

    def body(a_ref, b_ref, h_ref):
        rows = lax.broadcasted_iota(jnp.int32, (8, C), 0)

        def blk(i, carry):
            j = nb - 1 - i if reverse else i
            r = pl.ds(pl.multiple_of(j * 8, 8), 8)
            A = a_ref[r, :]
            B = b_ref[r, :]
            for s in (1, 2, 4):
                if reverse:
                    keep = rows < 8 - s
                    sh = 8 - s
                else:
                    keep = rows >= s
                    sh = s
                Bs = jnp.where(keep, pltpu.roll(B, sh, 0), 0.0)
                As = jnp.where(keep, pltpu.roll(A, sh, 0), 1.0)
                B = B + A * Bs
                A = A * As
            hb = B + A * carry
            h_ref[r, :] = hb
            edge = 0 if reverse else 7
            return jnp.sum(jnp.where(rows == edge, hb, 0.0), axis=0, keepdims=True)

        lax.fori_loop(0, nb, blk, jnp.zeros((1, C), F32))

    full = pl.BlockSpec((T, C), lambda: (0, 0))
    return pl.pallas_call(body, name=name, in_specs=[full, full], out_specs=full,
                          out_shape=jax.ShapeDtypeStruct((T, C), F32), compiler_params=_params())(a, b)


def make_rwkv_pre_fn(has_vres):
    def fn(p, pp, *rest):
        if has_vres:
            vf, mu, w_up, w_b, a_up, a_b, g_up, kk_w, ka_w, vw1, vw2, vb = rest
        else:
            mu, w_up, w_b, a_up, a_b, g_up, kk_w, ka_w = rest
        xm = p + (pp - p) * mu
        r, k, v = xm[:, 0:384], xm[:, 384:768], xm[:, 768:1152]
        xw, xa, xg = xm[:, 1152:1216], xm[:, 1216:1280], xm[:, 1280:1408]
        w_log = -jax.nn.softplus(-(w_b + _hdot(jnp.tanh(xw), w_up))) - 0.5
        lw = -jnp.exp(w_log)
        a = jax.nn.sigmoid(a_b + _hdot(xa, a_up))
        g = _hdot(jax.nn.sigmoid(xg), g_up)
        if has_vres:
            v = v + (vf - v) * jax.nn.sigmoid(vb + _hdot(_hdot(v, vw1), vw2))
        kkx = k * kk_w
        kk = kkx * lax.rsqrt(_segsum(kkx * kkx) + 1e-6)
        k2 = k * (1.0 + (a - 1.0) * ka_w)
        return r, lw, k2, v, kk, a, g
    return fn


def rwkv_post_fn(y, r, k2, v, g, ln_g, ln_b, r_k):
    mean = _segsum(y) * (1.0 / HEAD_DIM)
    yc = y - mean
    var = _segsum(yc * yc) * (1.0 / HEAD_DIM)
    yn = yc * lax.rsqrt(var + GN_EPS) * ln_g + ln_b
    bonus = _segsum(r * k2 * r_k) * v
    return ((yn + bonus) * g,)


def _head_expander(first_lane):
    ri, ci = _iota2(128, MIX_W)
    return (ri == ci // HEAD_DIM + first_lane).astype(F32)


def gdn_pre_fn(x0, x1, x2, x3, ab, w0, w1, w2, w3, alog, dtb):
    qkv = jax.nn.silu(w3 * x0 + w2 * x1 + w1 * x2 + w0 * x3)
    q, k, v = qkv[:, 0:384], qkv[:, 384:768], qkv[:, 768:1152]
    q = q * lax.rsqrt(_segsum(q * q) + 1e-6) * (HEAD_DIM ** -0.5)
    k = k * lax.rsqrt(_segsum(k * k) + 1e-6)
    g = -jnp.exp(alog) * jax.nn.softplus(ab + dtb)
    beta = jax.nn.sigmoid(ab)
    ge = _xdot(g, functools.partial(_head_expander, 0))
    be = _xdot(beta, functools.partial(_head_expander, HEADS))
    return q, k, v, ge, be


def gdn_post_fn(o, z, ng):
    ms = _segsum(o * o) * (1.0 / HEAD_DIM)
    return (o * lax.rsqrt(ms + NORM_EPS) * ng * jax.nn.silu(z),)


def _neumann_inv(m):
    n = m.shape[-1]
    ri, ci = _iota2(n, n)
    eye = (ri == ci).astype(F32)
    md = jnp.where(ri // 16 == ci // 16, m, 0.0)
    mo = m - md
    t0 = eye + md
    p2 = _hdot(md, md)
    t0 = t0 + _hdot(t0, p2)
    p4 = _hdot(p2, p2)
    t0 = t0 + _hdot(t0, p4)
    p8 = _hdot(p4, p4)
    t0 = t0 + _hdot(t0, p8)
    nn = _hdot(t0, mo)
    n2 = _hdot(nn, nn)
    t1 = eye + nn + n2 + _hdot(nn, n2)
    return _hdot(t1, t0)


@jax.custom_vjp
def _inv_saved(m, t_saved):
    return t_saved


def _inv_saved_fwd(m, t_saved):
    return t_saved, t_saved


def _inv_saved_bwd(t_saved, dt):
    tt = jnp.swapaxes(t_saved, -1, -2)
    return _hdot(_hdot(tt, dt), tt), jnp.zeros_like(t_saved)


_inv_saved.defvjp(_inv_saved_fwd, _inv_saved_bwd)


def _heads(x):
    return jnp.concatenate([x[None, :, h * HEAD_DIM:(h + 1) * HEAD_DIM] for h in range(HEADS)], axis=0)


def _unheads(y):
    return jnp.concatenate([lax.index_in_dim(y, h, 0, keepdims=False) for h in range(HEADS)], axis=1)


def rwkv_heads(s0, r, lw, k2, v, kk, a, inv):
    n = r.shape[0]
    ri, ci = _iota2(n, n)
    low, strict = ri >= ci, ri > ci
    cs = _cumsum_rows(lw)
    cl = jnp.sum(lw, axis=0, keepdims=True)
    p_in, p_prev, p_inv = jnp.exp(cs), jnp.exp(cs - lw), jnp.exp(-cs)
    p_rest, p_all = jnp.exp(cl - cs), jnp.exp(cl)
    bd = kk * a
    at, rt = _heads(-kk * p_prev), _heads(r * p_in)
    bh, kh = _heads(bd * p_inv), _heads(k2 * p_inv)
    vh = _heads(v)
    m_ab = jnp.where(strict, _cdot_nt(at, bh), 0.0)
    m_ak = jnp.where(strict, _cdot_nt(at, kh), 0.0)
    m_rb = jnp.where(low, _cdot_nt(rt, bh), 0.0)
    m_rk = jnp.where(low, _cdot_nt(rt, kh), 0.0)
    sa = _cdot(inv(m_ab), _cdot_nt(at, s0) + _cdot(m_ak, vh))
    y = _cdot_nt(rt, s0) + _cdot(m_rb, sa) + _cdot(m_rk, vh)
    s1 = s0 * _heads(p_all) + _cdot_tn(sa, _heads(bd * p_rest)) + _cdot_tn(vh, _heads(k2 * p_rest))
    return _unheads(y), s1


def gdn_heads(s0, q, k, v, ge, be, inv):
    n = q.shape[0]
    ri, ci = _iota2(n, n)
    low, strict = ri >= ci, ri > ci
    gc = _cumsum_rows(ge)
    gl = jnp.sum(ge, axis=0, keepdims=True)
    gch = _heads(gc)
    decay = jnp.where(low, jnp.exp(jnp.where(low, gch - jnp.swapaxes(gch, 1, 2), 0.0)), 0.0)
    kb = k * be
    e = jnp.exp(gc)
    kh = _heads(k)
    m = -jnp.where(strict, _cdot_nt(_heads(kb), kh) * decay, 0.0)
    mr = jnp.where(low, _cdot_nt(_heads(q), kh) * decay, 0.0)
    u = _cdot(inv(m), _heads(v * be) - _cdot_nt(_heads(kb * e), s0))
    y = _cdot_nt(_heads(q * e), s0) + _cdot(mr, u)
    s1 = s0 * _heads(jnp.exp(gl)) + _cdot_tn(u, _heads(k * jnp.exp(gl - gc)))
    return _unheads(y), s1


def core_fwd(heads_fn, ins, name, hosted=()):
    T = ins[0].shape[0]
    nc = T // CHUNK
    steps = nc // CHUNKS_PER_STEP
    n = len(ins)
    gather = ChipGather(list(hosted))
    ng = gather.n

    def body(*refs):
        hx = refs[n:n + ng]
        y_ref, s0_ref, t_ref = refs[n + ng:n + ng + 3]
        ho = refs[n + ng + 3:n + 2 * ng + 3]
        s_ref = refs[n + 2 * ng + 3]
        sems = refs[n + 2 * ng + 4:]
        c = pl.program_id(0)

        @pl.when(c == 0)
        def _():
            gather.start(hx, ho, sems)
            s_ref[...] = jnp.zeros_like(s_ref)

        state = s_ref[...]
        for u in range(CHUNKS_PER_STEP):
            rows = slice(u * CHUNK, (u + 1) * CHUNK)
            kept = []

            def inv(m):
                kept.append(_neumann_inv(m))
                return kept[0]

            y, after = heads_fn(state, *[r[rows, :] for r in refs[:n]], inv)
            y_ref[rows, :] = y
            s0_ref[u] = state
            t_ref[u] = kept[0]
            state = after
        s_ref[...] = state

        @pl.when(c == max(steps - 2, 0))
        def _():
            gather.relay(hx, ho, sems)

        @pl.when(c == steps - 1)
        def _():
            gather.finish(hx, ho, sems)

    row = pl.BlockSpec((CHUNKS_PER_STEP * CHUNK, MIX_W), lambda c: (c, 0))
    st_shape = (HEADS, HEAD_DIM, HEAD_DIM)
    st = pl.BlockSpec((CHUNKS_PER_STEP,) + st_shape, lambda c: (c, 0, 0, 0))
    res = pl.pallas_call(
        body, name=name, grid=(steps,), in_specs=[row] * n + gather.in_specs,
        out_specs=tuple([row, st, st] + gather.out_specs),
        out_shape=tuple([jax.ShapeDtypeStruct((T, MIX_W), F32), jax.ShapeDtypeStruct((nc,) + st_shape, F32),
                         jax.ShapeDtypeStruct((nc,) + st_shape, F32)] + gather.out_shape),
        scratch_shapes=[pltpu.VMEM(st_shape, F32)] + gather.scratch,
        compiler_params=_params(("arbitrary",)))(*ins, *hosted)
    return res[0], res[1], res[2], list(res[3:])


def core_bwd(heads_fn, ins, s0_all, t_all, dy, name):
    T = ins[0].shape[0]
    nc = T // CHUNK
    steps = nc // CHUNKS_PER_STEP
    n = len(ins)

    def body(*refs):
        s0_ref, t_ref, dy_ref = refs[n:n + 3]
        outs = refs[n + 3:n + 3 + n]
        ds_ref = refs[n + 3 + n]

        @pl.when(pl.program_id(0) == 0)
        def _():
            ds_ref[...] = jnp.zeros_like(ds_ref)

        d_state = ds_ref[...]
        for u in reversed(range(CHUNKS_PER_STEP)):
            rows = slice(u * CHUNK, (u + 1) * CHUNK)
            t_saved = t_ref[u]
            f = lambda s0, *xs: heads_fn(s0, *xs, lambda m: _inv_saved(m, t_saved))
            _, vjp = jax.vjp(f, s0_ref[u], *[r[rows, :] for r in refs[:n]])
            grads = vjp((dy_ref[rows, :], d_state))
            d_state = grads[0]
            for k in range(n):
                outs[k][rows, :] = grads[1 + k]
        ds_ref[...] = d_state

    row = pl.BlockSpec((CHUNKS_PER_STEP * CHUNK, MIX_W), lambda c: (steps - 1 - c, 0))
    st_shape = (HEADS, HEAD_DIM, HEAD_DIM)
    st = pl.BlockSpec((CHUNKS_PER_STEP,) + st_shape, lambda c: (steps - 1 - c, 0, 0, 0))
    return pl.pallas_call(
        body, name=name, grid=(steps,), in_specs=[row] * n + [st, st, row], out_specs=tuple([row] * n),
        out_shape=tuple(jax.ShapeDtypeStruct((T, MIX_W), F32) for _ in range(n)),
        scratch_shapes=[pltpu.VMEM(st_shape, F32)],
        compiler_params=_params(("arbitrary",)))(*ins, s0_all, t_all, dy)


def _block_diag(w):
    out = jnp.zeros((LRU_W, LRU_W), w.dtype)
    for n in range(LRU_BLOCKS):
        out = lax.dynamic_update_slice(out, w[n], (n * 64, n * 64))
    return out


def _block_diag_grad(g):
    return jnp.stack([g[n * 64:(n + 1) * 64, n * 64:(n + 1) * 64] for n in range(LRU_BLOCKS)])


def _row(v):
    return v.reshape(1, -1)


def _pad128(v):
    return jnp.pad(v.reshape(1, -1), ((0, 0), (0, 128 - v.size)))


def _layer_shared(w, l):
    cw = w['lru_conv_w'][l]
    lru_pre = [_row(cw[0]), _row(cw[1]), _row(cw[2]), _row(cw[3]), _row(w['lru_conv_b'][l]),
               _block_diag(w['lru_gate_a_w'][l]), _row(w['lru_gate_a_b'][l]),
               _block_diag(w['lru_gate_x_w'][l]), _row(w['lru_gate_x_b'][l]), _row(w['lru_lambda'][l])]
    rw_pre = [_row(w['rwkv_mu'][l]), w['rwkv_w_up'][l], _row(w['rwkv_w_bias'][l]), w['rwkv_a_up'][l],
              _row(w['rwkv_a_bias'][l]), w['rwkv_g_up'][l], _row(w['rwkv_k_k'][l]), _row(w['rwkv_k_a'][l])]
    if l > 0:
        rw_pre += [w['rwkv_vres_w1'][l - 1], w['rwkv_vres_w2'][l - 1], _row(w['rwkv_vres_b'][l - 1])]
    rw_post = [_row(w['rwkv_ln_g'][l]), _row(w['rwkv_ln_b'][l]), _row(w['rwkv_r_k'][l])]
    gw = w['gdn_conv_w'][l]
    gdn_pre = [_row(gw[0]), _row(gw[1]), _row(gw[2]), _row(gw[3]), _pad128(w['gdn_a_log'][l]),
               _pad128(w['gdn_dt_bias'][l])]
    gdn_post = [_row(jnp.tile(w['gdn_norm'][l], HEADS))]
    return dict(lru_pre=lru_pre, lru_post=[_row(w['lru_out_norm'][l])], rw_pre=rw_pre, rw_post=rw_post,
                gdn_pre=gdn_pre, gdn_post=gdn_post)


def _mixer_fwd(p, sh, l, v_first, host_rwkv=(), host_gdn=()):
    T = p.shape[0]
    lx, ly = p[:, 0:256], p[:, 256:512]
    prw, qkv, z, ab = p[:, 512:1920], p[:, 1920:3072], p[:, 3072:3456], p[:, 3456:3584]
    first = (lax.broadcasted_iota(jnp.int32, (T, LRU_W), 0) == 0).astype(F32)
    lru_rows = [lx, shift_rows(lx, 1), shift_rows(lx, 2), shift_rows(lx, 3), first]
    a, b = rowwise_fwd(lru_pre_fn, lru_rows, sh['lru_pre'], (LRU_W, LRU_W), f"lru_pre_fwd{l}")
    hseq = lru_scan(a, b, False, f"lru_scan_fwd{l}")
    (y_lru,) = rowwise_fwd(lru_post_fn, [hseq, ly], sh['lru_post'], (LRU_W,), f"lru_post_fwd{l}")

    rw_rows = [prw, shift_rows(prw, 1)] + ([v_first] if l > 0 else [])
    rw = rowwise_fwd(make_rwkv_pre_fn(l > 0), rw_rows, sh['rw_pre'], (MIX_W,) * 7, f"rwkv_pre_fwd{l}")
    r, lw, k2, v, kk, ar, g = rw
    y_raw, rs0, rt, got_rwkv = core_fwd(rwkv_heads, [r, lw, k2, v, kk, ar], f"rwkv_core_fwd{l}", host_rwkv)
    (y_rw,) = rowwise_fwd(rwkv_post_fn, [y_raw, r, k2, v, g], sh['rw_post'], (MIX_W,), f"rwkv_post_fwd{l}")

    gdn_rows = [qkv, shift_rows(qkv, 1), shift_rows(qkv, 2), shift_rows(qkv, 3), ab]
    gd = rowwise_fwd(gdn_pre_fn, gdn_rows, sh['gdn_pre'], (MIX_W,) * 5, f"gdn_pre_fwd{l}")
    o_raw, gs0, gt, got_gdn = core_fwd(gdn_heads, list(gd), f"gdn_core_fwd{l}", host_gdn)
    (y_gdn,) = rowwise_fwd(gdn_post_fn, [o_raw, z], sh['gdn_post'], (MIX_W,), f"gdn_post_fwd{l}")

    mixed = jnp.concatenate([y_lru, y_rw, y_gdn], axis=1)
    saved = dict(lru_rows=lru_rows, a=a, hseq=hseq, ly=ly, rw_rows=rw_rows, rw=rw, y_raw=y_raw, rs0=rs0, rt=rt,
                 gdn_rows=gdn_rows, gd=gd, o_raw=o_raw, gs0=gs0, gt=gt, z=z)
    v_layer0 = v if l == 0 else None
    return mixed, saved, v_layer0, got_rwkv, got_gdn


def _mixer_bwd(dmixed, sv, sh, l, dv_first):
    d_lru, d_rw, d_gdn = dmixed[:, 0:256], dmixed[:, 256:640], dmixed[:, 640:1024]
    gw = {}

    dh, dly, d_og = rowwise_bwd(lru_post_fn, [sv['hseq'], sv['ly']], sh['lru_post'], [d_lru], f"lru_post_bwd{l}")
    gscan = lru_scan(unshift_rows(sv['a'], 1), dh, True, f"lru_scan_bwd{l}")
    res = rowwise_bwd(lru_pre_fn, sv['lru_rows'], sh['lru_pre'], [gscan, shift_rows(sv['hseq'], 1)],
                      f"lru_pre_bwd{l}", ct_fn=lambda gs, hp: (gs * hp, gs))
    dlx = res[0] + unshift_rows(res[1], 1) + unshift_rows(res[2], 2) + unshift_rows(res[3], 3)
    dw0, dw1, dw2, dw3, dcb, dga, dgab, dgx, dgxb, dlam = res[5:]
    gw['lru_conv_w'] = jnp.concatenate([dw0, dw1, dw2, dw3], axis=0)
    gw['lru_conv_b'] = dcb[0]
    gw['lru_gate_a_w'] = _block_diag_grad(dga)
    gw['lru_gate_a_b'] = dgab.reshape(LRU_BLOCKS, 64)
    gw['lru_gate_x_w'] = _block_diag_grad(dgx)
    gw['lru_gate_x_b'] = dgxb.reshape(LRU_BLOCKS, 64)
    gw['lru_lambda'] = dlam[0]
    gw['lru_out_norm'] = d_og[0]

    r, lw, k2, v, kk, ar, g = sv['rw']
    res = rowwise_bwd(rwkv_post_fn, [sv['y_raw'], r, k2, v, g], sh['rw_post'], [d_rw], f"rwkv_post_bwd{l}")
    dy_raw, dr_p, dk2_p, dv_p, dg = res[:5]
    gw['rwkv_ln_g'], gw['rwkv_ln_b'], gw['rwkv_r_k'] = res[5][0], res[6][0], res[7].reshape(HEADS, HEAD_DIM)
    dr_c, dlw, dk2_c, dv_c, dkk, dar = core_bwd(rwkv_heads, [r, lw, k2, v, kk, ar], sv['rs0'], sv['rt'], dy_raw,
                                                 f"rwkv_core_bwd{l}")
    cts = [dr_p, dr_c, dlw, dk2_p, dk2_c, dv_p, dv_c, dkk, dar, dg]
    if l == 0:
        cts.append(dv_first)
        ct_fn = lambda a1, a2, b, c1, c2, d1, d2, e, f, gg, vf: (a1 + a2, b, c1 + c2, d1 + d2 + vf, e, f, gg)
    else:
        ct_fn = lambda a1, a2, b, c1, c2, d1, d2, e, f, gg: (a1 + a2, b, c1 + c2, d1 + d2, e, f, gg)
    res = rowwise_bwd(make_rwkv_pre_fn(l > 0), sv['rw_rows'], sh['rw_pre'], cts, f"rwkv_pre_bwd{l}", ct_fn=ct_fn)
    dprw = res[0] + unshift_rows(res[1], 1)
    nrow = len(sv['rw_rows'])
    dv_first_out = res[2] if l > 0 else None
    sg = res[nrow:]
    gw['rwkv_mu'], gw['rwkv_w_up'], gw['rwkv_w_bias'], gw['rwkv_a_up'] = sg[0][0], sg[1], sg[2][0], sg[3]
    gw['rwkv_a_bias'], gw['rwkv_g_up'], gw['rwkv_k_k'], gw['rwkv_k_a'] = sg[4][0], sg[5], sg[6][0], sg[7][0]
    if l > 0:
        gw['rwkv_vres_w1'], gw['rwkv_vres_w2'], gw['rwkv_vres_b'] = sg[8], sg[9], sg[10][0]

    do_raw, dz, d_ng = rowwise_bwd(gdn_post_fn, [sv['o_raw'], sv['z']], sh['gdn_post'], [d_gdn], f"gdn_post_bwd{l}")
    gw['gdn_norm'] = jnp.sum(d_ng.reshape(HEADS, HEAD_DIM), axis=0)
    dgd = core_bwd(gdn_heads, list(sv['gd']), sv['gs0'], sv['gt'], do_raw, f"gdn_core_bwd{l}")
    res = rowwise_bwd(gdn_pre_fn, sv['gdn_rows'], sh['gdn_pre'], list(dgd), f"gdn_pre_bwd{l}")
    dqkv = res[0] + unshift_rows(res[1], 1) + unshift_rows(res[2], 2) + unshift_rows(res[3], 3)
    dab = res[4]
    gw['gdn_conv_w'] = jnp.concatenate(res[5:9], axis=0)
    gw['gdn_a_log'], gw['gdn_dt_bias'] = res[9][0, :HEADS], res[10][0, :HEADS]

    dp = jnp.concatenate([dlx, dly, dprw, dqkv, dz, dab], axis=1)
    return dp, gw, dv_first_out


IN_SHARD = D_IN // N_CHIPS
IN_SHARD_PAD = D_IN_PAD // N_CHIPS


def _cols_to_chips(g, n=N_CHIPS):
    r = g.shape[0]
    return jnp.transpose(g.reshape(r, n, -1), (1, 0, 2))


def _cols_from_chips(g):
    return jnp.transpose(g, (1, 0, 2)).reshape(g.shape[1], -1)


def _w_in_from_chips(g):
    nat = _cols_from_chips(g[:, :, :IN_SHARD])
    return jnp.pad(nat, ((0, 0), (0, D_IN_PAD - D_IN)))


def _w_in_to_chips(g):
    return jnp.pad(_cols_to_chips(g[:, :D_IN]), ((0, 0), (0, 0), (0, IN_SHARD_PAD - IN_SHARD)))


def _natural(name, g):
    if name == 'w_in':
        return _w_in_from_chips(g)
    if BIG[name] == 2:
        return _cols_from_chips(g)
    return g.reshape(-1, g.shape[2])


def local_step(x, target, w, wb, shards=None):
    def hosted(keys):
        return [shards[k] for k in keys] if shards is not None else []

    def arrived(keys, gathered):
        for (name, layer), g in zip(keys if shards is not None else [], gathered):
            wb[name][layer] = _natural(name, g)

    saved = []
    v_first = None
    for l in range(N_LAYERS):
        sh = _layer_shared(w, l)
        more = l + 1 < N_LAYERS
        in_ffn1 = [('w_in', l)] + ([('w_out', l)] if more else [])
        in_rwkv = [('ffn2_wi', l)] + ([('ffn2_wo', l)] if more else [])
        in_gdn = [('ffn1_wi', l + 1)] if more else [('ffn2_wo', l), ('w_out', l)]
        in_ffn2 = [('ffn1_wo', l + 1)] if more else []
        x1, got = ffn_fwd(x, _row(w['ffn1_norm'][l]), wb['ffn1_wi'][l], wb['ffn1_wo'][l], f"ffn1_fwd{l}",
                          hosted(in_ffn1))
        arrived(in_ffn1, got)
        p = proj_fwd(x1, _row(w['mix_norm'][l]), wb['w_in'][l], f"proj_fwd{l}")
        mixed, sv, v0, got_rwkv, got_gdn = _mixer_fwd(p, sh, l, v_first, hosted(in_rwkv), hosted(in_gdn))
        arrived(in_rwkv, got_rwkv)
        arrived(in_gdn, got_gdn)
        if l == 0:
            v_first = v0
        x2 = out_fwd(mixed, wb['w_out'][l], x1, f"out_fwd{l}")
        x3, got = ffn_fwd(x2, _row(w['ffn2_norm'][l]), wb['ffn2_wi'][l], wb['ffn2_wo'][l], f"ffn2_fwd{l}",
                          hosted(in_ffn2))
        arrived(in_ffn2, got)
        saved.append(dict(x0=x, x1=x1, x2=x2, mixed=mixed, sv=sv, sh=sh))
        x = x3

    loss, dx, dgf = loss_head(x, _row(w['final_norm']), target, "loss_head")
    per_layer = [dict() for _ in range(N_LAYERS)]
    dv_first = jnp.zeros((x.shape[0], MIX_W), F32)

    waiting, chip_sums, arrived_parts = [], {}, {}

    def reduce_now(keys, tag):
        if shards is None:
            return
        sums = chip_sums_of([(n, k, per_layer[k][n]) for n, k in keys], lax.axis_index("c"), tag)
        for key, (total, total_bf) in zip(keys, sums):
            chip_sums[key] = total
            waiting.append((key, total_bf))

    def take_waiting():
        keys, parts = [k for k, _ in waiting], [p for _, p in waiting]
        waiting.clear()
        return keys, parts

    for l in reversed(range(N_LAYERS)):
        s = saved[l]
        gw = per_layer[l]
        keys, parts = take_waiting()
        dx, dg2, dwg, dwu, dwo, got = ffn_bwd(s['x2'], dx, _row(w['ffn2_norm'][l]), wb['ffn2_wi'][l],
                                              wb['ffn2_wo'][l], f"ffn2_bwd{l}", parts)
        arrived_parts.update(zip(keys, got))
        wi_parts = lambda dwg, dwu: (dwg, dwu)
        row_parts = lambda dw: dw.reshape(N_CHIPS, -1, dw.shape[1])
        gw['ffn2_norm'], gw['ffn2_wi'], gw['ffn2_wo'] = dg2[0], wi_parts(dwg, dwu), row_parts(dwo)
        if l == N_LAYERS - 1:
            reduce_now([('ffn2_wi', l), ('ffn2_wo', l)], f"ffn2_{l}")
        dmixed, dw_out = out_bwd(s['mixed'], wb['w_out'][l], dx, f"out_bwd{l}")
        gw['w_out'] = row_parts(dw_out)
        dp, gmix, dvf = _mixer_bwd(dmixed, s['sv'], s['sh'], l, dv_first)
        if l > 0:
            dv_first = dvf
        gw.update(gmix)
        dx, dgm, dwin = proj_bwd(s['x1'], dx, _row(w['mix_norm'][l]), wb['w_in'][l], dp, f"proj_bwd{l}")
        gw['mix_norm'], gw['w_in'] = dgm[0], _w_in_to_chips(dwin)
        if l < N_LAYERS - 1:
            reduce_now([('ffn2_wi', l), ('ffn2_wo', l), ('w_in', l), ('w_out', l)], f"mix_{l}")
        keys, parts = take_waiting()
        dx, dg1, dwg, dwu, dwo, got = ffn_bwd(s['x0'], dx, _row(w['ffn1_norm'][l]), wb['ffn1_wi'][l],
                                              wb['ffn1_wo'][l], f"ffn1_bwd{l}", parts)
        arrived_parts.update(zip(keys, got))
        gw['ffn1_norm'], gw['ffn1_wi'], gw['ffn1_wo'] = dg1[0], wi_parts(dwg, dwu), row_parts(dwo)
        if l == N_LAYERS - 1:
            reduce_now([('w_in', l), ('w_out', l), ('ffn1_wi', l), ('ffn1_wo', l)], f"ffn1_{l}")
        else:
            reduce_now([('ffn1_wi', l), ('ffn1_wo', l)], f"ffn1_{l}")
    grads = {'final_norm': dgf[0]}
    if shards is not None:
        grads['last_round'] = take_waiting()
        arrived_parts.update({key: None for key in grads['last_round'][0]})
    for name in WEIGHTS:
        if name == 'final_norm':
            continue
        if name in BIG:
            if shards is None:
                grads[name] = [per_layer[l][name] for l in range(N_LAYERS)]
            else:
                grads[name] = [(chip_sums[(name, l)], arrived_parts[(name, l)]) for l in range(N_LAYERS)]
        elif name.startswith('rwkv_vres'):
            grads[name] = per_layer[1][name][None]
        else:
            grads[name] = jnp.stack([per_layer[l][name] for l in range(N_LAYERS)])
    return loss[0, 0], dx, grads


ANY = pl.BlockSpec(memory_space=pl.ANY)


def _coords():
    return lax.axis_index("x"), lax.axis_index("y"), lax.axis_index("c")


def _other_chips(x, y):
    return [((x + 1) % 2, y), (x, (y + 1) % 2), ((x + 1) % 2, (y + 1) % 2)]


def allreduce_small(pack, name, hosted=()):
    R = pack.shape[0]
    rh = R // 2
    gather = ChipGather(list(hosted))
    ng = gather.n

    def body(*refs):
        x_ref, hx = refs[0], refs[1:1 + ng]
        o_ref, ho = refs[1 + ng], refs[2 + ng:2 + 2 * ng]
        sib_ref, chip_ref, parts_ref, send_sems, recv_sems = refs[2 + 2 * ng:7 + 2 * ng]
        gather_sems = refs[7 + 2 * ng:]
        gather.start(hx, ho, gather_sems)
        x, y, c = _coords()
        sib = (x, y, 1 - c)

        def copy(k, src, dst, to):
            return pltpu.make_async_remote_copy(src_ref=src, dst_ref=dst, send_sem=send_sems.at[k],
                                                recv_sem=recv_sems.at[k], device_id=to, device_id_type=MESH)

        swap = copy(0, x_ref, sib_ref, sib)
        swap.start()
        swap.wait()
        chip_ref[...] = jnp.where(c == 0, x_ref[...], sib_ref[...]) + jnp.where(c == 0, sib_ref[...], x_ref[...])

        mine = pl.ds(pl.multiple_of(c * rh, 8), rh)
        sends = [copy(1 + j, chip_ref.at[mine], parts_ref.at[j], (px, py, c))
                 for j, (px, py) in enumerate(_other_chips(x, y))]
        for cp in sends:
            cp.start()
        for cp in sends:
            cp.wait()
        s = 2 * x + y
        own = chip_ref[mine, :]
        from_chip = {2: parts_ref[0], 1: parts_ref[1], 3: parts_ref[2]}
        terms = []
        for k in range(N_CHIPS):
            t = own
            for d, part in from_chip.items():
                t = jnp.where(jnp.bitwise_xor(s, d) == k, part, t)
            terms.append(t)
        o_ref[mine, :] = ((terms[0] + terms[1]) + terms[2]) + terms[3]

        share = copy(4, o_ref.at[mine], o_ref.at[mine], sib)
        share.start()
        share.wait()
        gather.relay(hx, ho, gather_sems)
        gather.finish(hx, ho, gather_sems)

    vm = pl.BlockSpec(memory_space=pltpu.VMEM)
    res = pl.pallas_call(
        body, name=name, in_specs=[vm] + gather.in_specs, out_specs=tuple([vm] + gather.out_specs),
        out_shape=tuple([jax.ShapeDtypeStruct((R, 128), F32)] + gather.out_shape),
        scratch_shapes=[pltpu.VMEM((R, 128), F32), pltpu.VMEM((R, 128), F32), pltpu.VMEM((3, rh, 128), F32),
                        pltpu.SemaphoreType.DMA((5,)), pltpu.SemaphoreType.DMA((5,))] + gather.scratch,
        compiler_params=_params())(pack, *hosted)
    return res[0], list(res[1:])


GATHER_PIECES = 2


class ChipGather:
    def __init__(self, shards):
        self.shapes = [s.shape for s in shards]
        self.n = len(shards)
        self.in_specs = [ANY] * self.n
        self.out_specs = [ANY] * self.n
        self.out_shape = [jax.ShapeDtypeStruct((N_CHIPS,) + s.shape, s.dtype) for s in shards]
        k = 6 * GATHER_PIECES * self.n
        self.scratch = [pltpu.SemaphoreType.DMA((k,)), pltpu.SemaphoreType.DMA((k,)),
                        pltpu.SemaphoreType.DMA((self.n,))] if self.n else []

    def _rows(self, a, core, piece):
        rp = self.shapes[a][0] // (2 * GATHER_PIECES)
        return pl.ds(pl.multiple_of((core * GATHER_PIECES + piece) * rp, 16), rp)

    def _copies(self, kind, x_refs, o_refs, sems):
        send_sems, recv_sems, local_sems = sems
        x, y, c = _coords()
        s_me = 2 * x + y
        sib = (x, y, 1 - c)

        def copy(a, p, k, src, dst, to):
            sem = (a * GATHER_PIECES + p) * 6 + k
            return pltpu.make_async_remote_copy(src_ref=src, dst_ref=dst, send_sem=send_sems.at[sem],
                                                recv_sem=recv_sems.at[sem], device_id=to, device_id_type=MESH)

        if kind == 'own':
            return [pltpu.make_async_copy(x_refs[a], o_refs[a].at[s_me], local_sems.at[a]) for a in range(self.n)]
        out = []
        for p in range(GATHER_PIECES):
            for j, (px, py) in enumerate(_other_chips(x, y)):
                for a in range(self.n):
                    mine = self._rows(a, c, p)
                    part = o_refs[a].at[2 * px + py, mine]
                    if kind == 'sent':
                        out.append(copy(a, p, j, x_refs[a].at[mine], o_refs[a].at[s_me, mine], (px, py, c)))
                    elif kind == 'arrived':
                        out.append(copy(a, p, j, part, part, (px, py, c)))
                    elif kind == 'passed':
                        out.append(copy(a, p, 3 + j, part, part, sib))
                    else:
                        theirs = o_refs[a].at[2 * px + py, self._rows(a, 1 - c, p)]
                        out.append(copy(a, p, 3 + j, theirs, theirs, sib))
        return out

    def start(self, x_refs, o_refs, sems):
        if not self.n:
            return
        for cp in self._copies('own', x_refs, o_refs, sems) + self._copies('sent', x_refs, o_refs, sems):
            cp.start()

    def relay(self, x_refs, o_refs, sems):
        if not self.n:
            return
        for got, fw in zip(self._copies('arrived', x_refs, o_refs, sems),
                           self._copies('passed', x_refs, o_refs, sems)):
            got.wait_recv()
            fw.start()

    def finish(self, x_refs, o_refs, sems):
        if not self.n:
            return
        for cp in self._copies('from_sibling', x_refs, o_refs, sems):
            cp.wait_recv()
        for cp in self._copies('sent', x_refs, o_refs, sems) + self._copies('passed', x_refs, o_refs, sems):
            cp.wait_send()
        for cp in self._copies('own', x_refs, o_refs, sems):
            cp.wait()


def sibling_swap(srcs, halves, name):
    n = len(srcs)
    row_axis = [s.ndim - 2 for s in srcs]
    out_shapes = [s.shape[:ax] + (s.shape[ax] // 2,) + s.shape[ax + 1:] if halves else s.shape
                  for s, ax in zip(srcs, row_axis)]

    def body(*refs):
        x_refs, o_refs = refs[:n], refs[n:2 * n]
        send_sems, recv_sems = refs[2 * n:]
        x, y, c = _coords()
        copies = []
        for a in range(n):
            part = x_refs[a]
            if halves:
                rh = srcs[a].shape[row_axis[a]] // 2
                theirs = pl.ds(pl.multiple_of((1 - c) * rh, 16), rh)
                part = part.at[:, theirs] if row_axis[a] == 1 else part.at[theirs]
            cp = pltpu.make_async_remote_copy(src_ref=part, dst_ref=o_refs[a], send_sem=send_sems.at[a],
                                              recv_sem=recv_sems.at[a], device_id=(x, y, 1 - c), device_id_type=MESH)
            cp.start()
            copies.append(cp)
        for cp in copies:
            cp.wait()

    return pl.pallas_call(
        body, name=name, in_specs=[ANY] * n, out_specs=tuple([ANY] * n),
        out_shape=tuple(jax.ShapeDtypeStruct(sh, s.dtype) for sh, s in zip(out_shapes, srcs)),
        scratch_shapes=[pltpu.SemaphoreType.DMA((n,)), pltpu.SemaphoreType.DMA((n,))],
        compiler_params=_params())(*srcs)


class ChipScatter:
    def __init__(self, parts):
        self.n = len(parts)
        self.in_specs = [ANY] * self.n
        self.out_specs = [ANY] * self.n
        self.out_shape = [jax.ShapeDtypeStruct((3,) + p.shape[1:], p.dtype) for p in parts]
        self.scratch = [pltpu.SemaphoreType.DMA((3 * self.n,)), pltpu.SemaphoreType.DMA((3 * self.n,))] if self.n else []

    def _copies(self, x_refs, o_refs, sems):
        send_sems, recv_sems = sems
        x, y, c = _coords()
        return [pltpu.make_async_remote_copy(src_ref=x_refs[a].at[2 * px + py], dst_ref=o_refs[a].at[j],
                                             send_sem=send_sems.at[3 * a + j], recv_sem=recv_sems.at[3 * a + j],
                                             device_id=(px, py, c), device_id_type=MESH)
                for j, (px, py) in enumerate(_other_chips(x, y)) for a in range(self.n)]

    def start(self, x_refs, o_refs, sems):
        if self.n:
            for cp in self._copies(x_refs, o_refs, sems):
                cp.start()

    def finish(self, x_refs, o_refs, sems):
        if self.n:
            for cp in self._copies(x_refs, o_refs, sems):
                cp.wait()


HBM = pl.BlockSpec(memory_space=pltpu.HBM)
SEM = pl.BlockSpec(memory_space=pltpu.SEMAPHORE)
SIDE_EFFECT = pltpu.SideEffectType.DATAFLOW_SIDE_EFFECTING


def _scatter_copies(x_refs, land_refs, send_sems, recv_sems):
    x, y, c = _coords()
    n = len(x_refs)
    return [pltpu.make_async_remote_copy(src_ref=x_refs[a].at[2 * px + py], dst_ref=land_refs[a].at[j],
                                         send_sem=send_sems[3 * a + j], recv_sem=recv_sems[3 * a + j],
                                         device_id=(px, py, c), device_id_type=MESH)
            for j, (px, py) in enumerate(_other_chips(x, y)) for a in range(n)]


def scatter_start(parts, name):
    n = len(parts)
    k = 3 * n
    lands = [lax.empty((3,) + p.shape[1:], p.dtype) for p in parts]

    def body(*refs):
        x_refs, land_refs = refs[:n], refs[n:2 * n]
        send_sems, recv_sems = refs[2 * n:2 * n + k], refs[2 * n + k:2 * n + 2 * k]
        token = refs[-1]
        for cp in _scatter_copies(x_refs, land_refs, send_sems, recv_sems):
            cp.start()
        token[...] = jnp.zeros_like(token)

    hbm = lambda a: pltpu.HBM(a.shape, a.dtype)
    res = pl.pallas_call(
        body, name=name, in_specs=[HBM] * (2 * n),
        out_specs=tuple([SEM] * (2 * k) + [HBM] * (2 * n) + [pl.BlockSpec(memory_space=pltpu.VMEM)]),
        out_shape=tuple([pltpu.SemaphoreType.DMA(())] * (2 * k) + [hbm(p) for p in parts] + [hbm(b) for b in lands]
                        + [jax.ShapeDtypeStruct((8, 128), F32)]),
        input_output_aliases={i: 2 * k + i for i in range(2 * n)},
        compiler_params=pltpu.CompilerParams(has_side_effects=SIDE_EFFECT, vmem_limit_bytes=VMEM_LIMIT))(
            *[pltpu.with_memory_space_constraint(a, pltpu.HBM) for a in list(parts) + lands])
    return list(res[:2 * k]), list(res[2 * k:2 * k + n]), list(res[2 * k + n:2 * k + 2 * n]), res[-1]


def scatter_wait(sems, parts_thru, lands_thru, after, name):
    n = len(parts_thru)
    k = 3 * n

    def body(*refs):
        x_refs, land_refs = refs[:n], refs[n:2 * n]
        send_sems, recv_sems = refs[2 * n:2 * n + k], refs[2 * n + k:2 * n + 2 * k]
        for cp in _scatter_copies(x_refs, land_refs, send_sems, recv_sems):
            cp.wait_send()
            cp.wait_recv()

    hbm = lambda a: pltpu.HBM(a.shape, a.dtype)
    res = pl.pallas_call(
        body, name=name, in_specs=[HBM] * (2 * n) + [SEM] * (2 * k) + [ANY],
        out_specs=tuple([HBM] * (2 * n)), out_shape=tuple(hbm(a) for a in list(parts_thru) + list(lands_thru)),
        input_output_aliases={i: i for i in range(2 * n)},
        compiler_params=pltpu.CompilerParams(has_side_effects=SIDE_EFFECT, vmem_limit_bytes=VMEM_LIMIT))(
            *parts_thru, *lands_thru, *sems, after)
    return list(res[n:])


def _row_block(rows):
    return max(b for b in range(16, 257, 16) if rows % b == 0)


def chip_sum(gpack, recv, core, name):
    n, R, W = gpack.shape
    rh = R // 2
    rb = _row_block(rh)
    nb = rh // rb

    def body(c_ref, g_ref, r_ref, o_ref, ob_ref):
        s = g_ref[...] + r_ref[...]
        o_ref[...] = s
        ob_ref[...] = s.astype(BF16)

    blk = pl.BlockSpec((1, rb, W), lambda i, j, c_ref: (i, j, 0))
    spec = pltpu.PrefetchScalarGridSpec(
        num_scalar_prefetch=1, grid=(n, nb),
        in_specs=[pl.BlockSpec((1, rb, W), lambda i, j, c_ref: (i, c_ref[0] * nb + j, 0)), blk],
        out_specs=(blk, blk))
    return pl.pallas_call(
        body, name=name, grid_spec=spec,
        out_shape=(jax.ShapeDtypeStruct((n, rh, W), F32), jax.ShapeDtypeStruct((n, rh, W), BF16)),
        compiler_params=_params(("arbitrary", "arbitrary")))(core, gpack, recv)


def chip_sum_cols(gate, up, recv_gate, recv_up, core, name):
    R, W = gate.shape
    cw = W // 2
    rh = R // 2
    rb = _row_block(rh)
    nb = rh // rb

    def body(c_ref, g_ref, u_ref, rg_ref, ru_ref, o_ref, ob_ref):
        s = jnp.where(pl.program_id(0) < 2, g_ref[...] + rg_ref[...], u_ref[...] + ru_ref[...])
        o_ref[0] = s
        ob_ref[0] = s.astype(BF16)

    gate_blk = lambda s, j: (jnp.where(s < 2, j, nb - 1), jnp.minimum(s, 1))
    up_blk = lambda s, j: (jnp.where(s < 2, 0, j), jnp.maximum(s - 2, 0))
    out = pl.BlockSpec((1, rb, cw), lambda s, j, c_ref: (s, j, 0))

    def own(blk):
        return lambda s, j, c_ref: (c_ref[0] * nb + blk(s, j)[0], blk(s, j)[1])

    def theirs(blk):
        return lambda s, j, c_ref: blk(s, j)

    spec = pltpu.PrefetchScalarGridSpec(
        num_scalar_prefetch=1, grid=(N_CHIPS, nb),
        in_specs=[pl.BlockSpec((rb, cw), own(gate_blk)), pl.BlockSpec((rb, cw), own(up_blk)),
                  pl.BlockSpec((rb, cw), theirs(gate_blk)), pl.BlockSpec((rb, cw), theirs(up_blk))],
        out_specs=(out, out))
    return pl.pallas_call(
        body, name=name, grid_spec=spec,
        out_shape=(jax.ShapeDtypeStruct((N_CHIPS, rh, cw), F32), jax.ShapeDtypeStruct((N_CHIPS, rh, cw), BF16)),
        compiler_params=_params(("arbitrary", "arbitrary")))(core, gate, up, recv_gate, recv_up)


def chip_sums_of(items, core, tag):
    parts = []
    for _, _, g in items:
        parts += list(g) if isinstance(g, tuple) else [g]
    swapped = iter(zip(parts, sibling_swap(parts, True, f"grad_swap_cores_{tag}")))
    core_arg = core.reshape(1).astype(jnp.int32)
    sums = []
    for n, l, g in items:
        if isinstance(g, tuple):
            (dwg, from_g), (dwu, from_u) = next(swapped), next(swapped)
            sums.append(chip_sum_cols(dwg, dwu, from_g, from_u, core_arg, f"grad_chip_sum_{n}{l}"))
        else:
            p, r = next(swapped)
            sums.append(chip_sum(p, r, core_arg, f"grad_chip_sum_{n}{l}"))
    return sums


def shard_sum(own, recv, name):
    R, W = own.shape
    rb = _row_block(R)

    def body(a_ref, r_ref, o_ref):
        acc = a_ref[...]
        for j in range(3):
            acc = acc + r_ref[j].astype(F32)
        o_ref[...] = acc

    return pl.pallas_call(
        body, name=name, grid=(R // rb,),
        in_specs=[pl.BlockSpec((rb, W), lambda i: (i, 0)), pl.BlockSpec((3, rb, W), lambda i: (0, i, 0))],
        out_specs=pl.BlockSpec((rb, W), lambda i: (i, 0)), out_shape=jax.ShapeDtypeStruct((R, W), F32),
        compiler_params=_params(("arbitrary",)))(own, recv)


def adamw(w, m, v, g, name):
    L, R, C = w.shape
    rb = max(b for b in range(8, 257, 8) if R % b == 0)
    bc1 = 1.0 - ADAM_B1 ** ADAM_STEP
    bc2 = 1.0 - ADAM_B2 ** ADAM_STEP

    def body(w_ref, m_ref, v_ref, g_ref, d_ref, nm_ref, nv_ref):
        gv = g_ref[...]
        nm = ADAM_B1 * m_ref[...] + (1.0 - ADAM_B1) * gv
        nv = ADAM_B2 * v_ref[...] + (1.0 - ADAM_B2) * (gv * gv)
        d_ref[...] = -ADAM_LR * ((nm / bc1) / (jnp.sqrt(nv / bc2) + ADAM_EPS) + ADAM_WD * w_ref[...])
        nm_ref[...] = nm
        nv_ref[...] = nv

    blk = pl.BlockSpec((1, rb, C), lambda l, i: (l, i, 0))
    sh = jax.ShapeDtypeStruct((L, R, C), F32)
    return pl.pallas_call(body, name=name, grid=(L, R // rb), in_specs=[blk] * 4, out_specs=(blk,) * 3,
                          out_shape=(sh, sh, sh), compiler_params=_params(("arbitrary", "arbitrary")))(w, m, v, g)


SMALL = [n for n in WEIGHTS if n not in BIG]


PACK_TILE = 8 * 128


def _pack(arrays):
    blocks = []
    for a in arrays:
        flat = a.reshape(-1)
        flat = jnp.pad(flat, (0, -flat.size % PACK_TILE))
        blocks.append(flat.reshape(-1, 128))
    rows = sum(b.shape[0] for b in blocks)
    if rows % 16:
        blocks.append(jnp.zeros((8, 128), arrays[0].dtype))
    return jnp.concatenate(blocks, axis=0)


def _unpack(pack, shapes):
    out, row = [], 0
    for shape in shapes:
        size = int(np.prod(shape))
        rows = -(-size // PACK_TILE) * 8
        out.append(pack[row:row + rows].reshape(-1)[:size].reshape(shape))
        row += rows
    return out


def _pad_lanes(a):
    return jnp.pad(a, ((0, 0), (0, -a.shape[1] % 128)))


def _local_shard(full, axis, chip):
    size = full.shape[axis] // N_CHIPS
    return lax.dynamic_slice_in_dim(full, chip * size, size, axis)


def kernel(x, ffn1_norm, ffn1_wi, ffn1_wo, mix_norm, w_in, w_out, lru_conv_w, lru_conv_b, lru_gate_a_w, lru_gate_a_b, lru_gate_x_w, lru_gate_x_b, lru_lambda, lru_out_norm, rwkv_mu, rwkv_w_up, rwkv_w_bias, rwkv_a_up, rwkv_a_bias, rwkv_g_up, rwkv_k_k, rwkv_k_a, rwkv_r_k, rwkv_ln_g, rwkv_ln_b, rwkv_vres_w1, rwkv_vres_w2, rwkv_vres_b, gdn_conv_w, gdn_a_log, gdn_dt_bias, gdn_norm, ffn2_norm, ffn2_wi, ffn2_wo, final_norm, loss_target, m_ffn1_norm, m_ffn1_wi, m_ffn1_wo, m_mix_norm, m_w_in, m_w_out, m_lru_conv_w, m_lru_conv_b, m_lru_gate_a_w, m_lru_gate_a_b, m_lru_gate_x_w, m_lru_gate_x_b, m_lru_lambda, m_lru_out_norm, m_rwkv_mu, m_rwkv_w_up, m_rwkv_w_bias, m_rwkv_a_up, m_rwkv_a_bias, m_rwkv_g_up, m_rwkv_k_k, m_rwkv_k_a, m_rwkv_r_k, m_rwkv_ln_g, m_rwkv_ln_b, m_rwkv_vres_w1, m_rwkv_vres_w2, m_rwkv_vres_b, m_gdn_conv_w, m_gdn_a_log, m_gdn_dt_bias, m_gdn_norm, m_ffn2_norm, m_ffn2_wi, m_ffn2_wo, m_final_norm, v_ffn1_norm, v_ffn1_wi, v_ffn1_wo, v_mix_norm, v_w_in, v_w_out, v_lru_conv_w, v_lru_conv_b, v_lru_gate_a_w, v_lru_gate_a_b, v_lru_gate_x_w, v_lru_gate_x_b, v_lru_lambda, v_lru_out_norm, v_rwkv_mu, v_rwkv_w_up, v_rwkv_w_bias, v_rwkv_a_up, v_rwkv_a_bias, v_rwkv_g_up, v_rwkv_k_k, v_rwkv_k_a, v_rwkv_r_k, v_rwkv_ln_g, v_rwkv_ln_b, v_rwkv_vres_w1, v_rwkv_vres_w2, v_rwkv_vres_b, v_gdn_conv_w, v_gdn_a_log, v_gdn_dt_bias, v_gdn_norm, v_ffn2_norm, v_ffn2_wi, v_ffn2_wo, v_final_norm):
    args = locals()
    w_loc = {n: args[n] for n in WEIGHTS}
    m_loc = {n: args['m_' + n] for n in WEIGHTS}
    v_loc = {n: args['v_' + n] for n in WEIGHTS}
    chip = 2 * lax.axis_index("x") + lax.axis_index("y")
    core = lax.axis_index("c")

    big = [(n, l) for n in BIG for l in range(N_LAYERS)]
    shards = {(n, l): _pad_lanes(w_loc[n][l].astype(BF16)) for n, l in big}
    sm_names = list(SMALL_SHARDED)
    placed = []
    for n in sm_names:
        mine = [jnp.where((chip == s) & (core == 0), w_loc[n], 0.0) for s in range(N_CHIPS)]
        placed.append(jnp.concatenate(mine, axis=SMALL_SHARDED[n]))
    first = [('ffn1_wi', 0), ('ffn1_wo', 0)]
    summed, gathered = allreduce_small(_pack(placed), "allgather_first", [shards[k] for k in first])
    wb = {n: [None] * N_LAYERS for n in BIG}
    for (n, l), g in zip(first, gathered):
        wb[n][l] = _natural(n, g)
    w_full = dict(w_loc)
    w_full.update(zip(sm_names, _unpack(summed, [p.shape for p in placed])))

    loss, dx, grads = local_step(x[0], loss_target[0], w_full, wb, shards)
    loss = lax.psum(loss, ("x", "y", "c"))

    gsum, _ = allreduce_small(_pack([grads[n] for n in SMALL]), "allreduce_small")
    g_loc = {}
    for n, g in zip(SMALL, _unpack(gsum, [grads[n].shape for n in SMALL])):
        g_loc[n] = _local_shard(g, SMALL_SHARDED[n], chip) if n in SMALL_SHARDED else g

    last_keys, last_parts = grads['last_round']
    sems, parts_thru, lands_thru, token = scatter_start(last_parts, "grad_scatter_last_start")
    chip_after_start = chip + token[0, 0].astype(chip.dtype)
    rows = {n: [None] * N_LAYERS for n in BIG}
    delta, new_m, new_v = {}, {}, {}

    def finish(keys, arrived, which_chip, tag):
        halves = [shard_sum(lax.dynamic_index_in_dim(grads[n][l][0], which_chip, 0, keepdims=False), got,
                            f"grad_shard_sum_{n}{l}") for (n, l), got in zip(keys, arrived)]
        others = sibling_swap(halves, False, f"grad_share_cores_{tag}")
        for (n, l), half, other in zip(keys, halves, others):
            lower = jnp.where(core == 0, half, other)
            upper = jnp.where(core == 0, other, half)
            rows[n][l] = jnp.concatenate([lower, upper], axis=0)[:, :w_loc[n].shape[-1]]
        for n in BIG:
            if n not in delta and all(r is not None for r in rows[n]):
                g_loc[n] = jnp.stack(rows[n])
                delta[n], new_m[n], new_v[n] = adamw(w_loc[n], m_loc[n], v_loc[n], g_loc[n], f"adamw_{n}")

    early = [k for k in big if k not in last_keys]
    finish(early, [grads[n][l][1] for n, l in early], chip_after_start, "early")
    pack = lambda d: _pack([d[n] for n in SMALL])[None]
    res = adamw(pack(w_loc), pack(m_loc), pack(v_loc), pack(g_loc), "adamw_small")
    for dst, r in zip((delta, new_m, new_v), res):
        dst.update(zip(SMALL, _unpack(r[0], [w_loc[n].shape for n in SMALL])))
    arrived_last = scatter_wait(sems, parts_thru, lands_thru, delta['ffn2_wi'], "grad_scatter_last_wait")
    finish(last_keys, arrived_last, chip, "last")

    return (loss, dx[None], *[g_loc[n] for n in WEIGHTS], *[delta[n] for n in WEIGHTS],
            *[new_m[n] for n in WEIGHTS], *[new_v[n] for n in WEIGHTS])
```
